```python
import math
import jax, jax.numpy as jnp
from jax import lax
import numpy as np

D_MODEL = 1024
BATCH = 1
SEQ = 16384
DEPTH = 2
DEC_BATCH = 128
DEC_SEQ = 4
PAST_LEN = 16384
PAGE_SIZE = 128

EPS = 1e-6
ML_H = 4
ML_D = 128
ML_W = ML_H * ML_D
ML_CHUNK = 128
SWA_H = 8
SWA_KVH = 2
SWA_D = 64
SWA_W = SWA_H * SWA_D
SWA_KV_W = SWA_KVH * SWA_D
WINDOW = 128
N_BUCKETS = 32
MAX_DISTANCE = 128
RET_H = 4
RET_D = 128
RET_W = RET_H * RET_D
RET_CHUNK = 128
ROPE_BASE = 10000.0
N_MEM = 256
X_H = 4
X_D = 64
X_W = X_H * X_D
D_FF = ((8 * D_MODEL // 3 + 255) // 256) * 256
N_BRANCH = 3
W_BUF = min(WINDOW, PAST_LEN)
IN_SIZES = (ML_W, ML_W, ML_W, ML_H, ML_H, ML_W,
            SWA_W, SWA_KV_W, SWA_KV_W,
            RET_W, RET_W, RET_W, RET_W,
            N_BRANCH * D_MODEL)
IN_COLS = sum(IN_SIZES)
SPLIT_POINTS = tuple(int(c) for c in np.cumsum(IN_SIZES)[:-1])

kernel_name = "hybrid_mlstm_swa_retention_step"


def rms_norm(x, g):
    xf = x.astype(jnp.float32)
    y = xf * lax.rsqrt(jnp.mean(xf * xf, axis=-1, keepdims=True) + EPS)
    return (y * g.astype(jnp.float32)).astype(x.dtype)


def head_rms(h, g):
    B, T, H, d = h.shape
    y = h * lax.rsqrt(jnp.mean(h * h, axis=-1, keepdims=True) + EPS)
    return y.reshape(B, T, H * d) * g.astype(jnp.float32)


def rotary(x, pos):
    half = x.shape[-1] // 2
    inv = ROPE_BASE ** (-jnp.arange(half, dtype=jnp.float32) / half)
    ang = pos.astype(jnp.float32)[:, None] * inv[None, :]
    cos, sin = jnp.cos(ang)[None, :, None, :], jnp.sin(ang)[None, :, None, :]
    x1, x2 = x[..., :half], x[..., half:]
    return jnp.concatenate([x1 * cos - x2 * sin, x1 * sin + x2 * cos], axis=-1)


def t5_bucket(delta):
    n = jnp.maximum(delta, 0)
    max_exact = N_BUCKETS // 2
    nf = jnp.maximum(n, 1).astype(jnp.float32)
    large = max_exact + (jnp.log(nf / max_exact) / math.log(MAX_DISTANCE / max_exact)
                         * (N_BUCKETS - max_exact)).astype(jnp.int32)
    large = jnp.minimum(large, N_BUCKETS - 1)
    return jnp.where(n < max_exact, n, large)


def swa_attend(q, k, v, q_pos, k_pos, sinks, rel_bias):
    B, N, Lq, H, d = q.shape
    Lk = k.shape[2]
    G = H // SWA_KVH
    qg = q.reshape(B, N, Lq, SWA_KVH, G, d)
    s = jnp.einsum('bnqhgd,bnshd->bnhgqs', qg, k).astype(jnp.float32) * (d ** -0.5)
    delta = q_pos[:, :, None] - k_pos[:, None, :]
    valid = (delta >= 0) & (delta < WINDOW) & (k_pos[:, None, :] >= 0)
    bias = rel_bias.astype(jnp.float32)[t5_bucket(delta)]
    bias = jnp.moveaxis(bias, -1, 1).reshape(N, SWA_KVH, G, Lq, Lk)
    s = jnp.where(valid[None, :, None, None], s + bias[None], -jnp.inf)
    sink = sinks.astype(jnp.float32).reshape(SWA_KVH, G)[None, None, :, :, None, None]
    m = jnp.maximum(jnp.max(s, axis=-1, keepdims=True), sink)
    p = jnp.exp(s - m)
    p = p / (jnp.sum(p, axis=-1, keepdims=True) + jnp.exp(sink - m))
    o = jnp.einsum('bnhgqs,bnshd->bnqhgd', p.astype(v.dtype), v)
    return o.reshape(B, N, Lq, H * d)


def swa_prompt(q, k, v, sinks, rel_bias):
    B, T, H, d = q.shape
    NB = T // WINDOW
    qb = q.reshape(B, NB, WINDOW, H, d)

    def with_prev(a):
        ap = jnp.pad(a, ((0, 0), (WINDOW, 0), (0, 0), (0, 0)))
        ap = ap.reshape(B, NB + 1, WINDOW, a.shape[2], a.shape[3])
        return jnp.concatenate([ap[:, :-1], ap[:, 1:]], axis=2)

    q_pos = jnp.arange(T, dtype=jnp.int32).reshape(NB, WINDOW)
    kp = (jnp.arange(T + WINDOW, dtype=jnp.int32) - WINDOW).reshape(NB + 1, WINDOW)
    k_pos = jnp.concatenate([kp[:-1], kp[1:]], axis=1)
    o = swa_attend(qb, with_prev(k), with_prev(v), q_pos, k_pos, sinks, rel_bias)
    return o.reshape(B, T, H * d)


def swa_sample(q, k, v, k_buf, v_buf, sinks, rel_bias):
    B, T, H, d = q.shape
    W = k_buf.shape[1]
    kc = jnp.concatenate([k_buf.astype(k.dtype), k], axis=1)
    vc = jnp.concatenate([v_buf.astype(v.dtype), v], axis=1)
    q_pos = (PAST_LEN + jnp.arange(T, dtype=jnp.int32))[None]
    k_pos = (PAST_LEN - W + jnp.arange(W + T, dtype=jnp.int32))[None]
    o = swa_attend(q[:, None], kc[:, None], vc[:, None], q_pos, k_pos, sinks, rel_bias)
    return o.reshape(B, T, H * d), kc[:, -W:], vc[:, -W:]


def mlstm_chunkwise(q, k, v, log_i, log_f, C0, n0, m0):
    B, T, H, d = q.shape
    L = math.gcd(T, ML_CHUNK)
    NC = T // L

    def to_chunks(a):
        return jnp.moveaxis(a.reshape((B, NC, L) + a.shape[2:]), 1, 0)

    causal = jnp.tril(jnp.ones((L, L), dtype=bool))

    def step(carry, xs):
        C, n, m = carry
        qb, kb, vb, ib, fb = xs
        b = jnp.moveaxis(jnp.cumsum(fb, axis=1), 1, 2)
        ih = jnp.moveaxis(ib, 1, 2)
        logD = b[..., :, None] - b[..., None, :] + ih[..., None, :]
        logD = jnp.where(causal, logD, -jnp.inf)
        log_inter = b + m[..., None]
        m_t = jnp.maximum(log_inter, jnp.max(logD, axis=-1))
        Dm = jnp.exp(logD - m_t[..., None])
        inter = jnp.exp(log_inter - m_t)
        s = jnp.einsum('blhd,bshd->bhls', qb, kb) * Dm
        num = (jnp.einsum('bhls,bshd->blhd', s, vb)
               + jnp.einsum('bhed,blhd->blhe', C, qb) * jnp.moveaxis(inter, 1, 2)[..., None])
        den = jnp.sum(s, axis=-1) + inter * jnp.einsum('bhd,blhd->bhl', n, qb)
        den = jnp.maximum(jnp.abs(den), jnp.exp(-m_t))
        h = num / jnp.moveaxis(den, 1, 2)[..., None]
        m_new = m_t[..., -1]
        decay_prev = jnp.exp(b[..., -1] + m - m_new)
        w = jnp.exp(b[..., -1:] - b + ih - m_new[..., None])
        C_new = decay_prev[..., None, None] * C + jnp.einsum('bhs,bshe,bshd->bhed', w, vb, kb)
        n_new = decay_prev[..., None] * n + jnp.einsum('bhs,bshd->bhd', w, kb)
        return (C_new, n_new, m_new), h

    (C, n, m), hs = lax.scan(step, (C0, n0, m0),
                             (to_chunks(q), to_chunks(k), to_chunks(v), to_chunks(log_i), to_chunks(log_f)))
    return jnp.moveaxis(hs, 0, 1).reshape(B, T, H, d), C, n, m


def retention_chunkwise(q, k, v, log_gamma, S0):
    B, T, H, d = q.shape
    L = math.gcd(T, RET_CHUNK)
    NC = T // L

    def to_chunks(a):
        return jnp.moveaxis(a.reshape((B, NC, L) + a.shape[2:]), 1, 0)

    idx = jnp.arange(L, dtype=jnp.float32)
    rel = idx[:, None] - idx[None, :]
    decay_in = jnp.where(rel >= 0, jnp.exp(log_gamma[:, None, None] * jnp.maximum(rel, 0.0)), 0.0)
    q_decay = jnp.exp(log_gamma[None, :] * (idx[:, None] + 1.0))
    k_decay = jnp.exp(log_gamma[:, None] * (L - 1.0 - idx[None, :]))
    chunk_decay = jnp.exp(log_gamma * L)

    def step(S, xs):
        qb, kb, vb = xs
        s = jnp.einsum('blhd,bshd->bhls', qb, kb) * decay_in[None]
        o = (jnp.einsum('bhls,bshe->blhe', s, vb)
             + jnp.einsum('blhd,bhde->blhe', qb, S) * q_decay[None, :, :, None])
        S_new = chunk_decay[None, :, None, None] * S + jnp.einsum('bshd,bshe,hs->bhde', kb, vb, k_decay)
        return S_new, o

    S, outs = lax.scan(step, S0, (to_chunks(q), to_chunks(k), to_chunks(v)))
    return jnp.moveaxis(outs, 0, 1).reshape(B, T, H, v.shape[-1]), S


def mixer_block(u, lw, rel_bias, ml_state, ret_state, swa_past, pos0):
    (w_in, ml_bias, ml_g, ret_g, sinks, w_br_ml, w_br_swa, w_br_ret, w_out) = lw
    f32 = jnp.float32
    B, T, _ = u.shape
    z = jnp.einsum('btd,dc->btc', u, w_in)
    (ml_q, ml_k, ml_v, ml_i, ml_f, ml_o, sw_q, sw_k, sw_v,
     rt_q, rt_k, rt_v, rt_g, gates) = jnp.split(z, SPLIT_POINTS, axis=-1)

    q = ml_q.reshape(B, T, ML_H, ML_D).astype(f32)
    k = ml_k.reshape(B, T, ML_H, ML_D).astype(f32) * (ML_D ** -0.5)
    v = ml_v.reshape(B, T, ML_H, ML_D).astype(f32)
    gb = ml_bias.astype(f32)
    log_i = ml_i.astype(f32) + gb[0]
    log_f = jax.nn.log_sigmoid(ml_f.astype(f32) + gb[1])
    C0, n0, m0 = (a.astype(f32) for a in ml_state)
    h, C, n, m = mlstm_chunkwise(q, k, v, log_i, log_f, C0, n0, m0)
    h = head_rms(h, ml_g) * jax.nn.sigmoid(ml_o.astype(f32))
    y_ml = jnp.einsum('btc,cd->btd', h.astype(u.dtype), w_br_ml)

    q = sw_q.reshape(B, T, SWA_H, SWA_D)
    k = sw_k.reshape(B, T, SWA_KVH, SWA_D)
    v = sw_v.reshape(B, T, SWA_KVH, SWA_D)
    if swa_past is None:
        o = swa_prompt(q, k, v, sinks, rel_bias)
        wb = min(WINDOW, T)
        k_buf, v_buf = k[:, T - wb:], v[:, T - wb:]
    else:
        o, k_buf, v_buf = swa_sample(q, k, v, swa_past[0], swa_past[1], sinks, rel_bias)
    y_sw = jnp.einsum('btc,cd->btd', o, w_br_swa)

    pos = pos0 + jnp.arange(T, dtype=jnp.int32)
    log_gamma = jnp.log1p(-jnp.exp2(-5.0 - jnp.arange(RET_H, dtype=f32)))
    q = rotary(rt_q.reshape(B, T, RET_H, RET_D).astype(f32), pos)
    k = rotary(rt_k.reshape(B, T, RET_H, RET_D).astype(f32), pos) * (RET_D ** -0.5)
    v = rt_v.reshape(B, T, RET_H, RET_D).astype(f32)
    o, S = retention_chunkwise(q, k, v, log_gamma, ret_state.astype(f32))
    o = head_rms(o, ret_g) * jax.nn.silu(rt_g.astype(f32))
    y_rt = jnp.einsum('btc,cd->btd', o.astype(u.dtype), w_br_ret)

    g = jax.nn.sigmoid(gates.astype(f32)).reshape(B, T, N_BRANCH, D_MODEL)
    merged = (g[..., 0, :] * y_ml.astype(f32) + g[..., 1, :] * y_sw.astype(f32)
              + g[..., 2, :] * y_rt.astype(f32))
    out = jnp.einsum('btd,de->bte', merged.astype(u.dtype), w_out)
    dt = u.dtype
    return out, (C.astype(dt), n.astype(dt), m.astype(dt), S.astype(dt), k_buf, v_buf)


def memory_kv(mem, g, w_mkv):
    B, M, _ = mem.shape
    kv = jnp.einsum('bmd,dc->bmc', rms_norm(mem, g), w_mkv)
    k, v = jnp.split(kv, 2, axis=-1)
    return k.reshape(B, M, X_H, X_D), v.reshape(B, M, X_H, X_D)


def cross_attend(u, mem_k, mem_v, w_cq, w_co):
    B, T, _ = u.shape
    q = jnp.einsum('btd,dc->btc', u, w_cq).reshape(B, T, X_H, X_D)
    s = jnp.einsum('bthd,bmhd->bhtm', q, mem_k).astype(jnp.float32) * (X_D ** -0.5)
    p = jax.nn.softmax(s, axis=-1)
    o = jnp.einsum('bhtm,bmhd->bthd', p.astype(mem_v.dtype), mem_v).reshape(B, T, X_W)
    return jnp.einsum('btc,cd->btd', o, w_co)


def swiglu(u, w_gu, w_down):
    g, up = jnp.split(jnp.einsum('btd,df->btf', u, w_gu), 2, axis=-1)
    h = jax.nn.silu(g.astype(jnp.float32)) * up.astype(jnp.float32)
    return jnp.einsum('btf,fd->btd', h.astype(u.dtype), w_down)


def decoder_layer(x, ng, lw, x_w, ffn_w, rel_bias, mem_k, mem_v, ml_state, ret_state, swa_past, pos0):
    mix, st = mixer_block(rms_norm(x, ng[0]), lw, rel_bias, ml_state, ret_state, swa_past, pos0)
    x = x + rms_norm(mix, ng[1])
    xa = cross_attend(rms_norm(x, ng[2]), mem_k, mem_v, x_w[0], x_w[1])
    x = x + rms_norm(xa, ng[3])
    x = x + rms_norm(swiglu(rms_norm(x, ng[4]), ffn_w[0], ffn_w[1]), ng[5])
    return x, st


def setup_inputs(seed: int = 0) -> dict:
    key = jax.random.key(seed)
    ks = jax.random.split(key, 32)
    f32 = jnp.float32

    def nrm(k, shape, scale):
        return jax.random.normal(k, shape, f32) * scale

    D = D_MODEL
    ml_gate_bias = jnp.stack([nrm(ks[12], (DEPTH, ML_H), 0.1),
                              jnp.linspace(3.0, 6.0, ML_H, dtype=f32)[None] + nrm(ks[13], (DEPTH, ML_H), 0.1)],
                             axis=1)
    return {
        "x_prompt": nrm(ks[0], (BATCH, SEQ, D), 1.0),
        "x_sample": nrm(ks[1], (DEC_BATCH, DEC_SEQ, D), 1.0),
        "mem_prompt": nrm(ks[2], (BATCH, N_MEM, D), 1.0),
        "state_mlstm_C": nrm(ks[3], (DEPTH, DEC_BATCH, ML_H, ML_D, ML_D), 0.1),
        "state_mlstm_n": nrm(ks[4], (DEPTH, DEC_BATCH, ML_H, ML_D), 0.5),
        "state_mlstm_m": nrm(ks[5], (DEPTH, DEC_BATCH, ML_H), 1.0),
        "state_ret_S": nrm(ks[6], (DEPTH, DEC_BATCH, RET_H, RET_D, RET_D), 0.3),
        "cache_swa_k": nrm(ks[7], (DEPTH, DEC_BATCH, W_BUF, SWA_KVH, SWA_D), 1.0),
        "cache_swa_v": nrm(ks[8], (DEPTH, DEC_BATCH, W_BUF, SWA_KVH, SWA_D), 1.0),
        "cache_mem_k": nrm(ks[9], (DEPTH, DEC_BATCH, N_MEM, X_H, X_D), 1.0),
        "cache_mem_v": nrm(ks[10], (DEPTH, DEC_BATCH, N_MEM, X_H, X_D), 1.0),
        "norm_g": 1.0 + nrm(ks[11], (DEPTH, 7, D), 0.05),
        "w_in": nrm(ks[14], (DEPTH, D, IN_COLS), D ** -0.5),
        "ml_gate_bias": ml_gate_bias,
        "ml_head_g": 1.0 + nrm(ks[15], (DEPTH, ML_W), 0.05),
        "ret_head_g": 1.0 + nrm(ks[16], (DEPTH, RET_W), 0.05),
        "swa_sinks": nrm(ks[17], (DEPTH, SWA_H), 0.5),
        "rel_bias": nrm(ks[18], (N_BUCKETS, SWA_H), 0.1),
        "w_br_ml": nrm(ks[19], (DEPTH, ML_W, D), ML_W ** -0.5),
        "w_br_swa": nrm(ks[20], (DEPTH, SWA_W, D), SWA_W ** -0.5),
        "w_br_ret": nrm(ks[21], (DEPTH, RET_W, D), RET_W ** -0.5),
        "w_out": nrm(ks[22], (DEPTH, D, D), D ** -0.5),
        "w_cq": nrm(ks[23], (DEPTH, D, X_W), D ** -0.5),
        "w_mkv": nrm(ks[24], (DEPTH, D, 2 * X_W), D ** -0.5),
        "w_co": nrm(ks[25], (DEPTH, X_W, D), X_W ** -0.5),
        "w_gu": nrm(ks[26], (DEPTH, D, 2 * D_FF), D ** -0.5),
        "w_down": nrm(ks[27], (DEPTH, D_FF, D), D_FF ** -0.5),
    }


def reference(x_prompt, x_sample, mem_prompt, state_mlstm_C, state_mlstm_n, state_mlstm_m, state_ret_S,
              cache_swa_k, cache_swa_v, cache_mem_k, cache_mem_v, norm_g, w_in, ml_gate_bias, ml_head_g,
              ret_head_g, swa_sinks, rel_bias, w_br_ml, w_br_swa, w_br_ret, w_out, w_cq, w_mkv, w_co,
              w_gu, w_down):
    f32 = jnp.float32
    y_p, y_s = x_prompt, x_sample
    B = x_prompt.shape[0]
    p_names = ('C', 'n', 'm', 'S', 'k', 'v', 'mk', 'mv')
    s_names = ('C', 'n', 'm', 'S', 'k', 'v')
    p_st = {nm: [] for nm in p_names}
    s_st = {nm: [] for nm in s_names}
    for l in range(DEPTH):
        lw = (w_in[l], ml_gate_bias[l], ml_head_g[l], ret_head_g[l], swa_sinks[l],
              w_br_ml[l], w_br_swa[l], w_br_ret[l], w_out[l])
        x_w = (w_cq[l], w_co[l])
        ffn_w = (w_gu[l], w_down[l])
        mk_p, mv_p = memory_kv(mem_prompt, norm_g[l, 6], w_mkv[l])
        ml0 = (jnp.zeros((B, ML_H, ML_D, ML_D), f32), jnp.zeros((B, ML_H, ML_D), f32),
               jnp.zeros((B, ML_H), f32))
        S0 = jnp.zeros((B, RET_H, RET_D, RET_D), f32)
        y_p, st = decoder_layer(y_p, norm_g[l], lw, x_w, ffn_w, rel_bias, mk_p, mv_p, ml0, S0, None, 0)
        for nm, a in zip(p_names, st + (mk_p, mv_p)):
            p_st[nm].append(a)
        y_s, st = decoder_layer(y_s, norm_g[l], lw, x_w, ffn_w, rel_bias, cache_mem_k[l], cache_mem_v[l],
                                (state_mlstm_C[l], state_mlstm_n[l], state_mlstm_m[l]), state_ret_S[l],
                                (cache_swa_k[l], cache_swa_v[l]), PAST_LEN)
        for nm, a in zip(s_names, st):
            s_st[nm].append(a)
    p_mlstm_C = jnp.stack(p_st['C'])
    p_mlstm_n = jnp.stack(p_st['n'])
    p_mlstm_m = jnp.stack(p_st['m'])
    p_ret_S = jnp.stack(p_st['S'])
    p_swa_k = jnp.stack(p_st['k'])
    p_swa_v = jnp.stack(p_st['v'])
    p_mem_k = jnp.stack(p_st['mk'])
    p_mem_v = jnp.stack(p_st['mv'])
    s_mlstm_C = jnp.stack(s_st['C'])
    s_mlstm_n = jnp.stack(s_st['n'])
    s_mlstm_m = jnp.stack(s_st['m'])
    s_ret_S = jnp.stack(s_st['S'])
    s_swa_k = jnp.stack(s_st['k'])
    s_swa_v = jnp.stack(s_st['v'])
    return (y_p, y_s, p_mlstm_C, p_mlstm_n, p_mlstm_m, p_ret_S, p_swa_k, p_swa_v, p_mem_k, p_mem_v,
            s_mlstm_C, s_mlstm_n, s_mlstm_m, s_ret_S, s_swa_k, s_swa_v)
```

```python
import functools
import math

import numpy as np
import jax
import jax.numpy as jnp
from jax import lax
from jax.experimental import pallas as pl
from jax.experimental.pallas import tpu as pltpu

F32 = jnp.float32
BF16 = jnp.bfloat16

D_MODEL = 1024
EPS = 1e-6
PAST_LEN = 16384
HEAD_D = 128
N_HEADS = 4
CHUNK = 128
SWA_HEADS = 8
SWA_D = 64
N_BUCKETS = 32
MAX_DISTANCE = 128
ROPE_BASE = 10000.0
N_MEM = 256
X_HEADS = 4
X_D = 64
D_FF = 2816
SEQ_PAD = 8
N_NEW = 4
LANES = 128
VMEM_LIMIT = 48 * 1024 * 1024

ZC_ML_Q, ZC_ML_K, ZC_ML_V, ZC_ML_O, ZC_SW_Q, ZC_MISC, ZC_RT_Q, ZC_RT_K, ZC_RT_V, ZC_RT_G = range(10)
ZC_GATES = 5
Z_COLS = 8192
MISC_K, MISC_V, MISC_IF = 0, 128, 256

LOG_GAMMA = tuple(float(v) for v in np.log1p(-np.exp2(-5.0 - np.arange(N_HEADS, dtype=np.float32))))

SWA_SAME = (0, 2, 5, 7)
SWA_SWAP = (1, 3, 4, 6)


def _t5_bucket_bounds():
    n = np.arange(CHUNK)
    max_exact = N_BUCKETS // 2
    nf = np.maximum(n, 1).astype(np.float32)
    large = max_exact + (np.log(nf / np.float32(max_exact)) / np.float32(math.log(MAX_DISTANCE / max_exact))
                         * np.float32(N_BUCKETS - max_exact)).astype(np.int32)
    large = np.minimum(large, N_BUCKETS - 1)
    b = np.where(n < max_exact, n, large)
    assert np.all(np.diff(b) >= 0)
    runs = []
    for d in range(CHUNK):
        if runs and runs[-1][1] == int(b[d]):
            runs[-1][0] = d + 1
        else:
            runs.append([d + 1, int(b[d])])
    return tuple((hi, bk) for hi, bk in runs)


BUCKET_RUNS = _t5_bucket_bounds()


def _dot(a, b):
    return jnp.dot(a, b, preferred_element_type=F32)


def _dot_nt(a, b):
    return lax.dot_general(a, b, (((1,), (1,)), ((), ())), preferred_element_type=F32)


def _dot_tn(a, b):
    return lax.dot_general(a, b, (((0,), (0,)), ((), ())), preferred_element_type=F32)


def _rms(x, g):
    return x * lax.rsqrt(jnp.mean(x * x, axis=-1, keepdims=True) + EPS) * g


def _sigmoid(x):
    return 1.0 / (1.0 + jnp.exp(-x))


def _log_sigmoid(x):
    return jnp.minimum(x, 0.0) - jnp.log1p(jnp.exp(-jnp.abs(x)))


def _params(n_grid):
    return pltpu.CompilerParams(dimension_semantics=("arbitrary",) * n_grid, vmem_limit_bytes=VMEM_LIMIT)


def _norm_matmul_kernel(x_ref, g_ref, w_ref, o_ref, u_ref):
    @pl.when(pl.program_id(1) == 0)
    def _():
        u_ref[...] = _rms(x_ref[...], g_ref[...]).astype(BF16)

    o_ref[...] = _dot(u_ref[...], w_ref[...])


def norm_matmul(x, g, w, tm, tn, name):
    m, k = x.shape
    n = w.shape[1]
    return pl.pallas_call(
        _norm_matmul_kernel,
        grid=(m // tm, n // tn),
        in_specs=[pl.BlockSpec((tm, k), lambda i, j: (i, 0)),
                  pl.BlockSpec((1, k), lambda i, j: (0, 0)),
                  pl.BlockSpec((k, tn), lambda i, j: (0, j))],
        out_specs=pl.BlockSpec((tm, tn), lambda i, j: (i, j)),
        out_shape=jax.ShapeDtypeStruct((m, n), F32),
        scratch_shapes=[pltpu.VMEM((tm, k), BF16)],
        compiler_params=_params(2),
        name=name,
    )(x, g, w)


def _build_swa_table(tab_ref, rb_ref, lq):
    row = lax.broadcasted_iota(jnp.int32, (lq, 2 * CHUNK), 0)
    col = lax.broadcasted_iota(jnp.int32, (lq, 2 * CHUNK), 1)
    delta = row + CHUNK - col
    valid = (delta >= 0) & (delta < CHUNK)
    for var, heads in enumerate((SWA_SAME, SWA_SWAP)):
        for i, h in enumerate(heads):
            val = jnp.full((lq, 2 * CHUNK), rb_ref[BUCKET_RUNS[-1][1] * SWA_HEADS + h], F32)
            for hi, bk in reversed(BUCKET_RUNS[:-1]):
                val = jnp.where(delta < hi, rb_ref[bk * SWA_HEADS + h], val)
            tab_ref[var, i * lq:(i + 1) * lq, :] = jnp.where(valid, val, -jnp.inf)


def _swa_block(q, kbuf, vbuf, tab_ref, sinks_ref, prev_invalid):
    lq = q.shape[0]
    lane = lax.broadcasted_iota(jnp.int32, (1, LANES), 1)
    upper = lane >= SWA_D
    k = kbuf[...]
    v = vbuf[...]
    ks = (k.astype(BF16), pltpu.roll(k, SWA_D, 1).astype(BF16))
    vs = (v.astype(BF16), pltpu.roll(v, SWA_D, 1).astype(BF16))
    outs = [None] * SWA_HEADS
    for var, heads in enumerate((SWA_SAME, SWA_SWAP)):
        qm = []
        sk = []
        for h in heads:
            blk = q[:, LANES * (h // 2):LANES * (h // 2 + 1)]
            keep = upper if h % 2 == 1 else jnp.logical_not(upper)
            qm.append(jnp.where(keep, blk, 0.0))
            sk.append(jnp.full((lq, 1), sinks_ref[h], F32))
        qs = jnp.concatenate(qm, axis=0).astype(BF16)
        sink = jnp.concatenate(sk, axis=0)
        s = _dot_nt(qs, ks[var]) * (SWA_D ** -0.5) + tab_ref[var]
        if prev_invalid is not None:
            col = lax.broadcasted_iota(jnp.int32, s.shape, 1)
            s = jnp.where(jnp.logical_and(prev_invalid, col < CHUNK), -jnp.inf, s)
        m = jnp.maximum(jnp.max(s, axis=1, keepdims=True), sink)
        p = jnp.exp(s - m)
        p = p / (jnp.sum(p, axis=1, keepdims=True) + jnp.exp(sink - m))
        o = _dot(p.astype(BF16), vs[var])
        for i, h in enumerate(heads):
            outs[h] = o[i * lq:(i + 1) * lq]
    blocks = [jnp.where(upper, outs[2 * j + 1], outs[2 * j]) for j in range(SWA_HEADS // 2)]
    return jnp.concatenate(blocks, axis=1)


def _rotary_tables(pos, inv_ref):
    lane = lax.broadcasted_iota(jnp.int32, (1, LANES), 1)
    ang = pos * inv_ref[...]
    sin = jnp.sin(ang)
    return jnp.cos(ang), jnp.where(lane < HEAD_D // 2, -sin, sin)


def _rotate(x, cos, sin_signed):
    return x * cos + pltpu.roll(x, HEAD_D // 2, 1) * sin_signed


def _gate_slab(raw, gb_ref):
    lane = lax.broadcasted_iota(jnp.int32, (1, LANES), 1)
    x = raw + gb_ref[...]
    return jnp.where((lane >= N_HEADS) & (lane < 2 * N_HEADS), _log_sigmoid(x), x)


def _mixer_seq_prompt_kernel(mq_ref, mk_ref, mv_ref, swq_ref, misc_ref, rq_ref, rk_ref, rv_ref,
                             gb_ref, inv_ref, sinks_ref, rb_ref,
                             hml_ref, osw_ref, ort_ref, c_ref, n_ref, m_ref, s_ref,
                             kbuf, vbuf, tab):
    c = pl.program_id(0)
    L = CHUNK

    @pl.when(c == 0)
    def _init():
        c_ref[...] = jnp.zeros(c_ref.shape, F32)
        n_ref[...] = jnp.zeros(n_ref.shape, F32)
        m_ref[...] = jnp.zeros(m_ref.shape, F32)
        s_ref[...] = jnp.zeros(s_ref.shape, F32)
        kbuf[...] = jnp.zeros(kbuf.shape, F32)
        vbuf[...] = jnp.zeros(vbuf.shape, F32)
        _build_swa_table(tab, rb_ref, L)

    lane = lax.broadcasted_iota(jnp.int32, (1, LANES), 1)
    row = lax.broadcasted_iota(jnp.int32, (L, L), 0)
    col = lax.broadcasted_iota(jnp.int32, (L, L), 1)
    causal = row >= col

    lf = _gate_slab(misc_ref[:, MISC_IF:MISC_IF + LANES], gb_ref)
    bcum = jnp.dot(causal.astype(F32), lf, precision=lax.Precision.HIGHEST, preferred_element_type=F32)
    colslab = jnp.where(lane < N_HEADS, lf, bcum)
    rowslab = colslab.T
    m_all = m_ref[...]
    m_out = jnp.zeros((1, LANES), F32)
    for h in range(N_HEADS):
        hs = slice(HEAD_D * h, HEAD_D * (h + 1))
        i_c = colslab[:, h:h + 1]
        b_c = colslab[:, N_HEADS + h:N_HEADS + h + 1]
        i_r = rowslab[h:h + 1, :]
        b_r = rowslab[N_HEADS + h:N_HEADS + h + 1, :]
        m_prev = m_all[:, h:h + 1]
        logd = jnp.where(causal, b_c - b_r + i_r, -jnp.inf)
        log_inter = b_c + m_prev
        m_t = jnp.maximum(log_inter, jnp.max(logd, axis=1, keepdims=True))
        dm = jnp.exp(logd - m_t)
        inter = jnp.exp(log_inter - m_t)
        q = mq_ref[:, hs]
        k = mk_ref[:, hs] * (HEAD_D ** -0.5)
        v = mv_ref[:, hs]
        qb = q.astype(BF16)
        kb = k.astype(BF16)
        s = _dot_nt(qb, kb) * dm
        c_old = c_ref[h]
        n_old = n_ref[h]
        num = _dot(s.astype(BF16), v.astype(BF16)) + _dot_nt(qb, c_old.astype(BF16)) * inter
        den = jnp.sum(s, axis=1, keepdims=True) + inter * jnp.sum(q * n_old, axis=1, keepdims=True)
        den = jnp.maximum(jnp.abs(den), jnp.exp(-m_t))
        hml_ref[:, hs] = num / den
        m_new = m_t[L - 1:L, :]
        b_last = b_c[L - 1:L, :]
        decay = jnp.exp(b_last + m_prev - m_new)
        w = jnp.exp(b_last - b_c + i_c - m_new)
        c_ref[h] = decay * c_old + _dot_tn((v * w).astype(BF16), kb)
        n_ref[h] = decay * n_old + jnp.sum(k * w, axis=0, keepdims=True)
        m_out = jnp.where(lane == h, m_new, m_out)
    m_ref[...] = m_out

    idx = lax.broadcasted_iota(jnp.int32, (L, 1), 0)
    cos, sin_signed = _rotary_tables((c * L + idx).astype(F32), inv_ref)
    idx_f = idx.astype(F32)
    rel = (row - col).astype(F32)
    for h in range(N_HEADS):
        hs = slice(HEAD_D * h, HEAD_D * (h + 1))
        lg = LOG_GAMMA[h]
        decay_in = jnp.where(causal, jnp.exp(lg * rel), 0.0)
        q_decay = jnp.exp(lg * (idx_f + 1.0))
        k_decay = jnp.exp(lg * (L - 1.0 - idx_f))
        qr = _rotate(rq_ref[:, hs], cos, sin_signed).astype(BF16)
        kr = _rotate(rk_ref[:, hs], cos, sin_signed) * (HEAD_D ** -0.5)
        vb = rv_ref[:, hs].astype(BF16)
        s = _dot_nt(qr, kr.astype(BF16)) * decay_in
        s_old = s_ref[h]
        ort_ref[:, hs] = _dot(s.astype(BF16), vb) + _dot(qr, s_old.astype(BF16)) * q_decay
        s_ref[h] = math.exp(lg * L) * s_old + _dot_tn((kr * k_decay).astype(BF16), vb)

    kbuf[0:L, :] = kbuf[L:2 * L, :]
    vbuf[0:L, :] = vbuf[L:2 * L, :]
    kbuf[L:2 * L, :] = misc_ref[:, MISC_K:MISC_K + LANES]
    vbuf[L:2 * L, :] = misc_ref[:, MISC_V:MISC_V + LANES]
    osw_ref[...] = _swa_block(swq_ref[...], kbuf, vbuf, tab, sinks_ref, c == 0)


def mixer_seq_prompt(z, gb, inv, sinks, rb):
    t = z.shape[0]
    L = CHUNK

    def zspec(cb):
        return pl.BlockSpec((L, 512), lambda c, cb=cb: (c, cb))

    def const(shape):
        return pl.BlockSpec(shape, lambda c: (0,) * len(shape))

    smem = pl.BlockSpec(memory_space=pltpu.SMEM)
    tok = pl.BlockSpec((L, 512), lambda c: (c, 0))
    return pl.pallas_call(
        _mixer_seq_prompt_kernel,
        grid=(t // L,),
        in_specs=[zspec(ZC_ML_Q), zspec(ZC_ML_K), zspec(ZC_ML_V), zspec(ZC_SW_Q), zspec(ZC_MISC),
                  zspec(ZC_RT_Q), zspec(ZC_RT_K), zspec(ZC_RT_V),
                  const((1, LANES)), const((1, LANES)), smem, smem],
        out_specs=[tok, tok, tok,
                   const((N_HEADS, HEAD_D, HEAD_D)), const((N_HEADS, 1, HEAD_D)), const((1, LANES)),
                   const((N_HEADS, HEAD_D, HEAD_D))],
        out_shape=[jax.ShapeDtypeStruct((t, 512), F32)] * 3
        + [jax.ShapeDtypeStruct((N_HEADS, HEAD_D, HEAD_D), F32),
           jax.ShapeDtypeStruct((N_HEADS, 1, HEAD_D), F32),
           jax.ShapeDtypeStruct((1, LANES), F32),
           jax.ShapeDtypeStruct((N_HEADS, HEAD_D, HEAD_D), F32)],
        scratch_shapes=[pltpu.VMEM((2 * L, LANES), F32), pltpu.VMEM((2 * L, LANES), F32),
                        pltpu.VMEM((2, 4 * L, 2 * L), F32)],
        compiler_params=_params(1),
        name="mixer_seq_prompt",
    )(z, z, z, z, z, z, z, z, gb, inv, sinks, rb)


def _shift_rows(x, d):
    return x if d == 0 else pltpu.roll(x, d, 0)


def _mixer_seq_sample_kernel(mq_ref, mk_ref, mv_ref, swq_ref, misc_ref, rq_ref, rk_ref, rv_ref,
                             c0_ref, n0_ref, m0_ref, s0_ref, ck_ref, cv_ref,
                             gb_ref, inv_ref, sinks_ref, rb_ref,
                             hml_ref, osw_ref, ort_ref, c_ref, n_ref, m_ref, s_ref,
                             kbuf, vbuf, tab, *, group):
    R = SEQ_PAD

    @pl.when(pl.program_id(0) == 0)
    def _init():
        kbuf[...] = jnp.zeros(kbuf.shape, F32)
        vbuf[...] = jnp.zeros(vbuf.shape, F32)
        _build_swa_table(tab, rb_ref, R)

    lane = lax.broadcasted_iota(jnp.int32, (1, LANES), 1)
    l_idx = lax.broadcasted_iota(jnp.int32, (R, 1), 0)
    real = l_idx < N_NEW
    l_f = l_idx.astype(F32)
    cos, sin_signed = _rotary_tables((PAST_LEN + l_idx).astype(F32), inv_ref)

    def body(g, carry):
        rows = pl.ds(pl.multiple_of(g * R, R), R)

        lf = _gate_slab(misc_ref[rows, MISC_IF:MISC_IF + LANES], gb_ref)
        bsum = lf
        for d in range(1, N_NEW):
            bsum = bsum + jnp.where(l_idx >= d, _shift_rows(lf, d), 0.0)
        slab = jnp.where(lane < N_HEADS, lf, bsum)
        shifted = [_shift_rows(slab, d) for d in range(N_NEW)]
        m0 = m0_ref[g]
        n0 = n0_ref[g]
        m_out = jnp.zeros((1, LANES), F32)
        for h in range(N_HEADS):
            hs = slice(HEAD_D * h, HEAD_D * (h + 1))
            i_c = slab[:, h:h + 1]
            b_c = slab[:, N_HEADS + h:N_HEADS + h + 1]
            m_prev = m0[:, h:h + 1]
            logd = []
            for d in range(N_NEW):
                sd = shifted[d]
                ld = b_c - sd[:, N_HEADS + h:N_HEADS + h + 1] + sd[:, h:h + 1]
                logd.append(jnp.where(l_idx >= d, ld, -jnp.inf))
            log_inter = b_c + m_prev
            m_t = log_inter
            for d in range(N_NEW):
                m_t = jnp.maximum(m_t, logd[d])
            inter = jnp.exp(log_inter - m_t)
            q = mq_ref[rows, hs]
            k = mk_ref[rows, hs] * (HEAD_D ** -0.5)
            v = mv_ref[rows, hs]
            c_old = c0_ref[g, h]
            n_old = n0[:, hs]
            num = _dot_nt(q.astype(BF16), c_old.astype(BF16)) * inter
            den = inter * jnp.sum(q * n_old, axis=1, keepdims=True)
            for d in range(N_NEW):
                s_d = jnp.sum(q * _shift_rows(k, d), axis=1, keepdims=True) * jnp.exp(logd[d] - m_t)
                num = num + s_d * _shift_rows(v, d)
                den = den + s_d
            den = jnp.maximum(jnp.abs(den), jnp.exp(-m_t))
            hml_ref[rows, hs] = num / den
            m_new = m_t[N_NEW - 1:N_NEW, :]
            b_last = b_c[N_NEW - 1:N_NEW, :]
            decay = jnp.exp(b_last + m_prev - m_new)
            w = jnp.where(real, jnp.exp(b_last - b_c + i_c - m_new), 0.0)
            c_ref[g, h] = decay * c_old + _dot_tn((v * w).astype(BF16), k.astype(BF16))
            n_ref[g, :, hs] = decay * n_old + jnp.sum(k * w, axis=0, keepdims=True)
            m_out = jnp.where(lane == h, m_new, m_out)
        m_ref[g] = m_out

        for h in range(N_HEADS):
            hs = slice(HEAD_D * h, HEAD_D * (h + 1))
            lg = LOG_GAMMA[h]
            qr = _rotate(rq_ref[rows, hs], cos, sin_signed)
            kr = _rotate(rk_ref[rows, hs], cos, sin_signed) * (HEAD_D ** -0.5)
            v = rv_ref[rows, hs]
            s_old = s0_ref[g, h]
            o = _dot(qr.astype(BF16), s_old.astype(BF16)) * jnp.exp(lg * (l_f + 1.0))
            for d in range(N_NEW):
                s_d = jnp.sum(qr * _shift_rows(kr, d), axis=1, keepdims=True) * math.exp(lg * d)
                o = o + jnp.where(l_idx >= d, s_d, 0.0) * _shift_rows(v, d)
            ort_ref[rows, hs] = o
            k_decay = jnp.where(real, jnp.exp(lg * (N_NEW - 1.0 - l_f)), 0.0)
            s_ref[g, h] = math.exp(lg * N_NEW) * s_old + _dot_tn((kr * k_decay).astype(BF16), v.astype(BF16))

        kbuf[0:CHUNK, :] = ck_ref[g]
        vbuf[0:CHUNK, :] = cv_ref[g]
        kbuf[CHUNK:CHUNK + R, :] = misc_ref[rows, MISC_K:MISC_K + LANES]
        vbuf[CHUNK:CHUNK + R, :] = misc_ref[rows, MISC_V:MISC_V + LANES]
        osw_ref[rows, :] = _swa_block(swq_ref[rows, :], kbuf, vbuf, tab, sinks_ref, None)
        return carry

    lax.fori_loop(0, group, body, 0)


def mixer_seq_sample(z, c0, n0, m0, s0, ck, cv, gb, inv, sinks, rb, group=8):
    nb = c0.shape[0]
    R = SEQ_PAD

    def zspec(cb):
        return pl.BlockSpec((group * R, 512), lambda i, cb=cb: (i, cb))

    def const(shape):
        return pl.BlockSpec(shape, lambda i: (0,) * len(shape))

    def per_seq(shape):
        return pl.BlockSpec((group,) + shape, lambda i: (i,) + (0,) * len(shape))

    smem = pl.BlockSpec(memory_space=pltpu.SMEM)
    tok = pl.BlockSpec((group * R, 512), lambda i: (i, 0))
    st = (N_HEADS, HEAD_D, HEAD_D)
    return pl.pallas_call(
        functools.partial(_mixer_seq_sample_kernel, group=group),
        grid=(nb // group,),
        in_specs=[zspec(ZC_ML_Q), zspec(ZC_ML_K), zspec(ZC_ML_V), zspec(ZC_SW_Q), zspec(ZC_MISC),
                  zspec(ZC_RT_Q), zspec(ZC_RT_K), zspec(ZC_RT_V),
                  per_seq(st), per_seq((1, 512)), per_seq((1, LANES)), per_seq(st),
                  per_seq((CHUNK, LANES)), per_seq((CHUNK, LANES)),
                  const((1, LANES)), const((1, LANES)), smem, smem],
        out_specs=[tok, tok, tok, per_seq(st), per_seq((1, 512)), per_seq((1, LANES)), per_seq(st)],
        out_shape=[jax.ShapeDtypeStruct((nb * R, 512), F32)] * 3
        + [jax.ShapeDtypeStruct((nb,) + st, F32),
           jax.ShapeDtypeStruct((nb, 1, 512), F32),
           jax.ShapeDtypeStruct((nb, 1, LANES), F32),
           jax.ShapeDtypeStruct((nb,) + st, F32)],
        scratch_shapes=[pltpu.VMEM((2 * CHUNK, LANES), F32), pltpu.VMEM((2 * CHUNK, LANES), F32),
                        pltpu.VMEM((2, 4 * R, 2 * CHUNK), F32)],
        compiler_params=_params(1),
        name="mixer_seq_sample",
    )(z, z, z, z, z, z, z, z, c0, n0, m0, s0, ck, cv, gb, inv, sinks, rb)


def _head_rms(h, gain):
    parts = []
    for i in range(N_HEADS):
        blk = h[:, HEAD_D * i:HEAD_D * (i + 1)]
        parts.append(blk * lax.rsqrt(jnp.mean(blk * blk, axis=-1, keepdims=True) + EPS))
    return jnp.concatenate(parts, axis=-1) * gain


def _mixer_post_kernel(hml_ref, osw_ref, ort_ref, mlo_ref, rtg_ref, g0_ref, g1_ref, g2_ref, x_ref,
                       mlg_ref, retg_ref, ng_ref, wml_ref, wsw_ref, wrt_ref, wout_ref, o_ref):
    hm = _head_rms(hml_ref[...], mlg_ref[...]) * _sigmoid(mlo_ref[...])
    y_ml = _dot(hm.astype(BF16), wml_ref[...])
    y_sw = _dot(osw_ref[...].astype(BF16), wsw_ref[...])
    rtg = rtg_ref[...]
    rt = _head_rms(ort_ref[...], retg_ref[...]) * (rtg * _sigmoid(rtg))
    y_rt = _dot(rt.astype(BF16), wrt_ref[...])
    merged = _sigmoid(g0_ref[...]) * y_ml + _sigmoid(g1_ref[...]) * y_sw + _sigmoid(g2_ref[...]) * y_rt
    out = _dot(merged.astype(BF16), wout_ref[...])
    o_ref[...] = x_ref[...] + _rms(out, ng_ref[...])


def mixer_post(hml, osw, ort, z, x, mlg, retg, ng, wml, wsw, wrt, wout, tm, name):
    m = x.shape[0]

    def tok(w, cb=0):
        return pl.BlockSpec((tm, w), lambda i, cb=cb: (i, cb))

    def const(shape):
        return pl.BlockSpec(shape, lambda i: (0,) * len(shape))

    return pl.pallas_call(
        _mixer_post_kernel,
        grid=(m // tm,),
        in_specs=[tok(512), tok(512), tok(512), tok(512, ZC_ML_O), tok(512, ZC_RT_G),
                  tok(1024, ZC_GATES), tok(1024, ZC_GATES + 1), tok(1024, ZC_GATES + 2), tok(D_MODEL),
                  const((1, 512)), const((1, 512)), const((1, D_MODEL)),
                  const((512, D_MODEL)), const((512, D_MODEL)), const((512, D_MODEL)),
                  const((D_MODEL, D_MODEL))],
        out_specs=tok(D_MODEL),
        out_shape=jax.ShapeDtypeStruct((m, D_MODEL), F32),
        compiler_params=_params(1),
        name=name,
    )(hml, osw, ort, z, z, z, z, z, x, mlg, retg, ng, wml, wsw, wrt, wout)


def _xattn_heads(q, k, v):
    lane = lax.broadcasted_iota(jnp.int32, (1, X_HEADS * X_D), 1)
    acc = jnp.zeros(q.shape, F32)
    for h in range(X_HEADS):
        mask = (lane >= X_D * h) & (lane < X_D * (h + 1))
        s = _dot_nt(jnp.where(mask, q, 0.0).astype(BF16), k) * (X_D ** -0.5)
        m = jnp.max(s, axis=1, keepdims=True)
        p = jnp.exp(s - m)
        p = p / jnp.sum(p, axis=1, keepdims=True)
        acc = acc + jnp.where(mask, _dot(p.astype(BF16), v), 0.0)
    return acc


def _xattn_prompt_kernel(x_ref, gin_ref, gout_ref, wcq_ref, kv_ref, wco_ref, o_ref):
    x = x_ref[...]
    q = _dot(_rms(x, gin_ref[...]).astype(BF16), wcq_ref[...])
    k = kv_ref[:, 0:X_HEADS * X_D].astype(BF16)
    v = kv_ref[:, X_HEADS * X_D:2 * X_HEADS * X_D].astype(BF16)
    o = _xattn_heads(q, k, v)
    o_ref[...] = x + _rms(_dot(o.astype(BF16), wco_ref[...]), gout_ref[...])


def xattn_prompt(x, gin, gout, wcq, kv, wco, tm):
    m = x.shape[0]

    def const(shape):
        return pl.BlockSpec(shape, lambda i: (0,) * len(shape))

    tok = pl.BlockSpec((tm, D_MODEL), lambda i: (i, 0))
    return pl.pallas_call(
        _xattn_prompt_kernel,
        grid=(m // tm,),
        in_specs=[tok, const((1, D_MODEL)), const((1, D_MODEL)), const(wcq.shape), const(kv.shape),
                  const(wco.shape)],
        out_specs=tok,
        out_shape=jax.ShapeDtypeStruct((m, D_MODEL), F32),
        compiler_params=_params(1),
        name="xattn_prompt",
    )(x, gin, gout, wcq, kv, wco)


def _xattn_sample_kernel(x_ref, gin_ref, gout_ref, wcq_ref, k_ref, v_ref, wco_ref, o_ref, q_scr, a_scr,
                         *, group):
    R = SEQ_PAD
    x = x_ref[...]
    q_scr[...] = _dot(_rms(x, gin_ref[...]).astype(BF16), wcq_ref[...])
    lane = lax.broadcasted_iota(jnp.int32, (1, X_HEADS * X_D), 1)
    masks = [(lane >= X_D * h) & (lane < X_D * (h + 1)) for h in range(X_HEADS)]

    def body(g, carry):
        rows = pl.ds(pl.multiple_of(g * R, R), R)
        q = q_scr[rows, :]
        qs = jnp.concatenate([jnp.where(mk, q, 0.0) for mk in masks], axis=0)
        s = _dot_nt(qs.astype(BF16), k_ref[g].astype(BF16)) * (X_D ** -0.5)
        m = jnp.max(s, axis=1, keepdims=True)
        p = jnp.exp(s - m)
        p = p / jnp.sum(p, axis=1, keepdims=True)
        o = _dot(p.astype(BF16), v_ref[g].astype(BF16))
        acc = jnp.zeros((R, X_HEADS * X_D), F32)
        for h in range(X_HEADS):
            acc = acc + jnp.where(masks[h], o[R * h:R * (h + 1)], 0.0)
        a_scr[rows, :] = acc
        return carry

    lax.fori_loop(0, group, body, 0)
    o_ref[...] = x + _rms(_dot(a_scr[...].astype(BF16), wco_ref[...]), gout_ref[...])


def xattn_sample(x, gin, gout, wcq, mk, mv, wco, group=16):
    nb = mk.shape[0]
    R = SEQ_PAD

    def const(shape):
        return pl.BlockSpec(shape, lambda i: (0,) * len(shape))

    tok = pl.BlockSpec((group * R, D_MODEL), lambda i: (i, 0))
    mem = pl.BlockSpec((group, N_MEM, X_HEADS * X_D), lambda i: (i, 0, 0))
    return pl.pallas_call(
        functools.partial(_xattn_sample_kernel, group=group),
        grid=(nb // group,),
        in_specs=[tok, const((1, D_MODEL)), const((1, D_MODEL)), const(wcq.shape), mem, mem, const(wco.shape)],
        out_specs=tok,
        out_shape=jax.ShapeDtypeStruct((nb * R, D_MODEL), F32),
        scratch_shapes=[pltpu.VMEM((group * R, X_HEADS * X_D), F32),
                        pltpu.VMEM((group * R, X_HEADS * X_D), F32)],
        compiler_params=_params(1),
        name="xattn_sample",
    )(x, gin, gout, wcq, mk, mv, wco)


FFN_SPLIT = 2


def _ffn_kernel(x_ref, gin_ref, gout_ref, wgu_ref, wd_ref, o_ref):
    x = x_ref[...]
    u = _rms(x, gin_ref[...]).astype(BF16)
    fc = D_FF // FFN_SPLIT
    acc = None
    for c in range(FFN_SPLIT):
        g = _dot(u, wgu_ref[:, fc * c:fc * (c + 1)])
        up = _dot(u, wgu_ref[:, D_FF + fc * c:D_FF + fc * (c + 1)])
        h = (g * _sigmoid(g) * up).astype(BF16)
        part = _dot(h, wd_ref[fc * c:fc * (c + 1), :])
        acc = part if acc is None else acc + part
    o_ref[...] = x + _rms(acc, gout_ref[...])


def ffn(x, gin, gout, wgu, wd, tm, name):
    m = x.shape[0]

    def const(shape):
        return pl.BlockSpec(shape, lambda i: (0,) * len(shape))

    tok = pl.BlockSpec((tm, D_MODEL), lambda i: (i, 0))
    return pl.pallas_call(
        _ffn_kernel,
        grid=(m // tm,),
        in_specs=[tok, const((1, D_MODEL)), const((1, D_MODEL)),
                  pl.BlockSpec(wgu.shape, lambda i: (0, 0), pipeline_mode=pl.Buffered(1)),
                  pl.BlockSpec(wd.shape, lambda i: (0, 0), pipeline_mode=pl.Buffered(1))],
        out_specs=tok,
        out_shape=jax.ShapeDtypeStruct((m, D_MODEL), F32),
        compiler_params=_params(1),
        name=name,
    )(x, gin, gout, wgu, wd)


def _reorder_w_in(w):
    sizes = (512, 512, 512, 4, 4, 512, 512, 128, 128, 512, 512, 512, 512, 3072)
    offs = np.concatenate([[0], np.cumsum(sizes)])
    (ml_q, ml_k, ml_v, ml_i, ml_f, ml_o, sw_q, sw_k, sw_v, rt_q, rt_k, rt_v, rt_g, gates) = [
        w[:, int(offs[i]):int(offs[i + 1])] for i in range(len(sizes))]
    pad = jnp.zeros((w.shape[0], 512 - 128 - 128 - 8), w.dtype)
    out = jnp.concatenate([ml_q, ml_k, ml_v, ml_o, sw_q, sw_k, sw_v, ml_i, ml_f, pad,
                           rt_q, rt_k, rt_v, rt_g, gates], axis=1)
    assert out.shape[1] == Z_COLS
    return out.astype(BF16)


def _row(v):
    return v.reshape(1, -1).astype(F32)


def _decoder_layer(x, z_fn, seq_fn, xattn_fn, lw, tm, tag):
    z = z_fn(x)
    hml, osw, ort, states = seq_fn(z)
    x = mixer_post(hml, osw, ort, z, x, lw["mlg"], lw["retg"], lw["ng"][1], lw["wml"], lw["wsw"], lw["wrt"],
                   lw["wout"], tm=min(256, x.shape[0]), name="mixer_post_" + tag)
    x = xattn_fn(x)
    x = ffn(x, lw["ng"][4], lw["ng"][5], lw["wgu"], lw["wd"], tm=tm, name="ffn_" + tag)
    return x, z, states


def kernel(x_prompt, x_sample, mem_prompt, state_mlstm_C, state_mlstm_n, state_mlstm_m, state_ret_S,
           cache_swa_k, cache_swa_v, cache_mem_k, cache_mem_v, norm_g, w_in, ml_gate_bias, ml_head_g,
           ret_head_g, swa_sinks, rel_bias, w_br_ml, w_br_swa, w_br_ret, w_out, w_cq, w_mkv, w_co,
           w_gu, w_down):
    depth = w_in.shape[0]
    bp, t, d = x_prompt.shape
    assert bp == 1 and d == D_MODEL and t % 1024 == 0
    nb, n_new, _ = x_sample.shape
    assert n_new == N_NEW
    R = SEQ_PAD

    xp = x_prompt.reshape(t, d)
    xs = jnp.pad(x_sample, ((0, 0), (0, R - n_new), (0, 0))).reshape(nb * R, d)
    mem = mem_prompt.reshape(N_MEM, d)
    half = HEAD_D // 2
    inv = ROPE_BASE ** (-jnp.arange(half, dtype=F32) / half)
    inv = jnp.concatenate([inv, inv]).reshape(1, LANES)
    rb = rel_bias.astype(F32).reshape(-1)

    outs_p = {k: [] for k in ("C", "n", "m", "S", "k", "v", "mk", "mv")}
    outs_s = {k: [] for k in ("C", "n", "m", "S", "k", "v")}
    for l in range(depth):
        lw = dict(
            ng=[_row(norm_g[l, i]) for i in range(7)],
            mlg=_row(ml_head_g[l]), retg=_row(ret_head_g[l]),
            wml=w_br_ml[l].astype(BF16), wsw=w_br_swa[l].astype(BF16), wrt=w_br_ret[l].astype(BF16),
            wout=w_out[l].astype(BF16), wgu=w_gu[l].astype(BF16), wd=w_down[l].astype(BF16))
        win = _reorder_w_in(w_in[l])
        wcq = w_cq[l].astype(BF16)
        wco = w_co[l].astype(BF16)
        gb = jnp.concatenate([ml_gate_bias[l, 0], ml_gate_bias[l, 1],
                              jnp.zeros((LANES - 2 * N_HEADS,), F32)]).reshape(1, LANES).astype(F32)
        sinks = swa_sinks[l].astype(F32)

        kv = norm_matmul(mem, lw["ng"][6], w_mkv[l].astype(BF16), tm=N_MEM, tn=2 * X_HEADS * X_D, name="memory_kv")

        def seq_p(z):
            hml, osw, ort, c_, n_, m_, s_ = mixer_seq_prompt(z, gb, inv, sinks, rb)
            return hml, osw, ort, (c_, n_, m_, s_)

        xp, zp, (c_, n_, m_, s_) = _decoder_layer(
            xp,
            lambda x: norm_matmul(x, lw["ng"][0], win, tm=1024, tn=1024, name="in_proj_prompt"),
            seq_p,
            lambda x: xattn_prompt(x, lw["ng"][2], lw["ng"][3], wcq, kv, wco, tm=512),
            lw, tm=512, tag="prompt")
        outs_p["C"].append(c_.reshape(1, N_HEADS, HEAD_D, HEAD_D))
        outs_p["n"].append(n_.reshape(1, N_HEADS, HEAD_D))
        outs_p["m"].append(m_[:, :N_HEADS])
        outs_p["S"].append(s_.reshape(1, N_HEADS, HEAD_D, HEAD_D))
        misc_last = zp[t - CHUNK:, 512 * ZC_MISC:512 * ZC_MISC + 2 * LANES]
        outs_p["k"].append(misc_last[:, :LANES].reshape(1, CHUNK, 2, SWA_D))
        outs_p["v"].append(misc_last[:, LANES:].reshape(1, CHUNK, 2, SWA_D))
        outs_p["mk"].append(kv[:, :X_HEADS * X_D].reshape(1, N_MEM, X_HEADS, X_D))
        outs_p["mv"].append(kv[:, X_HEADS * X_D:].reshape(1, N_MEM, X_HEADS, X_D))

        c0 = state_mlstm_C[l].astype(F32)
        n0 = state_mlstm_n[l].astype(F32).reshape(nb, 1, N_HEADS * HEAD_D)
        m0 = jnp.pad(state_mlstm_m[l].astype(F32), ((0, 0), (0, LANES - N_HEADS))).reshape(nb, 1, LANES)
        s0 = state_ret_S[l].astype(F32)
        ck = cache_swa_k[l].reshape(nb, CHUNK, LANES)
        cv = cache_swa_v[l].reshape(nb, CHUNK, LANES)
        mk = cache_mem_k[l].reshape(nb, N_MEM, X_HEADS * X_D)
        mv = cache_mem_v[l].reshape(nb, N_MEM, X_HEADS * X_D)

        def seq_s(z):
            hml, osw, ort, c_, n_, m_, s_ = mixer_seq_sample(z, c0, n0, m0, s0, ck, cv, gb, inv, sinks, rb)
            return hml, osw, ort, (c_, n_, m_, s_)

        xs, zs, (c_, n_, m_, s_) = _decoder_layer(
            xs,
            lambda x: norm_matmul(x, lw["ng"][0], win, tm=min(1024, nb * R), tn=1024, name="in_proj_sample"),
            seq_s,
            lambda x: xattn_sample(x, lw["ng"][2], lw["ng"][3], wcq, mk, mv, wco),
            lw, tm=min(512, nb * R), tag="sample")
        outs_s["C"].append(c_)
        outs_s["n"].append(n_.reshape(nb, N_HEADS, HEAD_D))
        outs_s["m"].append(m_.reshape(nb, LANES)[:, :N_HEADS])
        outs_s["S"].append(s_)
        misc_new = zs.reshape(nb, R, Z_COLS)[:, :n_new, 512 * ZC_MISC:512 * ZC_MISC + 2 * LANES]
        k_new = misc_new[:, :, :LANES].reshape(nb, n_new, 2, SWA_D)
        v_new = misc_new[:, :, LANES:].reshape(nb, n_new, 2, SWA_D)
        outs_s["k"].append(jnp.concatenate([cache_swa_k[l][:, n_new:], k_new], axis=1))
        outs_s["v"].append(jnp.concatenate([cache_swa_v[l][:, n_new:], v_new], axis=1))

    y_p = xp.reshape(1, t, d)
    y_s = xs.reshape(nb, R, d)[:, :n_new]
    st = lambda d_, k: jnp.stack(d_[k])
    return (y_p, y_s,
            st(outs_p, "C"), st(outs_p, "n"), st(outs_p, "m"), st(outs_p, "S"),
            st(outs_p, "k"), st(outs_p, "v"), st(outs_p, "mk"), st(outs_p, "mv"),
            st(outs_s, "C"), st(outs_s, "n"), st(outs_s, "m"), st(outs_s, "S"),
            st(outs_s, "k"), st(outs_s, "v"))
```

```python
import functools
import math

import numpy as np
import jax
import jax.numpy as jnp
from jax import lax
from jax.experimental import pallas as pl
from jax.experimental.pallas import tpu as pltpu

F32 = jnp.float32
BF16 = jnp.bfloat16

D_MODEL = 1024
EPS = 1e-6
PAST_LEN = 16384
HEAD_D = 128
N_HEADS = 4
CHUNK = 128
SWA_HEADS = 8
SWA_D = 64
N_BUCKETS = 32
MAX_DISTANCE = 128
ROPE_BASE = 10000.0
N_MEM = 256
X_HEADS = 4
X_D = 64
D_FF = 2816
SEQ_PAD = 8
N_NEW = 4
LANES = 128
VMEM_LIMIT = 48 * 1024 * 1024

ZC_ML_Q, ZC_ML_K, ZC_ML_V, ZC_ML_O, ZC_SW_Q, ZC_MISC, ZC_RT_Q, ZC_RT_K, ZC_RT_V, ZC_RT_G = range(10)
ZC_GATES = 5
Z_COLS = 8192
MISC_K, MISC_V, MISC_IF = 0, 128, 256

LOG2E = math.log2(math.e)
LN2 = math.log(2.0)
LOG_GAMMA = tuple(float(v) for v in np.log1p(-np.exp2(-5.0 - np.arange(N_HEADS, dtype=np.float32))))

SWA_SAME = (0, 2, 5, 7)
SWA_SWAP = (1, 3, 4, 6)


def _t5_bucket_bounds():
    n = np.arange(CHUNK)
    max_exact = N_BUCKETS // 2
    nf = np.maximum(n, 1).astype(np.float32)
    large = max_exact + (np.log(nf / np.float32(max_exact)) / np.float32(math.log(MAX_DISTANCE / max_exact))
                         * np.float32(N_BUCKETS - max_exact)).astype(np.int32)
    large = np.minimum(large, N_BUCKETS - 1)
    b = np.where(n < max_exact, n, large)
    assert np.all(np.diff(b) >= 0)
    runs = []
    for d in range(CHUNK):
        if runs and runs[-1][1] == int(b[d]):
            runs[-1][0] = d + 1
        else:
            runs.append([d + 1, int(b[d])])
    return tuple((hi, bk) for hi, bk in runs)


BUCKET_RUNS = _t5_bucket_bounds()


def _dot(a, b):
    return jnp.dot(a, b, preferred_element_type=F32)


def _dot_nt(a, b):
    return lax.dot_general(a, b, (((1,), (1,)), ((), ())), preferred_element_type=F32)


def _dot_tn(a, b):
    return lax.dot_general(a, b, (((0,), (0,)), ((), ())), preferred_element_type=F32)


def _rms(x, g):
    return x * lax.rsqrt(jnp.mean(x * x, axis=-1, keepdims=True) + EPS) * g


def _sigmoid(x):
    return 1.0 / (1.0 + jnp.exp(-x))


def _log_sigmoid(x):
    return jnp.minimum(x, 0.0) - jnp.log1p(jnp.exp(-jnp.abs(x)))


def _params(n_grid):
    return pltpu.CompilerParams(dimension_semantics=("arbitrary",) * n_grid, vmem_limit_bytes=VMEM_LIMIT)


def _norm_matmul_kernel(x_ref, g_ref, w_ref, o_ref, u_ref):
    @pl.when(pl.program_id(1) == 0)
    def _():
        u_ref[...] = _rms(x_ref[...], g_ref[...]).astype(BF16)

    o_ref[...] = _dot(u_ref[...], w_ref[...])


def norm_matmul(x, g, w, tm, tn, name):
    m, k = x.shape
    n = w.shape[1]
    return pl.pallas_call(
        _norm_matmul_kernel,
        grid=(m // tm, n // tn),
        in_specs=[pl.BlockSpec((tm, k), lambda i, j: (i, 0)),
                  pl.BlockSpec((1, k), lambda i, j: (0, 0)),
                  pl.BlockSpec((k, tn), lambda i, j: (0, j))],
        out_specs=pl.BlockSpec((tm, tn), lambda i, j: (i, j)),
        out_shape=jax.ShapeDtypeStruct((m, n), F32),
        scratch_shapes=[pltpu.VMEM((tm, k), BF16)],
        compiler_params=_params(2),
        name=name,
    )(x, g, w)


def _build_swa_table(tab_ref, rb_ref, lq):
    row = lax.broadcasted_iota(jnp.int32, (lq, 2 * CHUNK), 0)
    col = lax.broadcasted_iota(jnp.int32, (lq, 2 * CHUNK), 1)
    delta = row + CHUNK - col
    valid = (delta >= 0) & (delta < CHUNK)
    for var, heads in enumerate((SWA_SAME, SWA_SWAP)):
        for i, h in enumerate(heads):
            val = jnp.full((lq, 2 * CHUNK), rb_ref[BUCKET_RUNS[-1][1] * SWA_HEADS + h], F32)
            for hi, bk in reversed(BUCKET_RUNS[:-1]):
                val = jnp.where(delta < hi, rb_ref[bk * SWA_HEADS + h], val)
            tab_ref[var, i * lq:(i + 1) * lq, :] = jnp.where(valid, val, -jnp.inf)


def _swa_block(q, kbuf, vbuf, tab_ref, sinks_ref, prev_invalid):
    lq = q.shape[0]
    lane = lax.broadcasted_iota(jnp.int32, (1, LANES), 1)
    upper = lane >= SWA_D
    q = q * (SWA_D ** -0.5)
    k = kbuf[...]
    v = vbuf[...]
    ks = (k.astype(BF16), pltpu.roll(k, SWA_D, 1).astype(BF16))
    vs = (v.astype(BF16), pltpu.roll(v, SWA_D, 1).astype(BF16))
    outs = [None] * SWA_HEADS
    for var, heads in enumerate((SWA_SAME, SWA_SWAP)):
        qm = []
        sk = []
        for h in heads:
            blk = q[:, LANES * (h // 2):LANES * (h // 2 + 1)]
            keep = upper if h % 2 == 1 else jnp.logical_not(upper)
            qm.append(jnp.where(keep, blk, 0.0))
            sk.append(jnp.full((lq, 1), sinks_ref[h], F32))
        qs = jnp.concatenate(qm, axis=0).astype(BF16)
        sink = jnp.concatenate(sk, axis=0)
        s = _dot_nt(qs, ks[var]) + tab_ref[var]
        if prev_invalid is not None:
            col = lax.broadcasted_iota(jnp.int32, s.shape, 1)
            s = jnp.where(jnp.logical_and(prev_invalid, col < CHUNK), -jnp.inf, s)
        m = jnp.maximum(jnp.max(s, axis=1, keepdims=True), sink)
        p = jnp.exp(s - m)
        norm = 1.0 / (jnp.sum(p, axis=1, keepdims=True) + jnp.exp(sink - m))
        o = _dot(p.astype(BF16), vs[var]) * norm
        for i, h in enumerate(heads):
            outs[h] = o[i * lq:(i + 1) * lq]
    blocks = [jnp.where(upper, outs[2 * j + 1], outs[2 * j]) for j in range(SWA_HEADS // 2)]
    return jnp.concatenate(blocks, axis=1)


def _rotary_tables(pos, inv_ref):
    lane = lax.broadcasted_iota(jnp.int32, (1, LANES), 1)
    ang = pos * inv_ref[...]
    sin = jnp.sin(ang)
    return jnp.cos(ang), jnp.where(lane < HEAD_D // 2, -sin, sin)


def _rotate(x, cos, sin_signed):
    return x * cos + pltpu.roll(x, HEAD_D // 2, 1) * sin_signed


def _gate_slab(raw, gb_ref):
    lane = lax.broadcasted_iota(jnp.int32, (1, LANES), 1)
    x = raw + gb_ref[...]
    return jnp.where((lane >= N_HEADS) & (lane < 2 * N_HEADS), _log_sigmoid(x), x)


def _mixer_seq_prompt_kernel(mq_ref, mk_ref, mv_ref, swq_ref, misc_ref, rq_ref, rk_ref, rv_ref,
                             gb_ref, inv_ref, sinks_ref, rb_ref,
                             hml_ref, osw_ref, ort_ref, c_ref, n_ref, m_ref, s_ref,
                             kbuf, vbuf, tab, cos_l, sin_l, dec_in, dec_q, dec_k):
    c = pl.program_id(0)
    L = CHUNK

    @pl.when(c == 0)
    def _init():
        c_ref[...] = jnp.zeros(c_ref.shape, F32)
        n_ref[...] = jnp.zeros(n_ref.shape, F32)
        m_ref[...] = jnp.zeros(m_ref.shape, F32)
        s_ref[...] = jnp.zeros(s_ref.shape, F32)
        kbuf[...] = jnp.zeros(kbuf.shape, F32)
        vbuf[...] = jnp.zeros(vbuf.shape, F32)
        _build_swa_table(tab, rb_ref, L)
        r = lax.broadcasted_iota(jnp.int32, (L, L), 0)
        rel = (r - lax.broadcasted_iota(jnp.int32, (L, L), 1)).astype(F32)
        r_f = r.astype(F32)
        ang = r_f * inv_ref[...]
        cos_l[...] = jnp.cos(ang)
        sin_l[...] = jnp.sin(ang)
        for h in range(N_HEADS):
            lg = LOG_GAMMA[h]
            dec_in[h] = jnp.where(rel >= 0.0, jnp.exp(lg * rel), 0.0)
            dec_q[h] = jnp.exp(lg * (r_f + 1.0))
            dec_k[h] = jnp.exp(lg * (L - 1.0 - r_f))

    lane = lax.broadcasted_iota(jnp.int32, (1, LANES), 1)
    row = lax.broadcasted_iota(jnp.int32, (L, L), 0)
    col = lax.broadcasted_iota(jnp.int32, (L, L), 1)
    causal = row >= col

    lf = _gate_slab(misc_ref[:, MISC_IF:MISC_IF + LANES], gb_ref)
    bcum = jnp.dot(causal.astype(F32), lf, precision=lax.Precision.HIGHEST, preferred_element_type=F32)
    colslab = jnp.where(lane < N_HEADS, lf, bcum)
    rowslab = colslab.T
    m_all = m_ref[...]
    m_out = jnp.zeros((1, LANES), F32)
    for h in range(N_HEADS):
        hs = slice(HEAD_D * h, HEAD_D * (h + 1))
        i_c = colslab[:, h:h + 1]
        b_c = colslab[:, N_HEADS + h:N_HEADS + h + 1]
        i_r = rowslab[h:h + 1, :]
        b_r = rowslab[N_HEADS + h:N_HEADS + h + 1, :]
        m_prev = m_all[:, h:h + 1]
        g2 = jnp.where(causal, (i_r - b_r) * LOG2E, -jnp.inf)
        log_inter = b_c + m_prev
        m_t = jnp.maximum(log_inter, b_c + jnp.max(g2, axis=1, keepdims=True) * LN2)
        dm = jnp.exp2((b_c - m_t) * LOG2E + g2)
        inter = jnp.exp(log_inter - m_t)
        q = mq_ref[:, hs]
        k = mk_ref[:, hs] * (HEAD_D ** -0.5)
        v = mv_ref[:, hs]
        qb = q.astype(BF16)
        kb = k.astype(BF16)
        s = _dot_nt(qb, kb) * dm
        c_old = c_ref[h]
        n_old = n_ref[h]
        num = _dot(s.astype(BF16), v.astype(BF16)) + _dot_nt(qb, c_old.astype(BF16)) * inter
        den = jnp.sum(s, axis=1, keepdims=True) + inter * jnp.sum(q * n_old, axis=1, keepdims=True)
        den = jnp.maximum(jnp.abs(den), jnp.exp(-m_t))
        hml_ref[:, hs] = num / den
        m_new = m_t[L - 1:L, :]
        b_last = b_c[L - 1:L, :]
        decay = jnp.exp(b_last + m_prev - m_new)
        w = jnp.exp(b_last - b_c + i_c - m_new)
        c_ref[h] = decay * c_old + _dot_tn((v * w).astype(BF16), kb)
        n_ref[h] = decay * n_old + jnp.sum(k * w, axis=0, keepdims=True)
        m_out = jnp.where(lane == h, m_new, m_out)
    m_ref[...] = m_out

    ang0 = (c * L).astype(F32) * inv_ref[...]
    cos0 = jnp.cos(ang0)
    sin0 = jnp.sin(ang0)
    cos_t = cos_l[...]
    sin_t = sin_l[...]
    cos = cos0 * cos_t - sin0 * sin_t
    sin = sin0 * cos_t + cos0 * sin_t
    sin_signed = jnp.where(lane < HEAD_D // 2, -sin, sin)
    for h in range(N_HEADS):
        hs = slice(HEAD_D * h, HEAD_D * (h + 1))
        qr = _rotate(rq_ref[:, hs], cos, sin_signed).astype(BF16)
        kr = _rotate(rk_ref[:, hs], cos, sin_signed) * (HEAD_D ** -0.5)
        vb = rv_ref[:, hs].astype(BF16)
        s = _dot_nt(qr, kr.astype(BF16)) * dec_in[h]
        s_old = s_ref[h]
        ort_ref[:, hs] = _dot(s.astype(BF16), vb) + _dot(qr, s_old.astype(BF16)) * dec_q[h]
        s_ref[h] = math.exp(LOG_GAMMA[h] * L) * s_old + _dot_tn((kr * dec_k[h]).astype(BF16), vb)

    kbuf[0:L, :] = kbuf[L:2 * L, :]
    vbuf[0:L, :] = vbuf[L:2 * L, :]
    kbuf[L:2 * L, :] = misc_ref[:, MISC_K:MISC_K + LANES]
    vbuf[L:2 * L, :] = misc_ref[:, MISC_V:MISC_V + LANES]
    osw_ref[...] = _swa_block(swq_ref[...], kbuf, vbuf, tab, sinks_ref, c == 0)


def mixer_seq_prompt(z, gb, inv, sinks, rb):
    t = z.shape[0]
    L = CHUNK

    def zspec(cb):
        return pl.BlockSpec((L, 512), lambda c, cb=cb: (c, cb))

    def const(shape):
        return pl.BlockSpec(shape, lambda c: (0,) * len(shape))

    smem = pl.BlockSpec(memory_space=pltpu.SMEM)
    tok = pl.BlockSpec((L, 512), lambda c: (c, 0))
    return pl.pallas_call(
        _mixer_seq_prompt_kernel,
        grid=(t // L,),
        in_specs=[zspec(ZC_ML_Q), zspec(ZC_ML_K), zspec(ZC_ML_V), zspec(ZC_SW_Q), zspec(ZC_MISC),
                  zspec(ZC_RT_Q), zspec(ZC_RT_K), zspec(ZC_RT_V),
                  const((1, LANES)), const((1, LANES)), smem, smem],
        out_specs=[tok, tok, tok,
                   const((N_HEADS, HEAD_D, HEAD_D)), const((N_HEADS, 1, HEAD_D)), const((1, LANES)),
                   const((N_HEADS, HEAD_D, HEAD_D))],
        out_shape=[jax.ShapeDtypeStruct((t, 512), F32)] * 3
        + [jax.ShapeDtypeStruct((N_HEADS, HEAD_D, HEAD_D), F32),
           jax.ShapeDtypeStruct((N_HEADS, 1, HEAD_D), F32),
           jax.ShapeDtypeStruct((1, LANES), F32),
           jax.ShapeDtypeStruct((N_HEADS, HEAD_D, HEAD_D), F32)],
        scratch_shapes=[pltpu.VMEM((2 * L, LANES), F32), pltpu.VMEM((2 * L, LANES), F32),
                        pltpu.VMEM((2, 4 * L, 2 * L), F32),
                        pltpu.VMEM((L, L), F32), pltpu.VMEM((L, L), F32),
                        pltpu.VMEM((N_HEADS, L, L), F32), pltpu.VMEM((N_HEADS, L, L), F32),
                        pltpu.VMEM((N_HEADS, L, L), F32)],
        compiler_params=_params(1),
        name="mixer_seq_prompt",
    )(z, z, z, z, z, z, z, z, gb, inv, sinks, rb)


def _shift_rows(x, d):
    return x if d == 0 else pltpu.roll(x, d, 0)


def _mixer_seq_sample_kernel(mq_ref, mk_ref, mv_ref, swq_ref, misc_ref, rq_ref, rk_ref, rv_ref,
                             c0_ref, n0_ref, m0_ref, s0_ref, ck_ref, cv_ref,
                             gb_ref, inv_ref, sinks_ref, rb_ref, *rest, group, first_layer):
    if not first_layer:
        rest = rest[2:]
    hml_ref, osw_ref, ort_ref, c_all, n_ref, m_ref, s_all, kbuf, vbuf, tab = rest
    R = SEQ_PAD
    if first_layer:
        c_ref, s_ref = c_all.at[0], s_all.at[0]
        c_all[1:] = jnp.zeros((c_all.shape[0] - 1,) + c_all.shape[1:], F32)
        s_all[1:] = jnp.zeros((s_all.shape[0] - 1,) + s_all.shape[1:], F32)
    else:
        c_ref, s_ref = c_all, s_all

    @pl.when(pl.program_id(0) == 0)
    def _init():
        kbuf[...] = jnp.zeros(kbuf.shape, F32)
        vbuf[...] = jnp.zeros(vbuf.shape, F32)
        _build_swa_table(tab, rb_ref, R)

    lane = lax.broadcasted_iota(jnp.int32, (1, LANES), 1)
    l_idx = lax.broadcasted_iota(jnp.int32, (R, 1), 0)
    real = l_idx < N_NEW
    l_f = l_idx.astype(F32)
    cos, sin_signed = _rotary_tables((PAST_LEN + l_idx).astype(F32), inv_ref)

    def body(g, carry):
        rows = pl.ds(pl.multiple_of(g * R, R), R)

        lf = _gate_slab(misc_ref[rows, MISC_IF:MISC_IF + LANES], gb_ref)
        bsum = lf
        for d in range(1, N_NEW):
            bsum = bsum + jnp.where(l_idx >= d, _shift_rows(lf, d), 0.0)
        slab = jnp.where(lane < N_HEADS, lf, bsum)
        shifted = [_shift_rows(slab, d) for d in range(N_NEW)]
        m0 = m0_ref[g]
        n0 = n0_ref[g]
        m_out = jnp.zeros((1, LANES), F32)
        for h in range(N_HEADS):
            hs = slice(HEAD_D * h, HEAD_D * (h + 1))
            i_c = slab[:, h:h + 1]
            b_c = slab[:, N_HEADS + h:N_HEADS + h + 1]
            m_prev = m0[:, h:h + 1]
            logd = []
            for d in range(N_NEW):
                sd = shifted[d]
                ld = b_c - sd[:, N_HEADS + h:N_HEADS + h + 1] + sd[:, h:h + 1]
                logd.append(jnp.where(l_idx >= d, ld, -jnp.inf))
            log_inter = b_c + m_prev
            m_t = log_inter
            for d in range(N_NEW):
                m_t = jnp.maximum(m_t, logd[d])
            inter = jnp.exp(log_inter - m_t)
            q = mq_ref[rows, hs]
            k = mk_ref[rows, hs] * (HEAD_D ** -0.5)
            v = mv_ref[rows, hs]
            c_old = c0_ref[g, h]
            n_old = n0[:, hs]
            num = _dot_nt(q.astype(BF16), c_old.astype(BF16)) * inter
            den = inter * jnp.sum(q * n_old, axis=1, keepdims=True)
            for d in range(N_NEW):
                s_d = jnp.sum(q * _shift_rows(k, d), axis=1, keepdims=True) * jnp.exp(logd[d] - m_t)
                num = num + s_d * _shift_rows(v, d)
                den = den + s_d
            den = jnp.maximum(jnp.abs(den), jnp.exp(-m_t))
            hml_ref[rows, hs] = num / den
            m_new = m_t[N_NEW - 1:N_NEW, :]
            b_last = b_c[N_NEW - 1:N_NEW, :]
            decay = jnp.exp(b_last + m_prev - m_new)
            w = jnp.where(real, jnp.exp(b_last - b_c + i_c - m_new), 0.0)
            c_ref[g, h] = decay * c_old + _dot_tn((v * w).astype(BF16), k.astype(BF16))
            n_ref[g, :, hs] = decay * n_old + jnp.sum(k * w, axis=0, keepdims=True)
            m_out = jnp.where(lane == h, m_new, m_out)
        m_ref[g] = m_out

        for h in range(N_HEADS):
            hs = slice(HEAD_D * h, HEAD_D * (h + 1))
            lg = LOG_GAMMA[h]
            qr = _rotate(rq_ref[rows, hs], cos, sin_signed)
            kr = _rotate(rk_ref[rows, hs], cos, sin_signed) * (HEAD_D ** -0.5)
            v = rv_ref[rows, hs]
            s_old = s0_ref[g, h]
            o = _dot(qr.astype(BF16), s_old.astype(BF16)) * jnp.exp(lg * (l_f + 1.0))
            for d in range(N_NEW):
                s_d = jnp.sum(qr * _shift_rows(kr, d), axis=1, keepdims=True) * math.exp(lg * d)
                o = o + jnp.where(l_idx >= d, s_d, 0.0) * _shift_rows(v, d)
            ort_ref[rows, hs] = o
            k_decay = jnp.where(real, jnp.exp(lg * (N_NEW - 1.0 - l_f)), 0.0)
            s_ref[g, h] = math.exp(lg * N_NEW) * s_old + _dot_tn((kr * k_decay).astype(BF16), v.astype(BF16))

        kbuf[0:CHUNK, :] = ck_ref[g]
        vbuf[0:CHUNK, :] = cv_ref[g]
        kbuf[CHUNK:CHUNK + R, :] = misc_ref[rows, MISC_K:MISC_K + LANES]
        vbuf[CHUNK:CHUNK + R, :] = misc_ref[rows, MISC_V:MISC_V + LANES]
        osw_ref[rows, :] = _swa_block(swq_ref[rows, :], kbuf, vbuf, tab, sinks_ref, None)
        return carry

    lax.fori_loop(0, group, body, 0, unroll=SAMPLE_UNROLL)


SAMPLE_UNROLL = 2


def mixer_seq_sample(z, layer, c0_all, n0, m0, s0_all, ck, cv, gb, inv, sinks, rb, c_prev, s_prev, group=8):
    depth, nb = c0_all.shape[:2]
    R = SEQ_PAD
    first_layer = c_prev is None
    assert first_layer == (layer == 0)

    def zspec(cb):
        return pl.BlockSpec((group * R, 512), lambda i, cb=cb: (i, cb))

    def const(shape):
        return pl.BlockSpec(shape, lambda i: (0,) * len(shape))

    def per_seq(shape):
        return pl.BlockSpec((group,) + shape, lambda i: (i,) + (0,) * len(shape))

    smem = pl.BlockSpec(memory_space=pltpu.SMEM)
    tok = pl.BlockSpec((group * R, 512), lambda i: (i, 0))
    st = (N_HEADS, HEAD_D, HEAD_D)
    st_in = pl.BlockSpec((None, group) + st, lambda i: (layer, i, 0, 0, 0))
    if first_layer:
        st_out = pl.BlockSpec((depth, group) + st, lambda i: (0, i, 0, 0, 0))
        extra_specs, extra_args, aliases = [], [], {}
    else:
        st_out = st_in
        extra_specs = [pl.BlockSpec(memory_space=pl.ANY)] * 2
        extra_args = [c_prev, s_prev]
        aliases = {18: 3, 19: 6}
    return pl.pallas_call(
        functools.partial(_mixer_seq_sample_kernel, group=group, first_layer=first_layer),
        grid=(nb // group,),
        in_specs=[zspec(ZC_ML_Q), zspec(ZC_ML_K), zspec(ZC_ML_V), zspec(ZC_SW_Q), zspec(ZC_MISC),
                  zspec(ZC_RT_Q), zspec(ZC_RT_K), zspec(ZC_RT_V),
                  st_in, per_seq((1, 512)), per_seq((1, LANES)), st_in,
                  per_seq((CHUNK, LANES)), per_seq((CHUNK, LANES)),
                  const((1, LANES)), const((1, LANES)), smem, smem] + extra_specs,
        out_specs=[tok, tok, tok, st_out, per_seq((1, 512)), per_seq((1, LANES)), st_out],
        out_shape=[jax.ShapeDtypeStruct((nb * R, 512), F32)] * 3
        + [jax.ShapeDtypeStruct((depth, nb) + st, F32),
           jax.ShapeDtypeStruct((nb, 1, 512), F32),
           jax.ShapeDtypeStruct((nb, 1, LANES), F32),
           jax.ShapeDtypeStruct((depth, nb) + st, F32)],
        scratch_shapes=[pltpu.VMEM((2 * CHUNK, LANES), F32), pltpu.VMEM((2 * CHUNK, LANES), F32),
                        pltpu.VMEM((2, 4 * R, 2 * CHUNK), F32)],
        input_output_aliases=aliases,
        compiler_params=_params(1),
        name="mixer_seq_sample",
    )(z, z, z, z, z, z, z, z, c0_all, n0, m0, s0_all, ck, cv, gb, inv, sinks, rb, *extra_args)


def _head_rms(h, gain):
    parts = []
    for i in range(N_HEADS):
        blk = h[:, HEAD_D * i:HEAD_D * (i + 1)]
        parts.append(blk * lax.rsqrt(jnp.mean(blk * blk, axis=-1, keepdims=True) + EPS))
    return jnp.concatenate(parts, axis=-1) * gain


def _mixer_post_kernel(hml_ref, osw_ref, ort_ref, mlo_ref, rtg_ref, g0_ref, g1_ref, g2_ref, x_ref,
                       mlg_ref, retg_ref, ng_ref, wml_ref, wsw_ref, wrt_ref, wout_ref, o_ref):
    hm = _head_rms(hml_ref[...], mlg_ref[...]) * _sigmoid(mlo_ref[...])
    y_ml = _dot(hm.astype(BF16), wml_ref[...])
    y_sw = _dot(osw_ref[...].astype(BF16), wsw_ref[...])
    rtg = rtg_ref[...]
    rt = _head_rms(ort_ref[...], retg_ref[...]) * (rtg * _sigmoid(rtg))
    y_rt = _dot(rt.astype(BF16), wrt_ref[...])
    merged = _sigmoid(g0_ref[...]) * y_ml + _sigmoid(g1_ref[...]) * y_sw + _sigmoid(g2_ref[...]) * y_rt
    out = _dot(merged.astype(BF16), wout_ref[...])
    o_ref[...] = x_ref[...] + _rms(out, ng_ref[...])


def mixer_post(hml, osw, ort, z, x, mlg, retg, ng, wml, wsw, wrt, wout, tm, name):
    m = x.shape[0]

    def tok(w, cb=0):
        return pl.BlockSpec((tm, w), lambda i, cb=cb: (i, cb))

    def const(shape):
        return pl.BlockSpec(shape, lambda i: (0,) * len(shape))

    return pl.pallas_call(
        _mixer_post_kernel,
        grid=(m // tm,),
        in_specs=[tok(512), tok(512), tok(512), tok(512, ZC_ML_O), tok(512, ZC_RT_G),
                  tok(1024, ZC_GATES), tok(1024, ZC_GATES + 1), tok(1024, ZC_GATES + 2), tok(D_MODEL),
                  const((1, 512)), const((1, 512)), const((1, D_MODEL)),
                  const((512, D_MODEL)), const((512, D_MODEL)), const((512, D_MODEL)),
                  const((D_MODEL, D_MODEL))],
        out_specs=tok(D_MODEL),
        out_shape=jax.ShapeDtypeStruct((m, D_MODEL), F32),
        compiler_params=_params(1),
        name=name,
    )(hml, osw, ort, z, z, z, z, z, x, mlg, retg, ng, wml, wsw, wrt, wout)


def _xattn_heads(q, k, v):
    lane = lax.broadcasted_iota(jnp.int32, (1, X_HEADS * X_D), 1)
    acc = jnp.zeros(q.shape, F32)
    for h in range(X_HEADS):
        mask = (lane >= X_D * h) & (lane < X_D * (h + 1))
        s = _dot_nt(jnp.where(mask, q, 0.0).astype(BF16), k) * (X_D ** -0.5)
        m = jnp.max(s, axis=1, keepdims=True)
        p = jnp.exp(s - m)
        p = p / jnp.sum(p, axis=1, keepdims=True)
        acc = acc + jnp.where(mask, _dot(p.astype(BF16), v), 0.0)
    return acc


def _xattn_prompt_kernel(x_ref, gin_ref, gout_ref, wcq_ref, kv_ref, wco_ref, o_ref):
    x = x_ref[...]
    q = _dot(_rms(x, gin_ref[...]).astype(BF16), wcq_ref[...])
    k = kv_ref[:, 0:X_HEADS * X_D].astype(BF16)
    v = kv_ref[:, X_HEADS * X_D:2 * X_HEADS * X_D].astype(BF16)
    o = _xattn_heads(q, k, v)
    o_ref[...] = x + _rms(_dot(o.astype(BF16), wco_ref[...]), gout_ref[...])


def xattn_prompt(x, gin, gout, wcq, kv, wco, tm):
    m = x.shape[0]

    def const(shape):
        return pl.BlockSpec(shape, lambda i: (0,) * len(shape))

    tok = pl.BlockSpec((tm, D_MODEL), lambda i: (i, 0))
    return pl.pallas_call(
        _xattn_prompt_kernel,
        grid=(m // tm,),
        in_specs=[tok, const((1, D_MODEL)), const((1, D_MODEL)), const(wcq.shape), const(kv.shape),
                  const(wco.shape)],
        out_specs=tok,
        out_shape=jax.ShapeDtypeStruct((m, D_MODEL), F32),
        compiler_params=_params(1),
        name="xattn_prompt",
    )(x, gin, gout, wcq, kv, wco)


def _xattn_sample_kernel(x_ref, gin_ref, gout_ref, wcq_ref, kt_ref, vt_ref, wco_ref, o_ref, q_scr, a_scr,
                         *, group):
    R = SEQ_PAD
    x = x_ref[...]
    q_scr[...] = _dot(_rms(x, gin_ref[...]).astype(BF16), wcq_ref[...])
    lane = lax.broadcasted_iota(jnp.int32, (1, X_HEADS * X_D), 1)
    masks = [(lane >= X_D * h) & (lane < X_D * (h + 1)) for h in range(X_HEADS)]

    def body(g, carry):
        rows = pl.ds(pl.multiple_of(g * R, R), R)
        q = q_scr[rows, :]
        qs = jnp.concatenate([jnp.where(mk, q, 0.0) for mk in masks], axis=0)
        s = _dot(qs.astype(BF16), kt_ref[g].astype(BF16)) * (X_D ** -0.5)
        m = jnp.max(s, axis=1, keepdims=True)
        p = jnp.exp(s - m)
        p = p / jnp.sum(p, axis=1, keepdims=True)
        o = _dot_nt(p.astype(BF16), vt_ref[g].astype(BF16))
        acc = jnp.zeros((R, X_HEADS * X_D), F32)
        for h in range(X_HEADS):
            acc = acc + jnp.where(masks[h], o[R * h:R * (h + 1)], 0.0)
        a_scr[rows, :] = acc
        return carry

    lax.fori_loop(0, group, body, 0)
    o_ref[...] = x + _rms(_dot(a_scr[...].astype(BF16), wco_ref[...]), gout_ref[...])


def xattn_sample(x, gin, gout, wcq, layer, mkt, mvt, wco, group=16):
    nb = mkt.shape[1]
    R = SEQ_PAD

    def const(shape):
        return pl.BlockSpec(shape, lambda i: (0,) * len(shape))

    tok = pl.BlockSpec((group * R, D_MODEL), lambda i: (i, 0))
    mem = pl.BlockSpec((None, group, X_HEADS * X_D, N_MEM), lambda i: (layer, i, 0, 0))
    return pl.pallas_call(
        functools.partial(_xattn_sample_kernel, group=group),
        grid=(nb // group,),
        in_specs=[tok, const((1, D_MODEL)), const((1, D_MODEL)), const(wcq.shape), mem, mem, const(wco.shape)],
        out_specs=tok,
        out_shape=jax.ShapeDtypeStruct((nb * R, D_MODEL), F32),
        scratch_shapes=[pltpu.VMEM((group * R, X_HEADS * X_D), F32),
                        pltpu.VMEM((group * R, X_HEADS * X_D), F32)],
        compiler_params=_params(1),
        name="xattn_sample",
    )(x, gin, gout, wcq, mkt, mvt, wco)


FFN_SPLIT = 2


def _ffn_kernel(x_ref, gin_ref, gout_ref, wgu_ref, wd_ref, o_ref):
    x = x_ref[...]
    u = _rms(x, gin_ref[...]).astype(BF16)
    fc = D_FF // FFN_SPLIT
    acc = None
    for c in range(FFN_SPLIT):
        g = _dot(u, wgu_ref[:, fc * c:fc * (c + 1)])
        up = _dot(u, wgu_ref[:, D_FF + fc * c:D_FF + fc * (c + 1)])
        h = (g * _sigmoid(g) * up).astype(BF16)
        part = _dot(h, wd_ref[fc * c:fc * (c + 1), :])
        acc = part if acc is None else acc + part
    o_ref[...] = x + _rms(acc, gout_ref[...])


def ffn(x, gin, gout, wgu, wd, tm, name):
    m = x.shape[0]

    def const(shape):
        return pl.BlockSpec(shape, lambda i: (0,) * len(shape))

    tok = pl.BlockSpec((tm, D_MODEL), lambda i: (i, 0))
    return pl.pallas_call(
        _ffn_kernel,
        grid=(m // tm,),
        in_specs=[tok, const((1, D_MODEL)), const((1, D_MODEL)),
                  pl.BlockSpec(wgu.shape, lambda i: (0, 0), pipeline_mode=pl.Buffered(1)),
                  pl.BlockSpec(wd.shape, lambda i: (0, 0), pipeline_mode=pl.Buffered(1))],
        out_specs=tok,
        out_shape=jax.ShapeDtypeStruct((m, D_MODEL), F32),
        compiler_params=_params(1),
        name=name,
    )(x, gin, gout, wgu, wd)


def _reorder_w_in(w):
    sizes = (512, 512, 512, 4, 4, 512, 512, 128, 128, 512, 512, 512, 512, 3072)
    offs = np.concatenate([[0], np.cumsum(sizes)])
    (ml_q, ml_k, ml_v, ml_i, ml_f, ml_o, sw_q, sw_k, sw_v, rt_q, rt_k, rt_v, rt_g, gates) = [
        w[:, int(offs[i]):int(offs[i + 1])] for i in range(len(sizes))]
    pad = jnp.zeros((w.shape[0], 512 - 128 - 128 - 8), w.dtype)
    out = jnp.concatenate([ml_q, ml_k, ml_v, ml_o, sw_q, sw_k, sw_v, ml_i, ml_f, pad,
                           rt_q, rt_k, rt_v, rt_g, gates], axis=1)
    assert out.shape[1] == Z_COLS
    return out.astype(BF16)


def _row(v):
    return v.reshape(1, -1).astype(F32)


def _decoder_layer(x, z_fn, seq_fn, xattn_fn, lw, tm, tag):
    z = z_fn(x)
    hml, osw, ort, states = seq_fn(z)
    x = mixer_post(hml, osw, ort, z, x, lw["mlg"], lw["retg"], lw["ng"][1], lw["wml"], lw["wsw"], lw["wrt"],
                   lw["wout"], tm=min(256, x.shape[0]), name="mixer_post_" + tag)
    x = xattn_fn(x)
    x = ffn(x, lw["ng"][4], lw["ng"][5], lw["wgu"], lw["wd"], tm=tm, name="ffn_" + tag)
    return x, z, states


def kernel(x_prompt, x_sample, mem_prompt, state_mlstm_C, state_mlstm_n, state_mlstm_m, state_ret_S,
           cache_swa_k, cache_swa_v, cache_mem_k, cache_mem_v, norm_g, w_in, ml_gate_bias, ml_head_g,
           ret_head_g, swa_sinks, rel_bias, w_br_ml, w_br_swa, w_br_ret, w_out, w_cq, w_mkv, w_co,
           w_gu, w_down):
    depth = w_in.shape[0]
    bp, t, d = x_prompt.shape
    assert bp == 1 and d == D_MODEL and t % 1024 == 0
    nb, n_new, _ = x_sample.shape
    assert n_new == N_NEW
    R = SEQ_PAD

    xp = x_prompt.reshape(t, d)
    xs = jnp.pad(x_sample, ((0, 0), (0, R - n_new), (0, 0))).reshape(nb * R, d)
    mem = mem_prompt.reshape(N_MEM, d)
    half = HEAD_D // 2
    inv = ROPE_BASE ** (-jnp.arange(half, dtype=F32) / half)
    inv = jnp.concatenate([inv, inv]).reshape(1, LANES)
    rb = rel_bias.astype(F32).reshape(-1)
    mem_kt = jnp.transpose(cache_mem_k, (0, 1, 3, 4, 2)).reshape(depth, nb, X_HEADS * X_D, N_MEM)
    mem_vt = jnp.transpose(cache_mem_v, (0, 1, 3, 4, 2)).reshape(depth, nb, X_HEADS * X_D, N_MEM)

    outs_p = {k: [] for k in ("C", "n", "m", "S", "k", "v", "mk", "mv")}
    outs_s = {k: [] for k in ("n", "m", "k", "v")}
    sample_c = sample_s = None
    for l in range(depth):
        lw = dict(
            ng=[_row(norm_g[l, i]) for i in range(7)],
            mlg=_row(ml_head_g[l]), retg=_row(ret_head_g[l]),
            wml=w_br_ml[l].astype(BF16), wsw=w_br_swa[l].astype(BF16), wrt=w_br_ret[l].astype(BF16),
            wout=w_out[l].astype(BF16), wgu=w_gu[l].astype(BF16), wd=w_down[l].astype(BF16))
        win = _reorder_w_in(w_in[l])
        wcq = w_cq[l].astype(BF16)
        wco = w_co[l].astype(BF16)
        gb = jnp.concatenate([ml_gate_bias[l, 0], ml_gate_bias[l, 1],
                              jnp.zeros((LANES - 2 * N_HEADS,), F32)]).reshape(1, LANES).astype(F32)
        sinks = swa_sinks[l].astype(F32)

        kv = norm_matmul(mem, lw["ng"][6], w_mkv[l].astype(BF16), tm=N_MEM, tn=2 * X_HEADS * X_D, name="memory_kv")

        def seq_p(z):
            hml, osw, ort, c_, n_, m_, s_ = mixer_seq_prompt(z, gb, inv, sinks, rb)
            return hml, osw, ort, (c_, n_, m_, s_)

        xp, zp, (c_, n_, m_, s_) = _decoder_layer(
            xp,
            lambda x: norm_matmul(x, lw["ng"][0], win, tm=1024, tn=1024, name="in_proj_prompt"),
            seq_p,
            lambda x: xattn_prompt(x, lw["ng"][2], lw["ng"][3], wcq, kv, wco, tm=512),
            lw, tm=512, tag="prompt")
        outs_p["C"].append(c_.reshape(1, N_HEADS, HEAD_D, HEAD_D))
        outs_p["n"].append(n_.reshape(1, N_HEADS, HEAD_D))
        outs_p["m"].append(m_[:, :N_HEADS])
        outs_p["S"].append(s_.reshape(1, N_HEADS, HEAD_D, HEAD_D))
        misc_last = zp[t - CHUNK:, 512 * ZC_MISC:512 * ZC_MISC + 2 * LANES]
        outs_p["k"].append(misc_last[:, :LANES].reshape(1, CHUNK, 2, SWA_D))
        outs_p["v"].append(misc_last[:, LANES:].reshape(1, CHUNK, 2, SWA_D))
        outs_p["mk"].append(kv[:, :X_HEADS * X_D].reshape(1, N_MEM, X_HEADS, X_D))
        outs_p["mv"].append(kv[:, X_HEADS * X_D:].reshape(1, N_MEM, X_HEADS, X_D))

        n0 = state_mlstm_n[l].astype(F32).reshape(nb, 1, N_HEADS * HEAD_D)
        m0 = jnp.pad(state_mlstm_m[l].astype(F32), ((0, 0), (0, LANES - N_HEADS))).reshape(nb, 1, LANES)
        ck = cache_swa_k[l].reshape(nb, CHUNK, LANES)
        cv = cache_swa_v[l].reshape(nb, CHUNK, LANES)

        def seq_s(z):
            hml, osw, ort, c_, n_, m_, s_ = mixer_seq_sample(
                z, l, state_mlstm_C, n0, m0, state_ret_S, ck, cv, gb, inv, sinks, rb, sample_c, sample_s)
            return hml, osw, ort, (c_, n_, m_, s_)

        xs, zs, (c_, n_, m_, s_) = _decoder_layer(
            xs,
            lambda x: norm_matmul(x, lw["ng"][0], win, tm=min(1024, nb * R), tn=1024, name="in_proj_sample"),
            seq_s,
            lambda x: xattn_sample(x, lw["ng"][2], lw["ng"][3], wcq, l, mem_kt, mem_vt, wco),
            lw, tm=min(512, nb * R), tag="sample")
        sample_c, sample_s = c_, s_
        outs_s["n"].append(n_.reshape(nb, N_HEADS, HEAD_D))
        outs_s["m"].append(m_.reshape(nb, LANES)[:, :N_HEADS])
        misc_new = zs.reshape(nb, R, Z_COLS)[:, :n_new, 512 * ZC_MISC:512 * ZC_MISC + 2 * LANES]
        k_new = misc_new[:, :, :LANES].reshape(nb, n_new, 2, SWA_D)
        v_new = misc_new[:, :, LANES:].reshape(nb, n_new, 2, SWA_D)
        outs_s["k"].append(jnp.concatenate([cache_swa_k[l][:, n_new:], k_new], axis=1))
        outs_s["v"].append(jnp.concatenate([cache_swa_v[l][:, n_new:], v_new], axis=1))

    y_p = xp.reshape(1, t, d)
    y_s = xs.reshape(nb, R, d)[:, :n_new]
    st = lambda d_, k: jnp.stack(d_[k])
    return (y_p, y_s,
            st(outs_p, "C"), st(outs_p, "n"), st(outs_p, "m"), st(outs_p, "S"),
            st(outs_p, "k"), st(outs_p, "v"), st(outs_p, "mk"), st(outs_p, "mv"),
            sample_c, st(outs_s, "n"), st(outs_s, "m"), sample_s,
            st(outs_s, "k"), st(outs_s, "v"))
```

```python
import functools
import math

import numpy as np
import jax
import jax.numpy as jnp
from jax import lax
from jax.experimental import pallas as pl
from jax.experimental.pallas import tpu as pltpu

F32 = jnp.float32
BF16 = jnp.bfloat16

D_MODEL = 1024
EPS = 1e-6
PAST_LEN = 16384
HEAD_D = 128
N_HEADS = 4
CHUNK = 128
SWA_HEADS = 8
SWA_D = 64
N_BUCKETS = 32
MAX_DISTANCE = 128
ROPE_BASE = 10000.0
N_MEM = 256
X_HEADS = 4
X_D = 64
D_FF = 2816
SEQ_PAD = 8
N_NEW = 4
LANES = 128
MXU_MIN_ROWS = 16
VMEM_LIMIT = 48 * 1024 * 1024

ZC_ML_Q, ZC_ML_K, ZC_ML_V, ZC_ML_O, ZC_SW_Q, ZC_MISC, ZC_RT_Q, ZC_RT_K, ZC_RT_V, ZC_RT_G = range(10)
ZC_GATES = 5
Z_COLS = 8192
MISC_K, MISC_V, MISC_IF = 0, 128, 256

LOG2E = math.log2(math.e)
LN2 = math.log(2.0)
LOG_GAMMA = tuple(float(v) for v in np.log1p(-np.exp2(-5.0 - np.arange(N_HEADS, dtype=np.float32))))

SWA_SAME = (0, 2, 5, 7)
SWA_SWAP = (1, 3, 4, 6)


def _t5_bucket_bounds():
    n = np.arange(CHUNK)
    max_exact = N_BUCKETS // 2
    nf = np.maximum(n, 1).astype(np.float32)
    large = max_exact + (np.log(nf / np.float32(max_exact)) / np.float32(math.log(MAX_DISTANCE / max_exact))
                         * np.float32(N_BUCKETS - max_exact)).astype(np.int32)
    large = np.minimum(large, N_BUCKETS - 1)
    b = np.where(n < max_exact, n, large)
    assert np.all(np.diff(b) >= 0)
    runs = []
    for d in range(CHUNK):
        if runs and runs[-1][1] == int(b[d]):
            runs[-1][0] = d + 1
        else:
            runs.append([d + 1, int(b[d])])
    return tuple((hi, bk) for hi, bk in runs)


BUCKET_RUNS = _t5_bucket_bounds()


def _dot(a, b):
    return jnp.dot(a, b, preferred_element_type=F32)


def _dot_nt(a, b):
    return lax.dot_general(a, b, (((1,), (1,)), ((), ())), preferred_element_type=F32)


def _dot_tn(a, b):
    return lax.dot_general(a, b, (((0,), (0,)), ((), ())), preferred_element_type=F32)


def _rms(x, g):
    return x * lax.rsqrt(jnp.mean(x * x, axis=-1, keepdims=True) + EPS) * g


def _sigmoid(x):
    return 1.0 / (1.0 + jnp.exp(-x))


def _log_sigmoid(x):
    return jnp.minimum(x, 0.0) - jnp.log1p(jnp.exp(-jnp.abs(x)))


def _params(n_grid):
    return pltpu.CompilerParams(dimension_semantics=("arbitrary",) * n_grid, vmem_limit_bytes=VMEM_LIMIT)


def _norm_matmul_kernel(x_ref, g_ref, w_ref, o_ref, u_ref):
    @pl.when(pl.program_id(1) == 0)
    def _():
        u_ref[...] = _rms(x_ref[...], g_ref[...]).astype(BF16)

    o_ref[...] = _dot(u_ref[...], w_ref[...])


def norm_matmul(x, g, w, tm, tn, name):
    m, k = x.shape
    n = w.shape[1]
    return pl.pallas_call(
        _norm_matmul_kernel,
        grid=(m // tm, n // tn),
        in_specs=[pl.BlockSpec((tm, k), lambda i, j: (i, 0)),
                  pl.BlockSpec((1, k), lambda i, j: (0, 0)),
                  pl.BlockSpec((k, tn), lambda i, j: (0, j))],
        out_specs=pl.BlockSpec((tm, tn), lambda i, j: (i, j)),
        out_shape=jax.ShapeDtypeStruct((m, n), F32),
        scratch_shapes=[pltpu.VMEM((tm, k), BF16)],
        compiler_params=_params(2),
        name=name,
    )(x, g, w)


def _build_swa_table(tab_ref, rb_ref, lq):
    row = lax.broadcasted_iota(jnp.int32, (lq, 2 * CHUNK), 0)
    col = lax.broadcasted_iota(jnp.int32, (lq, 2 * CHUNK), 1)
    delta = row + CHUNK - col
    valid = (delta >= 0) & (delta < CHUNK)
    for var, heads in enumerate((SWA_SAME, SWA_SWAP)):
        for i, h in enumerate(heads):
            val = jnp.full((lq, 2 * CHUNK), rb_ref[BUCKET_RUNS[-1][1] * SWA_HEADS + h], F32)
            for hi, bk in reversed(BUCKET_RUNS[:-1]):
                val = jnp.where(delta < hi, rb_ref[bk * SWA_HEADS + h], val)
            tab_ref[var, i * lq:(i + 1) * lq, :] = jnp.where(valid, val, -jnp.inf)


def _swa_block(q, kbuf, vbuf, tab_ref, sinks_ref, prev_invalid):
    lq = q.shape[0]
    lane = lax.broadcasted_iota(jnp.int32, (1, LANES), 1)
    upper = lane >= SWA_D
    q = q * (SWA_D ** -0.5)
    k = kbuf[...]
    v = vbuf[...]
    ks = (k.astype(BF16), pltpu.roll(k, SWA_D, 1).astype(BF16))
    vs = (v.astype(BF16), pltpu.roll(v, SWA_D, 1).astype(BF16))
    outs = [None] * SWA_HEADS
    for var, heads in enumerate((SWA_SAME, SWA_SWAP)):
        qm = []
        sk = []
        for h in heads:
            blk = q[:, LANES * (h // 2):LANES * (h // 2 + 1)]
            keep = upper if h % 2 == 1 else jnp.logical_not(upper)
            qm.append(jnp.where(keep, blk, 0.0))
            sk.append(jnp.full((lq, 1), sinks_ref[h], F32))
        qs = jnp.concatenate(qm, axis=0).astype(BF16)
        sink = jnp.concatenate(sk, axis=0)
        s = _dot_nt(qs, ks[var]) + tab_ref[var]
        if prev_invalid is not None:
            col = lax.broadcasted_iota(jnp.int32, s.shape, 1)
            s = jnp.where(jnp.logical_and(prev_invalid, col < CHUNK), -jnp.inf, s)
        m = jnp.maximum(jnp.max(s, axis=1, keepdims=True), sink)
        p = jnp.exp(s - m)
        norm = 1.0 / (jnp.sum(p, axis=1, keepdims=True) + jnp.exp(sink - m))
        o = _dot(p.astype(BF16), vs[var]) * norm
        for i, h in enumerate(heads):
            outs[h] = o[i * lq:(i + 1) * lq]
    blocks = [jnp.where(upper, outs[2 * j + 1], outs[2 * j]) for j in range(SWA_HEADS // 2)]
    return jnp.concatenate(blocks, axis=1)


def _transpose_bf16(x, eye):
    return _dot_nt(eye, x).astype(BF16)


def _build_swa_table_t(tab_ref, rb_ref):
    L = CHUNK
    srow = lax.broadcasted_iota(jnp.int32, (2 * L, L), 0)
    lcol = lax.broadcasted_iota(jnp.int32, (2 * L, L), 1)
    delta = lcol + L - srow
    valid = (delta >= 0) & (delta < L)
    for var, heads in enumerate((SWA_SAME, SWA_SWAP)):
        for i, h in enumerate(heads):
            val = jnp.full((2 * L, L), rb_ref[BUCKET_RUNS[-1][1] * SWA_HEADS + h], F32)
            for hi, bk in reversed(BUCKET_RUNS[:-1]):
                val = jnp.where(delta < hi, rb_ref[bk * SWA_HEADS + h], val)
            tab_ref[var, :, i * L:(i + 1) * L] = jnp.where(valid, val, -jnp.inf)


def _swa_block_t(q, k_same, k_swap, vt_buf, tab_ref, sinks_ref, prev_invalid):
    L = CHUNK
    lane = lax.broadcasted_iota(jnp.int32, (1, LANES), 1)
    upper = lane >= SWA_D
    q = q * (SWA_D ** -0.5)
    vt = vt_buf[...]
    vts = (vt, jnp.concatenate([vt[SWA_D:], vt[:SWA_D]], axis=0))
    ks = (k_same[...], k_swap[...])
    outs = [None] * SWA_HEADS
    for var, heads in enumerate((SWA_SAME, SWA_SWAP)):
        qm = []
        sk = []
        for h in heads:
            blk = q[:, LANES * (h // 2):LANES * (h // 2 + 1)]
            keep = upper if h % 2 == 1 else jnp.logical_not(upper)
            qm.append(jnp.where(keep, blk, 0.0))
            sk.append(jnp.full((1, L), sinks_ref[h], F32))
        qs = jnp.concatenate(qm, axis=0).astype(BF16)
        sink = jnp.concatenate(sk, axis=1)
        s_t = _dot_nt(ks[var], qs) + tab_ref[var]
        s_prev = jnp.where(prev_invalid, -jnp.inf, s_t[:L])
        s_t = jnp.concatenate([s_prev, s_t[L:]], axis=0)
        m = jnp.maximum(jnp.max(s_t, axis=0, keepdims=True), sink)
        p = jnp.exp(s_t - m)
        norm = 1.0 / (jnp.sum(p, axis=0, keepdims=True) + jnp.exp(sink - m))
        o_t = _dot(vts[var], p.astype(BF16)) * norm
        for i, h in enumerate(heads):
            outs[h] = o_t[:, i * L:(i + 1) * L].T
    blocks = [jnp.where(upper, outs[2 * j + 1], outs[2 * j]) for j in range(SWA_HEADS // 2)]
    return jnp.concatenate(blocks, axis=1)


def _rotary_tables(pos, inv_ref):
    lane = lax.broadcasted_iota(jnp.int32, (1, LANES), 1)
    ang = pos * inv_ref[...]
    sin = jnp.sin(ang)
    return jnp.cos(ang), jnp.where(lane < HEAD_D // 2, -sin, sin)


def _rotate(x, cos, sin_signed):
    return x * cos + pltpu.roll(x, HEAD_D // 2, 1) * sin_signed


def _gate_slab(raw, gb_ref):
    lane = lax.broadcasted_iota(jnp.int32, (1, LANES), 1)
    x = raw + gb_ref[...]
    return jnp.where((lane >= N_HEADS) & (lane < 2 * N_HEADS), _log_sigmoid(x), x)


def _mixer_seq_prompt_kernel(mq_ref, mk_ref, mv_ref, swq_ref, misc_ref, rq_ref, rk_ref, rv_ref,
                             gb_ref, inv_ref, sinks_ref, rb_ref,
                             hml_ref, osw_ref, ort_ref, c_ref, n_ref, m_ref, s_ref,
                             st_scr, k_same, k_swap, vt_buf, tab_t, cos_l, sin_l, dec_in_t, eye_ref):
    c = pl.program_id(0)
    L = CHUNK
    lane = lax.broadcasted_iota(jnp.int32, (1, LANES), 1)
    row = lax.broadcasted_iota(jnp.int32, (L, L), 0)
    col = lax.broadcasted_iota(jnp.int32, (L, L), 1)
    causal_t = row <= col

    @pl.when(c == 0)
    def _init():
        c_ref[...] = jnp.zeros(c_ref.shape, F32)
        n_ref[...] = jnp.zeros(n_ref.shape, F32)
        m_ref[...] = jnp.zeros(m_ref.shape, F32)
        s_ref[...] = jnp.zeros(s_ref.shape, F32)
        st_scr[...] = jnp.zeros(st_scr.shape, F32)
        k_same[...] = jnp.zeros(k_same.shape, BF16)
        k_swap[...] = jnp.zeros(k_swap.shape, BF16)
        vt_buf[...] = jnp.zeros(vt_buf.shape, BF16)
        _build_swa_table_t(tab_t, rb_ref)
        ang = row.astype(F32) * inv_ref[...]
        cos_l[...] = jnp.cos(ang)
        sin_l[...] = jnp.sin(ang)
        rel_t = (col - row).astype(F32)
        for h in range(N_HEADS):
            dec_in_t[h] = jnp.where(causal_t, jnp.exp(LOG_GAMMA[h] * rel_t), 0.0)
        eye_ref[...] = jnp.where(row == col, 1.0, 0.0).astype(BF16)

    eye = eye_ref[...]

    lf = _gate_slab(misc_ref[:, MISC_IF:MISC_IF + LANES], gb_ref)
    bcum = jnp.dot((row >= col).astype(F32), lf, precision=lax.Precision.HIGHEST, preferred_element_type=F32)
    colslab = jnp.where(lane < N_HEADS, lf, bcum)
    rowslab = colslab.T
    m_all = m_ref[...]
    m_out = jnp.zeros((1, LANES), F32)
    for h in range(N_HEADS):
        hs = slice(HEAD_D * h, HEAD_D * (h + 1))
        g_c = colslab[:, h:h + 1] - colslab[:, N_HEADS + h:N_HEADS + h + 1]
        i_r = rowslab[h:h + 1, :]
        b_r = rowslab[N_HEADS + h:N_HEADS + h + 1, :]
        m_prev = m_all[:, h:h + 1]
        gm = jnp.where(causal_t, g_c, -jnp.inf)
        mx = jnp.maximum(jnp.max(gm, axis=0, keepdims=True), m_prev)
        m_t = b_r + mx
        inter = jnp.exp(m_prev - mx)
        dm_t = jnp.exp(gm - mx)
        q = mq_ref[:, hs]
        k = mk_ref[:, hs] * (HEAD_D ** -0.5)
        qb = q.astype(BF16)
        kb = k.astype(BF16)
        vt = _transpose_bf16(mv_ref[:, hs].astype(BF16), eye)
        s_t = _dot_nt(kb, qb) * dm_t
        c_old = c_ref[h]
        n_old = n_ref[h]
        num_t = _dot(vt, s_t.astype(BF16)) + _dot_nt(c_old.astype(BF16), qb) * inter
        nq = _dot_nt(jnp.broadcast_to(n_old, (MXU_MIN_ROWS, HEAD_D)).astype(BF16), qb)[0:1, :]
        den = jnp.sum(s_t, axis=0, keepdims=True) + inter * nq
        den = jnp.maximum(jnp.abs(den), jnp.exp(-m_t))
        hml_ref[:, hs] = (num_t * (1.0 / den)).T
        m_new = m_t[:, L - 1:L]
        b_last = b_r[:, L - 1:L]
        decay = jnp.exp(b_last + m_prev - m_new)
        w_r = jnp.exp((i_r - b_r) + (b_last - m_new))
        c_ref[h] = decay * c_old + _dot((vt.astype(F32) * w_r).astype(BF16), kb)
        n_ref[h] = decay * n_old + _dot(jnp.broadcast_to(w_r, (MXU_MIN_ROWS, L)).astype(BF16), kb)[0:1, :]
        m_out = jnp.where(lane == h, m_new, m_out)
    m_ref[...] = m_out

    ang0 = (c * L).astype(F32) * inv_ref[...]
    cos0 = jnp.cos(ang0)
    sin0 = jnp.sin(ang0)
    cos_t = cos_l[...]
    sin_t = sin_l[...]
    cos = cos0 * cos_t - sin0 * sin_t
    sin = sin0 * cos_t + cos0 * sin_t
    sin_signed = jnp.where(lane < HEAD_D // 2, -sin, sin)
    l_row = lane.astype(F32)
    for h in range(N_HEADS):
        hs = slice(HEAD_D * h, HEAD_D * (h + 1))
        lg = LOG_GAMMA[h]
        qr = _rotate(rq_ref[:, hs], cos, sin_signed).astype(BF16)
        kr = (_rotate(rk_ref[:, hs], cos, sin_signed) * (HEAD_D ** -0.5)).astype(BF16)
        vt = _transpose_bf16(rv_ref[:, hs].astype(BF16), eye)
        s_t = _dot_nt(kr, qr) * dec_in_t[h]
        st_old = st_scr[h]
        o_t = _dot(vt, s_t.astype(BF16)) + _dot_nt(st_old.astype(BF16), qr) * jnp.exp(lg * (l_row + 1.0))
        ort_ref[:, hs] = o_t.T
        k_decay = jnp.exp(lg * (L - 1.0 - l_row))
        st_scr[h] = math.exp(lg * L) * st_old + _dot((vt.astype(F32) * k_decay).astype(BF16), kr)

    @pl.when(c == pl.num_programs(0) - 1)
    def _emit_state():
        for h in range(N_HEADS):
            s_ref[h] = st_scr[h].T

    k_new = misc_ref[:, MISC_K:MISC_K + LANES]
    k_same[0:L, :] = k_same[L:2 * L, :]
    k_swap[0:L, :] = k_swap[L:2 * L, :]
    k_same[L:2 * L, :] = k_new.astype(BF16)
    k_swap[L:2 * L, :] = pltpu.roll(k_new, SWA_D, 1).astype(BF16)
    vt_buf[:, 0:L] = vt_buf[:, L:2 * L]
    vt_buf[:, L:2 * L] = _transpose_bf16(misc_ref[:, MISC_V:MISC_V + LANES].astype(BF16), eye)
    osw_ref[...] = _swa_block_t(swq_ref[...], k_same, k_swap, vt_buf, tab_t, sinks_ref, c == 0)


def mixer_seq_prompt(z, gb, inv, sinks, rb):
    t = z.shape[0]
    L = CHUNK

    def zspec(cb):
        return pl.BlockSpec((L, 512), lambda c, cb=cb: (c, cb))

    def const(shape):
        return pl.BlockSpec(shape, lambda c: (0,) * len(shape))

    smem = pl.BlockSpec(memory_space=pltpu.SMEM)
    tok = pl.BlockSpec((L, 512), lambda c: (c, 0))
    return pl.pallas_call(
        _mixer_seq_prompt_kernel,
        grid=(t // L,),
        in_specs=[zspec(ZC_ML_Q), zspec(ZC_ML_K), zspec(ZC_ML_V), zspec(ZC_SW_Q), zspec(ZC_MISC),
                  zspec(ZC_RT_Q), zspec(ZC_RT_K), zspec(ZC_RT_V),
                  const((1, LANES)), const((1, LANES)), smem, smem],
        out_specs=[tok, tok, tok,
                   const((N_HEADS, HEAD_D, HEAD_D)), const((N_HEADS, 1, HEAD_D)), const((1, LANES)),
                   const((N_HEADS, HEAD_D, HEAD_D))],
        out_shape=[jax.ShapeDtypeStruct((t, 512), F32)] * 3
        + [jax.ShapeDtypeStruct((N_HEADS, HEAD_D, HEAD_D), F32),
           jax.ShapeDtypeStruct((N_HEADS, 1, HEAD_D), F32),
           jax.ShapeDtypeStruct((1, LANES), F32),
           jax.ShapeDtypeStruct((N_HEADS, HEAD_D, HEAD_D), F32)],
        scratch_shapes=[pltpu.VMEM((N_HEADS, L, L), F32),
                        pltpu.VMEM((2 * L, LANES), BF16), pltpu.VMEM((2 * L, LANES), BF16),
                        pltpu.VMEM((LANES, 2 * L), BF16),
                        pltpu.VMEM((2, 2 * L, 4 * L), F32),
                        pltpu.VMEM((L, L), F32), pltpu.VMEM((L, L), F32),
                        pltpu.VMEM((N_HEADS, L, L), F32),
                        pltpu.VMEM((L, L), BF16)],
        compiler_params=_params(1),
        name="mixer_seq_prompt",
    )(z, z, z, z, z, z, z, z, gb, inv, sinks, rb)


def _shift_rows(x, d):
    return x if d == 0 else pltpu.roll(x, d, 0)


def _mixer_seq_sample_kernel(mq_ref, mk_ref, mv_ref, swq_ref, misc_ref, rq_ref, rk_ref, rv_ref,
                             c0_ref, n0_ref, m0_ref, s0_ref, ck_ref, cv_ref,
                             gb_ref, inv_ref, sinks_ref, rb_ref, *rest, group, first_layer):
    if not first_layer:
        rest = rest[2:]
    hml_ref, osw_ref, ort_ref, c_all, n_ref, m_ref, s_all, kbuf, vbuf, tab = rest
    R = SEQ_PAD
    if first_layer:
        c_ref, s_ref = c_all.at[0], s_all.at[0]
        c_all[1:] = jnp.zeros((c_all.shape[0] - 1,) + c_all.shape[1:], F32)
        s_all[1:] = jnp.zeros((s_all.shape[0] - 1,) + s_all.shape[1:], F32)
    else:
        c_ref, s_ref = c_all, s_all

    @pl.when(pl.program_id(0) == 0)
    def _init():
        kbuf[...] = jnp.zeros(kbuf.shape, F32)
        vbuf[...] = jnp.zeros(vbuf.shape, F32)
        _build_swa_table(tab, rb_ref, R)

    lane = lax.broadcasted_iota(jnp.int32, (1, LANES), 1)
    l_idx = lax.broadcasted_iota(jnp.int32, (R, 1), 0)
    real = l_idx < N_NEW
    l_f = l_idx.astype(F32)
    cos, sin_signed = _rotary_tables((PAST_LEN + l_idx).astype(F32), inv_ref)

    def body(g, carry):
        rows = pl.ds(pl.multiple_of(g * R, R), R)

        lf = _gate_slab(misc_ref[rows, MISC_IF:MISC_IF + LANES], gb_ref)
        bsum = lf
        for d in range(1, N_NEW):
            bsum = bsum + jnp.where(l_idx >= d, _shift_rows(lf, d), 0.0)
        slab = jnp.where(lane < N_HEADS, lf, bsum)
        shifted = [_shift_rows(slab, d) for d in range(N_NEW)]
        m0 = m0_ref[g]
        n0 = n0_ref[g]
        m_out = jnp.zeros((1, LANES), F32)
        for h in range(N_HEADS):
            hs = slice(HEAD_D * h, HEAD_D * (h + 1))
            i_c = slab[:, h:h + 1]
            b_c = slab[:, N_HEADS + h:N_HEADS + h + 1]
            m_prev = m0[:, h:h + 1]
            logd = []
            for d in range(N_NEW):
                sd = shifted[d]
                ld = b_c - sd[:, N_HEADS + h:N_HEADS + h + 1] + sd[:, h:h + 1]
                logd.append(jnp.where(l_idx >= d, ld, -jnp.inf))
            log_inter = b_c + m_prev
            m_t = log_inter
            for d in range(N_NEW):
                m_t = jnp.maximum(m_t, logd[d])
            inter = jnp.exp(log_inter - m_t)
            q = mq_ref[rows, hs]
            k = mk_ref[rows, hs] * (HEAD_D ** -0.5)
            v = mv_ref[rows, hs]
            c_old = c0_ref[g, h]
            n_old = n0[:, hs]
            num = _dot_nt(q.astype(BF16), c_old.astype(BF16)) * inter
            den = inter * jnp.sum(q * n_old, axis=1, keepdims=True)
            for d in range(N_NEW):
                s_d = jnp.sum(q * _shift_rows(k, d), axis=1, keepdims=True) * jnp.exp(logd[d] - m_t)
                num = num + s_d * _shift_rows(v, d)
                den = den + s_d
            den = jnp.maximum(jnp.abs(den), jnp.exp(-m_t))
            hml_ref[rows, hs] = num / den
            m_new = m_t[N_NEW - 1:N_NEW, :]
            b_last = b_c[N_NEW - 1:N_NEW, :]
            decay = jnp.exp(b_last + m_prev - m_new)
            w = jnp.where(real, jnp.exp(b_last - b_c + i_c - m_new), 0.0)
            c_ref[g, h] = decay * c_old + _dot_tn((v * w).astype(BF16), k.astype(BF16))
            n_ref[g, :, hs] = decay * n_old + jnp.sum(k * w, axis=0, keepdims=True)
            m_out = jnp.where(lane == h, m_new, m_out)
        m_ref[g] = m_out

        for h in range(N_HEADS):
            hs = slice(HEAD_D * h, HEAD_D * (h + 1))
            lg = LOG_GAMMA[h]
            qr = _rotate(rq_ref[rows, hs], cos, sin_signed)
            kr = _rotate(rk_ref[rows, hs], cos, sin_signed) * (HEAD_D ** -0.5)
            v = rv_ref[rows, hs]
            s_old = s0_ref[g, h]
            o = _dot(qr.astype(BF16), s_old.astype(BF16)) * jnp.exp(lg * (l_f + 1.0))
            for d in range(N_NEW):
                s_d = jnp.sum(qr * _shift_rows(kr, d), axis=1, keepdims=True) * math.exp(lg * d)
                o = o + jnp.where(l_idx >= d, s_d, 0.0) * _shift_rows(v, d)
            ort_ref[rows, hs] = o
            k_decay = jnp.where(real, jnp.exp(lg * (N_NEW - 1.0 - l_f)), 0.0)
            s_ref[g, h] = math.exp(lg * N_NEW) * s_old + _dot_tn((kr * k_decay).astype(BF16), v.astype(BF16))

        kbuf[0:CHUNK, :] = ck_ref[g]
        vbuf[0:CHUNK, :] = cv_ref[g]
        kbuf[CHUNK:CHUNK + R, :] = misc_ref[rows, MISC_K:MISC_K + LANES]
        vbuf[CHUNK:CHUNK + R, :] = misc_ref[rows, MISC_V:MISC_V + LANES]
        osw_ref[rows, :] = _swa_block(swq_ref[rows, :], kbuf, vbuf, tab, sinks_ref, None)
        return carry

    lax.fori_loop(0, group, body, 0, unroll=SAMPLE_UNROLL)


SAMPLE_UNROLL = 2


def mixer_seq_sample(z, layer, c0_all, n0, m0, s0_all, ck, cv, gb, inv, sinks, rb, c_prev, s_prev, group=8):
    depth, nb = c0_all.shape[:2]
    R = SEQ_PAD
    first_layer = c_prev is None
    assert first_layer == (layer == 0)

    def zspec(cb):
        return pl.BlockSpec((group * R, 512), lambda i, cb=cb: (i, cb))

    def const(shape):
        return pl.BlockSpec(shape, lambda i: (0,) * len(shape))

    def per_seq(shape):
        return pl.BlockSpec((group,) + shape, lambda i: (i,) + (0,) * len(shape))

    smem = pl.BlockSpec(memory_space=pltpu.SMEM)
    tok = pl.BlockSpec((group * R, 512), lambda i: (i, 0))
    st = (N_HEADS, HEAD_D, HEAD_D)
    st_in = pl.BlockSpec((None, group) + st, lambda i: (layer, i, 0, 0, 0))
    if first_layer:
        st_out = pl.BlockSpec((depth, group) + st, lambda i: (0, i, 0, 0, 0))
        extra_specs, extra_args, aliases = [], [], {}
    else:
        st_out = st_in
        extra_specs = [pl.BlockSpec(memory_space=pl.ANY)] * 2
        extra_args = [c_prev, s_prev]
        aliases = {18: 3, 19: 6}
    return pl.pallas_call(
        functools.partial(_mixer_seq_sample_kernel, group=group, first_layer=first_layer),
        grid=(nb // group,),
        in_specs=[zspec(ZC_ML_Q), zspec(ZC_ML_K), zspec(ZC_ML_V), zspec(ZC_SW_Q), zspec(ZC_MISC),
                  zspec(ZC_RT_Q), zspec(ZC_RT_K), zspec(ZC_RT_V),
                  st_in, per_seq((1, 512)), per_seq((1, LANES)), st_in,
                  per_seq((CHUNK, LANES)), per_seq((CHUNK, LANES)),
                  const((1, LANES)), const((1, LANES)), smem, smem] + extra_specs,
        out_specs=[tok, tok, tok, st_out, per_seq((1, 512)), per_seq((1, LANES)), st_out],
        out_shape=[jax.ShapeDtypeStruct((nb * R, 512), F32)] * 3
        + [jax.ShapeDtypeStruct((depth, nb) + st, F32),
           jax.ShapeDtypeStruct((nb, 1, 512), F32),
           jax.ShapeDtypeStruct((nb, 1, LANES), F32),
           jax.ShapeDtypeStruct((depth, nb) + st, F32)],
        scratch_shapes=[pltpu.VMEM((2 * CHUNK, LANES), F32), pltpu.VMEM((2 * CHUNK, LANES), F32),
                        pltpu.VMEM((2, 4 * R, 2 * CHUNK), F32)],
        input_output_aliases=aliases,
        compiler_params=_params(1),
        name="mixer_seq_sample",
    )(z, z, z, z, z, z, z, z, c0_all, n0, m0, s0_all, ck, cv, gb, inv, sinks, rb, *extra_args)


def _head_rms(h, gain):
    parts = []
    for i in range(N_HEADS):
        blk = h[:, HEAD_D * i:HEAD_D * (i + 1)]
        parts.append(blk * lax.rsqrt(jnp.mean(blk * blk, axis=-1, keepdims=True) + EPS))
    return jnp.concatenate(parts, axis=-1) * gain


def _mixer_post_kernel(hml_ref, osw_ref, ort_ref, mlo_ref, rtg_ref, g0_ref, g1_ref, g2_ref, x_ref,
                       mlg_ref, retg_ref, ng_ref, wml_ref, wsw_ref, wrt_ref, wout_ref, o_ref):
    hm = _head_rms(hml_ref[...], mlg_ref[...]) * _sigmoid(mlo_ref[...])
    y_ml = _dot(hm.astype(BF16), wml_ref[...])
    y_sw = _dot(osw_ref[...].astype(BF16), wsw_ref[...])
    rtg = rtg_ref[...]
    rt = _head_rms(ort_ref[...], retg_ref[...]) * (rtg * _sigmoid(rtg))
    y_rt = _dot(rt.astype(BF16), wrt_ref[...])
    merged = _sigmoid(g0_ref[...]) * y_ml + _sigmoid(g1_ref[...]) * y_sw + _sigmoid(g2_ref[...]) * y_rt
    out = _dot(merged.astype(BF16), wout_ref[...])
    o_ref[...] = x_ref[...] + _rms(out, ng_ref[...])


def mixer_post(hml, osw, ort, z, x, mlg, retg, ng, wml, wsw, wrt, wout, tm, name):
    m = x.shape[0]

    def tok(w, cb=0):
        return pl.BlockSpec((tm, w), lambda i, cb=cb: (i, cb))

    def const(shape):
        return pl.BlockSpec(shape, lambda i: (0,) * len(shape))

    return pl.pallas_call(
        _mixer_post_kernel,
        grid=(m // tm,),
        in_specs=[tok(512), tok(512), tok(512), tok(512, ZC_ML_O), tok(512, ZC_RT_G),
                  tok(1024, ZC_GATES), tok(1024, ZC_GATES + 1), tok(1024, ZC_GATES + 2), tok(D_MODEL),
                  const((1, 512)), const((1, 512)), const((1, D_MODEL)),
                  const((512, D_MODEL)), const((512, D_MODEL)), const((512, D_MODEL)),
                  const((D_MODEL, D_MODEL))],
        out_specs=tok(D_MODEL),
        out_shape=jax.ShapeDtypeStruct((m, D_MODEL), F32),
        compiler_params=_params(1),
        name=name,
    )(hml, osw, ort, z, z, z, z, z, x, mlg, retg, ng, wml, wsw, wrt, wout)


def _xattn_heads(q, k, v):
    lane = lax.broadcasted_iota(jnp.int32, (1, X_HEADS * X_D), 1)
    acc = jnp.zeros(q.shape, F32)
    for h in range(X_HEADS):
        mask = (lane >= X_D * h) & (lane < X_D * (h + 1))
        s = _dot_nt(jnp.where(mask, q, 0.0).astype(BF16), k) * (X_D ** -0.5)
        m = jnp.max(s, axis=1, keepdims=True)
        p = jnp.exp(s - m)
        p = p / jnp.sum(p, axis=1, keepdims=True)
        acc = acc + jnp.where(mask, _dot(p.astype(BF16), v), 0.0)
    return acc


def _xattn_prompt_kernel(x_ref, gin_ref, gout_ref, wcq_ref, kv_ref, wco_ref, o_ref):
    x = x_ref[...]
    q = _dot(_rms(x, gin_ref[...]).astype(BF16), wcq_ref[...])
    k = kv_ref[:, 0:X_HEADS * X_D].astype(BF16)
    v = kv_ref[:, X_HEADS * X_D:2 * X_HEADS * X_D].astype(BF16)
    o = _xattn_heads(q, k, v)
    o_ref[...] = x + _rms(_dot(o.astype(BF16), wco_ref[...]), gout_ref[...])


def xattn_prompt(x, gin, gout, wcq, kv, wco, tm):
    m = x.shape[0]

    def const(shape):
        return pl.BlockSpec(shape, lambda i: (0,) * len(shape))

    tok = pl.BlockSpec((tm, D_MODEL), lambda i: (i, 0))
    return pl.pallas_call(
        _xattn_prompt_kernel,
        grid=(m // tm,),
        in_specs=[tok, const((1, D_MODEL)), const((1, D_MODEL)), const(wcq.shape), const(kv.shape),
                  const(wco.shape)],
        out_specs=tok,
        out_shape=jax.ShapeDtypeStruct((m, D_MODEL), F32),
        compiler_params=_params(1),
        name="xattn_prompt",
    )(x, gin, gout, wcq, kv, wco)


def _xattn_sample_kernel(x_ref, gin_ref, gout_ref, wcq_ref, kt_ref, vt_ref, wco_ref, o_ref, q_scr, a_scr,
                         *, group):
    R = SEQ_PAD
    x = x_ref[...]
    q_scr[...] = _dot(_rms(x, gin_ref[...]).astype(BF16), wcq_ref[...])
    lane = lax.broadcasted_iota(jnp.int32, (1, X_HEADS * X_D), 1)
    masks = [(lane >= X_D * h) & (lane < X_D * (h + 1)) for h in range(X_HEADS)]

    def body(g, carry):
        rows = pl.ds(pl.multiple_of(g * R, R), R)
        q = q_scr[rows, :]
        qs = jnp.concatenate([jnp.where(mk, q, 0.0) for mk in masks], axis=0)
        s = _dot(qs.astype(BF16), kt_ref[g].astype(BF16)) * (X_D ** -0.5)
        m = jnp.max(s, axis=1, keepdims=True)
        p = jnp.exp(s - m)
        p = p / jnp.sum(p, axis=1, keepdims=True)
        o = _dot_nt(p.astype(BF16), vt_ref[g].astype(BF16))
        acc = jnp.zeros((R, X_HEADS * X_D), F32)
        for h in range(X_HEADS):
            acc = acc + jnp.where(masks[h], o[R * h:R * (h + 1)], 0.0)
        a_scr[rows, :] = acc
        return carry

    lax.fori_loop(0, group, body, 0)
    o_ref[...] = x + _rms(_dot(a_scr[...].astype(BF16), wco_ref[...]), gout_ref[...])


def xattn_sample(x, gin, gout, wcq, layer, mkt, mvt, wco, group=16):
    nb = mkt.shape[1]
    R = SEQ_PAD

    def const(shape):
        return pl.BlockSpec(shape, lambda i: (0,) * len(shape))

    tok = pl.BlockSpec((group * R, D_MODEL), lambda i: (i, 0))
    mem = pl.BlockSpec((None, group, X_HEADS * X_D, N_MEM), lambda i: (layer, i, 0, 0))
    return pl.pallas_call(
        functools.partial(_xattn_sample_kernel, group=group),
        grid=(nb // group,),
        in_specs=[tok, const((1, D_MODEL)), const((1, D_MODEL)), const(wcq.shape), mem, mem, const(wco.shape)],
        out_specs=tok,
        out_shape=jax.ShapeDtypeStruct((nb * R, D_MODEL), F32),
        scratch_shapes=[pltpu.VMEM((group * R, X_HEADS * X_D), F32),
                        pltpu.VMEM((group * R, X_HEADS * X_D), F32)],
        compiler_params=_params(1),
        name="xattn_sample",
    )(x, gin, gout, wcq, mkt, mvt, wco)


FFN_SPLIT = 2


def _ffn_kernel(x_ref, gin_ref, gout_ref, wgu_ref, wd_ref, o_ref):
    x = x_ref[...]
    u = _rms(x, gin_ref[...]).astype(BF16)
    fc = D_FF // FFN_SPLIT
    acc = None
    for c in range(FFN_SPLIT):
        g = _dot(u, wgu_ref[:, fc * c:fc * (c + 1)])
        up = _dot(u, wgu_ref[:, D_FF + fc * c:D_FF + fc * (c + 1)])
        h = (g * _sigmoid(g) * up).astype(BF16)
        part = _dot(h, wd_ref[fc * c:fc * (c + 1), :])
        acc = part if acc is None else acc + part
    o_ref[...] = x + _rms(acc, gout_ref[...])


def ffn(x, gin, gout, wgu, wd, tm, name):
    m = x.shape[0]

    def const(shape):
        return pl.BlockSpec(shape, lambda i: (0,) * len(shape))

    tok = pl.BlockSpec((tm, D_MODEL), lambda i: (i, 0))
    return pl.pallas_call(
        _ffn_kernel,
        grid=(m // tm,),
        in_specs=[tok, const((1, D_MODEL)), const((1, D_MODEL)),
                  pl.BlockSpec(wgu.shape, lambda i: (0, 0), pipeline_mode=pl.Buffered(1)),
                  pl.BlockSpec(wd.shape, lambda i: (0, 0), pipeline_mode=pl.Buffered(1))],
        out_specs=tok,
        out_shape=jax.ShapeDtypeStruct((m, D_MODEL), F32),
        compiler_params=_params(1),
        name=name,
    )(x, gin, gout, wgu, wd)


def _reorder_w_in(w):
    sizes = (512, 512, 512, 4, 4, 512, 512, 128, 128, 512, 512, 512, 512, 3072)
    offs = np.concatenate([[0], np.cumsum(sizes)])
    (ml_q, ml_k, ml_v, ml_i, ml_f, ml_o, sw_q, sw_k, sw_v, rt_q, rt_k, rt_v, rt_g, gates) = [
        w[:, int(offs[i]):int(offs[i + 1])] for i in range(len(sizes))]
    pad = jnp.zeros((w.shape[0], 512 - 128 - 128 - 8), w.dtype)
    out = jnp.concatenate([ml_q, ml_k, ml_v, ml_o, sw_q, sw_k, sw_v, ml_i, ml_f, pad,
                           rt_q, rt_k, rt_v, rt_g, gates], axis=1)
    assert out.shape[1] == Z_COLS
    return out.astype(BF16)


def _row(v):
    return v.reshape(1, -1).astype(F32)


def _decoder_layer(x, z_fn, seq_fn, xattn_fn, lw, tm, tag):
    z = z_fn(x)
    hml, osw, ort, states = seq_fn(z)
    x = mixer_post(hml, osw, ort, z, x, lw["mlg"], lw["retg"], lw["ng"][1], lw["wml"], lw["wsw"], lw["wrt"],
                   lw["wout"], tm=min(256, x.shape[0]), name="mixer_post_" + tag)
    x = xattn_fn(x)
    x = ffn(x, lw["ng"][4], lw["ng"][5], lw["wgu"], lw["wd"], tm=tm, name="ffn_" + tag)
    return x, z, states


def kernel(x_prompt, x_sample, mem_prompt, state_mlstm_C, state_mlstm_n, state_mlstm_m, state_ret_S,
           cache_swa_k, cache_swa_v, cache_mem_k, cache_mem_v, norm_g, w_in, ml_gate_bias, ml_head_g,
           ret_head_g, swa_sinks, rel_bias, w_br_ml, w_br_swa, w_br_ret, w_out, w_cq, w_mkv, w_co,
           w_gu, w_down):
    depth = w_in.shape[0]
    bp, t, d = x_prompt.shape
    assert bp == 1 and d == D_MODEL and t % 1024 == 0
    nb, n_new, _ = x_sample.shape
    assert n_new == N_NEW
    R = SEQ_PAD

    xp = x_prompt.reshape(t, d)
    xs = jnp.pad(x_sample, ((0, 0), (0, R - n_new), (0, 0))).reshape(nb * R, d)
    mem = mem_prompt.reshape(N_MEM, d)
    half = HEAD_D // 2
    inv = ROPE_BASE ** (-jnp.arange(half, dtype=F32) / half)
    inv = jnp.concatenate([inv, inv]).reshape(1, LANES)
    rb = rel_bias.astype(F32).reshape(-1)
    mem_kt = jnp.transpose(cache_mem_k, (0, 1, 3, 4, 2)).reshape(depth, nb, X_HEADS * X_D, N_MEM)
    mem_vt = jnp.transpose(cache_mem_v, (0, 1, 3, 4, 2)).reshape(depth, nb, X_HEADS * X_D, N_MEM)

    outs_p = {k: [] for k in ("C", "n", "m", "S", "k", "v", "mk", "mv")}
    outs_s = {k: [] for k in ("n", "m", "k", "v")}
    sample_c = sample_s = None
    for l in range(depth):
        lw = dict(
            ng=[_row(norm_g[l, i]) for i in range(7)],
            mlg=_row(ml_head_g[l]), retg=_row(ret_head_g[l]),
            wml=w_br_ml[l].astype(BF16), wsw=w_br_swa[l].astype(BF16), wrt=w_br_ret[l].astype(BF16),
            wout=w_out[l].astype(BF16), wgu=w_gu[l].astype(BF16), wd=w_down[l].astype(BF16))
        win = _reorder_w_in(w_in[l])
        wcq = w_cq[l].astype(BF16)
        wco = w_co[l].astype(BF16)
        gb = jnp.concatenate([ml_gate_bias[l, 0], ml_gate_bias[l, 1],
                              jnp.zeros((LANES - 2 * N_HEADS,), F32)]).reshape(1, LANES).astype(F32)
        sinks = swa_sinks[l].astype(F32)

        kv = norm_matmul(mem, lw["ng"][6], w_mkv[l].astype(BF16), tm=N_MEM, tn=2 * X_HEADS * X_D, name="memory_kv")

        def seq_p(z):
            hml, osw, ort, c_, n_, m_, s_ = mixer_seq_prompt(z, gb, inv, sinks, rb)
            return hml, osw, ort, (c_, n_, m_, s_)

        xp, zp, (c_, n_, m_, s_) = _decoder_layer(
            xp,
            lambda x: norm_matmul(x, lw["ng"][0], win, tm=1024, tn=1024, name="in_proj_prompt"),
            seq_p,
            lambda x: xattn_prompt(x, lw["ng"][2], lw["ng"][3], wcq, kv, wco, tm=512),
            lw, tm=512, tag="prompt")
        outs_p["C"].append(c_.reshape(1, N_HEADS, HEAD_D, HEAD_D))
        outs_p["n"].append(n_.reshape(1, N_HEADS, HEAD_D))
        outs_p["m"].append(m_[:, :N_HEADS])
        outs_p["S"].append(s_.reshape(1, N_HEADS, HEAD_D, HEAD_D))
        misc_last = zp[t - CHUNK:, 512 * ZC_MISC:512 * ZC_MISC + 2 * LANES]
        outs_p["k"].append(misc_last[:, :LANES].reshape(1, CHUNK, 2, SWA_D))
        outs_p["v"].append(misc_last[:, LANES:].reshape(1, CHUNK, 2, SWA_D))
        outs_p["mk"].append(kv[:, :X_HEADS * X_D].reshape(1, N_MEM, X_HEADS, X_D))
        outs_p["mv"].append(kv[:, X_HEADS * X_D:].reshape(1, N_MEM, X_HEADS, X_D))

        n0 = state_mlstm_n[l].astype(F32).reshape(nb, 1, N_HEADS * HEAD_D)
        m0 = jnp.pad(state_mlstm_m[l].astype(F32), ((0, 0), (0, LANES - N_HEADS))).reshape(nb, 1, LANES)
        ck = cache_swa_k[l].reshape(nb, CHUNK, LANES)
        cv = cache_swa_v[l].reshape(nb, CHUNK, LANES)

        def seq_s(z):
            hml, osw, ort, c_, n_, m_, s_ = mixer_seq_sample(
                z, l, state_mlstm_C, n0, m0, state_ret_S, ck, cv, gb, inv, sinks, rb, sample_c, sample_s)
            return hml, osw, ort, (c_, n_, m_, s_)

        xs, zs, (c_, n_, m_, s_) = _decoder_layer(
            xs,
            lambda x: norm_matmul(x, lw["ng"][0], win, tm=min(1024, nb * R), tn=1024, name="in_proj_sample"),
            seq_s,
            lambda x: xattn_sample(x, lw["ng"][2], lw["ng"][3], wcq, l, mem_kt, mem_vt, wco),
            lw, tm=min(512, nb * R), tag="sample")
        sample_c, sample_s = c_, s_
        outs_s["n"].append(n_.reshape(nb, N_HEADS, HEAD_D))
        outs_s["m"].append(m_.reshape(nb, LANES)[:, :N_HEADS])
        misc_new = zs.reshape(nb, R, Z_COLS)[:, :n_new, 512 * ZC_MISC:512 * ZC_MISC + 2 * LANES]
        k_new = misc_new[:, :, :LANES].reshape(nb, n_new, 2, SWA_D)
        v_new = misc_new[:, :, LANES:].reshape(nb, n_new, 2, SWA_D)
        outs_s["k"].append(jnp.concatenate([cache_swa_k[l][:, n_new:], k_new], axis=1))
        outs_s["v"].append(jnp.concatenate([cache_swa_v[l][:, n_new:], v_new], axis=1))

    y_p = xp.reshape(1, t, d)
    y_s = xs.reshape(nb, R, d)[:, :n_new]
    st = lambda d_, k: jnp.stack(d_[k])
    return (y_p, y_s,
            st(outs_p, "C"), st(outs_p, "n"), st(outs_p, "m"), st(outs_p, "S"),
            st(outs_p, "k"), st(outs_p, "v"), st(outs_p, "mk"), st(outs_p, "mv"),
            sample_c, st(outs_s, "n"), st(outs_s, "m"), sample_s,
            st(outs_s, "k"), st(outs_s, "v"))
```

```python
import functools
import math

import numpy as np
import jax
import jax.numpy as jnp
from jax import lax
from jax.experimental import pallas as pl
from jax.experimental.pallas import tpu as pltpu

F32 = jnp.float32
BF16 = jnp.bfloat16

D_MODEL = 1024
EPS = 1e-6
PAST_LEN = 16384
HEAD_D = 128
N_HEADS = 4
CHUNK = 128
SWA_HEADS = 8
SWA_D = 64
N_BUCKETS = 32
MAX_DISTANCE = 128
ROPE_BASE = 10000.0
N_MEM = 256
X_HEADS = 4
X_D = 64
D_FF = 2816
SEQ_PAD = 8
N_NEW = 4
LANES = 128
MXU_MIN_ROWS = 16
VMEM_LIMIT = 48 * 1024 * 1024

ZC_ML_Q, ZC_ML_K, ZC_ML_V, ZC_ML_O, ZC_SW_Q, ZC_MISC, ZC_RT_Q, ZC_RT_K, ZC_RT_V, ZC_RT_G = range(10)
ZC_GATES = 5
Z_COLS = 8192
MISC_K, MISC_V, MISC_IF = 0, 128, 256

LOG2E = math.log2(math.e)
LN2 = math.log(2.0)
LOG_GAMMA = tuple(float(v) for v in np.log1p(-np.exp2(-5.0 - np.arange(N_HEADS, dtype=np.float32))))

SWA_SAME = (0, 2, 5, 7)
SWA_SWAP = (1, 3, 4, 6)


def _t5_bucket_bounds():
    n = np.arange(CHUNK)
    max_exact = N_BUCKETS // 2
    nf = np.maximum(n, 1).astype(np.float32)
    large = max_exact + (np.log(nf / np.float32(max_exact)) / np.float32(math.log(MAX_DISTANCE / max_exact))
                         * np.float32(N_BUCKETS - max_exact)).astype(np.int32)
    large = np.minimum(large, N_BUCKETS - 1)
    b = np.where(n < max_exact, n, large)
    assert np.all(np.diff(b) >= 0)
    runs = []
    for d in range(CHUNK):
        if runs and runs[-1][1] == int(b[d]):
            runs[-1][0] = d + 1
        else:
            runs.append([d + 1, int(b[d])])
    return tuple((hi, bk) for hi, bk in runs)


BUCKET_RUNS = _t5_bucket_bounds()


def _dot(a, b):
    return jnp.dot(a, b, preferred_element_type=F32)


def _dot_nt(a, b):
    return lax.dot_general(a, b, (((1,), (1,)), ((), ())), preferred_element_type=F32)


def _dot_tn(a, b):
    return lax.dot_general(a, b, (((0,), (0,)), ((), ())), preferred_element_type=F32)


def _rms(x, g):
    return x * lax.rsqrt(jnp.mean(x * x, axis=-1, keepdims=True) + EPS) * g


def _sigmoid(x):
    return 1.0 / (1.0 + jnp.exp(-x))


def _log_sigmoid(x):
    return jnp.minimum(x, 0.0) - jnp.log1p(jnp.exp(-jnp.abs(x)))


def _params(n_grid):
    return pltpu.CompilerParams(dimension_semantics=("arbitrary",) * n_grid, vmem_limit_bytes=VMEM_LIMIT)


def _norm_matmul_kernel(x_ref, g_ref, w_ref, o_ref, u_ref):
    @pl.when(pl.program_id(1) == 0)
    def _():
        u_ref[...] = _rms(x_ref[...], g_ref[...]).astype(BF16)

    o_ref[...] = _dot(u_ref[...], w_ref[...])


def norm_matmul(x, g, w, tm, tn, name):
    m, k = x.shape
    n = w.shape[1]
    return pl.pallas_call(
        _norm_matmul_kernel,
        grid=(m // tm, n // tn),
        in_specs=[pl.BlockSpec((tm, k), lambda i, j: (i, 0)),
                  pl.BlockSpec((1, k), lambda i, j: (0, 0)),
                  pl.BlockSpec((k, tn), lambda i, j: (0, j))],
        out_specs=pl.BlockSpec((tm, tn), lambda i, j: (i, j)),
        out_shape=jax.ShapeDtypeStruct((m, n), F32),
        scratch_shapes=[pltpu.VMEM((tm, k), BF16)],
        compiler_params=_params(2),
        name=name,
    )(x, g, w)


def _build_swa_table(tab_ref, rb_ref, lq):
    row = lax.broadcasted_iota(jnp.int32, (lq, 2 * CHUNK), 0)
    col = lax.broadcasted_iota(jnp.int32, (lq, 2 * CHUNK), 1)
    delta = row + CHUNK - col
    valid = (delta >= 0) & (delta < CHUNK)
    for var, heads in enumerate((SWA_SAME, SWA_SWAP)):
        for i, h in enumerate(heads):
            val = jnp.full((lq, 2 * CHUNK), rb_ref[BUCKET_RUNS[-1][1] * SWA_HEADS + h], F32)
            for hi, bk in reversed(BUCKET_RUNS[:-1]):
                val = jnp.where(delta < hi, rb_ref[bk * SWA_HEADS + h], val)
            tab_ref[var, i * lq:(i + 1) * lq, :] = jnp.where(valid, val, -jnp.inf)


def _swa_block(q, kbuf, vbuf, tab_ref, sinks_ref, prev_invalid):
    lq = q.shape[0]
    lane = lax.broadcasted_iota(jnp.int32, (1, LANES), 1)
    upper = lane >= SWA_D
    q = q * (SWA_D ** -0.5)
    k = kbuf[...]
    v = vbuf[...]
    ks = (k.astype(BF16), pltpu.roll(k, SWA_D, 1).astype(BF16))
    vs = (v.astype(BF16), pltpu.roll(v, SWA_D, 1).astype(BF16))
    outs = [None] * SWA_HEADS
    for var, heads in enumerate((SWA_SAME, SWA_SWAP)):
        qm = []
        sk = []
        for h in heads:
            blk = q[:, LANES * (h // 2):LANES * (h // 2 + 1)]
            keep = upper if h % 2 == 1 else jnp.logical_not(upper)
            qm.append(jnp.where(keep, blk, 0.0))
            sk.append(jnp.full((lq, 1), sinks_ref[h], F32))
        qs = jnp.concatenate(qm, axis=0).astype(BF16)
        sink = jnp.concatenate(sk, axis=0)
        s = _dot_nt(qs, ks[var]) + tab_ref[var]
        if prev_invalid is not None:
            col = lax.broadcasted_iota(jnp.int32, s.shape, 1)
            s = jnp.where(jnp.logical_and(prev_invalid, col < CHUNK), -jnp.inf, s)
        m = jnp.maximum(jnp.max(s, axis=1, keepdims=True), sink)
        p = jnp.exp(s - m)
        norm = 1.0 / (jnp.sum(p, axis=1, keepdims=True) + jnp.exp(sink - m))
        o = _dot(p.astype(BF16), vs[var]) * norm
        for i, h in enumerate(heads):
            outs[h] = o[i * lq:(i + 1) * lq]
    blocks = [jnp.where(upper, outs[2 * j + 1], outs[2 * j]) for j in range(SWA_HEADS // 2)]
    return jnp.concatenate(blocks, axis=1)


def _transpose_bf16(x, eye):
    return _dot_nt(eye, x).astype(BF16)


def _build_swa_table_t(tab_ref, rb_ref):
    L = CHUNK
    srow = lax.broadcasted_iota(jnp.int32, (2 * L, L), 0)
    lcol = lax.broadcasted_iota(jnp.int32, (2 * L, L), 1)
    delta = lcol + L - srow
    valid = (delta >= 0) & (delta < L)
    for var, heads in enumerate((SWA_SAME, SWA_SWAP)):
        for i, h in enumerate(heads):
            val = jnp.full((2 * L, L), rb_ref[BUCKET_RUNS[-1][1] * SWA_HEADS + h], F32)
            for hi, bk in reversed(BUCKET_RUNS[:-1]):
                val = jnp.where(delta < hi, rb_ref[bk * SWA_HEADS + h], val)
            tab_ref[var, :, i * L:(i + 1) * L] = jnp.where(valid, val, -jnp.inf)


def _swa_block_t(q, k_same, k_swap, vt_buf, tab_ref, sinks_ref, prev_invalid):
    L = CHUNK
    lane = lax.broadcasted_iota(jnp.int32, (1, LANES), 1)
    upper = lane >= SWA_D
    q = q * (SWA_D ** -0.5)
    vt = vt_buf[...]
    vts = (vt, jnp.concatenate([vt[SWA_D:], vt[:SWA_D]], axis=0))
    ks = (k_same[...], k_swap[...])
    outs = [None] * SWA_HEADS
    for var, heads in enumerate((SWA_SAME, SWA_SWAP)):
        qm = []
        sk = []
        for h in heads:
            blk = q[:, LANES * (h // 2):LANES * (h // 2 + 1)]
            keep = upper if h % 2 == 1 else jnp.logical_not(upper)
            qm.append(jnp.where(keep, blk, 0.0))
            sk.append(jnp.full((1, L), sinks_ref[h], F32))
        qs = jnp.concatenate(qm, axis=0).astype(BF16)
        sink = jnp.concatenate(sk, axis=1)
        s_t = _dot_nt(ks[var], qs) + tab_ref[var]
        s_prev = jnp.where(prev_invalid, -jnp.inf, s_t[:L])
        s_t = jnp.concatenate([s_prev, s_t[L:]], axis=0)
        m = jnp.maximum(jnp.max(s_t, axis=0, keepdims=True), sink)
        p = jnp.exp(s_t - m)
        norm = 1.0 / (jnp.sum(p, axis=0, keepdims=True) + jnp.exp(sink - m))
        o_t = _dot(vts[var], p.astype(BF16)) * norm
        for i, h in enumerate(heads):
            outs[h] = o_t[:, i * L:(i + 1) * L].T
    blocks = [jnp.where(upper, outs[2 * j + 1], outs[2 * j]) for j in range(SWA_HEADS // 2)]
    return jnp.concatenate(blocks, axis=1)


def _rotary_tables(pos, inv_ref):
    lane = lax.broadcasted_iota(jnp.int32, (1, LANES), 1)
    ang = pos * inv_ref[...]
    sin = jnp.sin(ang)
    return jnp.cos(ang), jnp.where(lane < HEAD_D // 2, -sin, sin)


def _rotate(x, cos, sin_signed):
    return x * cos + pltpu.roll(x, HEAD_D // 2, 1) * sin_signed


def _gate_slab(raw, gb_ref):
    lane = lax.broadcasted_iota(jnp.int32, (1, LANES), 1)
    x = raw + gb_ref[...]
    return jnp.where((lane >= N_HEADS) & (lane < 2 * N_HEADS), _log_sigmoid(x), x)


def _head_rms(h, gain):
    parts = []
    for i in range(N_HEADS):
        blk = h[:, HEAD_D * i:HEAD_D * (i + 1)]
        parts.append(blk * lax.rsqrt(jnp.mean(blk * blk, axis=-1, keepdims=True) + EPS))
    return jnp.concatenate(parts, axis=-1) * gain


def _post_math(hml, osw, ort, mlo, rtg, g0, g1, g2, x, mlg, retg, ng, wml, wsw, wrt, wout):
    hm = _head_rms(hml, mlg) * _sigmoid(mlo)
    y_ml = _dot(hm.astype(BF16), wml)
    y_sw = _dot(osw.astype(BF16), wsw)
    rt = _head_rms(ort, retg) * (rtg * _sigmoid(rtg))
    y_rt = _dot(rt.astype(BF16), wrt)
    merged = _sigmoid(g0) * y_ml + _sigmoid(g1) * y_sw + _sigmoid(g2) * y_rt
    return x + _rms(_dot(merged.astype(BF16), wout), ng)


def _mixer_seq_prompt_kernel(mq_ref, mk_ref, mv_ref, swq_ref, misc_ref, rq_ref, rk_ref, rv_ref,
                             mlo_ref, rtg_ref, g0_ref, g1_ref, g2_ref, x_ref,
                             gb_ref, inv_ref, sinks_ref, rb_ref,
                             mlg_ref, retg_ref, ng_ref, wml_ref, wsw_ref, wrt_ref, wout_ref,
                             x1_ref, c_ref, n_ref, m_ref, s_ref,
                             st_scr, k_same, k_swap, vt_buf, tab_t, cos_l, sin_l, dec_in_t, eye_ref,
                             hml_s, osw_s, ort_s):
    c = pl.program_id(0)
    active = c < pl.num_programs(0) - 1
    L = CHUNK
    lane = lax.broadcasted_iota(jnp.int32, (1, LANES), 1)
    row = lax.broadcasted_iota(jnp.int32, (L, L), 0)
    col = lax.broadcasted_iota(jnp.int32, (L, L), 1)
    causal_t = row <= col

    @pl.when(c == 0)
    def _init():
        c_ref[...] = jnp.zeros(c_ref.shape, F32)
        n_ref[...] = jnp.zeros(n_ref.shape, F32)
        m_ref[...] = jnp.zeros(m_ref.shape, F32)
        s_ref[...] = jnp.zeros(s_ref.shape, F32)
        hml_s[...] = jnp.zeros(hml_s.shape, F32)
        osw_s[...] = jnp.zeros(osw_s.shape, F32)
        ort_s[...] = jnp.zeros(ort_s.shape, F32)
        st_scr[...] = jnp.zeros(st_scr.shape, F32)
        k_same[...] = jnp.zeros(k_same.shape, BF16)
        k_swap[...] = jnp.zeros(k_swap.shape, BF16)
        vt_buf[...] = jnp.zeros(vt_buf.shape, BF16)
        _build_swa_table_t(tab_t, rb_ref)
        ang = row.astype(F32) * inv_ref[...]
        cos_l[...] = jnp.cos(ang)
        sin_l[...] = jnp.sin(ang)
        rel_t = (col - row).astype(F32)
        for h in range(N_HEADS):
            dec_in_t[h] = jnp.where(causal_t, jnp.exp(LOG_GAMMA[h] * rel_t), 0.0)
        eye_ref[...] = jnp.where(row == col, 1.0, 0.0).astype(BF16)

    eye = eye_ref[...]

    x1_ref[...] = _post_math(hml_s[...], osw_s[...], ort_s[...], mlo_ref[...], rtg_ref[...],
                             g0_ref[...], g1_ref[...], g2_ref[...], x_ref[...],
                             mlg_ref[...], retg_ref[...], ng_ref[...],
                             wml_ref[...], wsw_ref[...], wrt_ref[...], wout_ref[...])

    lf = _gate_slab(misc_ref[:, MISC_IF:MISC_IF + LANES], gb_ref)
    bcum = jnp.dot((row >= col).astype(F32), lf, precision=lax.Precision.HIGHEST, preferred_element_type=F32)
    colslab = jnp.where(lane < N_HEADS, lf, bcum)
    rowslab = colslab.T
    m_all = m_ref[...]
    m_out = jnp.zeros((1, LANES), F32)
    for h in range(N_HEADS):
        hs = slice(HEAD_D * h, HEAD_D * (h + 1))
        g_c = colslab[:, h:h + 1] - colslab[:, N_HEADS + h:N_HEADS + h + 1]
        i_r = rowslab[h:h + 1, :]
        b_r = rowslab[N_HEADS + h:N_HEADS + h + 1, :]
        m_prev = m_all[:, h:h + 1]
        gm = jnp.where(causal_t, g_c, -jnp.inf)
        mx = jnp.maximum(jnp.max(gm, axis=0, keepdims=True), m_prev)
        m_t = b_r + mx
        inter = jnp.exp(m_prev - mx)
        dm_t = jnp.exp(gm - mx)
        q = mq_ref[:, hs]
        k = mk_ref[:, hs] * (HEAD_D ** -0.5)
        qb = q.astype(BF16)
        kb = k.astype(BF16)
        vt = _transpose_bf16(mv_ref[:, hs].astype(BF16), eye)
        s_t = _dot_nt(kb, qb) * dm_t
        c_old = c_ref[h]
        n_old = n_ref[h]
        num_t = _dot(vt, s_t.astype(BF16)) + _dot_nt(c_old.astype(BF16), qb) * inter
        nq = _dot_nt(jnp.broadcast_to(n_old, (MXU_MIN_ROWS, HEAD_D)).astype(BF16), qb)[0:1, :]
        den = jnp.sum(s_t, axis=0, keepdims=True) + inter * nq
        den = jnp.maximum(jnp.abs(den), jnp.exp(-m_t))
        hml_s[:, hs] = (num_t * (1.0 / den)).T
        m_new = m_t[:, L - 1:L]
        b_last = b_r[:, L - 1:L]
        decay = jnp.exp(b_last + m_prev - m_new)
        w_r = jnp.exp((i_r - b_r) + (b_last - m_new))
        c_new = decay * c_old + _dot((vt.astype(F32) * w_r).astype(BF16), kb)
        n_new = decay * n_old + _dot(jnp.broadcast_to(w_r, (MXU_MIN_ROWS, L)).astype(BF16), kb)[0:1, :]
        c_ref[h] = jnp.where(active, c_new, c_old)
        n_ref[h] = jnp.where(active, n_new, n_old)
        m_out = jnp.where(lane == h, m_new, m_out)
    m_ref[...] = jnp.where(active, m_out, m_all)

    ang0 = (c * L).astype(F32) * inv_ref[...]
    cos0 = jnp.cos(ang0)
    sin0 = jnp.sin(ang0)
    cos_t = cos_l[...]
    sin_t = sin_l[...]
    cos = cos0 * cos_t - sin0 * sin_t
    sin = sin0 * cos_t + cos0 * sin_t
    sin_signed = jnp.where(lane < HEAD_D // 2, -sin, sin)
    l_row = lane.astype(F32)
    for h in range(N_HEADS):
        hs = slice(HEAD_D * h, HEAD_D * (h + 1))
        lg = LOG_GAMMA[h]
        qr = _rotate(rq_ref[:, hs], cos, sin_signed).astype(BF16)
        kr = (_rotate(rk_ref[:, hs], cos, sin_signed) * (HEAD_D ** -0.5)).astype(BF16)
        vt = _transpose_bf16(rv_ref[:, hs].astype(BF16), eye)
        s_t = _dot_nt(kr, qr) * dec_in_t[h]
        st_old = st_scr[h]
        o_t = _dot(vt, s_t.astype(BF16)) + _dot_nt(st_old.astype(BF16), qr) * jnp.exp(lg * (l_row + 1.0))
        ort_s[:, hs] = o_t.T
        k_decay = jnp.exp(lg * (L - 1.0 - l_row))
        st_new = math.exp(lg * L) * st_old + _dot((vt.astype(F32) * k_decay).astype(BF16), kr)
        st_scr[h] = jnp.where(active, st_new, st_old)

    @pl.when(c == pl.num_programs(0) - 1)
    def _emit_state():
        for h in range(N_HEADS):
            s_ref[h] = st_scr[h].T

    k_new = misc_ref[:, MISC_K:MISC_K + LANES]
    k_same[0:L, :] = k_same[L:2 * L, :]
    k_swap[0:L, :] = k_swap[L:2 * L, :]
    k_same[L:2 * L, :] = k_new.astype(BF16)
    k_swap[L:2 * L, :] = pltpu.roll(k_new, SWA_D, 1).astype(BF16)
    vt_buf[:, 0:L] = vt_buf[:, L:2 * L]
    vt_buf[:, L:2 * L] = _transpose_bf16(misc_ref[:, MISC_V:MISC_V + LANES].astype(BF16), eye)
    osw_s[...] = _swa_block_t(swq_ref[...], k_same, k_swap, vt_buf, tab_t, sinks_ref, c == 0)


def mixer_prompt(z, x, gb, inv, sinks, rb, mlg, retg, ng, wml, wsw, wrt, wout):
    t = z.shape[0]
    L = CHUNK
    n = t // L

    def zspec(cb):
        return pl.BlockSpec((L, 512), lambda c, cb=cb: (jnp.minimum(c, n - 1), cb))

    def tail(w, cb=0):
        return pl.BlockSpec((L, w), lambda c, cb=cb: (jnp.maximum(c - 1, 0), cb))

    def const(shape):
        return pl.BlockSpec(shape, lambda c: (0,) * len(shape))

    def weight(w):
        return pl.BlockSpec(w.shape, lambda c: (0, 0), pipeline_mode=pl.Buffered(1))

    smem = pl.BlockSpec(memory_space=pltpu.SMEM)
    return pl.pallas_call(
        _mixer_seq_prompt_kernel,
        grid=(n + 1,),
        in_specs=[zspec(ZC_ML_Q), zspec(ZC_ML_K), zspec(ZC_ML_V), zspec(ZC_SW_Q), zspec(ZC_MISC),
                  zspec(ZC_RT_Q), zspec(ZC_RT_K), zspec(ZC_RT_V),
                  tail(512, ZC_ML_O), tail(512, ZC_RT_G),
                  tail(1024, ZC_GATES), tail(1024, ZC_GATES + 1), tail(1024, ZC_GATES + 2), tail(D_MODEL),
                  const((1, LANES)), const((1, LANES)), smem, smem,
                  const((1, 512)), const((1, 512)), const((1, D_MODEL)),
                  weight(wml), weight(wsw), weight(wrt), weight(wout)],
        out_specs=[tail(D_MODEL),
                   const((N_HEADS, HEAD_D, HEAD_D)), const((N_HEADS, 1, HEAD_D)), const((1, LANES)),
                   const((N_HEADS, HEAD_D, HEAD_D))],
        out_shape=[jax.ShapeDtypeStruct((t, D_MODEL), F32),
                   jax.ShapeDtypeStruct((N_HEADS, HEAD_D, HEAD_D), F32),
                   jax.ShapeDtypeStruct((N_HEADS, 1, HEAD_D), F32),
                   jax.ShapeDtypeStruct((1, LANES), F32),
                   jax.ShapeDtypeStruct((N_HEADS, HEAD_D, HEAD_D), F32)],
        scratch_shapes=[pltpu.VMEM((N_HEADS, L, L), F32),
                        pltpu.VMEM((2 * L, LANES), BF16), pltpu.VMEM((2 * L, LANES), BF16),
                        pltpu.VMEM((LANES, 2 * L), BF16),
                        pltpu.VMEM((2, 2 * L, 4 * L), F32),
                        pltpu.VMEM((L, L), F32), pltpu.VMEM((L, L), F32),
                        pltpu.VMEM((N_HEADS, L, L), F32),
                        pltpu.VMEM((L, L), BF16),
                        pltpu.VMEM((L, 512), F32), pltpu.VMEM((L, 512), F32), pltpu.VMEM((L, 512), F32)],
        compiler_params=_params(1),
        name="mixer_prompt",
    )(z, z, z, z, z, z, z, z, z, z, z, z, z, x, gb, inv, sinks, rb, mlg, retg, ng, wml, wsw, wrt, wout)


def _shift_rows(x, d):
    return x if d == 0 else pltpu.roll(x, d, 0)


def _mixer_seq_sample_kernel(mq_ref, mk_ref, mv_ref, swq_ref, misc_ref, rq_ref, rk_ref, rv_ref,
                             c0_ref, n0_ref, m0_ref, s0_ref, ck_ref, cv_ref,
                             gb_ref, inv_ref, sinks_ref, rb_ref, *rest, group, first_layer):
    if not first_layer:
        rest = rest[2:]
    hml_ref, osw_ref, ort_ref, c_all, n_ref, m_ref, s_all, kbuf, vbuf, tab = rest
    R = SEQ_PAD
    if first_layer:
        c_ref, s_ref = c_all.at[0], s_all.at[0]
        c_all[1:] = jnp.zeros((c_all.shape[0] - 1,) + c_all.shape[1:], F32)
        s_all[1:] = jnp.zeros((s_all.shape[0] - 1,) + s_all.shape[1:], F32)
    else:
        c_ref, s_ref = c_all, s_all

    @pl.when(pl.program_id(0) == 0)
    def _init():
        kbuf[...] = jnp.zeros(kbuf.shape, F32)
        vbuf[...] = jnp.zeros(vbuf.shape, F32)
        _build_swa_table(tab, rb_ref, R)

    lane = lax.broadcasted_iota(jnp.int32, (1, LANES), 1)
    l_idx = lax.broadcasted_iota(jnp.int32, (R, 1), 0)
    real = l_idx < N_NEW
    l_f = l_idx.astype(F32)
    cos, sin_signed = _rotary_tables((PAST_LEN + l_idx).astype(F32), inv_ref)

    def body(g, carry):
        rows = pl.ds(pl.multiple_of(g * R, R), R)

        lf = _gate_slab(misc_ref[rows, MISC_IF:MISC_IF + LANES], gb_ref)
        bsum = lf
        for d in range(1, N_NEW):
            bsum = bsum + jnp.where(l_idx >= d, _shift_rows(lf, d), 0.0)
        slab = jnp.where(lane < N_HEADS, lf, bsum)
        shifted = [_shift_rows(slab, d) for d in range(N_NEW)]
        m0 = m0_ref[g]
        n0 = n0_ref[g]
        m_out = jnp.zeros((1, LANES), F32)
        for h in range(N_HEADS):
            hs = slice(HEAD_D * h, HEAD_D * (h + 1))
            i_c = slab[:, h:h + 1]
            b_c = slab[:, N_HEADS + h:N_HEADS + h + 1]
            m_prev = m0[:, h:h + 1]
            logd = []
            for d in range(N_NEW):
                sd = shifted[d]
                ld = b_c - sd[:, N_HEADS + h:N_HEADS + h + 1] + sd[:, h:h + 1]
                logd.append(jnp.where(l_idx >= d, ld, -jnp.inf))
            log_inter = b_c + m_prev
            m_t = log_inter
            for d in range(N_NEW):
                m_t = jnp.maximum(m_t, logd[d])
            inter = jnp.exp(log_inter - m_t)
            q = mq_ref[rows, hs]
            k = mk_ref[rows, hs] * (HEAD_D ** -0.5)
            v = mv_ref[rows, hs]
            c_old = c0_ref[g, h]
            n_old = n0[:, hs]
            num = _dot_nt(q.astype(BF16), c_old.astype(BF16)) * inter
            den = inter * jnp.sum(q * n_old, axis=1, keepdims=True)
            for d in range(N_NEW):
                s_d = jnp.sum(q * _shift_rows(k, d), axis=1, keepdims=True) * jnp.exp(logd[d] - m_t)
                num = num + s_d * _shift_rows(v, d)
                den = den + s_d
            den = jnp.maximum(jnp.abs(den), jnp.exp(-m_t))
            hml_ref[rows, hs] = num / den
            m_new = m_t[N_NEW - 1:N_NEW, :]
            b_last = b_c[N_NEW - 1:N_NEW, :]
            decay = jnp.exp(b_last + m_prev - m_new)
            w = jnp.where(real, jnp.exp(b_last - b_c + i_c - m_new), 0.0)
            c_ref[g, h] = decay * c_old + _dot_tn((v * w).astype(BF16), k.astype(BF16))
            n_ref[g, :, hs] = decay * n_old + jnp.sum(k * w, axis=0, keepdims=True)
            m_out = jnp.where(lane == h, m_new, m_out)
        m_ref[g] = m_out

        for h in range(N_HEADS):
            hs = slice(HEAD_D * h, HEAD_D * (h + 1))
            lg = LOG_GAMMA[h]
            qr = _rotate(rq_ref[rows, hs], cos, sin_signed)
            kr = _rotate(rk_ref[rows, hs], cos, sin_signed) * (HEAD_D ** -0.5)
            v = rv_ref[rows, hs]
            s_old = s0_ref[g, h]
            o = _dot(qr.astype(BF16), s_old.astype(BF16)) * jnp.exp(lg * (l_f + 1.0))
            for d in range(N_NEW):
                s_d = jnp.sum(qr * _shift_rows(kr, d), axis=1, keepdims=True) * math.exp(lg * d)
                o = o + jnp.where(l_idx >= d, s_d, 0.0) * _shift_rows(v, d)
            ort_ref[rows, hs] = o
            k_decay = jnp.where(real, jnp.exp(lg * (N_NEW - 1.0 - l_f)), 0.0)
            s_ref[g, h] = math.exp(lg * N_NEW) * s_old + _dot_tn((kr * k_decay).astype(BF16), v.astype(BF16))

        kbuf[0:CHUNK, :] = ck_ref[g]
        vbuf[0:CHUNK, :] = cv_ref[g]
        kbuf[CHUNK:CHUNK + R, :] = misc_ref[rows, MISC_K:MISC_K + LANES]
        vbuf[CHUNK:CHUNK + R, :] = misc_ref[rows, MISC_V:MISC_V + LANES]
        osw_ref[rows, :] = _swa_block(swq_ref[rows, :], kbuf, vbuf, tab, sinks_ref, None)
        return carry

    lax.fori_loop(0, group, body, 0, unroll=SAMPLE_UNROLL)


SAMPLE_UNROLL = 2


def mixer_seq_sample(z, layer, c0_all, n0, m0, s0_all, ck, cv, gb, inv, sinks, rb, c_prev, s_prev, group=8):
    depth, nb = c0_all.shape[:2]
    R = SEQ_PAD
    first_layer = c_prev is None
    assert first_layer == (layer == 0)

    def zspec(cb):
        return pl.BlockSpec((group * R, 512), lambda i, cb=cb: (i, cb))

    def const(shape):
        return pl.BlockSpec(shape, lambda i: (0,) * len(shape))

    def per_seq(shape):
        return pl.BlockSpec((group,) + shape, lambda i: (i,) + (0,) * len(shape))

    smem = pl.BlockSpec(memory_space=pltpu.SMEM)
    tok = pl.BlockSpec((group * R, 512), lambda i: (i, 0))
    st = (N_HEADS, HEAD_D, HEAD_D)
    st_in = pl.BlockSpec((None, group) + st, lambda i: (layer, i, 0, 0, 0))
    if first_layer:
        st_out = pl.BlockSpec((depth, group) + st, lambda i: (0, i, 0, 0, 0))
        extra_specs, extra_args, aliases = [], [], {}
    else:
        st_out = st_in
        extra_specs = [pl.BlockSpec(memory_space=pl.ANY)] * 2
        extra_args = [c_prev, s_prev]
        aliases = {18: 3, 19: 6}
    return pl.pallas_call(
        functools.partial(_mixer_seq_sample_kernel, group=group, first_layer=first_layer),
        grid=(nb // group,),
        in_specs=[zspec(ZC_ML_Q), zspec(ZC_ML_K), zspec(ZC_ML_V), zspec(ZC_SW_Q), zspec(ZC_MISC),
                  zspec(ZC_RT_Q), zspec(ZC_RT_K), zspec(ZC_RT_V),
                  st_in, per_seq((1, 512)), per_seq((1, LANES)), st_in,
                  per_seq((CHUNK, LANES)), per_seq((CHUNK, LANES)),
                  const((1, LANES)), const((1, LANES)), smem, smem] + extra_specs,
        out_specs=[tok, tok, tok, st_out, per_seq((1, 512)), per_seq((1, LANES)), st_out],
        out_shape=[jax.ShapeDtypeStruct((nb * R, 512), F32)] * 3
        + [jax.ShapeDtypeStruct((depth, nb) + st, F32),
           jax.ShapeDtypeStruct((nb, 1, 512), F32),
           jax.ShapeDtypeStruct((nb, 1, LANES), F32),
           jax.ShapeDtypeStruct((depth, nb) + st, F32)],
        scratch_shapes=[pltpu.VMEM((2 * CHUNK, LANES), F32), pltpu.VMEM((2 * CHUNK, LANES), F32),
                        pltpu.VMEM((2, 4 * R, 2 * CHUNK), F32)],
        input_output_aliases=aliases,
        compiler_params=_params(1),
        name="mixer_seq_sample",
    )(z, z, z, z, z, z, z, z, c0_all, n0, m0, s0_all, ck, cv, gb, inv, sinks, rb, *extra_args)


def _mixer_post_kernel(hml_ref, osw_ref, ort_ref, mlo_ref, rtg_ref, g0_ref, g1_ref, g2_ref, x_ref,
                       mlg_ref, retg_ref, ng_ref, wml_ref, wsw_ref, wrt_ref, wout_ref, o_ref):
    o_ref[...] = _post_math(hml_ref[...], osw_ref[...], ort_ref[...], mlo_ref[...], rtg_ref[...],
                            g0_ref[...], g1_ref[...], g2_ref[...], x_ref[...],
                            mlg_ref[...], retg_ref[...], ng_ref[...],
                            wml_ref[...], wsw_ref[...], wrt_ref[...], wout_ref[...])


def mixer_post(hml, osw, ort, z, x, mlg, retg, ng, wml, wsw, wrt, wout, tm, name):
    m = x.shape[0]

    def tok(w, cb=0):
        return pl.BlockSpec((tm, w), lambda i, cb=cb: (i, cb))

    def const(shape):
        return pl.BlockSpec(shape, lambda i: (0,) * len(shape))

    return pl.pallas_call(
        _mixer_post_kernel,
        grid=(m // tm,),
        in_specs=[tok(512), tok(512), tok(512), tok(512, ZC_ML_O), tok(512, ZC_RT_G),
                  tok(1024, ZC_GATES), tok(1024, ZC_GATES + 1), tok(1024, ZC_GATES + 2), tok(D_MODEL),
                  const((1, 512)), const((1, 512)), const((1, D_MODEL)),
                  const((512, D_MODEL)), const((512, D_MODEL)), const((512, D_MODEL)),
                  const((D_MODEL, D_MODEL))],
        out_specs=tok(D_MODEL),
        out_shape=jax.ShapeDtypeStruct((m, D_MODEL), F32),
        compiler_params=_params(1),
        name=name,
    )(hml, osw, ort, z, z, z, z, z, x, mlg, retg, ng, wml, wsw, wrt, wout)


def _xattn_heads(q, k, v):
    lane = lax.broadcasted_iota(jnp.int32, (1, X_HEADS * X_D), 1)
    acc = jnp.zeros(q.shape, F32)
    for h in range(X_HEADS):
        mask = (lane >= X_D * h) & (lane < X_D * (h + 1))
        s = _dot_nt(jnp.where(mask, q, 0.0).astype(BF16), k) * (X_D ** -0.5)
        m = jnp.max(s, axis=1, keepdims=True)
        p = jnp.exp(s - m)
        p = p / jnp.sum(p, axis=1, keepdims=True)
        acc = acc + jnp.where(mask, _dot(p.astype(BF16), v), 0.0)
    return acc


def _xattn_prompt_kernel(x_ref, gin_ref, gout_ref, wcq_ref, kv_ref, wco_ref, o_ref):
    x = x_ref[...]
    q = _dot(_rms(x, gin_ref[...]).astype(BF16), wcq_ref[...])
    k = kv_ref[:, 0:X_HEADS * X_D].astype(BF16)
    v = kv_ref[:, X_HEADS * X_D:2 * X_HEADS * X_D].astype(BF16)
    o = _xattn_heads(q, k, v)
    o_ref[...] = x + _rms(_dot(o.astype(BF16), wco_ref[...]), gout_ref[...])


def xattn_prompt(x, gin, gout, wcq, kv, wco, tm):
    m = x.shape[0]

    def const(shape):
        return pl.BlockSpec(shape, lambda i: (0,) * len(shape))

    tok = pl.BlockSpec((tm, D_MODEL), lambda i: (i, 0))
    return pl.pallas_call(
        _xattn_prompt_kernel,
        grid=(m // tm,),
        in_specs=[tok, const((1, D_MODEL)), const((1, D_MODEL)), const(wcq.shape), const(kv.shape),
                  const(wco.shape)],
        out_specs=tok,
        out_shape=jax.ShapeDtypeStruct((m, D_MODEL), F32),
        compiler_params=_params(1),
        name="xattn_prompt",
    )(x, gin, gout, wcq, kv, wco)


def _xattn_sample_kernel(x_ref, gin_ref, gout_ref, wcq_ref, kt_ref, vt_ref, wco_ref, o_ref, q_scr, a_scr,
                         *, group):
    R = SEQ_PAD
    x = x_ref[...]
    q_scr[...] = _dot(_rms(x, gin_ref[...]).astype(BF16), wcq_ref[...])
    lane = lax.broadcasted_iota(jnp.int32, (1, X_HEADS * X_D), 1)
    masks = [(lane >= X_D * h) & (lane < X_D * (h + 1)) for h in range(X_HEADS)]

    def body(g, carry):
        rows = pl.ds(pl.multiple_of(g * R, R), R)
        q = q_scr[rows, :]
        qs = jnp.concatenate([jnp.where(mk, q, 0.0) for mk in masks], axis=0)
        s = _dot(qs.astype(BF16), kt_ref[g].astype(BF16)) * (X_D ** -0.5)
        m = jnp.max(s, axis=1, keepdims=True)
        p = jnp.exp(s - m)
        p = p / jnp.sum(p, axis=1, keepdims=True)
        o = _dot_nt(p.astype(BF16), vt_ref[g].astype(BF16))
        acc = jnp.zeros((R, X_HEADS * X_D), F32)
        for h in range(X_HEADS):
            acc = acc + jnp.where(masks[h], o[R * h:R * (h + 1)], 0.0)
        a_scr[rows, :] = acc
        return carry

    lax.fori_loop(0, group, body, 0)
    o_ref[...] = x + _rms(_dot(a_scr[...].astype(BF16), wco_ref[...]), gout_ref[...])


def xattn_sample(x, gin, gout, wcq, layer, mkt, mvt, wco, group=16):
    nb = mkt.shape[1]
    R = SEQ_PAD

    def const(shape):
        return pl.BlockSpec(shape, lambda i: (0,) * len(shape))

    tok = pl.BlockSpec((group * R, D_MODEL), lambda i: (i, 0))
    mem = pl.BlockSpec((None, group, X_HEADS * X_D, N_MEM), lambda i: (layer, i, 0, 0))
    return pl.pallas_call(
        functools.partial(_xattn_sample_kernel, group=group),
        grid=(nb // group,),
        in_specs=[tok, const((1, D_MODEL)), const((1, D_MODEL)), const(wcq.shape), mem, mem, const(wco.shape)],
        out_specs=tok,
        out_shape=jax.ShapeDtypeStruct((nb * R, D_MODEL), F32),
        scratch_shapes=[pltpu.VMEM((group * R, X_HEADS * X_D), F32),
                        pltpu.VMEM((group * R, X_HEADS * X_D), F32)],
        compiler_params=_params(1),
        name="xattn_sample",
    )(x, gin, gout, wcq, mkt, mvt, wco)


FFN_SPLIT = 2


def _ffn_kernel(x_ref, gin_ref, gout_ref, wgu_ref, wd_ref, o_ref):
    x = x_ref[...]
    u = _rms(x, gin_ref[...]).astype(BF16)
    fc = D_FF // FFN_SPLIT
    acc = None
    for c in range(FFN_SPLIT):
        g = _dot(u, wgu_ref[:, fc * c:fc * (c + 1)])
        up = _dot(u, wgu_ref[:, D_FF + fc * c:D_FF + fc * (c + 1)])
        h = (g * _sigmoid(g) * up).astype(BF16)
        part = _dot(h, wd_ref[fc * c:fc * (c + 1), :])
        acc = part if acc is None else acc + part
    o_ref[...] = x + _rms(acc, gout_ref[...])


def ffn(x, gin, gout, wgu, wd, tm, name):
    m = x.shape[0]

    def const(shape):
        return pl.BlockSpec(shape, lambda i: (0,) * len(shape))

    tok = pl.BlockSpec((tm, D_MODEL), lambda i: (i, 0))
    return pl.pallas_call(
        _ffn_kernel,
        grid=(m // tm,),
        in_specs=[tok, const((1, D_MODEL)), const((1, D_MODEL)),
                  pl.BlockSpec(wgu.shape, lambda i: (0, 0), pipeline_mode=pl.Buffered(1)),
                  pl.BlockSpec(wd.shape, lambda i: (0, 0), pipeline_mode=pl.Buffered(1))],
        out_specs=tok,
        out_shape=jax.ShapeDtypeStruct((m, D_MODEL), F32),
        compiler_params=_params(1),
        name=name,
    )(x, gin, gout, wgu, wd)


def _reorder_w_in(w):
    sizes = (512, 512, 512, 4, 4, 512, 512, 128, 128, 512, 512, 512, 512, 3072)
    offs = np.concatenate([[0], np.cumsum(sizes)])
    (ml_q, ml_k, ml_v, ml_i, ml_f, ml_o, sw_q, sw_k, sw_v, rt_q, rt_k, rt_v, rt_g, gates) = [
        w[:, int(offs[i]):int(offs[i + 1])] for i in range(len(sizes))]
    pad = jnp.zeros((w.shape[0], 512 - 128 - 128 - 8), w.dtype)
    out = jnp.concatenate([ml_q, ml_k, ml_v, ml_o, sw_q, sw_k, sw_v, ml_i, ml_f, pad,
                           rt_q, rt_k, rt_v, rt_g, gates], axis=1)
    assert out.shape[1] == Z_COLS
    return out.astype(BF16)


def _row(v):
    return v.reshape(1, -1).astype(F32)


def _decoder_layer(x, z_fn, mixer_fn, xattn_fn, lw, tm, tag):
    z = z_fn(x)
    x, states = mixer_fn(z, x)
    x = xattn_fn(x)
    x = ffn(x, lw["ng"][4], lw["ng"][5], lw["wgu"], lw["wd"], tm=tm, name="ffn_" + tag)
    return x, z, states


def kernel(x_prompt, x_sample, mem_prompt, state_mlstm_C, state_mlstm_n, state_mlstm_m, state_ret_S,
           cache_swa_k, cache_swa_v, cache_mem_k, cache_mem_v, norm_g, w_in, ml_gate_bias, ml_head_g,
           ret_head_g, swa_sinks, rel_bias, w_br_ml, w_br_swa, w_br_ret, w_out, w_cq, w_mkv, w_co,
           w_gu, w_down):
    depth = w_in.shape[0]
    bp, t, d = x_prompt.shape
    assert bp == 1 and d == D_MODEL and t % 1024 == 0
    nb, n_new, _ = x_sample.shape
    assert n_new == N_NEW
    R = SEQ_PAD

    xp = x_prompt.reshape(t, d)
    xs = jnp.pad(x_sample, ((0, 0), (0, R - n_new), (0, 0))).reshape(nb * R, d)
    mem = mem_prompt.reshape(N_MEM, d)
    half = HEAD_D // 2
    inv = ROPE_BASE ** (-jnp.arange(half, dtype=F32) / half)
    inv = jnp.concatenate([inv, inv]).reshape(1, LANES)
    rb = rel_bias.astype(F32).reshape(-1)
    mem_kt = jnp.transpose(cache_mem_k, (0, 1, 3, 4, 2)).reshape(depth, nb, X_HEADS * X_D, N_MEM)
    mem_vt = jnp.transpose(cache_mem_v, (0, 1, 3, 4, 2)).reshape(depth, nb, X_HEADS * X_D, N_MEM)

    outs_p = {k: [] for k in ("C", "n", "m", "S", "k", "v", "mk", "mv")}
    outs_s = {k: [] for k in ("n", "m", "k", "v")}
    sample_c = sample_s = None
    for l in range(depth):
        lw = dict(
            ng=[_row(norm_g[l, i]) for i in range(7)],
            mlg=_row(ml_head_g[l]), retg=_row(ret_head_g[l]),
            wml=w_br_ml[l].astype(BF16), wsw=w_br_swa[l].astype(BF16), wrt=w_br_ret[l].astype(BF16),
            wout=w_out[l].astype(BF16), wgu=w_gu[l].astype(BF16), wd=w_down[l].astype(BF16))
        win = _reorder_w_in(w_in[l])
        wcq = w_cq[l].astype(BF16)
        wco = w_co[l].astype(BF16)
        gb = jnp.concatenate([ml_gate_bias[l, 0], ml_gate_bias[l, 1],
                              jnp.zeros((LANES - 2 * N_HEADS,), F32)]).reshape(1, LANES).astype(F32)
        sinks = swa_sinks[l].astype(F32)

        kv = norm_matmul(mem, lw["ng"][6], w_mkv[l].astype(BF16), tm=N_MEM, tn=2 * X_HEADS * X_D, name="memory_kv")

        def seq_p(z, x):
            x1, c_, n_, m_, s_ = mixer_prompt(z, x, gb, inv, sinks, rb, lw["mlg"], lw["retg"], lw["ng"][1],
                                              lw["wml"], lw["wsw"], lw["wrt"], lw["wout"])
            return x1, (c_, n_, m_, s_)

        xp, zp, (c_, n_, m_, s_) = _decoder_layer(
            xp,
            lambda x: norm_matmul(x, lw["ng"][0], win, tm=1024, tn=1024, name="in_proj_prompt"),
            seq_p,
            lambda x: xattn_prompt(x, lw["ng"][2], lw["ng"][3], wcq, kv, wco, tm=512),
            lw, tm=512, tag="prompt")
        outs_p["C"].append(c_.reshape(1, N_HEADS, HEAD_D, HEAD_D))
        outs_p["n"].append(n_.reshape(1, N_HEADS, HEAD_D))
        outs_p["m"].append(m_[:, :N_HEADS])
        outs_p["S"].append(s_.reshape(1, N_HEADS, HEAD_D, HEAD_D))
        misc_last = zp[t - CHUNK:, 512 * ZC_MISC:512 * ZC_MISC + 2 * LANES]
        outs_p["k"].append(misc_last[:, :LANES].reshape(1, CHUNK, 2, SWA_D))
        outs_p["v"].append(misc_last[:, LANES:].reshape(1, CHUNK, 2, SWA_D))
        outs_p["mk"].append(kv[:, :X_HEADS * X_D].reshape(1, N_MEM, X_HEADS, X_D))
        outs_p["mv"].append(kv[:, X_HEADS * X_D:].reshape(1, N_MEM, X_HEADS, X_D))

        n0 = state_mlstm_n[l].astype(F32).reshape(nb, 1, N_HEADS * HEAD_D)
        m0 = jnp.pad(state_mlstm_m[l].astype(F32), ((0, 0), (0, LANES - N_HEADS))).reshape(nb, 1, LANES)
        ck = cache_swa_k[l].reshape(nb, CHUNK, LANES)
        cv = cache_swa_v[l].reshape(nb, CHUNK, LANES)

        def seq_s(z, x):
            hml, osw, ort, c_, n_, m_, s_ = mixer_seq_sample(
                z, l, state_mlstm_C, n0, m0, state_ret_S, ck, cv, gb, inv, sinks, rb, sample_c, sample_s)
            x1 = mixer_post(hml, osw, ort, z, x, lw["mlg"], lw["retg"], lw["ng"][1], lw["wml"], lw["wsw"],
                            lw["wrt"], lw["wout"], tm=min(256, x.shape[0]), name="mixer_post_sample")
            return x1, (c_, n_, m_, s_)

        xs, zs, (c_, n_, m_, s_) = _decoder_layer(
            xs,
            lambda x: norm_matmul(x, lw["ng"][0], win, tm=min(1024, nb * R), tn=1024, name="in_proj_sample"),
            seq_s,
            lambda x: xattn_sample(x, lw["ng"][2], lw["ng"][3], wcq, l, mem_kt, mem_vt, wco),
            lw, tm=min(512, nb * R), tag="sample")
        sample_c, sample_s = c_, s_
        outs_s["n"].append(n_.reshape(nb, N_HEADS, HEAD_D))
        outs_s["m"].append(m_.reshape(nb, LANES)[:, :N_HEADS])
        misc_new = zs.reshape(nb, R, Z_COLS)[:, :n_new, 512 * ZC_MISC:512 * ZC_MISC + 2 * LANES]
        k_new = misc_new[:, :, :LANES].reshape(nb, n_new, 2, SWA_D)
        v_new = misc_new[:, :, LANES:].reshape(nb, n_new, 2, SWA_D)
        outs_s["k"].append(jnp.concatenate([cache_swa_k[l][:, n_new:], k_new], axis=1))
        outs_s["v"].append(jnp.concatenate([cache_swa_v[l][:, n_new:], v_new], axis=1))

    y_p = xp.reshape(1, t, d)
    y_s = xs.reshape(nb, R, d)[:, :n_new]
    st = lambda d_, k: jnp.stack(d_[k])
    return (y_p, y_s,
            st(outs_p, "C"), st(outs_p, "n"), st(outs_p, "m"), st(outs_p, "S"),
            st(outs_p, "k"), st(outs_p, "v"), st(outs_p, "mk"), st(outs_p, "mv"),
            sample_c, st(outs_s, "n"), st(outs_s, "m"), sample_s,
            st(outs_s, "k"), st(outs_s, "v"))
```

```python
import functools
import math

import numpy as np
import jax
import jax.numpy as jnp
from jax import lax
from jax.experimental import pallas as pl
from jax.experimental.pallas import tpu as pltpu

F32 = jnp.float32
BF16 = jnp.bfloat16

D_MODEL = 1024
EPS = 1e-6
PAST_LEN = 16384
HEAD_D = 128
N_HEADS = 4
CHUNK = 128
SWA_HEADS = 8
SWA_D = 64
N_BUCKETS = 32
MAX_DISTANCE = 128
ROPE_BASE = 10000.0
N_MEM = 256
X_HEADS = 4
X_D = 64
D_FF = 2816
SEQ_PAD = 8
N_NEW = 4
LANES = 128
IN_PROJ_TM = 2048
MXU_MIN_ROWS = 16
VMEM_LIMIT = 48 * 1024 * 1024

ZC_ML_Q, ZC_ML_K, ZC_ML_V, ZC_ML_O, ZC_SW_Q, ZC_MISC, ZC_RT_Q, ZC_RT_K, ZC_RT_V, ZC_RT_G = range(10)
ZC_GATES = 5
Z_COLS = 8192
MISC_K, MISC_V, MISC_IF = 0, 128, 256

LOG2E = math.log2(math.e)
LN2 = math.log(2.0)
LOG_GAMMA = tuple(float(v) for v in np.log1p(-np.exp2(-5.0 - np.arange(N_HEADS, dtype=np.float32))))

SWA_SAME = (0, 2, 5, 7)
SWA_SWAP = (1, 3, 4, 6)


def _t5_bucket_bounds():
    n = np.arange(CHUNK)
    max_exact = N_BUCKETS // 2
    nf = np.maximum(n, 1).astype(np.float32)
    large = max_exact + (np.log(nf / np.float32(max_exact)) / np.float32(math.log(MAX_DISTANCE / max_exact))
                         * np.float32(N_BUCKETS - max_exact)).astype(np.int32)
    large = np.minimum(large, N_BUCKETS - 1)
    b = np.where(n < max_exact, n, large)
    assert np.all(np.diff(b) >= 0)
    runs = []
    for d in range(CHUNK):
        if runs and runs[-1][1] == int(b[d]):
            runs[-1][0] = d + 1
        else:
            runs.append([d + 1, int(b[d])])
    return tuple((hi, bk) for hi, bk in runs)


BUCKET_RUNS = _t5_bucket_bounds()


def _dot(a, b):
    return jnp.dot(a, b, preferred_element_type=F32)


def _dot_nt(a, b):
    return lax.dot_general(a, b, (((1,), (1,)), ((), ())), preferred_element_type=F32)


def _dot_tn(a, b):
    return lax.dot_general(a, b, (((0,), (0,)), ((), ())), preferred_element_type=F32)


def _rms(x, g):
    return x * lax.rsqrt(jnp.mean(x * x, axis=-1, keepdims=True) + EPS) * g


def _sigmoid(x):
    return 1.0 / (1.0 + jnp.exp2(x * (-LOG2E)))


def _log_sigmoid(x):
    return jnp.minimum(x, 0.0) - jnp.log1p(jnp.exp(-jnp.abs(x)))


def _params(n_grid):
    return pltpu.CompilerParams(dimension_semantics=("arbitrary",) * n_grid, vmem_limit_bytes=VMEM_LIMIT)


def _norm_matmul_kernel(x_ref, g_ref, w_ref, o_ref, u_ref):
    @pl.when(pl.program_id(1) == 0)
    def _():
        u_ref[...] = _rms(x_ref[...], g_ref[...]).astype(BF16)

    o_ref[...] = _dot(u_ref[...], w_ref[...])


def norm_matmul(x, g, w, tm, tn, name):
    m, k = x.shape
    n = w.shape[1]
    return pl.pallas_call(
        _norm_matmul_kernel,
        grid=(m // tm, n // tn),
        in_specs=[pl.BlockSpec((tm, k), lambda i, j: (i, 0)),
                  pl.BlockSpec((1, k), lambda i, j: (0, 0)),
                  pl.BlockSpec((k, tn), lambda i, j: (0, j))],
        out_specs=pl.BlockSpec((tm, tn), lambda i, j: (i, j)),
        out_shape=jax.ShapeDtypeStruct((m, n), F32),
        scratch_shapes=[pltpu.VMEM((tm, k), BF16)],
        compiler_params=_params(2),
        name=name,
    )(x, g, w)


def _build_swa_table(tab_ref, rb_ref, lq):
    row = lax.broadcasted_iota(jnp.int32, (lq, 2 * CHUNK), 0)
    col = lax.broadcasted_iota(jnp.int32, (lq, 2 * CHUNK), 1)
    delta = row + CHUNK - col
    valid = (delta >= 0) & (delta < CHUNK)
    for var, heads in enumerate((SWA_SAME, SWA_SWAP)):
        for i, h in enumerate(heads):
            val = jnp.full((lq, 2 * CHUNK), rb_ref[BUCKET_RUNS[-1][1] * SWA_HEADS + h], F32)
            for hi, bk in reversed(BUCKET_RUNS[:-1]):
                val = jnp.where(delta < hi, rb_ref[bk * SWA_HEADS + h], val)
            tab_ref[var, i * lq:(i + 1) * lq, :] = jnp.where(valid, val, -jnp.inf)


def _swa_block(q, kbuf, vbuf, tab_ref, sinks_ref, prev_invalid):
    lq = q.shape[0]
    lane = lax.broadcasted_iota(jnp.int32, (1, LANES), 1)
    upper = lane >= SWA_D
    q = q * (SWA_D ** -0.5)
    k = kbuf[...]
    v = vbuf[...]
    ks = (k.astype(BF16), pltpu.roll(k, SWA_D, 1).astype(BF16))
    vs = (v.astype(BF16), pltpu.roll(v, SWA_D, 1).astype(BF16))
    outs = [None] * SWA_HEADS
    for var, heads in enumerate((SWA_SAME, SWA_SWAP)):
        qm = []
        sk = []
        for h in heads:
            blk = q[:, LANES * (h // 2):LANES * (h // 2 + 1)]
            keep = upper if h % 2 == 1 else jnp.logical_not(upper)
            qm.append(jnp.where(keep, blk, 0.0))
            sk.append(jnp.full((lq, 1), sinks_ref[h], F32))
        qs = jnp.concatenate(qm, axis=0).astype(BF16)
        sink = jnp.concatenate(sk, axis=0)
        s = _dot_nt(qs, ks[var]) + tab_ref[var]
        if prev_invalid is not None:
            col = lax.broadcasted_iota(jnp.int32, s.shape, 1)
            s = jnp.where(jnp.logical_and(prev_invalid, col < CHUNK), -jnp.inf, s)
        m = jnp.maximum(jnp.max(s, axis=1, keepdims=True), sink)
        p = jnp.exp(s - m)
        norm = 1.0 / (jnp.sum(p, axis=1, keepdims=True) + jnp.exp(sink - m))
        o = _dot(p.astype(BF16), vs[var]) * norm
        for i, h in enumerate(heads):
            outs[h] = o[i * lq:(i + 1) * lq]
    blocks = [jnp.where(upper, outs[2 * j + 1], outs[2 * j]) for j in range(SWA_HEADS // 2)]
    return jnp.concatenate(blocks, axis=1)


def _transpose_bf16(x, eye):
    return _dot_nt(eye, x).astype(BF16)


def _build_swa_table_t(tab_ref, rb_ref):
    L = CHUNK
    srow = lax.broadcasted_iota(jnp.int32, (2 * L, L), 0)
    lcol = lax.broadcasted_iota(jnp.int32, (2 * L, L), 1)
    delta = lcol + L - srow
    valid = (delta >= 0) & (delta < L)
    for var, heads in enumerate((SWA_SAME, SWA_SWAP)):
        for i, h in enumerate(heads):
            val = jnp.full((2 * L, L), rb_ref[BUCKET_RUNS[-1][1] * SWA_HEADS + h], F32)
            for hi, bk in reversed(BUCKET_RUNS[:-1]):
                val = jnp.where(delta < hi, rb_ref[bk * SWA_HEADS + h], val)
            tab_ref[var, :, i * L:(i + 1) * L] = jnp.where(valid, val, -jnp.inf)


def _swa_block_t(q, k_same, k_swap, vt_buf, tab_ref, sinks_ref, prev_invalid):
    L = CHUNK
    lane = lax.broadcasted_iota(jnp.int32, (1, LANES), 1)
    upper = lane >= SWA_D
    q = q * (SWA_D ** -0.5)
    vt = vt_buf[...]
    vts = (vt, jnp.concatenate([vt[SWA_D:], vt[:SWA_D]], axis=0))
    ks = (k_same[...], k_swap[...])
    outs = [None] * SWA_HEADS
    for var, heads in enumerate((SWA_SAME, SWA_SWAP)):
        qm = []
        sk = []
        for h in heads:
            blk = q[:, LANES * (h // 2):LANES * (h // 2 + 1)]
            keep = upper if h % 2 == 1 else jnp.logical_not(upper)
            qm.append(jnp.where(keep, blk, 0.0))
            sk.append(jnp.full((1, L), sinks_ref[h], F32))
        qs = jnp.concatenate(qm, axis=0).astype(BF16)
        sink = jnp.concatenate(sk, axis=1)
        s_t = _dot_nt(ks[var], qs) + tab_ref[var]
        s_prev = jnp.where(prev_invalid, -jnp.inf, s_t[:L])
        s_t = jnp.concatenate([s_prev, s_t[L:]], axis=0)
        m = jnp.maximum(jnp.max(s_t, axis=0, keepdims=True), sink)
        p = jnp.exp(s_t - m)
        norm = 1.0 / (jnp.sum(p, axis=0, keepdims=True) + jnp.exp(sink - m))
        o_t = _dot(vts[var], p.astype(BF16)) * norm
        for i, h in enumerate(heads):
            outs[h] = o_t[:, i * L:(i + 1) * L].T
    blocks = [jnp.where(upper, outs[2 * j + 1], outs[2 * j]) for j in range(SWA_HEADS // 2)]
    return jnp.concatenate(blocks, axis=1)


def _rotary_tables(pos, inv_ref):
    lane = lax.broadcasted_iota(jnp.int32, (1, LANES), 1)
    ang = pos * inv_ref[...]
    sin = jnp.sin(ang)
    return jnp.cos(ang), jnp.where(lane < HEAD_D // 2, -sin, sin)


def _rotate(x, cos, sin_signed):
    return x * cos + pltpu.roll(x, HEAD_D // 2, 1) * sin_signed


def _gate_slab(raw, gb_ref):
    lane = lax.broadcasted_iota(jnp.int32, (1, LANES), 1)
    x = raw + gb_ref[...]
    return jnp.where((lane >= N_HEADS) & (lane < 2 * N_HEADS), _log_sigmoid(x), x)


def _head_rms(h, gain):
    parts = []
    for i in range(N_HEADS):
        blk = h[:, HEAD_D * i:HEAD_D * (i + 1)]
        parts.append(blk * lax.rsqrt(jnp.mean(blk * blk, axis=-1, keepdims=True) + EPS))
    return jnp.concatenate(parts, axis=-1) * gain


def _rms_over_rows(h_t):
    return h_t * lax.rsqrt(jnp.mean(h_t * h_t, axis=0, keepdims=True) + EPS)


def _post_math(hml, osw, ort, mlo, rtg, g0, g1, g2, x, mlg, retg, ng, wml, wsw, wrt, wout, head_normed=False):
    if not head_normed:
        hml = _head_rms(hml, mlg)
        ort = _head_rms(ort, retg)
    hm = hml * _sigmoid(mlo)
    y_ml = _dot(hm.astype(BF16), wml)
    y_sw = _dot(osw.astype(BF16), wsw)
    rt = ort * (rtg * _sigmoid(rtg))
    y_rt = _dot(rt.astype(BF16), wrt)
    merged = _sigmoid(g0) * y_ml + _sigmoid(g1) * y_sw + _sigmoid(g2) * y_rt
    return x + _rms(_dot(merged.astype(BF16), wout), ng)


def _mixer_seq_prompt_kernel(mq_ref, mk_ref, mv_ref, swq_ref, misc_ref, rq_ref, rk_ref, rv_ref,
                             mlo_ref, rtg_ref, g0_ref, g1_ref, g2_ref, x_ref,
                             gb_ref, inv_ref, sinks_ref, rb_ref,
                             mlg_ref, retg_ref, ng_ref, wml_ref, wsw_ref, wrt_ref, wout_ref,
                             x1_ref, c_ref, n_ref, m_ref, s_ref,
                             st_scr, k_same, k_swap, vt_buf, tab_t, cos_l, sin_l, dec_in_t, eye_ref,
                             hml_s, osw_s, ort_s):
    c = pl.program_id(0)
    active = c < pl.num_programs(0) - 1
    L = CHUNK
    lane = lax.broadcasted_iota(jnp.int32, (1, LANES), 1)
    row = lax.broadcasted_iota(jnp.int32, (L, L), 0)
    col = lax.broadcasted_iota(jnp.int32, (L, L), 1)
    causal_t = row <= col

    @pl.when(c == 0)
    def _init():
        c_ref[...] = jnp.zeros(c_ref.shape, F32)
        n_ref[...] = jnp.zeros(n_ref.shape, F32)
        m_ref[...] = jnp.zeros(m_ref.shape, F32)
        s_ref[...] = jnp.zeros(s_ref.shape, F32)
        hml_s[...] = jnp.zeros(hml_s.shape, F32)
        osw_s[...] = jnp.zeros(osw_s.shape, F32)
        ort_s[...] = jnp.zeros(ort_s.shape, F32)
        st_scr[...] = jnp.zeros(st_scr.shape, F32)
        k_same[...] = jnp.zeros(k_same.shape, BF16)
        k_swap[...] = jnp.zeros(k_swap.shape, BF16)
        vt_buf[...] = jnp.zeros(vt_buf.shape, BF16)
        _build_swa_table_t(tab_t, rb_ref)
        ang = row.astype(F32) * inv_ref[...]
        cos_l[...] = jnp.cos(ang)
        sin_l[...] = jnp.sin(ang)
        rel_t = (col - row).astype(F32)
        for h in range(N_HEADS):
            dec_in_t[h] = jnp.where(causal_t, jnp.exp(LOG_GAMMA[h] * rel_t), 0.0)
        eye_ref[...] = jnp.where(row == col, 1.0, 0.0).astype(BF16)

    eye = eye_ref[...]

    x1_ref[...] = _post_math(hml_s[...], osw_s[...], ort_s[...], mlo_ref[...], rtg_ref[...],
                             g0_ref[...], g1_ref[...], g2_ref[...], x_ref[...],
                             mlg_ref[...], retg_ref[...], ng_ref[...],
                             wml_ref[...], wsw_ref[...], wrt_ref[...], wout_ref[...], head_normed=True)

    lf = _gate_slab(misc_ref[:, MISC_IF:MISC_IF + LANES], gb_ref)
    bcum = jnp.dot((row >= col).astype(F32), lf, precision=lax.Precision.HIGHEST, preferred_element_type=F32)
    colslab = jnp.where(lane < N_HEADS, lf, bcum)
    rowslab = colslab.T
    m_all = m_ref[...]
    m_out = jnp.zeros((1, LANES), F32)
    for h in range(N_HEADS):
        hs = slice(HEAD_D * h, HEAD_D * (h + 1))
        g_c = colslab[:, h:h + 1] - colslab[:, N_HEADS + h:N_HEADS + h + 1]
        i_r = rowslab[h:h + 1, :]
        b_r = rowslab[N_HEADS + h:N_HEADS + h + 1, :]
        m_prev = m_all[:, h:h + 1]
        gm = jnp.where(causal_t, g_c, -jnp.inf)
        mx = jnp.maximum(jnp.max(gm, axis=0, keepdims=True), m_prev)
        m_t = b_r + mx
        inter = jnp.exp(m_prev - mx)
        dm_t = jnp.exp(gm - mx)
        q = mq_ref[:, hs]
        k = mk_ref[:, hs] * (HEAD_D ** -0.5)
        qb = q.astype(BF16)
        kb = k.astype(BF16)
        vt = _transpose_bf16(mv_ref[:, hs].astype(BF16), eye)
        s_t = _dot_nt(kb, qb) * dm_t
        c_old = c_ref[h]
        n_old = n_ref[h]
        num_t = _dot(vt, s_t.astype(BF16)) + _dot_nt(c_old.astype(BF16), qb) * inter
        nq = _dot_nt(jnp.broadcast_to(n_old, (MXU_MIN_ROWS, HEAD_D)).astype(BF16), qb)[0:1, :]
        den = jnp.sum(s_t, axis=0, keepdims=True) + inter * nq
        den = jnp.maximum(jnp.abs(den), jnp.exp(-m_t))
        hml_s[:, hs] = _rms_over_rows(num_t * (1.0 / den)).T * mlg_ref[:, hs]
        m_new = m_t[:, L - 1:L]
        b_last = b_r[:, L - 1:L]
        decay = jnp.exp(b_last + m_prev - m_new)
        w_r = jnp.exp((i_r - b_r) + (b_last - m_new))
        c_new = decay * c_old + _dot((vt.astype(F32) * w_r).astype(BF16), kb)
        n_new = decay * n_old + _dot(jnp.broadcast_to(w_r, (MXU_MIN_ROWS, L)).astype(BF16), kb)[0:1, :]
        c_ref[h] = jnp.where(active, c_new, c_old)
        n_ref[h] = jnp.where(active, n_new, n_old)
        m_out = jnp.where(lane == h, m_new, m_out)
    m_ref[...] = jnp.where(active, m_out, m_all)

    ang0 = (c * L).astype(F32) * inv_ref[...]
    cos0 = jnp.cos(ang0)
    sin0 = jnp.sin(ang0)
    cos_t = cos_l[...]
    sin_t = sin_l[...]
    cos = cos0 * cos_t - sin0 * sin_t
    sin = sin0 * cos_t + cos0 * sin_t
    sin_signed = jnp.where(lane < HEAD_D // 2, -sin, sin)
    l_row = lane.astype(F32)
    for h in range(N_HEADS):
        hs = slice(HEAD_D * h, HEAD_D * (h + 1))
        lg = LOG_GAMMA[h]
        qr = _rotate(rq_ref[:, hs], cos, sin_signed).astype(BF16)
        kr = (_rotate(rk_ref[:, hs], cos, sin_signed) * (HEAD_D ** -0.5)).astype(BF16)
        vt = _transpose_bf16(rv_ref[:, hs].astype(BF16), eye)
        s_t = _dot_nt(kr, qr) * dec_in_t[h]
        st_old = st_scr[h]
        o_t = _dot(vt, s_t.astype(BF16)) + _dot_nt(st_old.astype(BF16), qr) * jnp.exp(lg * (l_row + 1.0))
        ort_s[:, hs] = _rms_over_rows(o_t).T * retg_ref[:, hs]
        k_decay = jnp.exp(lg * (L - 1.0 - l_row))
        st_new = math.exp(lg * L) * st_old + _dot((vt.astype(F32) * k_decay).astype(BF16), kr)
        st_scr[h] = jnp.where(active, st_new, st_old)

    @pl.when(c == pl.num_programs(0) - 1)
    def _emit_state():
        for h in range(N_HEADS):
            s_ref[h] = st_scr[h].T

    k_new = misc_ref[:, MISC_K:MISC_K + LANES]
    k_same[0:L, :] = k_same[L:2 * L, :]
    k_swap[0:L, :] = k_swap[L:2 * L, :]
    k_same[L:2 * L, :] = k_new.astype(BF16)
    k_swap[L:2 * L, :] = pltpu.roll(k_new, SWA_D, 1).astype(BF16)
    vt_buf[:, 0:L] = vt_buf[:, L:2 * L]
    vt_buf[:, L:2 * L] = _transpose_bf16(misc_ref[:, MISC_V:MISC_V + LANES].astype(BF16), eye)
    osw_s[...] = _swa_block_t(swq_ref[...], k_same, k_swap, vt_buf, tab_t, sinks_ref, c == 0)


def mixer_prompt(z, x, gb, inv, sinks, rb, mlg, retg, ng, wml, wsw, wrt, wout):
    t = z.shape[0]
    L = CHUNK
    n = t // L

    def zspec(cb):
        return pl.BlockSpec((L, 512), lambda c, cb=cb: (jnp.minimum(c, n - 1), cb))

    def tail(w, cb=0):
        return pl.BlockSpec((L, w), lambda c, cb=cb: (jnp.maximum(c - 1, 0), cb))

    def const(shape):
        return pl.BlockSpec(shape, lambda c: (0,) * len(shape))

    def weight(w):
        return pl.BlockSpec(w.shape, lambda c: (0, 0), pipeline_mode=pl.Buffered(1))

    smem = pl.BlockSpec(memory_space=pltpu.SMEM)
    return pl.pallas_call(
        _mixer_seq_prompt_kernel,
        grid=(n + 1,),
        in_specs=[zspec(ZC_ML_Q), zspec(ZC_ML_K), zspec(ZC_ML_V), zspec(ZC_SW_Q), zspec(ZC_MISC),
                  zspec(ZC_RT_Q), zspec(ZC_RT_K), zspec(ZC_RT_V),
                  tail(512, ZC_ML_O), tail(512, ZC_RT_G),
                  tail(1024, ZC_GATES), tail(1024, ZC_GATES + 1), tail(1024, ZC_GATES + 2), tail(D_MODEL),
                  const((1, LANES)), const((1, LANES)), smem, smem,
                  const((1, 512)), const((1, 512)), const((1, D_MODEL)),
                  weight(wml), weight(wsw), weight(wrt), weight(wout)],
        out_specs=[tail(D_MODEL),
                   const((N_HEADS, HEAD_D, HEAD_D)), const((N_HEADS, 1, HEAD_D)), const((1, LANES)),
                   const((N_HEADS, HEAD_D, HEAD_D))],
        out_shape=[jax.ShapeDtypeStruct((t, D_MODEL), F32),
                   jax.ShapeDtypeStruct((N_HEADS, HEAD_D, HEAD_D), F32),
                   jax.ShapeDtypeStruct((N_HEADS, 1, HEAD_D), F32),
                   jax.ShapeDtypeStruct((1, LANES), F32),
                   jax.ShapeDtypeStruct((N_HEADS, HEAD_D, HEAD_D), F32)],
        scratch_shapes=[pltpu.VMEM((N_HEADS, L, L), F32),
                        pltpu.VMEM((2 * L, LANES), BF16), pltpu.VMEM((2 * L, LANES), BF16),
                        pltpu.VMEM((LANES, 2 * L), BF16),
                        pltpu.VMEM((2, 2 * L, 4 * L), F32),
                        pltpu.VMEM((L, L), F32), pltpu.VMEM((L, L), F32),
                        pltpu.VMEM((N_HEADS, L, L), F32),
                        pltpu.VMEM((L, L), BF16),
                        pltpu.VMEM((L, 512), F32), pltpu.VMEM((L, 512), F32), pltpu.VMEM((L, 512), F32)],
        compiler_params=_params(1),
        name="mixer_prompt",
    )(z, z, z, z, z, z, z, z, z, z, z, z, z, x, gb, inv, sinks, rb, mlg, retg, ng, wml, wsw, wrt, wout)


def _shift_rows(x, d):
    return x if d == 0 else pltpu.roll(x, d, 0)


def _mixer_seq_sample_kernel(mq_ref, mk_ref, mv_ref, swq_ref, misc_ref, rq_ref, rk_ref, rv_ref,
                             c0_ref, n0_ref, m0_ref, s0_ref, ck_ref, cv_ref,
                             gb_ref, inv_ref, sinks_ref, rb_ref, *rest, group, first_layer):
    if not first_layer:
        rest = rest[2:]
    hml_ref, osw_ref, ort_ref, c_all, n_ref, m_ref, s_all, kbuf, vbuf, tab = rest
    R = SEQ_PAD
    if first_layer:
        c_ref, s_ref = c_all.at[0], s_all.at[0]
        c_all[1:] = jnp.zeros((c_all.shape[0] - 1,) + c_all.shape[1:], F32)
        s_all[1:] = jnp.zeros((s_all.shape[0] - 1,) + s_all.shape[1:], F32)
    else:
        c_ref, s_ref = c_all, s_all

    @pl.when(pl.program_id(0) == 0)
    def _init():
        kbuf[...] = jnp.zeros(kbuf.shape, F32)
        vbuf[...] = jnp.zeros(vbuf.shape, F32)
        _build_swa_table(tab, rb_ref, R)

    lane = lax.broadcasted_iota(jnp.int32, (1, LANES), 1)
    l_idx = lax.broadcasted_iota(jnp.int32, (R, 1), 0)
    real = l_idx < N_NEW
    l_f = l_idx.astype(F32)
    cos, sin_signed = _rotary_tables((PAST_LEN + l_idx).astype(F32), inv_ref)

    def body(g, carry):
        rows = pl.ds(pl.multiple_of(g * R, R), R)

        lf = _gate_slab(misc_ref[rows, MISC_IF:MISC_IF + LANES], gb_ref)
        bsum = lf
        for d in range(1, N_NEW):
            bsum = bsum + jnp.where(l_idx >= d, _shift_rows(lf, d), 0.0)
        slab = jnp.where(lane < N_HEADS, lf, bsum)
        shifted = [_shift_rows(slab, d) for d in range(N_NEW)]
        m0 = m0_ref[g]
        n0 = n0_ref[g]
        m_out = jnp.zeros((1, LANES), F32)
        for h in range(N_HEADS):
            hs = slice(HEAD_D * h, HEAD_D * (h + 1))
            i_c = slab[:, h:h + 1]
            b_c = slab[:, N_HEADS + h:N_HEADS + h + 1]
            m_prev = m0[:, h:h + 1]
            logd = []
            for d in range(N_NEW):
                sd = shifted[d]
                ld = b_c - sd[:, N_HEADS + h:N_HEADS + h + 1] + sd[:, h:h + 1]
                logd.append(jnp.where(l_idx >= d, ld, -jnp.inf))
            log_inter = b_c + m_prev
            m_t = log_inter
            for d in range(N_NEW):
                m_t = jnp.maximum(m_t, logd[d])
            inter = jnp.exp(log_inter - m_t)
            q = mq_ref[rows, hs]
            k = mk_ref[rows, hs] * (HEAD_D ** -0.5)
            v = mv_ref[rows, hs]
            c_old = c0_ref[g, h]
            n_old = n0[:, hs]
            num = _dot_nt(q.astype(BF16), c_old.astype(BF16)) * inter
            den = inter * jnp.sum(q * n_old, axis=1, keepdims=True)
            for d in range(N_NEW):
                s_d = jnp.sum(q * _shift_rows(k, d), axis=1, keepdims=True) * jnp.exp(logd[d] - m_t)
                num = num + s_d * _shift_rows(v, d)
                den = den + s_d
            den = jnp.maximum(jnp.abs(den), jnp.exp(-m_t))
            hml_ref[rows, hs] = num / den
            m_new = m_t[N_NEW - 1:N_NEW, :]
            b_last = b_c[N_NEW - 1:N_NEW, :]
            decay = jnp.exp(b_last + m_prev - m_new)
            w = jnp.where(real, jnp.exp(b_last - b_c + i_c - m_new), 0.0)
            c_ref[g, h] = decay * c_old + _dot_tn((v * w).astype(BF16), k.astype(BF16))
            n_ref[g, :, hs] = decay * n_old + jnp.sum(k * w, axis=0, keepdims=True)
            m_out = jnp.where(lane == h, m_new, m_out)
        m_ref[g] = m_out

        for h in range(N_HEADS):
            hs = slice(HEAD_D * h, HEAD_D * (h + 1))
            lg = LOG_GAMMA[h]
            qr = _rotate(rq_ref[rows, hs], cos, sin_signed)
            kr = _rotate(rk_ref[rows, hs], cos, sin_signed) * (HEAD_D ** -0.5)
            v = rv_ref[rows, hs]
            s_old = s0_ref[g, h]
            o = _dot(qr.astype(BF16), s_old.astype(BF16)) * jnp.exp(lg * (l_f + 1.0))
            for d in range(N_NEW):
                s_d = jnp.sum(qr * _shift_rows(kr, d), axis=1, keepdims=True) * math.exp(lg * d)
                o = o + jnp.where(l_idx >= d, s_d, 0.0) * _shift_rows(v, d)
            ort_ref[rows, hs] = o
            k_decay = jnp.where(real, jnp.exp(lg * (N_NEW - 1.0 - l_f)), 0.0)
            s_ref[g, h] = math.exp(lg * N_NEW) * s_old + _dot_tn((kr * k_decay).astype(BF16), v.astype(BF16))

        kbuf[0:CHUNK, :] = ck_ref[g]
        vbuf[0:CHUNK, :] = cv_ref[g]
        kbuf[CHUNK:CHUNK + R, :] = misc_ref[rows, MISC_K:MISC_K + LANES]
        vbuf[CHUNK:CHUNK + R, :] = misc_ref[rows, MISC_V:MISC_V + LANES]
        osw_ref[rows, :] = _swa_block(swq_ref[rows, :], kbuf, vbuf, tab, sinks_ref, None)
        return carry

    lax.fori_loop(0, group, body, 0, unroll=SAMPLE_UNROLL)


SAMPLE_UNROLL = 2


def mixer_seq_sample(z, layer, c0_all, n0, m0, s0_all, ck, cv, gb, inv, sinks, rb, c_prev, s_prev, group=8):
    depth, nb = c0_all.shape[:2]
    R = SEQ_PAD
    first_layer = c_prev is None
    assert first_layer == (layer == 0)

    def zspec(cb):
        return pl.BlockSpec((group * R, 512), lambda i, cb=cb: (i, cb))

    def const(shape):
        return pl.BlockSpec(shape, lambda i: (0,) * len(shape))

    def per_seq(shape):
        return pl.BlockSpec((group,) + shape, lambda i: (i,) + (0,) * len(shape))

    smem = pl.BlockSpec(memory_space=pltpu.SMEM)
    tok = pl.BlockSpec((group * R, 512), lambda i: (i, 0))
    st = (N_HEADS, HEAD_D, HEAD_D)
    st_in = pl.BlockSpec((None, group) + st, lambda i: (layer, i, 0, 0, 0))
    if first_layer:
        st_out = pl.BlockSpec((depth, group) + st, lambda i: (0, i, 0, 0, 0))
        extra_specs, extra_args, aliases = [], [], {}
    else:
        st_out = st_in
        extra_specs = [pl.BlockSpec(memory_space=pl.ANY)] * 2
        extra_args = [c_prev, s_prev]
        aliases = {18: 3, 19: 6}
    return pl.pallas_call(
        functools.partial(_mixer_seq_sample_kernel, group=group, first_layer=first_layer),
        grid=(nb // group,),
        in_specs=[zspec(ZC_ML_Q), zspec(ZC_ML_K), zspec(ZC_ML_V), zspec(ZC_SW_Q), zspec(ZC_MISC),
                  zspec(ZC_RT_Q), zspec(ZC_RT_K), zspec(ZC_RT_V),
                  st_in, per_seq((1, 512)), per_seq((1, LANES)), st_in,
                  per_seq((CHUNK, LANES)), per_seq((CHUNK, LANES)),
                  const((1, LANES)), const((1, LANES)), smem, smem] + extra_specs,
        out_specs=[tok, tok, tok, st_out, per_seq((1, 512)), per_seq((1, LANES)), st_out],
        out_shape=[jax.ShapeDtypeStruct((nb * R, 512), F32)] * 3
        + [jax.ShapeDtypeStruct((depth, nb) + st, F32),
           jax.ShapeDtypeStruct((nb, 1, 512), F32),
           jax.ShapeDtypeStruct((nb, 1, LANES), F32),
           jax.ShapeDtypeStruct((depth, nb) + st, F32)],
        scratch_shapes=[pltpu.VMEM((2 * CHUNK, LANES), F32), pltpu.VMEM((2 * CHUNK, LANES), F32),
                        pltpu.VMEM((2, 4 * R, 2 * CHUNK), F32)],
        input_output_aliases=aliases,
        compiler_params=_params(1),
        name="mixer_seq_sample",
    )(z, z, z, z, z, z, z, z, c0_all, n0, m0, s0_all, ck, cv, gb, inv, sinks, rb, *extra_args)


def _mixer_post_kernel(hml_ref, osw_ref, ort_ref, mlo_ref, rtg_ref, g0_ref, g1_ref, g2_ref, x_ref,
                       mlg_ref, retg_ref, ng_ref, wml_ref, wsw_ref, wrt_ref, wout_ref, o_ref):
    o_ref[...] = _post_math(hml_ref[...], osw_ref[...], ort_ref[...], mlo_ref[...], rtg_ref[...],
                            g0_ref[...], g1_ref[...], g2_ref[...], x_ref[...],
                            mlg_ref[...], retg_ref[...], ng_ref[...],
                            wml_ref[...], wsw_ref[...], wrt_ref[...], wout_ref[...])


def mixer_post(hml, osw, ort, z, x, mlg, retg, ng, wml, wsw, wrt, wout, tm, name):
    m = x.shape[0]

    def tok(w, cb=0):
        return pl.BlockSpec((tm, w), lambda i, cb=cb: (i, cb))

    def const(shape):
        return pl.BlockSpec(shape, lambda i: (0,) * len(shape))

    return pl.pallas_call(
        _mixer_post_kernel,
        grid=(m // tm,),
        in_specs=[tok(512), tok(512), tok(512), tok(512, ZC_ML_O), tok(512, ZC_RT_G),
                  tok(1024, ZC_GATES), tok(1024, ZC_GATES + 1), tok(1024, ZC_GATES + 2), tok(D_MODEL),
                  const((1, 512)), const((1, 512)), const((1, D_MODEL)),
                  const((512, D_MODEL)), const((512, D_MODEL)), const((512, D_MODEL)),
                  const((D_MODEL, D_MODEL))],
        out_specs=tok(D_MODEL),
        out_shape=jax.ShapeDtypeStruct((m, D_MODEL), F32),
        compiler_params=_params(1),
        name=name,
    )(hml, osw, ort, z, z, z, z, z, x, mlg, retg, ng, wml, wsw, wrt, wout)


def _xattn_heads(q, k, v):
    lane = lax.broadcasted_iota(jnp.int32, (1, X_HEADS * X_D), 1)
    acc = jnp.zeros(q.shape, F32)
    for h in range(X_HEADS):
        mask = (lane >= X_D * h) & (lane < X_D * (h + 1))
        s = _dot_nt(jnp.where(mask, q, 0.0).astype(BF16), k) * (X_D ** -0.5)
        m = jnp.max(s, axis=1, keepdims=True)
        p = jnp.exp(s - m)
        p = p / jnp.sum(p, axis=1, keepdims=True)
        acc = acc + jnp.where(mask, _dot(p.astype(BF16), v), 0.0)
    return acc


def _xattn_prompt_kernel(x_ref, gin_ref, gout_ref, wcq_ref, kv_ref, wco_ref, o_ref):
    x = x_ref[...]
    q = _dot(_rms(x, gin_ref[...]).astype(BF16), wcq_ref[...])
    k = kv_ref[:, 0:X_HEADS * X_D].astype(BF16)
    v = kv_ref[:, X_HEADS * X_D:2 * X_HEADS * X_D].astype(BF16)
    o = _xattn_heads(q, k, v)
    o_ref[...] = x + _rms(_dot(o.astype(BF16), wco_ref[...]), gout_ref[...])


def xattn_prompt(x, gin, gout, wcq, kv, wco, tm):
    m = x.shape[0]

    def const(shape):
        return pl.BlockSpec(shape, lambda i: (0,) * len(shape))

    tok = pl.BlockSpec((tm, D_MODEL), lambda i: (i, 0))
    return pl.pallas_call(
        _xattn_prompt_kernel,
        grid=(m // tm,),
        in_specs=[tok, const((1, D_MODEL)), const((1, D_MODEL)), const(wcq.shape), const(kv.shape),
                  const(wco.shape)],
        out_specs=tok,
        out_shape=jax.ShapeDtypeStruct((m, D_MODEL), F32),
        compiler_params=_params(1),
        name="xattn_prompt",
    )(x, gin, gout, wcq, kv, wco)


def _xattn_sample_kernel(x_ref, gin_ref, gout_ref, wcq_ref, kt_ref, vt_ref, wco_ref, o_ref, q_scr, a_scr,
                         *, group):
    R = SEQ_PAD
    x = x_ref[...]
    q_scr[...] = _dot(_rms(x, gin_ref[...]).astype(BF16), wcq_ref[...])
    lane = lax.broadcasted_iota(jnp.int32, (1, X_HEADS * X_D), 1)
    masks = [(lane >= X_D * h) & (lane < X_D * (h + 1)) for h in range(X_HEADS)]

    def body(g, carry):
        rows = pl.ds(pl.multiple_of(g * R, R), R)
        q = q_scr[rows, :]
        qs = jnp.concatenate([jnp.where(mk, q, 0.0) for mk in masks], axis=0)
        s = _dot(qs.astype(BF16), kt_ref[g].astype(BF16)) * (X_D ** -0.5)
        m = jnp.max(s, axis=1, keepdims=True)
        p = jnp.exp(s - m)
        p = p / jnp.sum(p, axis=1, keepdims=True)
        o = _dot_nt(p.astype(BF16), vt_ref[g].astype(BF16))
        acc = jnp.zeros((R, X_HEADS * X_D), F32)
        for h in range(X_HEADS):
            acc = acc + jnp.where(masks[h], o[R * h:R * (h + 1)], 0.0)
        a_scr[rows, :] = acc
        return carry

    lax.fori_loop(0, group, body, 0, unroll=4)
    o_ref[...] = x + _rms(_dot(a_scr[...].astype(BF16), wco_ref[...]), gout_ref[...])


def xattn_sample(x, gin, gout, wcq, layer, mkt, mvt, wco, group=16):
    nb = mkt.shape[1]
    R = SEQ_PAD

    def const(shape):
        return pl.BlockSpec(shape, lambda i: (0,) * len(shape))

    tok = pl.BlockSpec((group * R, D_MODEL), lambda i: (i, 0))
    mem = pl.BlockSpec((None, group, X_HEADS * X_D, N_MEM), lambda i: (layer, i, 0, 0))
    return pl.pallas_call(
        functools.partial(_xattn_sample_kernel, group=group),
        grid=(nb // group,),
        in_specs=[tok, const((1, D_MODEL)), const((1, D_MODEL)), const(wcq.shape), mem, mem, const(wco.shape)],
        out_specs=tok,
        out_shape=jax.ShapeDtypeStruct((nb * R, D_MODEL), F32),
        scratch_shapes=[pltpu.VMEM((group * R, X_HEADS * X_D), F32),
                        pltpu.VMEM((group * R, X_HEADS * X_D), F32)],
        compiler_params=_params(1),
        name="xattn_sample",
    )(x, gin, gout, wcq, mkt, mvt, wco)


FFN_SPLIT = 2


def _ffn_kernel(x_ref, gin_ref, gout_ref, wgu_ref, wd_ref, o_ref):
    x = x_ref[...]
    u = _rms(x, gin_ref[...]).astype(BF16)
    fc = D_FF // FFN_SPLIT
    acc = None
    for c in range(FFN_SPLIT):
        g = _dot(u, wgu_ref[:, fc * c:fc * (c + 1)])
        up = _dot(u, wgu_ref[:, D_FF + fc * c:D_FF + fc * (c + 1)])
        h = (g * _sigmoid(g) * up).astype(BF16)
        part = _dot(h, wd_ref[fc * c:fc * (c + 1), :])
        acc = part if acc is None else acc + part
    o_ref[...] = x + _rms(acc, gout_ref[...])


def ffn(x, gin, gout, wgu, wd, tm, name):
    m = x.shape[0]

    def const(shape):
        return pl.BlockSpec(shape, lambda i: (0,) * len(shape))

    tok = pl.BlockSpec((tm, D_MODEL), lambda i: (i, 0))
    return pl.pallas_call(
        _ffn_kernel,
        grid=(m // tm,),
        in_specs=[tok, const((1, D_MODEL)), const((1, D_MODEL)),
                  pl.BlockSpec(wgu.shape, lambda i: (0, 0), pipeline_mode=pl.Buffered(1)),
                  pl.BlockSpec(wd.shape, lambda i: (0, 0), pipeline_mode=pl.Buffered(1))],
        out_specs=tok,
        out_shape=jax.ShapeDtypeStruct((m, D_MODEL), F32),
        compiler_params=_params(1),
        name=name,
    )(x, gin, gout, wgu, wd)


def _reorder_w_in(w):
    sizes = (512, 512, 512, 4, 4, 512, 512, 128, 128, 512, 512, 512, 512, 3072)
    offs = np.concatenate([[0], np.cumsum(sizes)])
    (ml_q, ml_k, ml_v, ml_i, ml_f, ml_o, sw_q, sw_k, sw_v, rt_q, rt_k, rt_v, rt_g, gates) = [
        w[:, int(offs[i]):int(offs[i + 1])] for i in range(len(sizes))]
    pad = jnp.zeros((w.shape[0], 512 - 128 - 128 - 8), w.dtype)
    out = jnp.concatenate([ml_q, ml_k, ml_v, ml_o, sw_q, sw_k, sw_v, ml_i, ml_f, pad,
                           rt_q, rt_k, rt_v, rt_g, gates], axis=1)
    assert out.shape[1] == Z_COLS
    return out.astype(BF16)


def _row(v):
    return v.reshape(1, -1).astype(F32)


def _decoder_layer(x, z_fn, mixer_fn, xattn_fn, lw, tm, tag):
    z = z_fn(x)
    x, states = mixer_fn(z, x)
    x = xattn_fn(x)
    x = ffn(x, lw["ng"][4], lw["ng"][5], lw["wgu"], lw["wd"], tm=tm, name="ffn_" + tag)
    return x, z, states


def kernel(x_prompt, x_sample, mem_prompt, state_mlstm_C, state_mlstm_n, state_mlstm_m, state_ret_S,
           cache_swa_k, cache_swa_v, cache_mem_k, cache_mem_v, norm_g, w_in, ml_gate_bias, ml_head_g,
           ret_head_g, swa_sinks, rel_bias, w_br_ml, w_br_swa, w_br_ret, w_out, w_cq, w_mkv, w_co,
           w_gu, w_down):
    depth = w_in.shape[0]
    bp, t, d = x_prompt.shape
    assert bp == 1 and d == D_MODEL and t % 1024 == 0
    nb, n_new, _ = x_sample.shape
    assert n_new == N_NEW
    R = SEQ_PAD

    xp = x_prompt.reshape(t, d)
    xs = jnp.pad(x_sample, ((0, 0), (0, R - n_new), (0, 0))).reshape(nb * R, d)
    mem = mem_prompt.reshape(N_MEM, d)
    half = HEAD_D // 2
    inv = ROPE_BASE ** (-jnp.arange(half, dtype=F32) / half)
    inv = jnp.concatenate([inv, inv]).reshape(1, LANES)
    rb = rel_bias.astype(F32).reshape(-1)
    mem_kt = jnp.transpose(cache_mem_k, (0, 1, 3, 4, 2)).reshape(depth, nb, X_HEADS * X_D, N_MEM)
    mem_vt = jnp.transpose(cache_mem_v, (0, 1, 3, 4, 2)).reshape(depth, nb, X_HEADS * X_D, N_MEM)

    outs_p = {k: [] for k in ("C", "n", "m", "S", "k", "v", "mk", "mv")}
    outs_s = {k: [] for k in ("n", "m", "k", "v")}
    sample_c = sample_s = None
    for l in range(depth):
        lw = dict(
            ng=[_row(norm_g[l, i]) for i in range(7)],
            mlg=_row(ml_head_g[l]), retg=_row(ret_head_g[l]),
            wml=w_br_ml[l].astype(BF16), wsw=w_br_swa[l].astype(BF16), wrt=w_br_ret[l].astype(BF16),
            wout=w_out[l].astype(BF16), wgu=w_gu[l].astype(BF16), wd=w_down[l].astype(BF16))
        win = _reorder_w_in(w_in[l])
        wcq = w_cq[l].astype(BF16)
        wco = w_co[l].astype(BF16)
        gb = jnp.concatenate([ml_gate_bias[l, 0], ml_gate_bias[l, 1],
                              jnp.zeros((LANES - 2 * N_HEADS,), F32)]).reshape(1, LANES).astype(F32)
        sinks = swa_sinks[l].astype(F32)

        kv = norm_matmul(mem, lw["ng"][6], w_mkv[l].astype(BF16), tm=N_MEM, tn=2 * X_HEADS * X_D, name="memory_kv")

        def seq_p(z, x):
            x1, c_, n_, m_, s_ = mixer_prompt(z, x, gb, inv, sinks, rb, lw["mlg"], lw["retg"], lw["ng"][1],
                                              lw["wml"], lw["wsw"], lw["wrt"], lw["wout"])
            return x1, (c_, n_, m_, s_)

        xp, zp, (c_, n_, m_, s_) = _decoder_layer(
            xp,
            lambda x: norm_matmul(x, lw["ng"][0], win, tm=min(IN_PROJ_TM, t), tn=1024, name="in_proj_prompt"),
            seq_p,
            lambda x: xattn_prompt(x, lw["ng"][2], lw["ng"][3], wcq, kv, wco, tm=512),
            lw, tm=512, tag="prompt")
        outs_p["C"].append(c_.reshape(1, N_HEADS, HEAD_D, HEAD_D))
        outs_p["n"].append(n_.reshape(1, N_HEADS, HEAD_D))
        outs_p["m"].append(m_[:, :N_HEADS])
        outs_p["S"].append(s_.reshape(1, N_HEADS, HEAD_D, HEAD_D))
        misc_last = zp[t - CHUNK:, 512 * ZC_MISC:512 * ZC_MISC + 2 * LANES]
        outs_p["k"].append(misc_last[:, :LANES].reshape(1, CHUNK, 2, SWA_D))
        outs_p["v"].append(misc_last[:, LANES:].reshape(1, CHUNK, 2, SWA_D))
        outs_p["mk"].append(kv[:, :X_HEADS * X_D].reshape(1, N_MEM, X_HEADS, X_D))
        outs_p["mv"].append(kv[:, X_HEADS * X_D:].reshape(1, N_MEM, X_HEADS, X_D))

        n0 = state_mlstm_n[l].astype(F32).reshape(nb, 1, N_HEADS * HEAD_D)
        m0 = jnp.pad(state_mlstm_m[l].astype(F32), ((0, 0), (0, LANES - N_HEADS))).reshape(nb, 1, LANES)
        ck = cache_swa_k[l].reshape(nb, CHUNK, LANES)
        cv = cache_swa_v[l].reshape(nb, CHUNK, LANES)

        def seq_s(z, x):
            hml, osw, ort, c_, n_, m_, s_ = mixer_seq_sample(
                z, l, state_mlstm_C, n0, m0, state_ret_S, ck, cv, gb, inv, sinks, rb, sample_c, sample_s)
            x1 = mixer_post(hml, osw, ort, z, x, lw["mlg"], lw["retg"], lw["ng"][1], lw["wml"], lw["wsw"],
                            lw["wrt"], lw["wout"], tm=min(256, x.shape[0]), name="mixer_post_sample")
            return x1, (c_, n_, m_, s_)

        xs, zs, (c_, n_, m_, s_) = _decoder_layer(
            xs,
            lambda x: norm_matmul(x, lw["ng"][0], win, tm=min(1024, nb * R), tn=1024, name="in_proj_sample"),
            seq_s,
            lambda x: xattn_sample(x, lw["ng"][2], lw["ng"][3], wcq, l, mem_kt, mem_vt, wco),
            lw, tm=min(512, nb * R), tag="sample")
        sample_c, sample_s = c_, s_
        outs_s["n"].append(n_.reshape(nb, N_HEADS, HEAD_D))
        outs_s["m"].append(m_.reshape(nb, LANES)[:, :N_HEADS])
        misc_new = zs.reshape(nb, R, Z_COLS)[:, :n_new, 512 * ZC_MISC:512 * ZC_MISC + 2 * LANES]
        k_new = misc_new[:, :, :LANES].reshape(nb, n_new, 2, SWA_D)
        v_new = misc_new[:, :, LANES:].reshape(nb, n_new, 2, SWA_D)
        outs_s["k"].append(jnp.concatenate([cache_swa_k[l][:, n_new:], k_new], axis=1))
        outs_s["v"].append(jnp.concatenate([cache_swa_v[l][:, n_new:], v_new], axis=1))

    y_p = xp.reshape(1, t, d)
    y_s = xs.reshape(nb, R, d)[:, :n_new]
    st = lambda d_, k: jnp.stack(d_[k])
    return (y_p, y_s,
            st(outs_p, "C"), st(outs_p, "n"), st(outs_p, "m"), st(outs_p, "S"),
            st(outs_p, "k"), st(outs_p, "v"), st(outs_p, "mk"), st(outs_p, "mv"),
            sample_c, st(outs_s, "n"), st(outs_s, "m"), sample_s,
            st(outs_s, "k"), st(outs_s, "v"))
```

```python
import functools
import math

import numpy as np
import jax
import jax.numpy as jnp
from jax import lax
from jax.experimental import pallas as pl
from jax.experimental.pallas import tpu as pltpu

F32 = jnp.float32
BF16 = jnp.bfloat16

D_MODEL = 1024
EPS = 1e-6
PAST_LEN = 16384
HEAD_D = 128
N_HEADS = 4
CHUNK = 128
SWA_HEADS = 8
SWA_D = 64
N_BUCKETS = 32
MAX_DISTANCE = 128
ROPE_BASE = 10000.0
N_MEM = 256
X_HEADS = 4
X_D = 64
D_FF = 2816
SEQ_PAD = 8
N_NEW = 4
LANES = 128
IN_PROJ_TM = 2048
MXU_MIN_ROWS = 16
VMEM_LIMIT = 48 * 1024 * 1024

ZC_ML_Q, ZC_ML_K, ZC_ML_V, ZC_ML_O, ZC_SW_Q, ZC_MISC, ZC_RT_Q, ZC_RT_K, ZC_RT_V, ZC_RT_G = range(10)
ZC_GATES = 5
Z_COLS = 8192
MISC_K, MISC_V, MISC_IF = 0, 128, 256

LOG2E = math.log2(math.e)
LN2 = math.log(2.0)
LOG_GAMMA = tuple(float(v) for v in np.log1p(-np.exp2(-5.0 - np.arange(N_HEADS, dtype=np.float32))))

SWA_SAME = (0, 2, 5, 7)
SWA_SWAP = (1, 3, 4, 6)


def _t5_bucket_bounds():
    n = np.arange(CHUNK)
    max_exact = N_BUCKETS // 2
    nf = np.maximum(n, 1).astype(np.float32)
    large = max_exact + (np.log(nf / np.float32(max_exact)) / np.float32(math.log(MAX_DISTANCE / max_exact))
                         * np.float32(N_BUCKETS - max_exact)).astype(np.int32)
    large = np.minimum(large, N_BUCKETS - 1)
    b = np.where(n < max_exact, n, large)
    assert np.all(np.diff(b) >= 0)
    runs = []
    for d in range(CHUNK):
        if runs and runs[-1][1] == int(b[d]):
            runs[-1][0] = d + 1
        else:
            runs.append([d + 1, int(b[d])])
    return tuple((hi, bk) for hi, bk in runs)


BUCKET_RUNS = _t5_bucket_bounds()


def _dot(a, b):
    return jnp.dot(a, b, preferred_element_type=F32)


def _dot_nt(a, b):
    return lax.dot_general(a, b, (((1,), (1,)), ((), ())), preferred_element_type=F32)


def _dot_tn(a, b):
    return lax.dot_general(a, b, (((0,), (0,)), ((), ())), preferred_element_type=F32)


def _rms(x, g):
    return x * lax.rsqrt(jnp.mean(x * x, axis=-1, keepdims=True) + EPS) * g


def _sigmoid(x):
    return 1.0 / (1.0 + jnp.exp2(x * (-LOG2E)))


def _log_sigmoid(x):
    return jnp.minimum(x, 0.0) - jnp.log1p(jnp.exp(-jnp.abs(x)))


def _layer_weight(w_all, layer):
    shape = w_all.shape[1:]
    return pl.BlockSpec((None,) + shape, lambda *_: (layer,) + (0,) * len(shape), pipeline_mode=pl.Buffered(1))


def _params(n_grid):
    return pltpu.CompilerParams(dimension_semantics=("arbitrary",) * n_grid, vmem_limit_bytes=VMEM_LIMIT)


def _norm_matmul_kernel(x_ref, g_ref, w_ref, o_ref, u_ref, *, w_transposed):
    @pl.when(pl.program_id(1) == 0)
    def _():
        u_ref[...] = _rms(x_ref[...], g_ref[...]).astype(BF16)

    o_ref[...] = _dot_nt(u_ref[...], w_ref[...]) if w_transposed else _dot(u_ref[...], w_ref[...])


def norm_matmul(x, g, w_all, layer, tm, tn, name, w_transposed=False):
    m, k = x.shape
    n = w_all.shape[1] if w_transposed else w_all.shape[2]
    if w_transposed:
        w_spec = pl.BlockSpec((None, tn, k), lambda i, j: (layer, j, 0))
    else:
        w_spec = pl.BlockSpec((None, k, tn), lambda i, j: (layer, 0, j))
    return pl.pallas_call(
        functools.partial(_norm_matmul_kernel, w_transposed=w_transposed),
        grid=(m // tm, n // tn),
        in_specs=[pl.BlockSpec((tm, k), lambda i, j: (i, 0)),
                  pl.BlockSpec((1, k), lambda i, j: (0, 0)),
                  w_spec],
        out_specs=pl.BlockSpec((tm, tn), lambda i, j: (i, j)),
        out_shape=jax.ShapeDtypeStruct((m, n), F32),
        scratch_shapes=[pltpu.VMEM((tm, k), BF16)],
        compiler_params=_params(2),
        name=name,
    )(x, g, w_all)


def _build_swa_table(tab_ref, rb_ref, lq):
    row = lax.broadcasted_iota(jnp.int32, (lq, 2 * CHUNK), 0)
    col = lax.broadcasted_iota(jnp.int32, (lq, 2 * CHUNK), 1)
    delta = row + CHUNK - col
    valid = (delta >= 0) & (delta < CHUNK)
    for var, heads in enumerate((SWA_SAME, SWA_SWAP)):
        for i, h in enumerate(heads):
            val = jnp.full((lq, 2 * CHUNK), rb_ref[BUCKET_RUNS[-1][1] * SWA_HEADS + h], F32)
            for hi, bk in reversed(BUCKET_RUNS[:-1]):
                val = jnp.where(delta < hi, rb_ref[bk * SWA_HEADS + h], val)
            tab_ref[var, i * lq:(i + 1) * lq, :] = jnp.where(valid, val, -jnp.inf)


def _swa_block(q, kbuf, vbuf, tab_ref, sinks_ref, prev_invalid):
    lq = q.shape[0]
    lane = lax.broadcasted_iota(jnp.int32, (1, LANES), 1)
    upper = lane >= SWA_D
    q = q * (SWA_D ** -0.5)
    k = kbuf[...]
    v = vbuf[...]
    ks = (k.astype(BF16), pltpu.roll(k, SWA_D, 1).astype(BF16))
    vs = (v.astype(BF16), pltpu.roll(v, SWA_D, 1).astype(BF16))
    outs = [None] * SWA_HEADS
    for var, heads in enumerate((SWA_SAME, SWA_SWAP)):
        qm = []
        sk = []
        for h in heads:
            blk = q[:, LANES * (h // 2):LANES * (h // 2 + 1)]
            keep = upper if h % 2 == 1 else jnp.logical_not(upper)
            qm.append(jnp.where(keep, blk, 0.0))
            sk.append(jnp.full((lq, 1), sinks_ref[h], F32))
        qs = jnp.concatenate(qm, axis=0).astype(BF16)
        sink = jnp.concatenate(sk, axis=0)
        s = _dot_nt(qs, ks[var]) + tab_ref[var]
        if prev_invalid is not None:
            col = lax.broadcasted_iota(jnp.int32, s.shape, 1)
            s = jnp.where(jnp.logical_and(prev_invalid, col < CHUNK), -jnp.inf, s)
        m = jnp.maximum(jnp.max(s, axis=1, keepdims=True), sink)
        p = jnp.exp(s - m)
        norm = 1.0 / (jnp.sum(p, axis=1, keepdims=True) + jnp.exp(sink - m))
        o = _dot(p.astype(BF16), vs[var]) * norm
        for i, h in enumerate(heads):
            outs[h] = o[i * lq:(i + 1) * lq]
    blocks = [jnp.where(upper, outs[2 * j + 1], outs[2 * j]) for j in range(SWA_HEADS // 2)]
    return jnp.concatenate(blocks, axis=1)


def _transpose_bf16(x, eye):
    return _dot_nt(eye, x).astype(BF16)


def _build_swa_table_t(tab_ref, rb_ref):
    L = CHUNK
    srow = lax.broadcasted_iota(jnp.int32, (2 * L, L), 0)
    lcol = lax.broadcasted_iota(jnp.int32, (2 * L, L), 1)
    delta = lcol + L - srow
    valid = (delta >= 0) & (delta < L)
    for var, heads in enumerate((SWA_SAME, SWA_SWAP)):
        for i, h in enumerate(heads):
            val = jnp.full((2 * L, L), rb_ref[BUCKET_RUNS[-1][1] * SWA_HEADS + h], F32)
            for hi, bk in reversed(BUCKET_RUNS[:-1]):
                val = jnp.where(delta < hi, rb_ref[bk * SWA_HEADS + h], val)
            tab_ref[var, :, i * L:(i + 1) * L] = jnp.where(valid, val, -jnp.inf)


def _swa_block_t(q, k_same, k_swap, vt_buf, tab_ref, sinks_ref, prev_invalid):
    L = CHUNK
    lane = lax.broadcasted_iota(jnp.int32, (1, LANES), 1)
    upper = lane >= SWA_D
    q = q * (SWA_D ** -0.5)
    vt = vt_buf[...]
    vts = (vt, jnp.concatenate([vt[SWA_D:], vt[:SWA_D]], axis=0))
    ks = (k_same[...], k_swap[...])
    outs = [None] * SWA_HEADS
    for var, heads in enumerate((SWA_SAME, SWA_SWAP)):
        qm = []
        sk = []
        for h in heads:
            blk = q[:, LANES * (h // 2):LANES * (h // 2 + 1)]
            keep = upper if h % 2 == 1 else jnp.logical_not(upper)
            qm.append(jnp.where(keep, blk, 0.0))
            sk.append(jnp.full((1, L), sinks_ref[h], F32))
        qs = jnp.concatenate(qm, axis=0).astype(BF16)
        sink = jnp.concatenate(sk, axis=1)
        s_t = _dot_nt(ks[var], qs) + tab_ref[var]
        s_prev = jnp.where(prev_invalid, -jnp.inf, s_t[:L])
        s_t = jnp.concatenate([s_prev, s_t[L:]], axis=0)
        m = jnp.maximum(jnp.max(s_t, axis=0, keepdims=True), sink)
        p = jnp.exp(s_t - m)
        norm = 1.0 / (jnp.sum(p, axis=0, keepdims=True) + jnp.exp(sink - m))
        o_t = _dot(vts[var], p.astype(BF16)) * norm
        for i, h in enumerate(heads):
            outs[h] = o_t[:, i * L:(i + 1) * L].T
    blocks = [jnp.where(upper, outs[2 * j + 1], outs[2 * j]) for j in range(SWA_HEADS // 2)]
    return jnp.concatenate(blocks, axis=1)


def _rotary_tables(pos, inv_ref):
    lane = lax.broadcasted_iota(jnp.int32, (1, LANES), 1)
    ang = pos * inv_ref[...]
    sin = jnp.sin(ang)
    return jnp.cos(ang), jnp.where(lane < HEAD_D // 2, -sin, sin)


def _rotate(x, cos, sin_signed):
    return x * cos + pltpu.roll(x, HEAD_D // 2, 1) * sin_signed


def _gate_slab(raw, gb_ref):
    lane = lax.broadcasted_iota(jnp.int32, (1, LANES), 1)
    x = raw + gb_ref[...]
    return jnp.where((lane >= N_HEADS) & (lane < 2 * N_HEADS), _log_sigmoid(x), x)


def _head_rms(h, gain):
    parts = []
    for i in range(N_HEADS):
        blk = h[:, HEAD_D * i:HEAD_D * (i + 1)]
        parts.append(blk * lax.rsqrt(jnp.mean(blk * blk, axis=-1, keepdims=True) + EPS))
    return jnp.concatenate(parts, axis=-1) * gain


def _rms_over_rows(h_t):
    return h_t * lax.rsqrt(jnp.mean(h_t * h_t, axis=0, keepdims=True) + EPS)


def _post_math(hml, osw, ort, mlo, rtg, g0, g1, g2, x, mlg, retg, ng, wml, wsw, wrt, wout, head_normed=False):
    if not head_normed:
        hml = _head_rms(hml, mlg)
        ort = _head_rms(ort, retg)
    hm = hml * _sigmoid(mlo)
    y_ml = _dot(hm.astype(BF16), wml)
    y_sw = _dot(osw.astype(BF16), wsw)
    rt = ort * (rtg * _sigmoid(rtg))
    y_rt = _dot(rt.astype(BF16), wrt)
    merged = _sigmoid(g0) * y_ml + _sigmoid(g1) * y_sw + _sigmoid(g2) * y_rt
    return x + _rms(_dot(merged.astype(BF16), wout), ng)


def _mixer_seq_prompt_kernel(mq_ref, mk_ref, mv_ref, swq_ref, misc_ref, rq_ref, rk_ref, rv_ref,
                             mlo_ref, rtg_ref, g0_ref, g1_ref, g2_ref, x_ref,
                             gb_ref, inv_ref, sinks_ref, rb_ref,
                             mlg_ref, retg_ref, ng_ref, wml_ref, wsw_ref, wrt_ref, wout_ref,
                             x1_ref, c_ref, n_ref, m_ref, s_ref,
                             st_scr, k_same, k_swap, vt_buf, tab_t, cos_l, sin_l, dec_in_t, eye_ref,
                             hml_s, osw_s, ort_s):
    c = pl.program_id(0)
    active = c < pl.num_programs(0) - 1
    L = CHUNK
    lane = lax.broadcasted_iota(jnp.int32, (1, LANES), 1)
    row = lax.broadcasted_iota(jnp.int32, (L, L), 0)
    col = lax.broadcasted_iota(jnp.int32, (L, L), 1)
    causal_t = row <= col

    @pl.when(c == 0)
    def _init():
        c_ref[...] = jnp.zeros(c_ref.shape, F32)
        n_ref[...] = jnp.zeros(n_ref.shape, F32)
        m_ref[...] = jnp.zeros(m_ref.shape, F32)
        s_ref[...] = jnp.zeros(s_ref.shape, F32)
        hml_s[...] = jnp.zeros(hml_s.shape, F32)
        osw_s[...] = jnp.zeros(osw_s.shape, F32)
        ort_s[...] = jnp.zeros(ort_s.shape, F32)
        st_scr[...] = jnp.zeros(st_scr.shape, F32)
        k_same[...] = jnp.zeros(k_same.shape, BF16)
        k_swap[...] = jnp.zeros(k_swap.shape, BF16)
        vt_buf[...] = jnp.zeros(vt_buf.shape, BF16)
        _build_swa_table_t(tab_t, rb_ref)
        ang = row.astype(F32) * inv_ref[...]
        cos_l[...] = jnp.cos(ang)
        sin_l[...] = jnp.sin(ang)
        rel_t = (col - row).astype(F32)
        for h in range(N_HEADS):
            dec_in_t[h] = jnp.where(causal_t, jnp.exp(LOG_GAMMA[h] * rel_t), 0.0)
        eye_ref[...] = jnp.where(row == col, 1.0, 0.0).astype(BF16)

    eye = eye_ref[...]

    x1_ref[...] = _post_math(hml_s[...], osw_s[...], ort_s[...], mlo_ref[...], rtg_ref[...],
                             g0_ref[...], g1_ref[...], g2_ref[...], x_ref[...],
                             mlg_ref[...], retg_ref[...], ng_ref[...],
                             wml_ref[...], wsw_ref[...], wrt_ref[...], wout_ref[...], head_normed=True)

    lf = _gate_slab(misc_ref[:, MISC_IF:MISC_IF + LANES], gb_ref)
    bcum = jnp.dot((row >= col).astype(F32), lf, precision=lax.Precision.HIGHEST, preferred_element_type=F32)
    colslab = jnp.where(lane < N_HEADS, lf, bcum)
    rowslab = colslab.T
    m_all = m_ref[...]
    m_out = jnp.zeros((1, LANES), F32)
    for h in range(N_HEADS):
        hs = slice(HEAD_D * h, HEAD_D * (h + 1))
        g_c = colslab[:, h:h + 1] - colslab[:, N_HEADS + h:N_HEADS + h + 1]
        i_r = rowslab[h:h + 1, :]
        b_r = rowslab[N_HEADS + h:N_HEADS + h + 1, :]
        m_prev = m_all[:, h:h + 1]
        gm = jnp.where(causal_t, g_c, -jnp.inf)
        mx = jnp.maximum(jnp.max(gm, axis=0, keepdims=True), m_prev)
        m_t = b_r + mx
        inter = jnp.exp(m_prev - mx)
        dm_t = jnp.exp(gm - mx)
        q = mq_ref[:, hs]
        k = mk_ref[:, hs] * (HEAD_D ** -0.5)
        qb = q.astype(BF16)
        kb = k.astype(BF16)
        vt = _transpose_bf16(mv_ref[:, hs].astype(BF16), eye)
        s_t = _dot_nt(kb, qb) * dm_t
        c_old = c_ref[h]
        n_old = n_ref[h]
        num_t = _dot(vt, s_t.astype(BF16)) + _dot_nt(c_old.astype(BF16), qb) * inter
        nq = _dot_nt(jnp.broadcast_to(n_old, (MXU_MIN_ROWS, HEAD_D)).astype(BF16), qb)[0:1, :]
        den = jnp.sum(s_t, axis=0, keepdims=True) + inter * nq
        den = jnp.maximum(jnp.abs(den), jnp.exp(-m_t))
        hml_s[:, hs] = _rms_over_rows(num_t * (1.0 / den)).T * mlg_ref[:, hs]
        m_new = m_t[:, L - 1:L]
        b_last = b_r[:, L - 1:L]
        decay = jnp.exp(b_last + m_prev - m_new)
        w_r = jnp.exp((i_r - b_r) + (b_last - m_new))
        c_new = decay * c_old + _dot((vt.astype(F32) * w_r).astype(BF16), kb)
        n_new = decay * n_old + _dot(jnp.broadcast_to(w_r, (MXU_MIN_ROWS, L)).astype(BF16), kb)[0:1, :]
        c_ref[h] = jnp.where(active, c_new, c_old)
        n_ref[h] = jnp.where(active, n_new, n_old)
        m_out = jnp.where(lane == h, m_new, m_out)
    m_ref[...] = jnp.where(active, m_out, m_all)

    ang0 = (c * L).astype(F32) * inv_ref[...]
    cos0 = jnp.cos(ang0)
    sin0 = jnp.sin(ang0)
    cos_t = cos_l[...]
    sin_t = sin_l[...]
    cos = cos0 * cos_t - sin0 * sin_t
    sin = sin0 * cos_t + cos0 * sin_t
    sin_signed = jnp.where(lane < HEAD_D // 2, -sin, sin)
    l_row = lane.astype(F32)
    for h in range(N_HEADS):
        hs = slice(HEAD_D * h, HEAD_D * (h + 1))
        lg = LOG_GAMMA[h]
        qr = _rotate(rq_ref[:, hs], cos, sin_signed).astype(BF16)
        kr = (_rotate(rk_ref[:, hs], cos, sin_signed) * (HEAD_D ** -0.5)).astype(BF16)
        vt = _transpose_bf16(rv_ref[:, hs].astype(BF16), eye)
        s_t = _dot_nt(kr, qr) * dec_in_t[h]
        st_old = st_scr[h]
        o_t = _dot(vt, s_t.astype(BF16)) + _dot_nt(st_old.astype(BF16), qr) * jnp.exp(lg * (l_row + 1.0))
        ort_s[:, hs] = _rms_over_rows(o_t).T * retg_ref[:, hs]
        k_decay = jnp.exp(lg * (L - 1.0 - l_row))
        st_new = math.exp(lg * L) * st_old + _dot((vt.astype(F32) * k_decay).astype(BF16), kr)
        st_scr[h] = jnp.where(active, st_new, st_old)

    @pl.when(c == pl.num_programs(0) - 1)
    def _emit_state():
        for h in range(N_HEADS):
            s_ref[h] = st_scr[h].T

    k_new = misc_ref[:, MISC_K:MISC_K + LANES]
    k_same[0:L, :] = k_same[L:2 * L, :]
    k_swap[0:L, :] = k_swap[L:2 * L, :]
    k_same[L:2 * L, :] = k_new.astype(BF16)
    k_swap[L:2 * L, :] = pltpu.roll(k_new, SWA_D, 1).astype(BF16)
    vt_buf[:, 0:L] = vt_buf[:, L:2 * L]
    vt_buf[:, L:2 * L] = _transpose_bf16(misc_ref[:, MISC_V:MISC_V + LANES].astype(BF16), eye)
    osw_s[...] = _swa_block_t(swq_ref[...], k_same, k_swap, vt_buf, tab_t, sinks_ref, c == 0)


def mixer_prompt(z, x, gb, inv, sinks, rb, mlg, retg, ng, layer, wml, wsw, wrt, wout):
    t = z.shape[0]
    L = CHUNK
    n = t // L

    def zspec(cb):
        return pl.BlockSpec((L, 512), lambda c, cb=cb: (jnp.minimum(c, n - 1), cb))

    def tail(w, cb=0):
        return pl.BlockSpec((L, w), lambda c, cb=cb: (jnp.maximum(c - 1, 0), cb))

    def const(shape):
        return pl.BlockSpec(shape, lambda c: (0,) * len(shape))

    def weight(w):
        return _layer_weight(w, layer)

    smem = pl.BlockSpec(memory_space=pltpu.SMEM)
    return pl.pallas_call(
        _mixer_seq_prompt_kernel,
        grid=(n + 1,),
        in_specs=[zspec(ZC_ML_Q), zspec(ZC_ML_K), zspec(ZC_ML_V), zspec(ZC_SW_Q), zspec(ZC_MISC),
                  zspec(ZC_RT_Q), zspec(ZC_RT_K), zspec(ZC_RT_V),
                  tail(512, ZC_ML_O), tail(512, ZC_RT_G),
                  tail(1024, ZC_GATES), tail(1024, ZC_GATES + 1), tail(1024, ZC_GATES + 2), tail(D_MODEL),
                  const((1, LANES)), const((1, LANES)), smem, smem,
                  const((1, 512)), const((1, 512)), const((1, D_MODEL)),
                  weight(wml), weight(wsw), weight(wrt), weight(wout)],
        out_specs=[tail(D_MODEL),
                   const((N_HEADS, HEAD_D, HEAD_D)), const((N_HEADS, 1, HEAD_D)), const((1, LANES)),
                   const((N_HEADS, HEAD_D, HEAD_D))],
        out_shape=[jax.ShapeDtypeStruct((t, D_MODEL), F32),
                   jax.ShapeDtypeStruct((N_HEADS, HEAD_D, HEAD_D), F32),
                   jax.ShapeDtypeStruct((N_HEADS, 1, HEAD_D), F32),
                   jax.ShapeDtypeStruct((1, LANES), F32),
                   jax.ShapeDtypeStruct((N_HEADS, HEAD_D, HEAD_D), F32)],
        scratch_shapes=[pltpu.VMEM((N_HEADS, L, L), F32),
                        pltpu.VMEM((2 * L, LANES), BF16), pltpu.VMEM((2 * L, LANES), BF16),
                        pltpu.VMEM((LANES, 2 * L), BF16),
                        pltpu.VMEM((2, 2 * L, 4 * L), F32),
                        pltpu.VMEM((L, L), F32), pltpu.VMEM((L, L), F32),
                        pltpu.VMEM((N_HEADS, L, L), F32),
                        pltpu.VMEM((L, L), BF16),
                        pltpu.VMEM((L, 512), F32), pltpu.VMEM((L, 512), F32), pltpu.VMEM((L, 512), F32)],
        compiler_params=_params(1),
        name="mixer_prompt",
    )(z, z, z, z, z, z, z, z, z, z, z, z, z, x, gb, inv, sinks, rb, mlg, retg, ng, wml, wsw, wrt, wout)


def _shift_rows(x, d):
    return x if d == 0 else pltpu.roll(x, d, 0)


def _mixer_seq_sample_kernel(mq_ref, mk_ref, mv_ref, swq_ref, misc_ref, rq_ref, rk_ref, rv_ref,
                             c0_ref, n0_ref, m0_ref, s0_ref, ck_ref, cv_ref,
                             gb_ref, inv_ref, sinks_ref, rb_ref, *rest, group, first_layer):
    if not first_layer:
        rest = rest[2:]
    hml_ref, osw_ref, ort_ref, c_all, n_ref, m_ref, s_all, kbuf, vbuf, tab = rest
    R = SEQ_PAD
    if first_layer:
        c_ref, s_ref = c_all.at[0], s_all.at[0]
        c_all[1:] = jnp.zeros((c_all.shape[0] - 1,) + c_all.shape[1:], F32)
        s_all[1:] = jnp.zeros((s_all.shape[0] - 1,) + s_all.shape[1:], F32)
    else:
        c_ref, s_ref = c_all, s_all

    @pl.when(pl.program_id(0) == 0)
    def _init():
        kbuf[...] = jnp.zeros(kbuf.shape, F32)
        vbuf[...] = jnp.zeros(vbuf.shape, F32)
        _build_swa_table(tab, rb_ref, R)

    lane = lax.broadcasted_iota(jnp.int32, (1, LANES), 1)
    l_idx = lax.broadcasted_iota(jnp.int32, (R, 1), 0)
    real = l_idx < N_NEW
    l_f = l_idx.astype(F32)
    cos, sin_signed = _rotary_tables((PAST_LEN + l_idx).astype(F32), inv_ref)

    def body(g, carry):
        rows = pl.ds(pl.multiple_of(g * R, R), R)

        lf = _gate_slab(misc_ref[rows, MISC_IF:MISC_IF + LANES], gb_ref)
        bsum = lf
        for d in range(1, N_NEW):
            bsum = bsum + jnp.where(l_idx >= d, _shift_rows(lf, d), 0.0)
        slab = jnp.where(lane < N_HEADS, lf, bsum)
        shifted = [_shift_rows(slab, d) for d in range(N_NEW)]
        m0 = m0_ref[g]
        n0 = n0_ref[g]
        m_out = jnp.zeros((1, LANES), F32)
        for h in range(N_HEADS):
            hs = slice(HEAD_D * h, HEAD_D * (h + 1))
            i_c = slab[:, h:h + 1]
            b_c = slab[:, N_HEADS + h:N_HEADS + h + 1]
            m_prev = m0[:, h:h + 1]
            logd = []
            for d in range(N_NEW):
                sd = shifted[d]
                ld = b_c - sd[:, N_HEADS + h:N_HEADS + h + 1] + sd[:, h:h + 1]
                logd.append(jnp.where(l_idx >= d, ld, -jnp.inf))
            log_inter = b_c + m_prev
            m_t = log_inter
            for d in range(N_NEW):
                m_t = jnp.maximum(m_t, logd[d])
            inter = jnp.exp(log_inter - m_t)
            q = mq_ref[rows, hs]
            k = mk_ref[rows, hs] * (HEAD_D ** -0.5)
            v = mv_ref[rows, hs]
            c_old = c0_ref[g, h]
            n_old = n0[:, hs]
            num = _dot_nt(q.astype(BF16), c_old.astype(BF16)) * inter
            den = inter * jnp.sum(q * n_old, axis=1, keepdims=True)
            for d in range(N_NEW):
                s_d = jnp.sum(q * _shift_rows(k, d), axis=1, keepdims=True) * jnp.exp(logd[d] - m_t)
                num = num + s_d * _shift_rows(v, d)
                den = den + s_d
            den = jnp.maximum(jnp.abs(den), jnp.exp(-m_t))
            hml_ref[rows, hs] = num / den
            m_new = m_t[N_NEW - 1:N_NEW, :]
            b_last = b_c[N_NEW - 1:N_NEW, :]
            decay = jnp.exp(b_last + m_prev - m_new)
            w = jnp.where(real, jnp.exp(b_last - b_c + i_c - m_new), 0.0)
            c_ref[g, h] = decay * c_old + _dot_tn((v * w).astype(BF16), k.astype(BF16))
            n_ref[g, :, hs] = decay * n_old + jnp.sum(k * w, axis=0, keepdims=True)
            m_out = jnp.where(lane == h, m_new, m_out)
        m_ref[g] = m_out

        for h in range(N_HEADS):
            hs = slice(HEAD_D * h, HEAD_D * (h + 1))
            lg = LOG_GAMMA[h]
            qr = _rotate(rq_ref[rows, hs], cos, sin_signed)
            kr = _rotate(rk_ref[rows, hs], cos, sin_signed) * (HEAD_D ** -0.5)
            v = rv_ref[rows, hs]
            s_old = s0_ref[g, h]
            o = _dot(qr.astype(BF16), s_old.astype(BF16)) * jnp.exp(lg * (l_f + 1.0))
            for d in range(N_NEW):
                s_d = jnp.sum(qr * _shift_rows(kr, d), axis=1, keepdims=True) * math.exp(lg * d)
                o = o + jnp.where(l_idx >= d, s_d, 0.0) * _shift_rows(v, d)
            ort_ref[rows, hs] = o
            k_decay = jnp.where(real, jnp.exp(lg * (N_NEW - 1.0 - l_f)), 0.0)
            s_ref[g, h] = math.exp(lg * N_NEW) * s_old + _dot_tn((kr * k_decay).astype(BF16), v.astype(BF16))

        kbuf[0:CHUNK, :] = ck_ref[g]
        vbuf[0:CHUNK, :] = cv_ref[g]
        kbuf[CHUNK:CHUNK + R, :] = misc_ref[rows, MISC_K:MISC_K + LANES]
        vbuf[CHUNK:CHUNK + R, :] = misc_ref[rows, MISC_V:MISC_V + LANES]
        osw_ref[rows, :] = _swa_block(swq_ref[rows, :], kbuf, vbuf, tab, sinks_ref, None)
        return carry

    lax.fori_loop(0, group, body, 0, unroll=SAMPLE_UNROLL)


SAMPLE_UNROLL = 2


def mixer_seq_sample(z, layer, c0_all, n0, m0, s0_all, ck, cv, gb, inv, sinks, rb, c_prev, s_prev, group=8):
    depth, nb = c0_all.shape[:2]
    R = SEQ_PAD
    first_layer = c_prev is None
    assert first_layer == (layer == 0)

    def zspec(cb):
        return pl.BlockSpec((group * R, 512), lambda i, cb=cb: (i, cb))

    def const(shape):
        return pl.BlockSpec(shape, lambda i: (0,) * len(shape))

    def per_seq(shape):
        return pl.BlockSpec((group,) + shape, lambda i: (i,) + (0,) * len(shape))

    smem = pl.BlockSpec(memory_space=pltpu.SMEM)
    tok = pl.BlockSpec((group * R, 512), lambda i: (i, 0))
    st = (N_HEADS, HEAD_D, HEAD_D)
    st_in = pl.BlockSpec((None, group) + st, lambda i: (layer, i, 0, 0, 0))
    if first_layer:
        st_out = pl.BlockSpec((depth, group) + st, lambda i: (0, i, 0, 0, 0))
        extra_specs, extra_args, aliases = [], [], {}
    else:
        st_out = st_in
        extra_specs = [pl.BlockSpec(memory_space=pl.ANY)] * 2
        extra_args = [c_prev, s_prev]
        aliases = {18: 3, 19: 6}
    return pl.pallas_call(
        functools.partial(_mixer_seq_sample_kernel, group=group, first_layer=first_layer),
        grid=(nb // group,),
        in_specs=[zspec(ZC_ML_Q), zspec(ZC_ML_K), zspec(ZC_ML_V), zspec(ZC_SW_Q), zspec(ZC_MISC),
                  zspec(ZC_RT_Q), zspec(ZC_RT_K), zspec(ZC_RT_V),
                  st_in, per_seq((1, 512)), per_seq((1, LANES)), st_in,
                  per_seq((CHUNK, LANES)), per_seq((CHUNK, LANES)),
                  const((1, LANES)), const((1, LANES)), smem, smem] + extra_specs,
        out_specs=[tok, tok, tok, st_out, per_seq((1, 512)), per_seq((1, LANES)), st_out],
        out_shape=[jax.ShapeDtypeStruct((nb * R, 512), F32)] * 3
        + [jax.ShapeDtypeStruct((depth, nb) + st, F32),
           jax.ShapeDtypeStruct((nb, 1, 512), F32),
           jax.ShapeDtypeStruct((nb, 1, LANES), F32),
           jax.ShapeDtypeStruct((depth, nb) + st, F32)],
        scratch_shapes=[pltpu.VMEM((2 * CHUNK, LANES), F32), pltpu.VMEM((2 * CHUNK, LANES), F32),
                        pltpu.VMEM((2, 4 * R, 2 * CHUNK), F32)],
        input_output_aliases=aliases,
        compiler_params=_params(1),
        name="mixer_seq_sample",
    )(z, z, z, z, z, z, z, z, c0_all, n0, m0, s0_all, ck, cv, gb, inv, sinks, rb, *extra_args)


def _mixer_post_kernel(hml_ref, osw_ref, ort_ref, mlo_ref, rtg_ref, g0_ref, g1_ref, g2_ref, x_ref,
                       mlg_ref, retg_ref, ng_ref, wml_ref, wsw_ref, wrt_ref, wout_ref, o_ref):
    o_ref[...] = _post_math(hml_ref[...], osw_ref[...], ort_ref[...], mlo_ref[...], rtg_ref[...],
                            g0_ref[...], g1_ref[...], g2_ref[...], x_ref[...],
                            mlg_ref[...], retg_ref[...], ng_ref[...],
                            wml_ref[...], wsw_ref[...], wrt_ref[...], wout_ref[...])


def mixer_post(hml, osw, ort, z, x, mlg, retg, ng, layer, wml, wsw, wrt, wout, tm, name):
    m = x.shape[0]

    def tok(w, cb=0):
        return pl.BlockSpec((tm, w), lambda i, cb=cb: (i, cb))

    def const(shape):
        return pl.BlockSpec(shape, lambda i: (0,) * len(shape))

    return pl.pallas_call(
        _mixer_post_kernel,
        grid=(m // tm,),
        in_specs=[tok(512), tok(512), tok(512), tok(512, ZC_ML_O), tok(512, ZC_RT_G),
                  tok(1024, ZC_GATES), tok(1024, ZC_GATES + 1), tok(1024, ZC_GATES + 2), tok(D_MODEL),
                  const((1, 512)), const((1, 512)), const((1, D_MODEL)),
                  _layer_weight(wml, layer), _layer_weight(wsw, layer), _layer_weight(wrt, layer),
                  _layer_weight(wout, layer)],
        out_specs=tok(D_MODEL),
        out_shape=jax.ShapeDtypeStruct((m, D_MODEL), F32),
        compiler_params=_params(1),
        name=name,
    )(hml, osw, ort, z, z, z, z, z, x, mlg, retg, ng, wml, wsw, wrt, wout)


def _xattn_heads(q, k, v):
    lane = lax.broadcasted_iota(jnp.int32, (1, X_HEADS * X_D), 1)
    acc = jnp.zeros(q.shape, F32)
    for h in range(X_HEADS):
        mask = (lane >= X_D * h) & (lane < X_D * (h + 1))
        s = _dot_nt(jnp.where(mask, q, 0.0).astype(BF16), k) * (X_D ** -0.5)
        m = jnp.max(s, axis=1, keepdims=True)
        p = jnp.exp(s - m)
        p = p / jnp.sum(p, axis=1, keepdims=True)
        acc = acc + jnp.where(mask, _dot(p.astype(BF16), v), 0.0)
    return acc


def _xattn_prompt_kernel(x_ref, gin_ref, gout_ref, wcq_ref, kv_ref, wco_ref, o_ref):
    x = x_ref[...]
    q = _dot(_rms(x, gin_ref[...]).astype(BF16), wcq_ref[...])
    k = kv_ref[:, 0:X_HEADS * X_D].astype(BF16)
    v = kv_ref[:, X_HEADS * X_D:2 * X_HEADS * X_D].astype(BF16)
    o = _xattn_heads(q, k, v)
    o_ref[...] = x + _rms(_dot(o.astype(BF16), wco_ref[...]), gout_ref[...])


def xattn_prompt(x, gin, gout, layer, wcq, kv, wco, tm):
    m = x.shape[0]

    def const(shape):
        return pl.BlockSpec(shape, lambda i: (0,) * len(shape))

    tok = pl.BlockSpec((tm, D_MODEL), lambda i: (i, 0))
    return pl.pallas_call(
        _xattn_prompt_kernel,
        grid=(m // tm,),
        in_specs=[tok, const((1, D_MODEL)), const((1, D_MODEL)), _layer_weight(wcq, layer), const(kv.shape),
                  _layer_weight(wco, layer)],
        out_specs=tok,
        out_shape=jax.ShapeDtypeStruct((m, D_MODEL), F32),
        compiler_params=_params(1),
        name="xattn_prompt",
    )(x, gin, gout, wcq, kv, wco)


def _xattn_sample_kernel(x_ref, gin_ref, gout_ref, wcq_ref, kt_ref, vt_ref, wco_ref, o_ref, q_scr, a_scr,
                         *, group):
    R = SEQ_PAD
    x = x_ref[...]
    q_scr[...] = _dot(_rms(x, gin_ref[...]).astype(BF16), wcq_ref[...])
    lane = lax.broadcasted_iota(jnp.int32, (1, X_HEADS * X_D), 1)
    masks = [(lane >= X_D * h) & (lane < X_D * (h + 1)) for h in range(X_HEADS)]

    def body(g, carry):
        rows = pl.ds(pl.multiple_of(g * R, R), R)
        q = q_scr[rows, :]
        qs = jnp.concatenate([jnp.where(mk, q, 0.0) for mk in masks], axis=0)
        s = _dot(qs.astype(BF16), kt_ref[g].astype(BF16)) * (X_D ** -0.5)
        m = jnp.max(s, axis=1, keepdims=True)
        p = jnp.exp(s - m)
        p = p / jnp.sum(p, axis=1, keepdims=True)
        o = _dot_nt(p.astype(BF16), vt_ref[g].astype(BF16))
        acc = jnp.zeros((R, X_HEADS * X_D), F32)
        for h in range(X_HEADS):
            acc = acc + jnp.where(masks[h], o[R * h:R * (h + 1)], 0.0)
        a_scr[rows, :] = acc
        return carry

    lax.fori_loop(0, group, body, 0, unroll=4)
    o_ref[...] = x + _rms(_dot(a_scr[...].astype(BF16), wco_ref[...]), gout_ref[...])


def xattn_sample(x, gin, gout, wcq, layer, mkt, mvt, wco, group=16):
    nb = mkt.shape[1]
    R = SEQ_PAD

    def const(shape):
        return pl.BlockSpec(shape, lambda i: (0,) * len(shape))

    tok = pl.BlockSpec((group * R, D_MODEL), lambda i: (i, 0))
    mem = pl.BlockSpec((None, group, X_HEADS * X_D, N_MEM), lambda i: (layer, i, 0, 0))
    return pl.pallas_call(
        functools.partial(_xattn_sample_kernel, group=group),
        grid=(nb // group,),
        in_specs=[tok, const((1, D_MODEL)), const((1, D_MODEL)), _layer_weight(wcq, layer), mem, mem,
                  _layer_weight(wco, layer)],
        out_specs=tok,
        out_shape=jax.ShapeDtypeStruct((nb * R, D_MODEL), F32),
        scratch_shapes=[pltpu.VMEM((group * R, X_HEADS * X_D), F32),
                        pltpu.VMEM((group * R, X_HEADS * X_D), F32)],
        compiler_params=_params(1),
        name="xattn_sample",
    )(x, gin, gout, wcq, mkt, mvt, wco)


FFN_SPLIT = 2


def _ffn_kernel(x_ref, gin_ref, gout_ref, wgu_ref, wd_ref, o_ref):
    x = x_ref[...]
    u = _rms(x, gin_ref[...]).astype(BF16)
    fc = D_FF // FFN_SPLIT
    acc = None
    for c in range(FFN_SPLIT):
        g = _dot(u, wgu_ref[:, fc * c:fc * (c + 1)])
        up = _dot(u, wgu_ref[:, D_FF + fc * c:D_FF + fc * (c + 1)])
        h = (g * _sigmoid(g) * up).astype(BF16)
        part = _dot(h, wd_ref[fc * c:fc * (c + 1), :])
        acc = part if acc is None else acc + part
    o_ref[...] = x + _rms(acc, gout_ref[...])


def ffn(x, gin, gout, layer, wgu, wd, tm, name):
    m = x.shape[0]

    def const(shape):
        return pl.BlockSpec(shape, lambda i: (0,) * len(shape))

    tok = pl.BlockSpec((tm, D_MODEL), lambda i: (i, 0))
    return pl.pallas_call(
        _ffn_kernel,
        grid=(m // tm,),
        in_specs=[tok, const((1, D_MODEL)), const((1, D_MODEL)),
                  _layer_weight(wgu, layer), _layer_weight(wd, layer)],
        out_specs=tok,
        out_shape=jax.ShapeDtypeStruct((m, D_MODEL), F32),
        compiler_params=_params(1),
        name=name,
    )(x, gin, gout, wgu, wd)


def _reorder_w_in_t(w):
    wt = jnp.swapaxes(w, 1, 2)
    sizes = (512, 512, 512, 4, 4, 512, 512, 128, 128, 512, 512, 512, 512, 3072)
    offs = np.concatenate([[0], np.cumsum(sizes)])
    (ml_q, ml_k, ml_v, ml_i, ml_f, ml_o, sw_q, sw_k, sw_v, rt_q, rt_k, rt_v, rt_g, gates) = [
        wt[:, int(offs[i]):int(offs[i + 1])] for i in range(len(sizes))]
    pad = jnp.zeros((w.shape[0], 512 - 128 - 128 - 8, w.shape[1]), w.dtype)
    out = jnp.concatenate([ml_q, ml_k, ml_v, ml_o, sw_q, sw_k, sw_v, ml_i, ml_f, pad,
                           rt_q, rt_k, rt_v, rt_g, gates], axis=1)
    assert out.shape[1] == Z_COLS
    return out.astype(BF16)


def _row(v):
    return v.reshape(1, -1).astype(F32)


def _decoder_layer(x, z_fn, mixer_fn, xattn_fn, ffn_fn):
    z = z_fn(x)
    x, states = mixer_fn(z, x)
    x = ffn_fn(xattn_fn(x))
    return x, z, states


def kernel(x_prompt, x_sample, mem_prompt, state_mlstm_C, state_mlstm_n, state_mlstm_m, state_ret_S,
           cache_swa_k, cache_swa_v, cache_mem_k, cache_mem_v, norm_g, w_in, ml_gate_bias, ml_head_g,
           ret_head_g, swa_sinks, rel_bias, w_br_ml, w_br_swa, w_br_ret, w_out, w_cq, w_mkv, w_co,
           w_gu, w_down):
    depth = w_in.shape[0]
    bp, t, d = x_prompt.shape
    assert bp == 1 and d == D_MODEL and t % 1024 == 0
    nb, n_new, _ = x_sample.shape
    assert n_new == N_NEW
    R = SEQ_PAD

    xp = x_prompt.reshape(t, d)
    xs = jnp.pad(x_sample, ((0, 0), (0, R - n_new), (0, 0))).reshape(nb * R, d)
    mem = mem_prompt.reshape(N_MEM, d)
    half = HEAD_D // 2
    inv = ROPE_BASE ** (-jnp.arange(half, dtype=F32) / half)
    inv = jnp.concatenate([inv, inv]).reshape(1, LANES)
    rb = rel_bias.astype(F32).reshape(-1)
    mem_kt = jnp.transpose(cache_mem_k, (0, 1, 3, 4, 2)).reshape(depth, nb, X_HEADS * X_D, N_MEM)
    mem_vt = jnp.transpose(cache_mem_v, (0, 1, 3, 4, 2)).reshape(depth, nb, X_HEADS * X_D, N_MEM)

    win_t = _reorder_w_in_t(w_in)
    wml, wsw, wrt, wout = (w.astype(BF16) for w in (w_br_ml, w_br_swa, w_br_ret, w_out))
    wcq, wmkv, wco, wgu, wd = (w.astype(BF16) for w in (w_cq, w_mkv, w_co, w_gu, w_down))

    outs_p = {k: [] for k in ("C", "n", "m", "S", "k", "v", "mk", "mv")}
    outs_s = {k: [] for k in ("n", "m", "k", "v")}
    sample_c = sample_s = None
    for l in range(depth):
        ng = [_row(norm_g[l, i]) for i in range(7)]
        mlg, retg = _row(ml_head_g[l]), _row(ret_head_g[l])
        gb = jnp.concatenate([ml_gate_bias[l, 0], ml_gate_bias[l, 1],
                              jnp.zeros((LANES - 2 * N_HEADS,), F32)]).reshape(1, LANES).astype(F32)
        sinks = swa_sinks[l].astype(F32)

        kv = norm_matmul(mem, ng[6], wmkv, l, tm=N_MEM, tn=2 * X_HEADS * X_D, name="memory_kv")

        def seq_p(z, x):
            x1, c_, n_, m_, s_ = mixer_prompt(z, x, gb, inv, sinks, rb, mlg, retg, ng[1], l, wml, wsw, wrt, wout)
            return x1, (c_, n_, m_, s_)

        xp, zp, (c_, n_, m_, s_) = _decoder_layer(
            xp,
            lambda x: norm_matmul(x, ng[0], win_t, l, tm=min(IN_PROJ_TM, t), tn=1024, name="in_proj_prompt",
                                  w_transposed=True),
            seq_p,
            lambda x: xattn_prompt(x, ng[2], ng[3], l, wcq, kv, wco, tm=512),
            lambda x: ffn(x, ng[4], ng[5], l, wgu, wd, tm=512, name="ffn_prompt"))
        outs_p["C"].append(c_.reshape(1, N_HEADS, HEAD_D, HEAD_D))
        outs_p["n"].append(n_.reshape(1, N_HEADS, HEAD_D))
        outs_p["m"].append(m_[:, :N_HEADS])
        outs_p["S"].append(s_.reshape(1, N_HEADS, HEAD_D, HEAD_D))
        misc_last = zp[t - CHUNK:, 512 * ZC_MISC:512 * ZC_MISC + 2 * LANES]
        outs_p["k"].append(misc_last[:, :LANES].reshape(1, CHUNK, 2, SWA_D))
        outs_p["v"].append(misc_last[:, LANES:].reshape(1, CHUNK, 2, SWA_D))
        outs_p["mk"].append(kv[:, :X_HEADS * X_D].reshape(1, N_MEM, X_HEADS, X_D))
        outs_p["mv"].append(kv[:, X_HEADS * X_D:].reshape(1, N_MEM, X_HEADS, X_D))

        n0 = state_mlstm_n[l].astype(F32).reshape(nb, 1, N_HEADS * HEAD_D)
        m0 = jnp.pad(state_mlstm_m[l].astype(F32), ((0, 0), (0, LANES - N_HEADS))).reshape(nb, 1, LANES)
        ck = cache_swa_k[l].reshape(nb, CHUNK, LANES)
        cv = cache_swa_v[l].reshape(nb, CHUNK, LANES)

        def seq_s(z, x):
            hml, osw, ort, c_, n_, m_, s_ = mixer_seq_sample(
                z, l, state_mlstm_C, n0, m0, state_ret_S, ck, cv, gb, inv, sinks, rb, sample_c, sample_s)
            x1 = mixer_post(hml, osw, ort, z, x, mlg, retg, ng[1], l, wml, wsw, wrt, wout,
                            tm=min(256, x.shape[0]), name="mixer_post_sample")
            return x1, (c_, n_, m_, s_)

        xs, zs, (c_, n_, m_, s_) = _decoder_layer(
            xs,
            lambda x: norm_matmul(x, ng[0], win_t, l, tm=min(1024, nb * R), tn=1024, name="in_proj_sample",
                                  w_transposed=True),
            seq_s,
            lambda x: xattn_sample(x, ng[2], ng[3], wcq, l, mem_kt, mem_vt, wco),
            lambda x: ffn(x, ng[4], ng[5], l, wgu, wd, tm=min(512, nb * R), name="ffn_sample"))
        sample_c, sample_s = c_, s_
        outs_s["n"].append(n_.reshape(nb, N_HEADS, HEAD_D))
        outs_s["m"].append(m_.reshape(nb, LANES)[:, :N_HEADS])
        misc_new = zs.reshape(nb, R, Z_COLS)[:, :n_new, 512 * ZC_MISC:512 * ZC_MISC + 2 * LANES]
        k_new = misc_new[:, :, :LANES].reshape(nb, n_new, 2, SWA_D)
        v_new = misc_new[:, :, LANES:].reshape(nb, n_new, 2, SWA_D)
        outs_s["k"].append(jnp.concatenate([cache_swa_k[l][:, n_new:], k_new], axis=1))
        outs_s["v"].append(jnp.concatenate([cache_swa_v[l][:, n_new:], v_new], axis=1))

    y_p = xp.reshape(1, t, d)
    y_s = xs.reshape(nb, R, d)[:, :n_new]
    st = lambda d_, k: jnp.stack(d_[k])
    return (y_p, y_s,
            st(outs_p, "C"), st(outs_p, "n"), st(outs_p, "m"), st(outs_p, "S"),
            st(outs_p, "k"), st(outs_p, "v"), st(outs_p, "mk"), st(outs_p, "mv"),
            sample_c, st(outs_s, "n"), st(outs_s, "m"), sample_s,
            st(outs_s, "k"), st(outs_s, "v"))
```

```python
import functools
import math

import numpy as np
import jax
import jax.numpy as jnp
from jax import lax
from jax.experimental import pallas as pl
from jax.experimental.pallas import tpu as pltpu

F32 = jnp.float32
BF16 = jnp.bfloat16

D_MODEL = 1024
EPS = 1e-6
PAST_LEN = 16384
HEAD_D = 128
N_HEADS = 4
CHUNK = 128
SWA_HEADS = 8
SWA_D = 64
N_BUCKETS = 32
MAX_DISTANCE = 128
ROPE_BASE = 10000.0
N_MEM = 256
X_HEADS = 4
X_D = 64
D_FF = 2816
SEQ_PAD = 8
N_NEW = 4
LANES = 128
IN_PROJ_TM = 2048
MXU_MIN_ROWS = 16
VMEM_LIMIT = 48 * 1024 * 1024

ZC_ML_Q, ZC_ML_K, ZC_ML_V, ZC_ML_O, ZC_SW_Q, ZC_MISC, ZC_RT_Q, ZC_RT_K, ZC_RT_V, ZC_RT_G = range(10)
ZC_GATES = 5
Z_COLS = 8192
MISC_K, MISC_V, MISC_IF = 0, 128, 256

LOG2E = math.log2(math.e)
LN2 = math.log(2.0)
LOG_GAMMA = tuple(float(v) for v in np.log1p(-np.exp2(-5.0 - np.arange(N_HEADS, dtype=np.float32))))

SWA_SAME = (0, 2, 5, 7)
SWA_SWAP = (1, 3, 4, 6)


def _t5_bucket_bounds():
    n = np.arange(CHUNK)
    max_exact = N_BUCKETS // 2
    nf = np.maximum(n, 1).astype(np.float32)
    large = max_exact + (np.log(nf / np.float32(max_exact)) / np.float32(math.log(MAX_DISTANCE / max_exact))
                         * np.float32(N_BUCKETS - max_exact)).astype(np.int32)
    large = np.minimum(large, N_BUCKETS - 1)
    b = np.where(n < max_exact, n, large)
    assert np.all(np.diff(b) >= 0)
    runs = []
    for d in range(CHUNK):
        if runs and runs[-1][1] == int(b[d]):
            runs[-1][0] = d + 1
        else:
            runs.append([d + 1, int(b[d])])
    return tuple((hi, bk) for hi, bk in runs)


BUCKET_RUNS = _t5_bucket_bounds()


def _dot(a, b):
    return jnp.dot(a, b, preferred_element_type=F32)


def _dot_nt(a, b):
    return lax.dot_general(a, b, (((1,), (1,)), ((), ())), preferred_element_type=F32)


def _dot_tn(a, b):
    return lax.dot_general(a, b, (((0,), (0,)), ((), ())), preferred_element_type=F32)


def _rms(x, g):
    return x * lax.rsqrt(jnp.mean(x * x, axis=-1, keepdims=True) + EPS) * g


def _sigmoid(x):
    return 1.0 / (1.0 + jnp.exp2(x * (-LOG2E)))


def _log_sigmoid(x):
    return jnp.minimum(x, 0.0) - jnp.log1p(jnp.exp(-jnp.abs(x)))


def _layer_weight(w_all, layer):
    shape = w_all.shape[1:]
    return pl.BlockSpec((None,) + shape, lambda *_: (layer,) + (0,) * len(shape), pipeline_mode=pl.Buffered(1))


def _params(n_grid):
    return pltpu.CompilerParams(dimension_semantics=("arbitrary",) * n_grid, vmem_limit_bytes=VMEM_LIMIT)


def _norm_matmul_kernel(x_ref, g_ref, w_ref, o_ref, u_ref, *, w_transposed):
    @pl.when(pl.program_id(1) == 0)
    def _():
        u_ref[...] = _rms(x_ref[...], g_ref[...]).astype(BF16)

    o_ref[...] = _dot_nt(u_ref[...], w_ref[...]) if w_transposed else _dot(u_ref[...], w_ref[...])


def norm_matmul(x, g, w_all, layer, tm, tn, name, w_transposed=False):
    m, k = x.shape
    n = w_all.shape[1] if w_transposed else w_all.shape[2]
    if w_transposed:
        w_spec = pl.BlockSpec((None, tn, k), lambda i, j: (layer, j, 0))
    else:
        w_spec = pl.BlockSpec((None, k, tn), lambda i, j: (layer, 0, j))
    return pl.pallas_call(
        functools.partial(_norm_matmul_kernel, w_transposed=w_transposed),
        grid=(m // tm, n // tn),
        in_specs=[pl.BlockSpec((tm, k), lambda i, j: (i, 0)),
                  pl.BlockSpec((1, k), lambda i, j: (0, 0)),
                  w_spec],
        out_specs=pl.BlockSpec((tm, tn), lambda i, j: (i, j)),
        out_shape=jax.ShapeDtypeStruct((m, n), F32),
        scratch_shapes=[pltpu.VMEM((tm, k), BF16)],
        compiler_params=_params(2),
        name=name,
    )(x, g, w_all)


def _transpose_bf16(x, eye):
    return _dot_nt(eye, x).astype(BF16)


def _build_swa_table_t(tab_ref, rb_ref):
    L = CHUNK
    srow = lax.broadcasted_iota(jnp.int32, (2 * L, L), 0)
    lcol = lax.broadcasted_iota(jnp.int32, (2 * L, L), 1)
    delta = lcol + L - srow
    valid = (delta >= 0) & (delta < L)
    for var, heads in enumerate((SWA_SAME, SWA_SWAP)):
        for i, h in enumerate(heads):
            val = jnp.full((2 * L, L), rb_ref[BUCKET_RUNS[-1][1] * SWA_HEADS + h], F32)
            for hi, bk in reversed(BUCKET_RUNS[:-1]):
                val = jnp.where(delta < hi, rb_ref[bk * SWA_HEADS + h], val)
            tab_ref[var, :, i * L:(i + 1) * L] = jnp.where(valid, val, -jnp.inf)


def _swa_block_t(q, k_same, k_swap, vt_buf, tab_ref, sinks_ref, prev_invalid):
    L = CHUNK
    lane = lax.broadcasted_iota(jnp.int32, (1, LANES), 1)
    upper = lane >= SWA_D
    q = q * (SWA_D ** -0.5)
    vt = vt_buf[...]
    vts = (vt, jnp.concatenate([vt[SWA_D:], vt[:SWA_D]], axis=0))
    ks = (k_same[...], k_swap[...])
    outs = [None] * SWA_HEADS
    for var, heads in enumerate((SWA_SAME, SWA_SWAP)):
        qm = []
        sk = []
        for h in heads:
            blk = q[:, LANES * (h // 2):LANES * (h // 2 + 1)]
            keep = upper if h % 2 == 1 else jnp.logical_not(upper)
            qm.append(jnp.where(keep, blk, 0.0))
            sk.append(jnp.full((1, L), sinks_ref[h], F32))
        qs = jnp.concatenate(qm, axis=0).astype(BF16)
        sink = jnp.concatenate(sk, axis=1)
        s_t = _dot_nt(ks[var], qs) + tab_ref[var]
        s_prev = jnp.where(prev_invalid, -jnp.inf, s_t[:L])
        s_t = jnp.concatenate([s_prev, s_t[L:]], axis=0)
        m = jnp.maximum(jnp.max(s_t, axis=0, keepdims=True), sink)
        p = jnp.exp(s_t - m)
        norm = 1.0 / (jnp.sum(p, axis=0, keepdims=True) + jnp.exp(sink - m))
        o_t = _dot(vts[var], p.astype(BF16)) * norm
        for i, h in enumerate(heads):
            outs[h] = o_t[:, i * L:(i + 1) * L].T
    blocks = [jnp.where(upper, outs[2 * j + 1], outs[2 * j]) for j in range(SWA_HEADS // 2)]
    return jnp.concatenate(blocks, axis=1)


def _rotary_tables(pos, inv_ref):
    lane = lax.broadcasted_iota(jnp.int32, (1, LANES), 1)
    ang = pos * inv_ref[...]
    sin = jnp.sin(ang)
    return jnp.cos(ang), jnp.where(lane < HEAD_D // 2, -sin, sin)


def _rotate(x, cos, sin_signed):
    return x * cos + pltpu.roll(x, HEAD_D // 2, 1) * sin_signed


def _gate_slab(raw, gb_ref):
    lane = lax.broadcasted_iota(jnp.int32, (1, LANES), 1)
    x = raw + gb_ref[...]
    return jnp.where((lane >= N_HEADS) & (lane < 2 * N_HEADS), _log_sigmoid(x), x)


def _head_rms(h, gain):
    parts = []
    for i in range(N_HEADS):
        blk = h[:, HEAD_D * i:HEAD_D * (i + 1)]
        parts.append(blk * lax.rsqrt(jnp.mean(blk * blk, axis=-1, keepdims=True) + EPS))
    return jnp.concatenate(parts, axis=-1) * gain


def _rms_over_rows(h_t):
    return h_t * lax.rsqrt(jnp.mean(h_t * h_t, axis=0, keepdims=True) + EPS)


def _post_math(hml, osw, ort, mlo, rtg, g0, g1, g2, x, mlg, retg, ng, wml, wsw, wrt, wout, head_normed=False):
    if not head_normed:
        hml = _head_rms(hml, mlg)
        ort = _head_rms(ort, retg)
    hm = hml * _sigmoid(mlo)
    y_ml = _dot(hm.astype(BF16), wml)
    y_sw = _dot(osw.astype(BF16), wsw)
    rt = ort * (rtg * _sigmoid(rtg))
    y_rt = _dot(rt.astype(BF16), wrt)
    merged = _sigmoid(g0) * y_ml + _sigmoid(g1) * y_sw + _sigmoid(g2) * y_rt
    return x + _rms(_dot(merged.astype(BF16), wout), ng)


def _mixer_seq_prompt_kernel(mq_ref, mk_ref, mv_ref, swq_ref, misc_ref, rq_ref, rk_ref, rv_ref,
                             mlo_ref, rtg_ref, g0_ref, g1_ref, g2_ref, x_ref,
                             gb_ref, inv_ref, sinks_ref, rb_ref,
                             mlg_ref, retg_ref, ng_ref, wml_ref, wsw_ref, wrt_ref, wout_ref,
                             x1_ref, c_ref, n_ref, m_ref, s_ref,
                             st_scr, k_same, k_swap, vt_buf, tab_t, cos_l, sin_l, dec_in_t, eye_ref,
                             hml_s, osw_s, ort_s):
    c = pl.program_id(0)
    active = c < pl.num_programs(0) - 1
    L = CHUNK
    lane = lax.broadcasted_iota(jnp.int32, (1, LANES), 1)
    row = lax.broadcasted_iota(jnp.int32, (L, L), 0)
    col = lax.broadcasted_iota(jnp.int32, (L, L), 1)
    causal_t = row <= col

    @pl.when(c == 0)
    def _init():
        c_ref[...] = jnp.zeros(c_ref.shape, F32)
        n_ref[...] = jnp.zeros(n_ref.shape, F32)
        m_ref[...] = jnp.zeros(m_ref.shape, F32)
        s_ref[...] = jnp.zeros(s_ref.shape, F32)
        hml_s[...] = jnp.zeros(hml_s.shape, F32)
        osw_s[...] = jnp.zeros(osw_s.shape, F32)
        ort_s[...] = jnp.zeros(ort_s.shape, F32)
        st_scr[...] = jnp.zeros(st_scr.shape, F32)
        k_same[...] = jnp.zeros(k_same.shape, BF16)
        k_swap[...] = jnp.zeros(k_swap.shape, BF16)
        vt_buf[...] = jnp.zeros(vt_buf.shape, BF16)
        _build_swa_table_t(tab_t, rb_ref)
        ang = row.astype(F32) * inv_ref[...]
        cos_l[...] = jnp.cos(ang)
        sin_l[...] = jnp.sin(ang)
        rel_t = (col - row).astype(F32)
        for h in range(N_HEADS):
            dec_in_t[h] = jnp.where(causal_t, jnp.exp(LOG_GAMMA[h] * rel_t), 0.0)
        eye_ref[...] = jnp.where(row == col, 1.0, 0.0).astype(BF16)

    eye = eye_ref[...]

    x1_ref[...] = _post_math(hml_s[...], osw_s[...], ort_s[...], mlo_ref[...], rtg_ref[...],
                             g0_ref[...], g1_ref[...], g2_ref[...], x_ref[...],
                             mlg_ref[...], retg_ref[...], ng_ref[...],
                             wml_ref[...], wsw_ref[...], wrt_ref[...], wout_ref[...], head_normed=True)

    lf = _gate_slab(misc_ref[:, MISC_IF:MISC_IF + LANES], gb_ref)
    bcum = jnp.dot((row >= col).astype(F32), lf, precision=lax.Precision.HIGHEST, preferred_element_type=F32)
    colslab = jnp.where(lane < N_HEADS, lf, bcum)
    rowslab = colslab.T
    m_all = m_ref[...]
    m_out = jnp.zeros((1, LANES), F32)
    for h in range(N_HEADS):
        hs = slice(HEAD_D * h, HEAD_D * (h + 1))
        g_c = colslab[:, h:h + 1] - colslab[:, N_HEADS + h:N_HEADS + h + 1]
        i_r = rowslab[h:h + 1, :]
        b_r = rowslab[N_HEADS + h:N_HEADS + h + 1, :]
        m_prev = m_all[:, h:h + 1]
        gm = jnp.where(causal_t, g_c, -jnp.inf)
        mx = jnp.maximum(jnp.max(gm, axis=0, keepdims=True), m_prev)
        m_t = b_r + mx
        inter = jnp.exp(m_prev - mx)
        dm_t = jnp.exp(gm - mx)
        q = mq_ref[:, hs]
        k = mk_ref[:, hs] * (HEAD_D ** -0.5)
        qb = q.astype(BF16)
        kb = k.astype(BF16)
        vt = _transpose_bf16(mv_ref[:, hs].astype(BF16), eye)
        s_t = _dot_nt(kb, qb) * dm_t
        c_old = c_ref[h]
        n_old = n_ref[h]
        num_t = _dot(vt, s_t.astype(BF16)) + _dot_nt(c_old.astype(BF16), qb) * inter
        nq = _dot_nt(jnp.broadcast_to(n_old, (MXU_MIN_ROWS, HEAD_D)).astype(BF16), qb)[0:1, :]
        den = jnp.sum(s_t, axis=0, keepdims=True) + inter * nq
        den = jnp.maximum(jnp.abs(den), jnp.exp(-m_t))
        hml_s[:, hs] = _rms_over_rows(num_t * (1.0 / den)).T * mlg_ref[:, hs]
        m_new = m_t[:, L - 1:L]
        b_last = b_r[:, L - 1:L]
        decay = jnp.exp(b_last + m_prev - m_new)
        w_r = jnp.exp((i_r - b_r) + (b_last - m_new))
        c_new = decay * c_old + _dot((vt.astype(F32) * w_r).astype(BF16), kb)
        n_new = decay * n_old + _dot(jnp.broadcast_to(w_r, (MXU_MIN_ROWS, L)).astype(BF16), kb)[0:1, :]
        c_ref[h] = jnp.where(active, c_new, c_old)
        n_ref[h] = jnp.where(active, n_new, n_old)
        m_out = jnp.where(lane == h, m_new, m_out)
    m_ref[...] = jnp.where(active, m_out, m_all)

    ang0 = (c * L).astype(F32) * inv_ref[...]
    cos0 = jnp.cos(ang0)
    sin0 = jnp.sin(ang0)
    cos_t = cos_l[...]
    sin_t = sin_l[...]
    cos = cos0 * cos_t - sin0 * sin_t
    sin = sin0 * cos_t + cos0 * sin_t
    sin_signed = jnp.where(lane < HEAD_D // 2, -sin, sin)
    l_row = lane.astype(F32)
    for h in range(N_HEADS):
        hs = slice(HEAD_D * h, HEAD_D * (h + 1))
        lg = LOG_GAMMA[h]
        qr = _rotate(rq_ref[:, hs], cos, sin_signed).astype(BF16)
        kr = (_rotate(rk_ref[:, hs], cos, sin_signed) * (HEAD_D ** -0.5)).astype(BF16)
        vt = _transpose_bf16(rv_ref[:, hs].astype(BF16), eye)
        s_t = _dot_nt(kr, qr) * dec_in_t[h]
        st_old = st_scr[h]
        o_t = _dot(vt, s_t.astype(BF16)) + _dot_nt(st_old.astype(BF16), qr) * jnp.exp(lg * (l_row + 1.0))
        ort_s[:, hs] = _rms_over_rows(o_t).T * retg_ref[:, hs]
        k_decay = jnp.exp(lg * (L - 1.0 - l_row))
        st_new = math.exp(lg * L) * st_old + _dot((vt.astype(F32) * k_decay).astype(BF16), kr)
        st_scr[h] = jnp.where(active, st_new, st_old)

    @pl.when(c == pl.num_programs(0) - 1)
    def _emit_state():
        for h in range(N_HEADS):
            s_ref[h] = st_scr[h].T

    k_new = misc_ref[:, MISC_K:MISC_K + LANES]
    k_same[0:L, :] = k_same[L:2 * L, :]
    k_swap[0:L, :] = k_swap[L:2 * L, :]
    k_same[L:2 * L, :] = k_new.astype(BF16)
    k_swap[L:2 * L, :] = pltpu.roll(k_new, SWA_D, 1).astype(BF16)
    vt_buf[:, 0:L] = vt_buf[:, L:2 * L]
    vt_buf[:, L:2 * L] = _transpose_bf16(misc_ref[:, MISC_V:MISC_V + LANES].astype(BF16), eye)
    osw_s[...] = _swa_block_t(swq_ref[...], k_same, k_swap, vt_buf, tab_t, sinks_ref, c == 0)


def mixer_prompt(z, x, gb, inv, sinks, rb, mlg, retg, ng, layer, wml, wsw, wrt, wout):
    t = z.shape[0]
    L = CHUNK
    n = t // L

    def zspec(cb):
        return pl.BlockSpec((L, 512), lambda c, cb=cb: (jnp.minimum(c, n - 1), cb))

    def tail(w, cb=0):
        return pl.BlockSpec((L, w), lambda c, cb=cb: (jnp.maximum(c - 1, 0), cb))

    def const(shape):
        return pl.BlockSpec(shape, lambda c: (0,) * len(shape))

    def weight(w):
        return _layer_weight(w, layer)

    smem = pl.BlockSpec(memory_space=pltpu.SMEM)
    return pl.pallas_call(
        _mixer_seq_prompt_kernel,
        grid=(n + 1,),
        in_specs=[zspec(ZC_ML_Q), zspec(ZC_ML_K), zspec(ZC_ML_V), zspec(ZC_SW_Q), zspec(ZC_MISC),
                  zspec(ZC_RT_Q), zspec(ZC_RT_K), zspec(ZC_RT_V),
                  tail(512, ZC_ML_O), tail(512, ZC_RT_G),
                  tail(1024, ZC_GATES), tail(1024, ZC_GATES + 1), tail(1024, ZC_GATES + 2), tail(D_MODEL),
                  const((1, LANES)), const((1, LANES)), smem, smem,
                  const((1, 512)), const((1, 512)), const((1, D_MODEL)),
                  weight(wml), weight(wsw), weight(wrt), weight(wout)],
        out_specs=[tail(D_MODEL),
                   const((N_HEADS, HEAD_D, HEAD_D)), const((N_HEADS, 1, HEAD_D)), const((1, LANES)),
                   const((N_HEADS, HEAD_D, HEAD_D))],
        out_shape=[jax.ShapeDtypeStruct((t, D_MODEL), F32),
                   jax.ShapeDtypeStruct((N_HEADS, HEAD_D, HEAD_D), F32),
                   jax.ShapeDtypeStruct((N_HEADS, 1, HEAD_D), F32),
                   jax.ShapeDtypeStruct((1, LANES), F32),
                   jax.ShapeDtypeStruct((N_HEADS, HEAD_D, HEAD_D), F32)],
        scratch_shapes=[pltpu.VMEM((N_HEADS, L, L), F32),
                        pltpu.VMEM((2 * L, LANES), BF16), pltpu.VMEM((2 * L, LANES), BF16),
                        pltpu.VMEM((LANES, 2 * L), BF16),
                        pltpu.VMEM((2, 2 * L, 4 * L), F32),
                        pltpu.VMEM((L, L), F32), pltpu.VMEM((L, L), F32),
                        pltpu.VMEM((N_HEADS, L, L), F32),
                        pltpu.VMEM((L, L), BF16),
                        pltpu.VMEM((L, 512), F32), pltpu.VMEM((L, 512), F32), pltpu.VMEM((L, 512), F32)],
        compiler_params=_params(1),
        name="mixer_prompt",
    )(z, z, z, z, z, z, z, z, z, z, z, z, z, x, gb, inv, sinks, rb, mlg, retg, ng, wml, wsw, wrt, wout)


SAMPLE_GROUP = 8


def _tile_bcast(x, l_idx, src):
    n_rows = x.shape[0]
    out = jnp.zeros_like(x)
    for j in range(SEQ_PAD):
        out = out + jnp.where(l_idx == j, pltpu.roll(x, (j - src) % n_rows, 0), 0.0)
    return out


def _tile_total(x, l_idx):
    for d in (1, 2, 4):
        x = x + jnp.where(l_idx >= d, pltpu.roll(x, d, 0), 0.0)
    return x


def _per_seq(fn, group):
    return jnp.concatenate([fn(g) for g in range(group)], axis=0)


def _build_sample_swa_tables(tab_prev, tab_cur, rb_ref, group):
    R = SEQ_PAD
    rows = group * SWA_HEADS * R
    row = lax.broadcasted_iota(jnp.int32, (rows, LANES), 0)
    col = lax.broadcasted_iota(jnp.int32, (rows, LANES), 1)
    l = row & (R - 1)
    h = (row >> 3) & (SWA_HEADS - 1)
    g = row >> 6
    delta = CHUNK + l - col
    prev = jnp.zeros((rows, LANES), F32)
    cur = jnp.zeros((rows, LANES), F32)
    dcur = l - (col & (R - 1))
    for hh in range(SWA_HEADS):
        val = jnp.full((rows, LANES), rb_ref[BUCKET_RUNS[-1][1] * SWA_HEADS + hh], F32)
        for hi, bk in reversed(BUCKET_RUNS[:-1]):
            val = jnp.where(delta < hi, rb_ref[bk * SWA_HEADS + hh], val)
        prev = jnp.where(h == hh, val, prev)
        valc = jnp.zeros((rows, LANES), F32)
        for d in range(R):
            valc = jnp.where(dcur == d, rb_ref[d * SWA_HEADS + hh], valc)
        cur = jnp.where(h == hh, valc, cur)
    tab_prev[...] = jnp.where((delta >= 0) & (delta < CHUNK), prev, -jnp.inf)
    same_seq = ((col >> 3) == g) & (col < group * R)
    tab_cur[...] = jnp.where(same_seq & (dcur >= 0), cur, -jnp.inf)


def _sample_mixers_kernel(mq_ref, mk_ref, mv_ref, swq_ref, misc_ref, rq_ref, rk_ref, rv_ref,
                          c0_ref, n0_ref, m0_ref, s0_ref, ckt_ref, cvt_ref,
                          gb_ref, inv_ref, sinks_ref, rb_ref, *rest, group, first_layer):
    n_layered = 4
    if not first_layer:
        rest = rest[n_layered:]
    hml_ref, osw_ref, ort_ref, n_ref, m_ref, c_all, s_all, kc_all, vc_all, tab_prev, tab_cur = rest
    layered = []
    for ref in (c_all, s_all, kc_all, vc_all):
        if first_layer:
            ref[1:] = jnp.zeros((ref.shape[0] - 1,) + ref.shape[1:], F32)
            layered.append(ref.at[0])
        else:
            layered.append(ref)
    c_ref, s_ref, kc_ref, vc_ref = layered
    R = SEQ_PAD
    NR = group * R

    @pl.when(pl.program_id(0) == 0)
    def _init():
        _build_sample_swa_tables(tab_prev, tab_cur, rb_ref, group)

    lane = lax.broadcasted_iota(jnp.int32, (1, LANES), 1)
    l_idx = lax.broadcasted_iota(jnp.int32, (NR, 1), 0) & (R - 1)
    real = l_idx < N_NEW
    l_f = l_idx.astype(F32)

    def shift(x, d):
        return x if d == 0 else pltpu.roll(x, d, 0)

    def col(slab, h):
        return slab[:, h:h + 1]

    lf = _gate_slab(misc_ref[:, MISC_IF:MISC_IF + LANES], gb_ref)
    bsum = lf
    for d in range(1, N_NEW):
        bsum = bsum + jnp.where(l_idx >= d, shift(lf, d), 0.0)
    b = pltpu.roll(bsum, LANES - N_HEADS, 1)
    gs = lf - b
    m0 = m0_ref[...]
    log_inter = b + m0
    logd = [jnp.where(l_idx >= d, b + shift(gs, d), -jnp.inf) for d in range(N_NEW)]
    m_t = log_inter
    for d in range(N_NEW):
        m_t = jnp.maximum(m_t, logd[d])
    inter = jnp.exp(log_inter - m_t)
    dm = [jnp.exp(logd[d] - m_t) for d in range(N_NEW)]
    emt = jnp.exp(-m_t)
    b_last = _tile_bcast(b, l_idx, N_NEW - 1)
    m_new = _tile_bcast(m_t, l_idx, N_NEW - 1)
    decay = jnp.exp(b_last + m0 - m_new)
    w = jnp.where(real, jnp.exp(b_last - b + lf - m_new), 0.0)
    m_ref[...] = m_t
    n0 = n0_ref[...]
    for h in range(N_HEADS):
        hs = slice(HEAD_D * h, HEAD_D * (h + 1))
        q = mq_ref[:, hs]
        k = mk_ref[:, hs] * (HEAD_D ** -0.5)
        v = mv_ref[:, hs]
        qb = q.astype(BF16)
        kb = k.astype(BF16)
        inter_c = col(inter, h)
        num = _per_seq(lambda g: _dot_nt(qb[R * g:R * (g + 1)], c0_ref[g, h].astype(BF16)), group) * inter_c
        den = inter_c * jnp.sum(q * n0[:, hs], axis=1, keepdims=True)
        for d in range(N_NEW):
            s_d = jnp.sum(q * shift(k, d), axis=1, keepdims=True) * col(dm[d], h)
            num = num + s_d * shift(v, d)
            den = den + s_d
        den = jnp.maximum(jnp.abs(den), col(emt, h))
        hml_ref[:, hs] = num / den
        w_c = col(w, h)
        dec_c = col(decay, h)
        vw = (v * w_c).astype(BF16)
        for g in range(group):
            rs = slice(R * g, R * (g + 1))
            c_ref[g, h] = dec_c[R * g:R * g + 1] * c0_ref[g, h] + _dot_tn(vw[rs], kb[rs])
        n_ref[:, hs] = dec_c * n0[:, hs] + _tile_total(k * w_c, l_idx)

    cos, sin_signed = _rotary_tables((PAST_LEN + l_idx).astype(F32), inv_ref)
    for h in range(N_HEADS):
        hs = slice(HEAD_D * h, HEAD_D * (h + 1))
        lg = LOG_GAMMA[h]
        qr = _rotate(rq_ref[:, hs], cos, sin_signed)
        kr = _rotate(rk_ref[:, hs], cos, sin_signed) * (HEAD_D ** -0.5)
        v = rv_ref[:, hs]
        qrb = qr.astype(BF16)
        vb = v.astype(BF16)
        o = _per_seq(lambda g: _dot(qrb[R * g:R * (g + 1)], s0_ref[g, h].astype(BF16)), group)
        o = o * jnp.exp(lg * (l_f + 1.0))
        for d in range(N_NEW):
            s_d = jnp.sum(qr * shift(kr, d), axis=1, keepdims=True) * math.exp(lg * d)
            o = o + jnp.where(l_idx >= d, s_d, 0.0) * shift(v, d)
        ort_ref[:, hs] = o
        kd = (kr * jnp.where(real, jnp.exp(lg * (N_NEW - 1.0 - l_f)), 0.0)).astype(BF16)
        for g in range(group):
            rs = slice(R * g, R * (g + 1))
            s_ref[g, h] = math.exp(lg * N_NEW) * s0_ref[g, h] + _dot_tn(kd[rs], vb[rs])

    upper = lane >= SWA_D
    q_all = swq_ref[...] * (SWA_D ** -0.5)
    q_heads = []
    for h in range(SWA_HEADS):
        blk = q_all[:, LANES * (h // 2):LANES * (h // 2 + 1)]
        qh = jnp.where(upper if h % 2 == 1 else jnp.logical_not(upper), blk, 0.0)
        if h % 2 != h // (SWA_HEADS // 2):
            qh = pltpu.roll(qh, SWA_D, 1)
        q_heads.append(qh)
    qs = jnp.concatenate([q_heads[h][R * g:R * (g + 1)] for g in range(group) for h in range(SWA_HEADS)],
                         axis=0).astype(BF16)
    hr = SWA_HEADS * R
    zero_rows = jnp.zeros((LANES - NR, LANES), F32)
    k_new = jnp.concatenate([misc_ref[:, MISC_K:MISC_K + LANES], zero_rows], axis=0)
    v_new = jnp.concatenate([misc_ref[:, MISC_V:MISC_V + LANES], zero_rows], axis=0)
    s_prev = jnp.concatenate([_dot(qs[hr * g:hr * (g + 1)], ckt_ref[g].astype(BF16)) for g in range(group)],
                             axis=0) + tab_prev[...]
    s_cur = _dot_nt(qs, k_new.astype(BF16)) + tab_cur[...]
    sink64 = jnp.concatenate([jnp.full((R, 1), sinks_ref[h], F32) for h in range(SWA_HEADS)], axis=0)
    sink = jnp.concatenate([sink64] * group, axis=0)
    m = jnp.maximum(jnp.maximum(jnp.max(s_prev, axis=1, keepdims=True), jnp.max(s_cur, axis=1, keepdims=True)),
                    sink)
    p_prev = jnp.exp(s_prev - m)
    p_cur = jnp.exp(s_cur - m)
    norm = 1.0 / (jnp.sum(p_prev, axis=1, keepdims=True) + jnp.sum(p_cur, axis=1, keepdims=True)
                  + jnp.exp(sink - m))
    pb = p_prev.astype(BF16)
    o = jnp.concatenate([_dot_nt(pb[hr * g:hr * (g + 1)], cvt_ref[g].astype(BF16)) for g in range(group)],
                        axis=0)
    o = (o + _dot(p_cur.astype(BF16), v_new.astype(BF16))) * norm
    for g in range(group):
        blocks = []
        for j in range(SWA_HEADS // 2):
            pair = []
            for h in (2 * j, 2 * j + 1):
                oh = o[hr * g + R * h:hr * g + R * (h + 1)]
                if h % 2 != h // (SWA_HEADS // 2):
                    oh = pltpu.roll(oh, SWA_D, 1)
                pair.append(oh)
            blocks.append(jnp.where(upper, pair[1], pair[0]))
        osw_ref[R * g:R * (g + 1), :] = jnp.concatenate(blocks, axis=1)

    k_new_t = k_new.T
    v_new_t = v_new.T
    for g in range(group):
        back = (LANES - R * g) % LANES
        for new_t, cache_ref, out_ref in ((k_new_t, ckt_ref, kc_ref), (v_new_t, cvt_ref, vc_ref)):
            merged = jnp.where(lane < N_NEW, pltpu.roll(new_t, back, 1) if back else new_t, cache_ref[g])
            out_ref[g] = pltpu.roll(merged, LANES - N_NEW, 1)


def mixer_sample(z, layer, c0_all, n0_rows, m0_rows, s0_all, ckt_all, cvt_all, gb, inv, sinks, rb, prev):
    depth, nb = c0_all.shape[:2]
    group = SAMPLE_GROUP
    R = SEQ_PAD
    NR = group * R
    first_layer = prev is None
    assert first_layer == (layer == 0)

    def zspec(cb):
        return pl.BlockSpec((NR, 512), lambda i, cb=cb: (i, cb))

    def const(shape):
        return pl.BlockSpec(shape, lambda i: (0,) * len(shape))

    def rows(w):
        return pl.BlockSpec((NR, w), lambda i: (i, 0))

    def layered_in(shape):
        return pl.BlockSpec((None, group) + shape, lambda i: (layer, i) + (0,) * len(shape))

    def layered_out(shape):
        if first_layer:
            return pl.BlockSpec((depth, group) + shape, lambda i: (0, i) + (0,) * len(shape))
        return layered_in(shape)

    smem = pl.BlockSpec(memory_space=pltpu.SMEM)
    st = (N_HEADS, HEAD_D, HEAD_D)
    buf = (LANES, CHUNK)
    n_in = 18
    if first_layer:
        extra_specs, extra_args, aliases = [], [], {}
    else:
        extra_specs = [pl.BlockSpec(memory_space=pl.ANY)] * 4
        extra_args = list(prev)
        aliases = {n_in + i: 5 + i for i in range(4)}
    return pl.pallas_call(
        functools.partial(_sample_mixers_kernel, group=group, first_layer=first_layer),
        grid=(nb // group,),
        in_specs=[zspec(ZC_ML_Q), zspec(ZC_ML_K), zspec(ZC_ML_V), zspec(ZC_SW_Q), zspec(ZC_MISC),
                  zspec(ZC_RT_Q), zspec(ZC_RT_K), zspec(ZC_RT_V),
                  layered_in(st), rows(512), rows(LANES), layered_in(st),
                  layered_in(buf), layered_in(buf),
                  const((1, LANES)), const((1, LANES)), smem, smem] + extra_specs,
        out_specs=[rows(512), rows(512), rows(512), rows(512), rows(LANES),
                   layered_out(st), layered_out(st), layered_out(buf), layered_out(buf)],
        out_shape=[jax.ShapeDtypeStruct((nb * R, 512), F32)] * 4
        + [jax.ShapeDtypeStruct((nb * R, LANES), F32),
           jax.ShapeDtypeStruct((depth, nb) + st, F32), jax.ShapeDtypeStruct((depth, nb) + st, F32),
           jax.ShapeDtypeStruct((depth, nb) + buf, F32), jax.ShapeDtypeStruct((depth, nb) + buf, F32)],
        scratch_shapes=[pltpu.VMEM((group * SWA_HEADS * R, LANES), F32),
                        pltpu.VMEM((group * SWA_HEADS * R, LANES), F32)],
        input_output_aliases=aliases,
        compiler_params=_params(1),
        name="mixer_sample",
    )(z, z, z, z, z, z, z, z, c0_all, n0_rows, m0_rows, s0_all, ckt_all, cvt_all, gb, inv, sinks, rb, *extra_args)


def _mixer_post_kernel(hml_ref, osw_ref, ort_ref, mlo_ref, rtg_ref, g0_ref, g1_ref, g2_ref, x_ref,
                       mlg_ref, retg_ref, ng_ref, wml_ref, wsw_ref, wrt_ref, wout_ref, o_ref):
    o_ref[...] = _post_math(hml_ref[...], osw_ref[...], ort_ref[...], mlo_ref[...], rtg_ref[...],
                            g0_ref[...], g1_ref[...], g2_ref[...], x_ref[...],
                            mlg_ref[...], retg_ref[...], ng_ref[...],
                            wml_ref[...], wsw_ref[...], wrt_ref[...], wout_ref[...])


def mixer_post(hml, osw, ort, z, x, mlg, retg, ng, layer, wml, wsw, wrt, wout, tm, name):
    m = x.shape[0]

    def tok(w, cb=0):
        return pl.BlockSpec((tm, w), lambda i, cb=cb: (i, cb))

    def const(shape):
        return pl.BlockSpec(shape, lambda i: (0,) * len(shape))

    return pl.pallas_call(
        _mixer_post_kernel,
        grid=(m // tm,),
        in_specs=[tok(512), tok(512), tok(512), tok(512, ZC_ML_O), tok(512, ZC_RT_G),
                  tok(1024, ZC_GATES), tok(1024, ZC_GATES + 1), tok(1024, ZC_GATES + 2), tok(D_MODEL),
                  const((1, 512)), const((1, 512)), const((1, D_MODEL)),
                  _layer_weight(wml, layer), _layer_weight(wsw, layer), _layer_weight(wrt, layer),
                  _layer_weight(wout, layer)],
        out_specs=tok(D_MODEL),
        out_shape=jax.ShapeDtypeStruct((m, D_MODEL), F32),
        compiler_params=_params(1),
        name=name,
    )(hml, osw, ort, z, z, z, z, z, x, mlg, retg, ng, wml, wsw, wrt, wout)


def _xattn_heads(q, k, v):
    lane = lax.broadcasted_iota(jnp.int32, (1, X_HEADS * X_D), 1)
    acc = jnp.zeros(q.shape, F32)
    for h in range(X_HEADS):
        mask = (lane >= X_D * h) & (lane < X_D * (h + 1))
        s = _dot_nt(jnp.where(mask, q, 0.0).astype(BF16), k) * (X_D ** -0.5)
        m = jnp.max(s, axis=1, keepdims=True)
        p = jnp.exp(s - m)
        p = p / jnp.sum(p, axis=1, keepdims=True)
        acc = acc + jnp.where(mask, _dot(p.astype(BF16), v), 0.0)
    return acc


def _xattn_prompt_kernel(x_ref, gin_ref, gout_ref, wcq_ref, kv_ref, wco_ref, o_ref):
    x = x_ref[...]
    q = _dot(_rms(x, gin_ref[...]).astype(BF16), wcq_ref[...])
    k = kv_ref[:, 0:X_HEADS * X_D].astype(BF16)
    v = kv_ref[:, X_HEADS * X_D:2 * X_HEADS * X_D].astype(BF16)
    o = _xattn_heads(q, k, v)
    o_ref[...] = x + _rms(_dot(o.astype(BF16), wco_ref[...]), gout_ref[...])


def xattn_prompt(x, gin, gout, layer, wcq, kv, wco, tm):
    m = x.shape[0]

    def const(shape):
        return pl.BlockSpec(shape, lambda i: (0,) * len(shape))

    tok = pl.BlockSpec((tm, D_MODEL), lambda i: (i, 0))
    return pl.pallas_call(
        _xattn_prompt_kernel,
        grid=(m // tm,),
        in_specs=[tok, const((1, D_MODEL)), const((1, D_MODEL)), _layer_weight(wcq, layer), const(kv.shape),
                  _layer_weight(wco, layer)],
        out_specs=tok,
        out_shape=jax.ShapeDtypeStruct((m, D_MODEL), F32),
        compiler_params=_params(1),
        name="xattn_prompt",
    )(x, gin, gout, wcq, kv, wco)


def _xattn_sample_kernel(x_ref, gin_ref, gout_ref, wcq_ref, kt_ref, vt_ref, wco_ref, o_ref, q_scr, a_scr,
                         *, group):
    R = SEQ_PAD
    x = x_ref[...]
    q_scr[...] = _dot(_rms(x, gin_ref[...]).astype(BF16), wcq_ref[...])
    lane = lax.broadcasted_iota(jnp.int32, (1, X_HEADS * X_D), 1)
    masks = [(lane >= X_D * h) & (lane < X_D * (h + 1)) for h in range(X_HEADS)]

    def body(g, carry):
        rows = pl.ds(pl.multiple_of(g * R, R), R)
        q = q_scr[rows, :]
        qs = jnp.concatenate([jnp.where(mk, q, 0.0) for mk in masks], axis=0)
        s = _dot(qs.astype(BF16), kt_ref[g].astype(BF16)) * (X_D ** -0.5)
        m = jnp.max(s, axis=1, keepdims=True)
        p = jnp.exp(s - m)
        p = p / jnp.sum(p, axis=1, keepdims=True)
        o = _dot_nt(p.astype(BF16), vt_ref[g].astype(BF16))
        acc = jnp.zeros((R, X_HEADS * X_D), F32)
        for h in range(X_HEADS):
            acc = acc + jnp.where(masks[h], o[R * h:R * (h + 1)], 0.0)
        a_scr[rows, :] = acc
        return carry

    lax.fori_loop(0, group, body, 0, unroll=4)
    o_ref[...] = x + _rms(_dot(a_scr[...].astype(BF16), wco_ref[...]), gout_ref[...])


def xattn_sample(x, gin, gout, wcq, layer, mkt, mvt, wco, group=16):
    nb = mkt.shape[1]
    R = SEQ_PAD

    def const(shape):
        return pl.BlockSpec(shape, lambda i: (0,) * len(shape))

    tok = pl.BlockSpec((group * R, D_MODEL), lambda i: (i, 0))
    mem = pl.BlockSpec((None, group, X_HEADS * X_D, N_MEM), lambda i: (layer, i, 0, 0))
    return pl.pallas_call(
        functools.partial(_xattn_sample_kernel, group=group),
        grid=(nb // group,),
        in_specs=[tok, const((1, D_MODEL)), const((1, D_MODEL)), _layer_weight(wcq, layer), mem, mem,
                  _layer_weight(wco, layer)],
        out_specs=tok,
        out_shape=jax.ShapeDtypeStruct((nb * R, D_MODEL), F32),
        scratch_shapes=[pltpu.VMEM((group * R, X_HEADS * X_D), F32),
                        pltpu.VMEM((group * R, X_HEADS * X_D), F32)],
        compiler_params=_params(1),
        name="xattn_sample",
    )(x, gin, gout, wcq, mkt, mvt, wco)


FFN_SPLIT = 2


def _ffn_kernel(x_ref, gin_ref, gout_ref, wgu_ref, wd_ref, o_ref):
    x = x_ref[...]
    u = _rms(x, gin_ref[...]).astype(BF16)
    fc = D_FF // FFN_SPLIT
    acc = None
    for c in range(FFN_SPLIT):
        g = _dot(u, wgu_ref[:, fc * c:fc * (c + 1)])
        up = _dot(u, wgu_ref[:, D_FF + fc * c:D_FF + fc * (c + 1)])
        h = (g * _sigmoid(g) * up).astype(BF16)
        part = _dot(h, wd_ref[fc * c:fc * (c + 1), :])
        acc = part if acc is None else acc + part
    o_ref[...] = x + _rms(acc, gout_ref[...])


def ffn(x, gin, gout, layer, wgu, wd, tm, name):
    m = x.shape[0]

    def const(shape):
        return pl.BlockSpec(shape, lambda i: (0,) * len(shape))

    tok = pl.BlockSpec((tm, D_MODEL), lambda i: (i, 0))
    return pl.pallas_call(
        _ffn_kernel,
        grid=(m // tm,),
        in_specs=[tok, const((1, D_MODEL)), const((1, D_MODEL)),
                  _layer_weight(wgu, layer), _layer_weight(wd, layer)],
        out_specs=tok,
        out_shape=jax.ShapeDtypeStruct((m, D_MODEL), F32),
        compiler_params=_params(1),
        name=name,
    )(x, gin, gout, wgu, wd)


def _reorder_w_in_t(w):
    wt = jnp.swapaxes(w, 1, 2)
    sizes = (512, 512, 512, 4, 4, 512, 512, 128, 128, 512, 512, 512, 512, 3072)
    offs = np.concatenate([[0], np.cumsum(sizes)])
    (ml_q, ml_k, ml_v, ml_i, ml_f, ml_o, sw_q, sw_k, sw_v, rt_q, rt_k, rt_v, rt_g, gates) = [
        wt[:, int(offs[i]):int(offs[i + 1])] for i in range(len(sizes))]
    pad = jnp.zeros((w.shape[0], 512 - 128 - 128 - 8, w.shape[1]), w.dtype)
    out = jnp.concatenate([ml_q, ml_k, ml_v, ml_o, sw_q, sw_k, sw_v, ml_i, ml_f, pad,
                           rt_q, rt_k, rt_v, rt_g, gates], axis=1)
    assert out.shape[1] == Z_COLS
    return out.astype(BF16)


def _row(v):
    return v.reshape(1, -1).astype(F32)


def _decoder_layer(x, z_fn, mixer_fn, xattn_fn, ffn_fn):
    z = z_fn(x)
    x, states = mixer_fn(z, x)
    x = ffn_fn(xattn_fn(x))
    return x, z, states


def kernel(x_prompt, x_sample, mem_prompt, state_mlstm_C, state_mlstm_n, state_mlstm_m, state_ret_S,
           cache_swa_k, cache_swa_v, cache_mem_k, cache_mem_v, norm_g, w_in, ml_gate_bias, ml_head_g,
           ret_head_g, swa_sinks, rel_bias, w_br_ml, w_br_swa, w_br_ret, w_out, w_cq, w_mkv, w_co,
           w_gu, w_down):
    depth = w_in.shape[0]
    bp, t, d = x_prompt.shape
    assert bp == 1 and d == D_MODEL and t % 1024 == 0
    nb, n_new, _ = x_sample.shape
    assert n_new == N_NEW
    R = SEQ_PAD

    xp = x_prompt.reshape(t, d)
    xs = jnp.pad(x_sample, ((0, 0), (0, R - n_new), (0, 0))).reshape(nb * R, d)
    mem = mem_prompt.reshape(N_MEM, d)
    half = HEAD_D // 2
    inv = ROPE_BASE ** (-jnp.arange(half, dtype=F32) / half)
    inv = jnp.concatenate([inv, inv]).reshape(1, LANES)
    rb = rel_bias.astype(F32).reshape(-1)
    mem_kt = jnp.transpose(cache_mem_k, (0, 1, 3, 4, 2)).reshape(depth, nb, X_HEADS * X_D, N_MEM)
    mem_vt = jnp.transpose(cache_mem_v, (0, 1, 3, 4, 2)).reshape(depth, nb, X_HEADS * X_D, N_MEM)

    win_t = _reorder_w_in_t(w_in)
    wml, wsw, wrt, wout = (w.astype(BF16) for w in (w_br_ml, w_br_swa, w_br_ret, w_out))
    wcq, wmkv, wco, wgu, wd = (w.astype(BF16) for w in (w_cq, w_mkv, w_co, w_gu, w_down))

    outs_p = {k: [] for k in ("C", "n", "m", "S", "k", "v", "mk", "mv")}
    outs_s = {k: [] for k in ("n", "m")}
    sample_layered = None
    swa_kt = jnp.transpose(cache_swa_k, (0, 1, 3, 4, 2)).reshape(depth, nb, LANES, CHUNK)
    swa_vt = jnp.transpose(cache_swa_v, (0, 1, 3, 4, 2)).reshape(depth, nb, LANES, CHUNK)
    for l in range(depth):
        ng = [_row(norm_g[l, i]) for i in range(7)]
        mlg, retg = _row(ml_head_g[l]), _row(ret_head_g[l])
        gb = jnp.concatenate([ml_gate_bias[l, 0], ml_gate_bias[l, 1],
                              jnp.zeros((LANES - 2 * N_HEADS,), F32)]).reshape(1, LANES).astype(F32)
        sinks = swa_sinks[l].astype(F32)

        kv = norm_matmul(mem, ng[6], wmkv, l, tm=N_MEM, tn=2 * X_HEADS * X_D, name="memory_kv")

        def seq_p(z, x):
            x1, c_, n_, m_, s_ = mixer_prompt(z, x, gb, inv, sinks, rb, mlg, retg, ng[1], l, wml, wsw, wrt, wout)
            return x1, (c_, n_, m_, s_)

        xp, zp, (c_, n_, m_, s_) = _decoder_layer(
            xp,
            lambda x: norm_matmul(x, ng[0], win_t, l, tm=min(IN_PROJ_TM, t), tn=1024, name="in_proj_prompt",
                                  w_transposed=True),
            seq_p,
            lambda x: xattn_prompt(x, ng[2], ng[3], l, wcq, kv, wco, tm=512),
            lambda x: ffn(x, ng[4], ng[5], l, wgu, wd, tm=512, name="ffn_prompt"))
        outs_p["C"].append(c_.reshape(1, N_HEADS, HEAD_D, HEAD_D))
        outs_p["n"].append(n_.reshape(1, N_HEADS, HEAD_D))
        outs_p["m"].append(m_[:, :N_HEADS])
        outs_p["S"].append(s_.reshape(1, N_HEADS, HEAD_D, HEAD_D))
        misc_last = zp[t - CHUNK:, 512 * ZC_MISC:512 * ZC_MISC + 2 * LANES]
        outs_p["k"].append(misc_last[:, :LANES].reshape(1, CHUNK, 2, SWA_D))
        outs_p["v"].append(misc_last[:, LANES:].reshape(1, CHUNK, 2, SWA_D))
        outs_p["mk"].append(kv[:, :X_HEADS * X_D].reshape(1, N_MEM, X_HEADS, X_D))
        outs_p["mv"].append(kv[:, X_HEADS * X_D:].reshape(1, N_MEM, X_HEADS, X_D))

        n0_rows = jnp.repeat(state_mlstm_n[l].astype(F32).reshape(nb, N_HEADS * HEAD_D), R, axis=0)
        m0_rows = jnp.repeat(jnp.pad(state_mlstm_m[l].astype(F32), ((0, 0), (0, LANES - N_HEADS))), R, axis=0)

        def seq_s(z, x):
            hml, osw, ort, n_, m_, *layered = mixer_sample(
                z, l, state_mlstm_C, n0_rows, m0_rows, state_ret_S, swa_kt, swa_vt, gb, inv, sinks, rb,
                sample_layered)
            x1 = mixer_post(hml, osw, ort, z, x, mlg, retg, ng[1], l, wml, wsw, wrt, wout,
                            tm=min(256, x.shape[0]), name="mixer_post_sample")
            return x1, (n_, m_, layered)

        xs, zs, (n_, m_, sample_layered) = _decoder_layer(
            xs,
            lambda x: norm_matmul(x, ng[0], win_t, l, tm=min(1024, nb * R), tn=1024, name="in_proj_sample",
                                  w_transposed=True),
            seq_s,
            lambda x: xattn_sample(x, ng[2], ng[3], wcq, l, mem_kt, mem_vt, wco),
            lambda x: ffn(x, ng[4], ng[5], l, wgu, wd, tm=min(512, nb * R), name="ffn_sample"))
        outs_s["n"].append(n_.reshape(nb, R, N_HEADS, HEAD_D)[:, R - 1])
        outs_s["m"].append(m_.reshape(nb, R, LANES)[:, n_new - 1, :N_HEADS])

    sample_c, sample_s, swa_kt_new, swa_vt_new = sample_layered
    s_swa_k = jnp.transpose(swa_kt_new.reshape(depth, nb, 2, SWA_D, CHUNK), (0, 1, 4, 2, 3))
    s_swa_v = jnp.transpose(swa_vt_new.reshape(depth, nb, 2, SWA_D, CHUNK), (0, 1, 4, 2, 3))
    y_p = xp.reshape(1, t, d)
    y_s = xs.reshape(nb, R, d)[:, :n_new]
    st = lambda d_, k: jnp.stack(d_[k])
    return (y_p, y_s,
            st(outs_p, "C"), st(outs_p, "n"), st(outs_p, "m"), st(outs_p, "S"),
            st(outs_p, "k"), st(outs_p, "v"), st(outs_p, "mk"), st(outs_p, "mv"),
            sample_c, st(outs_s, "n"), st(outs_s, "m"), sample_s, s_swa_k, s_swa_v)
```

```python
import functools
import math

import numpy as np
import jax
import jax.numpy as jnp
from jax import lax
from jax.experimental import pallas as pl
from jax.experimental.pallas import tpu as pltpu

F32 = jnp.float32
BF16 = jnp.bfloat16

D_MODEL = 1024
EPS = 1e-6
PAST_LEN = 16384
HEAD_D = 128
N_HEADS = 4
CHUNK = 128
SWA_HEADS = 8
SWA_D = 64
N_BUCKETS = 32
MAX_DISTANCE = 128
ROPE_BASE = 10000.0
N_MEM = 256
X_HEADS = 4
X_D = 64
D_FF = 2816
SEQ_PAD = 8
N_NEW = 4
LANES = 128
IN_PROJ_TM = 2048
MXU_MIN_ROWS = 16
VMEM_LIMIT = 48 * 1024 * 1024

ZC_ML_Q, ZC_ML_K, ZC_ML_V, ZC_ML_O, ZC_SW_Q, ZC_MISC, ZC_RT_Q, ZC_RT_K, ZC_RT_V, ZC_RT_G = range(10)
ZC_GATES = 5
Z_COLS = 8192
MISC_K, MISC_V, MISC_IF = 0, 128, 256

LOG2E = math.log2(math.e)
LN2 = math.log(2.0)
LOG_GAMMA = tuple(float(v) for v in np.log1p(-np.exp2(-5.0 - np.arange(N_HEADS, dtype=np.float32))))

SWA_SAME = (0, 2, 5, 7)
SWA_SWAP = (1, 3, 4, 6)


def _t5_bucket_bounds():
    n = np.arange(CHUNK)
    max_exact = N_BUCKETS // 2
    nf = np.maximum(n, 1).astype(np.float32)
    large = max_exact + (np.log(nf / np.float32(max_exact)) / np.float32(math.log(MAX_DISTANCE / max_exact))
                         * np.float32(N_BUCKETS - max_exact)).astype(np.int32)
    large = np.minimum(large, N_BUCKETS - 1)
    b = np.where(n < max_exact, n, large)
    assert np.all(np.diff(b) >= 0)
    runs = []
    for d in range(CHUNK):
        if runs and runs[-1][1] == int(b[d]):
            runs[-1][0] = d + 1
        else:
            runs.append([d + 1, int(b[d])])
    return tuple((hi, bk) for hi, bk in runs)


BUCKET_RUNS = _t5_bucket_bounds()


def _dot(a, b):
    return jnp.dot(a, b, preferred_element_type=F32)


def _dot_nt(a, b):
    return lax.dot_general(a, b, (((1,), (1,)), ((), ())), preferred_element_type=F32)


def _dot_tn(a, b):
    return lax.dot_general(a, b, (((0,), (0,)), ((), ())), preferred_element_type=F32)


def _rms(x, g):
    return x * lax.rsqrt(jnp.mean(x * x, axis=-1, keepdims=True) + EPS) * g


def _sigmoid(x):
    return 1.0 / (1.0 + jnp.exp2(x * (-LOG2E)))


def _log_sigmoid(x):
    return jnp.minimum(x, 0.0) - jnp.log1p(jnp.exp(-jnp.abs(x)))


def _layer_weight(w_all, layer):
    shape = w_all.shape[1:]
    return pl.BlockSpec((None,) + shape, lambda *_: (layer,) + (0,) * len(shape), pipeline_mode=pl.Buffered(1))


def _params(n_grid):
    return pltpu.CompilerParams(dimension_semantics=("arbitrary",) * n_grid, vmem_limit_bytes=VMEM_LIMIT)


def _norm_matmul_kernel(x_ref, g_ref, w_ref, o_ref, u_ref, *, w_transposed):
    @pl.when(pl.program_id(1) == 0)
    def _():
        u_ref[...] = _rms(x_ref[...], g_ref[...]).astype(BF16)

    o_ref[...] = _dot_nt(u_ref[...], w_ref[...]) if w_transposed else _dot(u_ref[...], w_ref[...])


def norm_matmul(x, g, w_all, layer, tm, tn, name, w_transposed=False):
    m, k = x.shape
    n = w_all.shape[1] if w_transposed else w_all.shape[2]
    if w_transposed:
        w_spec = pl.BlockSpec((None, tn, k), lambda i, j: (layer, j, 0))
    else:
        w_spec = pl.BlockSpec((None, k, tn), lambda i, j: (layer, 0, j))
    return pl.pallas_call(
        functools.partial(_norm_matmul_kernel, w_transposed=w_transposed),
        grid=(m // tm, n // tn),
        in_specs=[pl.BlockSpec((tm, k), lambda i, j: (i, 0)),
                  pl.BlockSpec((1, k), lambda i, j: (0, 0)),
                  w_spec],
        out_specs=pl.BlockSpec((tm, tn), lambda i, j: (i, j)),
        out_shape=jax.ShapeDtypeStruct((m, n), F32),
        scratch_shapes=[pltpu.VMEM((tm, k), BF16)],
        compiler_params=_params(2),
        name=name,
    )(x, g, w_all)


def _transpose_bf16(x, eye):
    return _dot_nt(eye, x).astype(BF16)


def _build_swa_table_t(tab_ref, rb_ref):
    L = CHUNK
    srow = lax.broadcasted_iota(jnp.int32, (2 * L, L), 0)
    lcol = lax.broadcasted_iota(jnp.int32, (2 * L, L), 1)
    delta = lcol + L - srow
    valid = (delta >= 0) & (delta < L)
    for var, heads in enumerate((SWA_SAME, SWA_SWAP)):
        for i, h in enumerate(heads):
            val = jnp.full((2 * L, L), rb_ref[BUCKET_RUNS[-1][1] * SWA_HEADS + h], F32)
            for hi, bk in reversed(BUCKET_RUNS[:-1]):
                val = jnp.where(delta < hi, rb_ref[bk * SWA_HEADS + h], val)
            tab_ref[var, :, i * L:(i + 1) * L] = jnp.where(valid, val, -jnp.inf)


def _swa_block_t(q, k_same, k_swap, vt_buf, tab_ref, sinks_ref, prev_invalid):
    L = CHUNK
    lane = lax.broadcasted_iota(jnp.int32, (1, LANES), 1)
    upper = lane >= SWA_D
    q = q * (SWA_D ** -0.5)
    vt = vt_buf[...]
    vts = (vt, jnp.concatenate([vt[SWA_D:], vt[:SWA_D]], axis=0))
    ks = (k_same[...], k_swap[...])
    outs = [None] * SWA_HEADS
    for var, heads in enumerate((SWA_SAME, SWA_SWAP)):
        qm = []
        sk = []
        for h in heads:
            blk = q[:, LANES * (h // 2):LANES * (h // 2 + 1)]
            keep = upper if h % 2 == 1 else jnp.logical_not(upper)
            qm.append(jnp.where(keep, blk, 0.0))
            sk.append(jnp.full((1, L), sinks_ref[h], F32))
        qs = jnp.concatenate(qm, axis=0).astype(BF16)
        sink = jnp.concatenate(sk, axis=1)
        s_t = _dot_nt(ks[var], qs) + tab_ref[var]
        s_prev = jnp.where(prev_invalid, -jnp.inf, s_t[:L])
        s_t = jnp.concatenate([s_prev, s_t[L:]], axis=0)
        m = jnp.maximum(jnp.max(s_t, axis=0, keepdims=True), sink)
        p = jnp.exp(s_t - m)
        norm = 1.0 / (jnp.sum(p, axis=0, keepdims=True) + jnp.exp(sink - m))
        o_t = _dot(vts[var], p.astype(BF16)) * norm
        for i, h in enumerate(heads):
            outs[h] = o_t[:, i * L:(i + 1) * L].T
    blocks = [jnp.where(upper, outs[2 * j + 1], outs[2 * j]) for j in range(SWA_HEADS // 2)]
    return jnp.concatenate(blocks, axis=1)


def _rotary_tables(pos, inv_ref):
    lane = lax.broadcasted_iota(jnp.int32, (1, LANES), 1)
    ang = pos * inv_ref[...]
    sin = jnp.sin(ang)
    return jnp.cos(ang), jnp.where(lane < HEAD_D // 2, -sin, sin)


def _rotate(x, cos, sin_signed):
    return x * cos + pltpu.roll(x, HEAD_D // 2, 1) * sin_signed


def _gate_slab(raw, gb_ref):
    lane = lax.broadcasted_iota(jnp.int32, (1, LANES), 1)
    x = raw + gb_ref[...]
    return jnp.where((lane >= N_HEADS) & (lane < 2 * N_HEADS), _log_sigmoid(x), x)


def _head_rms(h, gain):
    parts = []
    for i in range(N_HEADS):
        blk = h[:, HEAD_D * i:HEAD_D * (i + 1)]
        parts.append(blk * lax.rsqrt(jnp.mean(blk * blk, axis=-1, keepdims=True) + EPS))
    return jnp.concatenate(parts, axis=-1) * gain


def _rms_over_rows(h_t):
    return h_t * lax.rsqrt(jnp.mean(h_t * h_t, axis=0, keepdims=True) + EPS)


def _post_math(hml, osw, ort, mlo, rtg, g0, g1, g2, x, mlg, retg, ng, wml, wsw, wrt, wout, head_normed=False):
    if not head_normed:
        hml = _head_rms(hml, mlg)
        ort = _head_rms(ort, retg)
    hm = hml * _sigmoid(mlo)
    y_ml = _dot(hm.astype(BF16), wml)
    y_sw = _dot(osw.astype(BF16), wsw)
    rt = ort * (rtg * _sigmoid(rtg))
    y_rt = _dot(rt.astype(BF16), wrt)
    merged = _sigmoid(g0) * y_ml + _sigmoid(g1) * y_sw + _sigmoid(g2) * y_rt
    return x + _rms(_dot(merged.astype(BF16), wout), ng)


def _mixer_seq_prompt_kernel(mq_ref, mk_ref, mv_ref, swq_ref, misc_ref, rq_ref, rk_ref, rv_ref,
                             mlo_ref, rtg_ref, g0_ref, g1_ref, g2_ref, x_ref,
                             gbr_ref, inv_ref, sinks_ref, rb_ref,
                             mlg_ref, retg_ref, ng_ref, wml_ref, wsw_ref, wrt_ref, wout_ref,
                             x1_ref, c_ref, n_ref, m_ref, s_ref,
                             st_scr, k_same, k_swap, vt_buf, tab_t, cos_l, sin_l, dec_in_t, eye_ref,
                             hml_s, osw_s, ort_s):
    c = pl.program_id(0)
    active = c < pl.num_programs(0) - 1
    L = CHUNK
    lane = lax.broadcasted_iota(jnp.int32, (1, LANES), 1)
    row = lax.broadcasted_iota(jnp.int32, (L, L), 0)
    col = lax.broadcasted_iota(jnp.int32, (L, L), 1)
    causal_t = row <= col

    @pl.when(c == 0)
    def _init():
        c_ref[...] = jnp.zeros(c_ref.shape, F32)
        n_ref[...] = jnp.zeros(n_ref.shape, F32)
        m_ref[...] = jnp.zeros(m_ref.shape, F32)
        s_ref[...] = jnp.zeros(s_ref.shape, F32)
        hml_s[...] = jnp.zeros(hml_s.shape, F32)
        osw_s[...] = jnp.zeros(osw_s.shape, F32)
        ort_s[...] = jnp.zeros(ort_s.shape, F32)
        st_scr[...] = jnp.zeros(st_scr.shape, F32)
        k_same[...] = jnp.zeros(k_same.shape, BF16)
        k_swap[...] = jnp.zeros(k_swap.shape, BF16)
        vt_buf[...] = jnp.zeros(vt_buf.shape, BF16)
        _build_swa_table_t(tab_t, rb_ref)
        ang = row.astype(F32) * inv_ref[...]
        cos_l[...] = jnp.cos(ang)
        sin_l[...] = jnp.sin(ang)
        rel_t = (col - row).astype(F32)
        for h in range(N_HEADS):
            dec_in_t[h] = jnp.where(causal_t, jnp.exp(LOG_GAMMA[h] * rel_t), 0.0)
        eye_ref[...] = jnp.where(row == col, 1.0, 0.0).astype(BF16)

    eye = eye_ref[...]
    heads = range(N_HEADS)
    hsl = [slice(HEAD_D * h, HEAD_D * (h + 1)) for h in heads]
    sub8 = lax.broadcasted_iota(jnp.int32, (2 * N_HEADS, 1), 0)
    hml_prev, osw_prev, ort_prev = hml_s[...], osw_s[...], ort_s[...]

    raw = misc_ref[:, MISC_IF:MISC_IF + LANES].T[0:2 * N_HEADS] + gbr_ref[...]
    gates = jnp.where(sub8 >= N_HEADS, _log_sigmoid(raw), raw)
    cum = gates
    for sh in (1, 2, 4, 8, 16, 32, 64):
        cum = cum + jnp.where(lane >= sh, pltpu.roll(cum, sh, 1), 0.0)
    g8 = jnp.where(sub8 < N_HEADS, gates - pltpu.roll(cum, N_HEADS, 0), 0.0)
    g_cols = jnp.concatenate([g8, jnp.zeros((L - 2 * N_HEADS, L), F32)], axis=0).T

    ang0 = (c * L).astype(F32) * inv_ref[...]
    cos0 = jnp.cos(ang0)
    sin0 = jnp.sin(ang0)
    cos_t = cos_l[...]
    sin_t = sin_l[...]
    cos = cos0 * cos_t - sin0 * sin_t
    sin = sin0 * cos_t + cos0 * sin_t
    sin_signed = jnp.where(lane < HEAD_D // 2, -sin, sin)
    ml_ops, rt_ops = [], []
    for h in heads:
        qb = mq_ref[:, hsl[h]].astype(BF16)
        kb = (mk_ref[:, hsl[h]] * (HEAD_D ** -0.5)).astype(BF16)
        vt = _transpose_bf16(mv_ref[:, hsl[h]].astype(BF16), eye)
        ml_ops.append((qb, kb, vt, _dot_nt(kb, qb)))
    for h in heads:
        qr = _rotate(rq_ref[:, hsl[h]], cos, sin_signed).astype(BF16)
        kr = (_rotate(rk_ref[:, hsl[h]], cos, sin_signed) * (HEAD_D ** -0.5)).astype(BF16)
        vt = _transpose_bf16(rv_ref[:, hsl[h]].astype(BF16), eye)
        rt_ops.append((qr, kr, vt, _dot_nt(kr, qr)))
    k_new = misc_ref[:, MISC_K:MISC_K + LANES]
    k_same[0:L, :] = k_same[L:2 * L, :]
    k_swap[0:L, :] = k_swap[L:2 * L, :]
    k_same[L:2 * L, :] = k_new.astype(BF16)
    k_swap[L:2 * L, :] = pltpu.roll(k_new, SWA_D, 1).astype(BF16)
    vt_buf[:, 0:L] = vt_buf[:, L:2 * L]
    vt_buf[:, L:2 * L] = _transpose_bf16(misc_ref[:, MISC_V:MISC_V + LANES].astype(BF16), eye)

    y_ml = _dot((hml_prev * _sigmoid(mlo_ref[...])).astype(BF16), wml_ref[...])

    m_all = m_ref[...]
    gate_ops = []
    for h in heads:
        m_prev = m_all[:, h:h + 1]
        gm = jnp.where(causal_t, g_cols[:, h:h + 1], -jnp.inf)
        mx = jnp.maximum(jnp.max(gm, axis=0, keepdims=True), m_prev)
        gate_ops.append((m_prev, mx, jnp.exp(m_prev - mx), jnp.exp(gm - mx)))

    y_sw = _dot(osw_prev.astype(BF16), wsw_ref[...])
    rtg = rtg_ref[...]
    y_rt = _dot((ort_prev * (rtg * _sigmoid(rtg))).astype(BF16), wrt_ref[...])

    l_row = lane.astype(F32)
    ml_out, rt_out = [], []
    for h in heads:
        qb, kb, vt, qk = ml_ops[h]
        m_prev, mx, inter, dm_t = gate_ops[h]
        s_t = qk * dm_t
        num_t = _dot(vt, s_t.astype(BF16)) + _dot_nt(c_ref[h].astype(BF16), qb) * inter
        nq = _dot_nt(jnp.broadcast_to(n_ref[h], (MXU_MIN_ROWS, HEAD_D)).astype(BF16), qb)[0:1, :]
        den = jnp.sum(s_t, axis=0, keepdims=True) + inter * nq
        m_t = cum[N_HEADS + h:N_HEADS + h + 1, :] + mx
        den = jnp.maximum(jnp.abs(den), jnp.exp(-m_t))
        ml_out.append((num_t * (1.0 / den), m_t))
    for h in heads:
        qr, kr, vt, qk = rt_ops[h]
        s_t = qk * dec_in_t[h]
        q_decay = jnp.exp(LOG_GAMMA[h] * (l_row + 1.0))
        rt_out.append(_dot(vt, s_t.astype(BF16)) + _dot_nt(st_scr[h].astype(BF16), qr) * q_decay)

    merged = _sigmoid(g0_ref[...]) * y_ml + _sigmoid(g1_ref[...]) * y_sw + _sigmoid(g2_ref[...]) * y_rt
    x1_ref[...] = x_ref[...] + _rms(_dot(merged.astype(BF16), wout_ref[...]), ng_ref[...])

    m_out = jnp.zeros((1, LANES), F32)
    for h in heads:
        qb, kb, vt, _ = ml_ops[h]
        m_prev = gate_ops[h][0]
        h_t, m_t = ml_out[h]
        hml_s[:, hsl[h]] = _rms_over_rows(h_t).T * mlg_ref[:, hsl[h]]
        m_new = m_t[:, L - 1:L]
        b_last = cum[N_HEADS + h:N_HEADS + h + 1, L - 1:L]
        decay = jnp.exp(b_last + m_prev - m_new)
        w_r = jnp.exp(g8[h:h + 1, :] + (b_last - m_new))
        c_old = c_ref[h]
        n_old = n_ref[h]
        c_new = decay * c_old + _dot((vt.astype(F32) * w_r).astype(BF16), kb)
        n_new = decay * n_old + _dot(jnp.broadcast_to(w_r, (MXU_MIN_ROWS, L)).astype(BF16), kb)[0:1, :]
        c_ref[h] = jnp.where(active, c_new, c_old)
        n_ref[h] = jnp.where(active, n_new, n_old)
        m_out = jnp.where(lane == h, m_new, m_out)
    m_ref[...] = jnp.where(active, m_out, m_all)
    for h in heads:
        qr, kr, vt, _ = rt_ops[h]
        lg = LOG_GAMMA[h]
        ort_s[:, hsl[h]] = _rms_over_rows(rt_out[h]).T * retg_ref[:, hsl[h]]
        k_decay = jnp.exp(lg * (L - 1.0 - l_row))
        st_old = st_scr[h]
        st_new = math.exp(lg * L) * st_old + _dot((vt.astype(F32) * k_decay).astype(BF16), kr)
        st_scr[h] = jnp.where(active, st_new, st_old)

    @pl.when(c == pl.num_programs(0) - 1)
    def _emit_state():
        for h in heads:
            s_ref[h] = st_scr[h].T

    osw_s[...] = _swa_block_t(swq_ref[...], k_same, k_swap, vt_buf, tab_t, sinks_ref, c == 0)


def mixer_prompt(z, x, gb, inv, sinks, rb, mlg, retg, ng, layer, wml, wsw, wrt, wout):
    t = z.shape[0]
    L = CHUNK
    n = t // L

    def zspec(cb):
        return pl.BlockSpec((L, 512), lambda c, cb=cb: (jnp.minimum(c, n - 1), cb))

    def tail(w, cb=0):
        return pl.BlockSpec((L, w), lambda c, cb=cb: (jnp.maximum(c - 1, 0), cb))

    def const(shape):
        return pl.BlockSpec(shape, lambda c: (0,) * len(shape))

    def weight(w):
        return _layer_weight(w, layer)

    smem = pl.BlockSpec(memory_space=pltpu.SMEM)
    return pl.pallas_call(
        _mixer_seq_prompt_kernel,
        grid=(n + 1,),
        in_specs=[zspec(ZC_ML_Q), zspec(ZC_ML_K), zspec(ZC_ML_V), zspec(ZC_SW_Q), zspec(ZC_MISC),
                  zspec(ZC_RT_Q), zspec(ZC_RT_K), zspec(ZC_RT_V),
                  tail(512, ZC_ML_O), tail(512, ZC_RT_G),
                  tail(1024, ZC_GATES), tail(1024, ZC_GATES + 1), tail(1024, ZC_GATES + 2), tail(D_MODEL),
                  const((2 * N_HEADS, LANES)), const((1, LANES)), smem, smem,
                  const((1, 512)), const((1, 512)), const((1, D_MODEL)),
                  weight(wml), weight(wsw), weight(wrt), weight(wout)],
        out_specs=[tail(D_MODEL),
                   const((N_HEADS, HEAD_D, HEAD_D)), const((N_HEADS, 1, HEAD_D)), const((1, LANES)),
                   const((N_HEADS, HEAD_D, HEAD_D))],
        out_shape=[jax.ShapeDtypeStruct((t, D_MODEL), F32),
                   jax.ShapeDtypeStruct((N_HEADS, HEAD_D, HEAD_D), F32),
                   jax.ShapeDtypeStruct((N_HEADS, 1, HEAD_D), F32),
                   jax.ShapeDtypeStruct((1, LANES), F32),
                   jax.ShapeDtypeStruct((N_HEADS, HEAD_D, HEAD_D), F32)],
        scratch_shapes=[pltpu.VMEM((N_HEADS, L, L), F32),
                        pltpu.VMEM((2 * L, LANES), BF16), pltpu.VMEM((2 * L, LANES), BF16),
                        pltpu.VMEM((LANES, 2 * L), BF16),
                        pltpu.VMEM((2, 2 * L, 4 * L), F32),
                        pltpu.VMEM((L, L), F32), pltpu.VMEM((L, L), F32),
                        pltpu.VMEM((N_HEADS, L, L), F32),
                        pltpu.VMEM((L, L), BF16),
                        pltpu.VMEM((L, 512), F32), pltpu.VMEM((L, 512), F32), pltpu.VMEM((L, 512), F32)],
        compiler_params=_params(1),
        name="mixer_prompt",
    )(z, z, z, z, z, z, z, z, z, z, z, z, z, x, gb, inv, sinks, rb, mlg, retg, ng, wml, wsw, wrt, wout)


SAMPLE_GROUP = 8


def _tile_bcast(x, l_idx, src):
    n_rows = x.shape[0]
    out = jnp.zeros_like(x)
    for j in range(SEQ_PAD):
        out = out + jnp.where(l_idx == j, pltpu.roll(x, (j - src) % n_rows, 0), 0.0)
    return out


def _tile_total(x, l_idx):
    for d in (1, 2, 4):
        x = x + jnp.where(l_idx >= d, pltpu.roll(x, d, 0), 0.0)
    return x


def _per_seq(fn, group):
    return jnp.concatenate([fn(g) for g in range(group)], axis=0)


def _build_sample_swa_tables(tab_prev, tab_cur, rb_ref, group):
    R = SEQ_PAD
    rows = group * SWA_HEADS * R
    row = lax.broadcasted_iota(jnp.int32, (rows, LANES), 0)
    col = lax.broadcasted_iota(jnp.int32, (rows, LANES), 1)
    l = row & (R - 1)
    h = (row >> 3) & (SWA_HEADS - 1)
    g = row >> 6
    delta = CHUNK + l - col
    prev = jnp.zeros((rows, LANES), F32)
    cur = jnp.zeros((rows, LANES), F32)
    dcur = l - (col & (R - 1))
    for hh in range(SWA_HEADS):
        val = jnp.full((rows, LANES), rb_ref[BUCKET_RUNS[-1][1] * SWA_HEADS + hh], F32)
        for hi, bk in reversed(BUCKET_RUNS[:-1]):
            val = jnp.where(delta < hi, rb_ref[bk * SWA_HEADS + hh], val)
        prev = jnp.where(h == hh, val, prev)
        valc = jnp.zeros((rows, LANES), F32)
        for d in range(R):
            valc = jnp.where(dcur == d, rb_ref[d * SWA_HEADS + hh], valc)
        cur = jnp.where(h == hh, valc, cur)
    tab_prev[...] = jnp.where((delta >= 0) & (delta < CHUNK), prev, -jnp.inf)
    same_seq = ((col >> 3) == g) & (col < group * R)
    tab_cur[...] = jnp.where(same_seq & (dcur >= 0), cur, -jnp.inf)


def _sample_mixers_kernel(mq_ref, mk_ref, mv_ref, swq_ref, misc_ref, rq_ref, rk_ref, rv_ref,
                          c0_ref, n0_ref, m0_ref, s0_ref, ckt_ref, cvt_ref,
                          gb_ref, inv_ref, sinks_ref, rb_ref, *rest, group, first_layer):
    n_layered = 4
    if not first_layer:
        rest = rest[n_layered:]
    hml_ref, osw_ref, ort_ref, n_ref, m_ref, c_all, s_all, kc_all, vc_all, tab_prev, tab_cur = rest
    layered = []
    for ref in (c_all, s_all, kc_all, vc_all):
        if first_layer:
            ref[1:] = jnp.zeros((ref.shape[0] - 1,) + ref.shape[1:], F32)
            layered.append(ref.at[0])
        else:
            layered.append(ref)
    c_ref, s_ref, kc_ref, vc_ref = layered
    R = SEQ_PAD
    NR = group * R

    @pl.when(pl.program_id(0) == 0)
    def _init():
        _build_sample_swa_tables(tab_prev, tab_cur, rb_ref, group)

    lane = lax.broadcasted_iota(jnp.int32, (1, LANES), 1)
    l_idx = lax.broadcasted_iota(jnp.int32, (NR, 1), 0) & (R - 1)
    real = l_idx < N_NEW
    l_f = l_idx.astype(F32)

    def shift(x, d):
        return x if d == 0 else pltpu.roll(x, d, 0)

    def col(slab, h):
        return slab[:, h:h + 1]

    lf = _gate_slab(misc_ref[:, MISC_IF:MISC_IF + LANES], gb_ref)
    bsum = lf
    for d in range(1, N_NEW):
        bsum = bsum + jnp.where(l_idx >= d, shift(lf, d), 0.0)
    b = pltpu.roll(bsum, LANES - N_HEADS, 1)
    gs = lf - b
    m0 = m0_ref[...]
    log_inter = b + m0
    logd = [jnp.where(l_idx >= d, b + shift(gs, d), -jnp.inf) for d in range(N_NEW)]
    m_t = log_inter
    for d in range(N_NEW):
        m_t = jnp.maximum(m_t, logd[d])
    inter = jnp.exp(log_inter - m_t)
    dm = [jnp.exp(logd[d] - m_t) for d in range(N_NEW)]
    emt = jnp.exp(-m_t)
    b_last = _tile_bcast(b, l_idx, N_NEW - 1)
    m_new = _tile_bcast(m_t, l_idx, N_NEW - 1)
    decay = jnp.exp(b_last + m0 - m_new)
    w = jnp.where(real, jnp.exp(b_last - b + lf - m_new), 0.0)
    m_ref[...] = m_t
    n0 = n0_ref[...]
    for h in range(N_HEADS):
        hs = slice(HEAD_D * h, HEAD_D * (h + 1))
        q = mq_ref[:, hs]
        k = mk_ref[:, hs] * (HEAD_D ** -0.5)
        v = mv_ref[:, hs]
        qb = q.astype(BF16)
        kb = k.astype(BF16)
        inter_c = col(inter, h)
        num = _per_seq(lambda g: _dot_nt(qb[R * g:R * (g + 1)], c0_ref[g, h].astype(BF16)), group) * inter_c
        den = inter_c * jnp.sum(q * n0[:, hs], axis=1, keepdims=True)
        for d in range(N_NEW):
            s_d = jnp.sum(q * shift(k, d), axis=1, keepdims=True) * col(dm[d], h)
            num = num + s_d * shift(v, d)
            den = den + s_d
        den = jnp.maximum(jnp.abs(den), col(emt, h))
        hml_ref[:, hs] = num / den
        w_c = col(w, h)
        dec_c = col(decay, h)
        vw = (v * w_c).astype(BF16)
        for g in range(group):
            rs = slice(R * g, R * (g + 1))
            c_ref[g, h] = dec_c[R * g:R * g + 1] * c0_ref[g, h] + _dot_tn(vw[rs], kb[rs])
        n_ref[:, hs] = dec_c * n0[:, hs] + _tile_total(k * w_c, l_idx)

    cos, sin_signed = _rotary_tables((PAST_LEN + l_idx).astype(F32), inv_ref)
    for h in range(N_HEADS):
        hs = slice(HEAD_D * h, HEAD_D * (h + 1))
        lg = LOG_GAMMA[h]
        qr = _rotate(rq_ref[:, hs], cos, sin_signed)
        kr = _rotate(rk_ref[:, hs], cos, sin_signed) * (HEAD_D ** -0.5)
        v = rv_ref[:, hs]
        qrb = qr.astype(BF16)
        vb = v.astype(BF16)
        o = _per_seq(lambda g: _dot(qrb[R * g:R * (g + 1)], s0_ref[g, h].astype(BF16)), group)
        o = o * jnp.exp(lg * (l_f + 1.0))
        for d in range(N_NEW):
            s_d = jnp.sum(qr * shift(kr, d), axis=1, keepdims=True) * math.exp(lg * d)
            o = o + jnp.where(l_idx >= d, s_d, 0.0) * shift(v, d)
        ort_ref[:, hs] = o
        kd = (kr * jnp.where(real, jnp.exp(lg * (N_NEW - 1.0 - l_f)), 0.0)).astype(BF16)
        for g in range(group):
            rs = slice(R * g, R * (g + 1))
            s_ref[g, h] = math.exp(lg * N_NEW) * s0_ref[g, h] + _dot_tn(kd[rs], vb[rs])

    upper = lane >= SWA_D
    q_all = swq_ref[...] * (SWA_D ** -0.5)
    q_heads = []
    for h in range(SWA_HEADS):
        blk = q_all[:, LANES * (h // 2):LANES * (h // 2 + 1)]
        qh = jnp.where(upper if h % 2 == 1 else jnp.logical_not(upper), blk, 0.0)
        if h % 2 != h // (SWA_HEADS // 2):
            qh = pltpu.roll(qh, SWA_D, 1)
        q_heads.append(qh)
    qs = jnp.concatenate([q_heads[h][R * g:R * (g + 1)] for g in range(group) for h in range(SWA_HEADS)],
                         axis=0).astype(BF16)
    hr = SWA_HEADS * R
    zero_rows = jnp.zeros((LANES - NR, LANES), F32)
    k_new = jnp.concatenate([misc_ref[:, MISC_K:MISC_K + LANES], zero_rows], axis=0)
    v_new = jnp.concatenate([misc_ref[:, MISC_V:MISC_V + LANES], zero_rows], axis=0)
    s_prev = jnp.concatenate([_dot(qs[hr * g:hr * (g + 1)], ckt_ref[g].astype(BF16)) for g in range(group)],
                             axis=0) + tab_prev[...]
    s_cur = _dot_nt(qs, k_new.astype(BF16)) + tab_cur[...]
    sink64 = jnp.concatenate([jnp.full((R, 1), sinks_ref[h], F32) for h in range(SWA_HEADS)], axis=0)
    sink = jnp.concatenate([sink64] * group, axis=0)
    m = jnp.maximum(jnp.maximum(jnp.max(s_prev, axis=1, keepdims=True), jnp.max(s_cur, axis=1, keepdims=True)),
                    sink)
    p_prev = jnp.exp(s_prev - m)
    p_cur = jnp.exp(s_cur - m)
    norm = 1.0 / (jnp.sum(p_prev, axis=1, keepdims=True) + jnp.sum(p_cur, axis=1, keepdims=True)
                  + jnp.exp(sink - m))
    pb = p_prev.astype(BF16)
    o = jnp.concatenate([_dot_nt(pb[hr * g:hr * (g + 1)], cvt_ref[g].astype(BF16)) for g in range(group)],
                        axis=0)
    o = (o + _dot(p_cur.astype(BF16), v_new.astype(BF16))) * norm
    for g in range(group):
        blocks = []
        for j in range(SWA_HEADS // 2):
            pair = []
            for h in (2 * j, 2 * j + 1):
                oh = o[hr * g + R * h:hr * g + R * (h + 1)]
                if h % 2 != h // (SWA_HEADS // 2):
                    oh = pltpu.roll(oh, SWA_D, 1)
                pair.append(oh)
            blocks.append(jnp.where(upper, pair[1], pair[0]))
        osw_ref[R * g:R * (g + 1), :] = jnp.concatenate(blocks, axis=1)

    k_new_t = k_new.T
    v_new_t = v_new.T
    for g in range(group):
        back = (LANES - R * g) % LANES
        for new_t, cache_ref, out_ref in ((k_new_t, ckt_ref, kc_ref), (v_new_t, cvt_ref, vc_ref)):
            merged = jnp.where(lane < N_NEW, pltpu.roll(new_t, back, 1) if back else new_t, cache_ref[g])
            out_ref[g] = pltpu.roll(merged, LANES - N_NEW, 1)


def mixer_sample(z, layer, c0_all, n0_rows, m0_rows, s0_all, ckt_all, cvt_all, gb, inv, sinks, rb, prev):
    depth, nb = c0_all.shape[:2]
    group = SAMPLE_GROUP
    R = SEQ_PAD
    NR = group * R
    first_layer = prev is None
    assert first_layer == (layer == 0)

    def zspec(cb):
        return pl.BlockSpec((NR, 512), lambda i, cb=cb: (i, cb))

    def const(shape):
        return pl.BlockSpec(shape, lambda i: (0,) * len(shape))

    def rows(w):
        return pl.BlockSpec((NR, w), lambda i: (i, 0))

    def layered_in(shape):
        return pl.BlockSpec((None, group) + shape, lambda i: (layer, i) + (0,) * len(shape))

    def layered_out(shape):
        if first_layer:
            return pl.BlockSpec((depth, group) + shape, lambda i: (0, i) + (0,) * len(shape))
        return layered_in(shape)

    smem = pl.BlockSpec(memory_space=pltpu.SMEM)
    st = (N_HEADS, HEAD_D, HEAD_D)
    buf = (LANES, CHUNK)
    n_in = 18
    if first_layer:
        extra_specs, extra_args, aliases = [], [], {}
    else:
        extra_specs = [pl.BlockSpec(memory_space=pl.ANY)] * 4
        extra_args = list(prev)
        aliases = {n_in + i: 5 + i for i in range(4)}
    return pl.pallas_call(
        functools.partial(_sample_mixers_kernel, group=group, first_layer=first_layer),
        grid=(nb // group,),
        in_specs=[zspec(ZC_ML_Q), zspec(ZC_ML_K), zspec(ZC_ML_V), zspec(ZC_SW_Q), zspec(ZC_MISC),
                  zspec(ZC_RT_Q), zspec(ZC_RT_K), zspec(ZC_RT_V),
                  layered_in(st), rows(512), rows(LANES), layered_in(st),
                  layered_in(buf), layered_in(buf),
                  const((1, LANES)), const((1, LANES)), smem, smem] + extra_specs,
        out_specs=[rows(512), rows(512), rows(512), rows(512), rows(LANES),
                   layered_out(st), layered_out(st), layered_out(buf), layered_out(buf)],
        out_shape=[jax.ShapeDtypeStruct((nb * R, 512), F32)] * 4
        + [jax.ShapeDtypeStruct((nb * R, LANES), F32),
           jax.ShapeDtypeStruct((depth, nb) + st, F32), jax.ShapeDtypeStruct((depth, nb) + st, F32),
           jax.ShapeDtypeStruct((depth, nb) + buf, F32), jax.ShapeDtypeStruct((depth, nb) + buf, F32)],
        scratch_shapes=[pltpu.VMEM((group * SWA_HEADS * R, LANES), F32),
                        pltpu.VMEM((group * SWA_HEADS * R, LANES), F32)],
        input_output_aliases=aliases,
        compiler_params=_params(1),
        name="mixer_sample",
    )(z, z, z, z, z, z, z, z, c0_all, n0_rows, m0_rows, s0_all, ckt_all, cvt_all, gb, inv, sinks, rb, *extra_args)


def _mixer_post_kernel(hml_ref, osw_ref, ort_ref, mlo_ref, rtg_ref, g0_ref, g1_ref, g2_ref, x_ref,
                       mlg_ref, retg_ref, ng_ref, wml_ref, wsw_ref, wrt_ref, wout_ref, o_ref):
    o_ref[...] = _post_math(hml_ref[...], osw_ref[...], ort_ref[...], mlo_ref[...], rtg_ref[...],
                            g0_ref[...], g1_ref[...], g2_ref[...], x_ref[...],
                            mlg_ref[...], retg_ref[...], ng_ref[...],
                            wml_ref[...], wsw_ref[...], wrt_ref[...], wout_ref[...])


def mixer_post(hml, osw, ort, z, x, mlg, retg, ng, layer, wml, wsw, wrt, wout, tm, name):
    m = x.shape[0]

    def tok(w, cb=0):
        return pl.BlockSpec((tm, w), lambda i, cb=cb: (i, cb))

    def const(shape):
        return pl.BlockSpec(shape, lambda i: (0,) * len(shape))

    return pl.pallas_call(
        _mixer_post_kernel,
        grid=(m // tm,),
        in_specs=[tok(512), tok(512), tok(512), tok(512, ZC_ML_O), tok(512, ZC_RT_G),
                  tok(1024, ZC_GATES), tok(1024, ZC_GATES + 1), tok(1024, ZC_GATES + 2), tok(D_MODEL),
                  const((1, 512)), const((1, 512)), const((1, D_MODEL)),
                  _layer_weight(wml, layer), _layer_weight(wsw, layer), _layer_weight(wrt, layer),
                  _layer_weight(wout, layer)],
        out_specs=tok(D_MODEL),
        out_shape=jax.ShapeDtypeStruct((m, D_MODEL), F32),
        compiler_params=_params(1),
        name=name,
    )(hml, osw, ort, z, z, z, z, z, x, mlg, retg, ng, wml, wsw, wrt, wout)


def _xattn_heads(q, k, v):
    lane = lax.broadcasted_iota(jnp.int32, (1, X_HEADS * X_D), 1)
    acc = jnp.zeros(q.shape, F32)
    for h in range(X_HEADS):
        mask = (lane >= X_D * h) & (lane < X_D * (h + 1))
        s = _dot_nt(jnp.where(mask, q, 0.0).astype(BF16), k) * (X_D ** -0.5)
        m = jnp.max(s, axis=1, keepdims=True)
        p = jnp.exp(s - m)
        p = p / jnp.sum(p, axis=1, keepdims=True)
        acc = acc + jnp.where(mask, _dot(p.astype(BF16), v), 0.0)
    return acc


def _xattn_prompt_kernel(x_ref, gin_ref, gout_ref, wcq_ref, kv_ref, wco_ref, o_ref):
    x = x_ref[...]
    q = _dot(_rms(x, gin_ref[...]).astype(BF16), wcq_ref[...])
    k = kv_ref[:, 0:X_HEADS * X_D].astype(BF16)
    v = kv_ref[:, X_HEADS * X_D:2 * X_HEADS * X_D].astype(BF16)
    o = _xattn_heads(q, k, v)
    o_ref[...] = x + _rms(_dot(o.astype(BF16), wco_ref[...]), gout_ref[...])


def xattn_prompt(x, gin, gout, layer, wcq, kv, wco, tm):
    m = x.shape[0]

    def const(shape):
        return pl.BlockSpec(shape, lambda i: (0,) * len(shape))

    tok = pl.BlockSpec((tm, D_MODEL), lambda i: (i, 0))
    return pl.pallas_call(
        _xattn_prompt_kernel,
        grid=(m // tm,),
        in_specs=[tok, const((1, D_MODEL)), const((1, D_MODEL)), _layer_weight(wcq, layer), const(kv.shape),
                  _layer_weight(wco, layer)],
        out_specs=tok,
        out_shape=jax.ShapeDtypeStruct((m, D_MODEL), F32),
        compiler_params=_params(1),
        name="xattn_prompt",
    )(x, gin, gout, wcq, kv, wco)


def _xattn_sample_kernel(x_ref, gin_ref, gout_ref, wcq_ref, kt_ref, vt_ref, wco_ref, o_ref, q_scr, a_scr,
                         *, group):
    R = SEQ_PAD
    x = x_ref[...]
    q_scr[...] = _dot(_rms(x, gin_ref[...]).astype(BF16), wcq_ref[...])
    lane = lax.broadcasted_iota(jnp.int32, (1, X_HEADS * X_D), 1)
    masks = [(lane >= X_D * h) & (lane < X_D * (h + 1)) for h in range(X_HEADS)]

    def body(g, carry):
        rows = pl.ds(pl.multiple_of(g * R, R), R)
        q = q_scr[rows, :]
        qs = jnp.concatenate([jnp.where(mk, q, 0.0) for mk in masks], axis=0)
        s = _dot(qs.astype(BF16), kt_ref[g].astype(BF16)) * (X_D ** -0.5)
        m = jnp.max(s, axis=1, keepdims=True)
        p = jnp.exp(s - m)
        p = p / jnp.sum(p, axis=1, keepdims=True)
        o = _dot_nt(p.astype(BF16), vt_ref[g].astype(BF16))
        acc = jnp.zeros((R, X_HEADS * X_D), F32)
        for h in range(X_HEADS):
            acc = acc + jnp.where(masks[h], o[R * h:R * (h + 1)], 0.0)
        a_scr[rows, :] = acc
        return carry

    lax.fori_loop(0, group, body, 0, unroll=4)
    o_ref[...] = x + _rms(_dot(a_scr[...].astype(BF16), wco_ref[...]), gout_ref[...])


def xattn_sample(x, gin, gout, wcq, layer, mkt, mvt, wco, group=16):
    nb = mkt.shape[1]
    R = SEQ_PAD

    def const(shape):
        return pl.BlockSpec(shape, lambda i: (0,) * len(shape))

    tok = pl.BlockSpec((group * R, D_MODEL), lambda i: (i, 0))
    mem = pl.BlockSpec((None, group, X_HEADS * X_D, N_MEM), lambda i: (layer, i, 0, 0))
    return pl.pallas_call(
        functools.partial(_xattn_sample_kernel, group=group),
        grid=(nb // group,),
        in_specs=[tok, const((1, D_MODEL)), const((1, D_MODEL)), _layer_weight(wcq, layer), mem, mem,
                  _layer_weight(wco, layer)],
        out_specs=tok,
        out_shape=jax.ShapeDtypeStruct((nb * R, D_MODEL), F32),
        scratch_shapes=[pltpu.VMEM((group * R, X_HEADS * X_D), F32),
                        pltpu.VMEM((group * R, X_HEADS * X_D), F32)],
        compiler_params=_params(1),
        name="xattn_sample",
    )(x, gin, gout, wcq, mkt, mvt, wco)


FFN_SPLIT = 2


def _ffn_kernel(x_ref, gin_ref, gout_ref, wgu_ref, wd_ref, o_ref):
    x = x_ref[...]
    u = _rms(x, gin_ref[...]).astype(BF16)
    fc = D_FF // FFN_SPLIT
    acc = None
    for c in range(FFN_SPLIT):
        g = _dot(u, wgu_ref[:, fc * c:fc * (c + 1)])
        up = _dot(u, wgu_ref[:, D_FF + fc * c:D_FF + fc * (c + 1)])
        h = (g * _sigmoid(g) * up).astype(BF16)
        part = _dot(h, wd_ref[fc * c:fc * (c + 1), :])
        acc = part if acc is None else acc + part
    o_ref[...] = x + _rms(acc, gout_ref[...])


def ffn(x, gin, gout, layer, wgu, wd, tm, name):
    m = x.shape[0]

    def const(shape):
        return pl.BlockSpec(shape, lambda i: (0,) * len(shape))

    tok = pl.BlockSpec((tm, D_MODEL), lambda i: (i, 0))
    return pl.pallas_call(
        _ffn_kernel,
        grid=(m // tm,),
        in_specs=[tok, const((1, D_MODEL)), const((1, D_MODEL)),
                  _layer_weight(wgu, layer), _layer_weight(wd, layer)],
        out_specs=tok,
        out_shape=jax.ShapeDtypeStruct((m, D_MODEL), F32),
        compiler_params=_params(1),
        name=name,
    )(x, gin, gout, wgu, wd)


def _reorder_w_in_t(w):
    wt = jnp.swapaxes(w, 1, 2)
    sizes = (512, 512, 512, 4, 4, 512, 512, 128, 128, 512, 512, 512, 512, 3072)
    offs = np.concatenate([[0], np.cumsum(sizes)])
    (ml_q, ml_k, ml_v, ml_i, ml_f, ml_o, sw_q, sw_k, sw_v, rt_q, rt_k, rt_v, rt_g, gates) = [
        wt[:, int(offs[i]):int(offs[i + 1])] for i in range(len(sizes))]
    pad = jnp.zeros((w.shape[0], 512 - 128 - 128 - 8, w.shape[1]), w.dtype)
    out = jnp.concatenate([ml_q, ml_k, ml_v, ml_o, sw_q, sw_k, sw_v, ml_i, ml_f, pad,
                           rt_q, rt_k, rt_v, rt_g, gates], axis=1)
    assert out.shape[1] == Z_COLS
    return out.astype(BF16)


def _row(v):
    return v.reshape(1, -1).astype(F32)


def _decoder_layer(x, z_fn, mixer_fn, xattn_fn, ffn_fn):
    z = z_fn(x)
    x, states = mixer_fn(z, x)
    x = ffn_fn(xattn_fn(x))
    return x, z, states


def kernel(x_prompt, x_sample, mem_prompt, state_mlstm_C, state_mlstm_n, state_mlstm_m, state_ret_S,
           cache_swa_k, cache_swa_v, cache_mem_k, cache_mem_v, norm_g, w_in, ml_gate_bias, ml_head_g,
           ret_head_g, swa_sinks, rel_bias, w_br_ml, w_br_swa, w_br_ret, w_out, w_cq, w_mkv, w_co,
           w_gu, w_down):
    depth = w_in.shape[0]
    bp, t, d = x_prompt.shape
    assert bp == 1 and d == D_MODEL and t % 1024 == 0
    nb, n_new, _ = x_sample.shape
    assert n_new == N_NEW
    R = SEQ_PAD

    xp = x_prompt.reshape(t, d)
    xs = jnp.pad(x_sample, ((0, 0), (0, R - n_new), (0, 0))).reshape(nb * R, d)
    mem = mem_prompt.reshape(N_MEM, d)
    half = HEAD_D // 2
    inv = ROPE_BASE ** (-jnp.arange(half, dtype=F32) / half)
    inv = jnp.concatenate([inv, inv]).reshape(1, LANES)
    rb = rel_bias.astype(F32).reshape(-1)
    mem_kt = jnp.transpose(cache_mem_k, (0, 1, 3, 4, 2)).reshape(depth, nb, X_HEADS * X_D, N_MEM)
    mem_vt = jnp.transpose(cache_mem_v, (0, 1, 3, 4, 2)).reshape(depth, nb, X_HEADS * X_D, N_MEM)

    win_t = _reorder_w_in_t(w_in)
    wml, wsw, wrt, wout = (w.astype(BF16) for w in (w_br_ml, w_br_swa, w_br_ret, w_out))
    wcq, wmkv, wco, wgu, wd = (w.astype(BF16) for w in (w_cq, w_mkv, w_co, w_gu, w_down))

    outs_p = {k: [] for k in ("C", "n", "m", "S", "k", "v", "mk", "mv")}
    outs_s = {k: [] for k in ("n", "m")}
    sample_layered = None
    swa_kt = jnp.transpose(cache_swa_k, (0, 1, 3, 4, 2)).reshape(depth, nb, LANES, CHUNK)
    swa_vt = jnp.transpose(cache_swa_v, (0, 1, 3, 4, 2)).reshape(depth, nb, LANES, CHUNK)
    for l in range(depth):
        ng = [_row(norm_g[l, i]) for i in range(7)]
        mlg, retg = _row(ml_head_g[l]), _row(ret_head_g[l])
        gb = jnp.concatenate([ml_gate_bias[l, 0], ml_gate_bias[l, 1],
                              jnp.zeros((LANES - 2 * N_HEADS,), F32)]).reshape(1, LANES).astype(F32)
        gb_rows = jnp.broadcast_to(ml_gate_bias[l].astype(F32).reshape(2 * N_HEADS, 1), (2 * N_HEADS, LANES))
        sinks = swa_sinks[l].astype(F32)

        kv = norm_matmul(mem, ng[6], wmkv, l, tm=N_MEM, tn=2 * X_HEADS * X_D, name="memory_kv")

        def seq_p(z, x):
            x1, c_, n_, m_, s_ = mixer_prompt(z, x, gb_rows, inv, sinks, rb, mlg, retg, ng[1], l,
                                              wml, wsw, wrt, wout)
            return x1, (c_, n_, m_, s_)

        xp, zp, (c_, n_, m_, s_) = _decoder_layer(
            xp,
            lambda x: norm_matmul(x, ng[0], win_t, l, tm=min(IN_PROJ_TM, t), tn=1024, name="in_proj_prompt",
                                  w_transposed=True),
            seq_p,
            lambda x: xattn_prompt(x, ng[2], ng[3], l, wcq, kv, wco, tm=512),
            lambda x: ffn(x, ng[4], ng[5], l, wgu, wd, tm=512, name="ffn_prompt"))
        outs_p["C"].append(c_.reshape(1, N_HEADS, HEAD_D, HEAD_D))
        outs_p["n"].append(n_.reshape(1, N_HEADS, HEAD_D))
        outs_p["m"].append(m_[:, :N_HEADS])
        outs_p["S"].append(s_.reshape(1, N_HEADS, HEAD_D, HEAD_D))
        misc_last = zp[t - CHUNK:, 512 * ZC_MISC:512 * ZC_MISC + 2 * LANES]
        outs_p["k"].append(misc_last[:, :LANES].reshape(1, CHUNK, 2, SWA_D))
        outs_p["v"].append(misc_last[:, LANES:].reshape(1, CHUNK, 2, SWA_D))
        outs_p["mk"].append(kv[:, :X_HEADS * X_D].reshape(1, N_MEM, X_HEADS, X_D))
        outs_p["mv"].append(kv[:, X_HEADS * X_D:].reshape(1, N_MEM, X_HEADS, X_D))

        n0_rows = jnp.repeat(state_mlstm_n[l].astype(F32).reshape(nb, N_HEADS * HEAD_D), R, axis=0)
        m0_rows = jnp.repeat(jnp.pad(state_mlstm_m[l].astype(F32), ((0, 0), (0, LANES - N_HEADS))), R, axis=0)

        def seq_s(z, x):
            hml, osw, ort, n_, m_, *layered = mixer_sample(
                z, l, state_mlstm_C, n0_rows, m0_rows, state_ret_S, swa_kt, swa_vt, gb, inv, sinks, rb,
                sample_layered)
            x1 = mixer_post(hml, osw, ort, z, x, mlg, retg, ng[1], l, wml, wsw, wrt, wout,
                            tm=min(256, x.shape[0]), name="mixer_post_sample")
            return x1, (n_, m_, layered)

        xs, zs, (n_, m_, sample_layered) = _decoder_layer(
            xs,
            lambda x: norm_matmul(x, ng[0], win_t, l, tm=min(1024, nb * R), tn=1024, name="in_proj_sample",
                                  w_transposed=True),
            seq_s,
            lambda x: xattn_sample(x, ng[2], ng[3], wcq, l, mem_kt, mem_vt, wco),
            lambda x: ffn(x, ng[4], ng[5], l, wgu, wd, tm=min(512, nb * R), name="ffn_sample"))
        outs_s["n"].append(n_.reshape(nb, R, N_HEADS, HEAD_D)[:, R - 1])
        outs_s["m"].append(m_.reshape(nb, R, LANES)[:, n_new - 1, :N_HEADS])

    sample_c, sample_s, swa_kt_new, swa_vt_new = sample_layered
    s_swa_k = jnp.transpose(swa_kt_new.reshape(depth, nb, 2, SWA_D, CHUNK), (0, 1, 4, 2, 3))
    s_swa_v = jnp.transpose(swa_vt_new.reshape(depth, nb, 2, SWA_D, CHUNK), (0, 1, 4, 2, 3))
    y_p = xp.reshape(1, t, d)
    y_s = xs.reshape(nb, R, d)[:, :n_new]
    st = lambda d_, k: jnp.stack(d_[k])
    return (y_p, y_s,
            st(outs_p, "C"), st(outs_p, "n"), st(outs_p, "m"), st(outs_p, "S"),
            st(outs_p, "k"), st(outs_p, "v"), st(outs_p, "mk"), st(outs_p, "mv"),
            sample_c, st(outs_s, "n"), st(outs_s, "m"), sample_s, s_swa_k, s_swa_v)
```

```python
import functools
import math

import numpy as np
import jax
import jax.numpy as jnp
from jax import lax
from jax.experimental import pallas as pl
from jax.experimental.pallas import tpu as pltpu

F32 = jnp.float32
BF16 = jnp.bfloat16

D_MODEL = 1024
EPS = 1e-6
PAST_LEN = 16384
HEAD_D = 128
N_HEADS = 4
CHUNK = 128
SWA_HEADS = 8
SWA_D = 64
N_BUCKETS = 32
MAX_DISTANCE = 128
ROPE_BASE = 10000.0
N_MEM = 256
X_HEADS = 4
X_D = 64
D_FF = 2816
SEQ_PAD = 8
N_NEW = 4
LANES = 128
IN_PROJ_TM = 2048
MXU_MIN_ROWS = 16
VMEM_LIMIT = 48 * 1024 * 1024

ZC_ML_Q, ZC_ML_K, ZC_ML_V, ZC_ML_O, ZC_SW_Q, ZC_MISC, ZC_RT_Q, ZC_RT_K, ZC_RT_V, ZC_RT_G = range(10)
ZC_GATES = 5
Z_COLS = 8192
MISC_K, MISC_V, MISC_IF = 0, 128, 256

LOG2E = math.log2(math.e)
LN2 = math.log(2.0)
LOG_GAMMA = tuple(float(v) for v in np.log1p(-np.exp2(-5.0 - np.arange(N_HEADS, dtype=np.float32))))

SWA_SAME = (0, 2, 5, 7)
SWA_SWAP = (1, 3, 4, 6)


def _t5_bucket_bounds():
    n = np.arange(CHUNK)
    max_exact = N_BUCKETS // 2
    nf = np.maximum(n, 1).astype(np.float32)
    large = max_exact + (np.log(nf / np.float32(max_exact)) / np.float32(math.log(MAX_DISTANCE / max_exact))
                         * np.float32(N_BUCKETS - max_exact)).astype(np.int32)
    large = np.minimum(large, N_BUCKETS - 1)
    b = np.where(n < max_exact, n, large)
    assert np.all(np.diff(b) >= 0)
    runs = []
    for d in range(CHUNK):
        if runs and runs[-1][1] == int(b[d]):
            runs[-1][0] = d + 1
        else:
            runs.append([d + 1, int(b[d])])
    return tuple((hi, bk) for hi, bk in runs)


BUCKET_RUNS = _t5_bucket_bounds()


def _dot(a, b):
    return jnp.dot(a, b, preferred_element_type=F32)


def _dot_nt(a, b):
    return lax.dot_general(a, b, (((1,), (1,)), ((), ())), preferred_element_type=F32)


def _dot_tn(a, b):
    return lax.dot_general(a, b, (((0,), (0,)), ((), ())), preferred_element_type=F32)


def _rms(x, g):
    return x * lax.rsqrt(jnp.mean(x * x, axis=-1, keepdims=True) + EPS) * g


def _sigmoid(x):
    return 1.0 / (1.0 + jnp.exp2(x * (-LOG2E)))


def _log_sigmoid(x):
    return jnp.minimum(x, 0.0) - jnp.log1p(jnp.exp(-jnp.abs(x)))


def _layer_weight(w_all, layer):
    shape = w_all.shape[1:]
    return pl.BlockSpec((None,) + shape, lambda *_: (layer,) + (0,) * len(shape), pipeline_mode=pl.Buffered(1))


def _params(n_grid):
    return pltpu.CompilerParams(dimension_semantics=("arbitrary",) * n_grid, vmem_limit_bytes=VMEM_LIMIT)


def _norm_matmul_kernel(x_ref, g_ref, w_ref, o_ref, u_ref, *, w_transposed):
    @pl.when(pl.program_id(1) == 0)
    def _():
        u_ref[...] = _rms(x_ref[...], g_ref[...]).astype(BF16)

    o_ref[...] = _dot_nt(u_ref[...], w_ref[...]) if w_transposed else _dot(u_ref[...], w_ref[...])


def norm_matmul(x, g, w_all, layer, tm, tn, name, w_transposed=False):
    m, k = x.shape
    n = w_all.shape[1] if w_transposed else w_all.shape[2]
    if w_transposed:
        w_spec = pl.BlockSpec((None, tn, k), lambda i, j: (layer, j, 0))
    else:
        w_spec = pl.BlockSpec((None, k, tn), lambda i, j: (layer, 0, j))
    return pl.pallas_call(
        functools.partial(_norm_matmul_kernel, w_transposed=w_transposed),
        grid=(m // tm, n // tn),
        in_specs=[pl.BlockSpec((tm, k), lambda i, j: (i, 0)),
                  pl.BlockSpec((1, k), lambda i, j: (0, 0)),
                  w_spec],
        out_specs=pl.BlockSpec((tm, tn), lambda i, j: (i, j)),
        out_shape=jax.ShapeDtypeStruct((m, n), F32),
        scratch_shapes=[pltpu.VMEM((tm, k), BF16)],
        compiler_params=_params(2),
        name=name,
    )(x, g, w_all)


def _transpose_bf16(x, eye):
    return _dot_nt(eye, x).astype(BF16)


def _build_swa_table_t(tab_ref, rb_ref):
    L = CHUNK
    srow = lax.broadcasted_iota(jnp.int32, (2 * L, L), 0)
    lcol = lax.broadcasted_iota(jnp.int32, (2 * L, L), 1)
    delta = lcol + L - srow
    valid = (delta >= 0) & (delta < L)
    for var, heads in enumerate((SWA_SAME, SWA_SWAP)):
        for i, h in enumerate(heads):
            val = jnp.full((2 * L, L), rb_ref[BUCKET_RUNS[-1][1] * SWA_HEADS + h], F32)
            for hi, bk in reversed(BUCKET_RUNS[:-1]):
                val = jnp.where(delta < hi, rb_ref[bk * SWA_HEADS + h], val)
            tab_ref[var, :, i * L:(i + 1) * L] = jnp.where(valid, val, -jnp.inf)


def _swa_block_t(q, k_same, k_swap, vt_buf, tab_ref, sinks_ref, prev_invalid):
    L = CHUNK
    lane = lax.broadcasted_iota(jnp.int32, (1, LANES), 1)
    upper = lane >= SWA_D
    q = q * (SWA_D ** -0.5)
    vt = vt_buf[...]
    vts = (vt, jnp.concatenate([vt[SWA_D:], vt[:SWA_D]], axis=0))
    ks = (k_same[...], k_swap[...])
    outs = [None] * SWA_HEADS
    for var, heads in enumerate((SWA_SAME, SWA_SWAP)):
        qm = []
        sk = []
        for h in heads:
            blk = q[:, LANES * (h // 2):LANES * (h // 2 + 1)]
            keep = upper if h % 2 == 1 else jnp.logical_not(upper)
            qm.append(jnp.where(keep, blk, 0.0))
            sk.append(jnp.full((1, L), sinks_ref[h], F32))
        qs = jnp.concatenate(qm, axis=0).astype(BF16)
        sink = jnp.concatenate(sk, axis=1)
        s_t = _dot_nt(ks[var], qs) + tab_ref[var]
        s_prev = jnp.where(prev_invalid, -jnp.inf, s_t[:L])
        s_t = jnp.concatenate([s_prev, s_t[L:]], axis=0)
        m = jnp.maximum(jnp.max(s_t, axis=0, keepdims=True), sink)
        p = jnp.exp(s_t - m)
        norm = 1.0 / (jnp.sum(p, axis=0, keepdims=True) + jnp.exp(sink - m))
        o_t = _dot(vts[var], p.astype(BF16)) * norm
        for i, h in enumerate(heads):
            outs[h] = o_t[:, i * L:(i + 1) * L].T
    blocks = [jnp.where(upper, outs[2 * j + 1], outs[2 * j]) for j in range(SWA_HEADS // 2)]
    return jnp.concatenate(blocks, axis=1)


def _rotary_tables(pos, inv_ref):
    lane = lax.broadcasted_iota(jnp.int32, (1, LANES), 1)
    ang = pos * inv_ref[...]
    sin = jnp.sin(ang)
    return jnp.cos(ang), jnp.where(lane < HEAD_D // 2, -sin, sin)


def _rotate(x, cos, sin_signed):
    return x * cos + pltpu.roll(x, HEAD_D // 2, 1) * sin_signed


def _gate_slab(raw, gb_ref):
    lane = lax.broadcasted_iota(jnp.int32, (1, LANES), 1)
    x = raw + gb_ref[...]
    return jnp.where((lane >= N_HEADS) & (lane < 2 * N_HEADS), _log_sigmoid(x), x)


def _head_rms(h, gain):
    parts = []
    for i in range(N_HEADS):
        blk = h[:, HEAD_D * i:HEAD_D * (i + 1)]
        parts.append(blk * lax.rsqrt(jnp.mean(blk * blk, axis=-1, keepdims=True) + EPS))
    return jnp.concatenate(parts, axis=-1) * gain


def _rms_over_rows(h_t):
    return h_t * lax.rsqrt(jnp.mean(h_t * h_t, axis=0, keepdims=True) + EPS)


def _post_math(hml, osw, ort, mlo, rtg, g0, g1, g2, x, mlg, retg, ng, wml, wsw, wrt, wout, head_normed=False):
    if not head_normed:
        hml = _head_rms(hml, mlg)
        ort = _head_rms(ort, retg)
    hm = hml * _sigmoid(mlo)
    y_ml = _dot(hm.astype(BF16), wml)
    y_sw = _dot(osw.astype(BF16), wsw)
    rt = ort * (rtg * _sigmoid(rtg))
    y_rt = _dot(rt.astype(BF16), wrt)
    merged = _sigmoid(g0) * y_ml + _sigmoid(g1) * y_sw + _sigmoid(g2) * y_rt
    return x + _rms(_dot(merged.astype(BF16), wout), ng)


def _mixer_seq_prompt_kernel(mq_ref, mk_ref, mv_ref, swq_ref, misc_ref, rq_ref, rk_ref, rv_ref,
                             mlo_ref, rtg_ref, g0_ref, g1_ref, g2_ref, x_ref,
                             gbr_ref, inv_ref, sinks_ref, rb_ref,
                             mlg_ref, retg_ref, ng_ref, wml_ref, wsw_ref, wrt_ref, wout_ref,
                             x1_ref, c_ref, n_ref, m_ref, s_ref,
                             st_scr, k_same, k_swap, vt_buf, tab_t, cos_l, sin_l, dec_in_t, eye_ref,
                             hml_s, osw_s, ort_s):
    c = pl.program_id(0)
    active = c < pl.num_programs(0) - 1
    L = CHUNK
    lane = lax.broadcasted_iota(jnp.int32, (1, LANES), 1)
    row = lax.broadcasted_iota(jnp.int32, (L, L), 0)
    col = lax.broadcasted_iota(jnp.int32, (L, L), 1)
    causal_t = row <= col

    @pl.when(c == 0)
    def _init():
        c_ref[...] = jnp.zeros(c_ref.shape, F32)
        n_ref[...] = jnp.zeros(n_ref.shape, F32)
        m_ref[...] = jnp.zeros(m_ref.shape, F32)
        s_ref[...] = jnp.zeros(s_ref.shape, F32)
        hml_s[...] = jnp.zeros(hml_s.shape, F32)
        osw_s[...] = jnp.zeros(osw_s.shape, F32)
        ort_s[...] = jnp.zeros(ort_s.shape, F32)
        st_scr[...] = jnp.zeros(st_scr.shape, F32)
        k_same[...] = jnp.zeros(k_same.shape, BF16)
        k_swap[...] = jnp.zeros(k_swap.shape, BF16)
        vt_buf[...] = jnp.zeros(vt_buf.shape, BF16)
        _build_swa_table_t(tab_t, rb_ref)
        ang = row.astype(F32) * inv_ref[...]
        cos_l[...] = jnp.cos(ang)
        sin_l[...] = jnp.sin(ang)
        rel_t = (col - row).astype(F32)
        for h in range(N_HEADS):
            dec_in_t[h] = jnp.where(causal_t, jnp.exp(LOG_GAMMA[h] * rel_t), 0.0)
        eye_ref[...] = jnp.where(row == col, 1.0, 0.0).astype(BF16)

    eye = eye_ref[...]
    heads = range(N_HEADS)
    hsl = [slice(HEAD_D * h, HEAD_D * (h + 1)) for h in heads]
    sub8 = lax.broadcasted_iota(jnp.int32, (2 * N_HEADS, 1), 0)
    hml_prev, osw_prev, ort_prev = hml_s[...], osw_s[...], ort_s[...]

    raw = misc_ref[:, MISC_IF:MISC_IF + LANES].T[0:2 * N_HEADS] + gbr_ref[...]
    gates = jnp.where(sub8 >= N_HEADS, _log_sigmoid(raw), raw)
    cum = gates
    for sh in (1, 2, 4, 8, 16, 32, 64):
        cum = cum + jnp.where(lane >= sh, pltpu.roll(cum, sh, 1), 0.0)
    g8 = jnp.where(sub8 < N_HEADS, gates - pltpu.roll(cum, N_HEADS, 0), 0.0)
    g_cols = jnp.concatenate([g8, jnp.zeros((L - 2 * N_HEADS, L), F32)], axis=0).T

    ang0 = (c * L).astype(F32) * inv_ref[...]
    cos0 = jnp.cos(ang0)
    sin0 = jnp.sin(ang0)
    cos_t = cos_l[...]
    sin_t = sin_l[...]
    cos = cos0 * cos_t - sin0 * sin_t
    sin = sin0 * cos_t + cos0 * sin_t
    sin_signed = jnp.where(lane < HEAD_D // 2, -sin, sin)
    ml_ops, rt_ops = [], []
    for h in heads:
        qb = mq_ref[:, hsl[h]].astype(BF16)
        kb = (mk_ref[:, hsl[h]] * (HEAD_D ** -0.5)).astype(BF16)
        vt = _transpose_bf16(mv_ref[:, hsl[h]].astype(BF16), eye)
        ml_ops.append((qb, kb, vt, _dot_nt(kb, qb)))
    for h in heads:
        qr = _rotate(rq_ref[:, hsl[h]], cos, sin_signed).astype(BF16)
        kr = (_rotate(rk_ref[:, hsl[h]], cos, sin_signed) * (HEAD_D ** -0.5)).astype(BF16)
        vt = _transpose_bf16(rv_ref[:, hsl[h]].astype(BF16), eye)
        rt_ops.append((qr, kr, vt, _dot_nt(kr, qr)))
    k_new = misc_ref[:, MISC_K:MISC_K + LANES]
    k_same[0:L, :] = k_same[L:2 * L, :]
    k_swap[0:L, :] = k_swap[L:2 * L, :]
    k_same[L:2 * L, :] = k_new.astype(BF16)
    k_swap[L:2 * L, :] = pltpu.roll(k_new, SWA_D, 1).astype(BF16)
    vt_buf[:, 0:L] = vt_buf[:, L:2 * L]
    vt_buf[:, L:2 * L] = _transpose_bf16(misc_ref[:, MISC_V:MISC_V + LANES].astype(BF16), eye)

    y_ml = _dot((hml_prev * _sigmoid(mlo_ref[...])).astype(BF16), wml_ref[...])

    m_all = m_ref[...]
    gate_ops = []
    for h in heads:
        m_prev = m_all[:, h:h + 1]
        gm = jnp.where(causal_t, g_cols[:, h:h + 1], -jnp.inf)
        mx = jnp.maximum(jnp.max(gm, axis=0, keepdims=True), m_prev)
        gate_ops.append((m_prev, mx, jnp.exp(m_prev - mx), jnp.exp(gm - mx)))

    y_sw = _dot(osw_prev.astype(BF16), wsw_ref[...])
    rtg = rtg_ref[...]
    y_rt = _dot((ort_prev * (rtg * _sigmoid(rtg))).astype(BF16), wrt_ref[...])

    l_row = lane.astype(F32)
    ml_out, rt_out = [], []
    for h in heads:
        qb, kb, vt, qk = ml_ops[h]
        m_prev, mx, inter, dm_t = gate_ops[h]
        s_t = qk * dm_t
        num_t = _dot(vt, s_t.astype(BF16)) + _dot_nt(c_ref[h].astype(BF16), qb) * inter
        nq = _dot_nt(jnp.broadcast_to(n_ref[h], (MXU_MIN_ROWS, HEAD_D)).astype(BF16), qb)[0:1, :]
        den = jnp.sum(s_t, axis=0, keepdims=True) + inter * nq
        m_t = cum[N_HEADS + h:N_HEADS + h + 1, :] + mx
        den = jnp.maximum(jnp.abs(den), jnp.exp(-m_t))
        ml_out.append((num_t * (1.0 / den), m_t))
    for h in heads:
        qr, kr, vt, qk = rt_ops[h]
        s_t = qk * dec_in_t[h]
        q_decay = jnp.exp(LOG_GAMMA[h] * (l_row + 1.0))
        rt_out.append(_dot(vt, s_t.astype(BF16)) + _dot_nt(st_scr[h].astype(BF16), qr) * q_decay)

    merged = _sigmoid(g0_ref[...]) * y_ml + _sigmoid(g1_ref[...]) * y_sw + _sigmoid(g2_ref[...]) * y_rt
    x1_ref[...] = x_ref[...] + _rms(_dot(merged.astype(BF16), wout_ref[...]), ng_ref[...])

    m_out = jnp.zeros((1, LANES), F32)
    for h in heads:
        qb, kb, vt, _ = ml_ops[h]
        m_prev = gate_ops[h][0]
        h_t, m_t = ml_out[h]
        hml_s[:, hsl[h]] = _rms_over_rows(h_t).T * mlg_ref[:, hsl[h]]
        m_new = m_t[:, L - 1:L]
        b_last = cum[N_HEADS + h:N_HEADS + h + 1, L - 1:L]
        decay = jnp.exp(b_last + m_prev - m_new)
        w_r = jnp.exp(g8[h:h + 1, :] + (b_last - m_new))
        c_old = c_ref[h]
        n_old = n_ref[h]
        c_new = decay * c_old + _dot((vt.astype(F32) * w_r).astype(BF16), kb)
        n_new = decay * n_old + _dot(jnp.broadcast_to(w_r, (MXU_MIN_ROWS, L)).astype(BF16), kb)[0:1, :]
        c_ref[h] = jnp.where(active, c_new, c_old)
        n_ref[h] = jnp.where(active, n_new, n_old)
        m_out = jnp.where(lane == h, m_new, m_out)
    m_ref[...] = jnp.where(active, m_out, m_all)
    for h in heads:
        qr, kr, vt, _ = rt_ops[h]
        lg = LOG_GAMMA[h]
        ort_s[:, hsl[h]] = _rms_over_rows(rt_out[h]).T * retg_ref[:, hsl[h]]
        k_decay = jnp.exp(lg * (L - 1.0 - l_row))
        st_old = st_scr[h]
        st_new = math.exp(lg * L) * st_old + _dot((vt.astype(F32) * k_decay).astype(BF16), kr)
        st_scr[h] = jnp.where(active, st_new, st_old)

    @pl.when(c == pl.num_programs(0) - 1)
    def _emit_state():
        for h in heads:
            s_ref[h] = st_scr[h].T

    osw_s[...] = _swa_block_t(swq_ref[...], k_same, k_swap, vt_buf, tab_t, sinks_ref, c == 0)


def mixer_prompt(z, x, gb, inv, sinks, rb, mlg, retg, ng, layer, wml, wsw, wrt, wout):
    t = z.shape[0]
    L = CHUNK
    n = t // L

    def zspec(cb):
        return pl.BlockSpec((L, 512), lambda c, cb=cb: (jnp.minimum(c, n - 1), cb))

    def tail(w, cb=0):
        return pl.BlockSpec((L, w), lambda c, cb=cb: (jnp.maximum(c - 1, 0), cb))

    def const(shape):
        return pl.BlockSpec(shape, lambda c: (0,) * len(shape))

    def weight(w):
        return _layer_weight(w, layer)

    smem = pl.BlockSpec(memory_space=pltpu.SMEM)
    return pl.pallas_call(
        _mixer_seq_prompt_kernel,
        grid=(n + 1,),
        in_specs=[zspec(ZC_ML_Q), zspec(ZC_ML_K), zspec(ZC_ML_V), zspec(ZC_SW_Q), zspec(ZC_MISC),
                  zspec(ZC_RT_Q), zspec(ZC_RT_K), zspec(ZC_RT_V),
                  tail(512, ZC_ML_O), tail(512, ZC_RT_G),
                  tail(1024, ZC_GATES), tail(1024, ZC_GATES + 1), tail(1024, ZC_GATES + 2), tail(D_MODEL),
                  const((2 * N_HEADS, LANES)), const((1, LANES)), smem, smem,
                  const((1, 512)), const((1, 512)), const((1, D_MODEL)),
                  weight(wml), weight(wsw), weight(wrt), weight(wout)],
        out_specs=[tail(D_MODEL),
                   const((N_HEADS, HEAD_D, HEAD_D)), const((N_HEADS, 1, HEAD_D)), const((1, LANES)),
                   const((N_HEADS, HEAD_D, HEAD_D))],
        out_shape=[jax.ShapeDtypeStruct((t, D_MODEL), F32),
                   jax.ShapeDtypeStruct((N_HEADS, HEAD_D, HEAD_D), F32),
                   jax.ShapeDtypeStruct((N_HEADS, 1, HEAD_D), F32),
                   jax.ShapeDtypeStruct((1, LANES), F32),
                   jax.ShapeDtypeStruct((N_HEADS, HEAD_D, HEAD_D), F32)],
        scratch_shapes=[pltpu.VMEM((N_HEADS, L, L), F32),
                        pltpu.VMEM((2 * L, LANES), BF16), pltpu.VMEM((2 * L, LANES), BF16),
                        pltpu.VMEM((LANES, 2 * L), BF16),
                        pltpu.VMEM((2, 2 * L, 4 * L), F32),
                        pltpu.VMEM((L, L), F32), pltpu.VMEM((L, L), F32),
                        pltpu.VMEM((N_HEADS, L, L), F32),
                        pltpu.VMEM((L, L), BF16),
                        pltpu.VMEM((L, 512), F32), pltpu.VMEM((L, 512), F32), pltpu.VMEM((L, 512), F32)],
        compiler_params=_params(1),
        name="mixer_prompt",
    )(z, z, z, z, z, z, z, z, z, z, z, z, z, x, gb, inv, sinks, rb, mlg, retg, ng, wml, wsw, wrt, wout)


SAMPLE_GROUP = 8


def _tile_bcast(x, l_idx, src):
    n_rows = x.shape[0]
    out = jnp.zeros_like(x)
    for j in range(SEQ_PAD):
        out = out + jnp.where(l_idx == j, pltpu.roll(x, (j - src) % n_rows, 0), 0.0)
    return out


def _tile_total(x, l_idx):
    for d in (1, 2, 4):
        x = x + jnp.where(l_idx >= d, pltpu.roll(x, d, 0), 0.0)
    return x


def _per_seq(fn, group):
    return jnp.concatenate([fn(g) for g in range(group)], axis=0)


def _build_sample_swa_tables(tab_prev, tab_cur, rb_ref, group):
    R = SEQ_PAD
    rows = group * SWA_HEADS * R
    row = lax.broadcasted_iota(jnp.int32, (rows, LANES), 0)
    col = lax.broadcasted_iota(jnp.int32, (rows, LANES), 1)
    l = row & (R - 1)
    h = (row >> 3) & (SWA_HEADS - 1)
    g = row >> 6
    delta = CHUNK + l - col
    prev = jnp.zeros((rows, LANES), F32)
    cur = jnp.zeros((rows, LANES), F32)
    dcur = l - (col & (R - 1))
    for hh in range(SWA_HEADS):
        val = jnp.full((rows, LANES), rb_ref[BUCKET_RUNS[-1][1] * SWA_HEADS + hh], F32)
        for hi, bk in reversed(BUCKET_RUNS[:-1]):
            val = jnp.where(delta < hi, rb_ref[bk * SWA_HEADS + hh], val)
        prev = jnp.where(h == hh, val, prev)
        valc = jnp.zeros((rows, LANES), F32)
        for d in range(R):
            valc = jnp.where(dcur == d, rb_ref[d * SWA_HEADS + hh], valc)
        cur = jnp.where(h == hh, valc, cur)
    tab_prev[...] = jnp.where((delta >= 0) & (delta < CHUNK), prev, -jnp.inf)
    same_seq = ((col >> 3) == g) & (col < group * R)
    tab_cur[...] = jnp.where(same_seq & (dcur >= 0), cur, -jnp.inf)


def _sample_mixers_kernel(mq_ref, mk_ref, mv_ref, swq_ref, misc_ref, rq_ref, rk_ref, rv_ref,
                          c0_ref, n0_ref, m0_ref, s0_ref, ckt_ref, cvt_ref,
                          gb_ref, inv_ref, sinks_ref, rb_ref, *rest, group, first_layer):
    n_layered = 4
    if not first_layer:
        rest = rest[n_layered:]
    hml_ref, osw_ref, ort_ref, n_ref, m_ref, c_all, s_all, kc_all, vc_all, tab_prev, tab_cur = rest
    layered = []
    for ref in (c_all, s_all, kc_all, vc_all):
        if first_layer:
            ref[1:] = jnp.zeros((ref.shape[0] - 1,) + ref.shape[1:], F32)
            layered.append(ref.at[0])
        else:
            layered.append(ref)
    c_ref, s_ref, kc_ref, vc_ref = layered
    R = SEQ_PAD
    NR = group * R

    @pl.when(pl.program_id(0) == 0)
    def _init():
        _build_sample_swa_tables(tab_prev, tab_cur, rb_ref, group)

    lane = lax.broadcasted_iota(jnp.int32, (1, LANES), 1)
    l_idx = lax.broadcasted_iota(jnp.int32, (NR, 1), 0) & (R - 1)
    real = l_idx < N_NEW
    l_f = l_idx.astype(F32)

    def shift(x, d):
        return x if d == 0 else pltpu.roll(x, d, 0)

    def col(slab, h):
        return slab[:, h:h + 1]

    lf = _gate_slab(misc_ref[:, MISC_IF:MISC_IF + LANES], gb_ref)
    bsum = lf
    for d in range(1, N_NEW):
        bsum = bsum + jnp.where(l_idx >= d, shift(lf, d), 0.0)
    b = pltpu.roll(bsum, LANES - N_HEADS, 1)
    gs = lf - b
    m0 = m0_ref[...]
    log_inter = b + m0
    logd = [jnp.where(l_idx >= d, b + shift(gs, d), -jnp.inf) for d in range(N_NEW)]
    m_t = log_inter
    for d in range(N_NEW):
        m_t = jnp.maximum(m_t, logd[d])
    inter = jnp.exp(log_inter - m_t)
    dm = [jnp.exp(logd[d] - m_t) for d in range(N_NEW)]
    emt = jnp.exp(-m_t)
    b_last = _tile_bcast(b, l_idx, N_NEW - 1)
    m_new = _tile_bcast(m_t, l_idx, N_NEW - 1)
    decay = jnp.exp(b_last + m0 - m_new)
    w = jnp.where(real, jnp.exp(b_last - b + lf - m_new), 0.0)
    m_ref[...] = m_t
    n0 = n0_ref[...]
    for h in range(N_HEADS):
        hs = slice(HEAD_D * h, HEAD_D * (h + 1))
        q = mq_ref[:, hs]
        k = mk_ref[:, hs] * (HEAD_D ** -0.5)
        v = mv_ref[:, hs]
        qb = q.astype(BF16)
        kb = k.astype(BF16)
        inter_c = col(inter, h)
        num = _per_seq(lambda g: _dot_nt(qb[R * g:R * (g + 1)], c0_ref[g, h].astype(BF16)), group) * inter_c
        den = inter_c * jnp.sum(q * n0[:, hs], axis=1, keepdims=True)
        for d in range(N_NEW):
            s_d = jnp.sum(q * shift(k, d), axis=1, keepdims=True) * col(dm[d], h)
            num = num + s_d * shift(v, d)
            den = den + s_d
        den = jnp.maximum(jnp.abs(den), col(emt, h))
        hml_ref[:, hs] = num / den
        w_c = col(w, h)
        dec_c = col(decay, h)
        vw = (v * w_c).astype(BF16)
        for g in range(group):
            rs = slice(R * g, R * (g + 1))
            c_ref[g, h] = dec_c[R * g:R * g + 1] * c0_ref[g, h] + _dot_tn(vw[rs], kb[rs])
        n_ref[:, hs] = dec_c * n0[:, hs] + _tile_total(k * w_c, l_idx)

    cos, sin_signed = _rotary_tables((PAST_LEN + l_idx).astype(F32), inv_ref)
    for h in range(N_HEADS):
        hs = slice(HEAD_D * h, HEAD_D * (h + 1))
        lg = LOG_GAMMA[h]
        qr = _rotate(rq_ref[:, hs], cos, sin_signed)
        kr = _rotate(rk_ref[:, hs], cos, sin_signed) * (HEAD_D ** -0.5)
        v = rv_ref[:, hs]
        qrb = qr.astype(BF16)
        vb = v.astype(BF16)
        o = _per_seq(lambda g: _dot(qrb[R * g:R * (g + 1)], s0_ref[g, h].astype(BF16)), group)
        o = o * jnp.exp(lg * (l_f + 1.0))
        for d in range(N_NEW):
            s_d = jnp.sum(qr * shift(kr, d), axis=1, keepdims=True) * math.exp(lg * d)
            o = o + jnp.where(l_idx >= d, s_d, 0.0) * shift(v, d)
        ort_ref[:, hs] = o
        kd = (kr * jnp.where(real, jnp.exp(lg * (N_NEW - 1.0 - l_f)), 0.0)).astype(BF16)
        for g in range(group):
            rs = slice(R * g, R * (g + 1))
            s_ref[g, h] = math.exp(lg * N_NEW) * s0_ref[g, h] + _dot_tn(kd[rs], vb[rs])

    upper = lane >= SWA_D
    q_all = swq_ref[...] * (SWA_D ** -0.5)
    q_heads = []
    for h in range(SWA_HEADS):
        blk = q_all[:, LANES * (h // 2):LANES * (h // 2 + 1)]
        qh = jnp.where(upper if h % 2 == 1 else jnp.logical_not(upper), blk, 0.0)
        if h % 2 != h // (SWA_HEADS // 2):
            qh = pltpu.roll(qh, SWA_D, 1)
        q_heads.append(qh)
    qs = jnp.concatenate([q_heads[h][R * g:R * (g + 1)] for g in range(group) for h in range(SWA_HEADS)],
                         axis=0).astype(BF16)
    hr = SWA_HEADS * R
    zero_rows = jnp.zeros((LANES - NR, LANES), F32)
    k_new = jnp.concatenate([misc_ref[:, MISC_K:MISC_K + LANES], zero_rows], axis=0)
    v_new = jnp.concatenate([misc_ref[:, MISC_V:MISC_V + LANES], zero_rows], axis=0)
    s_prev = jnp.concatenate([_dot(qs[hr * g:hr * (g + 1)], ckt_ref[g].astype(BF16)) for g in range(group)],
                             axis=0) + tab_prev[...]
    s_cur = _dot_nt(qs, k_new.astype(BF16)) + tab_cur[...]
    sink64 = jnp.concatenate([jnp.full((R, 1), sinks_ref[h], F32) for h in range(SWA_HEADS)], axis=0)
    sink = jnp.concatenate([sink64] * group, axis=0)
    m = jnp.maximum(jnp.maximum(jnp.max(s_prev, axis=1, keepdims=True), jnp.max(s_cur, axis=1, keepdims=True)),
                    sink)
    p_prev = jnp.exp(s_prev - m)
    p_cur = jnp.exp(s_cur - m)
    norm = 1.0 / (jnp.sum(p_prev, axis=1, keepdims=True) + jnp.sum(p_cur, axis=1, keepdims=True)
                  + jnp.exp(sink - m))
    pb = p_prev.astype(BF16)
    o = jnp.concatenate([_dot_nt(pb[hr * g:hr * (g + 1)], cvt_ref[g].astype(BF16)) for g in range(group)],
                        axis=0)
    o = (o + _dot(p_cur.astype(BF16), v_new.astype(BF16))) * norm
    for g in range(group):
        blocks = []
        for j in range(SWA_HEADS // 2):
            pair = []
            for h in (2 * j, 2 * j + 1):
                oh = o[hr * g + R * h:hr * g + R * (h + 1)]
                if h % 2 != h // (SWA_HEADS // 2):
                    oh = pltpu.roll(oh, SWA_D, 1)
                pair.append(oh)
            blocks.append(jnp.where(upper, pair[1], pair[0]))
        osw_ref[R * g:R * (g + 1), :] = jnp.concatenate(blocks, axis=1)

    k_new_t = k_new.T
    v_new_t = v_new.T
    for g in range(group):
        back = (LANES - R * g) % LANES
        for new_t, cache_ref, out_ref in ((k_new_t, ckt_ref, kc_ref), (v_new_t, cvt_ref, vc_ref)):
            merged = jnp.where(lane < N_NEW, pltpu.roll(new_t, back, 1) if back else new_t, cache_ref[g])
            out_ref[g] = pltpu.roll(merged, LANES - N_NEW, 1)


def mixer_sample(z, layer, c0_all, n0_rows, m0_rows, s0_all, ckt_all, cvt_all, gb, inv, sinks, rb, prev):
    depth, nb = c0_all.shape[:2]
    group = SAMPLE_GROUP
    R = SEQ_PAD
    NR = group * R
    first_layer = prev is None
    assert first_layer == (layer == 0)

    def zspec(cb):
        return pl.BlockSpec((NR, 512), lambda i, cb=cb: (i, cb))

    def const(shape):
        return pl.BlockSpec(shape, lambda i: (0,) * len(shape))

    def rows(w):
        return pl.BlockSpec((NR, w), lambda i: (i, 0))

    def layered_in(shape):
        return pl.BlockSpec((None, group) + shape, lambda i: (layer, i) + (0,) * len(shape))

    def layered_out(shape):
        if first_layer:
            return pl.BlockSpec((depth, group) + shape, lambda i: (0, i) + (0,) * len(shape))
        return layered_in(shape)

    smem = pl.BlockSpec(memory_space=pltpu.SMEM)
    st = (N_HEADS, HEAD_D, HEAD_D)
    buf = (LANES, CHUNK)
    n_in = 18
    if first_layer:
        extra_specs, extra_args, aliases = [], [], {}
    else:
        extra_specs = [pl.BlockSpec(memory_space=pl.ANY)] * 4
        extra_args = list(prev)
        aliases = {n_in + i: 5 + i for i in range(4)}
    return pl.pallas_call(
        functools.partial(_sample_mixers_kernel, group=group, first_layer=first_layer),
        grid=(nb // group,),
        in_specs=[zspec(ZC_ML_Q), zspec(ZC_ML_K), zspec(ZC_ML_V), zspec(ZC_SW_Q), zspec(ZC_MISC),
                  zspec(ZC_RT_Q), zspec(ZC_RT_K), zspec(ZC_RT_V),
                  layered_in(st), rows(512), rows(LANES), layered_in(st),
                  layered_in(buf), layered_in(buf),
                  const((1, LANES)), const((1, LANES)), smem, smem] + extra_specs,
        out_specs=[rows(512), rows(512), rows(512), rows(512), rows(LANES),
                   layered_out(st), layered_out(st), layered_out(buf), layered_out(buf)],
        out_shape=[jax.ShapeDtypeStruct((nb * R, 512), F32)] * 4
        + [jax.ShapeDtypeStruct((nb * R, LANES), F32),
           jax.ShapeDtypeStruct((depth, nb) + st, F32), jax.ShapeDtypeStruct((depth, nb) + st, F32),
           jax.ShapeDtypeStruct((depth, nb) + buf, F32), jax.ShapeDtypeStruct((depth, nb) + buf, F32)],
        scratch_shapes=[pltpu.VMEM((group * SWA_HEADS * R, LANES), F32),
                        pltpu.VMEM((group * SWA_HEADS * R, LANES), F32)],
        input_output_aliases=aliases,
        compiler_params=_params(1),
        name="mixer_sample",
    )(z, z, z, z, z, z, z, z, c0_all, n0_rows, m0_rows, s0_all, ckt_all, cvt_all, gb, inv, sinks, rb, *extra_args)


def _mixer_post_kernel(hml_ref, osw_ref, ort_ref, mlo_ref, rtg_ref, g0_ref, g1_ref, g2_ref, x_ref,
                       mlg_ref, retg_ref, ng_ref, wml_ref, wsw_ref, wrt_ref, wout_ref, o_ref):
    o_ref[...] = _post_math(hml_ref[...], osw_ref[...], ort_ref[...], mlo_ref[...], rtg_ref[...],
                            g0_ref[...], g1_ref[...], g2_ref[...], x_ref[...],
                            mlg_ref[...], retg_ref[...], ng_ref[...],
                            wml_ref[...], wsw_ref[...], wrt_ref[...], wout_ref[...])


def mixer_post(hml, osw, ort, z, x, mlg, retg, ng, layer, wml, wsw, wrt, wout, tm, name):
    m = x.shape[0]

    def tok(w, cb=0):
        return pl.BlockSpec((tm, w), lambda i, cb=cb: (i, cb))

    def const(shape):
        return pl.BlockSpec(shape, lambda i: (0,) * len(shape))

    return pl.pallas_call(
        _mixer_post_kernel,
        grid=(m // tm,),
        in_specs=[tok(512), tok(512), tok(512), tok(512, ZC_ML_O), tok(512, ZC_RT_G),
                  tok(1024, ZC_GATES), tok(1024, ZC_GATES + 1), tok(1024, ZC_GATES + 2), tok(D_MODEL),
                  const((1, 512)), const((1, 512)), const((1, D_MODEL)),
                  _layer_weight(wml, layer), _layer_weight(wsw, layer), _layer_weight(wrt, layer),
                  _layer_weight(wout, layer)],
        out_specs=tok(D_MODEL),
        out_shape=jax.ShapeDtypeStruct((m, D_MODEL), F32),
        compiler_params=_params(1),
        name=name,
    )(hml, osw, ort, z, z, z, z, z, x, mlg, retg, ng, wml, wsw, wrt, wout)


def _xattn_head(q, k, v, mask):

    s = _dot_nt(jnp.where(mask, q, 0.0).astype(BF16), k) * (X_D ** -0.5)
    m = jnp.max(s, axis=1, keepdims=True)
    p = jnp.exp(s - m)
    p = p / jnp.sum(p, axis=1, keepdims=True)
    return jnp.where(mask, _dot(p.astype(BF16), v), 0.0)


def _xattn_ffn_prompt_kernel(x_ref, gxi_ref, gxo_ref, wcq_ref, kv_ref, wco_ref, gfi_ref, gfo_ref, wgu_ref, wd_ref,
                             o_ref, mid_scr):
    @pl.when(pl.program_id(0) == 0)
    def _():
        mid_scr[...] = jnp.zeros(mid_scr.shape, F32)

    x_mid = mid_scr[...]
    x = x_ref[...]
    lane = lax.broadcasted_iota(jnp.int32, (1, X_HEADS * X_D), 1)
    masks = [(lane >= X_D * h) & (lane < X_D * (h + 1)) for h in range(X_HEADS)]
    fc = D_FF // FFN_SPLIT
    heads_per_piece = X_HEADS // FFN_SPLIT

    u = _rms(x_mid, gfi_ref[...]).astype(BF16)
    q = _dot(_rms(x, gxi_ref[...]).astype(BF16), wcq_ref[...])
    k = kv_ref[:, 0:X_HEADS * X_D].astype(BF16)
    v = kv_ref[:, X_HEADS * X_D:2 * X_HEADS * X_D].astype(BF16)
    att = jnp.zeros(q.shape, F32)
    acc = None
    for c in range(FFN_SPLIT):
        g = _dot(u, wgu_ref[:, fc * c:fc * (c + 1)])
        up = _dot(u, wgu_ref[:, D_FF + fc * c:D_FF + fc * (c + 1)])
        for h in range(heads_per_piece * c, heads_per_piece * (c + 1)):
            att = att + _xattn_head(q, k, v, masks[h])
        hid = (g * _sigmoid(g) * up).astype(BF16)
        part = _dot(hid, wd_ref[fc * c:fc * (c + 1), :])
        acc = part if acc is None else acc + part
    o_ref[...] = x_mid + _rms(acc, gfo_ref[...])
    mid_scr[...] = x + _rms(_dot(att.astype(BF16), wco_ref[...]), gxo_ref[...])


def xattn_ffn_prompt(x, gxi, gxo, gfi, gfo, layer, wcq, kv, wco, wgu, wd, tm):
    m = x.shape[0]
    n = m // tm

    def const(shape):
        return pl.BlockSpec(shape, lambda s: (0,) * len(shape))

    gain = const((1, D_MODEL))
    return pl.pallas_call(
        _xattn_ffn_prompt_kernel,
        grid=(n + 1,),
        in_specs=[pl.BlockSpec((tm, D_MODEL), lambda s: (jnp.minimum(s, n - 1), 0)),
                  gain, gain, _layer_weight(wcq, layer), const(kv.shape), _layer_weight(wco, layer),
                  gain, gain, _layer_weight(wgu, layer), _layer_weight(wd, layer)],
        out_specs=pl.BlockSpec((tm, D_MODEL), lambda s: (jnp.maximum(s - 1, 0), 0)),
        out_shape=jax.ShapeDtypeStruct((m, D_MODEL), F32),
        scratch_shapes=[pltpu.VMEM((tm, D_MODEL), F32)],
        compiler_params=_params(1),
        name="xattn_ffn_prompt",
    )(x, gxi, gxo, wcq, kv, wco, gfi, gfo, wgu, wd)


def _xattn_sample_kernel(x_ref, gin_ref, gout_ref, wcq_ref, kt_ref, vt_ref, wco_ref, o_ref, q_scr, a_scr,
                         *, group):
    R = SEQ_PAD
    x = x_ref[...]
    q_scr[...] = _dot(_rms(x, gin_ref[...]).astype(BF16), wcq_ref[...])
    lane = lax.broadcasted_iota(jnp.int32, (1, X_HEADS * X_D), 1)
    masks = [(lane >= X_D * h) & (lane < X_D * (h + 1)) for h in range(X_HEADS)]

    def body(g, carry):
        rows = pl.ds(pl.multiple_of(g * R, R), R)
        q = q_scr[rows, :]
        qs = jnp.concatenate([jnp.where(mk, q, 0.0) for mk in masks], axis=0)
        s = _dot(qs.astype(BF16), kt_ref[g].astype(BF16)) * (X_D ** -0.5)
        m = jnp.max(s, axis=1, keepdims=True)
        p = jnp.exp(s - m)
        p = p / jnp.sum(p, axis=1, keepdims=True)
        o = _dot_nt(p.astype(BF16), vt_ref[g].astype(BF16))
        acc = jnp.zeros((R, X_HEADS * X_D), F32)
        for h in range(X_HEADS):
            acc = acc + jnp.where(masks[h], o[R * h:R * (h + 1)], 0.0)
        a_scr[rows, :] = acc
        return carry

    lax.fori_loop(0, group, body, 0, unroll=4)
    o_ref[...] = x + _rms(_dot(a_scr[...].astype(BF16), wco_ref[...]), gout_ref[...])


def xattn_sample(x, gin, gout, wcq, layer, mkt, mvt, wco, group=16):
    nb = mkt.shape[1]
    R = SEQ_PAD

    def const(shape):
        return pl.BlockSpec(shape, lambda i: (0,) * len(shape))

    tok = pl.BlockSpec((group * R, D_MODEL), lambda i: (i, 0))
    mem = pl.BlockSpec((None, group, X_HEADS * X_D, N_MEM), lambda i: (layer, i, 0, 0))
    return pl.pallas_call(
        functools.partial(_xattn_sample_kernel, group=group),
        grid=(nb // group,),
        in_specs=[tok, const((1, D_MODEL)), const((1, D_MODEL)), _layer_weight(wcq, layer), mem, mem,
                  _layer_weight(wco, layer)],
        out_specs=tok,
        out_shape=jax.ShapeDtypeStruct((nb * R, D_MODEL), F32),
        scratch_shapes=[pltpu.VMEM((group * R, X_HEADS * X_D), F32),
                        pltpu.VMEM((group * R, X_HEADS * X_D), F32)],
        compiler_params=_params(1),
        name="xattn_sample",
    )(x, gin, gout, wcq, mkt, mvt, wco)


FFN_SPLIT = 2


def _ffn_kernel(x_ref, gin_ref, gout_ref, wgu_ref, wd_ref, o_ref):
    x = x_ref[...]
    u = _rms(x, gin_ref[...]).astype(BF16)
    fc = D_FF // FFN_SPLIT
    acc = None
    for c in range(FFN_SPLIT):
        g = _dot(u, wgu_ref[:, fc * c:fc * (c + 1)])
        up = _dot(u, wgu_ref[:, D_FF + fc * c:D_FF + fc * (c + 1)])
        h = (g * _sigmoid(g) * up).astype(BF16)
        part = _dot(h, wd_ref[fc * c:fc * (c + 1), :])
        acc = part if acc is None else acc + part
    o_ref[...] = x + _rms(acc, gout_ref[...])


def ffn(x, gin, gout, layer, wgu, wd, tm, name):
    m = x.shape[0]

    def const(shape):
        return pl.BlockSpec(shape, lambda i: (0,) * len(shape))

    tok = pl.BlockSpec((tm, D_MODEL), lambda i: (i, 0))
    return pl.pallas_call(
        _ffn_kernel,
        grid=(m // tm,),
        in_specs=[tok, const((1, D_MODEL)), const((1, D_MODEL)),
                  _layer_weight(wgu, layer), _layer_weight(wd, layer)],
        out_specs=tok,
        out_shape=jax.ShapeDtypeStruct((m, D_MODEL), F32),
        compiler_params=_params(1),
        name=name,
    )(x, gin, gout, wgu, wd)


def _reorder_w_in_t(w):
    wt = jnp.swapaxes(w, 1, 2)
    sizes = (512, 512, 512, 4, 4, 512, 512, 128, 128, 512, 512, 512, 512, 3072)
    offs = np.concatenate([[0], np.cumsum(sizes)])
    (ml_q, ml_k, ml_v, ml_i, ml_f, ml_o, sw_q, sw_k, sw_v, rt_q, rt_k, rt_v, rt_g, gates) = [
        wt[:, int(offs[i]):int(offs[i + 1])] for i in range(len(sizes))]
    pad = jnp.zeros((w.shape[0], 512 - 128 - 128 - 8, w.shape[1]), w.dtype)
    out = jnp.concatenate([ml_q, ml_k, ml_v, ml_o, sw_q, sw_k, sw_v, ml_i, ml_f, pad,
                           rt_q, rt_k, rt_v, rt_g, gates], axis=1)
    assert out.shape[1] == Z_COLS
    return out.astype(BF16)


def _row(v):
    return v.reshape(1, -1).astype(F32)


def _decoder_layer(x, z_fn, mixer_fn, xattn_ffn_fn):
    z = z_fn(x)
    x, states = mixer_fn(z, x)
    return xattn_ffn_fn(x), z, states


def kernel(x_prompt, x_sample, mem_prompt, state_mlstm_C, state_mlstm_n, state_mlstm_m, state_ret_S,
           cache_swa_k, cache_swa_v, cache_mem_k, cache_mem_v, norm_g, w_in, ml_gate_bias, ml_head_g,
           ret_head_g, swa_sinks, rel_bias, w_br_ml, w_br_swa, w_br_ret, w_out, w_cq, w_mkv, w_co,
           w_gu, w_down):
    depth = w_in.shape[0]
    bp, t, d = x_prompt.shape
    assert bp == 1 and d == D_MODEL and t % 1024 == 0
    nb, n_new, _ = x_sample.shape
    assert n_new == N_NEW
    R = SEQ_PAD

    xp = x_prompt.reshape(t, d)
    xs = jnp.pad(x_sample, ((0, 0), (0, R - n_new), (0, 0))).reshape(nb * R, d)
    mem = mem_prompt.reshape(N_MEM, d)
    half = HEAD_D // 2
    inv = ROPE_BASE ** (-jnp.arange(half, dtype=F32) / half)
    inv = jnp.concatenate([inv, inv]).reshape(1, LANES)
    rb = rel_bias.astype(F32).reshape(-1)
    mem_kt = jnp.transpose(cache_mem_k, (0, 1, 3, 4, 2)).reshape(depth, nb, X_HEADS * X_D, N_MEM)
    mem_vt = jnp.transpose(cache_mem_v, (0, 1, 3, 4, 2)).reshape(depth, nb, X_HEADS * X_D, N_MEM)

    win_t = _reorder_w_in_t(w_in)
    wml, wsw, wrt, wout = (w.astype(BF16) for w in (w_br_ml, w_br_swa, w_br_ret, w_out))
    wcq, wmkv, wco, wgu, wd = (w.astype(BF16) for w in (w_cq, w_mkv, w_co, w_gu, w_down))

    outs_p = {k: [] for k in ("C", "n", "m", "S", "k", "v", "mk", "mv")}
    outs_s = {k: [] for k in ("n", "m")}
    sample_layered = None
    swa_kt = jnp.transpose(cache_swa_k, (0, 1, 3, 4, 2)).reshape(depth, nb, LANES, CHUNK)
    swa_vt = jnp.transpose(cache_swa_v, (0, 1, 3, 4, 2)).reshape(depth, nb, LANES, CHUNK)
    for l in range(depth):
        ng = [_row(norm_g[l, i]) for i in range(7)]
        mlg, retg = _row(ml_head_g[l]), _row(ret_head_g[l])
        gb = jnp.concatenate([ml_gate_bias[l, 0], ml_gate_bias[l, 1],
                              jnp.zeros((LANES - 2 * N_HEADS,), F32)]).reshape(1, LANES).astype(F32)
        gb_rows = jnp.broadcast_to(ml_gate_bias[l].astype(F32).reshape(2 * N_HEADS, 1), (2 * N_HEADS, LANES))
        sinks = swa_sinks[l].astype(F32)

        kv = norm_matmul(mem, ng[6], wmkv, l, tm=N_MEM, tn=2 * X_HEADS * X_D, name="memory_kv")

        def seq_p(z, x):
            x1, c_, n_, m_, s_ = mixer_prompt(z, x, gb_rows, inv, sinks, rb, mlg, retg, ng[1], l,
                                              wml, wsw, wrt, wout)
            return x1, (c_, n_, m_, s_)

        xp, zp, (c_, n_, m_, s_) = _decoder_layer(
            xp,
            lambda x: norm_matmul(x, ng[0], win_t, l, tm=min(IN_PROJ_TM, t), tn=1024, name="in_proj_prompt",
                                  w_transposed=True),
            seq_p,
            lambda x: xattn_ffn_prompt(x, ng[2], ng[3], ng[4], ng[5], l, wcq, kv, wco, wgu, wd, tm=512))
        outs_p["C"].append(c_.reshape(1, N_HEADS, HEAD_D, HEAD_D))
        outs_p["n"].append(n_.reshape(1, N_HEADS, HEAD_D))
        outs_p["m"].append(m_[:, :N_HEADS])
        outs_p["S"].append(s_.reshape(1, N_HEADS, HEAD_D, HEAD_D))
        misc_last = zp[t - CHUNK:, 512 * ZC_MISC:512 * ZC_MISC + 2 * LANES]
        outs_p["k"].append(misc_last[:, :LANES].reshape(1, CHUNK, 2, SWA_D))
        outs_p["v"].append(misc_last[:, LANES:].reshape(1, CHUNK, 2, SWA_D))
        outs_p["mk"].append(kv[:, :X_HEADS * X_D].reshape(1, N_MEM, X_HEADS, X_D))
        outs_p["mv"].append(kv[:, X_HEADS * X_D:].reshape(1, N_MEM, X_HEADS, X_D))

        n0_rows = jnp.repeat(state_mlstm_n[l].astype(F32).reshape(nb, N_HEADS * HEAD_D), R, axis=0)
        m0_rows = jnp.repeat(jnp.pad(state_mlstm_m[l].astype(F32), ((0, 0), (0, LANES - N_HEADS))), R, axis=0)

        def seq_s(z, x):
            hml, osw, ort, n_, m_, *layered = mixer_sample(
                z, l, state_mlstm_C, n0_rows, m0_rows, state_ret_S, swa_kt, swa_vt, gb, inv, sinks, rb,
                sample_layered)
            x1 = mixer_post(hml, osw, ort, z, x, mlg, retg, ng[1], l, wml, wsw, wrt, wout,
                            tm=min(256, x.shape[0]), name="mixer_post_sample")
            return x1, (n_, m_, layered)

        xs, zs, (n_, m_, sample_layered) = _decoder_layer(
            xs,
            lambda x: norm_matmul(x, ng[0], win_t, l, tm=min(1024, nb * R), tn=1024, name="in_proj_sample",
                                  w_transposed=True),
            seq_s,
            lambda x: ffn(xattn_sample(x, ng[2], ng[3], wcq, l, mem_kt, mem_vt, wco),
                          ng[4], ng[5], l, wgu, wd, tm=min(512, nb * R), name="ffn_sample"))
        outs_s["n"].append(n_.reshape(nb, R, N_HEADS, HEAD_D)[:, R - 1])
        outs_s["m"].append(m_.reshape(nb, R, LANES)[:, n_new - 1, :N_HEADS])

    sample_c, sample_s, swa_kt_new, swa_vt_new = sample_layered
    s_swa_k = jnp.transpose(swa_kt_new.reshape(depth, nb, 2, SWA_D, CHUNK), (0, 1, 4, 2, 3))
    s_swa_v = jnp.transpose(swa_vt_new.reshape(depth, nb, 2, SWA_D, CHUNK), (0, 1, 4, 2, 3))
    y_p = xp.reshape(1, t, d)
    y_s = xs.reshape(nb, R, d)[:, :n_new]
    st = lambda d_, k: jnp.stack(d_[k])
    return (y_p, y_s,
            st(outs_p, "C"), st(outs_p, "n"), st(outs_p, "m"), st(outs_p, "S"),
            st(outs_p, "k"), st(outs_p, "v"), st(outs_p, "mk"), st(outs_p, "mv"),
            sample_c, st(outs_s, "n"), st(outs_s, "m"), sample_s, s_swa_k, s_swa_v)
```

```python
import functools
import math

import numpy as np
import jax
import jax.numpy as jnp
from jax import lax
from jax.experimental import pallas as pl
from jax.experimental.pallas import tpu as pltpu

F32 = jnp.float32
BF16 = jnp.bfloat16

D_MODEL = 1024
EPS = 1e-6
PAST_LEN = 16384
HEAD_D = 128
N_HEADS = 4
CHUNK = 128
SWA_HEADS = 8
SWA_D = 64
N_BUCKETS = 32
MAX_DISTANCE = 128
ROPE_BASE = 10000.0
N_MEM = 256
X_HEADS = 4
X_D = 64
D_FF = 2816
SEQ_PAD = 8
N_NEW = 4
LANES = 128
IN_PROJ_TM = 2048
MXU_MIN_ROWS = 16
VMEM_LIMIT = 48 * 1024 * 1024

ZC_ML_Q, ZC_ML_K, ZC_ML_V, ZC_ML_O, ZC_SW_Q, ZC_MISC, ZC_RT_Q, ZC_RT_K, ZC_RT_V, ZC_RT_G = range(10)
ZC_GATES = 5
Z_COLS = 8192
MISC_K, MISC_V, MISC_IF = 0, 128, 256

LOG2E = math.log2(math.e)
LN2 = math.log(2.0)
LOG_GAMMA = tuple(float(v) for v in np.log1p(-np.exp2(-5.0 - np.arange(N_HEADS, dtype=np.float32))))

SWA_SAME = (0, 2, 5, 7)
SWA_SWAP = (1, 3, 4, 6)


def _t5_bucket_bounds():
    n = np.arange(CHUNK)
    max_exact = N_BUCKETS // 2
    nf = np.maximum(n, 1).astype(np.float32)
    large = max_exact + (np.log(nf / np.float32(max_exact)) / np.float32(math.log(MAX_DISTANCE / max_exact))
                         * np.float32(N_BUCKETS - max_exact)).astype(np.int32)
    large = np.minimum(large, N_BUCKETS - 1)
    b = np.where(n < max_exact, n, large)
    assert np.all(np.diff(b) >= 0)
    runs = []
    for d in range(CHUNK):
        if runs and runs[-1][1] == int(b[d]):
            runs[-1][0] = d + 1
        else:
            runs.append([d + 1, int(b[d])])
    return tuple((hi, bk) for hi, bk in runs)


BUCKET_RUNS = _t5_bucket_bounds()


def _dot(a, b):
    return jnp.dot(a, b, preferred_element_type=F32)


def _dot_nt(a, b):
    return lax.dot_general(a, b, (((1,), (1,)), ((), ())), preferred_element_type=F32)


def _dot_tn(a, b):
    return lax.dot_general(a, b, (((0,), (0,)), ((), ())), preferred_element_type=F32)


def _rms(x, g):
    return x * lax.rsqrt(jnp.mean(x * x, axis=-1, keepdims=True) + EPS) * g


def _sigmoid(x):
    return 1.0 / (1.0 + jnp.exp2(x * (-LOG2E)))


def _log_sigmoid(x):
    return jnp.minimum(x, 0.0) - jnp.log1p(jnp.exp(-jnp.abs(x)))


def _layer_weight(w_all, layer):
    shape = w_all.shape[1:]
    return pl.BlockSpec((None,) + shape, lambda *_: (layer,) + (0,) * len(shape), pipeline_mode=pl.Buffered(1))


def _params(n_grid):
    return pltpu.CompilerParams(dimension_semantics=("arbitrary",) * n_grid, vmem_limit_bytes=VMEM_LIMIT)


def _norm_matmul_kernel(x_ref, g_ref, w_ref, o_ref, u_ref, *, w_transposed):
    @pl.when(pl.program_id(1) == 0)
    def _():
        u_ref[...] = _rms(x_ref[...], g_ref[...]).astype(BF16)

    o_ref[...] = _dot_nt(u_ref[...], w_ref[...]) if w_transposed else _dot(u_ref[...], w_ref[...])


def norm_matmul(x, g, w_all, layer, tm, tn, name, w_transposed=False):
    m, k = x.shape
    n = w_all.shape[1] if w_transposed else w_all.shape[2]
    if w_transposed:
        w_spec = pl.BlockSpec((None, tn, k), lambda i, j: (layer, j, 0))
    else:
        w_spec = pl.BlockSpec((None, k, tn), lambda i, j: (layer, 0, j))
    return pl.pallas_call(
        functools.partial(_norm_matmul_kernel, w_transposed=w_transposed),
        grid=(m // tm, n // tn),
        in_specs=[pl.BlockSpec((tm, k), lambda i, j: (i, 0)),
                  pl.BlockSpec((1, k), lambda i, j: (0, 0)),
                  w_spec],
        out_specs=pl.BlockSpec((tm, tn), lambda i, j: (i, j)),
        out_shape=jax.ShapeDtypeStruct((m, n), F32),
        scratch_shapes=[pltpu.VMEM((tm, k), BF16)],
        compiler_params=_params(2),
        name=name,
    )(x, g, w_all)


def _transpose_bf16(x, eye):
    return _dot_nt(eye, x).astype(BF16)


def _build_swa_table_t(tab_ref, rb_ref):
    L = CHUNK
    srow = lax.broadcasted_iota(jnp.int32, (2 * L, L), 0)
    lcol = lax.broadcasted_iota(jnp.int32, (2 * L, L), 1)
    delta = lcol + L - srow
    valid = (delta >= 0) & (delta < L)
    for var, heads in enumerate((SWA_SAME, SWA_SWAP)):
        for i, h in enumerate(heads):
            val = jnp.full((2 * L, L), rb_ref[BUCKET_RUNS[-1][1] * SWA_HEADS + h], F32)
            for hi, bk in reversed(BUCKET_RUNS[:-1]):
                val = jnp.where(delta < hi, rb_ref[bk * SWA_HEADS + h], val)
            tab_ref[var, :, i * L:(i + 1) * L] = jnp.where(valid, val, -jnp.inf)


def _swa_scores_t(q, k_same, k_swap, tab_ref, prev_invalid):
    L = CHUNK
    lane = lax.broadcasted_iota(jnp.int32, (1, LANES), 1)
    upper = lane >= SWA_D
    q = q * (SWA_D ** -0.5)
    ks = (k_same[...], k_swap[...])
    scores = []
    for var, heads in enumerate((SWA_SAME, SWA_SWAP)):
        qm = []
        for h in heads:
            blk = q[:, LANES * (h // 2):LANES * (h // 2 + 1)]
            qm.append(jnp.where(upper if h % 2 == 1 else jnp.logical_not(upper), blk, 0.0))
        qs = jnp.concatenate(qm, axis=0).astype(BF16)
        s_t = _dot_nt(ks[var], qs) + tab_ref[var]
        scores.append(jnp.concatenate([jnp.where(prev_invalid, -jnp.inf, s_t[:L]), s_t[L:]], axis=0))
    return scores


def _swa_softmax_t(scores, sinks_ref):
    probs = []
    for s_t, heads in zip(scores, (SWA_SAME, SWA_SWAP)):
        sink = jnp.concatenate([jnp.full((1, CHUNK), sinks_ref[h], F32) for h in heads], axis=1)
        m = jnp.maximum(jnp.max(s_t, axis=0, keepdims=True), sink)
        p = jnp.exp(s_t - m)
        norm = 1.0 / (jnp.sum(p, axis=0, keepdims=True) + jnp.exp(sink - m))
        probs.append((p.astype(BF16), norm))
    return probs


def _swa_output_t(probs, vt_buf):
    L = CHUNK
    lane = lax.broadcasted_iota(jnp.int32, (1, LANES), 1)
    upper = lane >= SWA_D
    vt = vt_buf[...]
    vts = (vt, jnp.concatenate([vt[SWA_D:], vt[:SWA_D]], axis=0))
    outs = [None] * SWA_HEADS
    for var, heads in enumerate((SWA_SAME, SWA_SWAP)):
        p, norm = probs[var]
        o_t = _dot(vts[var], p) * norm
        for i, h in enumerate(heads):
            outs[h] = o_t[:, i * L:(i + 1) * L].T
    blocks = [jnp.where(upper, outs[2 * j + 1], outs[2 * j]) for j in range(SWA_HEADS // 2)]
    return jnp.concatenate(blocks, axis=1)


def _rotary_tables(pos, inv_ref):
    lane = lax.broadcasted_iota(jnp.int32, (1, LANES), 1)
    ang = pos * inv_ref[...]
    sin = jnp.sin(ang)
    return jnp.cos(ang), jnp.where(lane < HEAD_D // 2, -sin, sin)


def _rotate(x, cos, sin_signed):
    return x * cos + pltpu.roll(x, HEAD_D // 2, 1) * sin_signed


def _gate_slab(raw, gb_ref):
    lane = lax.broadcasted_iota(jnp.int32, (1, LANES), 1)
    x = raw + gb_ref[...]
    return jnp.where((lane >= N_HEADS) & (lane < 2 * N_HEADS), _log_sigmoid(x), x)


def _head_rms(h, gain):
    parts = []
    for i in range(N_HEADS):
        blk = h[:, HEAD_D * i:HEAD_D * (i + 1)]
        parts.append(blk * lax.rsqrt(jnp.mean(blk * blk, axis=-1, keepdims=True) + EPS))
    return jnp.concatenate(parts, axis=-1) * gain


def _rms_over_rows(h_t):
    return h_t * lax.rsqrt(jnp.mean(h_t * h_t, axis=0, keepdims=True) + EPS)


def _post_math(hml, osw, ort, mlo, rtg, g0, g1, g2, x, mlg, retg, ng, wml, wsw, wrt, wout, head_normed=False):
    if not head_normed:
        hml = _head_rms(hml, mlg)
        ort = _head_rms(ort, retg)
    hm = hml * _sigmoid(mlo)
    y_ml = _dot(hm.astype(BF16), wml)
    y_sw = _dot(osw.astype(BF16), wsw)
    rt = ort * (rtg * _sigmoid(rtg))
    y_rt = _dot(rt.astype(BF16), wrt)
    merged = _sigmoid(g0) * y_ml + _sigmoid(g1) * y_sw + _sigmoid(g2) * y_rt
    return x + _rms(_dot(merged.astype(BF16), wout), ng)


def _mixer_seq_prompt_kernel(mq_ref, mk_ref, mv_ref, swq_ref, misc_ref, rq_ref, rk_ref, rv_ref,
                             mlo_ref, rtg_ref, g0_ref, g1_ref, g2_ref, x_ref,
                             gbr_ref, inv_ref, sinks_ref, rb_ref,
                             mlg_ref, retg_ref, ng_ref, wml_ref, wsw_ref, wrt_ref, wout_ref,
                             x1_ref, c_ref, n_ref, m_ref, s_ref,
                             st_scr, k_same, k_swap, vt_buf, tab_t, cos_l, sin_l, dec_in_t, eye_ref,
                             hml_s, osw_s, ort_s):
    c = pl.program_id(0)
    active = c < pl.num_programs(0) - 1
    L = CHUNK
    lane = lax.broadcasted_iota(jnp.int32, (1, LANES), 1)
    row = lax.broadcasted_iota(jnp.int32, (L, L), 0)
    col = lax.broadcasted_iota(jnp.int32, (L, L), 1)
    causal_t = row <= col

    @pl.when(c == 0)
    def _init():
        c_ref[...] = jnp.zeros(c_ref.shape, F32)
        n_ref[...] = jnp.zeros(n_ref.shape, F32)
        m_ref[...] = jnp.zeros(m_ref.shape, F32)
        s_ref[...] = jnp.zeros(s_ref.shape, F32)
        hml_s[...] = jnp.zeros(hml_s.shape, F32)
        osw_s[...] = jnp.zeros(osw_s.shape, F32)
        ort_s[...] = jnp.zeros(ort_s.shape, F32)
        st_scr[...] = jnp.zeros(st_scr.shape, F32)
        k_same[...] = jnp.zeros(k_same.shape, BF16)
        k_swap[...] = jnp.zeros(k_swap.shape, BF16)
        vt_buf[...] = jnp.zeros(vt_buf.shape, BF16)
        _build_swa_table_t(tab_t, rb_ref)
        ang = row.astype(F32) * inv_ref[...]
        cos_l[...] = jnp.cos(ang)
        sin_l[...] = jnp.sin(ang)
        rel_t = (col - row).astype(F32)
        for h in range(N_HEADS):
            dec_in_t[h] = jnp.where(causal_t, jnp.exp(LOG_GAMMA[h] * rel_t), 0.0)
        eye_ref[...] = jnp.where(row == col, 1.0, 0.0).astype(BF16)

    eye = eye_ref[...]
    heads = range(N_HEADS)
    hsl = [slice(HEAD_D * h, HEAD_D * (h + 1)) for h in heads]
    sub8 = lax.broadcasted_iota(jnp.int32, (2 * N_HEADS, 1), 0)
    hml_prev, osw_prev, ort_prev = hml_s[...], osw_s[...], ort_s[...]

    raw = misc_ref[:, MISC_IF:MISC_IF + LANES].T[0:2 * N_HEADS] + gbr_ref[...]
    gates = jnp.where(sub8 >= N_HEADS, _log_sigmoid(raw), raw)
    cum = gates
    for sh in (1, 2, 4, 8, 16, 32, 64):
        cum = cum + jnp.where(lane >= sh, pltpu.roll(cum, sh, 1), 0.0)
    g8 = jnp.where(sub8 < N_HEADS, gates - pltpu.roll(cum, N_HEADS, 0), 0.0)
    g_cols = jnp.concatenate([g8, jnp.zeros((L - 2 * N_HEADS, L), F32)], axis=0).T

    ang0 = (c * L).astype(F32) * inv_ref[...]
    cos0 = jnp.cos(ang0)
    sin0 = jnp.sin(ang0)
    cos_t = cos_l[...]
    sin_t = sin_l[...]
    cos = cos0 * cos_t - sin0 * sin_t
    sin = sin0 * cos_t + cos0 * sin_t
    sin_signed = jnp.where(lane < HEAD_D // 2, -sin, sin)
    ml_ops, rt_ops = [], []
    for h in heads:
        qb = mq_ref[:, hsl[h]].astype(BF16)
        kb = (mk_ref[:, hsl[h]] * (HEAD_D ** -0.5)).astype(BF16)
        vt = _transpose_bf16(mv_ref[:, hsl[h]].astype(BF16), eye)
        ml_ops.append((qb, kb, vt, _dot_nt(kb, qb)))
    for h in heads:
        qr = _rotate(rq_ref[:, hsl[h]], cos, sin_signed).astype(BF16)
        kr = (_rotate(rk_ref[:, hsl[h]], cos, sin_signed) * (HEAD_D ** -0.5)).astype(BF16)
        vt = _transpose_bf16(rv_ref[:, hsl[h]].astype(BF16), eye)
        rt_ops.append((qr, kr, vt, _dot_nt(kr, qr)))
    k_new = misc_ref[:, MISC_K:MISC_K + LANES]
    k_same[0:L, :] = k_same[L:2 * L, :]
    k_swap[0:L, :] = k_swap[L:2 * L, :]
    k_same[L:2 * L, :] = k_new.astype(BF16)
    k_swap[L:2 * L, :] = pltpu.roll(k_new, SWA_D, 1).astype(BF16)
    vt_buf[:, 0:L] = vt_buf[:, L:2 * L]
    vt_buf[:, L:2 * L] = _transpose_bf16(misc_ref[:, MISC_V:MISC_V + LANES].astype(BF16), eye)
    swa_scores = _swa_scores_t(swq_ref[...], k_same, k_swap, tab_t, c == 0)

    y_ml = _dot((hml_prev * _sigmoid(mlo_ref[...])).astype(BF16), wml_ref[...])

    m_all = m_ref[...]
    gate_ops = []
    for h in heads:
        m_prev = m_all[:, h:h + 1]
        gm = jnp.where(causal_t, g_cols[:, h:h + 1], -jnp.inf)
        mx = jnp.maximum(jnp.max(gm, axis=0, keepdims=True), m_prev)
        gate_ops.append((m_prev, mx, jnp.exp(m_prev - mx), jnp.exp(gm - mx)))

    y_sw = _dot(osw_prev.astype(BF16), wsw_ref[...])
    rtg = rtg_ref[...]
    y_rt = _dot((ort_prev * (rtg * _sigmoid(rtg))).astype(BF16), wrt_ref[...])

    l_row = lane.astype(F32)
    ml_out, rt_out = [], []
    for h in heads:
        qb, kb, vt, qk = ml_ops[h]
        m_prev, mx, inter, dm_t = gate_ops[h]
        s_t = qk * dm_t
        num_t = _dot(vt, s_t.astype(BF16)) + _dot_nt(c_ref[h].astype(BF16), qb) * inter
        nq = _dot_nt(jnp.broadcast_to(n_ref[h], (MXU_MIN_ROWS, HEAD_D)).astype(BF16), qb)[0:1, :]
        den = jnp.sum(s_t, axis=0, keepdims=True) + inter * nq
        m_t = cum[N_HEADS + h:N_HEADS + h + 1, :] + mx
        den = jnp.maximum(jnp.abs(den), jnp.exp(-m_t))
        ml_out.append((num_t * (1.0 / den), m_t))
    for h in heads:
        qr, kr, vt, qk = rt_ops[h]
        s_t = qk * dec_in_t[h]
        q_decay = jnp.exp(LOG_GAMMA[h] * (l_row + 1.0))
        rt_out.append(_dot(vt, s_t.astype(BF16)) + _dot_nt(st_scr[h].astype(BF16), qr) * q_decay)
    swa_probs = _swa_softmax_t(swa_scores, sinks_ref)

    merged = _sigmoid(g0_ref[...]) * y_ml + _sigmoid(g1_ref[...]) * y_sw + _sigmoid(g2_ref[...]) * y_rt
    x1_ref[...] = x_ref[...] + _rms(_dot(merged.astype(BF16), wout_ref[...]), ng_ref[...])

    m_out = jnp.zeros((1, LANES), F32)
    for h in heads:
        qb, kb, vt, _ = ml_ops[h]
        m_prev = gate_ops[h][0]
        h_t, m_t = ml_out[h]
        hml_s[:, hsl[h]] = _rms_over_rows(h_t).T * mlg_ref[:, hsl[h]]
        m_new = m_t[:, L - 1:L]
        b_last = cum[N_HEADS + h:N_HEADS + h + 1, L - 1:L]
        decay = jnp.exp(b_last + m_prev - m_new)
        w_r = jnp.exp(g8[h:h + 1, :] + (b_last - m_new))
        c_old = c_ref[h]
        n_old = n_ref[h]
        c_new = decay * c_old + _dot((vt.astype(F32) * w_r).astype(BF16), kb)
        n_new = decay * n_old + _dot(jnp.broadcast_to(w_r, (MXU_MIN_ROWS, L)).astype(BF16), kb)[0:1, :]
        c_ref[h] = jnp.where(active, c_new, c_old)
        n_ref[h] = jnp.where(active, n_new, n_old)
        m_out = jnp.where(lane == h, m_new, m_out)
    m_ref[...] = jnp.where(active, m_out, m_all)
    for h in heads:
        qr, kr, vt, _ = rt_ops[h]
        lg = LOG_GAMMA[h]
        ort_s[:, hsl[h]] = _rms_over_rows(rt_out[h]).T * retg_ref[:, hsl[h]]
        k_decay = jnp.exp(lg * (L - 1.0 - l_row))
        st_old = st_scr[h]
        st_new = math.exp(lg * L) * st_old + _dot((vt.astype(F32) * k_decay).astype(BF16), kr)
        st_scr[h] = jnp.where(active, st_new, st_old)

    osw_s[...] = _swa_output_t(swa_probs, vt_buf)

    @pl.when(c == pl.num_programs(0) - 1)
    def _emit_state():
        for h in heads:
            s_ref[h] = st_scr[h].T


def mixer_prompt(z, x, gb, inv, sinks, rb, mlg, retg, ng, layer, wml, wsw, wrt, wout):
    t = z.shape[0]
    L = CHUNK
    n = t // L

    def zspec(cb):
        return pl.BlockSpec((L, 512), lambda c, cb=cb: (jnp.minimum(c, n - 1), cb))

    def tail(w, cb=0):
        return pl.BlockSpec((L, w), lambda c, cb=cb: (jnp.maximum(c - 1, 0), cb))

    def const(shape):
        return pl.BlockSpec(shape, lambda c: (0,) * len(shape))

    def weight(w):
        return _layer_weight(w, layer)

    smem = pl.BlockSpec(memory_space=pltpu.SMEM)
    return pl.pallas_call(
        _mixer_seq_prompt_kernel,
        grid=(n + 1,),
        in_specs=[zspec(ZC_ML_Q), zspec(ZC_ML_K), zspec(ZC_ML_V), zspec(ZC_SW_Q), zspec(ZC_MISC),
                  zspec(ZC_RT_Q), zspec(ZC_RT_K), zspec(ZC_RT_V),
                  tail(512, ZC_ML_O), tail(512, ZC_RT_G),
                  tail(1024, ZC_GATES), tail(1024, ZC_GATES + 1), tail(1024, ZC_GATES + 2), tail(D_MODEL),
                  const((2 * N_HEADS, LANES)), const((1, LANES)), smem, smem,
                  const((1, 512)), const((1, 512)), const((1, D_MODEL)),
                  weight(wml), weight(wsw), weight(wrt), weight(wout)],
        out_specs=[tail(D_MODEL),
                   const((N_HEADS, HEAD_D, HEAD_D)), const((N_HEADS, 1, HEAD_D)), const((1, LANES)),
                   const((N_HEADS, HEAD_D, HEAD_D))],
        out_shape=[jax.ShapeDtypeStruct((t, D_MODEL), F32),
                   jax.ShapeDtypeStruct((N_HEADS, HEAD_D, HEAD_D), F32),
                   jax.ShapeDtypeStruct((N_HEADS, 1, HEAD_D), F32),
                   jax.ShapeDtypeStruct((1, LANES), F32),
                   jax.ShapeDtypeStruct((N_HEADS, HEAD_D, HEAD_D), F32)],
        scratch_shapes=[pltpu.VMEM((N_HEADS, L, L), F32),
                        pltpu.VMEM((2 * L, LANES), BF16), pltpu.VMEM((2 * L, LANES), BF16),
                        pltpu.VMEM((LANES, 2 * L), BF16),
                        pltpu.VMEM((2, 2 * L, 4 * L), F32),
                        pltpu.VMEM((L, L), F32), pltpu.VMEM((L, L), F32),
                        pltpu.VMEM((N_HEADS, L, L), F32),
                        pltpu.VMEM((L, L), BF16),
                        pltpu.VMEM((L, 512), F32), pltpu.VMEM((L, 512), F32), pltpu.VMEM((L, 512), F32)],
        compiler_params=_params(1),
        name="mixer_prompt",
    )(z, z, z, z, z, z, z, z, z, z, z, z, z, x, gb, inv, sinks, rb, mlg, retg, ng, wml, wsw, wrt, wout)


SAMPLE_GROUP = 8


def _tile_bcast(x, l_idx, src):
    n_rows = x.shape[0]
    out = jnp.zeros_like(x)
    for j in range(SEQ_PAD):
        out = out + jnp.where(l_idx == j, pltpu.roll(x, (j - src) % n_rows, 0), 0.0)
    return out


def _tile_total(x, l_idx):
    for d in (1, 2, 4):
        x = x + jnp.where(l_idx >= d, pltpu.roll(x, d, 0), 0.0)
    return x


def _per_seq(fn, group):
    return jnp.concatenate([fn(g) for g in range(group)], axis=0)


def _build_sample_swa_tables(tab_prev, tab_cur, rb_ref, group):
    R = SEQ_PAD
    rows = group * SWA_HEADS * R
    row = lax.broadcasted_iota(jnp.int32, (rows, LANES), 0)
    col = lax.broadcasted_iota(jnp.int32, (rows, LANES), 1)
    l = row & (R - 1)
    h = (row >> 3) & (SWA_HEADS - 1)
    g = row >> 6
    delta = CHUNK + l - col
    prev = jnp.zeros((rows, LANES), F32)
    cur = jnp.zeros((rows, LANES), F32)
    dcur = l - (col & (R - 1))
    for hh in range(SWA_HEADS):
        val = jnp.full((rows, LANES), rb_ref[BUCKET_RUNS[-1][1] * SWA_HEADS + hh], F32)
        for hi, bk in reversed(BUCKET_RUNS[:-1]):
            val = jnp.where(delta < hi, rb_ref[bk * SWA_HEADS + hh], val)
        prev = jnp.where(h == hh, val, prev)
        valc = jnp.zeros((rows, LANES), F32)
        for d in range(R):
            valc = jnp.where(dcur == d, rb_ref[d * SWA_HEADS + hh], valc)
        cur = jnp.where(h == hh, valc, cur)
    tab_prev[...] = jnp.where((delta >= 0) & (delta < CHUNK), prev, -jnp.inf)
    same_seq = ((col >> 3) == g) & (col < group * R)
    tab_cur[...] = jnp.where(same_seq & (dcur >= 0), cur, -jnp.inf)


def _sample_mixers_kernel(mq_ref, mk_ref, mv_ref, swq_ref, misc_ref, rq_ref, rk_ref, rv_ref,
                          c0_ref, n0_ref, m0_ref, s0_ref, ckt_ref, cvt_ref,
                          gb_ref, inv_ref, sinks_ref, rb_ref, *rest, group, first_layer):
    n_layered = 4
    if not first_layer:
        rest = rest[n_layered:]
    hml_ref, osw_ref, ort_ref, n_ref, m_ref, c_all, s_all, kc_all, vc_all, tab_prev, tab_cur = rest
    layered = []
    for ref in (c_all, s_all, kc_all, vc_all):
        if first_layer:
            ref[1:] = jnp.zeros((ref.shape[0] - 1,) + ref.shape[1:], F32)
            layered.append(ref.at[0])
        else:
            layered.append(ref)
    c_ref, s_ref, kc_ref, vc_ref = layered
    R = SEQ_PAD
    NR = group * R

    @pl.when(pl.program_id(0) == 0)
    def _init():
        _build_sample_swa_tables(tab_prev, tab_cur, rb_ref, group)

    lane = lax.broadcasted_iota(jnp.int32, (1, LANES), 1)
    l_idx = lax.broadcasted_iota(jnp.int32, (NR, 1), 0) & (R - 1)
    real = l_idx < N_NEW
    l_f = l_idx.astype(F32)

    def shift(x, d):
        return x if d == 0 else pltpu.roll(x, d, 0)

    def col(slab, h):
        return slab[:, h:h + 1]

    zero_rows = jnp.zeros((LANES - NR, LANES), F32)
    k_new = jnp.concatenate([misc_ref[:, MISC_K:MISC_K + LANES], zero_rows], axis=0)
    v_new = jnp.concatenate([misc_ref[:, MISC_V:MISC_V + LANES], zero_rows], axis=0)
    k_new_t = k_new.T
    v_new_t = v_new.T

    def emit_window_buffers(g):
        back = (LANES - R * g) % LANES
        for new_t, cache_ref, out_ref in ((k_new_t, ckt_ref, kc_ref), (v_new_t, cvt_ref, vc_ref)):
            merged = jnp.where(lane < N_NEW, pltpu.roll(new_t, back, 1) if back else new_t, cache_ref[g])
            out_ref[g] = pltpu.roll(merged, LANES - N_NEW, 1)

    lf = _gate_slab(misc_ref[:, MISC_IF:MISC_IF + LANES], gb_ref)
    bsum = lf
    for d in range(1, N_NEW):
        bsum = bsum + jnp.where(l_idx >= d, shift(lf, d), 0.0)
    b = pltpu.roll(bsum, LANES - N_HEADS, 1)
    gs = lf - b
    m0 = m0_ref[...]
    log_inter = b + m0
    logd = [jnp.where(l_idx >= d, b + shift(gs, d), -jnp.inf) for d in range(N_NEW)]
    m_t = log_inter
    for d in range(N_NEW):
        m_t = jnp.maximum(m_t, logd[d])
    inter = jnp.exp(log_inter - m_t)
    dm = [jnp.exp(logd[d] - m_t) for d in range(N_NEW)]
    emt = jnp.exp(-m_t)
    b_last = _tile_bcast(b, l_idx, N_NEW - 1)
    m_new = _tile_bcast(m_t, l_idx, N_NEW - 1)
    decay = jnp.exp(b_last + m0 - m_new)
    w = jnp.where(real, jnp.exp(b_last - b + lf - m_new), 0.0)
    m_ref[...] = m_t
    n0 = n0_ref[...]
    for h in range(N_HEADS):
        hs = slice(HEAD_D * h, HEAD_D * (h + 1))
        q = mq_ref[:, hs]
        k = mk_ref[:, hs] * (HEAD_D ** -0.5)
        v = mv_ref[:, hs]
        qb = q.astype(BF16)
        kb = k.astype(BF16)
        inter_c = col(inter, h)
        num = _per_seq(lambda g: _dot_nt(qb[R * g:R * (g + 1)], c0_ref[g, h].astype(BF16)), group) * inter_c
        den = inter_c * jnp.sum(q * n0[:, hs], axis=1, keepdims=True)
        for d in range(N_NEW):
            s_d = jnp.sum(q * shift(k, d), axis=1, keepdims=True) * col(dm[d], h)
            num = num + s_d * shift(v, d)
            den = den + s_d
        den = jnp.maximum(jnp.abs(den), col(emt, h))
        hml_ref[:, hs] = num / den
        w_c = col(w, h)
        dec_c = col(decay, h)
        vw = (v * w_c).astype(BF16)
        for g in range(group):
            rs = slice(R * g, R * (g + 1))
            c_ref[g, h] = dec_c[R * g:R * g + 1] * c0_ref[g, h] + _dot_tn(vw[rs], kb[rs])
        n_ref[:, hs] = dec_c * n0[:, hs] + _tile_total(k * w_c, l_idx)
        for g in range(h * group // N_HEADS, (h + 1) * group // N_HEADS):
            emit_window_buffers(g)

    cos, sin_signed = _rotary_tables((PAST_LEN + l_idx).astype(F32), inv_ref)
    for h in range(N_HEADS):
        hs = slice(HEAD_D * h, HEAD_D * (h + 1))
        lg = LOG_GAMMA[h]
        qr = _rotate(rq_ref[:, hs], cos, sin_signed)
        kr = _rotate(rk_ref[:, hs], cos, sin_signed) * (HEAD_D ** -0.5)
        v = rv_ref[:, hs]
        qrb = qr.astype(BF16)
        vb = v.astype(BF16)
        o = _per_seq(lambda g: _dot(qrb[R * g:R * (g + 1)], s0_ref[g, h].astype(BF16)), group)
        o = o * jnp.exp(lg * (l_f + 1.0))
        for d in range(N_NEW):
            s_d = jnp.sum(qr * shift(kr, d), axis=1, keepdims=True) * math.exp(lg * d)
            o = o + jnp.where(l_idx >= d, s_d, 0.0) * shift(v, d)
        ort_ref[:, hs] = o
        kd = (kr * jnp.where(real, jnp.exp(lg * (N_NEW - 1.0 - l_f)), 0.0)).astype(BF16)
        for g in range(group):
            rs = slice(R * g, R * (g + 1))
            s_ref[g, h] = math.exp(lg * N_NEW) * s0_ref[g, h] + _dot_tn(kd[rs], vb[rs])

    upper = lane >= SWA_D
    q_all = swq_ref[...] * (SWA_D ** -0.5)
    q_heads = []
    for h in range(SWA_HEADS):
        blk = q_all[:, LANES * (h // 2):LANES * (h // 2 + 1)]
        qh = jnp.where(upper if h % 2 == 1 else jnp.logical_not(upper), blk, 0.0)
        if h % 2 != h // (SWA_HEADS // 2):
            qh = pltpu.roll(qh, SWA_D, 1)
        q_heads.append(qh)
    qs = jnp.concatenate([q_heads[h][R * g:R * (g + 1)] for g in range(group) for h in range(SWA_HEADS)],
                         axis=0).astype(BF16)
    hr = SWA_HEADS * R
    s_prev = jnp.concatenate([_dot(qs[hr * g:hr * (g + 1)], ckt_ref[g].astype(BF16)) for g in range(group)],
                             axis=0) + tab_prev[...]
    s_cur = _dot_nt(qs, k_new.astype(BF16)) + tab_cur[...]
    sink64 = jnp.concatenate([jnp.full((R, 1), sinks_ref[h], F32) for h in range(SWA_HEADS)], axis=0)
    sink = jnp.concatenate([sink64] * group, axis=0)
    m = jnp.maximum(jnp.maximum(jnp.max(s_prev, axis=1, keepdims=True), jnp.max(s_cur, axis=1, keepdims=True)),
                    sink)
    p_prev = jnp.exp(s_prev - m)
    p_cur = jnp.exp(s_cur - m)
    norm = 1.0 / (jnp.sum(p_prev, axis=1, keepdims=True) + jnp.sum(p_cur, axis=1, keepdims=True)
                  + jnp.exp(sink - m))
    pb = p_prev.astype(BF16)
    o = jnp.concatenate([_dot_nt(pb[hr * g:hr * (g + 1)], cvt_ref[g].astype(BF16)) for g in range(group)],
                        axis=0)
    o = (o + _dot(p_cur.astype(BF16), v_new.astype(BF16))) * norm
    for g in range(group):
        blocks = []
        for j in range(SWA_HEADS // 2):
            pair = []
            for h in (2 * j, 2 * j + 1):
                oh = o[hr * g + R * h:hr * g + R * (h + 1)]
                if h % 2 != h // (SWA_HEADS // 2):
                    oh = pltpu.roll(oh, SWA_D, 1)
                pair.append(oh)
            blocks.append(jnp.where(upper, pair[1], pair[0]))
        osw_ref[R * g:R * (g + 1), :] = jnp.concatenate(blocks, axis=1)


def mixer_sample(z, layer, c0_all, n0_rows, m0_rows, s0_all, ckt_all, cvt_all, gb, inv, sinks, rb, prev):
    depth, nb = c0_all.shape[:2]
    group = SAMPLE_GROUP
    R = SEQ_PAD
    NR = group * R
    first_layer = prev is None
    assert first_layer == (layer == 0)

    def zspec(cb):
        return pl.BlockSpec((NR, 512), lambda i, cb=cb: (i, cb))

    def const(shape):
        return pl.BlockSpec(shape, lambda i: (0,) * len(shape))

    def rows(w):
        return pl.BlockSpec((NR, w), lambda i: (i, 0))

    def layered_in(shape):
        return pl.BlockSpec((None, group) + shape, lambda i: (layer, i) + (0,) * len(shape))

    def layered_out(shape):
        if first_layer:
            return pl.BlockSpec((depth, group) + shape, lambda i: (0, i) + (0,) * len(shape))
        return layered_in(shape)

    smem = pl.BlockSpec(memory_space=pltpu.SMEM)
    st = (N_HEADS, HEAD_D, HEAD_D)
    buf = (LANES, CHUNK)
    n_in = 18
    if first_layer:
        extra_specs, extra_args, aliases = [], [], {}
    else:
        extra_specs = [pl.BlockSpec(memory_space=pl.ANY)] * 4
        extra_args = list(prev)
        aliases = {n_in + i: 5 + i for i in range(4)}
    return pl.pallas_call(
        functools.partial(_sample_mixers_kernel, group=group, first_layer=first_layer),
        grid=(nb // group,),
        in_specs=[zspec(ZC_ML_Q), zspec(ZC_ML_K), zspec(ZC_ML_V), zspec(ZC_SW_Q), zspec(ZC_MISC),
                  zspec(ZC_RT_Q), zspec(ZC_RT_K), zspec(ZC_RT_V),
                  layered_in(st), rows(512), rows(LANES), layered_in(st),
                  layered_in(buf), layered_in(buf),
                  const((1, LANES)), const((1, LANES)), smem, smem] + extra_specs,
        out_specs=[rows(512), rows(512), rows(512), rows(512), rows(LANES),
                   layered_out(st), layered_out(st), layered_out(buf), layered_out(buf)],
        out_shape=[jax.ShapeDtypeStruct((nb * R, 512), F32)] * 4
        + [jax.ShapeDtypeStruct((nb * R, LANES), F32),
           jax.ShapeDtypeStruct((depth, nb) + st, F32), jax.ShapeDtypeStruct((depth, nb) + st, F32),
           jax.ShapeDtypeStruct((depth, nb) + buf, F32), jax.ShapeDtypeStruct((depth, nb) + buf, F32)],
        scratch_shapes=[pltpu.VMEM((group * SWA_HEADS * R, LANES), F32),
                        pltpu.VMEM((group * SWA_HEADS * R, LANES), F32)],
        input_output_aliases=aliases,
        compiler_params=_params(1),
        name="mixer_sample",
    )(z, z, z, z, z, z, z, z, c0_all, n0_rows, m0_rows, s0_all, ckt_all, cvt_all, gb, inv, sinks, rb, *extra_args)


def _mixer_post_kernel(hml_ref, osw_ref, ort_ref, mlo_ref, rtg_ref, g0_ref, g1_ref, g2_ref, x_ref,
                       mlg_ref, retg_ref, ng_ref, wml_ref, wsw_ref, wrt_ref, wout_ref, o_ref):
    o_ref[...] = _post_math(hml_ref[...], osw_ref[...], ort_ref[...], mlo_ref[...], rtg_ref[...],
                            g0_ref[...], g1_ref[...], g2_ref[...], x_ref[...],
                            mlg_ref[...], retg_ref[...], ng_ref[...],
                            wml_ref[...], wsw_ref[...], wrt_ref[...], wout_ref[...])


def mixer_post(hml, osw, ort, z, x, mlg, retg, ng, layer, wml, wsw, wrt, wout, tm, name):
    m = x.shape[0]

    def tok(w, cb=0):
        return pl.BlockSpec((tm, w), lambda i, cb=cb: (i, cb))

    def const(shape):
        return pl.BlockSpec(shape, lambda i: (0,) * len(shape))

    return pl.pallas_call(
        _mixer_post_kernel,
        grid=(m // tm,),
        in_specs=[tok(512), tok(512), tok(512), tok(512, ZC_ML_O), tok(512, ZC_RT_G),
                  tok(1024, ZC_GATES), tok(1024, ZC_GATES + 1), tok(1024, ZC_GATES + 2), tok(D_MODEL),
                  const((1, 512)), const((1, 512)), const((1, D_MODEL)),
                  _layer_weight(wml, layer), _layer_weight(wsw, layer), _layer_weight(wrt, layer),
                  _layer_weight(wout, layer)],
        out_specs=tok(D_MODEL),
        out_shape=jax.ShapeDtypeStruct((m, D_MODEL), F32),
        compiler_params=_params(1),
        name=name,
    )(hml, osw, ort, z, z, z, z, z, x, mlg, retg, ng, wml, wsw, wrt, wout)


def _xattn_head(q, k, v, mask):

    s = _dot_nt(jnp.where(mask, q, 0.0).astype(BF16), k) * (X_D ** -0.5)
    m = jnp.max(s, axis=1, keepdims=True)
    p = jnp.exp(s - m)
    p = p / jnp.sum(p, axis=1, keepdims=True)
    return jnp.where(mask, _dot(p.astype(BF16), v), 0.0)


def _xattn_ffn_prompt_kernel(x_ref, gxi_ref, gxo_ref, wcq_ref, kv_ref, wco_ref, gfi_ref, gfo_ref, wgu_ref, wd_ref,
                             o_ref, mid_scr):
    @pl.when(pl.program_id(0) == 0)
    def _():
        mid_scr[...] = jnp.zeros(mid_scr.shape, F32)

    x_mid = mid_scr[...]
    x = x_ref[...]
    lane = lax.broadcasted_iota(jnp.int32, (1, X_HEADS * X_D), 1)
    masks = [(lane >= X_D * h) & (lane < X_D * (h + 1)) for h in range(X_HEADS)]
    fc = D_FF // FFN_SPLIT
    heads_per_piece = X_HEADS // FFN_SPLIT

    u = _rms(x_mid, gfi_ref[...]).astype(BF16)
    q = _dot(_rms(x, gxi_ref[...]).astype(BF16), wcq_ref[...])
    k = kv_ref[:, 0:X_HEADS * X_D].astype(BF16)
    v = kv_ref[:, X_HEADS * X_D:2 * X_HEADS * X_D].astype(BF16)
    att = jnp.zeros(q.shape, F32)
    acc = None
    for c in range(FFN_SPLIT):
        g = _dot(u, wgu_ref[:, fc * c:fc * (c + 1)])
        up = _dot(u, wgu_ref[:, D_FF + fc * c:D_FF + fc * (c + 1)])
        for h in range(heads_per_piece * c, heads_per_piece * (c + 1)):
            att = att + _xattn_head(q, k, v, masks[h])
        hid = (g * _sigmoid(g) * up).astype(BF16)
        part = _dot(hid, wd_ref[fc * c:fc * (c + 1), :])
        acc = part if acc is None else acc + part
    o_ref[...] = x_mid + _rms(acc, gfo_ref[...])
    mid_scr[...] = x + _rms(_dot(att.astype(BF16), wco_ref[...]), gxo_ref[...])


def xattn_ffn_prompt(x, gxi, gxo, gfi, gfo, layer, wcq, kv, wco, wgu, wd, tm):
    m = x.shape[0]
    n = m // tm

    def const(shape):
        return pl.BlockSpec(shape, lambda s: (0,) * len(shape))

    gain = const((1, D_MODEL))
    return pl.pallas_call(
        _xattn_ffn_prompt_kernel,
        grid=(n + 1,),
        in_specs=[pl.BlockSpec((tm, D_MODEL), lambda s: (jnp.minimum(s, n - 1), 0)),
                  gain, gain, _layer_weight(wcq, layer), const(kv.shape), _layer_weight(wco, layer),
                  gain, gain, _layer_weight(wgu, layer), _layer_weight(wd, layer)],
        out_specs=pl.BlockSpec((tm, D_MODEL), lambda s: (jnp.maximum(s - 1, 0), 0)),
        out_shape=jax.ShapeDtypeStruct((m, D_MODEL), F32),
        scratch_shapes=[pltpu.VMEM((tm, D_MODEL), F32)],
        compiler_params=_params(1),
        name="xattn_ffn_prompt",
    )(x, gxi, gxo, wcq, kv, wco, gfi, gfo, wgu, wd)


def _xattn_sample_kernel(x_ref, gin_ref, gout_ref, wcq_ref, kt_ref, vt_ref, wco_ref, o_ref, q_scr, a_scr,
                         *, group):
    R = SEQ_PAD
    x = x_ref[...]
    q_scr[...] = _dot(_rms(x, gin_ref[...]).astype(BF16), wcq_ref[...])
    lane = lax.broadcasted_iota(jnp.int32, (1, X_HEADS * X_D), 1)
    masks = [(lane >= X_D * h) & (lane < X_D * (h + 1)) for h in range(X_HEADS)]

    def body(g, carry):
        rows = pl.ds(pl.multiple_of(g * R, R), R)
        q = q_scr[rows, :]
        qs = jnp.concatenate([jnp.where(mk, q, 0.0) for mk in masks], axis=0)
        s = _dot(qs.astype(BF16), kt_ref[g].astype(BF16)) * (X_D ** -0.5)
        m = jnp.max(s, axis=1, keepdims=True)
        p = jnp.exp(s - m)
        p = p / jnp.sum(p, axis=1, keepdims=True)
        o = _dot_nt(p.astype(BF16), vt_ref[g].astype(BF16))
        acc = jnp.zeros((R, X_HEADS * X_D), F32)
        for h in range(X_HEADS):
            acc = acc + jnp.where(masks[h], o[R * h:R * (h + 1)], 0.0)
        a_scr[rows, :] = acc
        return carry

    lax.fori_loop(0, group, body, 0, unroll=4)
    o_ref[...] = x + _rms(_dot(a_scr[...].astype(BF16), wco_ref[...]), gout_ref[...])


def xattn_sample(x, gin, gout, wcq, layer, mkt, mvt, wco, group=16):
    nb = mkt.shape[1]
    R = SEQ_PAD

    def const(shape):
        return pl.BlockSpec(shape, lambda i: (0,) * len(shape))

    tok = pl.BlockSpec((group * R, D_MODEL), lambda i: (i, 0))
    mem = pl.BlockSpec((None, group, X_HEADS * X_D, N_MEM), lambda i: (layer, i, 0, 0))
    return pl.pallas_call(
        functools.partial(_xattn_sample_kernel, group=group),
        grid=(nb // group,),
        in_specs=[tok, const((1, D_MODEL)), const((1, D_MODEL)), _layer_weight(wcq, layer), mem, mem,
                  _layer_weight(wco, layer)],
        out_specs=tok,
        out_shape=jax.ShapeDtypeStruct((nb * R, D_MODEL), F32),
        scratch_shapes=[pltpu.VMEM((group * R, X_HEADS * X_D), F32),
                        pltpu.VMEM((group * R, X_HEADS * X_D), F32)],
        compiler_params=_params(1),
        name="xattn_sample",
    )(x, gin, gout, wcq, mkt, mvt, wco)


FFN_SPLIT = 2


def _ffn_kernel(x_ref, gin_ref, gout_ref, wgu_ref, wd_ref, o_ref):
    x = x_ref[...]
    u = _rms(x, gin_ref[...]).astype(BF16)
    fc = D_FF // FFN_SPLIT
    acc = None
    for c in range(FFN_SPLIT):
        g = _dot(u, wgu_ref[:, fc * c:fc * (c + 1)])
        up = _dot(u, wgu_ref[:, D_FF + fc * c:D_FF + fc * (c + 1)])
        h = (g * _sigmoid(g) * up).astype(BF16)
        part = _dot(h, wd_ref[fc * c:fc * (c + 1), :])
        acc = part if acc is None else acc + part
    o_ref[...] = x + _rms(acc, gout_ref[...])


def ffn(x, gin, gout, layer, wgu, wd, tm, name):
    m = x.shape[0]

    def const(shape):
        return pl.BlockSpec(shape, lambda i: (0,) * len(shape))

    tok = pl.BlockSpec((tm, D_MODEL), lambda i: (i, 0))
    return pl.pallas_call(
        _ffn_kernel,
        grid=(m // tm,),
        in_specs=[tok, const((1, D_MODEL)), const((1, D_MODEL)),
                  _layer_weight(wgu, layer), _layer_weight(wd, layer)],
        out_specs=tok,
        out_shape=jax.ShapeDtypeStruct((m, D_MODEL), F32),
        compiler_params=_params(1),
        name=name,
    )(x, gin, gout, wgu, wd)


def _reorder_w_in_t(w):
    wt = jnp.swapaxes(w, 1, 2)
    sizes = (512, 512, 512, 4, 4, 512, 512, 128, 128, 512, 512, 512, 512, 3072)
    offs = np.concatenate([[0], np.cumsum(sizes)])
    (ml_q, ml_k, ml_v, ml_i, ml_f, ml_o, sw_q, sw_k, sw_v, rt_q, rt_k, rt_v, rt_g, gates) = [
        wt[:, int(offs[i]):int(offs[i + 1])] for i in range(len(sizes))]
    pad = jnp.zeros((w.shape[0], 512 - 128 - 128 - 8, w.shape[1]), w.dtype)
    out = jnp.concatenate([ml_q, ml_k, ml_v, ml_o, sw_q, sw_k, sw_v, ml_i, ml_f, pad,
                           rt_q, rt_k, rt_v, rt_g, gates], axis=1)
    assert out.shape[1] == Z_COLS
    return out.astype(BF16)


def _row(v):
    return v.reshape(1, -1).astype(F32)


def _decoder_layer(x, z_fn, mixer_fn, xattn_ffn_fn):
    z = z_fn(x)
    x, states = mixer_fn(z, x)
    return xattn_ffn_fn(x), z, states


def kernel(x_prompt, x_sample, mem_prompt, state_mlstm_C, state_mlstm_n, state_mlstm_m, state_ret_S,
           cache_swa_k, cache_swa_v, cache_mem_k, cache_mem_v, norm_g, w_in, ml_gate_bias, ml_head_g,
           ret_head_g, swa_sinks, rel_bias, w_br_ml, w_br_swa, w_br_ret, w_out, w_cq, w_mkv, w_co,
           w_gu, w_down):
    depth = w_in.shape[0]
    bp, t, d = x_prompt.shape
    assert bp == 1 and d == D_MODEL and t % 1024 == 0
    nb, n_new, _ = x_sample.shape
    assert n_new == N_NEW
    R = SEQ_PAD

    xp = x_prompt.reshape(t, d)
    xs = jnp.pad(x_sample, ((0, 0), (0, R - n_new), (0, 0))).reshape(nb * R, d)
    mem = mem_prompt.reshape(N_MEM, d)
    half = HEAD_D // 2
    inv = ROPE_BASE ** (-jnp.arange(half, dtype=F32) / half)
    inv = jnp.concatenate([inv, inv]).reshape(1, LANES)
    rb = rel_bias.astype(F32).reshape(-1)
    mem_kt = jnp.transpose(cache_mem_k, (0, 1, 3, 4, 2)).reshape(depth, nb, X_HEADS * X_D, N_MEM)
    mem_vt = jnp.transpose(cache_mem_v, (0, 1, 3, 4, 2)).reshape(depth, nb, X_HEADS * X_D, N_MEM)

    win_t = _reorder_w_in_t(w_in)
    wml, wsw, wrt, wout = (w.astype(BF16) for w in (w_br_ml, w_br_swa, w_br_ret, w_out))
    wcq, wmkv, wco, wgu, wd = (w.astype(BF16) for w in (w_cq, w_mkv, w_co, w_gu, w_down))

    outs_p = {k: [] for k in ("C", "n", "m", "S", "k", "v", "mk", "mv")}
    outs_s = {k: [] for k in ("n", "m")}
    sample_layered = None
    swa_kt = jnp.transpose(cache_swa_k, (0, 1, 3, 4, 2)).reshape(depth, nb, LANES, CHUNK)
    swa_vt = jnp.transpose(cache_swa_v, (0, 1, 3, 4, 2)).reshape(depth, nb, LANES, CHUNK)
    for l in range(depth):
        ng = [_row(norm_g[l, i]) for i in range(7)]
        mlg, retg = _row(ml_head_g[l]), _row(ret_head_g[l])
        gb = jnp.concatenate([ml_gate_bias[l, 0], ml_gate_bias[l, 1],
                              jnp.zeros((LANES - 2 * N_HEADS,), F32)]).reshape(1, LANES).astype(F32)
        gb_rows = jnp.broadcast_to(ml_gate_bias[l].astype(F32).reshape(2 * N_HEADS, 1), (2 * N_HEADS, LANES))
        sinks = swa_sinks[l].astype(F32)

        kv = norm_matmul(mem, ng[6], wmkv, l, tm=N_MEM, tn=2 * X_HEADS * X_D, name="memory_kv")

        def seq_p(z, x):
            x1, c_, n_, m_, s_ = mixer_prompt(z, x, gb_rows, inv, sinks, rb, mlg, retg, ng[1], l,
                                              wml, wsw, wrt, wout)
            return x1, (c_, n_, m_, s_)

        xp, zp, (c_, n_, m_, s_) = _decoder_layer(
            xp,
            lambda x: norm_matmul(x, ng[0], win_t, l, tm=min(IN_PROJ_TM, t), tn=1024, name="in_proj_prompt",
                                  w_transposed=True),
            seq_p,
            lambda x: xattn_ffn_prompt(x, ng[2], ng[3], ng[4], ng[5], l, wcq, kv, wco, wgu, wd, tm=512))
        outs_p["C"].append(c_.reshape(1, N_HEADS, HEAD_D, HEAD_D))
        outs_p["n"].append(n_.reshape(1, N_HEADS, HEAD_D))
        outs_p["m"].append(m_[:, :N_HEADS])
        outs_p["S"].append(s_.reshape(1, N_HEADS, HEAD_D, HEAD_D))
        misc_last = zp[t - CHUNK:, 512 * ZC_MISC:512 * ZC_MISC + 2 * LANES]
        outs_p["k"].append(misc_last[:, :LANES].reshape(1, CHUNK, 2, SWA_D))
        outs_p["v"].append(misc_last[:, LANES:].reshape(1, CHUNK, 2, SWA_D))
        outs_p["mk"].append(kv[:, :X_HEADS * X_D].reshape(1, N_MEM, X_HEADS, X_D))
        outs_p["mv"].append(kv[:, X_HEADS * X_D:].reshape(1, N_MEM, X_HEADS, X_D))

        n0_rows = jnp.repeat(state_mlstm_n[l].astype(F32).reshape(nb, N_HEADS * HEAD_D), R, axis=0)
        m0_rows = jnp.repeat(jnp.pad(state_mlstm_m[l].astype(F32), ((0, 0), (0, LANES - N_HEADS))), R, axis=0)

        def seq_s(z, x):
            hml, osw, ort, n_, m_, *layered = mixer_sample(
                z, l, state_mlstm_C, n0_rows, m0_rows, state_ret_S, swa_kt, swa_vt, gb, inv, sinks, rb,
                sample_layered)
            x1 = mixer_post(hml, osw, ort, z, x, mlg, retg, ng[1], l, wml, wsw, wrt, wout,
                            tm=min(256, x.shape[0]), name="mixer_post_sample")
            return x1, (n_, m_, layered)

        xs, zs, (n_, m_, sample_layered) = _decoder_layer(
            xs,
            lambda x: norm_matmul(x, ng[0], win_t, l, tm=min(1024, nb * R), tn=1024, name="in_proj_sample",
                                  w_transposed=True),
            seq_s,
            lambda x: ffn(xattn_sample(x, ng[2], ng[3], wcq, l, mem_kt, mem_vt, wco),
                          ng[4], ng[5], l, wgu, wd, tm=min(512, nb * R), name="ffn_sample"))
        outs_s["n"].append(n_.reshape(nb, R, N_HEADS, HEAD_D)[:, R - 1])
        outs_s["m"].append(m_.reshape(nb, R, LANES)[:, n_new - 1, :N_HEADS])

    sample_c, sample_s, swa_kt_new, swa_vt_new = sample_layered
    s_swa_k = jnp.transpose(swa_kt_new.reshape(depth, nb, 2, SWA_D, CHUNK), (0, 1, 4, 2, 3))
    s_swa_v = jnp.transpose(swa_vt_new.reshape(depth, nb, 2, SWA_D, CHUNK), (0, 1, 4, 2, 3))
    y_p = xp.reshape(1, t, d)
    y_s = xs.reshape(nb, R, d)[:, :n_new]
    st = lambda d_, k: jnp.stack(d_[k])
    return (y_p, y_s,
            st(outs_p, "C"), st(outs_p, "n"), st(outs_p, "m"), st(outs_p, "S"),
            st(outs_p, "k"), st(outs_p, "v"), st(outs_p, "mk"), st(outs_p, "mv"),
            sample_c, st(outs_s, "n"), st(outs_s, "m"), sample_s, s_swa_k, s_swa_v)
```

```python
import functools
import math

import numpy as np
import jax
import jax.numpy as jnp
from jax import lax
from jax.experimental import pallas as pl
from jax.experimental.pallas import tpu as pltpu

F32 = jnp.float32
BF16 = jnp.bfloat16

D_MODEL = 1024
EPS = 1e-6
PAST_LEN = 16384
HEAD_D = 128
N_HEADS = 4
CHUNK = 128
SWA_HEADS = 8
SWA_D = 64
N_BUCKETS = 32
MAX_DISTANCE = 128
ROPE_BASE = 10000.0
N_MEM = 256
X_HEADS = 4
X_D = 64
D_FF = 2816
SEQ_PAD = 8
N_NEW = 4
LANES = 128
IN_PROJ_TM = 2048
MXU_MIN_ROWS = 16
VMEM_LIMIT = 48 * 1024 * 1024

ZC_ML_Q, ZC_ML_K, ZC_ML_V, ZC_SW_Q, ZC_MISC, ZC_RT_Q, ZC_RT_K, ZC_RT_V, ZC_ML_O, ZC_RT_G = range(10)
ZC_GATES = 5
Z_COLS = 8192
Z_HALF = Z_COLS // 2
MISC_K, MISC_V, MISC_IF = 0, 128, 256

LOG2E = math.log2(math.e)
LN2 = math.log(2.0)
LOG_GAMMA = tuple(float(v) for v in np.log1p(-np.exp2(-5.0 - np.arange(N_HEADS, dtype=np.float32))))

SWA_SAME = (0, 2, 5, 7)
SWA_SWAP = (1, 3, 4, 6)


def _t5_bucket_bounds():
    n = np.arange(CHUNK)
    max_exact = N_BUCKETS // 2
    nf = np.maximum(n, 1).astype(np.float32)
    large = max_exact + (np.log(nf / np.float32(max_exact)) / np.float32(math.log(MAX_DISTANCE / max_exact))
                         * np.float32(N_BUCKETS - max_exact)).astype(np.int32)
    large = np.minimum(large, N_BUCKETS - 1)
    b = np.where(n < max_exact, n, large)
    assert np.all(np.diff(b) >= 0)
    runs = []
    for d in range(CHUNK):
        if runs and runs[-1][1] == int(b[d]):
            runs[-1][0] = d + 1
        else:
            runs.append([d + 1, int(b[d])])
    return tuple((hi, bk) for hi, bk in runs)


BUCKET_RUNS = _t5_bucket_bounds()


def _dot(a, b):
    return jnp.dot(a, b, preferred_element_type=F32)


def _dot_nt(a, b):
    return lax.dot_general(a, b, (((1,), (1,)), ((), ())), preferred_element_type=F32)


def _dot_tn(a, b):
    return lax.dot_general(a, b, (((0,), (0,)), ((), ())), preferred_element_type=F32)


def _rms(x, g):
    return x * lax.rsqrt(jnp.mean(x * x, axis=-1, keepdims=True) + EPS) * g


def _sigmoid(x):
    return 1.0 / (1.0 + jnp.exp2(x * (-LOG2E)))


def _log_sigmoid(x):
    return jnp.minimum(x, 0.0) - jnp.log1p(jnp.exp(-jnp.abs(x)))


def _layer_weight(w_all, layer):
    shape = w_all.shape[1:]
    return pl.BlockSpec((None,) + shape, lambda *_: (layer,) + (0,) * len(shape), pipeline_mode=pl.Buffered(1))


def _params(n_grid):
    return pltpu.CompilerParams(dimension_semantics=("arbitrary",) * n_grid, vmem_limit_bytes=VMEM_LIMIT)


def _norm_matmul_kernel(x_ref, g_ref, w_ref, o_ref, u_ref, *, w_transposed):
    @pl.when(pl.program_id(1) == 0)
    def _():
        u_ref[...] = _rms(x_ref[...], g_ref[...]).astype(BF16)

    o_ref[...] = _dot_nt(u_ref[...], w_ref[...]) if w_transposed else _dot(u_ref[...], w_ref[...])


def norm_matmul(x, g, w_all, layer, tm, tn, name, w_transposed=False):
    m, k = x.shape
    n = w_all.shape[1] if w_transposed else w_all.shape[2]
    if w_transposed:
        w_spec = pl.BlockSpec((None, tn, k), lambda i, j: (layer, j, 0))
    else:
        w_spec = pl.BlockSpec((None, k, tn), lambda i, j: (layer, 0, j))
    return pl.pallas_call(
        functools.partial(_norm_matmul_kernel, w_transposed=w_transposed),
        grid=(m // tm, n // tn),
        in_specs=[pl.BlockSpec((tm, k), lambda i, j: (i, 0)),
                  pl.BlockSpec((1, k), lambda i, j: (0, 0)),
                  w_spec],
        out_specs=pl.BlockSpec((tm, tn), lambda i, j: (i, j)),
        out_shape=jax.ShapeDtypeStruct((m, n), F32),
        scratch_shapes=[pltpu.VMEM((tm, k), BF16)],
        compiler_params=_params(2),
        name=name,
    )(x, g, w_all)


def _transpose_bf16(x, eye):
    return _dot_nt(eye, x).astype(BF16)


def _build_swa_table_t(tab_ref, rb_ref):
    L = CHUNK
    srow = lax.broadcasted_iota(jnp.int32, (2 * L, L), 0)
    lcol = lax.broadcasted_iota(jnp.int32, (2 * L, L), 1)
    delta = lcol + L - srow
    valid = (delta >= 0) & (delta < L)
    for var, heads in enumerate((SWA_SAME, SWA_SWAP)):
        for i, h in enumerate(heads):
            val = jnp.full((2 * L, L), rb_ref[BUCKET_RUNS[-1][1] * SWA_HEADS + h], F32)
            for hi, bk in reversed(BUCKET_RUNS[:-1]):
                val = jnp.where(delta < hi, rb_ref[bk * SWA_HEADS + h], val)
            tab_ref[var, :, i * L:(i + 1) * L] = jnp.where(valid, val, -jnp.inf)


def _swa_scores_t(q, k_same, k_swap, tab_ref, prev_invalid):
    L = CHUNK
    lane = lax.broadcasted_iota(jnp.int32, (1, LANES), 1)
    upper = lane >= SWA_D
    q = q * (SWA_D ** -0.5)
    ks = (k_same[...], k_swap[...])
    scores = []
    for var, heads in enumerate((SWA_SAME, SWA_SWAP)):
        qm = []
        for h in heads:
            blk = q[:, LANES * (h // 2):LANES * (h // 2 + 1)]
            qm.append(jnp.where(upper if h % 2 == 1 else jnp.logical_not(upper), blk, 0.0))
        qs = jnp.concatenate(qm, axis=0).astype(BF16)
        s_t = _dot_nt(ks[var], qs) + tab_ref[var]
        scores.append(jnp.concatenate([jnp.where(prev_invalid, -jnp.inf, s_t[:L]), s_t[L:]], axis=0))
    return scores


def _swa_softmax_t(scores, sinks_ref):
    probs = []
    for s_t, heads in zip(scores, (SWA_SAME, SWA_SWAP)):
        sink = jnp.concatenate([jnp.full((1, CHUNK), sinks_ref[h], F32) for h in heads], axis=1)
        m = jnp.maximum(jnp.max(s_t, axis=0, keepdims=True), sink)
        p = jnp.exp(s_t - m)
        norm = 1.0 / (jnp.sum(p, axis=0, keepdims=True) + jnp.exp(sink - m))
        probs.append((p.astype(BF16), norm))
    return probs


def _swa_output_t(probs, vt_buf):
    L = CHUNK
    lane = lax.broadcasted_iota(jnp.int32, (1, LANES), 1)
    upper = lane >= SWA_D
    vt = vt_buf[...]
    vts = (vt, jnp.concatenate([vt[SWA_D:], vt[:SWA_D]], axis=0))
    outs = [None] * SWA_HEADS
    for var, heads in enumerate((SWA_SAME, SWA_SWAP)):
        p, norm = probs[var]
        o_t = _dot(vts[var], p) * norm
        for i, h in enumerate(heads):
            outs[h] = o_t[:, i * L:(i + 1) * L].T
    blocks = [jnp.where(upper, outs[2 * j + 1], outs[2 * j]) for j in range(SWA_HEADS // 2)]
    return jnp.concatenate(blocks, axis=1)


def _rotary_tables(pos, inv_ref):
    lane = lax.broadcasted_iota(jnp.int32, (1, LANES), 1)
    ang = pos * inv_ref[...]
    sin = jnp.sin(ang)
    return jnp.cos(ang), jnp.where(lane < HEAD_D // 2, -sin, sin)


def _rotate(x, cos, sin_signed):
    return x * cos + pltpu.roll(x, HEAD_D // 2, 1) * sin_signed


def _gate_slab(raw, gb_ref):
    lane = lax.broadcasted_iota(jnp.int32, (1, LANES), 1)
    x = raw + gb_ref[...]
    return jnp.where((lane >= N_HEADS) & (lane < 2 * N_HEADS), _log_sigmoid(x), x)


def _head_rms(h, gain):
    parts = []
    for i in range(N_HEADS):
        blk = h[:, HEAD_D * i:HEAD_D * (i + 1)]
        parts.append(blk * lax.rsqrt(jnp.mean(blk * blk, axis=-1, keepdims=True) + EPS))
    return jnp.concatenate(parts, axis=-1) * gain


def _rms_over_rows(h_t):
    return h_t * lax.rsqrt(jnp.mean(h_t * h_t, axis=0, keepdims=True) + EPS)


def _post_math(hml, osw, ort, mlo, rtg, g0, g1, g2, x, mlg, retg, ng, wml, wsw, wrt, wout, head_normed=False):
    if not head_normed:
        hml = _head_rms(hml, mlg)
        ort = _head_rms(ort, retg)
    hm = hml * _sigmoid(mlo)
    y_ml = _dot(hm.astype(BF16), wml)
    y_sw = _dot(osw.astype(BF16), wsw)
    rt = ort * (rtg * _sigmoid(rtg))
    y_rt = _dot(rt.astype(BF16), wrt)
    merged = _sigmoid(g0) * y_ml + _sigmoid(g1) * y_sw + _sigmoid(g2) * y_rt
    return x + _rms(_dot(merged.astype(BF16), wout), ng)


def _z_views(z_half_ref, first_col_block):
    return {first_col_block + i: z_half_ref.at[:, 512 * i:512 * (i + 1)] for i in range(Z_HALF // 512)}


def _mixer_seq_prompt_kernel(zmix_ref, ztail_ref, x_ref,
                             gbr_ref, inv_ref, sinks_ref, rb_ref,
                             mlg_ref, retg_ref, ng_ref, wml_ref, wsw_ref, wrt_ref, wout_ref,
                             x1_ref, c_ref, n_ref, m_ref, s_ref,
                             st_scr, k_same, k_swap, vt_buf, tab_t, cos_l, sin_l, dec_in_t, eye_ref,
                             hml_s, osw_s, ort_s):
    zm = _z_views(zmix_ref, 0)
    mq_ref, mk_ref, mv_ref, swq_ref, misc_ref = (zm[i] for i in (ZC_ML_Q, ZC_ML_K, ZC_ML_V, ZC_SW_Q, ZC_MISC))
    rq_ref, rk_ref, rv_ref = (zm[i] for i in (ZC_RT_Q, ZC_RT_K, ZC_RT_V))
    zt = _z_views(ztail_ref, Z_HALF // 512)
    mlo_ref, rtg_ref = zt[ZC_ML_O], zt[ZC_RT_G]
    g0_ref, g1_ref, g2_ref = (ztail_ref.at[:, 1024 * (ZC_GATES + i) - Z_HALF:1024 * (ZC_GATES + i + 1) - Z_HALF]
                              for i in range(3))
    c = pl.program_id(0)
    active = c < pl.num_programs(0) - 1
    L = CHUNK
    lane = lax.broadcasted_iota(jnp.int32, (1, LANES), 1)
    row = lax.broadcasted_iota(jnp.int32, (L, L), 0)
    col = lax.broadcasted_iota(jnp.int32, (L, L), 1)
    causal_t = row <= col

    @pl.when(c == 0)
    def _init():
        c_ref[...] = jnp.zeros(c_ref.shape, F32)
        n_ref[...] = jnp.zeros(n_ref.shape, F32)
        m_ref[...] = jnp.zeros(m_ref.shape, F32)
        s_ref[...] = jnp.zeros(s_ref.shape, F32)
        hml_s[...] = jnp.zeros(hml_s.shape, F32)
        osw_s[...] = jnp.zeros(osw_s.shape, F32)
        ort_s[...] = jnp.zeros(ort_s.shape, F32)
        st_scr[...] = jnp.zeros(st_scr.shape, F32)
        k_same[...] = jnp.zeros(k_same.shape, BF16)
        k_swap[...] = jnp.zeros(k_swap.shape, BF16)
        vt_buf[...] = jnp.zeros(vt_buf.shape, BF16)
        _build_swa_table_t(tab_t, rb_ref)
        ang = row.astype(F32) * inv_ref[...]
        cos_l[...] = jnp.cos(ang)
        sin_l[...] = jnp.sin(ang)
        rel_t = (col - row).astype(F32)
        for h in range(N_HEADS):
            dec_in_t[h] = jnp.where(causal_t, jnp.exp(LOG_GAMMA[h] * rel_t), 0.0)
        eye_ref[...] = jnp.where(row == col, 1.0, 0.0).astype(BF16)

    eye = eye_ref[...]
    heads = range(N_HEADS)
    hsl = [slice(HEAD_D * h, HEAD_D * (h + 1)) for h in heads]
    sub8 = lax.broadcasted_iota(jnp.int32, (2 * N_HEADS, 1), 0)
    hml_prev, osw_prev, ort_prev = hml_s[...], osw_s[...], ort_s[...]

    raw = misc_ref[:, MISC_IF:MISC_IF + LANES].T[0:2 * N_HEADS] + gbr_ref[...]
    gates = jnp.where(sub8 >= N_HEADS, _log_sigmoid(raw), raw)
    cum = gates
    for sh in (1, 2, 4, 8, 16, 32, 64):
        cum = cum + jnp.where(lane >= sh, pltpu.roll(cum, sh, 1), 0.0)
    g8 = jnp.where(sub8 < N_HEADS, gates - pltpu.roll(cum, N_HEADS, 0), 0.0)
    g_cols = jnp.concatenate([g8, jnp.zeros((L - 2 * N_HEADS, L), F32)], axis=0).T

    ang0 = (c * L).astype(F32) * inv_ref[...]
    cos0 = jnp.cos(ang0)
    sin0 = jnp.sin(ang0)
    cos_t = cos_l[...]
    sin_t = sin_l[...]
    cos = cos0 * cos_t - sin0 * sin_t
    sin = sin0 * cos_t + cos0 * sin_t
    sin_signed = jnp.where(lane < HEAD_D // 2, -sin, sin)
    ml_ops, rt_ops = [], []
    for h in heads:
        qb = mq_ref[:, hsl[h]].astype(BF16)
        kb = (mk_ref[:, hsl[h]] * (HEAD_D ** -0.5)).astype(BF16)
        vt = _transpose_bf16(mv_ref[:, hsl[h]].astype(BF16), eye)
        ml_ops.append((qb, kb, vt, _dot_nt(kb, qb)))
    for h in heads:
        qr = _rotate(rq_ref[:, hsl[h]], cos, sin_signed).astype(BF16)
        kr = (_rotate(rk_ref[:, hsl[h]], cos, sin_signed) * (HEAD_D ** -0.5)).astype(BF16)
        vt = _transpose_bf16(rv_ref[:, hsl[h]].astype(BF16), eye)
        rt_ops.append((qr, kr, vt, _dot_nt(kr, qr)))
    k_new = misc_ref[:, MISC_K:MISC_K + LANES]
    k_same[0:L, :] = k_same[L:2 * L, :]
    k_swap[0:L, :] = k_swap[L:2 * L, :]
    k_same[L:2 * L, :] = k_new.astype(BF16)
    k_swap[L:2 * L, :] = pltpu.roll(k_new, SWA_D, 1).astype(BF16)
    vt_buf[:, 0:L] = vt_buf[:, L:2 * L]
    vt_buf[:, L:2 * L] = _transpose_bf16(misc_ref[:, MISC_V:MISC_V + LANES].astype(BF16), eye)
    swa_scores = _swa_scores_t(swq_ref[...], k_same, k_swap, tab_t, c == 0)

    y_ml = _dot((hml_prev * _sigmoid(mlo_ref[...])).astype(BF16), wml_ref[...])

    m_all = m_ref[...]
    gate_ops = []
    for h in heads:
        m_prev = m_all[:, h:h + 1]
        gm = jnp.where(causal_t, g_cols[:, h:h + 1], -jnp.inf)
        mx = jnp.maximum(jnp.max(gm, axis=0, keepdims=True), m_prev)
        gate_ops.append((m_prev, mx, jnp.exp(m_prev - mx), jnp.exp(gm - mx)))

    y_sw = _dot(osw_prev.astype(BF16), wsw_ref[...])
    rtg = rtg_ref[...]
    y_rt = _dot((ort_prev * (rtg * _sigmoid(rtg))).astype(BF16), wrt_ref[...])

    l_row = lane.astype(F32)
    ml_out, rt_out = [], []
    for h in heads:
        qb, kb, vt, qk = ml_ops[h]
        m_prev, mx, inter, dm_t = gate_ops[h]
        s_t = qk * dm_t
        num_t = _dot(vt, s_t.astype(BF16)) + _dot_nt(c_ref[h].astype(BF16), qb) * inter
        nq = _dot_nt(jnp.broadcast_to(n_ref[h], (MXU_MIN_ROWS, HEAD_D)).astype(BF16), qb)[0:1, :]
        den = jnp.sum(s_t, axis=0, keepdims=True) + inter * nq
        m_t = cum[N_HEADS + h:N_HEADS + h + 1, :] + mx
        den = jnp.maximum(jnp.abs(den), jnp.exp(-m_t))
        ml_out.append((num_t * (1.0 / den), m_t))
    for h in heads:
        qr, kr, vt, qk = rt_ops[h]
        s_t = qk * dec_in_t[h]
        q_decay = jnp.exp(LOG_GAMMA[h] * (l_row + 1.0))
        rt_out.append(_dot(vt, s_t.astype(BF16)) + _dot_nt(st_scr[h].astype(BF16), qr) * q_decay)
    swa_probs = _swa_softmax_t(swa_scores, sinks_ref)

    merged = _sigmoid(g0_ref[...]) * y_ml + _sigmoid(g1_ref[...]) * y_sw + _sigmoid(g2_ref[...]) * y_rt
    x1_ref[...] = x_ref[...] + _rms(_dot(merged.astype(BF16), wout_ref[...]), ng_ref[...])

    m_out = jnp.zeros((1, LANES), F32)
    for h in heads:
        qb, kb, vt, _ = ml_ops[h]
        m_prev = gate_ops[h][0]
        h_t, m_t = ml_out[h]
        hml_s[:, hsl[h]] = _rms_over_rows(h_t).T * mlg_ref[:, hsl[h]]
        m_new = m_t[:, L - 1:L]
        b_last = cum[N_HEADS + h:N_HEADS + h + 1, L - 1:L]
        decay = jnp.exp(b_last + m_prev - m_new)
        w_r = jnp.exp(g8[h:h + 1, :] + (b_last - m_new))
        c_old = c_ref[h]
        n_old = n_ref[h]
        c_new = decay * c_old + _dot((vt.astype(F32) * w_r).astype(BF16), kb)
        n_new = decay * n_old + _dot(jnp.broadcast_to(w_r, (MXU_MIN_ROWS, L)).astype(BF16), kb)[0:1, :]
        c_ref[h] = jnp.where(active, c_new, c_old)
        n_ref[h] = jnp.where(active, n_new, n_old)
        m_out = jnp.where(lane == h, m_new, m_out)
    m_ref[...] = jnp.where(active, m_out, m_all)
    for h in heads:
        qr, kr, vt, _ = rt_ops[h]
        lg = LOG_GAMMA[h]
        ort_s[:, hsl[h]] = _rms_over_rows(rt_out[h]).T * retg_ref[:, hsl[h]]
        k_decay = jnp.exp(lg * (L - 1.0 - l_row))
        st_old = st_scr[h]
        st_new = math.exp(lg * L) * st_old + _dot((vt.astype(F32) * k_decay).astype(BF16), kr)
        st_scr[h] = jnp.where(active, st_new, st_old)

    osw_s[...] = _swa_output_t(swa_probs, vt_buf)

    @pl.when(c == pl.num_programs(0) - 1)
    def _emit_state():
        for h in heads:
            s_ref[h] = st_scr[h].T


def mixer_prompt(z, x, gb, inv, sinks, rb, mlg, retg, ng, layer, wml, wsw, wrt, wout):
    t = z.shape[0]
    L = CHUNK
    n = t // L

    z_mix = pl.BlockSpec((L, Z_HALF), lambda c: (jnp.minimum(c, n - 1), 0))

    def tail(w, cb=0):
        return pl.BlockSpec((L, w), lambda c, cb=cb: (jnp.maximum(c - 1, 0), cb))

    def const(shape):
        return pl.BlockSpec(shape, lambda c: (0,) * len(shape))

    def weight(w):
        return _layer_weight(w, layer)

    smem = pl.BlockSpec(memory_space=pltpu.SMEM)
    return pl.pallas_call(
        _mixer_seq_prompt_kernel,
        grid=(n + 1,),
        in_specs=[z_mix, tail(Z_HALF, 1), tail(D_MODEL),
                  const((2 * N_HEADS, LANES)), const((1, LANES)), smem, smem,
                  const((1, 512)), const((1, 512)), const((1, D_MODEL)),
                  weight(wml), weight(wsw), weight(wrt), weight(wout)],
        out_specs=[tail(D_MODEL),
                   const((N_HEADS, HEAD_D, HEAD_D)), const((N_HEADS, 1, HEAD_D)), const((1, LANES)),
                   const((N_HEADS, HEAD_D, HEAD_D))],
        out_shape=[jax.ShapeDtypeStruct((t, D_MODEL), F32),
                   jax.ShapeDtypeStruct((N_HEADS, HEAD_D, HEAD_D), F32),
                   jax.ShapeDtypeStruct((N_HEADS, 1, HEAD_D), F32),
                   jax.ShapeDtypeStruct((1, LANES), F32),
                   jax.ShapeDtypeStruct((N_HEADS, HEAD_D, HEAD_D), F32)],
        scratch_shapes=[pltpu.VMEM((N_HEADS, L, L), F32),
                        pltpu.VMEM((2 * L, LANES), BF16), pltpu.VMEM((2 * L, LANES), BF16),
                        pltpu.VMEM((LANES, 2 * L), BF16),
                        pltpu.VMEM((2, 2 * L, 4 * L), F32),
                        pltpu.VMEM((L, L), F32), pltpu.VMEM((L, L), F32),
                        pltpu.VMEM((N_HEADS, L, L), F32),
                        pltpu.VMEM((L, L), BF16),
                        pltpu.VMEM((L, 512), F32), pltpu.VMEM((L, 512), F32), pltpu.VMEM((L, 512), F32)],
        compiler_params=_params(1),
        name="mixer_prompt",
    )(z, z, x, gb, inv, sinks, rb, mlg, retg, ng, wml, wsw, wrt, wout)


SAMPLE_GROUP = 8


def _tile_bcast(x, l_idx, src):
    n_rows = x.shape[0]
    out = jnp.zeros_like(x)
    for j in range(SEQ_PAD):
        out = out + jnp.where(l_idx == j, pltpu.roll(x, (j - src) % n_rows, 0), 0.0)
    return out


def _tile_total(x, l_idx):
    for d in (1, 2, 4):
        x = x + jnp.where(l_idx >= d, pltpu.roll(x, d, 0), 0.0)
    return x


def _per_seq(fn, group):
    return jnp.concatenate([fn(g) for g in range(group)], axis=0)


def _build_sample_swa_tables(tab_prev, tab_cur, rb_ref, group):
    R = SEQ_PAD
    rows = group * SWA_HEADS * R
    row = lax.broadcasted_iota(jnp.int32, (rows, LANES), 0)
    col = lax.broadcasted_iota(jnp.int32, (rows, LANES), 1)
    l = row & (R - 1)
    h = (row >> 3) & (SWA_HEADS - 1)
    g = row >> 6
    delta = CHUNK + l - col
    prev = jnp.zeros((rows, LANES), F32)
    cur = jnp.zeros((rows, LANES), F32)
    dcur = l - (col & (R - 1))
    for hh in range(SWA_HEADS):
        val = jnp.full((rows, LANES), rb_ref[BUCKET_RUNS[-1][1] * SWA_HEADS + hh], F32)
        for hi, bk in reversed(BUCKET_RUNS[:-1]):
            val = jnp.where(delta < hi, rb_ref[bk * SWA_HEADS + hh], val)
        prev = jnp.where(h == hh, val, prev)
        valc = jnp.zeros((rows, LANES), F32)
        for d in range(R):
            valc = jnp.where(dcur == d, rb_ref[d * SWA_HEADS + hh], valc)
        cur = jnp.where(h == hh, valc, cur)
    tab_prev[...] = jnp.where((delta >= 0) & (delta < CHUNK), prev, -jnp.inf)
    same_seq = ((col >> 3) == g) & (col < group * R)
    tab_cur[...] = jnp.where(same_seq & (dcur >= 0), cur, -jnp.inf)


def _sample_mixers_kernel(zmix_ref, c0_ref, n0_ref, m0_ref, s0_ref, ckt_ref, cvt_ref,
                          gb_ref, inv_ref, sinks_ref, rb_ref, *rest, group, first_layer):
    zm = _z_views(zmix_ref, 0)
    mq_ref, mk_ref, mv_ref, swq_ref, misc_ref = (zm[i] for i in (ZC_ML_Q, ZC_ML_K, ZC_ML_V, ZC_SW_Q, ZC_MISC))
    rq_ref, rk_ref, rv_ref = (zm[i] for i in (ZC_RT_Q, ZC_RT_K, ZC_RT_V))
    n_layered = 4
    if not first_layer:
        rest = rest[n_layered:]
    hml_ref, osw_ref, ort_ref, n_ref, m_ref, c_all, s_all, kc_all, vc_all, tab_prev, tab_cur = rest
    layered = []
    for ref in (c_all, s_all, kc_all, vc_all):
        if first_layer:
            ref[1:] = jnp.zeros((ref.shape[0] - 1,) + ref.shape[1:], F32)
            layered.append(ref.at[0])
        else:
            layered.append(ref)
    c_ref, s_ref, kc_ref, vc_ref = layered
    R = SEQ_PAD
    NR = group * R

    @pl.when(pl.program_id(0) == 0)
    def _init():
        _build_sample_swa_tables(tab_prev, tab_cur, rb_ref, group)

    lane = lax.broadcasted_iota(jnp.int32, (1, LANES), 1)
    l_idx = lax.broadcasted_iota(jnp.int32, (NR, 1), 0) & (R - 1)
    real = l_idx < N_NEW
    l_f = l_idx.astype(F32)

    def shift(x, d):
        return x if d == 0 else pltpu.roll(x, d, 0)

    def col(slab, h):
        return slab[:, h:h + 1]

    zero_rows = jnp.zeros((LANES - NR, LANES), F32)
    k_new = jnp.concatenate([misc_ref[:, MISC_K:MISC_K + LANES], zero_rows], axis=0)
    v_new = jnp.concatenate([misc_ref[:, MISC_V:MISC_V + LANES], zero_rows], axis=0)
    k_new_t = k_new.T
    v_new_t = v_new.T

    def emit_window_buffers(g):
        back = (LANES - R * g) % LANES
        for new_t, cache_ref, out_ref in ((k_new_t, ckt_ref, kc_ref), (v_new_t, cvt_ref, vc_ref)):
            merged = jnp.where(lane < N_NEW, pltpu.roll(new_t, back, 1) if back else new_t, cache_ref[g])
            out_ref[g] = pltpu.roll(merged, LANES - N_NEW, 1)

    lf = _gate_slab(misc_ref[:, MISC_IF:MISC_IF + LANES], gb_ref)
    bsum = lf
    for d in range(1, N_NEW):
        bsum = bsum + jnp.where(l_idx >= d, shift(lf, d), 0.0)
    b = pltpu.roll(bsum, LANES - N_HEADS, 1)
    gs = lf - b
    m0 = m0_ref[...]
    log_inter = b + m0
    logd = [jnp.where(l_idx >= d, b + shift(gs, d), -jnp.inf) for d in range(N_NEW)]
    m_t = log_inter
    for d in range(N_NEW):
        m_t = jnp.maximum(m_t, logd[d])
    inter = jnp.exp(log_inter - m_t)
    dm = [jnp.exp(logd[d] - m_t) for d in range(N_NEW)]
    emt = jnp.exp(-m_t)
    b_last = _tile_bcast(b, l_idx, N_NEW - 1)
    m_new = _tile_bcast(m_t, l_idx, N_NEW - 1)
    decay = jnp.exp(b_last + m0 - m_new)
    w = jnp.where(real, jnp.exp(b_last - b + lf - m_new), 0.0)
    m_ref[...] = m_t
    n0 = n0_ref[...]
    for h in range(N_HEADS):
        hs = slice(HEAD_D * h, HEAD_D * (h + 1))
        q = mq_ref[:, hs]
        k = mk_ref[:, hs] * (HEAD_D ** -0.5)
        v = mv_ref[:, hs]
        qb = q.astype(BF16)
        kb = k.astype(BF16)
        inter_c = col(inter, h)
        num = _per_seq(lambda g: _dot_nt(qb[R * g:R * (g + 1)], c0_ref[g, h].astype(BF16)), group) * inter_c
        den = inter_c * jnp.sum(q * n0[:, hs], axis=1, keepdims=True)
        for d in range(N_NEW):
            s_d = jnp.sum(q * shift(k, d), axis=1, keepdims=True) * col(dm[d], h)
            num = num + s_d * shift(v, d)
            den = den + s_d
        den = jnp.maximum(jnp.abs(den), col(emt, h))
        hml_ref[:, hs] = num / den
        w_c = col(w, h)
        dec_c = col(decay, h)
        vw = (v * w_c).astype(BF16)
        for g in range(group):
            rs = slice(R * g, R * (g + 1))
            c_ref[g, h] = dec_c[R * g:R * g + 1] * c0_ref[g, h] + _dot_tn(vw[rs], kb[rs])
        n_ref[:, hs] = dec_c * n0[:, hs] + _tile_total(k * w_c, l_idx)
        for g in range(h * group // N_HEADS, (h + 1) * group // N_HEADS):
            emit_window_buffers(g)

    cos, sin_signed = _rotary_tables((PAST_LEN + l_idx).astype(F32), inv_ref)
    for h in range(N_HEADS):
        hs = slice(HEAD_D * h, HEAD_D * (h + 1))
        lg = LOG_GAMMA[h]
        qr = _rotate(rq_ref[:, hs], cos, sin_signed)
        kr = _rotate(rk_ref[:, hs], cos, sin_signed) * (HEAD_D ** -0.5)
        v = rv_ref[:, hs]
        qrb = qr.astype(BF16)
        vb = v.astype(BF16)
        o = _per_seq(lambda g: _dot(qrb[R * g:R * (g + 1)], s0_ref[g, h].astype(BF16)), group)
        o = o * jnp.exp(lg * (l_f + 1.0))
        for d in range(N_NEW):
            s_d = jnp.sum(qr * shift(kr, d), axis=1, keepdims=True) * math.exp(lg * d)
            o = o + jnp.where(l_idx >= d, s_d, 0.0) * shift(v, d)
        ort_ref[:, hs] = o
        kd = (kr * jnp.where(real, jnp.exp(lg * (N_NEW - 1.0 - l_f)), 0.0)).astype(BF16)
        for g in range(group):
            rs = slice(R * g, R * (g + 1))
            s_ref[g, h] = math.exp(lg * N_NEW) * s0_ref[g, h] + _dot_tn(kd[rs], vb[rs])

    upper = lane >= SWA_D
    q_all = swq_ref[...] * (SWA_D ** -0.5)
    q_heads = []
    for h in range(SWA_HEADS):
        blk = q_all[:, LANES * (h // 2):LANES * (h // 2 + 1)]
        qh = jnp.where(upper if h % 2 == 1 else jnp.logical_not(upper), blk, 0.0)
        if h % 2 != h // (SWA_HEADS // 2):
            qh = pltpu.roll(qh, SWA_D, 1)
        q_heads.append(qh)
    qs = jnp.concatenate([q_heads[h][R * g:R * (g + 1)] for g in range(group) for h in range(SWA_HEADS)],
                         axis=0).astype(BF16)
    hr = SWA_HEADS * R
    s_prev = jnp.concatenate([_dot(qs[hr * g:hr * (g + 1)], ckt_ref[g].astype(BF16)) for g in range(group)],
                             axis=0) + tab_prev[...]
    s_cur = _dot_nt(qs, k_new.astype(BF16)) + tab_cur[...]
    sink64 = jnp.concatenate([jnp.full((R, 1), sinks_ref[h], F32) for h in range(SWA_HEADS)], axis=0)
    sink = jnp.concatenate([sink64] * group, axis=0)
    m = jnp.maximum(jnp.maximum(jnp.max(s_prev, axis=1, keepdims=True), jnp.max(s_cur, axis=1, keepdims=True)),
                    sink)
    p_prev = jnp.exp(s_prev - m)
    p_cur = jnp.exp(s_cur - m)
    norm = 1.0 / (jnp.sum(p_prev, axis=1, keepdims=True) + jnp.sum(p_cur, axis=1, keepdims=True)
                  + jnp.exp(sink - m))
    pb = p_prev.astype(BF16)
    o = jnp.concatenate([_dot_nt(pb[hr * g:hr * (g + 1)], cvt_ref[g].astype(BF16)) for g in range(group)],
                        axis=0)
    o = (o + _dot(p_cur.astype(BF16), v_new.astype(BF16))) * norm
    for g in range(group):
        blocks = []
        for j in range(SWA_HEADS // 2):
            pair = []
            for h in (2 * j, 2 * j + 1):
                oh = o[hr * g + R * h:hr * g + R * (h + 1)]
                if h % 2 != h // (SWA_HEADS // 2):
                    oh = pltpu.roll(oh, SWA_D, 1)
                pair.append(oh)
            blocks.append(jnp.where(upper, pair[1], pair[0]))
        osw_ref[R * g:R * (g + 1), :] = jnp.concatenate(blocks, axis=1)


def mixer_sample(z, layer, c0_all, n0_rows, m0_rows, s0_all, ckt_all, cvt_all, gb, inv, sinks, rb, prev):
    depth, nb = c0_all.shape[:2]
    group = SAMPLE_GROUP
    R = SEQ_PAD
    NR = group * R
    first_layer = prev is None
    assert first_layer == (layer == 0)

    def const(shape):
        return pl.BlockSpec(shape, lambda i: (0,) * len(shape))

    def rows(w):
        return pl.BlockSpec((NR, w), lambda i: (i, 0))

    def layered_in(shape):
        return pl.BlockSpec((None, group) + shape, lambda i: (layer, i) + (0,) * len(shape))

    def layered_out(shape):
        if first_layer:
            return pl.BlockSpec((depth, group) + shape, lambda i: (0, i) + (0,) * len(shape))
        return layered_in(shape)

    smem = pl.BlockSpec(memory_space=pltpu.SMEM)
    st = (N_HEADS, HEAD_D, HEAD_D)
    buf = (LANES, CHUNK)
    n_in = 11
    if first_layer:
        extra_specs, extra_args, aliases = [], [], {}
    else:
        extra_specs = [pl.BlockSpec(memory_space=pl.ANY)] * 4
        extra_args = list(prev)
        aliases = {n_in + i: 5 + i for i in range(4)}
    return pl.pallas_call(
        functools.partial(_sample_mixers_kernel, group=group, first_layer=first_layer),
        grid=(nb // group,),
        in_specs=[rows(Z_HALF),
                  layered_in(st), rows(512), rows(LANES), layered_in(st),
                  layered_in(buf), layered_in(buf),
                  const((1, LANES)), const((1, LANES)), smem, smem] + extra_specs,
        out_specs=[rows(512), rows(512), rows(512), rows(512), rows(LANES),
                   layered_out(st), layered_out(st), layered_out(buf), layered_out(buf)],
        out_shape=[jax.ShapeDtypeStruct((nb * R, 512), F32)] * 4
        + [jax.ShapeDtypeStruct((nb * R, LANES), F32),
           jax.ShapeDtypeStruct((depth, nb) + st, F32), jax.ShapeDtypeStruct((depth, nb) + st, F32),
           jax.ShapeDtypeStruct((depth, nb) + buf, F32), jax.ShapeDtypeStruct((depth, nb) + buf, F32)],
        scratch_shapes=[pltpu.VMEM((group * SWA_HEADS * R, LANES), F32),
                        pltpu.VMEM((group * SWA_HEADS * R, LANES), F32)],
        input_output_aliases=aliases,
        compiler_params=_params(1),
        name="mixer_sample",
    )(z, c0_all, n0_rows, m0_rows, s0_all, ckt_all, cvt_all, gb, inv, sinks, rb, *extra_args)


def _mixer_post_kernel(hml_ref, osw_ref, ort_ref, mlo_ref, rtg_ref, g0_ref, g1_ref, g2_ref, x_ref,
                       mlg_ref, retg_ref, ng_ref, wml_ref, wsw_ref, wrt_ref, wout_ref, o_ref):
    o_ref[...] = _post_math(hml_ref[...], osw_ref[...], ort_ref[...], mlo_ref[...], rtg_ref[...],
                            g0_ref[...], g1_ref[...], g2_ref[...], x_ref[...],
                            mlg_ref[...], retg_ref[...], ng_ref[...],
                            wml_ref[...], wsw_ref[...], wrt_ref[...], wout_ref[...])


def mixer_post(hml, osw, ort, z, x, mlg, retg, ng, layer, wml, wsw, wrt, wout, tm, name):
    m = x.shape[0]

    def tok(w, cb=0):
        return pl.BlockSpec((tm, w), lambda i, cb=cb: (i, cb))

    def const(shape):
        return pl.BlockSpec(shape, lambda i: (0,) * len(shape))

    return pl.pallas_call(
        _mixer_post_kernel,
        grid=(m // tm,),
        in_specs=[tok(512), tok(512), tok(512), tok(512, ZC_ML_O), tok(512, ZC_RT_G),
                  tok(1024, ZC_GATES), tok(1024, ZC_GATES + 1), tok(1024, ZC_GATES + 2), tok(D_MODEL),
                  const((1, 512)), const((1, 512)), const((1, D_MODEL)),
                  _layer_weight(wml, layer), _layer_weight(wsw, layer), _layer_weight(wrt, layer),
                  _layer_weight(wout, layer)],
        out_specs=tok(D_MODEL),
        out_shape=jax.ShapeDtypeStruct((m, D_MODEL), F32),
        compiler_params=_params(1),
        name=name,
    )(hml, osw, ort, z, z, z, z, z, x, mlg, retg, ng, wml, wsw, wrt, wout)


def _xattn_head(q, k, v, mask):

    s = _dot_nt(jnp.where(mask, q, 0.0).astype(BF16), k) * (X_D ** -0.5)
    m = jnp.max(s, axis=1, keepdims=True)
    p = jnp.exp(s - m)
    p = p / jnp.sum(p, axis=1, keepdims=True)
    return jnp.where(mask, _dot(p.astype(BF16), v), 0.0)


def _xattn_ffn_prompt_kernel(x_ref, gxi_ref, gxo_ref, wcq_ref, kv_ref, wco_ref, gfi_ref, gfo_ref, wgu_ref, wd_ref,
                             o_ref, mid_scr):
    @pl.when(pl.program_id(0) == 0)
    def _():
        mid_scr[...] = jnp.zeros(mid_scr.shape, F32)

    x_mid = mid_scr[...]
    x = x_ref[...]
    lane = lax.broadcasted_iota(jnp.int32, (1, X_HEADS * X_D), 1)
    masks = [(lane >= X_D * h) & (lane < X_D * (h + 1)) for h in range(X_HEADS)]
    fc = D_FF // FFN_SPLIT
    heads_per_piece = X_HEADS // FFN_SPLIT

    u = _rms(x_mid, gfi_ref[...]).astype(BF16)
    q = _dot(_rms(x, gxi_ref[...]).astype(BF16), wcq_ref[...])
    k = kv_ref[:, 0:X_HEADS * X_D].astype(BF16)
    v = kv_ref[:, X_HEADS * X_D:2 * X_HEADS * X_D].astype(BF16)
    att = jnp.zeros(q.shape, F32)
    acc = None
    for c in range(FFN_SPLIT):
        g = _dot(u, wgu_ref[:, fc * c:fc * (c + 1)])
        up = _dot(u, wgu_ref[:, D_FF + fc * c:D_FF + fc * (c + 1)])
        for h in range(heads_per_piece * c, heads_per_piece * (c + 1)):
            att = att + _xattn_head(q, k, v, masks[h])
        hid = (g * _sigmoid(g) * up).astype(BF16)
        part = _dot(hid, wd_ref[fc * c:fc * (c + 1), :])
        acc = part if acc is None else acc + part
    o_ref[...] = x_mid + _rms(acc, gfo_ref[...])
    mid_scr[...] = x + _rms(_dot(att.astype(BF16), wco_ref[...]), gxo_ref[...])


def xattn_ffn_prompt(x, gxi, gxo, gfi, gfo, layer, wcq, kv, wco, wgu, wd, tm):
    m = x.shape[0]
    n = m // tm

    def const(shape):
        return pl.BlockSpec(shape, lambda s: (0,) * len(shape))

    gain = const((1, D_MODEL))
    return pl.pallas_call(
        _xattn_ffn_prompt_kernel,
        grid=(n + 1,),
        in_specs=[pl.BlockSpec((tm, D_MODEL), lambda s: (jnp.minimum(s, n - 1), 0)),
                  gain, gain, _layer_weight(wcq, layer), const(kv.shape), _layer_weight(wco, layer),
                  gain, gain, _layer_weight(wgu, layer), _layer_weight(wd, layer)],
        out_specs=pl.BlockSpec((tm, D_MODEL), lambda s: (jnp.maximum(s - 1, 0), 0)),
        out_shape=jax.ShapeDtypeStruct((m, D_MODEL), F32),
        scratch_shapes=[pltpu.VMEM((tm, D_MODEL), F32)],
        compiler_params=_params(1),
        name="xattn_ffn_prompt",
    )(x, gxi, gxo, wcq, kv, wco, gfi, gfo, wgu, wd)


def _xattn_sample_kernel(x_ref, gin_ref, gout_ref, wcq_ref, kt_ref, vt_ref, wco_ref, o_ref, q_scr, a_scr,
                         *, group):
    R = SEQ_PAD
    x = x_ref[...]
    q_scr[...] = _dot(_rms(x, gin_ref[...]).astype(BF16), wcq_ref[...])
    lane = lax.broadcasted_iota(jnp.int32, (1, X_HEADS * X_D), 1)
    masks = [(lane >= X_D * h) & (lane < X_D * (h + 1)) for h in range(X_HEADS)]

    def body(g, carry):
        rows = pl.ds(pl.multiple_of(g * R, R), R)
        q = q_scr[rows, :]
        qs = jnp.concatenate([jnp.where(mk, q, 0.0) for mk in masks], axis=0)
        s = _dot(qs.astype(BF16), kt_ref[g].astype(BF16)) * (X_D ** -0.5)
        m = jnp.max(s, axis=1, keepdims=True)
        p = jnp.exp(s - m)
        p = p / jnp.sum(p, axis=1, keepdims=True)
        o = _dot_nt(p.astype(BF16), vt_ref[g].astype(BF16))
        acc = jnp.zeros((R, X_HEADS * X_D), F32)
        for h in range(X_HEADS):
            acc = acc + jnp.where(masks[h], o[R * h:R * (h + 1)], 0.0)
        a_scr[rows, :] = acc
        return carry

    lax.fori_loop(0, group, body, 0, unroll=4)
    o_ref[...] = x + _rms(_dot(a_scr[...].astype(BF16), wco_ref[...]), gout_ref[...])


def xattn_sample(x, gin, gout, wcq, layer, mkt, mvt, wco, group=16):
    nb = mkt.shape[1]
    R = SEQ_PAD

    def const(shape):
        return pl.BlockSpec(shape, lambda i: (0,) * len(shape))

    tok = pl.BlockSpec((group * R, D_MODEL), lambda i: (i, 0))
    mem = pl.BlockSpec((None, group, X_HEADS * X_D, N_MEM), lambda i: (layer, i, 0, 0))
    return pl.pallas_call(
        functools.partial(_xattn_sample_kernel, group=group),
        grid=(nb // group,),
        in_specs=[tok, const((1, D_MODEL)), const((1, D_MODEL)), _layer_weight(wcq, layer), mem, mem,
                  _layer_weight(wco, layer)],
        out_specs=tok,
        out_shape=jax.ShapeDtypeStruct((nb * R, D_MODEL), F32),
        scratch_shapes=[pltpu.VMEM((group * R, X_HEADS * X_D), F32),
                        pltpu.VMEM((group * R, X_HEADS * X_D), F32)],
        compiler_params=_params(1),
        name="xattn_sample",
    )(x, gin, gout, wcq, mkt, mvt, wco)


FFN_SPLIT = 2


def _ffn_kernel(x_ref, gin_ref, gout_ref, wgu_ref, wd_ref, o_ref):
    x = x_ref[...]
    u = _rms(x, gin_ref[...]).astype(BF16)
    fc = D_FF // FFN_SPLIT
    acc = None
    for c in range(FFN_SPLIT):
        g = _dot(u, wgu_ref[:, fc * c:fc * (c + 1)])
        up = _dot(u, wgu_ref[:, D_FF + fc * c:D_FF + fc * (c + 1)])
        h = (g * _sigmoid(g) * up).astype(BF16)
        part = _dot(h, wd_ref[fc * c:fc * (c + 1), :])
        acc = part if acc is None else acc + part
    o_ref[...] = x + _rms(acc, gout_ref[...])


def ffn(x, gin, gout, layer, wgu, wd, tm, name):
    m = x.shape[0]

    def const(shape):
        return pl.BlockSpec(shape, lambda i: (0,) * len(shape))

    tok = pl.BlockSpec((tm, D_MODEL), lambda i: (i, 0))
    return pl.pallas_call(
        _ffn_kernel,
        grid=(m // tm,),
        in_specs=[tok, const((1, D_MODEL)), const((1, D_MODEL)),
                  _layer_weight(wgu, layer), _layer_weight(wd, layer)],
        out_specs=tok,
        out_shape=jax.ShapeDtypeStruct((m, D_MODEL), F32),
        compiler_params=_params(1),
        name=name,
    )(x, gin, gout, wgu, wd)


def _reorder_w_in_t(w):
    wt = jnp.swapaxes(w, 1, 2)
    sizes = (512, 512, 512, 4, 4, 512, 512, 128, 128, 512, 512, 512, 512, 3072)
    offs = np.concatenate([[0], np.cumsum(sizes)])
    (ml_q, ml_k, ml_v, ml_i, ml_f, ml_o, sw_q, sw_k, sw_v, rt_q, rt_k, rt_v, rt_g, gates) = [
        wt[:, int(offs[i]):int(offs[i + 1])] for i in range(len(sizes))]
    pad = jnp.zeros((w.shape[0], 512 - 128 - 128 - 8, w.shape[1]), w.dtype)
    out = jnp.concatenate([ml_q, ml_k, ml_v, sw_q, sw_k, sw_v, ml_i, ml_f, pad, rt_q, rt_k, rt_v,
                           ml_o, rt_g, gates], axis=1)
    assert out.shape[1] == Z_COLS
    return out.astype(BF16)


def _row(v):
    return v.reshape(1, -1).astype(F32)


def _decoder_layer(x, z_fn, mixer_fn, xattn_ffn_fn):
    z = z_fn(x)
    x, states = mixer_fn(z, x)
    return xattn_ffn_fn(x), z, states


def kernel(x_prompt, x_sample, mem_prompt, state_mlstm_C, state_mlstm_n, state_mlstm_m, state_ret_S,
           cache_swa_k, cache_swa_v, cache_mem_k, cache_mem_v, norm_g, w_in, ml_gate_bias, ml_head_g,
           ret_head_g, swa_sinks, rel_bias, w_br_ml, w_br_swa, w_br_ret, w_out, w_cq, w_mkv, w_co,
           w_gu, w_down):
    depth = w_in.shape[0]
    bp, t, d = x_prompt.shape
    assert bp == 1 and d == D_MODEL and t % 1024 == 0
    nb, n_new, _ = x_sample.shape
    assert n_new == N_NEW
    R = SEQ_PAD

    xp = x_prompt.reshape(t, d)
    xs = jnp.pad(x_sample, ((0, 0), (0, R - n_new), (0, 0))).reshape(nb * R, d)
    mem = mem_prompt.reshape(N_MEM, d)
    half = HEAD_D // 2
    inv = ROPE_BASE ** (-jnp.arange(half, dtype=F32) / half)
    inv = jnp.concatenate([inv, inv]).reshape(1, LANES)
    rb = rel_bias.astype(F32).reshape(-1)
    mem_kt = jnp.transpose(cache_mem_k, (0, 1, 3, 4, 2)).reshape(depth, nb, X_HEADS * X_D, N_MEM)
    mem_vt = jnp.transpose(cache_mem_v, (0, 1, 3, 4, 2)).reshape(depth, nb, X_HEADS * X_D, N_MEM)

    win_t = _reorder_w_in_t(w_in)
    wml, wsw, wrt, wout = (w.astype(BF16) for w in (w_br_ml, w_br_swa, w_br_ret, w_out))
    wcq, wmkv, wco, wgu, wd = (w.astype(BF16) for w in (w_cq, w_mkv, w_co, w_gu, w_down))

    outs_p = {k: [] for k in ("C", "n", "m", "S", "k", "v", "mk", "mv")}
    outs_s = {k: [] for k in ("n", "m")}
    sample_layered = None
    swa_kt = jnp.transpose(cache_swa_k, (0, 1, 3, 4, 2)).reshape(depth, nb, LANES, CHUNK)
    swa_vt = jnp.transpose(cache_swa_v, (0, 1, 3, 4, 2)).reshape(depth, nb, LANES, CHUNK)
    for l in range(depth):
        ng = [_row(norm_g[l, i]) for i in range(7)]
        mlg, retg = _row(ml_head_g[l]), _row(ret_head_g[l])
        gb = jnp.concatenate([ml_gate_bias[l, 0], ml_gate_bias[l, 1],
                              jnp.zeros((LANES - 2 * N_HEADS,), F32)]).reshape(1, LANES).astype(F32)
        gb_rows = jnp.broadcast_to(ml_gate_bias[l].astype(F32).reshape(2 * N_HEADS, 1), (2 * N_HEADS, LANES))
        sinks = swa_sinks[l].astype(F32)

        kv = norm_matmul(mem, ng[6], wmkv, l, tm=N_MEM, tn=2 * X_HEADS * X_D, name="memory_kv")

        def seq_p(z, x):
            x1, c_, n_, m_, s_ = mixer_prompt(z, x, gb_rows, inv, sinks, rb, mlg, retg, ng[1], l,
                                              wml, wsw, wrt, wout)
            return x1, (c_, n_, m_, s_)

        xp, zp, (c_, n_, m_, s_) = _decoder_layer(
            xp,
            lambda x: norm_matmul(x, ng[0], win_t, l, tm=min(IN_PROJ_TM, t), tn=1024, name="in_proj_prompt",
                                  w_transposed=True),
            seq_p,
            lambda x: xattn_ffn_prompt(x, ng[2], ng[3], ng[4], ng[5], l, wcq, kv, wco, wgu, wd, tm=512))
        outs_p["C"].append(c_.reshape(1, N_HEADS, HEAD_D, HEAD_D))
        outs_p["n"].append(n_.reshape(1, N_HEADS, HEAD_D))
        outs_p["m"].append(m_[:, :N_HEADS])
        outs_p["S"].append(s_.reshape(1, N_HEADS, HEAD_D, HEAD_D))
        misc_last = zp[t - CHUNK:, 512 * ZC_MISC:512 * ZC_MISC + 2 * LANES]
        outs_p["k"].append(misc_last[:, :LANES].reshape(1, CHUNK, 2, SWA_D))
        outs_p["v"].append(misc_last[:, LANES:].reshape(1, CHUNK, 2, SWA_D))
        outs_p["mk"].append(kv[:, :X_HEADS * X_D].reshape(1, N_MEM, X_HEADS, X_D))
        outs_p["mv"].append(kv[:, X_HEADS * X_D:].reshape(1, N_MEM, X_HEADS, X_D))

        n0_rows = jnp.repeat(state_mlstm_n[l].astype(F32).reshape(nb, N_HEADS * HEAD_D), R, axis=0)
        m0_rows = jnp.repeat(jnp.pad(state_mlstm_m[l].astype(F32), ((0, 0), (0, LANES - N_HEADS))), R, axis=0)

        def seq_s(z, x):
            hml, osw, ort, n_, m_, *layered = mixer_sample(
                z, l, state_mlstm_C, n0_rows, m0_rows, state_ret_S, swa_kt, swa_vt, gb, inv, sinks, rb,
                sample_layered)
            x1 = mixer_post(hml, osw, ort, z, x, mlg, retg, ng[1], l, wml, wsw, wrt, wout,
                            tm=min(256, x.shape[0]), name="mixer_post_sample")
            return x1, (n_, m_, layered)

        xs, zs, (n_, m_, sample_layered) = _decoder_layer(
            xs,
            lambda x: norm_matmul(x, ng[0], win_t, l, tm=min(1024, nb * R), tn=1024, name="in_proj_sample",
                                  w_transposed=True),
            seq_s,
            lambda x: ffn(xattn_sample(x, ng[2], ng[3], wcq, l, mem_kt, mem_vt, wco),
                          ng[4], ng[5], l, wgu, wd, tm=min(512, nb * R), name="ffn_sample"))
        outs_s["n"].append(n_.reshape(nb, R, N_HEADS, HEAD_D)[:, R - 1])
        outs_s["m"].append(m_.reshape(nb, R, LANES)[:, n_new - 1, :N_HEADS])

    sample_c, sample_s, swa_kt_new, swa_vt_new = sample_layered
    s_swa_k = jnp.transpose(swa_kt_new.reshape(depth, nb, 2, SWA_D, CHUNK), (0, 1, 4, 2, 3))
    s_swa_v = jnp.transpose(swa_vt_new.reshape(depth, nb, 2, SWA_D, CHUNK), (0, 1, 4, 2, 3))
    y_p = xp.reshape(1, t, d)
    y_s = xs.reshape(nb, R, d)[:, :n_new]
    st = lambda d_, k: jnp.stack(d_[k])
    return (y_p, y_s,
            st(outs_p, "C"), st(outs_p, "n"), st(outs_p, "m"), st(outs_p, "S"),
            st(outs_p, "k"), st(outs_p, "v"), st(outs_p, "mk"), st(outs_p, "mv"),
            sample_c, st(outs_s, "n"), st(outs_s, "m"), sample_s, s_swa_k, s_swa_v)
```

```python
import functools
import math

import numpy as np
import jax
import jax.numpy as jnp
from jax import lax
from jax.experimental import pallas as pl
from jax.experimental.pallas import tpu as pltpu

F32 = jnp.float32
BF16 = jnp.bfloat16

D_MODEL = 1024
EPS = 1e-6
PAST_LEN = 16384
HEAD_D = 128
N_HEADS = 4
CHUNK = 128
SWA_HEADS = 8
SWA_D = 64
N_BUCKETS = 32
MAX_DISTANCE = 128
ROPE_BASE = 10000.0
N_MEM = 256
X_HEADS = 4
X_D = 64
D_FF = 2816
SEQ_PAD = 8
N_NEW = 4
LANES = 128
IN_PROJ_TM = 2048
MXU_MIN_ROWS = 16
VMEM_LIMIT = 48 * 1024 * 1024

ZC_ML_Q, ZC_ML_K, ZC_ML_V, ZC_SW_Q, ZC_MISC, ZC_RT_Q, ZC_RT_K, ZC_RT_V, ZC_ML_O, ZC_RT_G = range(10)
ZC_GATES = 5
Z_COLS = 8192
Z_HALF = Z_COLS // 2
MISC_K, MISC_V, MISC_IF = 0, 128, 256

LOG2E = math.log2(math.e)
LN2 = math.log(2.0)
LOG_GAMMA = tuple(float(v) for v in np.log1p(-np.exp2(-5.0 - np.arange(N_HEADS, dtype=np.float32))))

SWA_SAME = (0, 2, 5, 7)
SWA_SWAP = (1, 3, 4, 6)


def _t5_bucket_bounds():
    n = np.arange(CHUNK)
    max_exact = N_BUCKETS // 2
    nf = np.maximum(n, 1).astype(np.float32)
    large = max_exact + (np.log(nf / np.float32(max_exact)) / np.float32(math.log(MAX_DISTANCE / max_exact))
                         * np.float32(N_BUCKETS - max_exact)).astype(np.int32)
    large = np.minimum(large, N_BUCKETS - 1)
    b = np.where(n < max_exact, n, large)
    assert np.all(np.diff(b) >= 0)
    runs = []
    for d in range(CHUNK):
        if runs and runs[-1][1] == int(b[d]):
            runs[-1][0] = d + 1
        else:
            runs.append([d + 1, int(b[d])])
    return tuple((hi, bk) for hi, bk in runs)


BUCKET_RUNS = _t5_bucket_bounds()


def _dot(a, b):
    return jnp.dot(a, b, preferred_element_type=F32)


def _dot_nt(a, b):
    return lax.dot_general(a, b, (((1,), (1,)), ((), ())), preferred_element_type=F32)


def _dot_tn(a, b):
    return lax.dot_general(a, b, (((0,), (0,)), ((), ())), preferred_element_type=F32)


def _rms(x, g):
    return x * lax.rsqrt(jnp.mean(x * x, axis=-1, keepdims=True) + EPS) * g


def _sigmoid(x):
    return 1.0 / (1.0 + jnp.exp2(x * (-LOG2E)))


def _log_sigmoid(x):
    return jnp.minimum(x, 0.0) - jnp.log1p(jnp.exp(-jnp.abs(x)))


def _layer_weight(w_all, layer):
    shape = w_all.shape[1:]
    return pl.BlockSpec((None,) + shape, lambda *_: (layer,) + (0,) * len(shape), pipeline_mode=pl.Buffered(1))


def _params(n_grid):
    return pltpu.CompilerParams(dimension_semantics=("arbitrary",) * n_grid, vmem_limit_bytes=VMEM_LIMIT)


def _norm_matmul_kernel(x_ref, g_ref, w_ref, o_ref, u_ref, *, w_transposed):
    @pl.when(pl.program_id(1) == 0)
    def _():
        u_ref[...] = _rms(x_ref[...], g_ref[...]).astype(BF16)

    o_ref[...] = _dot_nt(u_ref[...], w_ref[...]) if w_transposed else _dot(u_ref[...], w_ref[...])


def norm_matmul(x, g, w_all, layer, tm, tn, name, w_transposed=False):
    m, k = x.shape
    n = w_all.shape[1] if w_transposed else w_all.shape[2]
    if w_transposed:
        w_spec = pl.BlockSpec((None, tn, k), lambda i, j: (layer, j, 0))
    else:
        w_spec = pl.BlockSpec((None, k, tn), lambda i, j: (layer, 0, j))
    return pl.pallas_call(
        functools.partial(_norm_matmul_kernel, w_transposed=w_transposed),
        grid=(m // tm, n // tn),
        in_specs=[pl.BlockSpec((tm, k), lambda i, j: (i, 0)),
                  pl.BlockSpec((1, k), lambda i, j: (0, 0)),
                  w_spec],
        out_specs=pl.BlockSpec((tm, tn), lambda i, j: (i, j)),
        out_shape=jax.ShapeDtypeStruct((m, n), F32),
        scratch_shapes=[pltpu.VMEM((tm, k), BF16)],
        compiler_params=_params(2),
        name=name,
    )(x, g, w_all)


def _transpose_bf16(x, eye):
    return _dot_nt(eye, x).astype(BF16)


def _build_swa_table_t(tab_ref, rb_ref):
    L = CHUNK
    srow = lax.broadcasted_iota(jnp.int32, (2 * L, L), 0)
    lcol = lax.broadcasted_iota(jnp.int32, (2 * L, L), 1)
    delta = lcol + L - srow
    valid = (delta >= 0) & (delta < L)
    for var, heads in enumerate((SWA_SAME, SWA_SWAP)):
        for i, h in enumerate(heads):
            val = jnp.full((2 * L, L), rb_ref[BUCKET_RUNS[-1][1] * SWA_HEADS + h], F32)
            for hi, bk in reversed(BUCKET_RUNS[:-1]):
                val = jnp.where(delta < hi, rb_ref[bk * SWA_HEADS + h], val)
            tab_ref[var, :, i * L:(i + 1) * L] = jnp.where(valid, val, -jnp.inf)


def _swa_scores_t(q, k_same, k_swap, tab_ref, prev_invalid):
    L = CHUNK
    lane = lax.broadcasted_iota(jnp.int32, (1, LANES), 1)
    upper = lane >= SWA_D
    q = q * (SWA_D ** -0.5)
    ks = (k_same[...], k_swap[...])
    scores = []
    for var, heads in enumerate((SWA_SAME, SWA_SWAP)):
        qm = []
        for h in heads:
            blk = q[:, LANES * (h // 2):LANES * (h // 2 + 1)]
            qm.append(jnp.where(upper if h % 2 == 1 else jnp.logical_not(upper), blk, 0.0))
        qs = jnp.concatenate(qm, axis=0).astype(BF16)
        s_t = _dot_nt(ks[var], qs) + tab_ref[var]
        scores.append(jnp.concatenate([jnp.where(prev_invalid, -jnp.inf, s_t[:L]), s_t[L:]], axis=0))
    return scores


def _swa_softmax_t(scores, sinks_ref):
    probs = []
    for s_t, heads in zip(scores, (SWA_SAME, SWA_SWAP)):
        sink = jnp.concatenate([jnp.full((1, CHUNK), sinks_ref[h], F32) for h in heads], axis=1)
        m = jnp.maximum(jnp.max(s_t, axis=0, keepdims=True), sink)
        p = jnp.exp(s_t - m)
        norm = 1.0 / (jnp.sum(p, axis=0, keepdims=True) + jnp.exp(sink - m))
        probs.append((p.astype(BF16), norm))
    return probs


def _swa_values_t(probs, vt_buf):
    vt = vt_buf[...]
    vts = (vt, jnp.concatenate([vt[SWA_D:], vt[:SWA_D]], axis=0))
    return [_dot(vts[var], p) * norm for var, (p, norm) in enumerate(probs)]


def _swa_output_t(values):
    L = CHUNK
    lane = lax.broadcasted_iota(jnp.int32, (1, LANES), 1)
    upper = lane >= SWA_D
    outs = [None] * SWA_HEADS
    for o_t, heads in zip(values, (SWA_SAME, SWA_SWAP)):
        for i, h in enumerate(heads):
            outs[h] = o_t[:, i * L:(i + 1) * L].T
    blocks = [jnp.where(upper, outs[2 * j + 1], outs[2 * j]) for j in range(SWA_HEADS // 2)]
    return jnp.concatenate(blocks, axis=1)


def _rotary_tables(pos, inv_ref):
    lane = lax.broadcasted_iota(jnp.int32, (1, LANES), 1)
    ang = pos * inv_ref[...]
    sin = jnp.sin(ang)
    return jnp.cos(ang), jnp.where(lane < HEAD_D // 2, -sin, sin)


def _rotate(x, cos, sin_signed):
    return x * cos + pltpu.roll(x, HEAD_D // 2, 1) * sin_signed


def _gate_slab(raw, gb_ref):
    lane = lax.broadcasted_iota(jnp.int32, (1, LANES), 1)
    x = raw + gb_ref[...]
    return jnp.where((lane >= N_HEADS) & (lane < 2 * N_HEADS), _log_sigmoid(x), x)


def _head_rms(h, gain):
    parts = []
    for i in range(N_HEADS):
        blk = h[:, HEAD_D * i:HEAD_D * (i + 1)]
        parts.append(blk * lax.rsqrt(jnp.mean(blk * blk, axis=-1, keepdims=True) + EPS))
    return jnp.concatenate(parts, axis=-1) * gain


def _rms_over_rows(h_t):
    return h_t * lax.rsqrt(jnp.mean(h_t * h_t, axis=0, keepdims=True) + EPS)


def _post_math(hml, osw, ort, mlo, rtg, g0, g1, g2, x, mlg, retg, ng, wml, wsw, wrt, wout, head_normed=False):
    if not head_normed:
        hml = _head_rms(hml, mlg)
        ort = _head_rms(ort, retg)
    hm = hml * _sigmoid(mlo)
    y_ml = _dot(hm.astype(BF16), wml)
    y_sw = _dot(osw.astype(BF16), wsw)
    rt = ort * (rtg * _sigmoid(rtg))
    y_rt = _dot(rt.astype(BF16), wrt)
    merged = _sigmoid(g0) * y_ml + _sigmoid(g1) * y_sw + _sigmoid(g2) * y_rt
    return x + _rms(_dot(merged.astype(BF16), wout), ng)


def _z_views(z_half_ref, first_col_block):
    return {first_col_block + i: z_half_ref.at[:, 512 * i:512 * (i + 1)] for i in range(Z_HALF // 512)}


def _mixer_seq_prompt_kernel(zmix_ref, ztail_ref, x_ref,
                             gbr_ref, inv_ref, sinks_ref, rb_ref,
                             mlg_ref, retg_ref, ng_ref, wml_ref, wsw_ref, wrt_ref, wout_ref,
                             x1_ref, c_ref, n_ref, m_ref, s_ref,
                             st_scr, k_same, k_swap, vt_buf, tab_t, cos_l, sin_l, dec_in_t, eye_ref,
                             hml_s, osw_s, ort_s):
    zm = _z_views(zmix_ref, 0)
    mq_ref, mk_ref, mv_ref, swq_ref, misc_ref = (zm[i] for i in (ZC_ML_Q, ZC_ML_K, ZC_ML_V, ZC_SW_Q, ZC_MISC))
    rq_ref, rk_ref, rv_ref = (zm[i] for i in (ZC_RT_Q, ZC_RT_K, ZC_RT_V))
    zt = _z_views(ztail_ref, Z_HALF // 512)
    mlo_ref, rtg_ref = zt[ZC_ML_O], zt[ZC_RT_G]
    g0_ref, g1_ref, g2_ref = (ztail_ref.at[:, 1024 * (ZC_GATES + i) - Z_HALF:1024 * (ZC_GATES + i + 1) - Z_HALF]
                              for i in range(3))
    c = pl.program_id(0)
    active = c < pl.num_programs(0) - 1
    L = CHUNK
    lane = lax.broadcasted_iota(jnp.int32, (1, LANES), 1)
    row = lax.broadcasted_iota(jnp.int32, (L, L), 0)
    col = lax.broadcasted_iota(jnp.int32, (L, L), 1)
    causal_t = row <= col

    @pl.when(c == 0)
    def _init():
        c_ref[...] = jnp.zeros(c_ref.shape, F32)
        n_ref[...] = jnp.zeros(n_ref.shape, F32)
        m_ref[...] = jnp.zeros(m_ref.shape, F32)
        s_ref[...] = jnp.zeros(s_ref.shape, F32)
        hml_s[...] = jnp.zeros(hml_s.shape, F32)
        osw_s[...] = jnp.zeros(osw_s.shape, F32)
        ort_s[...] = jnp.zeros(ort_s.shape, F32)
        st_scr[...] = jnp.zeros(st_scr.shape, F32)
        k_same[...] = jnp.zeros(k_same.shape, BF16)
        k_swap[...] = jnp.zeros(k_swap.shape, BF16)
        vt_buf[...] = jnp.zeros(vt_buf.shape, BF16)
        _build_swa_table_t(tab_t, rb_ref)
        ang = row.astype(F32) * inv_ref[...]
        cos_l[...] = jnp.cos(ang)
        sin_l[...] = jnp.sin(ang)
        rel_t = (col - row).astype(F32)
        for h in range(N_HEADS):
            dec_in_t[h] = jnp.where(causal_t, jnp.exp(LOG_GAMMA[h] * rel_t), 0.0)
        eye_ref[...] = jnp.where(row == col, 1.0, 0.0).astype(BF16)

    eye = eye_ref[...]
    heads = range(N_HEADS)
    hsl = [slice(HEAD_D * h, HEAD_D * (h + 1)) for h in heads]
    sub8 = lax.broadcasted_iota(jnp.int32, (2 * N_HEADS, 1), 0)
    hml_prev, osw_prev, ort_prev = hml_s[...], osw_s[...], ort_s[...]

    raw = misc_ref[:, MISC_IF:MISC_IF + LANES].T[0:2 * N_HEADS] + gbr_ref[...]
    gates = jnp.where(sub8 >= N_HEADS, _log_sigmoid(raw), raw)
    cum = gates
    for sh in (1, 2, 4, 8, 16, 32, 64):
        cum = cum + jnp.where(lane >= sh, pltpu.roll(cum, sh, 1), 0.0)
    g8 = jnp.where(sub8 < N_HEADS, gates - pltpu.roll(cum, N_HEADS, 0), 0.0)
    g_cols = jnp.concatenate([g8, jnp.zeros((L - 2 * N_HEADS, L), F32)], axis=0).T

    ang0 = (c * L).astype(F32) * inv_ref[...]
    cos0 = jnp.cos(ang0)
    sin0 = jnp.sin(ang0)
    cos_t = cos_l[...]
    sin_t = sin_l[...]
    cos = cos0 * cos_t - sin0 * sin_t
    sin = sin0 * cos_t + cos0 * sin_t
    sin_signed = jnp.where(lane < HEAD_D // 2, -sin, sin)
    ml_ops, rt_ops = [], []
    for h in heads:
        qb = mq_ref[:, hsl[h]].astype(BF16)
        kb = (mk_ref[:, hsl[h]] * (HEAD_D ** -0.5)).astype(BF16)
        vt = _transpose_bf16(mv_ref[:, hsl[h]].astype(BF16), eye)
        ml_ops.append((qb, kb, vt, _dot_nt(kb, qb)))
    for h in heads:
        qr = _rotate(rq_ref[:, hsl[h]], cos, sin_signed).astype(BF16)
        kr = (_rotate(rk_ref[:, hsl[h]], cos, sin_signed) * (HEAD_D ** -0.5)).astype(BF16)
        vt = _transpose_bf16(rv_ref[:, hsl[h]].astype(BF16), eye)
        rt_ops.append((qr, kr, vt, _dot_nt(kr, qr)))
    k_new = misc_ref[:, MISC_K:MISC_K + LANES]
    k_same[0:L, :] = k_same[L:2 * L, :]
    k_swap[0:L, :] = k_swap[L:2 * L, :]
    k_same[L:2 * L, :] = k_new.astype(BF16)
    k_swap[L:2 * L, :] = pltpu.roll(k_new, SWA_D, 1).astype(BF16)
    vt_buf[:, 0:L] = vt_buf[:, L:2 * L]
    vt_buf[:, L:2 * L] = _transpose_bf16(misc_ref[:, MISC_V:MISC_V + LANES].astype(BF16), eye)
    swa_scores = _swa_scores_t(swq_ref[...], k_same, k_swap, tab_t, c == 0)

    y_ml = _dot((hml_prev * _sigmoid(mlo_ref[...])).astype(BF16), wml_ref[...])

    m_all = m_ref[...]
    gate_ops = []
    for h in heads:
        m_prev = m_all[:, h:h + 1]
        gm = jnp.where(causal_t, g_cols[:, h:h + 1], -jnp.inf)
        mx = jnp.maximum(jnp.max(gm, axis=0, keepdims=True), m_prev)
        gate_ops.append((m_prev, mx, jnp.exp(m_prev - mx), jnp.exp(gm - mx)))

    y_sw = _dot(osw_prev.astype(BF16), wsw_ref[...])
    rtg = rtg_ref[...]
    y_rt = _dot((ort_prev * (rtg * _sigmoid(rtg))).astype(BF16), wrt_ref[...])

    l_row = lane.astype(F32)
    ml_out, rt_out = [], []
    for h in heads:
        qb, kb, vt, qk = ml_ops[h]
        m_prev, mx, inter, dm_t = gate_ops[h]
        s_t = qk * dm_t
        num_t = _dot(vt, s_t.astype(BF16)) + _dot_nt(c_ref[h].astype(BF16), qb) * inter
        nq = _dot_nt(jnp.broadcast_to(n_ref[h], (MXU_MIN_ROWS, HEAD_D)).astype(BF16), qb)[0:1, :]
        den = jnp.sum(s_t, axis=0, keepdims=True) + inter * nq
        m_t = cum[N_HEADS + h:N_HEADS + h + 1, :] + mx
        den = jnp.maximum(jnp.abs(den), jnp.exp(-m_t))
        ml_out.append((num_t * (1.0 / den), m_t))
    for h in heads:
        qr, kr, vt, qk = rt_ops[h]
        s_t = qk * dec_in_t[h]
        q_decay = jnp.exp(LOG_GAMMA[h] * (l_row + 1.0))
        rt_out.append(_dot(vt, s_t.astype(BF16)) + _dot_nt(st_scr[h].astype(BF16), qr) * q_decay)
    swa_values = _swa_values_t(_swa_softmax_t(swa_scores, sinks_ref), vt_buf)

    merged = _sigmoid(g0_ref[...]) * y_ml + _sigmoid(g1_ref[...]) * y_sw + _sigmoid(g2_ref[...]) * y_rt
    x1_ref[...] = x_ref[...] + _rms(_dot(merged.astype(BF16), wout_ref[...]), ng_ref[...])

    osw_s[...] = _swa_output_t(swa_values)
    for h in heads:
        hml_s[:, hsl[h]] = _rms_over_rows(ml_out[h][0]).T * mlg_ref[:, hsl[h]]
        ort_s[:, hsl[h]] = _rms_over_rows(rt_out[h]).T * retg_ref[:, hsl[h]]
    m_out = jnp.zeros((1, LANES), F32)
    for h in heads:
        qb, kb, vt, _ = ml_ops[h]
        m_prev = gate_ops[h][0]
        m_t = ml_out[h][1]
        m_new = m_t[:, L - 1:L]
        b_last = cum[N_HEADS + h:N_HEADS + h + 1, L - 1:L]
        decay = jnp.exp(b_last + m_prev - m_new)
        w_r = jnp.exp(g8[h:h + 1, :] + (b_last - m_new))
        c_old = c_ref[h]
        n_old = n_ref[h]
        c_new = decay * c_old + _dot((vt.astype(F32) * w_r).astype(BF16), kb)
        n_new = decay * n_old + _dot(jnp.broadcast_to(w_r, (MXU_MIN_ROWS, L)).astype(BF16), kb)[0:1, :]
        c_ref[h] = jnp.where(active, c_new, c_old)
        n_ref[h] = jnp.where(active, n_new, n_old)
        m_out = jnp.where(lane == h, m_new, m_out)
    m_ref[...] = jnp.where(active, m_out, m_all)
    for h in heads:
        qr, kr, vt, _ = rt_ops[h]
        lg = LOG_GAMMA[h]
        k_decay = jnp.exp(lg * (L - 1.0 - l_row))
        st_old = st_scr[h]
        st_new = math.exp(lg * L) * st_old + _dot((vt.astype(F32) * k_decay).astype(BF16), kr)
        st_scr[h] = jnp.where(active, st_new, st_old)

    @pl.when(c == pl.num_programs(0) - 1)
    def _emit_state():
        for h in heads:
            s_ref[h] = st_scr[h].T


def mixer_prompt(z, x, gb, inv, sinks, rb, mlg, retg, ng, layer, wml, wsw, wrt, wout):
    t = z.shape[0]
    L = CHUNK
    n = t // L

    z_mix = pl.BlockSpec((L, Z_HALF), lambda c: (jnp.minimum(c, n - 1), 0))

    def tail(w, cb=0):
        return pl.BlockSpec((L, w), lambda c, cb=cb: (jnp.maximum(c - 1, 0), cb))

    def const(shape):
        return pl.BlockSpec(shape, lambda c: (0,) * len(shape))

    def weight(w):
        return _layer_weight(w, layer)

    smem = pl.BlockSpec(memory_space=pltpu.SMEM)
    return pl.pallas_call(
        _mixer_seq_prompt_kernel,
        grid=(n + 1,),
        in_specs=[z_mix, tail(Z_HALF, 1), tail(D_MODEL),
                  const((2 * N_HEADS, LANES)), const((1, LANES)), smem, smem,
                  const((1, 512)), const((1, 512)), const((1, D_MODEL)),
                  weight(wml), weight(wsw), weight(wrt), weight(wout)],
        out_specs=[tail(D_MODEL),
                   const((N_HEADS, HEAD_D, HEAD_D)), const((N_HEADS, 1, HEAD_D)), const((1, LANES)),
                   const((N_HEADS, HEAD_D, HEAD_D))],
        out_shape=[jax.ShapeDtypeStruct((t, D_MODEL), F32),
                   jax.ShapeDtypeStruct((N_HEADS, HEAD_D, HEAD_D), F32),
                   jax.ShapeDtypeStruct((N_HEADS, 1, HEAD_D), F32),
                   jax.ShapeDtypeStruct((1, LANES), F32),
                   jax.ShapeDtypeStruct((N_HEADS, HEAD_D, HEAD_D), F32)],
        scratch_shapes=[pltpu.VMEM((N_HEADS, L, L), F32),
                        pltpu.VMEM((2 * L, LANES), BF16), pltpu.VMEM((2 * L, LANES), BF16),
                        pltpu.VMEM((LANES, 2 * L), BF16),
                        pltpu.VMEM((2, 2 * L, 4 * L), F32),
                        pltpu.VMEM((L, L), F32), pltpu.VMEM((L, L), F32),
                        pltpu.VMEM((N_HEADS, L, L), F32),
                        pltpu.VMEM((L, L), BF16),
                        pltpu.VMEM((L, 512), F32), pltpu.VMEM((L, 512), F32), pltpu.VMEM((L, 512), F32)],
        compiler_params=_params(1),
        name="mixer_prompt",
    )(z, z, x, gb, inv, sinks, rb, mlg, retg, ng, wml, wsw, wrt, wout)


SAMPLE_GROUP = 8


def _tile_bcast(x, l_idx, src):
    n_rows = x.shape[0]
    out = jnp.zeros_like(x)
    for j in range(SEQ_PAD):
        out = out + jnp.where(l_idx == j, pltpu.roll(x, (j - src) % n_rows, 0), 0.0)
    return out


def _tile_total(x, l_idx):
    for d in (1, 2, 4):
        x = x + jnp.where(l_idx >= d, pltpu.roll(x, d, 0), 0.0)
    return x


def _per_seq(fn, group):
    return jnp.concatenate([fn(g) for g in range(group)], axis=0)


def _build_sample_swa_tables(tab_prev, tab_cur, rb_ref, group):
    R = SEQ_PAD
    rows = group * SWA_HEADS * R
    row = lax.broadcasted_iota(jnp.int32, (rows, LANES), 0)
    col = lax.broadcasted_iota(jnp.int32, (rows, LANES), 1)
    l = row & (R - 1)
    h = (row >> 3) & (SWA_HEADS - 1)
    g = row >> 6
    delta = CHUNK + l - col
    prev = jnp.zeros((rows, LANES), F32)
    cur = jnp.zeros((rows, LANES), F32)
    dcur = l - (col & (R - 1))
    for hh in range(SWA_HEADS):
        val = jnp.full((rows, LANES), rb_ref[BUCKET_RUNS[-1][1] * SWA_HEADS + hh], F32)
        for hi, bk in reversed(BUCKET_RUNS[:-1]):
            val = jnp.where(delta < hi, rb_ref[bk * SWA_HEADS + hh], val)
        prev = jnp.where(h == hh, val, prev)
        valc = jnp.zeros((rows, LANES), F32)
        for d in range(R):
            valc = jnp.where(dcur == d, rb_ref[d * SWA_HEADS + hh], valc)
        cur = jnp.where(h == hh, valc, cur)
    tab_prev[...] = jnp.where((delta >= 0) & (delta < CHUNK), prev, -jnp.inf)
    same_seq = ((col >> 3) == g) & (col < group * R)
    tab_cur[...] = jnp.where(same_seq & (dcur >= 0), cur, -jnp.inf)


def _sample_mixers_kernel(zmix_ref, c0_ref, n0_ref, m0_ref, s0_ref, ckt_ref, cvt_ref,
                          gb_ref, inv_ref, sinks_ref, rb_ref, *rest, group, first_layer):
    zm = _z_views(zmix_ref, 0)
    mq_ref, mk_ref, mv_ref, swq_ref, misc_ref = (zm[i] for i in (ZC_ML_Q, ZC_ML_K, ZC_ML_V, ZC_SW_Q, ZC_MISC))
    rq_ref, rk_ref, rv_ref = (zm[i] for i in (ZC_RT_Q, ZC_RT_K, ZC_RT_V))
    n_layered = 4
    if not first_layer:
        rest = rest[n_layered:]
    hml_ref, osw_ref, ort_ref, n_ref, m_ref, c_all, s_all, kc_all, vc_all, tab_prev, tab_cur = rest
    layered = []
    for ref in (c_all, s_all, kc_all, vc_all):
        if first_layer:
            ref[1:] = jnp.zeros((ref.shape[0] - 1,) + ref.shape[1:], F32)
            layered.append(ref.at[0])
        else:
            layered.append(ref)
    c_ref, s_ref, kc_ref, vc_ref = layered
    R = SEQ_PAD
    NR = group * R

    @pl.when(pl.program_id(0) == 0)
    def _init():
        _build_sample_swa_tables(tab_prev, tab_cur, rb_ref, group)

    lane = lax.broadcasted_iota(jnp.int32, (1, LANES), 1)
    l_idx = lax.broadcasted_iota(jnp.int32, (NR, 1), 0) & (R - 1)
    real = l_idx < N_NEW
    l_f = l_idx.astype(F32)

    def shift(x, d):
        return x if d == 0 else pltpu.roll(x, d, 0)

    def col(slab, h):
        return slab[:, h:h + 1]

    zero_rows = jnp.zeros((LANES - NR, LANES), F32)
    k_new = jnp.concatenate([misc_ref[:, MISC_K:MISC_K + LANES], zero_rows], axis=0)
    v_new = jnp.concatenate([misc_ref[:, MISC_V:MISC_V + LANES], zero_rows], axis=0)
    k_new_t = k_new.T
    v_new_t = v_new.T

    def emit_window_buffers(g):
        back = (LANES - R * g) % LANES
        for new_t, cache_ref, out_ref in ((k_new_t, ckt_ref, kc_ref), (v_new_t, cvt_ref, vc_ref)):
            merged = jnp.where(lane < N_NEW, pltpu.roll(new_t, back, 1) if back else new_t, cache_ref[g])
            out_ref[g] = pltpu.roll(merged, LANES - N_NEW, 1)

    lf = _gate_slab(misc_ref[:, MISC_IF:MISC_IF + LANES], gb_ref)
    bsum = lf
    for d in range(1, N_NEW):
        bsum = bsum + jnp.where(l_idx >= d, shift(lf, d), 0.0)
    b = pltpu.roll(bsum, LANES - N_HEADS, 1)
    gs = lf - b
    m0 = m0_ref[...]
    log_inter = b + m0
    logd = [jnp.where(l_idx >= d, b + shift(gs, d), -jnp.inf) for d in range(N_NEW)]
    m_t = log_inter
    for d in range(N_NEW):
        m_t = jnp.maximum(m_t, logd[d])
    inter = jnp.exp(log_inter - m_t)
    dm = [jnp.exp(logd[d] - m_t) for d in range(N_NEW)]
    emt = jnp.exp(-m_t)
    b_last = _tile_bcast(b, l_idx, N_NEW - 1)
    m_new = _tile_bcast(m_t, l_idx, N_NEW - 1)
    decay = jnp.exp(b_last + m0 - m_new)
    w = jnp.where(real, jnp.exp(b_last - b + lf - m_new), 0.0)
    m_ref[...] = m_t
    n0 = n0_ref[...]
    for h in range(N_HEADS):
        hs = slice(HEAD_D * h, HEAD_D * (h + 1))
        q = mq_ref[:, hs]
        k = mk_ref[:, hs] * (HEAD_D ** -0.5)
        v = mv_ref[:, hs]
        qb = q.astype(BF16)
        kb = k.astype(BF16)
        inter_c = col(inter, h)
        num = _per_seq(lambda g: _dot_nt(qb[R * g:R * (g + 1)], c0_ref[g, h].astype(BF16)), group) * inter_c
        den = inter_c * jnp.sum(q * n0[:, hs], axis=1, keepdims=True)
        for d in range(N_NEW):
            s_d = jnp.sum(q * shift(k, d), axis=1, keepdims=True) * col(dm[d], h)
            num = num + s_d * shift(v, d)
            den = den + s_d
        den = jnp.maximum(jnp.abs(den), col(emt, h))
        hml_ref[:, hs] = num / den
        w_c = col(w, h)
        dec_c = col(decay, h)
        vw = (v * w_c).astype(BF16)
        for g in range(group):
            rs = slice(R * g, R * (g + 1))
            c_ref[g, h] = dec_c[R * g:R * g + 1] * c0_ref[g, h] + _dot_tn(vw[rs], kb[rs])
        n_ref[:, hs] = dec_c * n0[:, hs] + _tile_total(k * w_c, l_idx)
        for g in range(h * group // N_HEADS, (h + 1) * group // N_HEADS):
            emit_window_buffers(g)

    cos, sin_signed = _rotary_tables((PAST_LEN + l_idx).astype(F32), inv_ref)
    for h in range(N_HEADS):
        hs = slice(HEAD_D * h, HEAD_D * (h + 1))
        lg = LOG_GAMMA[h]
        qr = _rotate(rq_ref[:, hs], cos, sin_signed)
        kr = _rotate(rk_ref[:, hs], cos, sin_signed) * (HEAD_D ** -0.5)
        v = rv_ref[:, hs]
        qrb = qr.astype(BF16)
        vb = v.astype(BF16)
        o = _per_seq(lambda g: _dot(qrb[R * g:R * (g + 1)], s0_ref[g, h].astype(BF16)), group)
        o = o * jnp.exp(lg * (l_f + 1.0))
        for d in range(N_NEW):
            s_d = jnp.sum(qr * shift(kr, d), axis=1, keepdims=True) * math.exp(lg * d)
            o = o + jnp.where(l_idx >= d, s_d, 0.0) * shift(v, d)
        ort_ref[:, hs] = o
        kd = (kr * jnp.where(real, jnp.exp(lg * (N_NEW - 1.0 - l_f)), 0.0)).astype(BF16)
        for g in range(group):
            rs = slice(R * g, R * (g + 1))
            s_ref[g, h] = math.exp(lg * N_NEW) * s0_ref[g, h] + _dot_tn(kd[rs], vb[rs])

    upper = lane >= SWA_D
    q_all = swq_ref[...] * (SWA_D ** -0.5)
    q_heads = []
    for h in range(SWA_HEADS):
        blk = q_all[:, LANES * (h // 2):LANES * (h // 2 + 1)]
        qh = jnp.where(upper if h % 2 == 1 else jnp.logical_not(upper), blk, 0.0)
        if h % 2 != h // (SWA_HEADS // 2):
            qh = pltpu.roll(qh, SWA_D, 1)
        q_heads.append(qh)
    qs = jnp.concatenate([q_heads[h][R * g:R * (g + 1)] for g in range(group) for h in range(SWA_HEADS)],
                         axis=0).astype(BF16)
    hr = SWA_HEADS * R
    s_prev = jnp.concatenate([_dot(qs[hr * g:hr * (g + 1)], ckt_ref[g].astype(BF16)) for g in range(group)],
                             axis=0) + tab_prev[...]
    s_cur = _dot_nt(qs, k_new.astype(BF16)) + tab_cur[...]
    sink64 = jnp.concatenate([jnp.full((R, 1), sinks_ref[h], F32) for h in range(SWA_HEADS)], axis=0)
    sink = jnp.concatenate([sink64] * group, axis=0)
    m = jnp.maximum(jnp.maximum(jnp.max(s_prev, axis=1, keepdims=True), jnp.max(s_cur, axis=1, keepdims=True)),
                    sink)
    p_prev = jnp.exp(s_prev - m)
    p_cur = jnp.exp(s_cur - m)
    norm = 1.0 / (jnp.sum(p_prev, axis=1, keepdims=True) + jnp.sum(p_cur, axis=1, keepdims=True)
                  + jnp.exp(sink - m))
    pb = p_prev.astype(BF16)
    o = jnp.concatenate([_dot_nt(pb[hr * g:hr * (g + 1)], cvt_ref[g].astype(BF16)) for g in range(group)],
                        axis=0)
    o = (o + _dot(p_cur.astype(BF16), v_new.astype(BF16))) * norm
    for g in range(group):
        blocks = []
        for j in range(SWA_HEADS // 2):
            pair = []
            for h in (2 * j, 2 * j + 1):
                oh = o[hr * g + R * h:hr * g + R * (h + 1)]
                if h % 2 != h // (SWA_HEADS // 2):
                    oh = pltpu.roll(oh, SWA_D, 1)
                pair.append(oh)
            blocks.append(jnp.where(upper, pair[1], pair[0]))
        osw_ref[R * g:R * (g + 1), :] = jnp.concatenate(blocks, axis=1)


def mixer_sample(z, layer, c0_all, n0_rows, m0_rows, s0_all, ckt_all, cvt_all, gb, inv, sinks, rb, prev):
    depth, nb = c0_all.shape[:2]
    group = SAMPLE_GROUP
    R = SEQ_PAD
    NR = group * R
    first_layer = prev is None
    assert first_layer == (layer == 0)

    def const(shape):
        return pl.BlockSpec(shape, lambda i: (0,) * len(shape))

    def rows(w):
        return pl.BlockSpec((NR, w), lambda i: (i, 0))

    def layered_in(shape):
        return pl.BlockSpec((None, group) + shape, lambda i: (layer, i) + (0,) * len(shape))

    def layered_out(shape):
        if first_layer:
            return pl.BlockSpec((depth, group) + shape, lambda i: (0, i) + (0,) * len(shape))
        return layered_in(shape)

    smem = pl.BlockSpec(memory_space=pltpu.SMEM)
    st = (N_HEADS, HEAD_D, HEAD_D)
    buf = (LANES, CHUNK)
    n_in = 11
    if first_layer:
        extra_specs, extra_args, aliases = [], [], {}
    else:
        extra_specs = [pl.BlockSpec(memory_space=pl.ANY)] * 4
        extra_args = list(prev)
        aliases = {n_in + i: 5 + i for i in range(4)}
    return pl.pallas_call(
        functools.partial(_sample_mixers_kernel, group=group, first_layer=first_layer),
        grid=(nb // group,),
        in_specs=[rows(Z_HALF),
                  layered_in(st), rows(512), rows(LANES), layered_in(st),
                  layered_in(buf), layered_in(buf),
                  const((1, LANES)), const((1, LANES)), smem, smem] + extra_specs,
        out_specs=[rows(512), rows(512), rows(512), rows(512), rows(LANES),
                   layered_out(st), layered_out(st), layered_out(buf), layered_out(buf)],
        out_shape=[jax.ShapeDtypeStruct((nb * R, 512), F32)] * 4
        + [jax.ShapeDtypeStruct((nb * R, LANES), F32),
           jax.ShapeDtypeStruct((depth, nb) + st, F32), jax.ShapeDtypeStruct((depth, nb) + st, F32),
           jax.ShapeDtypeStruct((depth, nb) + buf, F32), jax.ShapeDtypeStruct((depth, nb) + buf, F32)],
        scratch_shapes=[pltpu.VMEM((group * SWA_HEADS * R, LANES), F32),
                        pltpu.VMEM((group * SWA_HEADS * R, LANES), F32)],
        input_output_aliases=aliases,
        compiler_params=_params(1),
        name="mixer_sample",
    )(z, c0_all, n0_rows, m0_rows, s0_all, ckt_all, cvt_all, gb, inv, sinks, rb, *extra_args)


def _mixer_post_kernel(hml_ref, osw_ref, ort_ref, mlo_ref, rtg_ref, g0_ref, g1_ref, g2_ref, x_ref,
                       mlg_ref, retg_ref, ng_ref, wml_ref, wsw_ref, wrt_ref, wout_ref, o_ref):
    o_ref[...] = _post_math(hml_ref[...], osw_ref[...], ort_ref[...], mlo_ref[...], rtg_ref[...],
                            g0_ref[...], g1_ref[...], g2_ref[...], x_ref[...],
                            mlg_ref[...], retg_ref[...], ng_ref[...],
                            wml_ref[...], wsw_ref[...], wrt_ref[...], wout_ref[...])


def mixer_post(hml, osw, ort, z, x, mlg, retg, ng, layer, wml, wsw, wrt, wout, tm, name):
    m = x.shape[0]

    def tok(w, cb=0):
        return pl.BlockSpec((tm, w), lambda i, cb=cb: (i, cb))

    def const(shape):
        return pl.BlockSpec(shape, lambda i: (0,) * len(shape))

    return pl.pallas_call(
        _mixer_post_kernel,
        grid=(m // tm,),
        in_specs=[tok(512), tok(512), tok(512), tok(512, ZC_ML_O), tok(512, ZC_RT_G),
                  tok(1024, ZC_GATES), tok(1024, ZC_GATES + 1), tok(1024, ZC_GATES + 2), tok(D_MODEL),
                  const((1, 512)), const((1, 512)), const((1, D_MODEL)),
                  _layer_weight(wml, layer), _layer_weight(wsw, layer), _layer_weight(wrt, layer),
                  _layer_weight(wout, layer)],
        out_specs=tok(D_MODEL),
        out_shape=jax.ShapeDtypeStruct((m, D_MODEL), F32),
        compiler_params=_params(1),
        name=name,
    )(hml, osw, ort, z, z, z, z, z, x, mlg, retg, ng, wml, wsw, wrt, wout)


def _xattn_head(q, k, v, mask):

    s = _dot_nt(jnp.where(mask, q, 0.0).astype(BF16), k) * (X_D ** -0.5)
    m = jnp.max(s, axis=1, keepdims=True)
    p = jnp.exp(s - m)
    p = p / jnp.sum(p, axis=1, keepdims=True)
    return jnp.where(mask, _dot(p.astype(BF16), v), 0.0)


def _xattn_ffn_prompt_kernel(x_ref, gxi_ref, gxo_ref, wcq_ref, kv_ref, wco_ref, gfi_ref, gfo_ref, wgu_ref, wd_ref,
                             o_ref, mid_scr):
    @pl.when(pl.program_id(0) == 0)
    def _():
        mid_scr[...] = jnp.zeros(mid_scr.shape, F32)

    x_mid = mid_scr[...]
    x = x_ref[...]
    lane = lax.broadcasted_iota(jnp.int32, (1, X_HEADS * X_D), 1)
    masks = [(lane >= X_D * h) & (lane < X_D * (h + 1)) for h in range(X_HEADS)]
    fc = D_FF // FFN_SPLIT
    heads_per_piece = X_HEADS // FFN_SPLIT

    u = _rms(x_mid, gfi_ref[...]).astype(BF16)
    q = _dot(_rms(x, gxi_ref[...]).astype(BF16), wcq_ref[...])
    k = kv_ref[:, 0:X_HEADS * X_D].astype(BF16)
    v = kv_ref[:, X_HEADS * X_D:2 * X_HEADS * X_D].astype(BF16)
    att = jnp.zeros(q.shape, F32)
    acc = None
    for c in range(FFN_SPLIT):
        g = _dot(u, wgu_ref[:, fc * c:fc * (c + 1)])
        up = _dot(u, wgu_ref[:, D_FF + fc * c:D_FF + fc * (c + 1)])
        for h in range(heads_per_piece * c, heads_per_piece * (c + 1)):
            att = att + _xattn_head(q, k, v, masks[h])
        hid = (g * _sigmoid(g) * up).astype(BF16)
        part = _dot(hid, wd_ref[fc * c:fc * (c + 1), :])
        acc = part if acc is None else acc + part
    o_ref[...] = x_mid + _rms(acc, gfo_ref[...])
    mid_scr[...] = x + _rms(_dot(att.astype(BF16), wco_ref[...]), gxo_ref[...])


def xattn_ffn_prompt(x, gxi, gxo, gfi, gfo, layer, wcq, kv, wco, wgu, wd, tm):
    m = x.shape[0]
    n = m // tm

    def const(shape):
        return pl.BlockSpec(shape, lambda s: (0,) * len(shape))

    gain = const((1, D_MODEL))
    return pl.pallas_call(
        _xattn_ffn_prompt_kernel,
        grid=(n + 1,),
        in_specs=[pl.BlockSpec((tm, D_MODEL), lambda s: (jnp.minimum(s, n - 1), 0)),
                  gain, gain, _layer_weight(wcq, layer), const(kv.shape), _layer_weight(wco, layer),
                  gain, gain, _layer_weight(wgu, layer), _layer_weight(wd, layer)],
        out_specs=pl.BlockSpec((tm, D_MODEL), lambda s: (jnp.maximum(s - 1, 0), 0)),
        out_shape=jax.ShapeDtypeStruct((m, D_MODEL), F32),
        scratch_shapes=[pltpu.VMEM((tm, D_MODEL), F32)],
        compiler_params=_params(1),
        name="xattn_ffn_prompt",
    )(x, gxi, gxo, wcq, kv, wco, gfi, gfo, wgu, wd)


def _xattn_sample_kernel(x_ref, gin_ref, gout_ref, wcq_ref, kt_ref, vt_ref, wco_ref, o_ref, q_scr, a_scr,
                         *, group):
    R = SEQ_PAD
    x = x_ref[...]
    q_scr[...] = _dot(_rms(x, gin_ref[...]).astype(BF16), wcq_ref[...])
    lane = lax.broadcasted_iota(jnp.int32, (1, X_HEADS * X_D), 1)
    masks = [(lane >= X_D * h) & (lane < X_D * (h + 1)) for h in range(X_HEADS)]

    def body(g, carry):
        rows = pl.ds(pl.multiple_of(g * R, R), R)
        q = q_scr[rows, :]
        qs = jnp.concatenate([jnp.where(mk, q, 0.0) for mk in masks], axis=0)
        s = _dot(qs.astype(BF16), kt_ref[g].astype(BF16)) * (X_D ** -0.5)
        m = jnp.max(s, axis=1, keepdims=True)
        p = jnp.exp(s - m)
        p = p / jnp.sum(p, axis=1, keepdims=True)
        o = _dot_nt(p.astype(BF16), vt_ref[g].astype(BF16))
        acc = jnp.zeros((R, X_HEADS * X_D), F32)
        for h in range(X_HEADS):
            acc = acc + jnp.where(masks[h], o[R * h:R * (h + 1)], 0.0)
        a_scr[rows, :] = acc
        return carry

    lax.fori_loop(0, group, body, 0, unroll=4)
    o_ref[...] = x + _rms(_dot(a_scr[...].astype(BF16), wco_ref[...]), gout_ref[...])


def xattn_sample(x, gin, gout, wcq, layer, mkt, mvt, wco, group=16):
    nb = mkt.shape[1]
    R = SEQ_PAD

    def const(shape):
        return pl.BlockSpec(shape, lambda i: (0,) * len(shape))

    tok = pl.BlockSpec((group * R, D_MODEL), lambda i: (i, 0))
    mem = pl.BlockSpec((None, group, X_HEADS * X_D, N_MEM), lambda i: (layer, i, 0, 0))
    return pl.pallas_call(
        functools.partial(_xattn_sample_kernel, group=group),
        grid=(nb // group,),
        in_specs=[tok, const((1, D_MODEL)), const((1, D_MODEL)), _layer_weight(wcq, layer), mem, mem,
                  _layer_weight(wco, layer)],
        out_specs=tok,
        out_shape=jax.ShapeDtypeStruct((nb * R, D_MODEL), F32),
        scratch_shapes=[pltpu.VMEM((group * R, X_HEADS * X_D), F32),
                        pltpu.VMEM((group * R, X_HEADS * X_D), F32)],
        compiler_params=_params(1),
        name="xattn_sample",
    )(x, gin, gout, wcq, mkt, mvt, wco)


FFN_SPLIT = 2


def _ffn_kernel(x_ref, gin_ref, gout_ref, wgu_ref, wd_ref, o_ref):
    x = x_ref[...]
    u = _rms(x, gin_ref[...]).astype(BF16)
    fc = D_FF // FFN_SPLIT
    acc = None
    for c in range(FFN_SPLIT):
        g = _dot(u, wgu_ref[:, fc * c:fc * (c + 1)])
        up = _dot(u, wgu_ref[:, D_FF + fc * c:D_FF + fc * (c + 1)])
        h = (g * _sigmoid(g) * up).astype(BF16)
        part = _dot(h, wd_ref[fc * c:fc * (c + 1), :])
        acc = part if acc is None else acc + part
    o_ref[...] = x + _rms(acc, gout_ref[...])


def ffn(x, gin, gout, layer, wgu, wd, tm, name):
    m = x.shape[0]

    def const(shape):
        return pl.BlockSpec(shape, lambda i: (0,) * len(shape))

    tok = pl.BlockSpec((tm, D_MODEL), lambda i: (i, 0))
    return pl.pallas_call(
        _ffn_kernel,
        grid=(m // tm,),
        in_specs=[tok, const((1, D_MODEL)), const((1, D_MODEL)),
                  _layer_weight(wgu, layer), _layer_weight(wd, layer)],
        out_specs=tok,
        out_shape=jax.ShapeDtypeStruct((m, D_MODEL), F32),
        compiler_params=_params(1),
        name=name,
    )(x, gin, gout, wgu, wd)


def _reorder_w_in_t(w):
    wt = jnp.swapaxes(w, 1, 2)
    sizes = (512, 512, 512, 4, 4, 512, 512, 128, 128, 512, 512, 512, 512, 3072)
    offs = np.concatenate([[0], np.cumsum(sizes)])
    (ml_q, ml_k, ml_v, ml_i, ml_f, ml_o, sw_q, sw_k, sw_v, rt_q, rt_k, rt_v, rt_g, gates) = [
        wt[:, int(offs[i]):int(offs[i + 1])] for i in range(len(sizes))]
    pad = jnp.zeros((w.shape[0], 512 - 128 - 128 - 8, w.shape[1]), w.dtype)
    out = jnp.concatenate([ml_q, ml_k, ml_v, sw_q, sw_k, sw_v, ml_i, ml_f, pad, rt_q, rt_k, rt_v,
                           ml_o, rt_g, gates], axis=1)
    assert out.shape[1] == Z_COLS
    return out.astype(BF16)


def _row(v):
    return v.reshape(1, -1).astype(F32)


def _decoder_layer(x, z_fn, mixer_fn, xattn_ffn_fn):
    z = z_fn(x)
    x, states = mixer_fn(z, x)
    return xattn_ffn_fn(x), z, states


def kernel(x_prompt, x_sample, mem_prompt, state_mlstm_C, state_mlstm_n, state_mlstm_m, state_ret_S,
           cache_swa_k, cache_swa_v, cache_mem_k, cache_mem_v, norm_g, w_in, ml_gate_bias, ml_head_g,
           ret_head_g, swa_sinks, rel_bias, w_br_ml, w_br_swa, w_br_ret, w_out, w_cq, w_mkv, w_co,
           w_gu, w_down):
    depth = w_in.shape[0]
    bp, t, d = x_prompt.shape
    assert bp == 1 and d == D_MODEL and t % 1024 == 0
    nb, n_new, _ = x_sample.shape
    assert n_new == N_NEW
    R = SEQ_PAD

    xp = x_prompt.reshape(t, d)
    xs = jnp.pad(x_sample, ((0, 0), (0, R - n_new), (0, 0))).reshape(nb * R, d)
    mem = mem_prompt.reshape(N_MEM, d)
    half = HEAD_D // 2
    inv = ROPE_BASE ** (-jnp.arange(half, dtype=F32) / half)
    inv = jnp.concatenate([inv, inv]).reshape(1, LANES)
    rb = rel_bias.astype(F32).reshape(-1)
    mem_kt = jnp.transpose(cache_mem_k, (0, 1, 3, 4, 2)).reshape(depth, nb, X_HEADS * X_D, N_MEM)
    mem_vt = jnp.transpose(cache_mem_v, (0, 1, 3, 4, 2)).reshape(depth, nb, X_HEADS * X_D, N_MEM)

    win_t = _reorder_w_in_t(w_in)
    wml, wsw, wrt, wout = (w.astype(BF16) for w in (w_br_ml, w_br_swa, w_br_ret, w_out))
    wcq, wmkv, wco, wgu, wd = (w.astype(BF16) for w in (w_cq, w_mkv, w_co, w_gu, w_down))

    outs_p = {k: [] for k in ("C", "n", "m", "S", "k", "v", "mk", "mv")}
    outs_s = {k: [] for k in ("n", "m")}
    sample_layered = None
    swa_kt = jnp.transpose(cache_swa_k, (0, 1, 3, 4, 2)).reshape(depth, nb, LANES, CHUNK)
    swa_vt = jnp.transpose(cache_swa_v, (0, 1, 3, 4, 2)).reshape(depth, nb, LANES, CHUNK)
    for l in range(depth):
        ng = [_row(norm_g[l, i]) for i in range(7)]
        mlg, retg = _row(ml_head_g[l]), _row(ret_head_g[l])
        gb = jnp.concatenate([ml_gate_bias[l, 0], ml_gate_bias[l, 1],
                              jnp.zeros((LANES - 2 * N_HEADS,), F32)]).reshape(1, LANES).astype(F32)
        gb_rows = jnp.broadcast_to(ml_gate_bias[l].astype(F32).reshape(2 * N_HEADS, 1), (2 * N_HEADS, LANES))
        sinks = swa_sinks[l].astype(F32)

        kv = norm_matmul(mem, ng[6], wmkv, l, tm=N_MEM, tn=2 * X_HEADS * X_D, name="memory_kv")

        def seq_p(z, x):
            x1, c_, n_, m_, s_ = mixer_prompt(z, x, gb_rows, inv, sinks, rb, mlg, retg, ng[1], l,
                                              wml, wsw, wrt, wout)
            return x1, (c_, n_, m_, s_)

        xp, zp, (c_, n_, m_, s_) = _decoder_layer(
            xp,
            lambda x: norm_matmul(x, ng[0], win_t, l, tm=min(IN_PROJ_TM, t), tn=1024, name="in_proj_prompt",
                                  w_transposed=True),
            seq_p,
            lambda x: xattn_ffn_prompt(x, ng[2], ng[3], ng[4], ng[5], l, wcq, kv, wco, wgu, wd, tm=512))
        outs_p["C"].append(c_.reshape(1, N_HEADS, HEAD_D, HEAD_D))
        outs_p["n"].append(n_.reshape(1, N_HEADS, HEAD_D))
        outs_p["m"].append(m_[:, :N_HEADS])
        outs_p["S"].append(s_.reshape(1, N_HEADS, HEAD_D, HEAD_D))
        misc_last = zp[t - CHUNK:, 512 * ZC_MISC:512 * ZC_MISC + 2 * LANES]
        outs_p["k"].append(misc_last[:, :LANES].reshape(1, CHUNK, 2, SWA_D))
        outs_p["v"].append(misc_last[:, LANES:].reshape(1, CHUNK, 2, SWA_D))
        outs_p["mk"].append(kv[:, :X_HEADS * X_D].reshape(1, N_MEM, X_HEADS, X_D))
        outs_p["mv"].append(kv[:, X_HEADS * X_D:].reshape(1, N_MEM, X_HEADS, X_D))

        n0_rows = jnp.repeat(state_mlstm_n[l].astype(F32).reshape(nb, N_HEADS * HEAD_D), R, axis=0)
        m0_rows = jnp.repeat(jnp.pad(state_mlstm_m[l].astype(F32), ((0, 0), (0, LANES - N_HEADS))), R, axis=0)

        def seq_s(z, x):
            hml, osw, ort, n_, m_, *layered = mixer_sample(
                z, l, state_mlstm_C, n0_rows, m0_rows, state_ret_S, swa_kt, swa_vt, gb, inv, sinks, rb,
                sample_layered)
            x1 = mixer_post(hml, osw, ort, z, x, mlg, retg, ng[1], l, wml, wsw, wrt, wout,
                            tm=min(256, x.shape[0]), name="mixer_post_sample")
            return x1, (n_, m_, layered)

        xs, zs, (n_, m_, sample_layered) = _decoder_layer(
            xs,
            lambda x: norm_matmul(x, ng[0], win_t, l, tm=min(1024, nb * R), tn=1024, name="in_proj_sample",
                                  w_transposed=True),
            seq_s,
            lambda x: ffn(xattn_sample(x, ng[2], ng[3], wcq, l, mem_kt, mem_vt, wco),
                          ng[4], ng[5], l, wgu, wd, tm=min(512, nb * R), name="ffn_sample"))
        outs_s["n"].append(n_.reshape(nb, R, N_HEADS, HEAD_D)[:, R - 1])
        outs_s["m"].append(m_.reshape(nb, R, LANES)[:, n_new - 1, :N_HEADS])

    sample_c, sample_s, swa_kt_new, swa_vt_new = sample_layered
    s_swa_k = jnp.transpose(swa_kt_new.reshape(depth, nb, 2, SWA_D, CHUNK), (0, 1, 4, 2, 3))
    s_swa_v = jnp.transpose(swa_vt_new.reshape(depth, nb, 2, SWA_D, CHUNK), (0, 1, 4, 2, 3))
    y_p = xp.reshape(1, t, d)
    y_s = xs.reshape(nb, R, d)[:, :n_new]
    st = lambda d_, k: jnp.stack(d_[k])
    return (y_p, y_s,
            st(outs_p, "C"), st(outs_p, "n"), st(outs_p, "m"), st(outs_p, "S"),
            st(outs_p, "k"), st(outs_p, "v"), st(outs_p, "mk"), st(outs_p, "mv"),
            sample_c, st(outs_s, "n"), st(outs_s, "m"), sample_s, s_swa_k, s_swa_v)
```

```python
import functools
import math

import numpy as np
import jax
import jax.numpy as jnp
from jax import lax
from jax.experimental import pallas as pl
from jax.experimental.pallas import tpu as pltpu

F32 = jnp.float32
BF16 = jnp.bfloat16

D_MODEL = 1024
EPS = 1e-6
PAST_LEN = 16384
HEAD_D = 128
N_HEADS = 4
CHUNK = 128
SWA_HEADS = 8
SWA_D = 64
N_BUCKETS = 32
MAX_DISTANCE = 128
ROPE_BASE = 10000.0
N_MEM = 256
X_HEADS = 4
X_D = 64
D_FF = 2816
SEQ_PAD = 8
N_NEW = 4
LANES = 128
IN_PROJ_TM = 2048
MXU_MIN_ROWS = 16
VMEM_LIMIT = 48 * 1024 * 1024

ZC_ML_Q, ZC_ML_K, ZC_ML_V, ZC_SW_Q, ZC_MISC, ZC_RT_Q, ZC_RT_K, ZC_RT_V, ZC_ML_O, ZC_RT_G = range(10)
ZC_GATES = 5
Z_COLS = 8192
Z_HALF = Z_COLS // 2
MISC_K, MISC_V, MISC_IF = 0, 128, 256

LOG2E = math.log2(math.e)
LN2 = math.log(2.0)
LOG_GAMMA = tuple(float(v) for v in np.log1p(-np.exp2(-5.0 - np.arange(N_HEADS, dtype=np.float32))))

SWA_SAME = (0, 2, 5, 7)
SWA_SWAP = (1, 3, 4, 6)


def _t5_bucket_bounds():
    n = np.arange(CHUNK)
    max_exact = N_BUCKETS // 2
    nf = np.maximum(n, 1).astype(np.float32)
    large = max_exact + (np.log(nf / np.float32(max_exact)) / np.float32(math.log(MAX_DISTANCE / max_exact))
                         * np.float32(N_BUCKETS - max_exact)).astype(np.int32)
    large = np.minimum(large, N_BUCKETS - 1)
    b = np.where(n < max_exact, n, large)
    assert np.all(np.diff(b) >= 0)
    runs = []
    for d in range(CHUNK):
        if runs and runs[-1][1] == int(b[d]):
            runs[-1][0] = d + 1
        else:
            runs.append([d + 1, int(b[d])])
    return tuple((hi, bk) for hi, bk in runs)


BUCKET_RUNS = _t5_bucket_bounds()


def _dot(a, b):
    return jnp.dot(a, b, preferred_element_type=F32)


def _dot_nt(a, b):
    return lax.dot_general(a, b, (((1,), (1,)), ((), ())), preferred_element_type=F32)


def _dot_tn(a, b):
    return lax.dot_general(a, b, (((0,), (0,)), ((), ())), preferred_element_type=F32)


def _rms(x, g):
    return x * lax.rsqrt(jnp.mean(x * x, axis=-1, keepdims=True) + EPS) * g


def _sigmoid(x):
    return 1.0 / (1.0 + jnp.exp2(x * (-LOG2E)))


def _log_sigmoid(x):
    return jnp.minimum(x, 0.0) - jnp.log1p(jnp.exp(-jnp.abs(x)))


def _layer_weight(w_all, layer):
    shape = w_all.shape[1:]
    return pl.BlockSpec((None,) + shape, lambda *_: (layer,) + (0,) * len(shape), pipeline_mode=pl.Buffered(1))


def _params(n_grid):
    return pltpu.CompilerParams(dimension_semantics=("arbitrary",) * n_grid, vmem_limit_bytes=VMEM_LIMIT)


def _norm_matmul_kernel(x_ref, g_ref, w_ref, o_ref, u_ref, *, w_transposed):
    @pl.when(pl.program_id(1) == 0)
    def _():
        u_ref[...] = _rms(x_ref[...], g_ref[...]).astype(BF16)

    o_ref[...] = _dot_nt(u_ref[...], w_ref[...]) if w_transposed else _dot(u_ref[...], w_ref[...])


def norm_matmul(x, g, w_all, layer, tm, tn, name, w_transposed=False):
    m, k = x.shape
    n = w_all.shape[1] if w_transposed else w_all.shape[2]
    if w_transposed:
        w_spec = pl.BlockSpec((None, tn, k), lambda i, j: (layer, j, 0))
    else:
        w_spec = pl.BlockSpec((None, k, tn), lambda i, j: (layer, 0, j))
    return pl.pallas_call(
        functools.partial(_norm_matmul_kernel, w_transposed=w_transposed),
        grid=(m // tm, n // tn),
        in_specs=[pl.BlockSpec((tm, k), lambda i, j: (i, 0)),
                  pl.BlockSpec((1, k), lambda i, j: (0, 0)),
                  w_spec],
        out_specs=pl.BlockSpec((tm, tn), lambda i, j: (i, j)),
        out_shape=jax.ShapeDtypeStruct((m, n), F32),
        scratch_shapes=[pltpu.VMEM((tm, k), BF16)],
        compiler_params=_params(2),
        name=name,
    )(x, g, w_all)


def _transpose_bf16(x, eye):
    return _dot_nt(eye, x).astype(BF16)


def _build_swa_table_t(tab_ref, rb_ref):
    L = CHUNK
    srow = lax.broadcasted_iota(jnp.int32, (2 * L, L), 0)
    lcol = lax.broadcasted_iota(jnp.int32, (2 * L, L), 1)
    delta = lcol + L - srow
    valid = (delta >= 0) & (delta < L)
    for var, heads in enumerate((SWA_SAME, SWA_SWAP)):
        for i, h in enumerate(heads):
            val = jnp.full((2 * L, L), rb_ref[BUCKET_RUNS[-1][1] * SWA_HEADS + h], F32)
            for hi, bk in reversed(BUCKET_RUNS[:-1]):
                val = jnp.where(delta < hi, rb_ref[bk * SWA_HEADS + h], val)
            tab_ref[var, :, i * L:(i + 1) * L] = jnp.where(valid, val, -jnp.inf)


def _swa_scores_t(q, k_same, k_swap, tab_ref, prev_invalid):
    L = CHUNK
    lane = lax.broadcasted_iota(jnp.int32, (1, LANES), 1)
    upper = lane >= SWA_D
    q = q * (SWA_D ** -0.5)
    ks = (k_same[...], k_swap[...])
    scores = []
    for var, heads in enumerate((SWA_SAME, SWA_SWAP)):
        qm = []
        for h in heads:
            blk = q[:, LANES * (h // 2):LANES * (h // 2 + 1)]
            qm.append(jnp.where(upper if h % 2 == 1 else jnp.logical_not(upper), blk, 0.0))
        qs = jnp.concatenate(qm, axis=0).astype(BF16)
        s_t = _dot_nt(ks[var], qs) + tab_ref[var]
        scores.append(jnp.concatenate([jnp.where(prev_invalid, -jnp.inf, s_t[:L]), s_t[L:]], axis=0))
    return scores


def _swa_softmax_t(scores, sinks_ref):
    probs = []
    for s_t, heads in zip(scores, (SWA_SAME, SWA_SWAP)):
        sink = jnp.concatenate([jnp.full((1, CHUNK), sinks_ref[h], F32) for h in heads], axis=1)
        m = jnp.maximum(jnp.max(s_t, axis=0, keepdims=True), sink)
        p = jnp.exp(s_t - m)
        norm = 1.0 / (jnp.sum(p, axis=0, keepdims=True) + jnp.exp(sink - m))
        probs.append((p.astype(BF16), norm))
    return probs


def _swa_values_t(probs, vt_buf):
    vt = vt_buf[...]
    vts = (vt, jnp.concatenate([vt[SWA_D:], vt[:SWA_D]], axis=0))
    return [_dot(vts[var], p) * norm for var, (p, norm) in enumerate(probs)]


def _swa_output_t(values):
    L = CHUNK
    lane = lax.broadcasted_iota(jnp.int32, (1, LANES), 1)
    upper = lane >= SWA_D
    outs = [None] * SWA_HEADS
    for o_t, heads in zip(values, (SWA_SAME, SWA_SWAP)):
        for i, h in enumerate(heads):
            outs[h] = o_t[:, i * L:(i + 1) * L].T
    blocks = [jnp.where(upper, outs[2 * j + 1], outs[2 * j]) for j in range(SWA_HEADS // 2)]
    return jnp.concatenate(blocks, axis=1)


def _rotary_tables(pos, inv_ref):
    lane = lax.broadcasted_iota(jnp.int32, (1, LANES), 1)
    ang = pos * inv_ref[...]
    sin = jnp.sin(ang)
    return jnp.cos(ang), jnp.where(lane < HEAD_D // 2, -sin, sin)


def _rotate(x, cos, sin_signed):
    return x * cos + pltpu.roll(x, HEAD_D // 2, 1) * sin_signed


def _gate_slab(raw, gb_ref):
    lane = lax.broadcasted_iota(jnp.int32, (1, LANES), 1)
    x = raw + gb_ref[...]
    return jnp.where((lane >= N_HEADS) & (lane < 2 * N_HEADS), _log_sigmoid(x), x)


def _head_rms(h, gain):
    parts = []
    for i in range(N_HEADS):
        blk = h[:, HEAD_D * i:HEAD_D * (i + 1)]
        parts.append(blk * lax.rsqrt(jnp.mean(blk * blk, axis=-1, keepdims=True) + EPS))
    return jnp.concatenate(parts, axis=-1) * gain


def _rms_over_rows(h_t):
    return h_t * lax.rsqrt(jnp.mean(h_t * h_t, axis=0, keepdims=True) + EPS)


def _post_math(hml, osw, ort, mlo, rtg, g0, g1, g2, x, mlg, retg, ng, wml, wsw, wrt, wout, head_normed=False):
    if not head_normed:
        hml = _head_rms(hml, mlg)
        ort = _head_rms(ort, retg)
    hm = hml * _sigmoid(mlo)
    y_ml = _dot(hm.astype(BF16), wml)
    y_sw = _dot(osw.astype(BF16), wsw)
    rt = ort * (rtg * _sigmoid(rtg))
    y_rt = _dot(rt.astype(BF16), wrt)
    merged = _sigmoid(g0) * y_ml + _sigmoid(g1) * y_sw + _sigmoid(g2) * y_rt
    return x + _rms(_dot(merged.astype(BF16), wout), ng)


def _z_views(z_half_ref, first_col_block):
    return {first_col_block + i: z_half_ref.at[:, 512 * i:512 * (i + 1)] for i in range(Z_HALF // 512)}


def _mixer_seq_prompt_kernel(zmix_ref, ztail_ref, x_ref,
                             gbr_ref, inv_ref, sinks_ref, rb_ref,
                             mlg_ref, retg_ref, ng_ref, wml_ref, wsw_ref, wrt_ref, wout_ref,
                             x1_ref, c_ref, n_ref, m_ref, s_ref,
                             st_scr, k_same, k_swap, vt_buf, tab_t, cos_l, sin_l, dec_in_t, eye_ref,
                             hml_s, osw_s, ort_s):
    zm = _z_views(zmix_ref, 0)
    mq_ref, mk_ref, mv_ref, swq_ref, misc_ref = (zm[i] for i in (ZC_ML_Q, ZC_ML_K, ZC_ML_V, ZC_SW_Q, ZC_MISC))
    rq_ref, rk_ref, rv_ref = (zm[i] for i in (ZC_RT_Q, ZC_RT_K, ZC_RT_V))
    zt = _z_views(ztail_ref, Z_HALF // 512)
    mlo_ref, rtg_ref = zt[ZC_ML_O], zt[ZC_RT_G]
    g0_ref, g1_ref, g2_ref = (ztail_ref.at[:, 1024 * (ZC_GATES + i) - Z_HALF:1024 * (ZC_GATES + i + 1) - Z_HALF]
                              for i in range(3))
    c = pl.program_id(0)
    active = c < pl.num_programs(0) - 1
    L = CHUNK
    lane = lax.broadcasted_iota(jnp.int32, (1, LANES), 1)
    row = lax.broadcasted_iota(jnp.int32, (L, L), 0)
    col = lax.broadcasted_iota(jnp.int32, (L, L), 1)
    causal_t = row <= col

    @pl.when(c == 0)
    def _init():
        c_ref[...] = jnp.zeros(c_ref.shape, F32)
        n_ref[...] = jnp.zeros(n_ref.shape, F32)
        m_ref[...] = jnp.zeros(m_ref.shape, F32)
        s_ref[...] = jnp.zeros(s_ref.shape, F32)
        hml_s[...] = jnp.zeros(hml_s.shape, F32)
        osw_s[...] = jnp.zeros(osw_s.shape, F32)
        ort_s[...] = jnp.zeros(ort_s.shape, F32)
        st_scr[...] = jnp.zeros(st_scr.shape, F32)
        k_same[...] = jnp.zeros(k_same.shape, BF16)
        k_swap[...] = jnp.zeros(k_swap.shape, BF16)
        vt_buf[...] = jnp.zeros(vt_buf.shape, BF16)
        _build_swa_table_t(tab_t, rb_ref)
        ang = row.astype(F32) * inv_ref[...]
        cos_l[...] = jnp.cos(ang)
        sin_l[...] = jnp.sin(ang)
        rel_t = (col - row).astype(F32)
        for h in range(N_HEADS):
            dec_in_t[h] = jnp.where(causal_t, jnp.exp(LOG_GAMMA[h] * rel_t), 0.0)
        eye_ref[...] = jnp.where(row == col, 1.0, 0.0).astype(BF16)

    eye = eye_ref[...]
    heads = range(N_HEADS)
    hsl = [slice(HEAD_D * h, HEAD_D * (h + 1)) for h in heads]
    sub8 = lax.broadcasted_iota(jnp.int32, (2 * N_HEADS, 1), 0)
    hml_prev, osw_prev, ort_prev = hml_s[...], osw_s[...], ort_s[...]

    raw = misc_ref[:, MISC_IF:MISC_IF + LANES].T[0:2 * N_HEADS] + gbr_ref[...]
    gates = jnp.where(sub8 >= N_HEADS, _log_sigmoid(raw), raw)
    cum = gates
    for sh in (1, 2, 4, 8, 16, 32, 64):
        cum = cum + jnp.where(lane >= sh, pltpu.roll(cum, sh, 1), 0.0)
    g8 = jnp.where(sub8 < N_HEADS, gates - pltpu.roll(cum, N_HEADS, 0), 0.0)
    g_cols = jnp.concatenate([g8, jnp.zeros((L - 2 * N_HEADS, L), F32)], axis=0).T

    ang0 = (c * L).astype(F32) * inv_ref[...]
    cos0 = jnp.cos(ang0)
    sin0 = jnp.sin(ang0)
    cos_t = cos_l[...]
    sin_t = sin_l[...]
    cos = cos0 * cos_t - sin0 * sin_t
    sin = sin0 * cos_t + cos0 * sin_t
    sin_signed = jnp.where(lane < HEAD_D // 2, -sin, sin)
    ml_ops, rt_ops = [], []
    for h in heads:
        qb = mq_ref[:, hsl[h]].astype(BF16)
        kb = (mk_ref[:, hsl[h]] * (HEAD_D ** -0.5)).astype(BF16)
        vt = _transpose_bf16(mv_ref[:, hsl[h]].astype(BF16), eye)
        ml_ops.append((qb, kb, vt, _dot_nt(kb, qb)))
    for h in heads:
        qr = _rotate(rq_ref[:, hsl[h]], cos, sin_signed).astype(BF16)
        kr = (_rotate(rk_ref[:, hsl[h]], cos, sin_signed) * (HEAD_D ** -0.5)).astype(BF16)
        vt = _transpose_bf16(rv_ref[:, hsl[h]].astype(BF16), eye)
        rt_ops.append((qr, kr, vt, _dot_nt(kr, qr)))
    k_new = misc_ref[:, MISC_K:MISC_K + LANES]
    k_same[0:L, :] = k_same[L:2 * L, :]
    k_swap[0:L, :] = k_swap[L:2 * L, :]
    k_same[L:2 * L, :] = k_new.astype(BF16)
    k_swap[L:2 * L, :] = pltpu.roll(k_new, SWA_D, 1).astype(BF16)
    vt_buf[:, 0:L] = vt_buf[:, L:2 * L]
    vt_buf[:, L:2 * L] = _transpose_bf16(misc_ref[:, MISC_V:MISC_V + LANES].astype(BF16), eye)
    swa_scores = _swa_scores_t(swq_ref[...], k_same, k_swap, tab_t, c == 0)

    y_ml = _dot((hml_prev * _sigmoid(mlo_ref[...])).astype(BF16), wml_ref[...])

    m_all = m_ref[...]
    gate_ops = []
    for h in heads:
        m_prev = m_all[:, h:h + 1]
        gm = jnp.where(causal_t, g_cols[:, h:h + 1], -jnp.inf)
        mx = jnp.maximum(jnp.max(gm, axis=0, keepdims=True), m_prev)
        gate_ops.append((m_prev, mx, jnp.exp(m_prev - mx), jnp.exp(gm - mx)))

    y_sw = _dot(osw_prev.astype(BF16), wsw_ref[...])
    rtg = rtg_ref[...]
    y_rt = _dot((ort_prev * (rtg * _sigmoid(rtg))).astype(BF16), wrt_ref[...])

    l_row = lane.astype(F32)
    ml_out, rt_out = [], []
    for h in heads:
        qb, kb, vt, qk = ml_ops[h]
        m_prev, mx, inter, dm_t = gate_ops[h]
        s_t = qk * dm_t
        num_t = _dot(vt, s_t.astype(BF16)) + _dot_nt(c_ref[h].astype(BF16), qb) * inter
        nq = _dot_nt(jnp.broadcast_to(n_ref[h], (MXU_MIN_ROWS, HEAD_D)).astype(BF16), qb)[0:1, :]
        den = jnp.sum(s_t, axis=0, keepdims=True) + inter * nq
        m_t = cum[N_HEADS + h:N_HEADS + h + 1, :] + mx
        den = jnp.maximum(jnp.abs(den), jnp.exp(-m_t))
        ml_out.append((num_t * (1.0 / den), m_t))
    for h in heads:
        qr, kr, vt, qk = rt_ops[h]
        s_t = qk * dec_in_t[h]
        q_decay = jnp.exp(LOG_GAMMA[h] * (l_row + 1.0))
        rt_out.append(_dot(vt, s_t.astype(BF16)) + _dot_nt(st_scr[h].astype(BF16), qr) * q_decay)
    swa_values = _swa_values_t(_swa_softmax_t(swa_scores, sinks_ref), vt_buf)

    merged = _sigmoid(g0_ref[...]) * y_ml + _sigmoid(g1_ref[...]) * y_sw + _sigmoid(g2_ref[...]) * y_rt
    x1_ref[...] = x_ref[...] + _rms(_dot(merged.astype(BF16), wout_ref[...]), ng_ref[...])

    osw_s[...] = _swa_output_t(swa_values)
    for h in heads:
        hml_s[:, hsl[h]] = _rms_over_rows(ml_out[h][0]).T * mlg_ref[:, hsl[h]]
        ort_s[:, hsl[h]] = _rms_over_rows(rt_out[h]).T * retg_ref[:, hsl[h]]
    m_out = jnp.zeros((1, LANES), F32)
    for h in heads:
        qb, kb, vt, _ = ml_ops[h]
        m_prev = gate_ops[h][0]
        m_t = ml_out[h][1]
        m_new = m_t[:, L - 1:L]
        b_last = cum[N_HEADS + h:N_HEADS + h + 1, L - 1:L]
        decay = jnp.exp(b_last + m_prev - m_new)
        w_r = jnp.exp(g8[h:h + 1, :] + (b_last - m_new))
        c_old = c_ref[h]
        n_old = n_ref[h]
        c_new = decay * c_old + _dot((vt.astype(F32) * w_r).astype(BF16), kb)
        n_new = decay * n_old + _dot(jnp.broadcast_to(w_r, (MXU_MIN_ROWS, L)).astype(BF16), kb)[0:1, :]
        c_ref[h] = jnp.where(active, c_new, c_old)
        n_ref[h] = jnp.where(active, n_new, n_old)
        m_out = jnp.where(lane == h, m_new, m_out)
    m_ref[...] = jnp.where(active, m_out, m_all)
    for h in heads:
        qr, kr, vt, _ = rt_ops[h]
        lg = LOG_GAMMA[h]
        k_decay = jnp.exp(lg * (L - 1.0 - l_row))
        st_old = st_scr[h]
        st_new = math.exp(lg * L) * st_old + _dot((vt.astype(F32) * k_decay).astype(BF16), kr)
        st_scr[h] = jnp.where(active, st_new, st_old)

    @pl.when(c == pl.num_programs(0) - 1)
    def _emit_state():
        for h in heads:
            s_ref[h] = st_scr[h].T


def mixer_prompt(z, x, gb, inv, sinks, rb, mlg, retg, ng, layer, wml, wsw, wrt, wout):
    t = z.shape[0]
    L = CHUNK
    n = t // L

    z_mix = pl.BlockSpec((L, Z_HALF), lambda c: (jnp.minimum(c, n - 1), 0))

    def tail(w, cb=0):
        return pl.BlockSpec((L, w), lambda c, cb=cb: (jnp.maximum(c - 1, 0), cb))

    def const(shape):
        return pl.BlockSpec(shape, lambda c: (0,) * len(shape))

    def weight(w):
        return _layer_weight(w, layer)

    smem = pl.BlockSpec(memory_space=pltpu.SMEM)
    return pl.pallas_call(
        _mixer_seq_prompt_kernel,
        grid=(n + 1,),
        in_specs=[z_mix, tail(Z_HALF, 1), tail(D_MODEL),
                  const((2 * N_HEADS, LANES)), const((1, LANES)), smem, smem,
                  const((1, 512)), const((1, 512)), const((1, D_MODEL)),
                  weight(wml), weight(wsw), weight(wrt), weight(wout)],
        out_specs=[tail(D_MODEL),
                   const((N_HEADS, HEAD_D, HEAD_D)), const((N_HEADS, 1, HEAD_D)), const((1, LANES)),
                   const((N_HEADS, HEAD_D, HEAD_D))],
        out_shape=[jax.ShapeDtypeStruct((t, D_MODEL), F32),
                   jax.ShapeDtypeStruct((N_HEADS, HEAD_D, HEAD_D), F32),
                   jax.ShapeDtypeStruct((N_HEADS, 1, HEAD_D), F32),
                   jax.ShapeDtypeStruct((1, LANES), F32),
                   jax.ShapeDtypeStruct((N_HEADS, HEAD_D, HEAD_D), F32)],
        scratch_shapes=[pltpu.VMEM((N_HEADS, L, L), F32),
                        pltpu.VMEM((2 * L, LANES), BF16), pltpu.VMEM((2 * L, LANES), BF16),
                        pltpu.VMEM((LANES, 2 * L), BF16),
                        pltpu.VMEM((2, 2 * L, 4 * L), F32),
                        pltpu.VMEM((L, L), F32), pltpu.VMEM((L, L), F32),
                        pltpu.VMEM((N_HEADS, L, L), F32),
                        pltpu.VMEM((L, L), BF16),
                        pltpu.VMEM((L, 512), F32), pltpu.VMEM((L, 512), F32), pltpu.VMEM((L, 512), F32)],
        compiler_params=_params(1),
        name="mixer_prompt",
    )(z, z, x, gb, inv, sinks, rb, mlg, retg, ng, wml, wsw, wrt, wout)


SAMPLE_GROUP = 8


def _tile_bcast(x, l_idx, src):
    n_rows = x.shape[0]
    out = jnp.zeros_like(x)
    for j in range(SEQ_PAD):
        out = out + jnp.where(l_idx == j, pltpu.roll(x, (j - src) % n_rows, 0), 0.0)
    return out


def _tile_total(x, l_idx):
    for d in (1, 2, 4):
        x = x + jnp.where(l_idx >= d, pltpu.roll(x, d, 0), 0.0)
    return x


def _per_seq(fn, group):
    return jnp.concatenate([fn(g) for g in range(group)], axis=0)


def _build_sample_swa_tables(tab_prev, tab_cur, rb_ref, group):
    R = SEQ_PAD
    rows = group * SWA_HEADS * R
    row = lax.broadcasted_iota(jnp.int32, (rows, LANES), 0)
    col = lax.broadcasted_iota(jnp.int32, (rows, LANES), 1)
    l = row & (R - 1)
    h = (row >> 3) & (SWA_HEADS - 1)
    g = row >> 6
    delta = CHUNK + l - col
    prev = jnp.zeros((rows, LANES), F32)
    cur = jnp.zeros((rows, LANES), F32)
    dcur = l - (col & (R - 1))
    for hh in range(SWA_HEADS):
        val = jnp.full((rows, LANES), rb_ref[BUCKET_RUNS[-1][1] * SWA_HEADS + hh], F32)
        for hi, bk in reversed(BUCKET_RUNS[:-1]):
            val = jnp.where(delta < hi, rb_ref[bk * SWA_HEADS + hh], val)
        prev = jnp.where(h == hh, val, prev)
        valc = jnp.zeros((rows, LANES), F32)
        for d in range(R):
            valc = jnp.where(dcur == d, rb_ref[d * SWA_HEADS + hh], valc)
        cur = jnp.where(h == hh, valc, cur)
    tab_prev[...] = jnp.where((delta >= 0) & (delta < CHUNK), prev, -jnp.inf)
    same_seq = ((col >> 3) == g) & (col < group * R)
    tab_cur[...] = jnp.where(same_seq & (dcur >= 0), cur, -jnp.inf)


def _sample_mixers_kernel(zmix_ref, c0_ref, n0_ref, m0_ref, s0_ref, ckt_ref, cvt_ref,
                          gb_ref, inv_ref, sinks_ref, rb_ref, *rest, group, first_layer):
    zm = _z_views(zmix_ref, 0)
    mq_ref, mk_ref, mv_ref, swq_ref, misc_ref = (zm[i] for i in (ZC_ML_Q, ZC_ML_K, ZC_ML_V, ZC_SW_Q, ZC_MISC))
    rq_ref, rk_ref, rv_ref = (zm[i] for i in (ZC_RT_Q, ZC_RT_K, ZC_RT_V))
    n_layered = 4
    if not first_layer:
        rest = rest[n_layered:]
    hml_ref, osw_ref, ort_ref, n_ref, m_ref, c_all, s_all, kc_all, vc_all, tab_prev, tab_cur = rest
    layered = []
    for ref in (c_all, s_all, kc_all, vc_all):
        if first_layer:
            ref[1:] = jnp.zeros((ref.shape[0] - 1,) + ref.shape[1:], F32)
            layered.append(ref.at[0])
        else:
            layered.append(ref)
    c_ref, s_ref, kc_ref, vc_ref = layered
    R = SEQ_PAD
    NR = group * R

    @pl.when(pl.program_id(0) == 0)
    def _init():
        _build_sample_swa_tables(tab_prev, tab_cur, rb_ref, group)

    lane = lax.broadcasted_iota(jnp.int32, (1, LANES), 1)
    l_idx = lax.broadcasted_iota(jnp.int32, (NR, 1), 0) & (R - 1)
    real = l_idx < N_NEW
    l_f = l_idx.astype(F32)

    def shift(x, d):
        return x if d == 0 else pltpu.roll(x, d, 0)

    def col(slab, h):
        return slab[:, h:h + 1]

    sub = lax.broadcasted_iota(jnp.int32, (R, 1), 0)

    def ld(ref, cols=slice(None)):
        xc = ref[:, cols]
        tiles = []
        for t in range(group // 2):
            tile = xc[R * t:R * (t + 1)]
            tiles.append(jnp.where(sub < N_NEW, tile, 0.0))
            tiles.append(jnp.where(sub < N_NEW, pltpu.roll(tile, N_NEW, 0), 0.0))
        return jnp.concatenate(tiles, axis=0)

    def compact(xp):
        return jnp.concatenate(
            [jnp.where(sub < N_NEW, xp[2 * R * t:2 * R * t + R], pltpu.roll(xp[2 * R * t + R:2 * R * (t + 1)], N_NEW, 0))
             for t in range(group // 2)], axis=0)

    zero_rows = jnp.zeros((LANES - NR, LANES), F32)
    k_new = jnp.concatenate([ld(misc_ref, slice(MISC_K, MISC_K + LANES)), zero_rows], axis=0)
    v_new = jnp.concatenate([ld(misc_ref, slice(MISC_V, MISC_V + LANES)), zero_rows], axis=0)
    k_new_t = k_new.T
    v_new_t = v_new.T

    def emit_window_buffers(g):
        back = (LANES - R * g) % LANES
        for new_t, cache_ref, out_ref in ((k_new_t, ckt_ref, kc_ref), (v_new_t, cvt_ref, vc_ref)):
            merged = jnp.where(lane < N_NEW, pltpu.roll(new_t, back, 1) if back else new_t, cache_ref[g])
            out_ref[g] = pltpu.roll(merged, LANES - N_NEW, 1)

    lf = _gate_slab(ld(misc_ref, slice(MISC_IF, MISC_IF + LANES)), gb_ref)
    bsum = lf
    for d in range(1, N_NEW):
        bsum = bsum + jnp.where(l_idx >= d, shift(lf, d), 0.0)
    b = pltpu.roll(bsum, LANES - N_HEADS, 1)
    gs = lf - b
    m0 = m0_ref[...]
    log_inter = b + m0
    logd = [jnp.where(l_idx >= d, b + shift(gs, d), -jnp.inf) for d in range(N_NEW)]
    m_t = log_inter
    for d in range(N_NEW):
        m_t = jnp.maximum(m_t, logd[d])
    inter = jnp.exp(log_inter - m_t)
    dm = [jnp.exp(logd[d] - m_t) for d in range(N_NEW)]
    emt = jnp.exp(-m_t)
    b_last = _tile_bcast(b, l_idx, N_NEW - 1)
    m_new = _tile_bcast(m_t, l_idx, N_NEW - 1)
    decay = jnp.exp(b_last + m0 - m_new)
    w = jnp.where(real, jnp.exp(b_last - b + lf - m_new), 0.0)
    m_ref[...] = m_t
    n0 = n0_ref[...]
    for h in range(N_HEADS):
        hs = slice(HEAD_D * h, HEAD_D * (h + 1))
        q = ld(mq_ref, hs)
        k = ld(mk_ref, hs) * (HEAD_D ** -0.5)
        v = ld(mv_ref, hs)
        qb = q.astype(BF16)
        kb = k.astype(BF16)
        inter_c = col(inter, h)
        num = _per_seq(lambda g: _dot_nt(qb[R * g:R * (g + 1)], c0_ref[g, h].astype(BF16)), group) * inter_c
        den = inter_c * jnp.sum(q * n0[:, hs], axis=1, keepdims=True)
        for d in range(N_NEW):
            s_d = jnp.sum(q * shift(k, d), axis=1, keepdims=True) * col(dm[d], h)
            num = num + s_d * shift(v, d)
            den = den + s_d
        den = jnp.maximum(jnp.abs(den), col(emt, h))
        hml_ref[:, hs] = compact(num / den)
        w_c = col(w, h)
        dec_c = col(decay, h)
        vw = (v * w_c).astype(BF16)
        for g in range(group):
            rs = slice(R * g, R * (g + 1))
            c_ref[g, h] = dec_c[R * g:R * g + 1] * c0_ref[g, h] + _dot_tn(vw[rs], kb[rs])
        n_ref[:, hs] = dec_c * n0[:, hs] + _tile_total(k * w_c, l_idx)
        for g in range(h * group // N_HEADS, (h + 1) * group // N_HEADS):
            emit_window_buffers(g)

    cos, sin_signed = _rotary_tables((PAST_LEN + l_idx).astype(F32), inv_ref)
    for h in range(N_HEADS):
        hs = slice(HEAD_D * h, HEAD_D * (h + 1))
        lg = LOG_GAMMA[h]
        qr = _rotate(ld(rq_ref, hs), cos, sin_signed)
        kr = _rotate(ld(rk_ref, hs), cos, sin_signed) * (HEAD_D ** -0.5)
        v = ld(rv_ref, hs)
        qrb = qr.astype(BF16)
        vb = v.astype(BF16)
        o = _per_seq(lambda g: _dot(qrb[R * g:R * (g + 1)], s0_ref[g, h].astype(BF16)), group)
        o = o * jnp.exp(lg * (l_f + 1.0))
        for d in range(N_NEW):
            s_d = jnp.sum(qr * shift(kr, d), axis=1, keepdims=True) * math.exp(lg * d)
            o = o + jnp.where(l_idx >= d, s_d, 0.0) * shift(v, d)
        ort_ref[:, hs] = compact(o)
        kd =(kr * jnp.where(real, jnp.exp(lg * (N_NEW - 1.0 - l_f)), 0.0)).astype(BF16)
        for g in range(group):
            rs = slice(R * g, R * (g + 1))
            s_ref[g, h] = math.exp(lg * N_NEW) * s0_ref[g, h] + _dot_tn(kd[rs], vb[rs])

    upper = lane >= SWA_D
    q_all = ld(swq_ref) * (SWA_D ** -0.5)
    q_heads = []
    for h in range(SWA_HEADS):
        blk = q_all[:, LANES * (h // 2):LANES * (h // 2 + 1)]
        qh = jnp.where(upper if h % 2 == 1 else jnp.logical_not(upper), blk, 0.0)
        if h % 2 != h // (SWA_HEADS // 2):
            qh = pltpu.roll(qh, SWA_D, 1)
        q_heads.append(qh)
    qs = jnp.concatenate([q_heads[h][R * g:R * (g + 1)] for g in range(group) for h in range(SWA_HEADS)],
                         axis=0).astype(BF16)
    hr = SWA_HEADS * R
    s_prev = jnp.concatenate([_dot(qs[hr * g:hr * (g + 1)], ckt_ref[g].astype(BF16)) for g in range(group)],
                             axis=0) + tab_prev[...]
    s_cur = _dot_nt(qs, k_new.astype(BF16)) + tab_cur[...]
    sink64 = jnp.concatenate([jnp.full((R, 1), sinks_ref[h], F32) for h in range(SWA_HEADS)], axis=0)
    sink = jnp.concatenate([sink64] * group, axis=0)
    m = jnp.maximum(jnp.maximum(jnp.max(s_prev, axis=1, keepdims=True), jnp.max(s_cur, axis=1, keepdims=True)),
                    sink)
    p_prev = jnp.exp(s_prev - m)
    p_cur = jnp.exp(s_cur - m)
    norm = 1.0 / (jnp.sum(p_prev, axis=1, keepdims=True) + jnp.sum(p_cur, axis=1, keepdims=True)
                  + jnp.exp(sink - m))
    pb = p_prev.astype(BF16)
    o = jnp.concatenate([_dot_nt(pb[hr * g:hr * (g + 1)], cvt_ref[g].astype(BF16)) for g in range(group)],
                        axis=0)
    o = (o + _dot(p_cur.astype(BF16), v_new.astype(BF16))) * norm
    seq_tiles = []
    for g in range(group):
        blocks = []
        for j in range(SWA_HEADS // 2):
            pair = []
            for h in (2 * j, 2 * j + 1):
                oh = o[hr * g + R * h:hr * g + R * (h + 1)]
                if h % 2 != h // (SWA_HEADS // 2):
                    oh = pltpu.roll(oh, SWA_D, 1)
                pair.append(oh)
            blocks.append(jnp.where(upper, pair[1], pair[0]))
        seq_tiles.append(jnp.concatenate(blocks, axis=1))
    osw_ref[...] = compact(jnp.concatenate(seq_tiles, axis=0))


def mixer_sample(z, layer, c0_all, n0_rows, m0_rows, s0_all, ckt_all, cvt_all, gb, inv, sinks, rb, prev):
    depth, nb = c0_all.shape[:2]
    group = SAMPLE_GROUP
    R = SEQ_PAD
    NR = group * R
    first_layer = prev is None
    assert first_layer == (layer == 0)

    def const(shape):
        return pl.BlockSpec(shape, lambda i: (0,) * len(shape))

    def rows(w):
        return pl.BlockSpec((NR, w), lambda i: (i, 0))

    def tokens(w):
        return pl.BlockSpec((group * N_NEW, w), lambda i: (i, 0))

    def layered_in(shape):
        return pl.BlockSpec((None, group) + shape, lambda i: (layer, i) + (0,) * len(shape))

    def layered_out(shape):
        if first_layer:
            return pl.BlockSpec((depth, group) + shape, lambda i: (0, i) + (0,) * len(shape))
        return layered_in(shape)

    smem = pl.BlockSpec(memory_space=pltpu.SMEM)
    st = (N_HEADS, HEAD_D, HEAD_D)
    buf = (LANES, CHUNK)
    n_in = 11
    if first_layer:
        extra_specs, extra_args, aliases = [], [], {}
    else:
        extra_specs = [pl.BlockSpec(memory_space=pl.ANY)] * 4
        extra_args = list(prev)
        aliases = {n_in + i: 5 + i for i in range(4)}
    return pl.pallas_call(
        functools.partial(_sample_mixers_kernel, group=group, first_layer=first_layer),
        grid=(nb // group,),
        in_specs=[tokens(Z_HALF),
                  layered_in(st), rows(512), rows(LANES), layered_in(st),
                  layered_in(buf), layered_in(buf),
                  const((1, LANES)), const((1, LANES)), smem, smem] + extra_specs,
        out_specs=[tokens(512), tokens(512), tokens(512), rows(512), rows(LANES),
                   layered_out(st), layered_out(st), layered_out(buf), layered_out(buf)],
        out_shape=[jax.ShapeDtypeStruct((nb * N_NEW, 512), F32)] * 3
        + [jax.ShapeDtypeStruct((nb * R, 512), F32), jax.ShapeDtypeStruct((nb * R, LANES), F32),
           jax.ShapeDtypeStruct((depth, nb) + st, F32), jax.ShapeDtypeStruct((depth, nb) + st, F32),
           jax.ShapeDtypeStruct((depth, nb) + buf, F32), jax.ShapeDtypeStruct((depth, nb) + buf, F32)],
        scratch_shapes=[pltpu.VMEM((group * SWA_HEADS * R, LANES), F32),
                        pltpu.VMEM((group * SWA_HEADS * R, LANES), F32)],
        input_output_aliases=aliases,
        compiler_params=_params(1),
        name="mixer_sample",
    )(z, c0_all, n0_rows, m0_rows, s0_all, ckt_all, cvt_all, gb, inv, sinks, rb, *extra_args)


def _mixer_post_kernel(hml_ref, osw_ref, ort_ref, mlo_ref, rtg_ref, g0_ref, g1_ref, g2_ref, x_ref,
                       mlg_ref, retg_ref, ng_ref, wml_ref, wsw_ref, wrt_ref, wout_ref, o_ref):
    o_ref[...] = _post_math(hml_ref[...], osw_ref[...], ort_ref[...], mlo_ref[...], rtg_ref[...],
                            g0_ref[...], g1_ref[...], g2_ref[...], x_ref[...],
                            mlg_ref[...], retg_ref[...], ng_ref[...],
                            wml_ref[...], wsw_ref[...], wrt_ref[...], wout_ref[...])


def mixer_post(hml, osw, ort, z, x, mlg, retg, ng, layer, wml, wsw, wrt, wout, tm, name):
    m = x.shape[0]

    def tok(w, cb=0):
        return pl.BlockSpec((tm, w), lambda i, cb=cb: (i, cb))

    def const(shape):
        return pl.BlockSpec(shape, lambda i: (0,) * len(shape))

    return pl.pallas_call(
        _mixer_post_kernel,
        grid=(m // tm,),
        in_specs=[tok(512), tok(512), tok(512), tok(512, ZC_ML_O), tok(512, ZC_RT_G),
                  tok(1024, ZC_GATES), tok(1024, ZC_GATES + 1), tok(1024, ZC_GATES + 2), tok(D_MODEL),
                  const((1, 512)), const((1, 512)), const((1, D_MODEL)),
                  _layer_weight(wml, layer), _layer_weight(wsw, layer), _layer_weight(wrt, layer),
                  _layer_weight(wout, layer)],
        out_specs=tok(D_MODEL),
        out_shape=jax.ShapeDtypeStruct((m, D_MODEL), F32),
        compiler_params=_params(1),
        name=name,
    )(hml, osw, ort, z, z, z, z, z, x, mlg, retg, ng, wml, wsw, wrt, wout)


def _xattn_head(q, k, v, mask):

    s = _dot_nt(jnp.where(mask, q, 0.0).astype(BF16), k) * (X_D ** -0.5)
    m = jnp.max(s, axis=1, keepdims=True)
    p = jnp.exp(s - m)
    p = p / jnp.sum(p, axis=1, keepdims=True)
    return jnp.where(mask, _dot(p.astype(BF16), v), 0.0)


def _xattn_ffn_prompt_kernel(x_ref, gxi_ref, gxo_ref, wcq_ref, kv_ref, wco_ref, gfi_ref, gfo_ref, wgu_ref, wd_ref,
                             o_ref, mid_scr):
    @pl.when(pl.program_id(0) == 0)
    def _():
        mid_scr[...] = jnp.zeros(mid_scr.shape, F32)

    x_mid = mid_scr[...]
    x = x_ref[...]
    lane = lax.broadcasted_iota(jnp.int32, (1, X_HEADS * X_D), 1)
    masks = [(lane >= X_D * h) & (lane < X_D * (h + 1)) for h in range(X_HEADS)]
    fc = D_FF // FFN_SPLIT
    heads_per_piece = X_HEADS // FFN_SPLIT

    u = _rms(x_mid, gfi_ref[...]).astype(BF16)
    q = _dot(_rms(x, gxi_ref[...]).astype(BF16), wcq_ref[...])
    k = kv_ref[:, 0:X_HEADS * X_D].astype(BF16)
    v = kv_ref[:, X_HEADS * X_D:2 * X_HEADS * X_D].astype(BF16)
    att = jnp.zeros(q.shape, F32)
    acc = None
    for c in range(FFN_SPLIT):
        g = _dot(u, wgu_ref[:, fc * c:fc * (c + 1)])
        up = _dot(u, wgu_ref[:, D_FF + fc * c:D_FF + fc * (c + 1)])
        for h in range(heads_per_piece * c, heads_per_piece * (c + 1)):
            att = att + _xattn_head(q, k, v, masks[h])
        hid = (g * _sigmoid(g) * up).astype(BF16)
        part = _dot(hid, wd_ref[fc * c:fc * (c + 1), :])
        acc = part if acc is None else acc + part
    o_ref[...] = x_mid + _rms(acc, gfo_ref[...])
    mid_scr[...] = x + _rms(_dot(att.astype(BF16), wco_ref[...]), gxo_ref[...])


def xattn_ffn_prompt(x, gxi, gxo, gfi, gfo, layer, wcq, kv, wco, wgu, wd, tm):
    m = x.shape[0]
    n = m // tm

    def const(shape):
        return pl.BlockSpec(shape, lambda s: (0,) * len(shape))

    gain = const((1, D_MODEL))
    return pl.pallas_call(
        _xattn_ffn_prompt_kernel,
        grid=(n + 1,),
        in_specs=[pl.BlockSpec((tm, D_MODEL), lambda s: (jnp.minimum(s, n - 1), 0)),
                  gain, gain, _layer_weight(wcq, layer), const(kv.shape), _layer_weight(wco, layer),
                  gain, gain, _layer_weight(wgu, layer), _layer_weight(wd, layer)],
        out_specs=pl.BlockSpec((tm, D_MODEL), lambda s: (jnp.maximum(s - 1, 0), 0)),
        out_shape=jax.ShapeDtypeStruct((m, D_MODEL), F32),
        scratch_shapes=[pltpu.VMEM((tm, D_MODEL), F32)],
        compiler_params=_params(1),
        name="xattn_ffn_prompt",
    )(x, gxi, gxo, wcq, kv, wco, gfi, gfo, wgu, wd)


def _xattn_sample_kernel(x_ref, gin_ref, gout_ref, wcq_ref, kt_ref, vt_ref, wco_ref, o_ref, q_scr, a_scr,
                         *, group):
    R = SEQ_PAD
    x = x_ref[...]
    q_scr[...] = _dot(_rms(x, gin_ref[...]).astype(BF16), wcq_ref[...])
    lane = lax.broadcasted_iota(jnp.int32, (1, X_HEADS * X_D), 1)
    masks = [(lane >= X_D * h) & (lane < X_D * (h + 1)) for h in range(X_HEADS)]
    second = (lax.broadcasted_iota(jnp.int32, (X_HEADS * R, 1), 0) & (R - 1)) >= N_NEW

    def body(pair, carry):
        rows = pl.ds(pl.multiple_of(pair * R, R), R)
        q = q_scr[rows, :]
        qs = jnp.concatenate([jnp.where(mk, q, 0.0) for mk in masks], axis=0).astype(BF16)
        s = jnp.where(second, _dot(qs, kt_ref[2 * pair + 1].astype(BF16)), _dot(qs, kt_ref[2 * pair].astype(BF16)))
        s = s * (X_D ** -0.5)
        m = jnp.max(s, axis=1, keepdims=True)
        p = jnp.exp(s - m)
        p = (p / jnp.sum(p, axis=1, keepdims=True)).astype(BF16)
        o = jnp.where(second, _dot_nt(p, vt_ref[2 * pair + 1].astype(BF16)), _dot_nt(p, vt_ref[2 * pair].astype(BF16)))
        acc = jnp.zeros((R, X_HEADS * X_D), F32)
        for h in range(X_HEADS):
            acc = acc + jnp.where(masks[h], o[R * h:R * (h + 1)], 0.0)
        a_scr[rows, :] = acc
        return carry

    lax.fori_loop(0, group // 2, body, 0, unroll=2)
    o_ref[...] = x + _rms(_dot(a_scr[...].astype(BF16), wco_ref[...]), gout_ref[...])


def xattn_sample(x, gin, gout, wcq, layer, mkt, mvt, wco, group=16):
    nb = mkt.shape[1]
    R = N_NEW

    def const(shape):
        return pl.BlockSpec(shape, lambda i: (0,) * len(shape))

    tok = pl.BlockSpec((group * R, D_MODEL), lambda i: (i, 0))
    mem = pl.BlockSpec((None, group, X_HEADS * X_D, N_MEM), lambda i: (layer, i, 0, 0))
    return pl.pallas_call(
        functools.partial(_xattn_sample_kernel, group=group),
        grid=(nb // group,),
        in_specs=[tok, const((1, D_MODEL)), const((1, D_MODEL)), _layer_weight(wcq, layer), mem, mem,
                  _layer_weight(wco, layer)],
        out_specs=tok,
        out_shape=jax.ShapeDtypeStruct((nb * R, D_MODEL), F32),
        scratch_shapes=[pltpu.VMEM((group * R, X_HEADS * X_D), F32),
                        pltpu.VMEM((group * R, X_HEADS * X_D), F32)],
        compiler_params=_params(1),
        name="xattn_sample",
    )(x, gin, gout, wcq, mkt, mvt, wco)


FFN_SPLIT = 2


def _ffn_kernel(x_ref, gin_ref, gout_ref, wgu_ref, wd_ref, o_ref):
    x = x_ref[...]
    u = _rms(x, gin_ref[...]).astype(BF16)
    fc = D_FF // FFN_SPLIT
    acc = None
    for c in range(FFN_SPLIT):
        g = _dot(u, wgu_ref[:, fc * c:fc * (c + 1)])
        up = _dot(u, wgu_ref[:, D_FF + fc * c:D_FF + fc * (c + 1)])
        h = (g * _sigmoid(g) * up).astype(BF16)
        part = _dot(h, wd_ref[fc * c:fc * (c + 1), :])
        acc = part if acc is None else acc + part
    o_ref[...] = x + _rms(acc, gout_ref[...])


def ffn(x, gin, gout, layer, wgu, wd, tm, name):
    m = x.shape[0]

    def const(shape):
        return pl.BlockSpec(shape, lambda i: (0,) * len(shape))

    tok = pl.BlockSpec((tm, D_MODEL), lambda i: (i, 0))
    return pl.pallas_call(
        _ffn_kernel,
        grid=(m // tm,),
        in_specs=[tok, const((1, D_MODEL)), const((1, D_MODEL)),
                  _layer_weight(wgu, layer), _layer_weight(wd, layer)],
        out_specs=tok,
        out_shape=jax.ShapeDtypeStruct((m, D_MODEL), F32),
        compiler_params=_params(1),
        name=name,
    )(x, gin, gout, wgu, wd)


def _reorder_w_in_t(w):
    wt = jnp.swapaxes(w, 1, 2)
    sizes = (512, 512, 512, 4, 4, 512, 512, 128, 128, 512, 512, 512, 512, 3072)
    offs = np.concatenate([[0], np.cumsum(sizes)])
    (ml_q, ml_k, ml_v, ml_i, ml_f, ml_o, sw_q, sw_k, sw_v, rt_q, rt_k, rt_v, rt_g, gates) = [
        wt[:, int(offs[i]):int(offs[i + 1])] for i in range(len(sizes))]
    pad = jnp.zeros((w.shape[0], 512 - 128 - 128 - 8, w.shape[1]), w.dtype)
    out = jnp.concatenate([ml_q, ml_k, ml_v, sw_q, sw_k, sw_v, ml_i, ml_f, pad, rt_q, rt_k, rt_v,
                           ml_o, rt_g, gates], axis=1)
    assert out.shape[1] == Z_COLS
    return out.astype(BF16)


def _row(v):
    return v.reshape(1, -1).astype(F32)


def _decoder_layer(x, z_fn, mixer_fn, xattn_ffn_fn):
    z = z_fn(x)
    x, states = mixer_fn(z, x)
    return xattn_ffn_fn(x), z, states


def kernel(x_prompt, x_sample, mem_prompt, state_mlstm_C, state_mlstm_n, state_mlstm_m, state_ret_S,
           cache_swa_k, cache_swa_v, cache_mem_k, cache_mem_v, norm_g, w_in, ml_gate_bias, ml_head_g,
           ret_head_g, swa_sinks, rel_bias, w_br_ml, w_br_swa, w_br_ret, w_out, w_cq, w_mkv, w_co,
           w_gu, w_down):
    depth = w_in.shape[0]
    bp, t, d = x_prompt.shape
    assert bp == 1 and d == D_MODEL and t % 1024 == 0
    nb, n_new, _ = x_sample.shape
    assert n_new == N_NEW
    R = SEQ_PAD

    xp = x_prompt.reshape(t, d)
    xs = x_sample.reshape(nb * n_new, d)
    mem = mem_prompt.reshape(N_MEM, d)
    half = HEAD_D // 2
    inv = ROPE_BASE ** (-jnp.arange(half, dtype=F32) / half)
    inv = jnp.concatenate([inv, inv]).reshape(1, LANES)
    rb = rel_bias.astype(F32).reshape(-1)
    mem_kt = jnp.transpose(cache_mem_k, (0, 1, 3, 4, 2)).reshape(depth, nb, X_HEADS * X_D, N_MEM)
    mem_vt = jnp.transpose(cache_mem_v, (0, 1, 3, 4, 2)).reshape(depth, nb, X_HEADS * X_D, N_MEM)

    win_t = _reorder_w_in_t(w_in)
    wml, wsw, wrt, wout = (w.astype(BF16) for w in (w_br_ml, w_br_swa, w_br_ret, w_out))
    wcq, wmkv, wco, wgu, wd = (w.astype(BF16) for w in (w_cq, w_mkv, w_co, w_gu, w_down))

    outs_p = {k: [] for k in ("C", "n", "m", "S", "k", "v", "mk", "mv")}
    outs_s = {k: [] for k in ("n", "m")}
    sample_layered = None
    swa_kt = jnp.transpose(cache_swa_k, (0, 1, 3, 4, 2)).reshape(depth, nb, LANES, CHUNK)
    swa_vt = jnp.transpose(cache_swa_v, (0, 1, 3, 4, 2)).reshape(depth, nb, LANES, CHUNK)
    for l in range(depth):
        ng = [_row(norm_g[l, i]) for i in range(7)]
        mlg, retg = _row(ml_head_g[l]), _row(ret_head_g[l])
        gb = jnp.concatenate([ml_gate_bias[l, 0], ml_gate_bias[l, 1],
                              jnp.zeros((LANES - 2 * N_HEADS,), F32)]).reshape(1, LANES).astype(F32)
        gb_rows = jnp.broadcast_to(ml_gate_bias[l].astype(F32).reshape(2 * N_HEADS, 1), (2 * N_HEADS, LANES))
        sinks = swa_sinks[l].astype(F32)

        kv = norm_matmul(mem, ng[6], wmkv, l, tm=N_MEM, tn=2 * X_HEADS * X_D, name="memory_kv")

        def seq_p(z, x):
            x1, c_, n_, m_, s_ = mixer_prompt(z, x, gb_rows, inv, sinks, rb, mlg, retg, ng[1], l,
                                              wml, wsw, wrt, wout)
            return x1, (c_, n_, m_, s_)

        xp, zp, (c_, n_, m_, s_) = _decoder_layer(
            xp,
            lambda x: norm_matmul(x, ng[0], win_t, l, tm=min(IN_PROJ_TM, t), tn=1024, name="in_proj_prompt",
                                  w_transposed=True),
            seq_p,
            lambda x: xattn_ffn_prompt(x, ng[2], ng[3], ng[4], ng[5], l, wcq, kv, wco, wgu, wd, tm=512))
        outs_p["C"].append(c_.reshape(1, N_HEADS, HEAD_D, HEAD_D))
        outs_p["n"].append(n_.reshape(1, N_HEADS, HEAD_D))
        outs_p["m"].append(m_[:, :N_HEADS])
        outs_p["S"].append(s_.reshape(1, N_HEADS, HEAD_D, HEAD_D))
        misc_last = zp[t - CHUNK:, 512 * ZC_MISC:512 * ZC_MISC + 2 * LANES]
        outs_p["k"].append(misc_last[:, :LANES].reshape(1, CHUNK, 2, SWA_D))
        outs_p["v"].append(misc_last[:, LANES:].reshape(1, CHUNK, 2, SWA_D))
        outs_p["mk"].append(kv[:, :X_HEADS * X_D].reshape(1, N_MEM, X_HEADS, X_D))
        outs_p["mv"].append(kv[:, X_HEADS * X_D:].reshape(1, N_MEM, X_HEADS, X_D))

        n0_rows = jnp.repeat(state_mlstm_n[l].astype(F32).reshape(nb, N_HEADS * HEAD_D), R, axis=0)
        m0_rows = jnp.repeat(jnp.pad(state_mlstm_m[l].astype(F32), ((0, 0), (0, LANES - N_HEADS))), R, axis=0)

        def seq_s(z, x):
            hml, osw, ort, n_, m_, *layered = mixer_sample(
                z, l, state_mlstm_C, n0_rows, m0_rows, state_ret_S, swa_kt, swa_vt, gb, inv, sinks, rb,
                sample_layered)
            x1 = mixer_post(hml, osw, ort, z, x, mlg, retg, ng[1], l, wml, wsw, wrt, wout,
                            tm=min(256, x.shape[0]), name="mixer_post_sample")
            return x1, (n_, m_, layered)

        xs, zs, (n_, m_, sample_layered) = _decoder_layer(
            xs,
            lambda x: norm_matmul(x, ng[0], win_t, l, tm=min(1024, nb * n_new), tn=1024, name="in_proj_sample",
                                  w_transposed=True),
            seq_s,
            lambda x: ffn(xattn_sample(x, ng[2], ng[3], wcq, l, mem_kt, mem_vt, wco),
                          ng[4], ng[5], l, wgu, wd, tm=min(512, nb * n_new), name="ffn_sample"))
        outs_s["n"].append(n_.reshape(nb, R, N_HEADS, HEAD_D)[:, R - 1])
        outs_s["m"].append(m_.reshape(nb, R, LANES)[:, n_new - 1, :N_HEADS])

    sample_c, sample_s, swa_kt_new, swa_vt_new = sample_layered
    s_swa_k = jnp.transpose(swa_kt_new.reshape(depth, nb, 2, SWA_D, CHUNK), (0, 1, 4, 2, 3))
    s_swa_v = jnp.transpose(swa_vt_new.reshape(depth, nb, 2, SWA_D, CHUNK), (0, 1, 4, 2, 3))
    y_p = xp.reshape(1, t, d)
    y_s = xs.reshape(nb, n_new, d)
    st = lambda d_, k: jnp.stack(d_[k])
    return (y_p, y_s,
            st(outs_p, "C"), st(outs_p, "n"), st(outs_p, "m"), st(outs_p, "S"),
            st(outs_p, "k"), st(outs_p, "v"), st(outs_p, "mk"), st(outs_p, "mv"),
            sample_c, st(outs_s, "n"), st(outs_s, "m"), sample_s, s_swa_k, s_swa_v)
```

```python
import functools
import math

import numpy as np
import jax
import jax.numpy as jnp
from jax import lax
from jax.experimental import pallas as pl
from jax.experimental.pallas import tpu as pltpu

F32 = jnp.float32
BF16 = jnp.bfloat16

D_MODEL = 1024
EPS = 1e-6
PAST_LEN = 16384
HEAD_D = 128
N_HEADS = 4
CHUNK = 128
SWA_HEADS = 8
SWA_D = 64
N_BUCKETS = 32
MAX_DISTANCE = 128
ROPE_BASE = 10000.0
N_MEM = 256
X_HEADS = 4
X_D = 64
D_FF = 2816
SEQ_PAD = 8
N_NEW = 4
LANES = 128
XATTN_FFN_TM = 1024
IN_PROJ_TM = 2048
MXU_MIN_ROWS = 16
VMEM_LIMIT = 48 * 1024 * 1024

ZC_ML_Q, ZC_ML_K, ZC_ML_V, ZC_SW_Q, ZC_MISC, ZC_RT_Q, ZC_RT_K, ZC_RT_V, ZC_ML_O, ZC_RT_G = range(10)
ZC_GATES = 5
Z_COLS = 8192
Z_HALF = Z_COLS // 2
MISC_K, MISC_V, MISC_IF = 0, 128, 256

LOG2E = math.log2(math.e)
LN2 = math.log(2.0)
LOG_GAMMA = tuple(float(v) for v in np.log1p(-np.exp2(-5.0 - np.arange(N_HEADS, dtype=np.float32))))

SWA_SAME = (0, 2, 5, 7)
SWA_SWAP = (1, 3, 4, 6)


def _t5_bucket_bounds():
    n = np.arange(CHUNK)
    max_exact = N_BUCKETS // 2
    nf = np.maximum(n, 1).astype(np.float32)
    large = max_exact + (np.log(nf / np.float32(max_exact)) / np.float32(math.log(MAX_DISTANCE / max_exact))
                         * np.float32(N_BUCKETS - max_exact)).astype(np.int32)
    large = np.minimum(large, N_BUCKETS - 1)
    b = np.where(n < max_exact, n, large)
    assert np.all(np.diff(b) >= 0)
    runs = []
    for d in range(CHUNK):
        if runs and runs[-1][1] == int(b[d]):
            runs[-1][0] = d + 1
        else:
            runs.append([d + 1, int(b[d])])
    return tuple((hi, bk) for hi, bk in runs)


BUCKET_RUNS = _t5_bucket_bounds()


def _dot(a, b):
    return jnp.dot(a, b, preferred_element_type=F32)


def _dot_nt(a, b):
    return lax.dot_general(a, b, (((1,), (1,)), ((), ())), preferred_element_type=F32)


def _dot_tn(a, b):
    return lax.dot_general(a, b, (((0,), (0,)), ((), ())), preferred_element_type=F32)


def _rms(x, g):
    return x * lax.rsqrt(jnp.mean(x * x, axis=-1, keepdims=True) + EPS) * g


def _sigmoid(x):
    return 1.0 / (1.0 + jnp.exp2(x * (-LOG2E)))


def _log_sigmoid(x):
    return jnp.minimum(x, 0.0) - jnp.log1p(jnp.exp(-jnp.abs(x)))


def _layer_weight(w_all, layer):
    shape = w_all.shape[1:]
    return pl.BlockSpec((None,) + shape, lambda *_: (layer,) + (0,) * len(shape), pipeline_mode=pl.Buffered(1))


def _params(n_grid):
    return pltpu.CompilerParams(dimension_semantics=("arbitrary",) * n_grid, vmem_limit_bytes=VMEM_LIMIT)


def _norm_matmul_kernel(x_ref, g_ref, w_ref, o_ref, u_ref, *, w_transposed):
    @pl.when(pl.program_id(1) == 0)
    def _():
        u_ref[...] = _rms(x_ref[...], g_ref[...]).astype(BF16)

    o_ref[...] = _dot_nt(u_ref[...], w_ref[...]) if w_transposed else _dot(u_ref[...], w_ref[...])


def norm_matmul(x, g, w_all, layer, tm, tn, name, w_transposed=False):
    m, k = x.shape
    n = w_all.shape[1] if w_transposed else w_all.shape[2]
    if w_transposed:
        w_spec = pl.BlockSpec((None, tn, k), lambda i, j: (layer, j, 0))
    else:
        w_spec = pl.BlockSpec((None, k, tn), lambda i, j: (layer, 0, j))
    return pl.pallas_call(
        functools.partial(_norm_matmul_kernel, w_transposed=w_transposed),
        grid=(m // tm, n // tn),
        in_specs=[pl.BlockSpec((tm, k), lambda i, j: (i, 0)),
                  pl.BlockSpec((1, k), lambda i, j: (0, 0)),
                  w_spec],
        out_specs=pl.BlockSpec((tm, tn), lambda i, j: (i, j)),
        out_shape=jax.ShapeDtypeStruct((m, n), F32),
        scratch_shapes=[pltpu.VMEM((tm, k), BF16)],
        compiler_params=_params(2),
        name=name,
    )(x, g, w_all)


def _transpose_bf16(x, eye):
    return _dot_nt(eye, x).astype(BF16)


def _build_swa_table_t(tab_ref, rb_ref):
    L = CHUNK
    srow = lax.broadcasted_iota(jnp.int32, (2 * L, L), 0)
    lcol = lax.broadcasted_iota(jnp.int32, (2 * L, L), 1)
    delta = lcol + L - srow
    valid = (delta >= 0) & (delta < L)
    for var, heads in enumerate((SWA_SAME, SWA_SWAP)):
        for i, h in enumerate(heads):
            val = jnp.full((2 * L, L), rb_ref[BUCKET_RUNS[-1][1] * SWA_HEADS + h], F32)
            for hi, bk in reversed(BUCKET_RUNS[:-1]):
                val = jnp.where(delta < hi, rb_ref[bk * SWA_HEADS + h], val)
            tab_ref[var, :, i * L:(i + 1) * L] = jnp.where(valid, val, -jnp.inf)


def _swa_scores_t(q, k_same, k_swap, tab_ref, prev_invalid):
    L = CHUNK
    lane = lax.broadcasted_iota(jnp.int32, (1, LANES), 1)
    upper = lane >= SWA_D
    q = q * (SWA_D ** -0.5)
    ks = (k_same[...], k_swap[...])
    scores = []
    for var, heads in enumerate((SWA_SAME, SWA_SWAP)):
        qm = []
        for h in heads:
            blk = q[:, LANES * (h // 2):LANES * (h // 2 + 1)]
            qm.append(jnp.where(upper if h % 2 == 1 else jnp.logical_not(upper), blk, 0.0))
        qs = jnp.concatenate(qm, axis=0).astype(BF16)
        s_t = _dot_nt(ks[var], qs) + tab_ref[var]
        scores.append(jnp.concatenate([jnp.where(prev_invalid, -jnp.inf, s_t[:L]), s_t[L:]], axis=0))
    return scores


def _swa_softmax_t(scores, sinks_ref):
    probs = []
    for s_t, heads in zip(scores, (SWA_SAME, SWA_SWAP)):
        sink = jnp.concatenate([jnp.full((1, CHUNK), sinks_ref[h], F32) for h in heads], axis=1)
        m = jnp.maximum(jnp.max(s_t, axis=0, keepdims=True), sink)
        p = jnp.exp(s_t - m)
        norm = 1.0 / (jnp.sum(p, axis=0, keepdims=True) + jnp.exp(sink - m))
        probs.append((p.astype(BF16), norm))
    return probs


def _swa_values_t(probs, vt_buf):
    vt = vt_buf[...]
    vts = (vt, jnp.concatenate([vt[SWA_D:], vt[:SWA_D]], axis=0))
    return [_dot(vts[var], p) * norm for var, (p, norm) in enumerate(probs)]


def _swa_output_t(values):
    L = CHUNK
    lane = lax.broadcasted_iota(jnp.int32, (1, LANES), 1)
    upper = lane >= SWA_D
    outs = [None] * SWA_HEADS
    for o_t, heads in zip(values, (SWA_SAME, SWA_SWAP)):
        for i, h in enumerate(heads):
            outs[h] = o_t[:, i * L:(i + 1) * L].T
    blocks = [jnp.where(upper, outs[2 * j + 1], outs[2 * j]) for j in range(SWA_HEADS // 2)]
    return jnp.concatenate(blocks, axis=1)


def _rotary_tables(pos, inv_ref):
    lane = lax.broadcasted_iota(jnp.int32, (1, LANES), 1)
    ang = pos * inv_ref[...]
    sin = jnp.sin(ang)
    return jnp.cos(ang), jnp.where(lane < HEAD_D // 2, -sin, sin)


def _rotate(x, cos, sin_signed):
    return x * cos + pltpu.roll(x, HEAD_D // 2, 1) * sin_signed


def _gate_slab(raw, gb_ref):
    lane = lax.broadcasted_iota(jnp.int32, (1, LANES), 1)
    x = raw + gb_ref[...]
    return jnp.where((lane >= N_HEADS) & (lane < 2 * N_HEADS), _log_sigmoid(x), x)


def _head_rms(h, gain):
    parts = []
    for i in range(N_HEADS):
        blk = h[:, HEAD_D * i:HEAD_D * (i + 1)]
        parts.append(blk * lax.rsqrt(jnp.mean(blk * blk, axis=-1, keepdims=True) + EPS))
    return jnp.concatenate(parts, axis=-1) * gain


def _rms_over_rows(h_t):
    return h_t * lax.rsqrt(jnp.mean(h_t * h_t, axis=0, keepdims=True) + EPS)


def _post_math(hml, osw, ort, mlo, rtg, g0, g1, g2, x, mlg, retg, ng, wml, wsw, wrt, wout, head_normed=False):
    if not head_normed:
        hml = _head_rms(hml, mlg)
        ort = _head_rms(ort, retg)
    hm = hml * _sigmoid(mlo)
    y_ml = _dot(hm.astype(BF16), wml)
    y_sw = _dot(osw.astype(BF16), wsw)
    rt = ort * (rtg * _sigmoid(rtg))
    y_rt = _dot(rt.astype(BF16), wrt)
    merged = _sigmoid(g0) * y_ml + _sigmoid(g1) * y_sw + _sigmoid(g2) * y_rt
    return x + _rms(_dot(merged.astype(BF16), wout), ng)


def _z_views(z_half_ref, first_col_block):
    return {first_col_block + i: z_half_ref.at[:, 512 * i:512 * (i + 1)] for i in range(Z_HALF // 512)}


def _mixer_seq_prompt_kernel(zmix_ref, ztail_ref, x_ref,
                             gbr_ref, inv_ref, sinks_ref, rb_ref,
                             mlg_ref, retg_ref, ng_ref, wml_ref, wsw_ref, wrt_ref, wout_ref,
                             x1_ref, c_ref, n_ref, m_ref, s_ref,
                             st_scr, k_same, k_swap, vt_buf, tab_t, cos_l, sin_l, dec_in_t, eye_ref,
                             hml_s, osw_s, ort_s):
    zm = _z_views(zmix_ref, 0)
    mq_ref, mk_ref, mv_ref, swq_ref, misc_ref = (zm[i] for i in (ZC_ML_Q, ZC_ML_K, ZC_ML_V, ZC_SW_Q, ZC_MISC))
    rq_ref, rk_ref, rv_ref = (zm[i] for i in (ZC_RT_Q, ZC_RT_K, ZC_RT_V))
    zt = _z_views(ztail_ref, Z_HALF // 512)
    mlo_ref, rtg_ref = zt[ZC_ML_O], zt[ZC_RT_G]
    g0_ref, g1_ref, g2_ref = (ztail_ref.at[:, 1024 * (ZC_GATES + i) - Z_HALF:1024 * (ZC_GATES + i + 1) - Z_HALF]
                              for i in range(3))
    c = pl.program_id(0)
    active = c < pl.num_programs(0) - 1
    L = CHUNK
    lane = lax.broadcasted_iota(jnp.int32, (1, LANES), 1)
    row = lax.broadcasted_iota(jnp.int32, (L, L), 0)
    col = lax.broadcasted_iota(jnp.int32, (L, L), 1)
    causal_t = row <= col

    @pl.when(c == 0)
    def _init():
        c_ref[...] = jnp.zeros(c_ref.shape, F32)
        n_ref[...] = jnp.zeros(n_ref.shape, F32)
        m_ref[...] = jnp.zeros(m_ref.shape, F32)
        s_ref[...] = jnp.zeros(s_ref.shape, F32)
        hml_s[...] = jnp.zeros(hml_s.shape, F32)
        osw_s[...] = jnp.zeros(osw_s.shape, F32)
        ort_s[...] = jnp.zeros(ort_s.shape, F32)
        st_scr[...] = jnp.zeros(st_scr.shape, F32)
        k_same[...] = jnp.zeros(k_same.shape, BF16)
        k_swap[...] = jnp.zeros(k_swap.shape, BF16)
        vt_buf[...] = jnp.zeros(vt_buf.shape, BF16)
        _build_swa_table_t(tab_t, rb_ref)
        ang = row.astype(F32) * inv_ref[...]
        cos_l[...] = jnp.cos(ang)
        sin_l[...] = jnp.sin(ang)
        rel_t = (col - row).astype(F32)
        for h in range(N_HEADS):
            dec_in_t[h] = jnp.where(causal_t, jnp.exp(LOG_GAMMA[h] * rel_t), 0.0)
        eye_ref[...] = jnp.where(row == col, 1.0, 0.0).astype(BF16)

    eye = eye_ref[...]
    heads = range(N_HEADS)
    hsl = [slice(HEAD_D * h, HEAD_D * (h + 1)) for h in heads]
    sub8 = lax.broadcasted_iota(jnp.int32, (2 * N_HEADS, 1), 0)
    hml_prev, osw_prev, ort_prev = hml_s[...], osw_s[...], ort_s[...]

    raw = misc_ref[:, MISC_IF:MISC_IF + LANES].T[0:2 * N_HEADS] + gbr_ref[...]
    gates = jnp.where(sub8 >= N_HEADS, _log_sigmoid(raw), raw)
    cum = gates
    for sh in (1, 2, 4, 8, 16, 32, 64):
        cum = cum + jnp.where(lane >= sh, pltpu.roll(cum, sh, 1), 0.0)
    g8 = jnp.where(sub8 < N_HEADS, gates - pltpu.roll(cum, N_HEADS, 0), 0.0)
    g_cols = jnp.concatenate([g8, jnp.zeros((L - 2 * N_HEADS, L), F32)], axis=0).T

    ang0 = (c * L).astype(F32) * inv_ref[...]
    cos0 = jnp.cos(ang0)
    sin0 = jnp.sin(ang0)
    cos_t = cos_l[...]
    sin_t = sin_l[...]
    cos = cos0 * cos_t - sin0 * sin_t
    sin = sin0 * cos_t + cos0 * sin_t
    sin_signed = jnp.where(lane < HEAD_D // 2, -sin, sin)
    ml_ops, rt_ops = [], []
    for h in heads:
        qb = mq_ref[:, hsl[h]].astype(BF16)
        kb = (mk_ref[:, hsl[h]] * (HEAD_D ** -0.5)).astype(BF16)
        vt = _transpose_bf16(mv_ref[:, hsl[h]].astype(BF16), eye)
        ml_ops.append((qb, kb, vt, _dot_nt(kb, qb)))
    for h in heads:
        qr = _rotate(rq_ref[:, hsl[h]], cos, sin_signed).astype(BF16)
        kr = (_rotate(rk_ref[:, hsl[h]], cos, sin_signed) * (HEAD_D ** -0.5)).astype(BF16)
        vt = _transpose_bf16(rv_ref[:, hsl[h]].astype(BF16), eye)
        rt_ops.append((qr, kr, vt, _dot_nt(kr, qr)))
    k_new = misc_ref[:, MISC_K:MISC_K + LANES]
    k_same[0:L, :] = k_same[L:2 * L, :]
    k_swap[0:L, :] = k_swap[L:2 * L, :]
    k_same[L:2 * L, :] = k_new.astype(BF16)
    k_swap[L:2 * L, :] = pltpu.roll(k_new, SWA_D, 1).astype(BF16)
    vt_buf[:, 0:L] = vt_buf[:, L:2 * L]
    vt_buf[:, L:2 * L] = _transpose_bf16(misc_ref[:, MISC_V:MISC_V + LANES].astype(BF16), eye)
    swa_scores = _swa_scores_t(swq_ref[...], k_same, k_swap, tab_t, c == 0)

    y_ml = _dot((hml_prev * _sigmoid(mlo_ref[...])).astype(BF16), wml_ref[...])

    m_all = m_ref[...]
    gate_ops = []
    for h in heads:
        m_prev = m_all[:, h:h + 1]
        gm = jnp.where(causal_t, g_cols[:, h:h + 1], -jnp.inf)
        mx = jnp.maximum(jnp.max(gm, axis=0, keepdims=True), m_prev)
        gate_ops.append((m_prev, mx, jnp.exp(m_prev - mx), jnp.exp(gm - mx)))

    y_sw = _dot(osw_prev.astype(BF16), wsw_ref[...])
    rtg = rtg_ref[...]
    y_rt = _dot((ort_prev * (rtg * _sigmoid(rtg))).astype(BF16), wrt_ref[...])

    l_row = lane.astype(F32)
    ml_out, rt_out = [], []
    for h in heads:
        qb, kb, vt, qk = ml_ops[h]
        m_prev, mx, inter, dm_t = gate_ops[h]
        s_t = qk * dm_t
        num_t = _dot(vt, s_t.astype(BF16)) + _dot_nt(c_ref[h].astype(BF16), qb) * inter
        nq = _dot_nt(jnp.broadcast_to(n_ref[h], (MXU_MIN_ROWS, HEAD_D)).astype(BF16), qb)[0:1, :]
        den = jnp.sum(s_t, axis=0, keepdims=True) + inter * nq
        m_t = cum[N_HEADS + h:N_HEADS + h + 1, :] + mx
        den = jnp.maximum(jnp.abs(den), jnp.exp(-m_t))
        ml_out.append((num_t * (1.0 / den), m_t))
    for h in heads:
        qr, kr, vt, qk = rt_ops[h]
        s_t = qk * dec_in_t[h]
        q_decay = jnp.exp(LOG_GAMMA[h] * (l_row + 1.0))
        rt_out.append(_dot(vt, s_t.astype(BF16)) + _dot_nt(st_scr[h].astype(BF16), qr) * q_decay)
    swa_values = _swa_values_t(_swa_softmax_t(swa_scores, sinks_ref), vt_buf)

    merged = _sigmoid(g0_ref[...]) * y_ml + _sigmoid(g1_ref[...]) * y_sw + _sigmoid(g2_ref[...]) * y_rt
    x1_ref[...] = x_ref[...] + _rms(_dot(merged.astype(BF16), wout_ref[...]), ng_ref[...])

    osw_s[...] = _swa_output_t(swa_values)
    for h in heads:
        hml_s[:, hsl[h]] = _rms_over_rows(ml_out[h][0]).T * mlg_ref[:, hsl[h]]
        ort_s[:, hsl[h]] = _rms_over_rows(rt_out[h]).T * retg_ref[:, hsl[h]]
    m_out = jnp.zeros((1, LANES), F32)
    for h in heads:
        qb, kb, vt, _ = ml_ops[h]
        m_prev = gate_ops[h][0]
        m_t = ml_out[h][1]
        m_new = m_t[:, L - 1:L]
        b_last = cum[N_HEADS + h:N_HEADS + h + 1, L - 1:L]
        decay = jnp.exp(b_last + m_prev - m_new)
        w_r = jnp.exp(g8[h:h + 1, :] + (b_last - m_new))
        c_old = c_ref[h]
        n_old = n_ref[h]
        c_new = decay * c_old + _dot((vt.astype(F32) * w_r).astype(BF16), kb)
        n_new = decay * n_old + _dot(jnp.broadcast_to(w_r, (MXU_MIN_ROWS, L)).astype(BF16), kb)[0:1, :]
        c_ref[h] = jnp.where(active, c_new, c_old)
        n_ref[h] = jnp.where(active, n_new, n_old)
        m_out = jnp.where(lane == h, m_new, m_out)
    m_ref[...] = jnp.where(active, m_out, m_all)
    for h in heads:
        qr, kr, vt, _ = rt_ops[h]
        lg = LOG_GAMMA[h]
        k_decay = jnp.exp(lg * (L - 1.0 - l_row))
        st_old = st_scr[h]
        st_new = math.exp(lg * L) * st_old + _dot((vt.astype(F32) * k_decay).astype(BF16), kr)
        st_scr[h] = jnp.where(active, st_new, st_old)

    @pl.when(c == pl.num_programs(0) - 1)
    def _emit_state():
        for h in heads:
            s_ref[h] = st_scr[h].T


def mixer_prompt(z, x, gb, inv, sinks, rb, mlg, retg, ng, layer, wml, wsw, wrt, wout):
    t = z.shape[0]
    L = CHUNK
    n = t // L

    z_mix = pl.BlockSpec((L, Z_HALF), lambda c: (jnp.minimum(c, n - 1), 0))

    def tail(w, cb=0):
        return pl.BlockSpec((L, w), lambda c, cb=cb: (jnp.maximum(c - 1, 0), cb))

    def const(shape):
        return pl.BlockSpec(shape, lambda c: (0,) * len(shape))

    def weight(w):
        return _layer_weight(w, layer)

    smem = pl.BlockSpec(memory_space=pltpu.SMEM)
    return pl.pallas_call(
        _mixer_seq_prompt_kernel,
        grid=(n + 1,),
        in_specs=[z_mix, tail(Z_HALF, 1), tail(D_MODEL),
                  const((2 * N_HEADS, LANES)), const((1, LANES)), smem, smem,
                  const((1, 512)), const((1, 512)), const((1, D_MODEL)),
                  weight(wml), weight(wsw), weight(wrt), weight(wout)],
        out_specs=[tail(D_MODEL),
                   const((N_HEADS, HEAD_D, HEAD_D)), const((N_HEADS, 1, HEAD_D)), const((1, LANES)),
                   const((N_HEADS, HEAD_D, HEAD_D))],
        out_shape=[jax.ShapeDtypeStruct((t, D_MODEL), F32),
                   jax.ShapeDtypeStruct((N_HEADS, HEAD_D, HEAD_D), F32),
                   jax.ShapeDtypeStruct((N_HEADS, 1, HEAD_D), F32),
                   jax.ShapeDtypeStruct((1, LANES), F32),
                   jax.ShapeDtypeStruct((N_HEADS, HEAD_D, HEAD_D), F32)],
        scratch_shapes=[pltpu.VMEM((N_HEADS, L, L), F32),
                        pltpu.VMEM((2 * L, LANES), BF16), pltpu.VMEM((2 * L, LANES), BF16),
                        pltpu.VMEM((LANES, 2 * L), BF16),
                        pltpu.VMEM((2, 2 * L, 4 * L), F32),
                        pltpu.VMEM((L, L), F32), pltpu.VMEM((L, L), F32),
                        pltpu.VMEM((N_HEADS, L, L), F32),
                        pltpu.VMEM((L, L), BF16),
                        pltpu.VMEM((L, 512), F32), pltpu.VMEM((L, 512), F32), pltpu.VMEM((L, 512), F32)],
        compiler_params=_params(1),
        name="mixer_prompt",
    )(z, z, x, gb, inv, sinks, rb, mlg, retg, ng, wml, wsw, wrt, wout)


SAMPLE_GROUP = 8


def _tile_bcast(x, l_idx, src):
    n_rows = x.shape[0]
    out = jnp.zeros_like(x)
    for j in range(SEQ_PAD):
        out = out + jnp.where(l_idx == j, pltpu.roll(x, (j - src) % n_rows, 0), 0.0)
    return out


def _tile_total(x, l_idx):
    for d in (1, 2, 4):
        x = x + jnp.where(l_idx >= d, pltpu.roll(x, d, 0), 0.0)
    return x


def _per_seq(fn, group):
    return jnp.concatenate([fn(g) for g in range(group)], axis=0)


def _build_sample_swa_tables(tab_prev, tab_cur, rb_ref, group):
    R = SEQ_PAD
    rows = group * SWA_HEADS * R
    row = lax.broadcasted_iota(jnp.int32, (rows, LANES), 0)
    col = lax.broadcasted_iota(jnp.int32, (rows, LANES), 1)
    l = row & (R - 1)
    h = (row >> 3) & (SWA_HEADS - 1)
    g = row >> 6
    delta = CHUNK + l - col
    prev = jnp.zeros((rows, LANES), F32)
    cur = jnp.zeros((rows, LANES), F32)
    dcur = l - (col & (R - 1))
    for hh in range(SWA_HEADS):
        val = jnp.full((rows, LANES), rb_ref[BUCKET_RUNS[-1][1] * SWA_HEADS + hh], F32)
        for hi, bk in reversed(BUCKET_RUNS[:-1]):
            val = jnp.where(delta < hi, rb_ref[bk * SWA_HEADS + hh], val)
        prev = jnp.where(h == hh, val, prev)
        valc = jnp.zeros((rows, LANES), F32)
        for d in range(R):
            valc = jnp.where(dcur == d, rb_ref[d * SWA_HEADS + hh], valc)
        cur = jnp.where(h == hh, valc, cur)
    tab_prev[...] = jnp.where((delta >= 0) & (delta < CHUNK), prev, -jnp.inf)
    same_seq = ((col >> 3) == g) & (col < group * R)
    tab_cur[...] = jnp.where(same_seq & (dcur >= 0), cur, -jnp.inf)


def _sample_mixers_kernel(zmix_ref, c0_ref, n0_ref, m0_ref, s0_ref, ckt_ref, cvt_ref,
                          gb_ref, inv_ref, sinks_ref, rb_ref, *rest, group, first_layer):
    zm = _z_views(zmix_ref, 0)
    mq_ref, mk_ref, mv_ref, swq_ref, misc_ref = (zm[i] for i in (ZC_ML_Q, ZC_ML_K, ZC_ML_V, ZC_SW_Q, ZC_MISC))
    rq_ref, rk_ref, rv_ref = (zm[i] for i in (ZC_RT_Q, ZC_RT_K, ZC_RT_V))
    n_layered = 4
    if not first_layer:
        rest = rest[n_layered:]
    hml_ref, osw_ref, ort_ref, n_ref, m_ref, c_all, s_all, kc_all, vc_all, tab_prev, tab_cur = rest
    layered = []
    for ref in (c_all, s_all, kc_all, vc_all):
        if first_layer:
            ref[1:] = jnp.zeros((ref.shape[0] - 1,) + ref.shape[1:], F32)
            layered.append(ref.at[0])
        else:
            layered.append(ref)
    c_ref, s_ref, kc_ref, vc_ref = layered
    R = SEQ_PAD
    NR = group * R

    @pl.when(pl.program_id(0) == 0)
    def _init():
        _build_sample_swa_tables(tab_prev, tab_cur, rb_ref, group)

    lane = lax.broadcasted_iota(jnp.int32, (1, LANES), 1)
    l_idx = lax.broadcasted_iota(jnp.int32, (NR, 1), 0) & (R - 1)
    real = l_idx < N_NEW
    l_f = l_idx.astype(F32)

    def shift(x, d):
        return x if d == 0 else pltpu.roll(x, d, 0)

    def col(slab, h):
        return slab[:, h:h + 1]

    sub = lax.broadcasted_iota(jnp.int32, (R, 1), 0)

    def ld(ref, cols=slice(None)):
        xc = ref[:, cols]
        tiles = []
        for t in range(group // 2):
            tile = xc[R * t:R * (t + 1)]
            tiles.append(jnp.where(sub < N_NEW, tile, 0.0))
            tiles.append(jnp.where(sub < N_NEW, pltpu.roll(tile, N_NEW, 0), 0.0))
        return jnp.concatenate(tiles, axis=0)

    def compact(xp):
        return jnp.concatenate(
            [jnp.where(sub < N_NEW, xp[2 * R * t:2 * R * t + R], pltpu.roll(xp[2 * R * t + R:2 * R * (t + 1)], N_NEW, 0))
             for t in range(group // 2)], axis=0)

    zero_rows = jnp.zeros((LANES - NR, LANES), F32)
    k_new = jnp.concatenate([ld(misc_ref, slice(MISC_K, MISC_K + LANES)), zero_rows], axis=0)
    v_new = jnp.concatenate([ld(misc_ref, slice(MISC_V, MISC_V + LANES)), zero_rows], axis=0)
    k_new_t = k_new.T
    v_new_t = v_new.T

    def emit_window_buffers(g):
        back = (LANES - R * g) % LANES
        for new_t, cache_ref, out_ref in ((k_new_t, ckt_ref, kc_ref), (v_new_t, cvt_ref, vc_ref)):
            merged = jnp.where(lane < N_NEW, pltpu.roll(new_t, back, 1) if back else new_t, cache_ref[g])
            out_ref[g] = pltpu.roll(merged, LANES - N_NEW, 1)

    lf = _gate_slab(ld(misc_ref, slice(MISC_IF, MISC_IF + LANES)), gb_ref)
    bsum = lf
    for d in range(1, N_NEW):
        bsum = bsum + jnp.where(l_idx >= d, shift(lf, d), 0.0)
    b = pltpu.roll(bsum, LANES - N_HEADS, 1)
    gs = lf - b
    m0 = m0_ref[...]
    log_inter = b + m0
    logd = [jnp.where(l_idx >= d, b + shift(gs, d), -jnp.inf) for d in range(N_NEW)]
    m_t = log_inter
    for d in range(N_NEW):
        m_t = jnp.maximum(m_t, logd[d])
    inter = jnp.exp(log_inter - m_t)
    dm = [jnp.exp(logd[d] - m_t) for d in range(N_NEW)]
    emt = jnp.exp(-m_t)
    b_last = _tile_bcast(b, l_idx, N_NEW - 1)
    m_new = _tile_bcast(m_t, l_idx, N_NEW - 1)
    decay = jnp.exp(b_last + m0 - m_new)
    w = jnp.where(real, jnp.exp(b_last - b + lf - m_new), 0.0)
    m_ref[...] = m_t
    n0 = n0_ref[...]
    for h in range(N_HEADS):
        hs = slice(HEAD_D * h, HEAD_D * (h + 1))
        q = ld(mq_ref, hs)
        k = ld(mk_ref, hs) * (HEAD_D ** -0.5)
        v = ld(mv_ref, hs)
        qb = q.astype(BF16)
        kb = k.astype(BF16)
        inter_c = col(inter, h)
        num = _per_seq(lambda g: _dot_nt(qb[R * g:R * (g + 1)], c0_ref[g, h].astype(BF16)), group) * inter_c
        den = inter_c * jnp.sum(q * n0[:, hs], axis=1, keepdims=True)
        for d in range(N_NEW):
            s_d = jnp.sum(q * shift(k, d), axis=1, keepdims=True) * col(dm[d], h)
            num = num + s_d * shift(v, d)
            den = den + s_d
        den = jnp.maximum(jnp.abs(den), col(emt, h))
        hml_ref[:, hs] = compact(num / den)
        w_c = col(w, h)
        dec_c = col(decay, h)
        vw = (v * w_c).astype(BF16)
        for g in range(group):
            rs = slice(R * g, R * (g + 1))
            c_ref[g, h] = dec_c[R * g:R * g + 1] * c0_ref[g, h] + _dot_tn(vw[rs], kb[rs])
        n_ref[:, hs] = dec_c * n0[:, hs] + _tile_total(k * w_c, l_idx)
        for g in range(h * group // N_HEADS, (h + 1) * group // N_HEADS):
            emit_window_buffers(g)

    cos, sin_signed = _rotary_tables((PAST_LEN + l_idx).astype(F32), inv_ref)
    for h in range(N_HEADS):
        hs = slice(HEAD_D * h, HEAD_D * (h + 1))
        lg = LOG_GAMMA[h]
        qr = _rotate(ld(rq_ref, hs), cos, sin_signed)
        kr = _rotate(ld(rk_ref, hs), cos, sin_signed) * (HEAD_D ** -0.5)
        v = ld(rv_ref, hs)
        qrb = qr.astype(BF16)
        vb = v.astype(BF16)
        o = _per_seq(lambda g: _dot(qrb[R * g:R * (g + 1)], s0_ref[g, h].astype(BF16)), group)
        o = o * jnp.exp(lg * (l_f + 1.0))
        for d in range(N_NEW):
            s_d = jnp.sum(qr * shift(kr, d), axis=1, keepdims=True) * math.exp(lg * d)
            o = o + jnp.where(l_idx >= d, s_d, 0.0) * shift(v, d)
        ort_ref[:, hs] = compact(o)
        kd =(kr * jnp.where(real, jnp.exp(lg * (N_NEW - 1.0 - l_f)), 0.0)).astype(BF16)
        for g in range(group):
            rs = slice(R * g, R * (g + 1))
            s_ref[g, h] = math.exp(lg * N_NEW) * s0_ref[g, h] + _dot_tn(kd[rs], vb[rs])

    upper = lane >= SWA_D
    q_all = ld(swq_ref) * (SWA_D ** -0.5)
    q_heads = []
    for h in range(SWA_HEADS):
        blk = q_all[:, LANES * (h // 2):LANES * (h // 2 + 1)]
        qh = jnp.where(upper if h % 2 == 1 else jnp.logical_not(upper), blk, 0.0)
        if h % 2 != h // (SWA_HEADS // 2):
            qh = pltpu.roll(qh, SWA_D, 1)
        q_heads.append(qh)
    qs = jnp.concatenate([q_heads[h][R * g:R * (g + 1)] for g in range(group) for h in range(SWA_HEADS)],
                         axis=0).astype(BF16)
    hr = SWA_HEADS * R
    s_prev = jnp.concatenate([_dot(qs[hr * g:hr * (g + 1)], ckt_ref[g].astype(BF16)) for g in range(group)],
                             axis=0) + tab_prev[...]
    s_cur = _dot_nt(qs, k_new.astype(BF16)) + tab_cur[...]
    sink64 = jnp.concatenate([jnp.full((R, 1), sinks_ref[h], F32) for h in range(SWA_HEADS)], axis=0)
    sink = jnp.concatenate([sink64] * group, axis=0)
    m = jnp.maximum(jnp.maximum(jnp.max(s_prev, axis=1, keepdims=True), jnp.max(s_cur, axis=1, keepdims=True)),
                    sink)
    p_prev = jnp.exp(s_prev - m)
    p_cur = jnp.exp(s_cur - m)
    norm = 1.0 / (jnp.sum(p_prev, axis=1, keepdims=True) + jnp.sum(p_cur, axis=1, keepdims=True)
                  + jnp.exp(sink - m))
    pb = p_prev.astype(BF16)
    o = jnp.concatenate([_dot_nt(pb[hr * g:hr * (g + 1)], cvt_ref[g].astype(BF16)) for g in range(group)],
                        axis=0)
    o = (o + _dot(p_cur.astype(BF16), v_new.astype(BF16))) * norm
    seq_tiles = []
    for g in range(group):
        blocks = []
        for j in range(SWA_HEADS // 2):
            pair = []
            for h in (2 * j, 2 * j + 1):
                oh = o[hr * g + R * h:hr * g + R * (h + 1)]
                if h % 2 != h // (SWA_HEADS // 2):
                    oh = pltpu.roll(oh, SWA_D, 1)
                pair.append(oh)
            blocks.append(jnp.where(upper, pair[1], pair[0]))
        seq_tiles.append(jnp.concatenate(blocks, axis=1))
    osw_ref[...] = compact(jnp.concatenate(seq_tiles, axis=0))


def mixer_sample(z, layer, c0_all, n0_rows, m0_rows, s0_all, ckt_all, cvt_all, gb, inv, sinks, rb, prev):
    depth, nb = c0_all.shape[:2]
    group = SAMPLE_GROUP
    R = SEQ_PAD
    NR = group * R
    first_layer = prev is None
    assert first_layer == (layer == 0)

    def const(shape):
        return pl.BlockSpec(shape, lambda i: (0,) * len(shape))

    def rows(w):
        return pl.BlockSpec((NR, w), lambda i: (i, 0))

    def tokens(w):
        return pl.BlockSpec((group * N_NEW, w), lambda i: (i, 0))

    def layered_in(shape):
        return pl.BlockSpec((None, group) + shape, lambda i: (layer, i) + (0,) * len(shape))

    def layered_out(shape):
        if first_layer:
            return pl.BlockSpec((depth, group) + shape, lambda i: (0, i) + (0,) * len(shape))
        return layered_in(shape)

    smem = pl.BlockSpec(memory_space=pltpu.SMEM)
    st = (N_HEADS, HEAD_D, HEAD_D)
    buf = (LANES, CHUNK)
    n_in = 11
    if first_layer:
        extra_specs, extra_args, aliases = [], [], {}
    else:
        extra_specs = [pl.BlockSpec(memory_space=pl.ANY)] * 4
        extra_args = list(prev)
        aliases = {n_in + i: 5 + i for i in range(4)}
    return pl.pallas_call(
        functools.partial(_sample_mixers_kernel, group=group, first_layer=first_layer),
        grid=(nb // group,),
        in_specs=[tokens(Z_HALF),
                  layered_in(st), rows(512), rows(LANES), layered_in(st),
                  layered_in(buf), layered_in(buf),
                  const((1, LANES)), const((1, LANES)), smem, smem] + extra_specs,
        out_specs=[tokens(512), tokens(512), tokens(512), rows(512), rows(LANES),
                   layered_out(st), layered_out(st), layered_out(buf), layered_out(buf)],
        out_shape=[jax.ShapeDtypeStruct((nb * N_NEW, 512), F32)] * 3
        + [jax.ShapeDtypeStruct((nb * R, 512), F32), jax.ShapeDtypeStruct((nb * R, LANES), F32),
           jax.ShapeDtypeStruct((depth, nb) + st, F32), jax.ShapeDtypeStruct((depth, nb) + st, F32),
           jax.ShapeDtypeStruct((depth, nb) + buf, F32), jax.ShapeDtypeStruct((depth, nb) + buf, F32)],
        scratch_shapes=[pltpu.VMEM((group * SWA_HEADS * R, LANES), F32),
                        pltpu.VMEM((group * SWA_HEADS * R, LANES), F32)],
        input_output_aliases=aliases,
        compiler_params=_params(1),
        name="mixer_sample",
    )(z, c0_all, n0_rows, m0_rows, s0_all, ckt_all, cvt_all, gb, inv, sinks, rb, *extra_args)


def _mixer_post_kernel(hml_ref, osw_ref, ort_ref, mlo_ref, rtg_ref, g0_ref, g1_ref, g2_ref, x_ref,
                       mlg_ref, retg_ref, ng_ref, wml_ref, wsw_ref, wrt_ref, wout_ref, o_ref):
    o_ref[...] = _post_math(hml_ref[...], osw_ref[...], ort_ref[...], mlo_ref[...], rtg_ref[...],
                            g0_ref[...], g1_ref[...], g2_ref[...], x_ref[...],
                            mlg_ref[...], retg_ref[...], ng_ref[...],
                            wml_ref[...], wsw_ref[...], wrt_ref[...], wout_ref[...])


def mixer_post(hml, osw, ort, z, x, mlg, retg, ng, layer, wml, wsw, wrt, wout, tm, name):
    m = x.shape[0]

    def tok(w, cb=0):
        return pl.BlockSpec((tm, w), lambda i, cb=cb: (i, cb))

    def const(shape):
        return pl.BlockSpec(shape, lambda i: (0,) * len(shape))

    return pl.pallas_call(
        _mixer_post_kernel,
        grid=(m // tm,),
        in_specs=[tok(512), tok(512), tok(512), tok(512, ZC_ML_O), tok(512, ZC_RT_G),
                  tok(1024, ZC_GATES), tok(1024, ZC_GATES + 1), tok(1024, ZC_GATES + 2), tok(D_MODEL),
                  const((1, 512)), const((1, 512)), const((1, D_MODEL)),
                  _layer_weight(wml, layer), _layer_weight(wsw, layer), _layer_weight(wrt, layer),
                  _layer_weight(wout, layer)],
        out_specs=tok(D_MODEL),
        out_shape=jax.ShapeDtypeStruct((m, D_MODEL), F32),
        compiler_params=_params(1),
        name=name,
    )(hml, osw, ort, z, z, z, z, z, x, mlg, retg, ng, wml, wsw, wrt, wout)


def _xattn_head(q, k, v, mask):

    s = _dot_nt(jnp.where(mask, q, 0.0).astype(BF16), k) * (X_D ** -0.5)
    m = jnp.max(s, axis=1, keepdims=True)
    p = jnp.exp(s - m)
    p = p / jnp.sum(p, axis=1, keepdims=True)
    return jnp.where(mask, _dot(p.astype(BF16), v), 0.0)


def _xattn_ffn_prompt_kernel(x_ref, gxi_ref, gxo_ref, wcq_ref, kv_ref, wco_ref, gfi_ref, gfo_ref, wgu_ref, wd_ref,
                             o_ref, mid_scr):
    @pl.when(pl.program_id(0) == 0)
    def _():
        mid_scr[...] = jnp.zeros(mid_scr.shape, F32)

    lane = lax.broadcasted_iota(jnp.int32, (1, X_HEADS * X_D), 1)
    masks = [(lane >= X_D * h) & (lane < X_D * (h + 1)) for h in range(X_HEADS)]
    fc = D_FF // FFN_SPLIT
    heads_per_piece = X_HEADS // FFN_SPLIT
    k = kv_ref[:, 0:X_HEADS * X_D].astype(BF16)
    v = kv_ref[:, X_HEADS * X_D:2 * X_HEADS * X_D].astype(BF16)
    tm = x_ref.shape[0]
    halves = [slice(0, tm // 2), slice(tm // 2, tm)]

    def start(rows):
        x_mid, x = mid_scr[rows, :], x_ref[rows, :]
        u = _rms(x_mid, gfi_ref[...]).astype(BF16)
        q = _dot(_rms(x, gxi_ref[...]).astype(BF16), wcq_ref[...])
        return x_mid, x, u, q

    def finish(rows, x_mid, x, acc, att):
        o_ref[rows, :] = x_mid + _rms(acc, gfo_ref[...])
        mid_scr[rows, :] = x + _rms(_dot(att.astype(BF16), wco_ref[...]), gxo_ref[...])

    started = [start(halves[0]), None]
    pending = None
    for i, rows in enumerate(halves):
        x_mid, x, u, q = started[i]
        att = jnp.zeros(q.shape, F32)
        acc = None
        for c in range(FFN_SPLIT):
            g = _dot(u, wgu_ref[:, fc * c:fc * (c + 1)])
            up = _dot(u, wgu_ref[:, D_FF + fc * c:D_FF + fc * (c + 1)])
            if c == 0:
                if i + 1 < len(halves):
                    started[i + 1] = start(halves[i + 1])
                if pending is not None:
                    finish(*pending)
            for h in range(heads_per_piece * c, heads_per_piece * (c + 1)):
                att = att + _xattn_head(q, k, v, masks[h])
            hid = (g * _sigmoid(g) * up).astype(BF16)
            part = _dot(hid, wd_ref[fc * c:fc * (c + 1), :])
            acc = part if acc is None else acc + part
        pending = (rows, x_mid, x, acc, att)
    finish(*pending)


def xattn_ffn_prompt(x, gxi, gxo, gfi, gfo, layer, wcq, kv, wco, wgu, wd, tm):
    m = x.shape[0]
    n = m // tm

    def const(shape):
        return pl.BlockSpec(shape, lambda s: (0,) * len(shape))

    gain = const((1, D_MODEL))
    return pl.pallas_call(
        _xattn_ffn_prompt_kernel,
        grid=(n + 1,),
        in_specs=[pl.BlockSpec((tm, D_MODEL), lambda s: (jnp.minimum(s, n - 1), 0)),
                  gain, gain, _layer_weight(wcq, layer), const(kv.shape), _layer_weight(wco, layer),
                  gain, gain, _layer_weight(wgu, layer), _layer_weight(wd, layer)],
        out_specs=pl.BlockSpec((tm, D_MODEL), lambda s: (jnp.maximum(s - 1, 0), 0)),
        out_shape=jax.ShapeDtypeStruct((m, D_MODEL), F32),
        scratch_shapes=[pltpu.VMEM((tm, D_MODEL), F32)],
        compiler_params=_params(1),
        name="xattn_ffn_prompt",
    )(x, gxi, gxo, wcq, kv, wco, gfi, gfo, wgu, wd)


def _xattn_sample_kernel(x_ref, gin_ref, gout_ref, wcq_ref, kt_ref, vt_ref, wco_ref, o_ref, q_scr, a_scr,
                         *, group):
    R = SEQ_PAD
    x = x_ref[...]
    q_scr[...] = _dot(_rms(x, gin_ref[...]).astype(BF16), wcq_ref[...])
    lane = lax.broadcasted_iota(jnp.int32, (1, X_HEADS * X_D), 1)
    masks = [(lane >= X_D * h) & (lane < X_D * (h + 1)) for h in range(X_HEADS)]
    second = (lax.broadcasted_iota(jnp.int32, (X_HEADS * R, 1), 0) & (R - 1)) >= N_NEW

    def body(pair, carry):
        rows = pl.ds(pl.multiple_of(pair * R, R), R)
        q = q_scr[rows, :]
        qs = jnp.concatenate([jnp.where(mk, q, 0.0) for mk in masks], axis=0).astype(BF16)
        s = jnp.where(second, _dot(qs, kt_ref[2 * pair + 1].astype(BF16)), _dot(qs, kt_ref[2 * pair].astype(BF16)))
        s = s * (X_D ** -0.5)
        m = jnp.max(s, axis=1, keepdims=True)
        p = jnp.exp(s - m)
        p = (p / jnp.sum(p, axis=1, keepdims=True)).astype(BF16)
        o = jnp.where(second, _dot_nt(p, vt_ref[2 * pair + 1].astype(BF16)), _dot_nt(p, vt_ref[2 * pair].astype(BF16)))
        acc = jnp.zeros((R, X_HEADS * X_D), F32)
        for h in range(X_HEADS):
            acc = acc + jnp.where(masks[h], o[R * h:R * (h + 1)], 0.0)
        a_scr[rows, :] = acc
        return carry

    lax.fori_loop(0, group // 2, body, 0, unroll=2)
    o_ref[...] = x + _rms(_dot(a_scr[...].astype(BF16), wco_ref[...]), gout_ref[...])


def xattn_sample(x, gin, gout, wcq, layer, mkt, mvt, wco, group=16):
    nb = mkt.shape[1]
    R = N_NEW

    def const(shape):
        return pl.BlockSpec(shape, lambda i: (0,) * len(shape))

    tok = pl.BlockSpec((group * R, D_MODEL), lambda i: (i, 0))
    mem = pl.BlockSpec((None, group, X_HEADS * X_D, N_MEM), lambda i: (layer, i, 0, 0))
    return pl.pallas_call(
        functools.partial(_xattn_sample_kernel, group=group),
        grid=(nb // group,),
        in_specs=[tok, const((1, D_MODEL)), const((1, D_MODEL)), _layer_weight(wcq, layer), mem, mem,
                  _layer_weight(wco, layer)],
        out_specs=tok,
        out_shape=jax.ShapeDtypeStruct((nb * R, D_MODEL), F32),
        scratch_shapes=[pltpu.VMEM((group * R, X_HEADS * X_D), F32),
                        pltpu.VMEM((group * R, X_HEADS * X_D), F32)],
        compiler_params=_params(1),
        name="xattn_sample",
    )(x, gin, gout, wcq, mkt, mvt, wco)


FFN_SPLIT = 2


def _ffn_kernel(x_ref, gin_ref, gout_ref, wgu_ref, wd_ref, o_ref):
    x = x_ref[...]
    u = _rms(x, gin_ref[...]).astype(BF16)
    fc = D_FF // FFN_SPLIT
    acc = None
    for c in range(FFN_SPLIT):
        g = _dot(u, wgu_ref[:, fc * c:fc * (c + 1)])
        up = _dot(u, wgu_ref[:, D_FF + fc * c:D_FF + fc * (c + 1)])
        h = (g * _sigmoid(g) * up).astype(BF16)
        part = _dot(h, wd_ref[fc * c:fc * (c + 1), :])
        acc = part if acc is None else acc + part
    o_ref[...] = x + _rms(acc, gout_ref[...])


def ffn(x, gin, gout, layer, wgu, wd, tm, name):
    m = x.shape[0]

    def const(shape):
        return pl.BlockSpec(shape, lambda i: (0,) * len(shape))

    tok = pl.BlockSpec((tm, D_MODEL), lambda i: (i, 0))
    return pl.pallas_call(
        _ffn_kernel,
        grid=(m // tm,),
        in_specs=[tok, const((1, D_MODEL)), const((1, D_MODEL)),
                  _layer_weight(wgu, layer), _layer_weight(wd, layer)],
        out_specs=tok,
        out_shape=jax.ShapeDtypeStruct((m, D_MODEL), F32),
        compiler_params=_params(1),
        name=name,
    )(x, gin, gout, wgu, wd)


def _reorder_w_in_t(w):
    wt = jnp.swapaxes(w, 1, 2)
    sizes = (512, 512, 512, 4, 4, 512, 512, 128, 128, 512, 512, 512, 512, 3072)
    offs = np.concatenate([[0], np.cumsum(sizes)])
    (ml_q, ml_k, ml_v, ml_i, ml_f, ml_o, sw_q, sw_k, sw_v, rt_q, rt_k, rt_v, rt_g, gates) = [
        wt[:, int(offs[i]):int(offs[i + 1])] for i in range(len(sizes))]
    pad = jnp.zeros((w.shape[0], 512 - 128 - 128 - 8, w.shape[1]), w.dtype)
    out = jnp.concatenate([ml_q, ml_k, ml_v, sw_q, sw_k, sw_v, ml_i, ml_f, pad, rt_q, rt_k, rt_v,
                           ml_o, rt_g, gates], axis=1)
    assert out.shape[1] == Z_COLS
    return out.astype(BF16)


def _row(v):
    return v.reshape(1, -1).astype(F32)


def _decoder_layer(x, z_fn, mixer_fn, xattn_ffn_fn):
    z = z_fn(x)
    x, states = mixer_fn(z, x)
    return xattn_ffn_fn(x), z, states


def kernel(x_prompt, x_sample, mem_prompt, state_mlstm_C, state_mlstm_n, state_mlstm_m, state_ret_S,
           cache_swa_k, cache_swa_v, cache_mem_k, cache_mem_v, norm_g, w_in, ml_gate_bias, ml_head_g,
           ret_head_g, swa_sinks, rel_bias, w_br_ml, w_br_swa, w_br_ret, w_out, w_cq, w_mkv, w_co,
           w_gu, w_down):
    depth = w_in.shape[0]
    bp, t, d = x_prompt.shape
    assert bp == 1 and d == D_MODEL and t % 1024 == 0
    nb, n_new, _ = x_sample.shape
    assert n_new == N_NEW
    R = SEQ_PAD

    xp = x_prompt.reshape(t, d)
    xs = x_sample.reshape(nb * n_new, d)
    mem = mem_prompt.reshape(N_MEM, d)
    half = HEAD_D // 2
    inv = ROPE_BASE ** (-jnp.arange(half, dtype=F32) / half)
    inv = jnp.concatenate([inv, inv]).reshape(1, LANES)
    rb = rel_bias.astype(F32).reshape(-1)
    mem_kt = jnp.transpose(cache_mem_k, (0, 1, 3, 4, 2)).reshape(depth, nb, X_HEADS * X_D, N_MEM)
    mem_vt = jnp.transpose(cache_mem_v, (0, 1, 3, 4, 2)).reshape(depth, nb, X_HEADS * X_D, N_MEM)

    win_t = _reorder_w_in_t(w_in)
    wml, wsw, wrt, wout = (w.astype(BF16) for w in (w_br_ml, w_br_swa, w_br_ret, w_out))
    wcq, wmkv, wco, wgu, wd = (w.astype(BF16) for w in (w_cq, w_mkv, w_co, w_gu, w_down))

    outs_p = {k: [] for k in ("C", "n", "m", "S", "k", "v", "mk", "mv")}
    outs_s = {k: [] for k in ("n", "m")}
    sample_layered = None
    swa_kt = jnp.transpose(cache_swa_k, (0, 1, 3, 4, 2)).reshape(depth, nb, LANES, CHUNK)
    swa_vt = jnp.transpose(cache_swa_v, (0, 1, 3, 4, 2)).reshape(depth, nb, LANES, CHUNK)
    for l in range(depth):
        ng = [_row(norm_g[l, i]) for i in range(7)]
        mlg, retg = _row(ml_head_g[l]), _row(ret_head_g[l])
        gb = jnp.concatenate([ml_gate_bias[l, 0], ml_gate_bias[l, 1],
                              jnp.zeros((LANES - 2 * N_HEADS,), F32)]).reshape(1, LANES).astype(F32)
        gb_rows = jnp.broadcast_to(ml_gate_bias[l].astype(F32).reshape(2 * N_HEADS, 1), (2 * N_HEADS, LANES))
        sinks = swa_sinks[l].astype(F32)

        kv = norm_matmul(mem, ng[6], wmkv, l, tm=N_MEM, tn=2 * X_HEADS * X_D, name="memory_kv")

        def seq_p(z, x):
            x1, c_, n_, m_, s_ = mixer_prompt(z, x, gb_rows, inv, sinks, rb, mlg, retg, ng[1], l,
                                              wml, wsw, wrt, wout)
            return x1, (c_, n_, m_, s_)

        xp, zp, (c_, n_, m_, s_) = _decoder_layer(
            xp,
            lambda x: norm_matmul(x, ng[0], win_t, l, tm=min(IN_PROJ_TM, t), tn=1024, name="in_proj_prompt",
                                  w_transposed=True),
            seq_p,
            lambda x: xattn_ffn_prompt(x, ng[2], ng[3], ng[4], ng[5], l, wcq, kv, wco, wgu, wd,
                                       tm=min(XATTN_FFN_TM, t)))
        outs_p["C"].append(c_.reshape(1, N_HEADS, HEAD_D, HEAD_D))
        outs_p["n"].append(n_.reshape(1, N_HEADS, HEAD_D))
        outs_p["m"].append(m_[:, :N_HEADS])
        outs_p["S"].append(s_.reshape(1, N_HEADS, HEAD_D, HEAD_D))
        misc_last = zp[t - CHUNK:, 512 * ZC_MISC:512 * ZC_MISC + 2 * LANES]
        outs_p["k"].append(misc_last[:, :LANES].reshape(1, CHUNK, 2, SWA_D))
        outs_p["v"].append(misc_last[:, LANES:].reshape(1, CHUNK, 2, SWA_D))
        outs_p["mk"].append(kv[:, :X_HEADS * X_D].reshape(1, N_MEM, X_HEADS, X_D))
        outs_p["mv"].append(kv[:, X_HEADS * X_D:].reshape(1, N_MEM, X_HEADS, X_D))

        n0_rows = jnp.repeat(state_mlstm_n[l].astype(F32).reshape(nb, N_HEADS * HEAD_D), R, axis=0)
        m0_rows = jnp.repeat(jnp.pad(state_mlstm_m[l].astype(F32), ((0, 0), (0, LANES - N_HEADS))), R, axis=0)

        def seq_s(z, x):
            hml, osw, ort, n_, m_, *layered = mixer_sample(
                z, l, state_mlstm_C, n0_rows, m0_rows, state_ret_S, swa_kt, swa_vt, gb, inv, sinks, rb,
                sample_layered)
            x1 = mixer_post(hml, osw, ort, z, x, mlg, retg, ng[1], l, wml, wsw, wrt, wout,
                            tm=min(256, x.shape[0]), name="mixer_post_sample")
            return x1, (n_, m_, layered)

        xs, zs, (n_, m_, sample_layered) = _decoder_layer(
            xs,
            lambda x: norm_matmul(x, ng[0], win_t, l, tm=min(1024, nb * n_new), tn=1024, name="in_proj_sample",
                                  w_transposed=True),
            seq_s,
            lambda x: ffn(xattn_sample(x, ng[2], ng[3], wcq, l, mem_kt, mem_vt, wco),
                          ng[4], ng[5], l, wgu, wd, tm=min(512, nb * n_new), name="ffn_sample"))
        outs_s["n"].append(n_.reshape(nb, R, N_HEADS, HEAD_D)[:, R - 1])
        outs_s["m"].append(m_.reshape(nb, R, LANES)[:, n_new - 1, :N_HEADS])

    sample_c, sample_s, swa_kt_new, swa_vt_new = sample_layered
    s_swa_k = jnp.transpose(swa_kt_new.reshape(depth, nb, 2, SWA_D, CHUNK), (0, 1, 4, 2, 3))
    s_swa_v = jnp.transpose(swa_vt_new.reshape(depth, nb, 2, SWA_D, CHUNK), (0, 1, 4, 2, 3))
    y_p = xp.reshape(1, t, d)
    y_s = xs.reshape(nb, n_new, d)
    st = lambda d_, k: jnp.stack(d_[k])
    return (y_p, y_s,
            st(outs_p, "C"), st(outs_p, "n"), st(outs_p, "m"), st(outs_p, "S"),
            st(outs_p, "k"), st(outs_p, "v"), st(outs_p, "mk"), st(outs_p, "mv"),
            sample_c, st(outs_s, "n"), st(outs_s, "m"), sample_s, s_swa_k, s_swa_v)
```

```python
import functools
import math

import numpy as np
import jax
import jax.numpy as jnp
from jax import lax
from jax.experimental import pallas as pl
from jax.experimental.pallas import tpu as pltpu

F32 = jnp.float32
BF16 = jnp.bfloat16

D_MODEL = 1024
EPS = 1e-6
PAST_LEN = 16384
HEAD_D = 128
N_HEADS = 4
CHUNK = 128
SWA_HEADS = 8
SWA_D = 64
N_BUCKETS = 32
MAX_DISTANCE = 128
ROPE_BASE = 10000.0
N_MEM = 256
X_HEADS = 4
X_D = 64
D_FF = 2816
SEQ_PAD = 8
N_NEW = 4
LANES = 128
IN_PROJ_TM = 2048
MXU_MIN_ROWS = 16
VMEM_LIMIT = 48 * 1024 * 1024

ZC_ML_Q, ZC_ML_K, ZC_ML_V, ZC_SW_Q, ZC_MISC, ZC_RT_Q, ZC_RT_K, ZC_RT_V, ZC_ML_O, ZC_RT_G = range(10)
ZC_GATES = 5
Z_COLS = 8192
Z_HALF = Z_COLS // 2
MISC_K, MISC_V, MISC_IF = 0, 128, 256

LOG2E = math.log2(math.e)
LN2 = math.log(2.0)
LOG_GAMMA = tuple(float(v) for v in np.log1p(-np.exp2(-5.0 - np.arange(N_HEADS, dtype=np.float32))))

SWA_SAME = (0, 2, 5, 7)
SWA_SWAP = (1, 3, 4, 6)


def _t5_bucket_bounds():
    n = np.arange(CHUNK)
    max_exact = N_BUCKETS // 2
    nf = np.maximum(n, 1).astype(np.float32)
    large = max_exact + (np.log(nf / np.float32(max_exact)) / np.float32(math.log(MAX_DISTANCE / max_exact))
                         * np.float32(N_BUCKETS - max_exact)).astype(np.int32)
    large = np.minimum(large, N_BUCKETS - 1)
    b = np.where(n < max_exact, n, large)
    assert np.all(np.diff(b) >= 0)
    runs = []
    for d in range(CHUNK):
        if runs and runs[-1][1] == int(b[d]):
            runs[-1][0] = d + 1
        else:
            runs.append([d + 1, int(b[d])])
    return tuple((hi, bk) for hi, bk in runs)


BUCKET_RUNS = _t5_bucket_bounds()


def _dot(a, b):
    return jnp.dot(a, b, preferred_element_type=F32)


def _dot_nt(a, b):
    return lax.dot_general(a, b, (((1,), (1,)), ((), ())), preferred_element_type=F32)


def _dot_tn(a, b):
    return lax.dot_general(a, b, (((0,), (0,)), ((), ())), preferred_element_type=F32)


def _rms(x, g):
    return x * lax.rsqrt(jnp.mean(x * x, axis=-1, keepdims=True) + EPS) * g


def _sigmoid(x):
    return 1.0 / (1.0 + jnp.exp2(x * (-LOG2E)))


def _log_sigmoid(x):
    return jnp.minimum(x, 0.0) - jnp.log1p(jnp.exp(-jnp.abs(x)))


def _layer_weight(w_all, layer):
    shape = w_all.shape[1:]
    return pl.BlockSpec((None,) + shape, lambda *_: (layer,) + (0,) * len(shape), pipeline_mode=pl.Buffered(1))


def _params(n_grid):
    return pltpu.CompilerParams(dimension_semantics=("arbitrary",) * n_grid, vmem_limit_bytes=VMEM_LIMIT)


def _norm_matmul_kernel(x_ref, g_ref, w_ref, o_ref, u_ref, *, w_transposed):
    @pl.when(pl.program_id(1) == 0)
    def _():
        u_ref[...] = _rms(x_ref[...], g_ref[...]).astype(BF16)

    o_ref[...] = _dot_nt(u_ref[...], w_ref[...]) if w_transposed else _dot(u_ref[...], w_ref[...])


def norm_matmul(x, g, w_all, layer, tm, tn, name, w_transposed=False):
    m, k = x.shape
    n = w_all.shape[1] if w_transposed else w_all.shape[2]
    if w_transposed:
        w_spec = pl.BlockSpec((None, tn, k), lambda i, j: (layer, j, 0))
    else:
        w_spec = pl.BlockSpec((None, k, tn), lambda i, j: (layer, 0, j))
    return pl.pallas_call(
        functools.partial(_norm_matmul_kernel, w_transposed=w_transposed),
        grid=(m // tm, n // tn),
        in_specs=[pl.BlockSpec((tm, k), lambda i, j: (i, 0)),
                  pl.BlockSpec((1, k), lambda i, j: (0, 0)),
                  w_spec],
        out_specs=pl.BlockSpec((tm, tn), lambda i, j: (i, j)),
        out_shape=jax.ShapeDtypeStruct((m, n), F32),
        scratch_shapes=[pltpu.VMEM((tm, k), BF16)],
        compiler_params=_params(2),
        name=name,
    )(x, g, w_all)


def _transpose_bf16(x, eye):
    return _dot_nt(eye, x).astype(BF16)


def _build_swa_table_t(tab_ref, rb_ref):
    L = CHUNK
    srow = lax.broadcasted_iota(jnp.int32, (2 * L, L), 0)
    lcol = lax.broadcasted_iota(jnp.int32, (2 * L, L), 1)
    delta = lcol + L - srow
    valid = (delta >= 0) & (delta < L)
    for var, heads in enumerate((SWA_SAME, SWA_SWAP)):
        for i, h in enumerate(heads):
            val = jnp.full((2 * L, L), rb_ref[BUCKET_RUNS[-1][1] * SWA_HEADS + h], F32)
            for hi, bk in reversed(BUCKET_RUNS[:-1]):
                val = jnp.where(delta < hi, rb_ref[bk * SWA_HEADS + h], val)
            tab_ref[var, :, i * L:(i + 1) * L] = jnp.where(valid, val, -jnp.inf)
            tab_ref[2 + var, :, i * L:(i + 1) * L] = jnp.where(valid & (srow >= L), val, -jnp.inf)


def _swa_scores_t(q, k_same, k_swap, tab_ref, first):
    lane = lax.broadcasted_iota(jnp.int32, (1, LANES), 1)
    upper = lane >= SWA_D
    q = q * (SWA_D ** -0.5)
    ks = (k_same[...], k_swap[...])
    scores = []
    for var, heads in enumerate((SWA_SAME, SWA_SWAP)):
        qm = []
        for h in heads:
            blk = q[:, LANES * (h // 2):LANES * (h // 2 + 1)]
            qm.append(jnp.where(upper if h % 2 == 1 else jnp.logical_not(upper), blk, 0.0))
        qs = jnp.concatenate(qm, axis=0).astype(BF16)
        scores.append(_dot_nt(ks[var], qs) + tab_ref[var + 2 * first])
    return scores


def _swa_softmax_t(scores, sinks_ref):
    probs = []
    for s_t, heads in zip(scores, (SWA_SAME, SWA_SWAP)):
        sink = jnp.concatenate([jnp.full((1, CHUNK), sinks_ref[h], F32) for h in heads], axis=1)
        m = jnp.maximum(jnp.max(s_t, axis=0, keepdims=True), sink)
        p = jnp.exp(s_t - m)
        norm = 1.0 / (jnp.sum(p, axis=0, keepdims=True) + jnp.exp(sink - m))
        probs.append((p.astype(BF16), norm))
    return probs


def _swa_values_t(probs, vt_buf):
    vt = vt_buf[...]
    vts = (vt, jnp.concatenate([vt[SWA_D:], vt[:SWA_D]], axis=0))
    return [_dot(vts[var], p) * norm for var, (p, norm) in enumerate(probs)]


def _swa_output_t(values):
    L = CHUNK
    lane = lax.broadcasted_iota(jnp.int32, (1, LANES), 1)
    upper = lane >= SWA_D
    outs = [None] * SWA_HEADS
    for o_t, heads in zip(values, (SWA_SAME, SWA_SWAP)):
        for i, h in enumerate(heads):
            outs[h] = o_t[:, i * L:(i + 1) * L].T
    blocks = [jnp.where(upper, outs[2 * j + 1], outs[2 * j]) for j in range(SWA_HEADS // 2)]
    return jnp.concatenate(blocks, axis=1)


def _rotary_tables(pos, inv_ref):
    lane = lax.broadcasted_iota(jnp.int32, (1, LANES), 1)
    ang = pos * inv_ref[...]
    sin = jnp.sin(ang)
    return jnp.cos(ang), jnp.where(lane < HEAD_D // 2, -sin, sin)


def _rotate(x, cos, sin_signed):
    return x * cos + pltpu.roll(x, HEAD_D // 2, 1) * sin_signed


def _gate_slab(raw, gb_ref):
    lane = lax.broadcasted_iota(jnp.int32, (1, LANES), 1)
    x = raw + gb_ref[...]
    return jnp.where((lane >= N_HEADS) & (lane < 2 * N_HEADS), _log_sigmoid(x), x)


def _head_rms(h, gain):
    parts = []
    for i in range(N_HEADS):
        blk = h[:, HEAD_D * i:HEAD_D * (i + 1)]
        parts.append(blk * lax.rsqrt(jnp.mean(blk * blk, axis=-1, keepdims=True) + EPS))
    return jnp.concatenate(parts, axis=-1) * gain


def _rms_over_rows(h_t):
    return h_t * lax.rsqrt(jnp.mean(h_t * h_t, axis=0, keepdims=True) + EPS)


def _post_math(hml, osw, ort, mlo, rtg, g0, g1, g2, x, mlg, retg, ng, wml, wsw, wrt, wout, head_normed=False):
    if not head_normed:
        hml = _head_rms(hml, mlg)
        ort = _head_rms(ort, retg)
    hm = hml * _sigmoid(mlo)
    y_ml = _dot(hm.astype(BF16), wml)
    y_sw = _dot(osw.astype(BF16), wsw)
    rt = ort * (rtg * _sigmoid(rtg))
    y_rt = _dot(rt.astype(BF16), wrt)
    merged = _sigmoid(g0) * y_ml + _sigmoid(g1) * y_sw + _sigmoid(g2) * y_rt
    return x + _rms(_dot(merged.astype(BF16), wout), ng)


def _z_views(z_half_ref, first_col_block):
    return {first_col_block + i: z_half_ref.at[:, 512 * i:512 * (i + 1)] for i in range(Z_HALF // 512)}


def _mixer_seq_prompt_kernel(zmix_ref, ztail_ref, x_ref,
                             gbr_ref, inv_ref, sinks_ref, rb_ref,
                             mlg_ref, retg_ref, ng_ref, wml_ref, wsw_ref, wrt_ref, wout_ref,
                             x1_ref, c_ref, n_ref, m_ref, s_ref,
                             st_scr, k_same, k_swap, vt_buf, tab_t, cos_l, sin_l, dec_in_t, eye_ref,
                             hml_s, osw_s, ort_s):
    zm = _z_views(zmix_ref, 0)
    mq_ref, mk_ref, mv_ref, swq_ref, misc_ref = (zm[i] for i in (ZC_ML_Q, ZC_ML_K, ZC_ML_V, ZC_SW_Q, ZC_MISC))
    rq_ref, rk_ref, rv_ref = (zm[i] for i in (ZC_RT_Q, ZC_RT_K, ZC_RT_V))
    zt = _z_views(ztail_ref, Z_HALF // 512)
    mlo_ref, rtg_ref = zt[ZC_ML_O], zt[ZC_RT_G]
    g0_ref, g1_ref, g2_ref = (ztail_ref.at[:, 1024 * (ZC_GATES + i) - Z_HALF:1024 * (ZC_GATES + i + 1) - Z_HALF]
                              for i in range(3))
    c = pl.program_id(0)
    active = c < pl.num_programs(0) - 1
    L = CHUNK
    lane = lax.broadcasted_iota(jnp.int32, (1, LANES), 1)
    row = lax.broadcasted_iota(jnp.int32, (L, L), 0)
    col = lax.broadcasted_iota(jnp.int32, (L, L), 1)
    causal_t = row <= col

    @pl.when(c == 0)
    def _init():
        c_ref[...] = jnp.zeros(c_ref.shape, F32)
        n_ref[...] = jnp.zeros(n_ref.shape, F32)
        m_ref[...] = jnp.zeros(m_ref.shape, F32)
        s_ref[...] = jnp.zeros(s_ref.shape, F32)
        hml_s[...] = jnp.zeros(hml_s.shape, F32)
        osw_s[...] = jnp.zeros(osw_s.shape, F32)
        ort_s[...] = jnp.zeros(ort_s.shape, F32)
        st_scr[...] = jnp.zeros(st_scr.shape, F32)
        k_same[...] = jnp.zeros(k_same.shape, BF16)
        k_swap[...] = jnp.zeros(k_swap.shape, BF16)
        vt_buf[...] = jnp.zeros(vt_buf.shape, BF16)
        _build_swa_table_t(tab_t, rb_ref)
        ang = row.astype(F32) * inv_ref[...]
        cos_l[...] = jnp.cos(ang)
        sin_l[...] = jnp.sin(ang)
        rel_t = (col - row).astype(F32)
        for h in range(N_HEADS):
            dec_in_t[h] = jnp.where(causal_t, jnp.exp(LOG_GAMMA[h] * rel_t), 0.0)
        eye_ref[...] = jnp.where(row == col, 1.0, 0.0).astype(BF16)

    eye = eye_ref[...]
    heads = range(N_HEADS)
    hsl = [slice(HEAD_D * h, HEAD_D * (h + 1)) for h in heads]
    sub8 = lax.broadcasted_iota(jnp.int32, (2 * N_HEADS, 1), 0)
    hml_prev, osw_prev, ort_prev = hml_s[...], osw_s[...], ort_s[...]

    raw = misc_ref[:, MISC_IF:MISC_IF + LANES].T[0:2 * N_HEADS] + gbr_ref[...]
    gates = jnp.where(sub8 >= N_HEADS, _log_sigmoid(raw), raw)
    cum = gates
    for sh in (1, 2, 4, 8, 16, 32, 64):
        cum = cum + jnp.where(lane >= sh, pltpu.roll(cum, sh, 1), 0.0)
    g8 = jnp.where(sub8 < N_HEADS, gates - pltpu.roll(cum, N_HEADS, 0), 0.0)
    g2_cols = jnp.concatenate([g8 * LOG2E, jnp.zeros((L - 2 * N_HEADS, L), F32)], axis=0).T

    ang0 = (c * L).astype(F32) * inv_ref[...]
    cos0 = jnp.cos(ang0)
    sin0 = jnp.sin(ang0)
    cos_t = cos_l[...]
    sin_t = sin_l[...]
    cos = cos0 * cos_t - sin0 * sin_t
    sin = sin0 * cos_t + cos0 * sin_t
    sin_signed = jnp.where(lane < HEAD_D // 2, -sin, sin)
    ml_ops, rt_ops = [], []
    for h in heads:
        qb = mq_ref[:, hsl[h]].astype(BF16)
        kb = (mk_ref[:, hsl[h]] * (HEAD_D ** -0.5)).astype(BF16)
        vt = _transpose_bf16(mv_ref[:, hsl[h]].astype(BF16), eye)
        ml_ops.append((qb, kb, vt, _dot_nt(kb, qb)))
    for h in heads:
        qr = _rotate(rq_ref[:, hsl[h]], cos, sin_signed).astype(BF16)
        kr = (_rotate(rk_ref[:, hsl[h]], cos, sin_signed) * (HEAD_D ** -0.5)).astype(BF16)
        vt = _transpose_bf16(rv_ref[:, hsl[h]].astype(BF16), eye)
        rt_ops.append((qr, kr, vt, _dot_nt(kr, qr)))
    k_new = misc_ref[:, MISC_K:MISC_K + LANES]
    k_same[0:L, :] = k_same[L:2 * L, :]
    k_swap[0:L, :] = k_swap[L:2 * L, :]
    k_same[L:2 * L, :] = k_new.astype(BF16)
    k_swap[L:2 * L, :] = pltpu.roll(k_new, SWA_D, 1).astype(BF16)
    vt_buf[:, 0:L] = vt_buf[:, L:2 * L]
    vt_buf[:, L:2 * L] = _transpose_bf16(misc_ref[:, MISC_V:MISC_V + LANES].astype(BF16), eye)
    swa_scores = _swa_scores_t(swq_ref[...], k_same, k_swap, tab_t, (c == 0).astype(jnp.int32))

    y_ml = _dot((hml_prev * _sigmoid(mlo_ref[...])).astype(BF16), wml_ref[...])

    m_all = m_ref[...]
    gate_ops = []
    for h in heads:
        m_prev = m_all[:, h:h + 1]
        gm2 = jnp.where(causal_t, g2_cols[:, h:h + 1], -jnp.inf)
        mx2 = jnp.maximum(jnp.max(gm2, axis=0, keepdims=True), m_prev * LOG2E)
        gate_ops.append((m_prev, mx2 * LN2, jnp.exp2(m_prev * LOG2E - mx2), jnp.exp2(gm2 - mx2)))

    y_sw = _dot(osw_prev.astype(BF16), wsw_ref[...])
    rtg = rtg_ref[...]
    y_rt = _dot((ort_prev * (rtg * _sigmoid(rtg))).astype(BF16), wrt_ref[...])

    l_row = lane.astype(F32)
    ml_out, rt_out = [], []
    for h in heads:
        qb, kb, vt, qk = ml_ops[h]
        m_prev, mx, inter, dm_t = gate_ops[h]
        s_t = qk * dm_t
        num_t = _dot(vt, s_t.astype(BF16)) + _dot_nt(c_ref[h].astype(BF16), qb) * inter
        nq = _dot_nt(jnp.broadcast_to(n_ref[h], (MXU_MIN_ROWS, HEAD_D)).astype(BF16), qb)[0:1, :]
        den = jnp.sum(s_t, axis=0, keepdims=True) + inter * nq
        m_t = cum[N_HEADS + h:N_HEADS + h + 1, :] + mx
        den = jnp.maximum(jnp.abs(den), jnp.exp(-m_t))
        ml_out.append((num_t * (1.0 / den), m_t))
    for h in heads:
        qr, kr, vt, qk = rt_ops[h]
        s_t = qk * dec_in_t[h]
        q_decay = jnp.exp(LOG_GAMMA[h] * (l_row + 1.0))
        rt_out.append(_dot(vt, s_t.astype(BF16)) + _dot_nt(st_scr[h].astype(BF16), qr) * q_decay)
    swa_values = _swa_values_t(_swa_softmax_t(swa_scores, sinks_ref), vt_buf)

    merged = _sigmoid(g0_ref[...]) * y_ml + _sigmoid(g1_ref[...]) * y_sw + _sigmoid(g2_ref[...]) * y_rt
    x1_ref[...] = x_ref[...] + _rms(_dot(merged.astype(BF16), wout_ref[...]), ng_ref[...])

    osw_s[...] = _swa_output_t(swa_values)
    for h in heads:
        hml_s[:, hsl[h]] = _rms_over_rows(ml_out[h][0]).T * mlg_ref[:, hsl[h]]
        ort_s[:, hsl[h]] = _rms_over_rows(rt_out[h]).T * retg_ref[:, hsl[h]]
    m_out = jnp.zeros((1, LANES), F32)
    for h in heads:
        qb, kb, vt, _ = ml_ops[h]
        m_prev = gate_ops[h][0]
        m_t = ml_out[h][1]
        m_new = m_t[:, L - 1:L]
        b_last = cum[N_HEADS + h:N_HEADS + h + 1, L - 1:L]
        decay = jnp.exp(b_last + m_prev - m_new)
        w_r = jnp.exp(g8[h:h + 1, :] + (b_last - m_new))
        c_old = c_ref[h]
        n_old = n_ref[h]
        c_new = decay * c_old + _dot((vt.astype(F32) * w_r).astype(BF16), kb)
        n_new = decay * n_old + _dot(jnp.broadcast_to(w_r, (MXU_MIN_ROWS, L)).astype(BF16), kb)[0:1, :]
        c_ref[h] = jnp.where(active, c_new, c_old)
        n_ref[h] = jnp.where(active, n_new, n_old)
        m_out = jnp.where(lane == h, m_new, m_out)
    m_ref[...] = jnp.where(active, m_out, m_all)
    for h in heads:
        qr, kr, vt, _ = rt_ops[h]
        lg = LOG_GAMMA[h]
        k_decay = jnp.exp(lg * (L - 1.0 - l_row))
        st_old = st_scr[h]
        st_new = math.exp(lg * L) * st_old + _dot((vt.astype(F32) * k_decay).astype(BF16), kr)
        st_scr[h] = jnp.where(active, st_new, st_old)

    @pl.when(c == pl.num_programs(0) - 1)
    def _emit_state():
        for h in heads:
            s_ref[h] = st_scr[h].T


def mixer_prompt(z, x, gb, inv, sinks, rb, mlg, retg, ng, layer, wml, wsw, wrt, wout):
    t = z.shape[0]
    L = CHUNK
    n = t // L

    z_mix = pl.BlockSpec((L, Z_HALF), lambda c: (jnp.minimum(c, n - 1), 0))

    def tail(w, cb=0):
        return pl.BlockSpec((L, w), lambda c, cb=cb: (jnp.maximum(c - 1, 0), cb))

    def const(shape):
        return pl.BlockSpec(shape, lambda c: (0,) * len(shape))

    def weight(w):
        return _layer_weight(w, layer)

    smem = pl.BlockSpec(memory_space=pltpu.SMEM)
    return pl.pallas_call(
        _mixer_seq_prompt_kernel,
        grid=(n + 1,),
        in_specs=[z_mix, tail(Z_HALF, 1), tail(D_MODEL),
                  const((2 * N_HEADS, LANES)), const((1, LANES)), smem, smem,
                  const((1, 512)), const((1, 512)), const((1, D_MODEL)),
                  weight(wml), weight(wsw), weight(wrt), weight(wout)],
        out_specs=[tail(D_MODEL),
                   const((N_HEADS, HEAD_D, HEAD_D)), const((N_HEADS, 1, HEAD_D)), const((1, LANES)),
                   const((N_HEADS, HEAD_D, HEAD_D))],
        out_shape=[jax.ShapeDtypeStruct((t, D_MODEL), F32),
                   jax.ShapeDtypeStruct((N_HEADS, HEAD_D, HEAD_D), F32),
                   jax.ShapeDtypeStruct((N_HEADS, 1, HEAD_D), F32),
                   jax.ShapeDtypeStruct((1, LANES), F32),
                   jax.ShapeDtypeStruct((N_HEADS, HEAD_D, HEAD_D), F32)],
        scratch_shapes=[pltpu.VMEM((N_HEADS, L, L), F32),
                        pltpu.VMEM((2 * L, LANES), BF16), pltpu.VMEM((2 * L, LANES), BF16),
                        pltpu.VMEM((LANES, 2 * L), BF16),
                        pltpu.VMEM((4, 2 * L, 4 * L), F32),
                        pltpu.VMEM((L, L), F32), pltpu.VMEM((L, L), F32),
                        pltpu.VMEM((N_HEADS, L, L), F32),
                        pltpu.VMEM((L, L), BF16),
                        pltpu.VMEM((L, 512), F32), pltpu.VMEM((L, 512), F32), pltpu.VMEM((L, 512), F32)],
        compiler_params=_params(1),
        name="mixer_prompt",
    )(z, z, x, gb, inv, sinks, rb, mlg, retg, ng, wml, wsw, wrt, wout)


SAMPLE_GROUP = 8


def _tile_bcast(x, l_idx, src):
    n_rows = x.shape[0]
    out = jnp.zeros_like(x)
    for j in range(SEQ_PAD):
        out = out + jnp.where(l_idx == j, pltpu.roll(x, (j - src) % n_rows, 0), 0.0)
    return out


def _tile_total(x, l_idx):
    for d in (1, 2, 4):
        x = x + jnp.where(l_idx >= d, pltpu.roll(x, d, 0), 0.0)
    return x


def _per_seq(fn, group):
    return jnp.concatenate([fn(g) for g in range(group)], axis=0)


def _build_sample_swa_tables(tab_prev, tab_cur, rb_ref, group):
    R = SEQ_PAD
    rows = group * SWA_HEADS * R
    row = lax.broadcasted_iota(jnp.int32, (rows, LANES), 0)
    col = lax.broadcasted_iota(jnp.int32, (rows, LANES), 1)
    l = row & (R - 1)
    h = (row >> 3) & (SWA_HEADS - 1)
    g = row >> 6
    delta = CHUNK + l - col
    prev = jnp.zeros((rows, LANES), F32)
    cur = jnp.zeros((rows, LANES), F32)
    dcur = l - (col & (R - 1))
    for hh in range(SWA_HEADS):
        val = jnp.full((rows, LANES), rb_ref[BUCKET_RUNS[-1][1] * SWA_HEADS + hh], F32)
        for hi, bk in reversed(BUCKET_RUNS[:-1]):
            val = jnp.where(delta < hi, rb_ref[bk * SWA_HEADS + hh], val)
        prev = jnp.where(h == hh, val, prev)
        valc = jnp.zeros((rows, LANES), F32)
        for d in range(R):
            valc = jnp.where(dcur == d, rb_ref[d * SWA_HEADS + hh], valc)
        cur = jnp.where(h == hh, valc, cur)
    tab_prev[...] = jnp.where((delta >= 0) & (delta < CHUNK), prev, -jnp.inf)
    same_seq = ((col >> 3) == g) & (col < group * R)
    tab_cur[...] = jnp.where(same_seq & (dcur >= 0), cur, -jnp.inf)


def _sample_mixers_kernel(zmix_ref, c0_ref, n0_ref, m0_ref, s0_ref, ckt_ref, cvt_ref,
                          gb_ref, inv_ref, sinks_ref, rb_ref, *rest, group, first_layer):
    zm = _z_views(zmix_ref, 0)
    mq_ref, mk_ref, mv_ref, swq_ref, misc_ref = (zm[i] for i in (ZC_ML_Q, ZC_ML_K, ZC_ML_V, ZC_SW_Q, ZC_MISC))
    rq_ref, rk_ref, rv_ref = (zm[i] for i in (ZC_RT_Q, ZC_RT_K, ZC_RT_V))
    n_layered = 4
    if not first_layer:
        rest = rest[n_layered:]
    hml_ref, osw_ref, ort_ref, n_ref, m_ref, c_all, s_all, kc_all, vc_all, tab_prev, tab_cur = rest
    layered = []
    for ref in (c_all, s_all, kc_all, vc_all):
        if first_layer:
            ref[1:] = jnp.zeros((ref.shape[0] - 1,) + ref.shape[1:], F32)
            layered.append(ref.at[0])
        else:
            layered.append(ref)
    c_ref, s_ref, kc_ref, vc_ref = layered
    R = SEQ_PAD
    NR = group * R

    @pl.when(pl.program_id(0) == 0)
    def _init():
        _build_sample_swa_tables(tab_prev, tab_cur, rb_ref, group)

    lane = lax.broadcasted_iota(jnp.int32, (1, LANES), 1)
    l_idx = lax.broadcasted_iota(jnp.int32, (NR, 1), 0) & (R - 1)
    real = l_idx < N_NEW
    l_f = l_idx.astype(F32)

    def shift(x, d):
        return x if d == 0 else pltpu.roll(x, d, 0)

    def col(slab, h):
        return slab[:, h:h + 1]

    sub = lax.broadcasted_iota(jnp.int32, (R, 1), 0)

    def ld(ref, cols=slice(None)):
        xc = ref[:, cols]
        tiles = []
        for t in range(group // 2):
            tile = xc[R * t:R * (t + 1)]
            tiles.append(jnp.where(sub < N_NEW, tile, 0.0))
            tiles.append(jnp.where(sub < N_NEW, pltpu.roll(tile, N_NEW, 0), 0.0))
        return jnp.concatenate(tiles, axis=0)

    def compact(xp):
        return jnp.concatenate(
            [jnp.where(sub < N_NEW, xp[2 * R * t:2 * R * t + R], pltpu.roll(xp[2 * R * t + R:2 * R * (t + 1)], N_NEW, 0))
             for t in range(group // 2)], axis=0)

    zero_rows = jnp.zeros((LANES - NR, LANES), F32)
    k_new = jnp.concatenate([ld(misc_ref, slice(MISC_K, MISC_K + LANES)), zero_rows], axis=0)
    v_new = jnp.concatenate([ld(misc_ref, slice(MISC_V, MISC_V + LANES)), zero_rows], axis=0)
    k_new_t = k_new.T
    v_new_t = v_new.T

    def emit_window_buffers(g):
        back = (LANES - R * g) % LANES
        for new_t, cache_ref, out_ref in ((k_new_t, ckt_ref, kc_ref), (v_new_t, cvt_ref, vc_ref)):
            merged = jnp.where(lane < N_NEW, pltpu.roll(new_t, back, 1) if back else new_t, cache_ref[g])
            out_ref[g] = pltpu.roll(merged, LANES - N_NEW, 1)

    lf = _gate_slab(ld(misc_ref, slice(MISC_IF, MISC_IF + LANES)), gb_ref)
    bsum = lf
    for d in range(1, N_NEW):
        bsum = bsum + jnp.where(l_idx >= d, shift(lf, d), 0.0)
    b = pltpu.roll(bsum, LANES - N_HEADS, 1)
    gs = lf - b
    m0 = m0_ref[...]
    log_inter = b + m0
    logd = [jnp.where(l_idx >= d, b + shift(gs, d), -jnp.inf) for d in range(N_NEW)]
    m_t = log_inter
    for d in range(N_NEW):
        m_t = jnp.maximum(m_t, logd[d])
    inter = jnp.exp(log_inter - m_t)
    dm = [jnp.exp(logd[d] - m_t) for d in range(N_NEW)]
    emt = jnp.exp(-m_t)
    b_last = _tile_bcast(b, l_idx, N_NEW - 1)
    m_new = _tile_bcast(m_t, l_idx, N_NEW - 1)
    decay = jnp.exp(b_last + m0 - m_new)
    w = jnp.where(real, jnp.exp(b_last - b + lf - m_new), 0.0)
    m_ref[...] = m_t
    n0 = n0_ref[...]
    for h in range(N_HEADS):
        hs = slice(HEAD_D * h, HEAD_D * (h + 1))
        q = ld(mq_ref, hs)
        k = ld(mk_ref, hs) * (HEAD_D ** -0.5)
        v = ld(mv_ref, hs)
        qb = q.astype(BF16)
        kb = k.astype(BF16)
        inter_c = col(inter, h)
        num = _per_seq(lambda g: _dot_nt(qb[R * g:R * (g + 1)], c0_ref[g, h].astype(BF16)), group) * inter_c
        den = inter_c * jnp.sum(q * n0[:, hs], axis=1, keepdims=True)
        for d in range(N_NEW):
            s_d = jnp.sum(q * shift(k, d), axis=1, keepdims=True) * col(dm[d], h)
            num = num + s_d * shift(v, d)
            den = den + s_d
        den = jnp.maximum(jnp.abs(den), col(emt, h))
        hml_ref[:, hs] = compact(num / den)
        w_c = col(w, h)
        dec_c = col(decay, h)
        vw = (v * w_c).astype(BF16)
        for g in range(group):
            rs = slice(R * g, R * (g + 1))
            c_ref[g, h] = dec_c[R * g:R * g + 1] * c0_ref[g, h] + _dot_tn(vw[rs], kb[rs])
        n_ref[:, hs] = dec_c * n0[:, hs] + _tile_total(k * w_c, l_idx)
        for g in range(h * group // N_HEADS, (h + 1) * group // N_HEADS):
            emit_window_buffers(g)

    cos, sin_signed = _rotary_tables((PAST_LEN + l_idx).astype(F32), inv_ref)
    for h in range(N_HEADS):
        hs = slice(HEAD_D * h, HEAD_D * (h + 1))
        lg = LOG_GAMMA[h]
        qr = _rotate(ld(rq_ref, hs), cos, sin_signed)
        kr = _rotate(ld(rk_ref, hs), cos, sin_signed) * (HEAD_D ** -0.5)
        v = ld(rv_ref, hs)
        qrb = qr.astype(BF16)
        vb = v.astype(BF16)
        o = _per_seq(lambda g: _dot(qrb[R * g:R * (g + 1)], s0_ref[g, h].astype(BF16)), group)
        o = o * jnp.exp(lg * (l_f + 1.0))
        for d in range(N_NEW):
            s_d = jnp.sum(qr * shift(kr, d), axis=1, keepdims=True) * math.exp(lg * d)
            o = o + jnp.where(l_idx >= d, s_d, 0.0) * shift(v, d)
        ort_ref[:, hs] = compact(o)
        kd =(kr * jnp.where(real, jnp.exp(lg * (N_NEW - 1.0 - l_f)), 0.0)).astype(BF16)
        for g in range(group):
            rs = slice(R * g, R * (g + 1))
            s_ref[g, h] = math.exp(lg * N_NEW) * s0_ref[g, h] + _dot_tn(kd[rs], vb[rs])

    upper = lane >= SWA_D
    q_all = ld(swq_ref) * (SWA_D ** -0.5)
    q_heads = []
    for h in range(SWA_HEADS):
        blk = q_all[:, LANES * (h // 2):LANES * (h // 2 + 1)]
        qh = jnp.where(upper if h % 2 == 1 else jnp.logical_not(upper), blk, 0.0)
        if h % 2 != h // (SWA_HEADS // 2):
            qh = pltpu.roll(qh, SWA_D, 1)
        q_heads.append(qh)
    qs = jnp.concatenate([q_heads[h][R * g:R * (g + 1)] for g in range(group) for h in range(SWA_HEADS)],
                         axis=0).astype(BF16)
    hr = SWA_HEADS * R
    s_prev = jnp.concatenate([_dot(qs[hr * g:hr * (g + 1)], ckt_ref[g].astype(BF16)) for g in range(group)],
                             axis=0) + tab_prev[...]
    s_cur = _dot_nt(qs, k_new.astype(BF16)) + tab_cur[...]
    sink64 = jnp.concatenate([jnp.full((R, 1), sinks_ref[h], F32) for h in range(SWA_HEADS)], axis=0)
    sink = jnp.concatenate([sink64] * group, axis=0)
    m = jnp.maximum(jnp.maximum(jnp.max(s_prev, axis=1, keepdims=True), jnp.max(s_cur, axis=1, keepdims=True)),
                    sink)
    p_prev = jnp.exp(s_prev - m)
    p_cur = jnp.exp(s_cur - m)
    norm = 1.0 / (jnp.sum(p_prev, axis=1, keepdims=True) + jnp.sum(p_cur, axis=1, keepdims=True)
                  + jnp.exp(sink - m))
    pb = p_prev.astype(BF16)
    o = jnp.concatenate([_dot_nt(pb[hr * g:hr * (g + 1)], cvt_ref[g].astype(BF16)) for g in range(group)],
                        axis=0)
    o = (o + _dot(p_cur.astype(BF16), v_new.astype(BF16))) * norm
    seq_tiles = []
    for g in range(group):
        blocks = []
        for j in range(SWA_HEADS // 2):
            pair = []
            for h in (2 * j, 2 * j + 1):
                oh = o[hr * g + R * h:hr * g + R * (h + 1)]
                if h % 2 != h // (SWA_HEADS // 2):
                    oh = pltpu.roll(oh, SWA_D, 1)
                pair.append(oh)
            blocks.append(jnp.where(upper, pair[1], pair[0]))
        seq_tiles.append(jnp.concatenate(blocks, axis=1))
    osw_ref[...] = compact(jnp.concatenate(seq_tiles, axis=0))


def mixer_sample(z, layer, c0_all, n0_rows, m0_rows, s0_all, ckt_all, cvt_all, gb, inv, sinks, rb, prev):
    depth, nb = c0_all.shape[:2]
    group = SAMPLE_GROUP
    R = SEQ_PAD
    NR = group * R
    first_layer = prev is None
    assert first_layer == (layer == 0)

    def const(shape):
        return pl.BlockSpec(shape, lambda i: (0,) * len(shape))

    def rows(w):
        return pl.BlockSpec((NR, w), lambda i: (i, 0))

    def tokens(w):
        return pl.BlockSpec((group * N_NEW, w), lambda i: (i, 0))

    def layered_in(shape):
        return pl.BlockSpec((None, group) + shape, lambda i: (layer, i) + (0,) * len(shape))

    def layered_out(shape):
        if first_layer:
            return pl.BlockSpec((depth, group) + shape, lambda i: (0, i) + (0,) * len(shape))
        return layered_in(shape)

    smem = pl.BlockSpec(memory_space=pltpu.SMEM)
    st = (N_HEADS, HEAD_D, HEAD_D)
    buf = (LANES, CHUNK)
    n_in = 11
    if first_layer:
        extra_specs, extra_args, aliases = [], [], {}
    else:
        extra_specs = [pl.BlockSpec(memory_space=pl.ANY)] * 4
        extra_args = list(prev)
        aliases = {n_in + i: 5 + i for i in range(4)}
    return pl.pallas_call(
        functools.partial(_sample_mixers_kernel, group=group, first_layer=first_layer),
        grid=(nb // group,),
        in_specs=[tokens(Z_HALF),
                  layered_in(st), rows(512), rows(LANES), layered_in(st),
                  layered_in(buf), layered_in(buf),
                  const((1, LANES)), const((1, LANES)), smem, smem] + extra_specs,
        out_specs=[tokens(512), tokens(512), tokens(512), rows(512), rows(LANES),
                   layered_out(st), layered_out(st), layered_out(buf), layered_out(buf)],
        out_shape=[jax.ShapeDtypeStruct((nb * N_NEW, 512), F32)] * 3
        + [jax.ShapeDtypeStruct((nb * R, 512), F32), jax.ShapeDtypeStruct((nb * R, LANES), F32),
           jax.ShapeDtypeStruct((depth, nb) + st, F32), jax.ShapeDtypeStruct((depth, nb) + st, F32),
           jax.ShapeDtypeStruct((depth, nb) + buf, F32), jax.ShapeDtypeStruct((depth, nb) + buf, F32)],
        scratch_shapes=[pltpu.VMEM((group * SWA_HEADS * R, LANES), F32),
                        pltpu.VMEM((group * SWA_HEADS * R, LANES), F32)],
        input_output_aliases=aliases,
        compiler_params=_params(1),
        name="mixer_sample",
    )(z, c0_all, n0_rows, m0_rows, s0_all, ckt_all, cvt_all, gb, inv, sinks, rb, *extra_args)


def _mixer_post_kernel(hml_ref, osw_ref, ort_ref, mlo_ref, rtg_ref, g0_ref, g1_ref, g2_ref, x_ref,
                       mlg_ref, retg_ref, ng_ref, wml_ref, wsw_ref, wrt_ref, wout_ref, o_ref):
    o_ref[...] = _post_math(hml_ref[...], osw_ref[...], ort_ref[...], mlo_ref[...], rtg_ref[...],
                            g0_ref[...], g1_ref[...], g2_ref[...], x_ref[...],
                            mlg_ref[...], retg_ref[...], ng_ref[...],
                            wml_ref[...], wsw_ref[...], wrt_ref[...], wout_ref[...])


def mixer_post(hml, osw, ort, z, x, mlg, retg, ng, layer, wml, wsw, wrt, wout, tm, name):
    m = x.shape[0]

    def tok(w, cb=0):
        return pl.BlockSpec((tm, w), lambda i, cb=cb: (i, cb))

    def const(shape):
        return pl.BlockSpec(shape, lambda i: (0,) * len(shape))

    return pl.pallas_call(
        _mixer_post_kernel,
        grid=(m // tm,),
        in_specs=[tok(512), tok(512), tok(512), tok(512, ZC_ML_O), tok(512, ZC_RT_G),
                  tok(1024, ZC_GATES), tok(1024, ZC_GATES + 1), tok(1024, ZC_GATES + 2), tok(D_MODEL),
                  const((1, 512)), const((1, 512)), const((1, D_MODEL)),
                  _layer_weight(wml, layer), _layer_weight(wsw, layer), _layer_weight(wrt, layer),
                  _layer_weight(wout, layer)],
        out_specs=tok(D_MODEL),
        out_shape=jax.ShapeDtypeStruct((m, D_MODEL), F32),
        compiler_params=_params(1),
        name=name,
    )(hml, osw, ort, z, z, z, z, z, x, mlg, retg, ng, wml, wsw, wrt, wout)


def _xattn_head(q, k, v, mask):

    s = _dot_nt(jnp.where(mask, q, 0.0).astype(BF16), k) * (X_D ** -0.5)
    m = jnp.max(s, axis=1, keepdims=True)
    p = jnp.exp(s - m)
    p = p / jnp.sum(p, axis=1, keepdims=True)
    return jnp.where(mask, _dot(p.astype(BF16), v), 0.0)


def _xattn_ffn_prompt_kernel(x_ref, gxi_ref, gxo_ref, wcq_ref, kv_ref, wco_ref, gfi_ref, gfo_ref, wgu_ref, wd_ref,
                             o_ref, mid_scr):
    @pl.when(pl.program_id(0) == 0)
    def _():
        mid_scr[...] = jnp.zeros(mid_scr.shape, F32)

    x_mid = mid_scr[...]
    x = x_ref[...]
    lane = lax.broadcasted_iota(jnp.int32, (1, X_HEADS * X_D), 1)
    masks = [(lane >= X_D * h) & (lane < X_D * (h + 1)) for h in range(X_HEADS)]
    fc = D_FF // FFN_SPLIT
    heads_per_piece = X_HEADS // FFN_SPLIT

    u = _rms(x_mid, gfi_ref[...]).astype(BF16)
    q = _dot(_rms(x, gxi_ref[...]).astype(BF16), wcq_ref[...])
    k = kv_ref[:, 0:X_HEADS * X_D].astype(BF16)
    v = kv_ref[:, X_HEADS * X_D:2 * X_HEADS * X_D].astype(BF16)
    att = jnp.zeros(q.shape, F32)
    acc = None
    for c in range(FFN_SPLIT):
        g = _dot(u, wgu_ref[:, fc * c:fc * (c + 1)])
        up = _dot(u, wgu_ref[:, D_FF + fc * c:D_FF + fc * (c + 1)])
        for h in range(heads_per_piece * c, heads_per_piece * (c + 1)):
            att = att + _xattn_head(q, k, v, masks[h])
        hid = (g * _sigmoid(g) * up).astype(BF16)
        part = _dot(hid, wd_ref[fc * c:fc * (c + 1), :])
        acc = part if acc is None else acc + part
    o_ref[...] = x_mid + _rms(acc, gfo_ref[...])
    mid_scr[...] = x + _rms(_dot(att.astype(BF16), wco_ref[...]), gxo_ref[...])


def xattn_ffn_prompt(x, gxi, gxo, gfi, gfo, layer, wcq, kv, wco, wgu, wd, tm):
    m = x.shape[0]
    n = m // tm

    def const(shape):
        return pl.BlockSpec(shape, lambda s: (0,) * len(shape))

    gain = const((1, D_MODEL))
    return pl.pallas_call(
        _xattn_ffn_prompt_kernel,
        grid=(n + 1,),
        in_specs=[pl.BlockSpec((tm, D_MODEL), lambda s: (jnp.minimum(s, n - 1), 0)),
                  gain, gain, _layer_weight(wcq, layer), const(kv.shape), _layer_weight(wco, layer),
                  gain, gain, _layer_weight(wgu, layer), _layer_weight(wd, layer)],
        out_specs=pl.BlockSpec((tm, D_MODEL), lambda s: (jnp.maximum(s - 1, 0), 0)),
        out_shape=jax.ShapeDtypeStruct((m, D_MODEL), F32),
        scratch_shapes=[pltpu.VMEM((tm, D_MODEL), F32)],
        compiler_params=_params(1),
        name="xattn_ffn_prompt",
    )(x, gxi, gxo, wcq, kv, wco, gfi, gfo, wgu, wd)


def _xattn_sample_kernel(x_ref, gin_ref, gout_ref, wcq_ref, kt_ref, vt_ref, wco_ref, o_ref, q_scr, a_scr,
                         *, group):
    R = SEQ_PAD
    x = x_ref[...]
    q_scr[...] = _dot(_rms(x, gin_ref[...]).astype(BF16), wcq_ref[...])
    lane = lax.broadcasted_iota(jnp.int32, (1, X_HEADS * X_D), 1)
    masks = [(lane >= X_D * h) & (lane < X_D * (h + 1)) for h in range(X_HEADS)]
    second = (lax.broadcasted_iota(jnp.int32, (X_HEADS * R, 1), 0) & (R - 1)) >= N_NEW

    def body(pair, carry):
        rows = pl.ds(pl.multiple_of(pair * R, R), R)
        q = q_scr[rows, :]
        qs = jnp.concatenate([jnp.where(mk, q, 0.0) for mk in masks], axis=0).astype(BF16)
        s = jnp.where(second, _dot(qs, kt_ref[2 * pair + 1].astype(BF16)), _dot(qs, kt_ref[2 * pair].astype(BF16)))
        s = s * (X_D ** -0.5)
        m = jnp.max(s, axis=1, keepdims=True)
        p = jnp.exp(s - m)
        p = (p / jnp.sum(p, axis=1, keepdims=True)).astype(BF16)
        o = jnp.where(second, _dot_nt(p, vt_ref[2 * pair + 1].astype(BF16)), _dot_nt(p, vt_ref[2 * pair].astype(BF16)))
        acc = jnp.zeros((R, X_HEADS * X_D), F32)
        for h in range(X_HEADS):
            acc = acc + jnp.where(masks[h], o[R * h:R * (h + 1)], 0.0)
        a_scr[rows, :] = acc
        return carry

    lax.fori_loop(0, group // 2, body, 0, unroll=2)
    o_ref[...] = x + _rms(_dot(a_scr[...].astype(BF16), wco_ref[...]), gout_ref[...])


def xattn_sample(x, gin, gout, wcq, layer, mkt, mvt, wco, group=16):
    nb = mkt.shape[1]
    R = N_NEW

    def const(shape):
        return pl.BlockSpec(shape, lambda i: (0,) * len(shape))

    tok = pl.BlockSpec((group * R, D_MODEL), lambda i: (i, 0))
    mem = pl.BlockSpec((None, group, X_HEADS * X_D, N_MEM), lambda i: (layer, i, 0, 0))
    return pl.pallas_call(
        functools.partial(_xattn_sample_kernel, group=group),
        grid=(nb // group,),
        in_specs=[tok, const((1, D_MODEL)), const((1, D_MODEL)), _layer_weight(wcq, layer), mem, mem,
                  _layer_weight(wco, layer)],
        out_specs=tok,
        out_shape=jax.ShapeDtypeStruct((nb * R, D_MODEL), F32),
        scratch_shapes=[pltpu.VMEM((group * R, X_HEADS * X_D), F32),
                        pltpu.VMEM((group * R, X_HEADS * X_D), F32)],
        compiler_params=_params(1),
        name="xattn_sample",
    )(x, gin, gout, wcq, mkt, mvt, wco)


FFN_SPLIT = 2


def _ffn_kernel(x_ref, gin_ref, gout_ref, wgu_ref, wd_ref, o_ref):
    x = x_ref[...]
    u = _rms(x, gin_ref[...]).astype(BF16)
    fc = D_FF // FFN_SPLIT
    acc = None
    for c in range(FFN_SPLIT):
        g = _dot(u, wgu_ref[:, fc * c:fc * (c + 1)])
        up = _dot(u, wgu_ref[:, D_FF + fc * c:D_FF + fc * (c + 1)])
        h = (g * _sigmoid(g) * up).astype(BF16)
        part = _dot(h, wd_ref[fc * c:fc * (c + 1), :])
        acc = part if acc is None else acc + part
    o_ref[...] = x + _rms(acc, gout_ref[...])


def ffn(x, gin, gout, layer, wgu, wd, tm, name):
    m = x.shape[0]

    def const(shape):
        return pl.BlockSpec(shape, lambda i: (0,) * len(shape))

    tok = pl.BlockSpec((tm, D_MODEL), lambda i: (i, 0))
    return pl.pallas_call(
        _ffn_kernel,
        grid=(m // tm,),
        in_specs=[tok, const((1, D_MODEL)), const((1, D_MODEL)),
                  _layer_weight(wgu, layer), _layer_weight(wd, layer)],
        out_specs=tok,
        out_shape=jax.ShapeDtypeStruct((m, D_MODEL), F32),
        compiler_params=_params(1),
        name=name,
    )(x, gin, gout, wgu, wd)


def _reorder_w_in_t(w):
    wt = jnp.swapaxes(w, 1, 2)
    sizes = (512, 512, 512, 4, 4, 512, 512, 128, 128, 512, 512, 512, 512, 3072)
    offs = np.concatenate([[0], np.cumsum(sizes)])
    (ml_q, ml_k, ml_v, ml_i, ml_f, ml_o, sw_q, sw_k, sw_v, rt_q, rt_k, rt_v, rt_g, gates) = [
        wt[:, int(offs[i]):int(offs[i + 1])] for i in range(len(sizes))]
    pad = jnp.zeros((w.shape[0], 512 - 128 - 128 - 8, w.shape[1]), w.dtype)
    out = jnp.concatenate([ml_q, ml_k, ml_v, sw_q, sw_k, sw_v, ml_i, ml_f, pad, rt_q, rt_k, rt_v,
                           ml_o, rt_g, gates], axis=1)
    assert out.shape[1] == Z_COLS
    return out.astype(BF16)


def _row(v):
    return v.reshape(1, -1).astype(F32)


def _decoder_layer(x, z_fn, mixer_fn, xattn_ffn_fn):
    z = z_fn(x)
    x, states = mixer_fn(z, x)
    return xattn_ffn_fn(x), z, states


def kernel(x_prompt, x_sample, mem_prompt, state_mlstm_C, state_mlstm_n, state_mlstm_m, state_ret_S,
           cache_swa_k, cache_swa_v, cache_mem_k, cache_mem_v, norm_g, w_in, ml_gate_bias, ml_head_g,
           ret_head_g, swa_sinks, rel_bias, w_br_ml, w_br_swa, w_br_ret, w_out, w_cq, w_mkv, w_co,
           w_gu, w_down):
    depth = w_in.shape[0]
    bp, t, d = x_prompt.shape
    assert bp == 1 and d == D_MODEL and t % 1024 == 0
    nb, n_new, _ = x_sample.shape
    assert n_new == N_NEW
    R = SEQ_PAD

    xp = x_prompt.reshape(t, d)
    xs = x_sample.reshape(nb * n_new, d)
    mem = mem_prompt.reshape(N_MEM, d)
    half = HEAD_D // 2
    inv = ROPE_BASE ** (-jnp.arange(half, dtype=F32) / half)
    inv = jnp.concatenate([inv, inv]).reshape(1, LANES)
    rb = rel_bias.astype(F32).reshape(-1)
    mem_kt = jnp.transpose(cache_mem_k, (0, 1, 3, 4, 2)).reshape(depth, nb, X_HEADS * X_D, N_MEM)
    mem_vt = jnp.transpose(cache_mem_v, (0, 1, 3, 4, 2)).reshape(depth, nb, X_HEADS * X_D, N_MEM)

    win_t = _reorder_w_in_t(w_in)
    wml, wsw, wrt, wout = (w.astype(BF16) for w in (w_br_ml, w_br_swa, w_br_ret, w_out))
    wcq, wmkv, wco, wgu, wd = (w.astype(BF16) for w in (w_cq, w_mkv, w_co, w_gu, w_down))

    outs_p = {k: [] for k in ("C", "n", "m", "S", "k", "v", "mk", "mv")}
    outs_s = {k: [] for k in ("n", "m")}
    sample_layered = None
    swa_kt = jnp.transpose(cache_swa_k, (0, 1, 3, 4, 2)).reshape(depth, nb, LANES, CHUNK)
    swa_vt = jnp.transpose(cache_swa_v, (0, 1, 3, 4, 2)).reshape(depth, nb, LANES, CHUNK)
    for l in range(depth):
        ng = [_row(norm_g[l, i]) for i in range(7)]
        mlg, retg = _row(ml_head_g[l]), _row(ret_head_g[l])
        gb = jnp.concatenate([ml_gate_bias[l, 0], ml_gate_bias[l, 1],
                              jnp.zeros((LANES - 2 * N_HEADS,), F32)]).reshape(1, LANES).astype(F32)
        gb_rows = jnp.broadcast_to(ml_gate_bias[l].astype(F32).reshape(2 * N_HEADS, 1), (2 * N_HEADS, LANES))
        sinks = swa_sinks[l].astype(F32)

        kv = norm_matmul(mem, ng[6], wmkv, l, tm=N_MEM, tn=2 * X_HEADS * X_D, name="memory_kv")

        def seq_p(z, x):
            x1, c_, n_, m_, s_ = mixer_prompt(z, x, gb_rows, inv, sinks, rb, mlg, retg, ng[1], l,
                                              wml, wsw, wrt, wout)
            return x1, (c_, n_, m_, s_)

        xp, zp, (c_, n_, m_, s_) = _decoder_layer(
            xp,
            lambda x: norm_matmul(x, ng[0], win_t, l, tm=min(IN_PROJ_TM, t), tn=1024, name="in_proj_prompt",
                                  w_transposed=True),
            seq_p,
            lambda x: xattn_ffn_prompt(x, ng[2], ng[3], ng[4], ng[5], l, wcq, kv, wco, wgu, wd, tm=512))
        outs_p["C"].append(c_.reshape(1, N_HEADS, HEAD_D, HEAD_D))
        outs_p["n"].append(n_.reshape(1, N_HEADS, HEAD_D))
        outs_p["m"].append(m_[:, :N_HEADS])
        outs_p["S"].append(s_.reshape(1, N_HEADS, HEAD_D, HEAD_D))
        misc_last = zp[t - CHUNK:, 512 * ZC_MISC:512 * ZC_MISC + 2 * LANES]
        outs_p["k"].append(misc_last[:, :LANES].reshape(1, CHUNK, 2, SWA_D))
        outs_p["v"].append(misc_last[:, LANES:].reshape(1, CHUNK, 2, SWA_D))
        outs_p["mk"].append(kv[:, :X_HEADS * X_D].reshape(1, N_MEM, X_HEADS, X_D))
        outs_p["mv"].append(kv[:, X_HEADS * X_D:].reshape(1, N_MEM, X_HEADS, X_D))

        n0_rows = jnp.repeat(state_mlstm_n[l].astype(F32).reshape(nb, N_HEADS * HEAD_D), R, axis=0)
        m0_rows = jnp.repeat(jnp.pad(state_mlstm_m[l].astype(F32), ((0, 0), (0, LANES - N_HEADS))), R, axis=0)

        def seq_s(z, x):
            hml, osw, ort, n_, m_, *layered = mixer_sample(
                z, l, state_mlstm_C, n0_rows, m0_rows, state_ret_S, swa_kt, swa_vt, gb, inv, sinks, rb,
                sample_layered)
            x1 = mixer_post(hml, osw, ort, z, x, mlg, retg, ng[1], l, wml, wsw, wrt, wout,
                            tm=min(256, x.shape[0]), name="mixer_post_sample")
            return x1, (n_, m_, layered)

        xs, zs, (n_, m_, sample_layered) = _decoder_layer(
            xs,
            lambda x: norm_matmul(x, ng[0], win_t, l, tm=min(1024, nb * n_new), tn=1024, name="in_proj_sample",
                                  w_transposed=True),
            seq_s,
            lambda x: ffn(xattn_sample(x, ng[2], ng[3], wcq, l, mem_kt, mem_vt, wco),
                          ng[4], ng[5], l, wgu, wd, tm=min(512, nb * n_new), name="ffn_sample"))
        outs_s["n"].append(n_.reshape(nb, R, N_HEADS, HEAD_D)[:, R - 1])
        outs_s["m"].append(m_.reshape(nb, R, LANES)[:, n_new - 1, :N_HEADS])

    sample_c, sample_s, swa_kt_new, swa_vt_new = sample_layered
    s_swa_k = jnp.transpose(swa_kt_new.reshape(depth, nb, 2, SWA_D, CHUNK), (0, 1, 4, 2, 3))
    s_swa_v = jnp.transpose(swa_vt_new.reshape(depth, nb, 2, SWA_D, CHUNK), (0, 1, 4, 2, 3))
    y_p = xp.reshape(1, t, d)
    y_s = xs.reshape(nb, n_new, d)
    st = lambda d_, k: jnp.stack(d_[k])
    return (y_p, y_s,
            st(outs_p, "C"), st(outs_p, "n"), st(outs_p, "m"), st(outs_p, "S"),
            st(outs_p, "k"), st(outs_p, "v"), st(outs_p, "mk"), st(outs_p, "mv"),
            sample_c, st(outs_s, "n"), st(outs_s, "m"), sample_s, s_swa_k, s_swa_v)
```

```python
import functools
import math

import numpy as np
import jax
import jax.numpy as jnp
from jax import lax
from jax.experimental import pallas as pl
from jax.experimental.pallas import tpu as pltpu

F32 = jnp.float32
BF16 = jnp.bfloat16

D_MODEL = 1024
EPS = 1e-6
PAST_LEN = 16384
HEAD_D = 128
N_HEADS = 4
CHUNK = 128
SWA_HEADS = 8
SWA_D = 64
N_BUCKETS = 32
MAX_DISTANCE = 128
ROPE_BASE = 10000.0
N_MEM = 256
X_HEADS = 4
X_D = 64
D_FF = 2816
SEQ_PAD = 8
N_NEW = 4
LANES = 128
IN_PROJ_TM = 2048
MXU_MIN_ROWS = 16
VMEM_LIMIT = 48 * 1024 * 1024

ZC_ML_Q, ZC_ML_K, ZC_ML_V, ZC_SW_Q, ZC_MISC, ZC_RT_Q, ZC_RT_K, ZC_RT_V, ZC_ML_O, ZC_RT_G = range(10)
ZC_GATES = 5
Z_COLS = 8192
Z_HALF = Z_COLS // 2
MISC_K, MISC_V, MISC_IF = 0, 128, 256

LOG2E = math.log2(math.e)
LN2 = math.log(2.0)
LOG_GAMMA = tuple(float(v) for v in np.log1p(-np.exp2(-5.0 - np.arange(N_HEADS, dtype=np.float32))))

SWA_SAME = (0, 2, 5, 7)
SWA_SWAP = (1, 3, 4, 6)


def _t5_bucket_bounds():
    n = np.arange(CHUNK)
    max_exact = N_BUCKETS // 2
    nf = np.maximum(n, 1).astype(np.float32)
    large = max_exact + (np.log(nf / np.float32(max_exact)) / np.float32(math.log(MAX_DISTANCE / max_exact))
                         * np.float32(N_BUCKETS - max_exact)).astype(np.int32)
    large = np.minimum(large, N_BUCKETS - 1)
    b = np.where(n < max_exact, n, large)
    assert np.all(np.diff(b) >= 0)
    runs = []
    for d in range(CHUNK):
        if runs and runs[-1][1] == int(b[d]):
            runs[-1][0] = d + 1
        else:
            runs.append([d + 1, int(b[d])])
    return tuple((hi, bk) for hi, bk in runs)


BUCKET_RUNS = _t5_bucket_bounds()


def _dot(a, b):
    return jnp.dot(a, b, preferred_element_type=F32)


def _dot_nt(a, b):
    return lax.dot_general(a, b, (((1,), (1,)), ((), ())), preferred_element_type=F32)


def _dot_tn(a, b):
    return lax.dot_general(a, b, (((0,), (0,)), ((), ())), preferred_element_type=F32)


def _rms(x, g):
    return x * lax.rsqrt(jnp.mean(x * x, axis=-1, keepdims=True) + EPS) * g


def _sigmoid(x):
    return 1.0 / (1.0 + jnp.exp2(x * (-LOG2E)))


def _log_sigmoid(x):
    return jnp.minimum(x, 0.0) - jnp.log1p(jnp.exp(-jnp.abs(x)))


def _layer_weight(w_all, layer):
    shape = w_all.shape[1:]
    return pl.BlockSpec((None,) + shape, lambda *_: (layer,) + (0,) * len(shape), pipeline_mode=pl.Buffered(1))


def _params(n_grid):
    return pltpu.CompilerParams(dimension_semantics=("arbitrary",) * n_grid, vmem_limit_bytes=VMEM_LIMIT)


def _norm_matmul_kernel(x_ref, g_ref, w_ref, o_ref, u_ref, *, w_transposed):
    @pl.when(pl.program_id(1) == 0)
    def _():
        u_ref[...] = _rms(x_ref[...], g_ref[...]).astype(BF16)

    o_ref[...] = _dot_nt(u_ref[...], w_ref[...]) if w_transposed else _dot(u_ref[...], w_ref[...])


def norm_matmul(x, g, w_all, layer, tm, tn, name, w_transposed=False):
    m, k = x.shape
    n = w_all.shape[1] if w_transposed else w_all.shape[2]
    if w_transposed:
        w_spec = pl.BlockSpec((None, tn, k), lambda i, j: (layer, j, 0))
    else:
        w_spec = pl.BlockSpec((None, k, tn), lambda i, j: (layer, 0, j))
    return pl.pallas_call(
        functools.partial(_norm_matmul_kernel, w_transposed=w_transposed),
        grid=(m // tm, n // tn),
        in_specs=[pl.BlockSpec((tm, k), lambda i, j: (i, 0)),
                  pl.BlockSpec((1, k), lambda i, j: (0, 0)),
                  w_spec],
        out_specs=pl.BlockSpec((tm, tn), lambda i, j: (i, j)),
        out_shape=jax.ShapeDtypeStruct((m, n), F32),
        scratch_shapes=[pltpu.VMEM((tm, k), BF16)],
        compiler_params=_params(2),
        name=name,
    )(x, g, w_all)


def _transpose_bf16(x, eye):
    return _dot_nt(eye, x).astype(BF16)


def _build_swa_table_t(tab_ref, rb_ref):
    L = CHUNK
    srow = lax.broadcasted_iota(jnp.int32, (2 * L, L), 0)
    lcol = lax.broadcasted_iota(jnp.int32, (2 * L, L), 1)
    delta = lcol + L - srow
    valid = (delta >= 0) & (delta < L)
    for var, heads in enumerate((SWA_SAME, SWA_SWAP)):
        for i, h in enumerate(heads):
            val = jnp.full((2 * L, L), rb_ref[BUCKET_RUNS[-1][1] * SWA_HEADS + h], F32)
            for hi, bk in reversed(BUCKET_RUNS[:-1]):
                val = jnp.where(delta < hi, rb_ref[bk * SWA_HEADS + h], val)
            tab_ref[var, :, i * L:(i + 1) * L] = jnp.where(valid, val, -jnp.inf)
            tab_ref[2 + var, :, i * L:(i + 1) * L] = jnp.where(valid & (srow >= L), val, -jnp.inf)


def _swa_scores_t(q, k_same, k_swap, tab_ref, first):
    lane = lax.broadcasted_iota(jnp.int32, (1, LANES), 1)
    upper = lane >= SWA_D
    q = q * (SWA_D ** -0.5)
    ks = (k_same[...], k_swap[...])
    scores = []
    for var, heads in enumerate((SWA_SAME, SWA_SWAP)):
        qm = []
        for h in heads:
            blk = q[:, LANES * (h // 2):LANES * (h // 2 + 1)]
            qm.append(jnp.where(upper if h % 2 == 1 else jnp.logical_not(upper), blk, 0.0))
        qs = jnp.concatenate(qm, axis=0).astype(BF16)
        scores.append(_dot_nt(ks[var], qs) + tab_ref[var + 2 * first])
    return scores


def _swa_softmax_t(scores, sinks_ref):
    probs = []
    for s_t, heads in zip(scores, (SWA_SAME, SWA_SWAP)):
        sink = jnp.concatenate([jnp.full((1, CHUNK), sinks_ref[h], F32) for h in heads], axis=1)
        m = jnp.maximum(jnp.max(s_t, axis=0, keepdims=True), sink)
        p = jnp.exp(s_t - m)
        norm = 1.0 / (jnp.sum(p, axis=0, keepdims=True) + jnp.exp(sink - m))
        probs.append((p.astype(BF16), norm))
    return probs


def _swa_values_t(probs, vt_buf):
    vt = vt_buf[...]
    vts = (vt, jnp.concatenate([vt[SWA_D:], vt[:SWA_D]], axis=0))
    return [_dot(vts[var], p) * norm for var, (p, norm) in enumerate(probs)]


def _swa_output_t(values):
    L = CHUNK
    lane = lax.broadcasted_iota(jnp.int32, (1, LANES), 1)
    upper = lane >= SWA_D
    outs = [None] * SWA_HEADS
    for o_t, heads in zip(values, (SWA_SAME, SWA_SWAP)):
        for i, h in enumerate(heads):
            outs[h] = o_t[:, i * L:(i + 1) * L].T
    blocks = [jnp.where(upper, outs[2 * j + 1], outs[2 * j]) for j in range(SWA_HEADS // 2)]
    return jnp.concatenate(blocks, axis=1)


def _rotary_tables(pos, inv_ref):
    lane = lax.broadcasted_iota(jnp.int32, (1, LANES), 1)
    ang = pos * inv_ref[...]
    sin = jnp.sin(ang)
    return jnp.cos(ang), jnp.where(lane < HEAD_D // 2, -sin, sin)


def _rotate(x, cos, sin_signed):
    return x * cos + pltpu.roll(x, HEAD_D // 2, 1) * sin_signed


def _gate_slab(raw, gb_ref):
    lane = lax.broadcasted_iota(jnp.int32, (1, LANES), 1)
    x = raw + gb_ref[...]
    return jnp.where((lane >= N_HEADS) & (lane < 2 * N_HEADS), _log_sigmoid(x), x)


def _head_rms(h, gain):
    parts = []
    for i in range(N_HEADS):
        blk = h[:, HEAD_D * i:HEAD_D * (i + 1)]
        parts.append(blk * lax.rsqrt(jnp.mean(blk * blk, axis=-1, keepdims=True) + EPS))
    return jnp.concatenate(parts, axis=-1) * gain


def _rms_over_rows(h_t):
    return h_t * lax.rsqrt(jnp.mean(h_t * h_t, axis=0, keepdims=True) + EPS)


def _post_math(hml, osw, ort, mlo, rtg, g0, g1, g2, x, mlg, retg, ng, wml, wsw, wrt, wout, head_normed=False):
    if not head_normed:
        hml = _head_rms(hml, mlg)
        ort = _head_rms(ort, retg)
    hm = hml * _sigmoid(mlo)
    y_ml = _dot(hm.astype(BF16), wml)
    y_sw = _dot(osw.astype(BF16), wsw)
    rt = ort * (rtg * _sigmoid(rtg))
    y_rt = _dot(rt.astype(BF16), wrt)
    merged = _sigmoid(g0) * y_ml + _sigmoid(g1) * y_sw + _sigmoid(g2) * y_rt
    return x + _rms(_dot(merged.astype(BF16), wout), ng)


def _z_views(z_half_ref, first_col_block):
    return {first_col_block + i: z_half_ref.at[:, 512 * i:512 * (i + 1)] for i in range(Z_HALF // 512)}


def _mixer_seq_prompt_kernel(zmix_ref, ztail_ref, x_ref,
                             gbr_ref, inv_ref, sinks_ref, rb_ref,
                             mlg_ref, retg_ref, ng_ref, wml_ref, wsw_ref, wrt_ref, wout_ref,
                             x1_ref, c_ref, n_ref, m_ref, s_ref,
                             st_scr, k_same, k_swap, vt_buf, tab_t, cos_l, sin_l, dec_in_t, eye_ref,
                             hml_s, osw_s, ort_s):
    zm = _z_views(zmix_ref, 0)
    mq_ref, mk_ref, mv_ref, swq_ref, misc_ref = (zm[i] for i in (ZC_ML_Q, ZC_ML_K, ZC_ML_V, ZC_SW_Q, ZC_MISC))
    rq_ref, rk_ref, rv_ref = (zm[i] for i in (ZC_RT_Q, ZC_RT_K, ZC_RT_V))
    zt = _z_views(ztail_ref, Z_HALF // 512)
    mlo_ref, rtg_ref = zt[ZC_ML_O], zt[ZC_RT_G]
    g0_ref, g1_ref, g2_ref = (ztail_ref.at[:, 1024 * (ZC_GATES + i) - Z_HALF:1024 * (ZC_GATES + i + 1) - Z_HALF]
                              for i in range(3))
    c = pl.program_id(0)
    active = c < pl.num_programs(0) - 1
    L = CHUNK
    lane = lax.broadcasted_iota(jnp.int32, (1, LANES), 1)
    row = lax.broadcasted_iota(jnp.int32, (L, L), 0)
    col = lax.broadcasted_iota(jnp.int32, (L, L), 1)
    causal_t = row <= col

    @pl.when(c == 0)
    def _init():
        c_ref[...] = jnp.zeros(c_ref.shape, F32)
        n_ref[...] = jnp.zeros(n_ref.shape, F32)
        m_ref[...] = jnp.zeros(m_ref.shape, F32)
        s_ref[...] = jnp.zeros(s_ref.shape, F32)
        hml_s[...] = jnp.zeros(hml_s.shape, F32)
        osw_s[...] = jnp.zeros(osw_s.shape, F32)
        ort_s[...] = jnp.zeros(ort_s.shape, F32)
        st_scr[...] = jnp.zeros(st_scr.shape, F32)
        k_same[...] = jnp.zeros(k_same.shape, BF16)
        k_swap[...] = jnp.zeros(k_swap.shape, BF16)
        vt_buf[...] = jnp.zeros(vt_buf.shape, BF16)
        _build_swa_table_t(tab_t, rb_ref)
        ang = row.astype(F32) * inv_ref[...]
        cos_l[...] = jnp.cos(ang)
        sin_l[...] = jnp.sin(ang)
        rel_t = (col - row).astype(F32)
        for h in range(N_HEADS):
            dec_in_t[h] = jnp.where(causal_t, jnp.exp(LOG_GAMMA[h] * rel_t), 0.0)
        eye_ref[...] = jnp.where(row == col, 1.0, 0.0).astype(BF16)

    eye = eye_ref[...]
    heads = range(N_HEADS)
    hsl = [slice(HEAD_D * h, HEAD_D * (h + 1)) for h in heads]
    sub8 = lax.broadcasted_iota(jnp.int32, (2 * N_HEADS, 1), 0)
    hml_prev, osw_prev, ort_prev = hml_s[...], osw_s[...], ort_s[...]

    raw = misc_ref[:, MISC_IF:MISC_IF + LANES].T[0:2 * N_HEADS] + gbr_ref[...]
    gates = jnp.where(sub8 >= N_HEADS, _log_sigmoid(raw), raw)
    cum = gates
    for sh in (1, 2, 4, 8, 16, 32, 64):
        cum = cum + jnp.where(lane >= sh, pltpu.roll(cum, sh, 1), 0.0)
    g8 = jnp.where(sub8 < N_HEADS, gates - pltpu.roll(cum, N_HEADS, 0), 0.0)
    g2_cols = jnp.concatenate([g8 * LOG2E, jnp.zeros((L - 2 * N_HEADS, L), F32)], axis=0).T

    ang0 = (c * L).astype(F32) * inv_ref[...]
    cos0 = jnp.cos(ang0)
    sin0 = jnp.sin(ang0)
    cos_t = cos_l[...]
    sin_t = sin_l[...]
    cos = cos0 * cos_t - sin0 * sin_t
    sin = sin0 * cos_t + cos0 * sin_t
    sin_signed = jnp.where(lane < HEAD_D // 2, -sin, sin)
    ml_ops, rt_ops = [], []
    for h in heads:
        qb = mq_ref[:, hsl[h]].astype(BF16)
        kb = (mk_ref[:, hsl[h]] * (HEAD_D ** -0.5)).astype(BF16)
        vt = _transpose_bf16(mv_ref[:, hsl[h]].astype(BF16), eye)
        ml_ops.append((qb, kb, vt, _dot_nt(kb, qb)))
    for h in heads:
        qr = _rotate(rq_ref[:, hsl[h]], cos, sin_signed).astype(BF16)
        kr = (_rotate(rk_ref[:, hsl[h]], cos, sin_signed) * (HEAD_D ** -0.5)).astype(BF16)
        vt = _transpose_bf16(rv_ref[:, hsl[h]].astype(BF16), eye)
        rt_ops.append((qr, kr, vt, _dot_nt(kr, qr)))
    k_new = misc_ref[:, MISC_K:MISC_K + LANES]
    k_same[0:L, :] = k_same[L:2 * L, :]
    k_swap[0:L, :] = k_swap[L:2 * L, :]
    k_same[L:2 * L, :] = k_new.astype(BF16)
    k_swap[L:2 * L, :] = pltpu.roll(k_new, SWA_D, 1).astype(BF16)
    vt_buf[:, 0:L] = vt_buf[:, L:2 * L]
    vt_buf[:, L:2 * L] = _transpose_bf16(misc_ref[:, MISC_V:MISC_V + LANES].astype(BF16), eye)
    swa_scores = _swa_scores_t(swq_ref[...], k_same, k_swap, tab_t, (c == 0).astype(jnp.int32))

    y_ml = _dot((hml_prev * _sigmoid(mlo_ref[...])).astype(BF16), wml_ref[...])

    m_all = m_ref[...]
    gate_ops = []
    for h in heads:
        m_prev = m_all[:, h:h + 1]
        gm2 = jnp.where(causal_t, g2_cols[:, h:h + 1], -jnp.inf)
        mx2 = jnp.maximum(jnp.max(gm2, axis=0, keepdims=True), m_prev * LOG2E)
        gate_ops.append((m_prev, mx2 * LN2, jnp.exp2(m_prev * LOG2E - mx2), jnp.exp2(gm2 - mx2)))

    y_sw = _dot(osw_prev.astype(BF16), wsw_ref[...])
    rtg = rtg_ref[...]
    y_rt = _dot((ort_prev * (rtg * _sigmoid(rtg))).astype(BF16), wrt_ref[...])

    l_row = lane.astype(F32)
    ml_out, rt_out = [], []
    for h in heads:
        qb, kb, vt, qk = ml_ops[h]
        m_prev, mx, inter, dm_t = gate_ops[h]
        s_t = qk * dm_t
        num_t = _dot(vt, s_t.astype(BF16)) + _dot_nt(c_ref[h].astype(BF16), qb) * inter
        nq = _dot_nt(jnp.broadcast_to(n_ref[h], (MXU_MIN_ROWS, HEAD_D)).astype(BF16), qb)[0:1, :]
        den = jnp.sum(s_t, axis=0, keepdims=True) + inter * nq
        m_t = cum[N_HEADS + h:N_HEADS + h + 1, :] + mx
        den = jnp.maximum(jnp.abs(den), jnp.exp(-m_t))
        ml_out.append((num_t * (1.0 / den), m_t))
    for h in heads:
        qr, kr, vt, qk = rt_ops[h]
        s_t = qk * dec_in_t[h]
        q_decay = jnp.exp(LOG_GAMMA[h] * (l_row + 1.0))
        rt_out.append(_dot(vt, s_t.astype(BF16)) + _dot_nt(st_scr[h].astype(BF16), qr) * q_decay)
    swa_values = _swa_values_t(_swa_softmax_t(swa_scores, sinks_ref), vt_buf)

    merged = _sigmoid(g0_ref[...]) * y_ml + _sigmoid(g1_ref[...]) * y_sw + _sigmoid(g2_ref[...]) * y_rt
    x1_ref[...] = x_ref[...] + _rms(_dot(merged.astype(BF16), wout_ref[...]), ng_ref[...])

    osw_s[...] = _swa_output_t(swa_values)
    for h in heads:
        hml_s[:, hsl[h]] = _rms_over_rows(ml_out[h][0]).T * mlg_ref[:, hsl[h]]
        ort_s[:, hsl[h]] = _rms_over_rows(rt_out[h]).T * retg_ref[:, hsl[h]]
    m_out = jnp.zeros((1, LANES), F32)
    for h in heads:
        qb, kb, vt, _ = ml_ops[h]
        m_prev = gate_ops[h][0]
        m_t = ml_out[h][1]
        m_new = m_t[:, L - 1:L]
        b_last = cum[N_HEADS + h:N_HEADS + h + 1, L - 1:L]
        decay = jnp.exp(b_last + m_prev - m_new)
        w_r = jnp.exp(g8[h:h + 1, :] + (b_last - m_new))
        c_old = c_ref[h]
        n_old = n_ref[h]
        c_new = decay * c_old + _dot((vt.astype(F32) * w_r).astype(BF16), kb)
        n_new = decay * n_old + _dot(jnp.broadcast_to(w_r, (MXU_MIN_ROWS, L)).astype(BF16), kb)[0:1, :]
        c_ref[h] = jnp.where(active, c_new, c_old)
        n_ref[h] = jnp.where(active, n_new, n_old)
        m_out = jnp.where(lane == h, m_new, m_out)
    m_ref[...] = jnp.where(active, m_out, m_all)
    for h in heads:
        qr, kr, vt, _ = rt_ops[h]
        lg = LOG_GAMMA[h]
        k_decay = jnp.exp(lg * (L - 1.0 - l_row))
        st_old = st_scr[h]
        st_new = math.exp(lg * L) * st_old + _dot((vt.astype(F32) * k_decay).astype(BF16), kr)
        st_scr[h] = jnp.where(active, st_new, st_old)

    @pl.when(c == pl.num_programs(0) - 1)
    def _emit_state():
        for h in heads:
            s_ref[h] = st_scr[h].T


def mixer_prompt(z, x, gb, inv, sinks, rb, mlg, retg, ng, layer, wml, wsw, wrt, wout):
    t = z.shape[0]
    L = CHUNK
    n = t // L

    z_mix = pl.BlockSpec((L, Z_HALF), lambda c: (jnp.minimum(c, n - 1), 0))

    def tail(w, cb=0):
        return pl.BlockSpec((L, w), lambda c, cb=cb: (jnp.maximum(c - 1, 0), cb))

    def const(shape):
        return pl.BlockSpec(shape, lambda c: (0,) * len(shape))

    def weight(w):
        return _layer_weight(w, layer)

    smem = pl.BlockSpec(memory_space=pltpu.SMEM)
    return pl.pallas_call(
        _mixer_seq_prompt_kernel,
        grid=(n + 1,),
        in_specs=[z_mix, tail(Z_HALF, 1), tail(D_MODEL),
                  const((2 * N_HEADS, LANES)), const((1, LANES)), smem, smem,
                  const((1, 512)), const((1, 512)), const((1, D_MODEL)),
                  weight(wml), weight(wsw), weight(wrt), weight(wout)],
        out_specs=[tail(D_MODEL),
                   const((N_HEADS, HEAD_D, HEAD_D)), const((N_HEADS, 1, HEAD_D)), const((1, LANES)),
                   const((N_HEADS, HEAD_D, HEAD_D))],
        out_shape=[jax.ShapeDtypeStruct((t, D_MODEL), F32),
                   jax.ShapeDtypeStruct((N_HEADS, HEAD_D, HEAD_D), F32),
                   jax.ShapeDtypeStruct((N_HEADS, 1, HEAD_D), F32),
                   jax.ShapeDtypeStruct((1, LANES), F32),
                   jax.ShapeDtypeStruct((N_HEADS, HEAD_D, HEAD_D), F32)],
        scratch_shapes=[pltpu.VMEM((N_HEADS, L, L), F32),
                        pltpu.VMEM((2 * L, LANES), BF16), pltpu.VMEM((2 * L, LANES), BF16),
                        pltpu.VMEM((LANES, 2 * L), BF16),
                        pltpu.VMEM((4, 2 * L, 4 * L), F32),
                        pltpu.VMEM((L, L), F32), pltpu.VMEM((L, L), F32),
                        pltpu.VMEM((N_HEADS, L, L), F32),
                        pltpu.VMEM((L, L), BF16),
                        pltpu.VMEM((L, 512), F32), pltpu.VMEM((L, 512), F32), pltpu.VMEM((L, 512), F32)],
        compiler_params=_params(1),
        name="mixer_prompt",
    )(z, z, x, gb, inv, sinks, rb, mlg, retg, ng, wml, wsw, wrt, wout)


SAMPLE_GROUP = 8


def _tile_bcast(x, l_idx, src):
    n_rows = x.shape[0]
    out = jnp.zeros_like(x)
    for j in range(SEQ_PAD):
        out = out + jnp.where(l_idx == j, pltpu.roll(x, (j - src) % n_rows, 0), 0.0)
    return out


def _tile_total(x, l_idx):
    for d in (1, 2, 4):
        x = x + jnp.where(l_idx >= d, pltpu.roll(x, d, 0), 0.0)
    return x


def _per_seq(fn, group):
    return jnp.concatenate([fn(g) for g in range(group)], axis=0)


def _build_sample_swa_tables(tab_prev, tab_cur, rb_ref, group):
    R = SEQ_PAD
    rows = group * SWA_HEADS * R
    row = lax.broadcasted_iota(jnp.int32, (rows, LANES), 0)
    col = lax.broadcasted_iota(jnp.int32, (rows, LANES), 1)
    l = row & (R - 1)
    h = (row >> 3) & (SWA_HEADS - 1)
    g = row >> 6
    delta = CHUNK + l - col
    prev = jnp.zeros((rows, LANES), F32)
    cur = jnp.zeros((rows, LANES), F32)
    dcur = l - (col & (R - 1))
    for hh in range(SWA_HEADS):
        val = jnp.full((rows, LANES), rb_ref[BUCKET_RUNS[-1][1] * SWA_HEADS + hh], F32)
        for hi, bk in reversed(BUCKET_RUNS[:-1]):
            val = jnp.where(delta < hi, rb_ref[bk * SWA_HEADS + hh], val)
        prev = jnp.where(h == hh, val, prev)
        valc = jnp.zeros((rows, LANES), F32)
        for d in range(R):
            valc = jnp.where(dcur == d, rb_ref[d * SWA_HEADS + hh], valc)
        cur = jnp.where(h == hh, valc, cur)
    tab_prev[...] = jnp.where((delta >= 0) & (delta < CHUNK), prev, -jnp.inf)
    same_seq = ((col >> 3) == g) & (col < group * R)
    tab_cur[...] = jnp.where(same_seq & (dcur >= 0), cur, -jnp.inf)


def _sample_mixers_kernel(zmix_ref, c0_ref, n0_ref, m0_ref, s0_ref, ckt_ref, cvt_ref,
                          gb_ref, inv_ref, sinks_ref, rb_ref, *rest, group, first_layer):
    zm = _z_views(zmix_ref, 0)
    mq_ref, mk_ref, mv_ref, swq_ref, misc_ref = (zm[i] for i in (ZC_ML_Q, ZC_ML_K, ZC_ML_V, ZC_SW_Q, ZC_MISC))
    rq_ref, rk_ref, rv_ref = (zm[i] for i in (ZC_RT_Q, ZC_RT_K, ZC_RT_V))
    n_layered = 4
    if not first_layer:
        rest = rest[n_layered:]
    hml_ref, osw_ref, ort_ref, n_ref, m_ref, c_all, s_all, kc_all, vc_all, tab_prev, tab_cur = rest
    layered = []
    for ref in (c_all, s_all, kc_all, vc_all):
        if first_layer:
            ref[1:] = jnp.zeros((ref.shape[0] - 1,) + ref.shape[1:], F32)
            layered.append(ref.at[0])
        else:
            layered.append(ref)
    c_ref, s_ref, kc_ref, vc_ref = layered
    R = SEQ_PAD
    NR = group * R

    @pl.when(pl.program_id(0) == 0)
    def _init():
        _build_sample_swa_tables(tab_prev, tab_cur, rb_ref, group)

    lane = lax.broadcasted_iota(jnp.int32, (1, LANES), 1)
    l_idx = lax.broadcasted_iota(jnp.int32, (NR, 1), 0) & (R - 1)
    real = l_idx < N_NEW
    l_f = l_idx.astype(F32)

    def shift(x, d):
        return x if d == 0 else pltpu.roll(x, d, 0)

    def col(slab, h):
        return slab[:, h:h + 1]

    sub = lax.broadcasted_iota(jnp.int32, (R, 1), 0)

    def ld(ref, cols=slice(None)):
        xc = ref[:, cols]
        tiles = []
        for t in range(group // 2):
            tile = xc[R * t:R * (t + 1)]
            tiles.append(jnp.where(sub < N_NEW, tile, 0.0))
            tiles.append(jnp.where(sub < N_NEW, pltpu.roll(tile, N_NEW, 0), 0.0))
        return jnp.concatenate(tiles, axis=0)

    def compact(xp):
        return jnp.concatenate(
            [jnp.where(sub < N_NEW, xp[2 * R * t:2 * R * t + R], pltpu.roll(xp[2 * R * t + R:2 * R * (t + 1)], N_NEW, 0))
             for t in range(group // 2)], axis=0)

    zero_rows = jnp.zeros((LANES - NR, LANES), F32)
    k_new = jnp.concatenate([ld(misc_ref, slice(MISC_K, MISC_K + LANES)), zero_rows], axis=0)
    v_new = jnp.concatenate([ld(misc_ref, slice(MISC_V, MISC_V + LANES)), zero_rows], axis=0)
    k_new_t = k_new.T
    v_new_t = v_new.T

    def emit_window_buffers(g):
        back = (LANES - R * g) % LANES
        for new_t, cache_ref, out_ref in ((k_new_t, ckt_ref, kc_ref), (v_new_t, cvt_ref, vc_ref)):
            merged = jnp.where(lane < N_NEW, pltpu.roll(new_t, back, 1) if back else new_t, cache_ref[g])
            out_ref[g] = pltpu.roll(merged, LANES - N_NEW, 1)

    lf = _gate_slab(ld(misc_ref, slice(MISC_IF, MISC_IF + LANES)), gb_ref)
    bsum = lf
    for d in range(1, N_NEW):
        bsum = bsum + jnp.where(l_idx >= d, shift(lf, d), 0.0)
    b = pltpu.roll(bsum, LANES - N_HEADS, 1)
    gs = lf - b
    m0 = m0_ref[...]
    log_inter = b + m0
    logd = [jnp.where(l_idx >= d, b + shift(gs, d), -jnp.inf) for d in range(N_NEW)]
    m_t = log_inter
    for d in range(N_NEW):
        m_t = jnp.maximum(m_t, logd[d])
    inter = jnp.exp(log_inter - m_t)
    dm = [jnp.exp(logd[d] - m_t) for d in range(N_NEW)]
    emt = jnp.exp(-m_t)
    b_last = _tile_bcast(b, l_idx, N_NEW - 1)
    m_new = _tile_bcast(m_t, l_idx, N_NEW - 1)
    decay = jnp.exp(b_last + m0 - m_new)
    w = jnp.where(real, jnp.exp(b_last - b + lf - m_new), 0.0)
    m_ref[...] = m_t
    n0 = n0_ref[...]
    for h in range(N_HEADS):
        hs = slice(HEAD_D * h, HEAD_D * (h + 1))
        q = ld(mq_ref, hs)
        k = ld(mk_ref, hs) * (HEAD_D ** -0.5)
        v = ld(mv_ref, hs)
        qb = q.astype(BF16)
        kb = k.astype(BF16)
        inter_c = col(inter, h)
        num = _per_seq(lambda g: _dot_nt(qb[R * g:R * (g + 1)], c0_ref[g, h].astype(BF16)), group) * inter_c
        den = inter_c * jnp.sum(q * n0[:, hs], axis=1, keepdims=True)
        for d in range(N_NEW):
            s_d = jnp.sum(q * shift(k, d), axis=1, keepdims=True) * col(dm[d], h)
            num = num + s_d * shift(v, d)
            den = den + s_d
        den = jnp.maximum(jnp.abs(den), col(emt, h))
        hml_ref[:, hs] = compact(num / den)
        w_c = col(w, h)
        dec_c = col(decay, h)
        vw = (v * w_c).astype(BF16)
        for g in range(group):
            rs = slice(R * g, R * (g + 1))
            c_ref[g, h] = dec_c[R * g:R * g + 1] * c0_ref[g, h] + _dot_tn(vw[rs], kb[rs])
        n_ref[:, hs] = dec_c * n0[:, hs] + _tile_total(k * w_c, l_idx)
        for g in range(h * group // N_HEADS, (h + 1) * group // N_HEADS):
            emit_window_buffers(g)

    cos, sin_signed = _rotary_tables((PAST_LEN + l_idx).astype(F32), inv_ref)
    for h in range(N_HEADS):
        hs = slice(HEAD_D * h, HEAD_D * (h + 1))
        lg = LOG_GAMMA[h]
        qr = _rotate(ld(rq_ref, hs), cos, sin_signed)
        kr = _rotate(ld(rk_ref, hs), cos, sin_signed) * (HEAD_D ** -0.5)
        v = ld(rv_ref, hs)
        qrb = qr.astype(BF16)
        vb = v.astype(BF16)
        o = _per_seq(lambda g: _dot(qrb[R * g:R * (g + 1)], s0_ref[g, h].astype(BF16)), group)
        o = o * jnp.exp(lg * (l_f + 1.0))
        for d in range(N_NEW):
            s_d = jnp.sum(qr * shift(kr, d), axis=1, keepdims=True) * math.exp(lg * d)
            o = o + jnp.where(l_idx >= d, s_d, 0.0) * shift(v, d)
        ort_ref[:, hs] = compact(o)
        kd =(kr * jnp.where(real, jnp.exp(lg * (N_NEW - 1.0 - l_f)), 0.0)).astype(BF16)
        for g in range(group):
            rs = slice(R * g, R * (g + 1))
            s_ref[g, h] = math.exp(lg * N_NEW) * s0_ref[g, h] + _dot_tn(kd[rs], vb[rs])

    upper = lane >= SWA_D
    q_all = ld(swq_ref) * (SWA_D ** -0.5)
    q_heads = []
    for h in range(SWA_HEADS):
        blk = q_all[:, LANES * (h // 2):LANES * (h // 2 + 1)]
        qh = jnp.where(upper if h % 2 == 1 else jnp.logical_not(upper), blk, 0.0)
        if h % 2 != h // (SWA_HEADS // 2):
            qh = pltpu.roll(qh, SWA_D, 1)
        q_heads.append(qh)
    qs = jnp.concatenate([q_heads[h][R * g:R * (g + 1)] for g in range(group) for h in range(SWA_HEADS)],
                         axis=0).astype(BF16)
    hr = SWA_HEADS * R
    s_prev = jnp.concatenate([_dot(qs[hr * g:hr * (g + 1)], ckt_ref[g].astype(BF16)) for g in range(group)],
                             axis=0) + tab_prev[...]
    s_cur = _dot_nt(qs, k_new.astype(BF16)) + tab_cur[...]
    sink64 = jnp.concatenate([jnp.full((R, 1), sinks_ref[h], F32) for h in range(SWA_HEADS)], axis=0)
    sink = jnp.concatenate([sink64] * group, axis=0)
    m = jnp.maximum(jnp.maximum(jnp.max(s_prev, axis=1, keepdims=True), jnp.max(s_cur, axis=1, keepdims=True)),
                    sink)
    p_prev = jnp.exp(s_prev - m)
    p_cur = jnp.exp(s_cur - m)
    norm = 1.0 / (jnp.sum(p_prev, axis=1, keepdims=True) + jnp.sum(p_cur, axis=1, keepdims=True)
                  + jnp.exp(sink - m))
    pb = p_prev.astype(BF16)
    o = jnp.concatenate([_dot_nt(pb[hr * g:hr * (g + 1)], cvt_ref[g].astype(BF16)) for g in range(group)],
                        axis=0)
    o = (o + _dot(p_cur.astype(BF16), v_new.astype(BF16))) * norm
    seq_tiles = []
    for g in range(group):
        blocks = []
        for j in range(SWA_HEADS // 2):
            pair = []
            for h in (2 * j, 2 * j + 1):
                oh = o[hr * g + R * h:hr * g + R * (h + 1)]
                if h % 2 != h // (SWA_HEADS // 2):
                    oh = pltpu.roll(oh, SWA_D, 1)
                pair.append(oh)
            blocks.append(jnp.where(upper, pair[1], pair[0]))
        seq_tiles.append(jnp.concatenate(blocks, axis=1))
    osw_ref[...] = compact(jnp.concatenate(seq_tiles, axis=0))


def mixer_sample(z, layer, c0_all, n0_rows, m0_rows, s0_all, ckt_all, cvt_all, gb, inv, sinks, rb, prev):
    depth, nb = c0_all.shape[:2]
    group = SAMPLE_GROUP
    R = SEQ_PAD
    NR = group * R
    first_layer = prev is None
    assert first_layer == (layer == 0)

    def const(shape):
        return pl.BlockSpec(shape, lambda i: (0,) * len(shape))

    def rows(w):
        return pl.BlockSpec((NR, w), lambda i: (i, 0))

    def tokens(w):
        return pl.BlockSpec((group * N_NEW, w), lambda i: (i, 0))

    def layered_in(shape):
        return pl.BlockSpec((None, group) + shape, lambda i: (layer, i) + (0,) * len(shape))

    def layered_out(shape):
        if first_layer:
            return pl.BlockSpec((depth, group) + shape, lambda i: (0, i) + (0,) * len(shape))
        return layered_in(shape)

    smem = pl.BlockSpec(memory_space=pltpu.SMEM)
    st = (N_HEADS, HEAD_D, HEAD_D)
    buf = (LANES, CHUNK)
    n_in = 11
    if first_layer:
        extra_specs, extra_args, aliases = [], [], {}
    else:
        extra_specs = [pl.BlockSpec(memory_space=pl.ANY)] * 4
        extra_args = list(prev)
        aliases = {n_in + i: 5 + i for i in range(4)}
    return pl.pallas_call(
        functools.partial(_sample_mixers_kernel, group=group, first_layer=first_layer),
        grid=(nb // group,),
        in_specs=[tokens(Z_HALF),
                  layered_in(st), rows(512), rows(LANES), layered_in(st),
                  layered_in(buf), layered_in(buf),
                  const((1, LANES)), const((1, LANES)), smem, smem] + extra_specs,
        out_specs=[tokens(512), tokens(512), tokens(512), rows(512), rows(LANES),
                   layered_out(st), layered_out(st), layered_out(buf), layered_out(buf)],
        out_shape=[jax.ShapeDtypeStruct((nb * N_NEW, 512), F32)] * 3
        + [jax.ShapeDtypeStruct((nb * R, 512), F32), jax.ShapeDtypeStruct((nb * R, LANES), F32),
           jax.ShapeDtypeStruct((depth, nb) + st, F32), jax.ShapeDtypeStruct((depth, nb) + st, F32),
           jax.ShapeDtypeStruct((depth, nb) + buf, F32), jax.ShapeDtypeStruct((depth, nb) + buf, F32)],
        scratch_shapes=[pltpu.VMEM((group * SWA_HEADS * R, LANES), F32),
                        pltpu.VMEM((group * SWA_HEADS * R, LANES), F32)],
        input_output_aliases=aliases,
        compiler_params=_params(1),
        name="mixer_sample",
    )(z, c0_all, n0_rows, m0_rows, s0_all, ckt_all, cvt_all, gb, inv, sinks, rb, *extra_args)


def _mixer_post_kernel(hml_ref, osw_ref, ort_ref, mlo_ref, rtg_ref, g0_ref, g1_ref, g2_ref, x_ref,
                       mlg_ref, retg_ref, ng_ref, wml_ref, wsw_ref, wrt_ref, wout_ref, o_ref):
    o_ref[...] = _post_math(hml_ref[...], osw_ref[...], ort_ref[...], mlo_ref[...], rtg_ref[...],
                            g0_ref[...], g1_ref[...], g2_ref[...], x_ref[...],
                            mlg_ref[...], retg_ref[...], ng_ref[...],
                            wml_ref[...], wsw_ref[...], wrt_ref[...], wout_ref[...])


def mixer_post(hml, osw, ort, z, x, mlg, retg, ng, layer, wml, wsw, wrt, wout, tm, name):
    m = x.shape[0]

    def tok(w, cb=0):
        return pl.BlockSpec((tm, w), lambda i, cb=cb: (i, cb))

    def const(shape):
        return pl.BlockSpec(shape, lambda i: (0,) * len(shape))

    return pl.pallas_call(
        _mixer_post_kernel,
        grid=(m // tm,),
        in_specs=[tok(512), tok(512), tok(512), tok(512, ZC_ML_O), tok(512, ZC_RT_G),
                  tok(1024, ZC_GATES), tok(1024, ZC_GATES + 1), tok(1024, ZC_GATES + 2), tok(D_MODEL),
                  const((1, 512)), const((1, 512)), const((1, D_MODEL)),
                  _layer_weight(wml, layer), _layer_weight(wsw, layer), _layer_weight(wrt, layer),
                  _layer_weight(wout, layer)],
        out_specs=tok(D_MODEL),
        out_shape=jax.ShapeDtypeStruct((m, D_MODEL), F32),
        compiler_params=_params(1),
        name=name,
    )(hml, osw, ort, z, z, z, z, z, x, mlg, retg, ng, wml, wsw, wrt, wout)


def _xattn_head(q, k, v, mask):

    s = _dot_nt(jnp.where(mask, q, 0.0).astype(BF16), k) * (X_D ** -0.5)
    m = jnp.max(s, axis=1, keepdims=True)
    p = jnp.exp(s - m)
    p = p / jnp.sum(p, axis=1, keepdims=True)
    return jnp.where(mask, _dot(p.astype(BF16), v), 0.0)


def _xattn_ffn_prompt_kernel(x_ref, gxi_ref, gxo_ref, wcq_ref, kv_ref, wco_ref, gfi_ref, gfo_ref, wgu_ref, wd_ref,
                             o_ref, mid_scr):
    @pl.when(pl.program_id(0) == 0)
    def _():
        mid_scr[...] = jnp.zeros(mid_scr.shape, F32)

    x_mid = mid_scr[...]
    x = x_ref[...]
    lane = lax.broadcasted_iota(jnp.int32, (1, X_HEADS * X_D), 1)
    masks = [(lane >= X_D * h) & (lane < X_D * (h + 1)) for h in range(X_HEADS)]
    fc = D_FF // FFN_SPLIT

    u = _rms(x_mid, gfi_ref[...]).astype(BF16)
    q = _dot(_rms(x, gxi_ref[...]).astype(BF16), wcq_ref[...])
    k = kv_ref[:, 0:X_HEADS * X_D].astype(BF16)
    v = kv_ref[:, X_HEADS * X_D:2 * X_HEADS * X_D].astype(BF16)
    att = jnp.zeros(q.shape, F32)
    acc = None
    for c in range(FFN_SPLIT):
        g = _dot(u, wgu_ref[:, fc * c:fc * (c + 1)])
        up = _dot(u, wgu_ref[:, D_FF + fc * c:D_FF + fc * (c + 1)])
        for h in range(X_HEADS * c // FFN_SPLIT, X_HEADS * (c + 1) // FFN_SPLIT):
            att = att + _xattn_head(q, k, v, masks[h])
        hid = (g * _sigmoid(g) * up).astype(BF16)
        part = _dot(hid, wd_ref[fc * c:fc * (c + 1), :])
        acc = part if acc is None else acc + part
    o_ref[...] = x_mid + _rms(acc, gfo_ref[...])
    mid_scr[...] = x + _rms(_dot(att.astype(BF16), wco_ref[...]), gxo_ref[...])


def xattn_ffn_prompt(x, gxi, gxo, gfi, gfo, layer, wcq, kv, wco, wgu, wd, tm):
    m = x.shape[0]
    n = m // tm

    def const(shape):
        return pl.BlockSpec(shape, lambda s: (0,) * len(shape))

    gain = const((1, D_MODEL))
    return pl.pallas_call(
        _xattn_ffn_prompt_kernel,
        grid=(n + 1,),
        in_specs=[pl.BlockSpec((tm, D_MODEL), lambda s: (jnp.minimum(s, n - 1), 0)),
                  gain, gain, _layer_weight(wcq, layer), const(kv.shape), _layer_weight(wco, layer),
                  gain, gain, _layer_weight(wgu, layer), _layer_weight(wd, layer)],
        out_specs=pl.BlockSpec((tm, D_MODEL), lambda s: (jnp.maximum(s - 1, 0), 0)),
        out_shape=jax.ShapeDtypeStruct((m, D_MODEL), F32),
        scratch_shapes=[pltpu.VMEM((tm, D_MODEL), F32)],
        compiler_params=_params(1),
        name="xattn_ffn_prompt",
    )(x, gxi, gxo, wcq, kv, wco, gfi, gfo, wgu, wd)


def _xattn_sample_kernel(x_ref, gin_ref, gout_ref, wcq_ref, kt_ref, vt_ref, wco_ref, o_ref, q_scr, a_scr,
                         *, group):
    R = SEQ_PAD
    x = x_ref[...]
    q_scr[...] = _dot(_rms(x, gin_ref[...]).astype(BF16), wcq_ref[...])
    lane = lax.broadcasted_iota(jnp.int32, (1, X_HEADS * X_D), 1)
    masks = [(lane >= X_D * h) & (lane < X_D * (h + 1)) for h in range(X_HEADS)]
    second = (lax.broadcasted_iota(jnp.int32, (X_HEADS * R, 1), 0) & (R - 1)) >= N_NEW

    def body(pair, carry):
        rows = pl.ds(pl.multiple_of(pair * R, R), R)
        q = q_scr[rows, :]
        qs = jnp.concatenate([jnp.where(mk, q, 0.0) for mk in masks], axis=0).astype(BF16)
        s = jnp.where(second, _dot(qs, kt_ref[2 * pair + 1].astype(BF16)), _dot(qs, kt_ref[2 * pair].astype(BF16)))
        s = s * (X_D ** -0.5)
        m = jnp.max(s, axis=1, keepdims=True)
        p = jnp.exp(s - m)
        p = (p / jnp.sum(p, axis=1, keepdims=True)).astype(BF16)
        o = jnp.where(second, _dot_nt(p, vt_ref[2 * pair + 1].astype(BF16)), _dot_nt(p, vt_ref[2 * pair].astype(BF16)))
        acc = jnp.zeros((R, X_HEADS * X_D), F32)
        for h in range(X_HEADS):
            acc = acc + jnp.where(masks[h], o[R * h:R * (h + 1)], 0.0)
        a_scr[rows, :] = acc
        return carry

    lax.fori_loop(0, group // 2, body, 0, unroll=2)
    o_ref[...] = x + _rms(_dot(a_scr[...].astype(BF16), wco_ref[...]), gout_ref[...])


def xattn_sample(x, gin, gout, wcq, layer, mkt, mvt, wco, group=16):
    nb = mkt.shape[1]
    R = N_NEW

    def const(shape):
        return pl.BlockSpec(shape, lambda i: (0,) * len(shape))

    tok = pl.BlockSpec((group * R, D_MODEL), lambda i: (i, 0))
    mem = pl.BlockSpec((None, group, X_HEADS * X_D, N_MEM), lambda i: (layer, i, 0, 0))
    return pl.pallas_call(
        functools.partial(_xattn_sample_kernel, group=group),
        grid=(nb // group,),
        in_specs=[tok, const((1, D_MODEL)), const((1, D_MODEL)), _layer_weight(wcq, layer), mem, mem,
                  _layer_weight(wco, layer)],
        out_specs=tok,
        out_shape=jax.ShapeDtypeStruct((nb * R, D_MODEL), F32),
        scratch_shapes=[pltpu.VMEM((group * R, X_HEADS * X_D), F32),
                        pltpu.VMEM((group * R, X_HEADS * X_D), F32)],
        compiler_params=_params(1),
        name="xattn_sample",
    )(x, gin, gout, wcq, mkt, mvt, wco)


FFN_SPLIT = 1


def _ffn_kernel(x_ref, gin_ref, gout_ref, wgu_ref, wd_ref, o_ref):
    x = x_ref[...]
    u = _rms(x, gin_ref[...]).astype(BF16)
    fc = D_FF // FFN_SPLIT
    acc = None
    for c in range(FFN_SPLIT):
        g = _dot(u, wgu_ref[:, fc * c:fc * (c + 1)])
        up = _dot(u, wgu_ref[:, D_FF + fc * c:D_FF + fc * (c + 1)])
        h = (g * _sigmoid(g) * up).astype(BF16)
        part = _dot(h, wd_ref[fc * c:fc * (c + 1), :])
        acc = part if acc is None else acc + part
    o_ref[...] = x + _rms(acc, gout_ref[...])


def ffn(x, gin, gout, layer, wgu, wd, tm, name):
    m = x.shape[0]

    def const(shape):
        return pl.BlockSpec(shape, lambda i: (0,) * len(shape))

    tok = pl.BlockSpec((tm, D_MODEL), lambda i: (i, 0))
    return pl.pallas_call(
        _ffn_kernel,
        grid=(m // tm,),
        in_specs=[tok, const((1, D_MODEL)), const((1, D_MODEL)),
                  _layer_weight(wgu, layer), _layer_weight(wd, layer)],
        out_specs=tok,
        out_shape=jax.ShapeDtypeStruct((m, D_MODEL), F32),
        compiler_params=_params(1),
        name=name,
    )(x, gin, gout, wgu, wd)


def _reorder_w_in_t(w):
    wt = jnp.swapaxes(w, 1, 2)
    sizes = (512, 512, 512, 4, 4, 512, 512, 128, 128, 512, 512, 512, 512, 3072)
    offs = np.concatenate([[0], np.cumsum(sizes)])
    (ml_q, ml_k, ml_v, ml_i, ml_f, ml_o, sw_q, sw_k, sw_v, rt_q, rt_k, rt_v, rt_g, gates) = [
        wt[:, int(offs[i]):int(offs[i + 1])] for i in range(len(sizes))]
    pad = jnp.zeros((w.shape[0], 512 - 128 - 128 - 8, w.shape[1]), w.dtype)
    out = jnp.concatenate([ml_q, ml_k, ml_v, sw_q, sw_k, sw_v, ml_i, ml_f, pad, rt_q, rt_k, rt_v,
                           ml_o, rt_g, gates], axis=1)
    assert out.shape[1] == Z_COLS
    return out.astype(BF16)


def _row(v):
    return v.reshape(1, -1).astype(F32)


def _decoder_layer(x, z_fn, mixer_fn, xattn_ffn_fn):
    z = z_fn(x)
    x, states = mixer_fn(z, x)
    return xattn_ffn_fn(x), z, states


def kernel(x_prompt, x_sample, mem_prompt, state_mlstm_C, state_mlstm_n, state_mlstm_m, state_ret_S,
           cache_swa_k, cache_swa_v, cache_mem_k, cache_mem_v, norm_g, w_in, ml_gate_bias, ml_head_g,
           ret_head_g, swa_sinks, rel_bias, w_br_ml, w_br_swa, w_br_ret, w_out, w_cq, w_mkv, w_co,
           w_gu, w_down):
    depth = w_in.shape[0]
    bp, t, d = x_prompt.shape
    assert bp == 1 and d == D_MODEL and t % 1024 == 0
    nb, n_new, _ = x_sample.shape
    assert n_new == N_NEW
    R = SEQ_PAD

    xp = x_prompt.reshape(t, d)
    xs = x_sample.reshape(nb * n_new, d)
    mem = mem_prompt.reshape(N_MEM, d)
    half = HEAD_D // 2
    inv = ROPE_BASE ** (-jnp.arange(half, dtype=F32) / half)
    inv = jnp.concatenate([inv, inv]).reshape(1, LANES)
    rb = rel_bias.astype(F32).reshape(-1)
    mem_kt = jnp.transpose(cache_mem_k, (0, 1, 3, 4, 2)).reshape(depth, nb, X_HEADS * X_D, N_MEM)
    mem_vt = jnp.transpose(cache_mem_v, (0, 1, 3, 4, 2)).reshape(depth, nb, X_HEADS * X_D, N_MEM)

    win_t = _reorder_w_in_t(w_in)
    wml, wsw, wrt, wout = (w.astype(BF16) for w in (w_br_ml, w_br_swa, w_br_ret, w_out))
    wcq, wmkv, wco, wgu, wd = (w.astype(BF16) for w in (w_cq, w_mkv, w_co, w_gu, w_down))

    outs_p = {k: [] for k in ("C", "n", "m", "S", "k", "v", "mk", "mv")}
    outs_s = {k: [] for k in ("n", "m")}
    sample_layered = None
    swa_kt = jnp.transpose(cache_swa_k, (0, 1, 3, 4, 2)).reshape(depth, nb, LANES, CHUNK)
    swa_vt = jnp.transpose(cache_swa_v, (0, 1, 3, 4, 2)).reshape(depth, nb, LANES, CHUNK)
    for l in range(depth):
        ng = [_row(norm_g[l, i]) for i in range(7)]
        mlg, retg = _row(ml_head_g[l]), _row(ret_head_g[l])
        gb = jnp.concatenate([ml_gate_bias[l, 0], ml_gate_bias[l, 1],
                              jnp.zeros((LANES - 2 * N_HEADS,), F32)]).reshape(1, LANES).astype(F32)
        gb_rows = jnp.broadcast_to(ml_gate_bias[l].astype(F32).reshape(2 * N_HEADS, 1), (2 * N_HEADS, LANES))
        sinks = swa_sinks[l].astype(F32)

        kv = norm_matmul(mem, ng[6], wmkv, l, tm=N_MEM, tn=2 * X_HEADS * X_D, name="memory_kv")

        def seq_p(z, x):
            x1, c_, n_, m_, s_ = mixer_prompt(z, x, gb_rows, inv, sinks, rb, mlg, retg, ng[1], l,
                                              wml, wsw, wrt, wout)
            return x1, (c_, n_, m_, s_)

        xp, zp, (c_, n_, m_, s_) = _decoder_layer(
            xp,
            lambda x: norm_matmul(x, ng[0], win_t, l, tm=min(IN_PROJ_TM, t), tn=1024, name="in_proj_prompt",
                                  w_transposed=True),
            seq_p,
            lambda x: xattn_ffn_prompt(x, ng[2], ng[3], ng[4], ng[5], l, wcq, kv, wco, wgu, wd, tm=512))
        outs_p["C"].append(c_.reshape(1, N_HEADS, HEAD_D, HEAD_D))
        outs_p["n"].append(n_.reshape(1, N_HEADS, HEAD_D))
        outs_p["m"].append(m_[:, :N_HEADS])
        outs_p["S"].append(s_.reshape(1, N_HEADS, HEAD_D, HEAD_D))
        misc_last = zp[t - CHUNK:, 512 * ZC_MISC:512 * ZC_MISC + 2 * LANES]
        outs_p["k"].append(misc_last[:, :LANES].reshape(1, CHUNK, 2, SWA_D))
        outs_p["v"].append(misc_last[:, LANES:].reshape(1, CHUNK, 2, SWA_D))
        outs_p["mk"].append(kv[:, :X_HEADS * X_D].reshape(1, N_MEM, X_HEADS, X_D))
        outs_p["mv"].append(kv[:, X_HEADS * X_D:].reshape(1, N_MEM, X_HEADS, X_D))

        n0_rows = jnp.repeat(state_mlstm_n[l].astype(F32).reshape(nb, N_HEADS * HEAD_D), R, axis=0)
        m0_rows = jnp.repeat(jnp.pad(state_mlstm_m[l].astype(F32), ((0, 0), (0, LANES - N_HEADS))), R, axis=0)

        def seq_s(z, x):
            hml, osw, ort, n_, m_, *layered = mixer_sample(
                z, l, state_mlstm_C, n0_rows, m0_rows, state_ret_S, swa_kt, swa_vt, gb, inv, sinks, rb,
                sample_layered)
            x1 = mixer_post(hml, osw, ort, z, x, mlg, retg, ng[1], l, wml, wsw, wrt, wout,
                            tm=min(256, x.shape[0]), name="mixer_post_sample")
            return x1, (n_, m_, layered)

        xs, zs, (n_, m_, sample_layered) = _decoder_layer(
            xs,
            lambda x: norm_matmul(x, ng[0], win_t, l, tm=min(1024, nb * n_new), tn=1024, name="in_proj_sample",
                                  w_transposed=True),
            seq_s,
            lambda x: ffn(xattn_sample(x, ng[2], ng[3], wcq, l, mem_kt, mem_vt, wco),
                          ng[4], ng[5], l, wgu, wd, tm=min(512, nb * n_new), name="ffn_sample"))
        outs_s["n"].append(n_.reshape(nb, R, N_HEADS, HEAD_D)[:, R - 1])
        outs_s["m"].append(m_.reshape(nb, R, LANES)[:, n_new - 1, :N_HEADS])

    sample_c, sample_s, swa_kt_new, swa_vt_new = sample_layered
    s_swa_k = jnp.transpose(swa_kt_new.reshape(depth, nb, 2, SWA_D, CHUNK), (0, 1, 4, 2, 3))
    s_swa_v = jnp.transpose(swa_vt_new.reshape(depth, nb, 2, SWA_D, CHUNK), (0, 1, 4, 2, 3))
    y_p = xp.reshape(1, t, d)
    y_s = xs.reshape(nb, n_new, d)
    st = lambda d_, k: jnp.stack(d_[k])
    return (y_p, y_s,
            st(outs_p, "C"), st(outs_p, "n"), st(outs_p, "m"), st(outs_p, "S"),
            st(outs_p, "k"), st(outs_p, "v"), st(outs_p, "mk"), st(outs_p, "mv"),
            sample_c, st(outs_s, "n"), st(outs_s, "m"), sample_s, s_swa_k, s_swa_v)
```

```python
import functools
import math

import numpy as np
import jax
import jax.numpy as jnp
from jax import lax
from jax.experimental import pallas as pl
from jax.experimental.pallas import tpu as pltpu

F32 = jnp.float32
BF16 = jnp.bfloat16

D_MODEL = 1024
EPS = 1e-6
PAST_LEN = 16384
HEAD_D = 128
N_HEADS = 4
CHUNK = 128
SWA_HEADS = 8
SWA_D = 64
N_BUCKETS = 32
MAX_DISTANCE = 128
ROPE_BASE = 10000.0
N_MEM = 256
X_HEADS = 4
X_D = 64
D_FF = 2816
SEQ_PAD = 8
N_NEW = 4
LANES = 128
IN_PROJ_TM = 2048
MXU_MIN_ROWS = 16
VMEM_LIMIT = 48 * 1024 * 1024

ZC_ML_Q, ZC_ML_K, ZC_ML_V, ZC_SW_Q, ZC_MISC, ZC_RT_Q, ZC_RT_K, ZC_RT_V, ZC_ML_O, ZC_RT_G = range(10)
ZC_GATES = 5
Z_COLS = 8192
Z_HALF = Z_COLS // 2
MISC_K, MISC_V, MISC_IF = 0, 128, 256

LOG2E = math.log2(math.e)
LN2 = math.log(2.0)
LOG_GAMMA = tuple(float(v) for v in np.log1p(-np.exp2(-5.0 - np.arange(N_HEADS, dtype=np.float32))))

SWA_SAME = (0, 2, 5, 7)
SWA_SWAP = (1, 3, 4, 6)


def _t5_bucket_bounds():
    n = np.arange(CHUNK)
    max_exact = N_BUCKETS // 2
    nf = np.maximum(n, 1).astype(np.float32)
    large = max_exact + (np.log(nf / np.float32(max_exact)) / np.float32(math.log(MAX_DISTANCE / max_exact))
                         * np.float32(N_BUCKETS - max_exact)).astype(np.int32)
    large = np.minimum(large, N_BUCKETS - 1)
    b = np.where(n < max_exact, n, large)
    assert np.all(np.diff(b) >= 0)
    runs = []
    for d in range(CHUNK):
        if runs and runs[-1][1] == int(b[d]):
            runs[-1][0] = d + 1
        else:
            runs.append([d + 1, int(b[d])])
    return tuple((hi, bk) for hi, bk in runs)


BUCKET_RUNS = _t5_bucket_bounds()


def _dot(a, b):
    return jnp.dot(a, b, preferred_element_type=F32)


def _dot_nt(a, b):
    return lax.dot_general(a, b, (((1,), (1,)), ((), ())), preferred_element_type=F32)


def _dot_tn(a, b):
    return lax.dot_general(a, b, (((0,), (0,)), ((), ())), preferred_element_type=F32)


def _rms(x, g):
    return x * lax.rsqrt(jnp.mean(x * x, axis=-1, keepdims=True) + EPS) * g


def _sigmoid(x):
    return 1.0 / (1.0 + jnp.exp2(x * (-LOG2E)))


def _log_sigmoid(x):
    return jnp.minimum(x, 0.0) - jnp.log1p(jnp.exp(-jnp.abs(x)))


def _layer_weight(w_all, layer):
    shape = w_all.shape[1:]
    return pl.BlockSpec((None,) + shape, lambda *_: (layer,) + (0,) * len(shape), pipeline_mode=pl.Buffered(1))


def _params(n_grid):
    return pltpu.CompilerParams(dimension_semantics=("arbitrary",) * n_grid, vmem_limit_bytes=VMEM_LIMIT)


def _norm_matmul_kernel(x_ref, g_ref, w_ref, o_ref, u_ref, *, w_transposed):
    @pl.when(pl.program_id(1) == 0)
    def _():
        u_ref[...] = _rms(x_ref[...], g_ref[...]).astype(BF16)

    o_ref[...] = _dot_nt(u_ref[...], w_ref[...]) if w_transposed else _dot(u_ref[...], w_ref[...])


def norm_matmul(x, g, w_all, layer, tm, tn, name, w_transposed=False):
    m, k = x.shape
    n = w_all.shape[1] if w_transposed else w_all.shape[2]
    if w_transposed:
        w_spec = pl.BlockSpec((None, tn, k), lambda i, j: (layer, j, 0))
    else:
        w_spec = pl.BlockSpec((None, k, tn), lambda i, j: (layer, 0, j))
    return pl.pallas_call(
        functools.partial(_norm_matmul_kernel, w_transposed=w_transposed),
        grid=(m // tm, n // tn),
        in_specs=[pl.BlockSpec((tm, k), lambda i, j: (i, 0)),
                  pl.BlockSpec((1, k), lambda i, j: (0, 0)),
                  w_spec],
        out_specs=pl.BlockSpec((tm, tn), lambda i, j: (i, j)),
        out_shape=jax.ShapeDtypeStruct((m, n), F32),
        scratch_shapes=[pltpu.VMEM((tm, k), BF16)],
        compiler_params=_params(2),
        name=name,
    )(x, g, w_all)


def _transpose_bf16(x, eye):
    return _dot_nt(eye, x).astype(BF16)


def _build_swa_table_t(tab_ref, rb_ref):
    L = CHUNK
    srow = lax.broadcasted_iota(jnp.int32, (2 * L, L), 0)
    lcol = lax.broadcasted_iota(jnp.int32, (2 * L, L), 1)
    delta = lcol + L - srow
    valid = (delta >= 0) & (delta < L)
    for var, heads in enumerate((SWA_SAME, SWA_SWAP)):
        for i, h in enumerate(heads):
            val = jnp.full((2 * L, L), rb_ref[BUCKET_RUNS[-1][1] * SWA_HEADS + h], F32)
            for hi, bk in reversed(BUCKET_RUNS[:-1]):
                val = jnp.where(delta < hi, rb_ref[bk * SWA_HEADS + h], val)
            tab_ref[var, :, i * L:(i + 1) * L] = jnp.where(valid, val, -jnp.inf)
            tab_ref[2 + var, :, i * L:(i + 1) * L] = jnp.where(valid & (srow >= L), val, -jnp.inf)


def _swa_scores_t(q, k_same, k_swap, tab_ref, first):
    lane = lax.broadcasted_iota(jnp.int32, (1, LANES), 1)
    upper = lane >= SWA_D
    q = q * (SWA_D ** -0.5)
    ks = (k_same[...], k_swap[...])
    scores = []
    for var, heads in enumerate((SWA_SAME, SWA_SWAP)):
        qm = []
        for h in heads:
            blk = q[:, LANES * (h // 2):LANES * (h // 2 + 1)]
            qm.append(jnp.where(upper if h % 2 == 1 else jnp.logical_not(upper), blk, 0.0))
        qs = jnp.concatenate(qm, axis=0).astype(BF16)
        scores.append(_dot_nt(ks[var], qs) + tab_ref[var + 2 * first])
    return scores


def _swa_softmax_t(scores, sinks_ref):
    probs = []
    for s_t, heads in zip(scores, (SWA_SAME, SWA_SWAP)):
        sink = jnp.concatenate([jnp.full((1, CHUNK), sinks_ref[h], F32) for h in heads], axis=1)
        m = jnp.maximum(jnp.max(s_t, axis=0, keepdims=True), sink)
        p = jnp.exp(s_t - m)
        norm = 1.0 / (jnp.sum(p, axis=0, keepdims=True) + jnp.exp(sink - m))
        probs.append((p.astype(BF16), norm))
    return probs


def _swa_values_t(probs, vt_buf):
    vt = vt_buf[...]
    vts = (vt, jnp.concatenate([vt[SWA_D:], vt[:SWA_D]], axis=0))
    return [_dot(vts[var], p) * norm for var, (p, norm) in enumerate(probs)]


def _swa_output_t(values):
    L = CHUNK
    lane = lax.broadcasted_iota(jnp.int32, (1, LANES), 1)
    upper = lane >= SWA_D
    outs = [None] * SWA_HEADS
    for o_t, heads in zip(values, (SWA_SAME, SWA_SWAP)):
        for i, h in enumerate(heads):
            outs[h] = o_t[:, i * L:(i + 1) * L].T
    blocks = [jnp.where(upper, outs[2 * j + 1], outs[2 * j]) for j in range(SWA_HEADS // 2)]
    return jnp.concatenate(blocks, axis=1)


def _rotary_tables(pos, inv_ref):
    lane = lax.broadcasted_iota(jnp.int32, (1, LANES), 1)
    ang = pos * inv_ref[...]
    sin = jnp.sin(ang)
    return jnp.cos(ang), jnp.where(lane < HEAD_D // 2, -sin, sin)


def _rotate(x, cos, sin_signed):
    return x * cos + pltpu.roll(x, HEAD_D // 2, 1) * sin_signed


def _gate_slab(raw, gb_ref):
    lane = lax.broadcasted_iota(jnp.int32, (1, LANES), 1)
    x = raw + gb_ref[...]
    return jnp.where((lane >= N_HEADS) & (lane < 2 * N_HEADS), _log_sigmoid(x), x)


def _head_rms(h, gain):
    parts = []
    for i in range(N_HEADS):
        blk = h[:, HEAD_D * i:HEAD_D * (i + 1)]
        parts.append(blk * lax.rsqrt(jnp.mean(blk * blk, axis=-1, keepdims=True) + EPS))
    return jnp.concatenate(parts, axis=-1) * gain


def _rms_over_rows(h_t):
    return h_t * lax.rsqrt(jnp.mean(h_t * h_t, axis=0, keepdims=True) + EPS)


def _post_math(hml, osw, ort, mlo, rtg, g0, g1, g2, x, mlg, retg, ng, wml, wsw, wrt, wout, head_normed=False):
    if not head_normed:
        hml = _head_rms(hml, mlg)
        ort = _head_rms(ort, retg)
    hm = hml * _sigmoid(mlo)
    y_ml = _dot(hm.astype(BF16), wml)
    y_sw = _dot(osw.astype(BF16), wsw)
    rt = ort * (rtg * _sigmoid(rtg))
    y_rt = _dot(rt.astype(BF16), wrt)
    merged = _sigmoid(g0) * y_ml + _sigmoid(g1) * y_sw + _sigmoid(g2) * y_rt
    return x + _rms(_dot(merged.astype(BF16), wout), ng)


def _z_views(z_half_ref, first_col_block):
    return {first_col_block + i: z_half_ref.at[:, 512 * i:512 * (i + 1)] for i in range(Z_HALF // 512)}


def _mixer_seq_prompt_kernel(zmix_ref, ztail_ref, x_ref,
                             gbr_ref, inv_ref, sinks_ref, rb_ref,
                             mlg_ref, retg_ref, ng_ref, wml_ref, wsw_ref, wrt_ref, wout_ref,
                             x1_ref, c_ref, n_ref, m_ref, s_ref,
                             st_scr, k_same, k_swap, vt_buf, tab_t, cos_l, sin_l, dec_in_t, eye_ref,
                             hml_s, osw_s, ort_s):
    zm = _z_views(zmix_ref, 0)
    mq_ref, mk_ref, mv_ref, swq_ref, misc_ref = (zm[i] for i in (ZC_ML_Q, ZC_ML_K, ZC_ML_V, ZC_SW_Q, ZC_MISC))
    rq_ref, rk_ref, rv_ref = (zm[i] for i in (ZC_RT_Q, ZC_RT_K, ZC_RT_V))
    zt = _z_views(ztail_ref, Z_HALF // 512)
    mlo_ref, rtg_ref = zt[ZC_ML_O], zt[ZC_RT_G]
    g0_ref, g1_ref, g2_ref = (ztail_ref.at[:, 1024 * (ZC_GATES + i) - Z_HALF:1024 * (ZC_GATES + i + 1) - Z_HALF]
                              for i in range(3))
    c = pl.program_id(0)
    active = c < pl.num_programs(0) - 1
    L = CHUNK
    lane = lax.broadcasted_iota(jnp.int32, (1, LANES), 1)
    row = lax.broadcasted_iota(jnp.int32, (L, L), 0)
    col = lax.broadcasted_iota(jnp.int32, (L, L), 1)
    causal_t = row <= col

    @pl.when(c == 0)
    def _init():
        c_ref[...] = jnp.zeros(c_ref.shape, F32)
        n_ref[...] = jnp.zeros(n_ref.shape, F32)
        m_ref[...] = jnp.zeros(m_ref.shape, F32)
        s_ref[...] = jnp.zeros(s_ref.shape, F32)
        hml_s[...] = jnp.zeros(hml_s.shape, F32)
        osw_s[...] = jnp.zeros(osw_s.shape, F32)
        ort_s[...] = jnp.zeros(ort_s.shape, F32)
        st_scr[...] = jnp.zeros(st_scr.shape, F32)
        k_same[...] = jnp.zeros(k_same.shape, BF16)
        k_swap[...] = jnp.zeros(k_swap.shape, BF16)
        vt_buf[...] = jnp.zeros(vt_buf.shape, BF16)
        _build_swa_table_t(tab_t, rb_ref)
        ang = row.astype(F32) * inv_ref[...]
        cos_l[...] = jnp.cos(ang)
        sin_l[...] = jnp.sin(ang)
        rel_t = (col - row).astype(F32)
        for h in range(N_HEADS):
            dec_in_t[h] = jnp.where(causal_t, jnp.exp(LOG_GAMMA[h] * rel_t), 0.0)
        eye_ref[...] = jnp.where(row == col, 1.0, 0.0).astype(BF16)

    eye = eye_ref[...]
    heads = range(N_HEADS)
    hsl = [slice(HEAD_D * h, HEAD_D * (h + 1)) for h in heads]
    sub8 = lax.broadcasted_iota(jnp.int32, (2 * N_HEADS, 1), 0)
    hml_prev, osw_prev, ort_prev = hml_s[...], osw_s[...], ort_s[...]

    raw = misc_ref[:, MISC_IF:MISC_IF + LANES].T[0:2 * N_HEADS] + gbr_ref[...]
    gates = jnp.where(sub8 >= N_HEADS, _log_sigmoid(raw), raw)
    cum = gates
    for sh in (1, 2, 4, 8, 16, 32, 64):
        cum = cum + jnp.where(lane >= sh, pltpu.roll(cum, sh, 1), 0.0)
    g8 = jnp.where(sub8 < N_HEADS, gates - pltpu.roll(cum, N_HEADS, 0), 0.0)
    g2_cols = jnp.concatenate([g8 * LOG2E, jnp.zeros((L - 2 * N_HEADS, L), F32)], axis=0).T

    ang0 = (c * L).astype(F32) * inv_ref[...]
    cos0 = jnp.cos(ang0)
    sin0 = jnp.sin(ang0)
    cos_t = cos_l[...]
    sin_t = sin_l[...]
    cos = cos0 * cos_t - sin0 * sin_t
    sin = sin0 * cos_t + cos0 * sin_t
    sin_signed = jnp.where(lane < HEAD_D // 2, -sin, sin)
    ml_ops, rt_ops = [], []
    for h in heads:
        qb = mq_ref[:, hsl[h]].astype(BF16)
        kb = (mk_ref[:, hsl[h]] * (HEAD_D ** -0.5)).astype(BF16)
        vt = _transpose_bf16(mv_ref[:, hsl[h]].astype(BF16), eye)
        ml_ops.append((qb, kb, vt, _dot_nt(kb, qb)))
    for h in heads:
        qr = _rotate(rq_ref[:, hsl[h]], cos, sin_signed).astype(BF16)
        kr = (_rotate(rk_ref[:, hsl[h]], cos, sin_signed) * (HEAD_D ** -0.5)).astype(BF16)
        vt = _transpose_bf16(rv_ref[:, hsl[h]].astype(BF16), eye)
        rt_ops.append((qr, kr, vt, _dot_nt(kr, qr)))
    k_new = misc_ref[:, MISC_K:MISC_K + LANES]
    k_same[0:L, :] = k_same[L:2 * L, :]
    k_swap[0:L, :] = k_swap[L:2 * L, :]
    k_same[L:2 * L, :] = k_new.astype(BF16)
    k_swap[L:2 * L, :] = pltpu.roll(k_new, SWA_D, 1).astype(BF16)
    vt_buf[:, 0:L] = vt_buf[:, L:2 * L]
    vt_buf[:, L:2 * L] = _transpose_bf16(misc_ref[:, MISC_V:MISC_V + LANES].astype(BF16), eye)
    swa_scores = _swa_scores_t(swq_ref[...], k_same, k_swap, tab_t, (c == 0).astype(jnp.int32))

    y_ml = _dot((hml_prev * _sigmoid(mlo_ref[...])).astype(BF16), wml_ref[...])

    m_all = m_ref[...]
    gate_ops = []
    for h in heads:
        m_prev = m_all[:, h:h + 1]
        gm2 = jnp.where(causal_t, g2_cols[:, h:h + 1], -jnp.inf)
        mx2 = jnp.maximum(jnp.max(gm2, axis=0, keepdims=True), m_prev * LOG2E)
        gate_ops.append((m_prev, mx2 * LN2, jnp.exp2(m_prev * LOG2E - mx2), jnp.exp2(gm2 - mx2)))

    y_sw = _dot(osw_prev.astype(BF16), wsw_ref[...])
    rtg = rtg_ref[...]
    y_rt = _dot((ort_prev * (rtg * _sigmoid(rtg))).astype(BF16), wrt_ref[...])

    l_row = lane.astype(F32)
    ml_out, rt_out = [], []
    for h in heads:
        qb, kb, vt, qk = ml_ops[h]
        m_prev, mx, inter, dm_t = gate_ops[h]
        s_t = qk * dm_t
        num_t = _dot(vt, s_t.astype(BF16)) + _dot_nt(c_ref[h].astype(BF16), qb) * inter
        nq = _dot_nt(jnp.broadcast_to(n_ref[h], (MXU_MIN_ROWS, HEAD_D)).astype(BF16), qb)[0:1, :]
        den = jnp.sum(s_t, axis=0, keepdims=True) + inter * nq
        m_t = cum[N_HEADS + h:N_HEADS + h + 1, :] + mx
        den = jnp.maximum(jnp.abs(den), jnp.exp(-m_t))
        ml_out.append((num_t * (1.0 / den), m_t))
    for h in heads:
        qr, kr, vt, qk = rt_ops[h]
        s_t = qk * dec_in_t[h]
        q_decay = jnp.exp(LOG_GAMMA[h] * (l_row + 1.0))
        rt_out.append(_dot(vt, s_t.astype(BF16)) + _dot_nt(st_scr[h].astype(BF16), qr) * q_decay)
    swa_values = _swa_values_t(_swa_softmax_t(swa_scores, sinks_ref), vt_buf)

    merged = _sigmoid(g0_ref[...]) * y_ml + _sigmoid(g1_ref[...]) * y_sw + _sigmoid(g2_ref[...]) * y_rt
    x1_ref[...] = x_ref[...] + _rms(_dot(merged.astype(BF16), wout_ref[...]), ng_ref[...])

    osw_s[...] = _swa_output_t(swa_values)
    for h in heads:
        hml_s[:, hsl[h]] = _rms_over_rows(ml_out[h][0]).T * mlg_ref[:, hsl[h]]
        ort_s[:, hsl[h]] = _rms_over_rows(rt_out[h]).T * retg_ref[:, hsl[h]]
    m_out = jnp.zeros((1, LANES), F32)
    for h in heads:
        qb, kb, vt, _ = ml_ops[h]
        m_prev = gate_ops[h][0]
        m_t = ml_out[h][1]
        m_new = m_t[:, L - 1:L]
        b_last = cum[N_HEADS + h:N_HEADS + h + 1, L - 1:L]
        decay = jnp.exp(b_last + m_prev - m_new)
        w_r = jnp.exp(g8[h:h + 1, :] + (b_last - m_new))
        c_old = c_ref[h]
        n_old = n_ref[h]
        c_new = decay * c_old + _dot((vt.astype(F32) * w_r).astype(BF16), kb)
        n_new = decay * n_old + _dot(jnp.broadcast_to(w_r, (MXU_MIN_ROWS, L)).astype(BF16), kb)[0:1, :]
        c_ref[h] = jnp.where(active, c_new, c_old)
        n_ref[h] = jnp.where(active, n_new, n_old)
        m_out = jnp.where(lane == h, m_new, m_out)
    m_ref[...] = jnp.where(active, m_out, m_all)
    for h in heads:
        qr, kr, vt, _ = rt_ops[h]
        lg = LOG_GAMMA[h]
        k_decay = jnp.exp(lg * (L - 1.0 - l_row))
        st_old = st_scr[h]
        st_new = math.exp(lg * L) * st_old + _dot((vt.astype(F32) * k_decay).astype(BF16), kr)
        st_scr[h] = jnp.where(active, st_new, st_old)

    @pl.when(c == pl.num_programs(0) - 1)
    def _emit_state():
        for h in heads:
            s_ref[h] = st_scr[h].T


def mixer_prompt(z, x, gb, inv, sinks, rb, mlg, retg, ng, layer, wml, wsw, wrt, wout):
    t = z.shape[0]
    L = CHUNK
    n = t // L

    z_mix = pl.BlockSpec((L, Z_HALF), lambda c: (jnp.minimum(c, n - 1), 0))

    def tail(w, cb=0):
        return pl.BlockSpec((L, w), lambda c, cb=cb: (jnp.maximum(c - 1, 0), cb))

    def const(shape):
        return pl.BlockSpec(shape, lambda c: (0,) * len(shape))

    def weight(w):
        return _layer_weight(w, layer)

    smem = pl.BlockSpec(memory_space=pltpu.SMEM)
    return pl.pallas_call(
        _mixer_seq_prompt_kernel,
        grid=(n + 1,),
        in_specs=[z_mix, tail(Z_HALF, 1), tail(D_MODEL),
                  const((2 * N_HEADS, LANES)), const((1, LANES)), smem, smem,
                  const((1, 512)), const((1, 512)), const((1, D_MODEL)),
                  weight(wml), weight(wsw), weight(wrt), weight(wout)],
        out_specs=[tail(D_MODEL),
                   const((N_HEADS, HEAD_D, HEAD_D)), const((N_HEADS, 1, HEAD_D)), const((1, LANES)),
                   const((N_HEADS, HEAD_D, HEAD_D))],
        out_shape=[jax.ShapeDtypeStruct((t, D_MODEL), F32),
                   jax.ShapeDtypeStruct((N_HEADS, HEAD_D, HEAD_D), F32),
                   jax.ShapeDtypeStruct((N_HEADS, 1, HEAD_D), F32),
                   jax.ShapeDtypeStruct((1, LANES), F32),
                   jax.ShapeDtypeStruct((N_HEADS, HEAD_D, HEAD_D), F32)],
        scratch_shapes=[pltpu.VMEM((N_HEADS, L, L), F32),
                        pltpu.VMEM((2 * L, LANES), BF16), pltpu.VMEM((2 * L, LANES), BF16),
                        pltpu.VMEM((LANES, 2 * L), BF16),
                        pltpu.VMEM((4, 2 * L, 4 * L), F32),
                        pltpu.VMEM((L, L), F32), pltpu.VMEM((L, L), F32),
                        pltpu.VMEM((N_HEADS, L, L), F32),
                        pltpu.VMEM((L, L), BF16),
                        pltpu.VMEM((L, 512), F32), pltpu.VMEM((L, 512), F32), pltpu.VMEM((L, 512), F32)],
        compiler_params=_params(1),
        name="mixer_prompt",
    )(z, z, x, gb, inv, sinks, rb, mlg, retg, ng, wml, wsw, wrt, wout)


SAMPLE_GROUP = 8


def _tile_bcast(x, l_idx, src):
    n_rows = x.shape[0]
    out = jnp.zeros_like(x)
    for j in range(SEQ_PAD):
        out = out + jnp.where(l_idx == j, pltpu.roll(x, (j - src) % n_rows, 0), 0.0)
    return out


def _tile_total(x, l_idx):
    for d in (1, 2, 4):
        x = x + jnp.where(l_idx >= d, pltpu.roll(x, d, 0), 0.0)
    return x


def _per_seq(fn, group):
    return jnp.concatenate([fn(g) for g in range(group)], axis=0)


def _build_sample_swa_tables(tab_prev, tab_cur, rb_ref, group):
    R = SEQ_PAD
    rows = group * SWA_HEADS * R
    row = lax.broadcasted_iota(jnp.int32, (rows, LANES), 0)
    col = lax.broadcasted_iota(jnp.int32, (rows, LANES), 1)
    l = row & (R - 1)
    h = (row >> 3) & (SWA_HEADS - 1)
    g = row >> 6
    delta = CHUNK + l - col
    prev = jnp.zeros((rows, LANES), F32)
    cur = jnp.zeros((rows, LANES), F32)
    dcur = l - (col & (R - 1))
    for hh in range(SWA_HEADS):
        val = jnp.full((rows, LANES), rb_ref[BUCKET_RUNS[-1][1] * SWA_HEADS + hh], F32)
        for hi, bk in reversed(BUCKET_RUNS[:-1]):
            val = jnp.where(delta < hi, rb_ref[bk * SWA_HEADS + hh], val)
        prev = jnp.where(h == hh, val, prev)
        valc = jnp.zeros((rows, LANES), F32)
        for d in range(R):
            valc = jnp.where(dcur == d, rb_ref[d * SWA_HEADS + hh], valc)
        cur = jnp.where(h == hh, valc, cur)
    tab_prev[...] = jnp.where((delta >= 0) & (delta < CHUNK), prev, -jnp.inf)
    same_seq = ((col >> 3) == g) & (col < group * R)
    tab_cur[...] = jnp.where(same_seq & (dcur >= 0), cur, -jnp.inf)


def _sample_mixers_kernel(zmix_ref, c0_ref, n0_ref, m0_ref, s0_ref, ckt_ref, cvt_ref,
                          gb_ref, inv_ref, sinks_ref, rb_ref, *rest, group, first_layer):
    zm = _z_views(zmix_ref, 0)
    mq_ref, mk_ref, mv_ref, swq_ref, misc_ref = (zm[i] for i in (ZC_ML_Q, ZC_ML_K, ZC_ML_V, ZC_SW_Q, ZC_MISC))
    rq_ref, rk_ref, rv_ref = (zm[i] for i in (ZC_RT_Q, ZC_RT_K, ZC_RT_V))
    n_layered = 4
    if not first_layer:
        rest = rest[n_layered:]
    hml_ref, osw_ref, ort_ref, n_ref, m_ref, c_all, s_all, kc_all, vc_all, tab_prev, tab_cur = rest
    layered = []
    for ref in (c_all, s_all, kc_all, vc_all):
        if first_layer:
            ref[1:] = jnp.zeros((ref.shape[0] - 1,) + ref.shape[1:], F32)
            layered.append(ref.at[0])
        else:
            layered.append(ref)
    c_ref, s_ref, kc_ref, vc_ref = layered
    R = SEQ_PAD
    NR = group * R

    @pl.when(pl.program_id(0) == 0)
    def _init():
        _build_sample_swa_tables(tab_prev, tab_cur, rb_ref, group)

    lane = lax.broadcasted_iota(jnp.int32, (1, LANES), 1)
    l_idx = lax.broadcasted_iota(jnp.int32, (NR, 1), 0) & (R - 1)
    real = l_idx < N_NEW
    l_f = l_idx.astype(F32)

    def shift(x, d):
        return x if d == 0 else pltpu.roll(x, d, 0)

    def col(slab, h):
        return slab[:, h:h + 1]

    sub = lax.broadcasted_iota(jnp.int32, (R, 1), 0)

    def ld(ref, cols=slice(None)):
        xc = ref[:, cols]
        tiles = []
        for t in range(group // 2):
            tile = xc[R * t:R * (t + 1)]
            tiles.append(jnp.where(sub < N_NEW, tile, 0.0))
            tiles.append(jnp.where(sub < N_NEW, pltpu.roll(tile, N_NEW, 0), 0.0))
        return jnp.concatenate(tiles, axis=0)

    def compact(xp):
        return jnp.concatenate(
            [jnp.where(sub < N_NEW, xp[2 * R * t:2 * R * t + R], pltpu.roll(xp[2 * R * t + R:2 * R * (t + 1)], N_NEW, 0))
             for t in range(group // 2)], axis=0)

    zero_rows = jnp.zeros((LANES - NR, LANES), F32)
    k_new = jnp.concatenate([ld(misc_ref, slice(MISC_K, MISC_K + LANES)), zero_rows], axis=0)
    v_new = jnp.concatenate([ld(misc_ref, slice(MISC_V, MISC_V + LANES)), zero_rows], axis=0)
    k_new_t = k_new.T
    v_new_t = v_new.T

    def emit_window_buffers(g):
        back = (LANES - R * g) % LANES
        for new_t, cache_ref, out_ref in ((k_new_t, ckt_ref, kc_ref), (v_new_t, cvt_ref, vc_ref)):
            merged = jnp.where(lane < N_NEW, pltpu.roll(new_t, back, 1) if back else new_t, cache_ref[g])
            out_ref[g] = pltpu.roll(merged, LANES - N_NEW, 1)

    lf = _gate_slab(ld(misc_ref, slice(MISC_IF, MISC_IF + LANES)), gb_ref)
    bsum = lf
    for d in range(1, N_NEW):
        bsum = bsum + jnp.where(l_idx >= d, shift(lf, d), 0.0)
    b = pltpu.roll(bsum, LANES - N_HEADS, 1)
    gs = lf - b
    m0 = m0_ref[...]
    log_inter = b + m0
    logd = [jnp.where(l_idx >= d, b + shift(gs, d), -jnp.inf) for d in range(N_NEW)]
    m_t = log_inter
    for d in range(N_NEW):
        m_t = jnp.maximum(m_t, logd[d])
    inter = jnp.exp(log_inter - m_t)
    dm = [jnp.exp(logd[d] - m_t) for d in range(N_NEW)]
    emt = jnp.exp(-m_t)
    b_last = _tile_bcast(b, l_idx, N_NEW - 1)
    m_new = _tile_bcast(m_t, l_idx, N_NEW - 1)
    decay = jnp.exp(b_last + m0 - m_new)
    w = jnp.where(real, jnp.exp(b_last - b + lf - m_new), 0.0)
    m_ref[...] = m_t
    n0 = n0_ref[...]
    for h in range(N_HEADS):
        hs = slice(HEAD_D * h, HEAD_D * (h + 1))
        q = ld(mq_ref, hs)
        k = ld(mk_ref, hs) * (HEAD_D ** -0.5)
        v = ld(mv_ref, hs)
        qb = q.astype(BF16)
        kb = k.astype(BF16)
        inter_c = col(inter, h)
        num = _per_seq(lambda g: _dot_nt(qb[R * g:R * (g + 1)], c0_ref[g, h].astype(BF16)), group) * inter_c
        den = inter_c * jnp.sum(q * n0[:, hs], axis=1, keepdims=True)
        for d in range(N_NEW):
            s_d = jnp.sum(q * shift(k, d), axis=1, keepdims=True) * col(dm[d], h)
            num = num + s_d * shift(v, d)
            den = den + s_d
        den = jnp.maximum(jnp.abs(den), col(emt, h))
        hml_ref[:, hs] = compact(num / den)
        w_c = col(w, h)
        dec_c = col(decay, h)
        vw = (v * w_c).astype(BF16)
        for g in range(group):
            rs = slice(R * g, R * (g + 1))
            c_ref[g, h] = dec_c[R * g:R * g + 1] * c0_ref[g, h] + _dot_tn(vw[rs], kb[rs])
        n_ref[:, hs] = dec_c * n0[:, hs] + _tile_total(k * w_c, l_idx)
        for g in range(h * group // N_HEADS, (h + 1) * group // N_HEADS):
            emit_window_buffers(g)

    cos, sin_signed = _rotary_tables((PAST_LEN + l_idx).astype(F32), inv_ref)
    for h in range(N_HEADS):
        hs = slice(HEAD_D * h, HEAD_D * (h + 1))
        lg = LOG_GAMMA[h]
        qr = _rotate(ld(rq_ref, hs), cos, sin_signed)
        kr = _rotate(ld(rk_ref, hs), cos, sin_signed) * (HEAD_D ** -0.5)
        v = ld(rv_ref, hs)
        qrb = qr.astype(BF16)
        vb = v.astype(BF16)
        o = _per_seq(lambda g: _dot(qrb[R * g:R * (g + 1)], s0_ref[g, h].astype(BF16)), group)
        o = o * jnp.exp(lg * (l_f + 1.0))
        for d in range(N_NEW):
            s_d = jnp.sum(qr * shift(kr, d), axis=1, keepdims=True) * math.exp(lg * d)
            o = o + jnp.where(l_idx >= d, s_d, 0.0) * shift(v, d)
        ort_ref[:, hs] = compact(o)
        kd =(kr * jnp.where(real, jnp.exp(lg * (N_NEW - 1.0 - l_f)), 0.0)).astype(BF16)
        for g in range(group):
            rs = slice(R * g, R * (g + 1))
            s_ref[g, h] = math.exp(lg * N_NEW) * s0_ref[g, h] + _dot_tn(kd[rs], vb[rs])

    upper = lane >= SWA_D
    q_all = ld(swq_ref) * (SWA_D ** -0.5)
    q_heads = []
    for h in range(SWA_HEADS):
        blk = q_all[:, LANES * (h // 2):LANES * (h // 2 + 1)]
        qh = jnp.where(upper if h % 2 == 1 else jnp.logical_not(upper), blk, 0.0)
        if h % 2 != h // (SWA_HEADS // 2):
            qh = pltpu.roll(qh, SWA_D, 1)
        q_heads.append(qh)
    qs = jnp.concatenate([q_heads[h][R * g:R * (g + 1)] for g in range(group) for h in range(SWA_HEADS)],
                         axis=0).astype(BF16)
    hr = SWA_HEADS * R
    s_prev = jnp.concatenate([_dot(qs[hr * g:hr * (g + 1)], ckt_ref[g].astype(BF16)) for g in range(group)],
                             axis=0) + tab_prev[...]
    s_cur = _dot_nt(qs, k_new.astype(BF16)) + tab_cur[...]
    sink64 = jnp.concatenate([jnp.full((R, 1), sinks_ref[h], F32) for h in range(SWA_HEADS)], axis=0)
    sink = jnp.concatenate([sink64] * group, axis=0)
    m = jnp.maximum(jnp.maximum(jnp.max(s_prev, axis=1, keepdims=True), jnp.max(s_cur, axis=1, keepdims=True)),
                    sink)
    p_prev = jnp.exp(s_prev - m)
    p_cur = jnp.exp(s_cur - m)
    norm = 1.0 / (jnp.sum(p_prev, axis=1, keepdims=True) + jnp.sum(p_cur, axis=1, keepdims=True)
                  + jnp.exp(sink - m))
    pb = p_prev.astype(BF16)
    o = jnp.concatenate([_dot_nt(pb[hr * g:hr * (g + 1)], cvt_ref[g].astype(BF16)) for g in range(group)],
                        axis=0)
    o = (o + _dot(p_cur.astype(BF16), v_new.astype(BF16))) * norm
    seq_tiles = []
    for g in range(group):
        blocks = []
        for j in range(SWA_HEADS // 2):
            pair = []
            for h in (2 * j, 2 * j + 1):
                oh = o[hr * g + R * h:hr * g + R * (h + 1)]
                if h % 2 != h // (SWA_HEADS // 2):
                    oh = pltpu.roll(oh, SWA_D, 1)
                pair.append(oh)
            blocks.append(jnp.where(upper, pair[1], pair[0]))
        seq_tiles.append(jnp.concatenate(blocks, axis=1))
    osw_ref[...] = compact(jnp.concatenate(seq_tiles, axis=0))


def mixer_sample(z, layer, c0_all, n0_rows, m0_rows, s0_all, ckt_all, cvt_all, gb, inv, sinks, rb, prev):
    depth, nb = c0_all.shape[:2]
    group = SAMPLE_GROUP
    R = SEQ_PAD
    NR = group * R
    first_layer = prev is None
    assert first_layer == (layer == 0)

    def const(shape):
        return pl.BlockSpec(shape, lambda i: (0,) * len(shape))

    def rows(w):
        return pl.BlockSpec((NR, w), lambda i: (i, 0))

    def tokens(w):
        return pl.BlockSpec((group * N_NEW, w), lambda i: (i, 0))

    def layered_in(shape):
        return pl.BlockSpec((None, group) + shape, lambda i: (layer, i) + (0,) * len(shape))

    def layered_out(shape):
        if first_layer:
            return pl.BlockSpec((depth, group) + shape, lambda i: (0, i) + (0,) * len(shape))
        return layered_in(shape)

    smem = pl.BlockSpec(memory_space=pltpu.SMEM)
    st = (N_HEADS, HEAD_D, HEAD_D)
    buf = (LANES, CHUNK)
    n_in = 11
    if first_layer:
        extra_specs, extra_args, aliases = [], [], {}
    else:
        extra_specs = [pl.BlockSpec(memory_space=pl.ANY)] * 4
        extra_args = list(prev)
        aliases = {n_in + i: 5 + i for i in range(4)}
    return pl.pallas_call(
        functools.partial(_sample_mixers_kernel, group=group, first_layer=first_layer),
        grid=(nb // group,),
        in_specs=[tokens(Z_HALF),
                  layered_in(st), rows(512), rows(LANES), layered_in(st),
                  layered_in(buf), layered_in(buf),
                  const((1, LANES)), const((1, LANES)), smem, smem] + extra_specs,
        out_specs=[tokens(512), tokens(512), tokens(512), rows(512), rows(LANES),
                   layered_out(st), layered_out(st), layered_out(buf), layered_out(buf)],
        out_shape=[jax.ShapeDtypeStruct((nb * N_NEW, 512), F32)] * 3
        + [jax.ShapeDtypeStruct((nb * R, 512), F32), jax.ShapeDtypeStruct((nb * R, LANES), F32),
           jax.ShapeDtypeStruct((depth, nb) + st, F32), jax.ShapeDtypeStruct((depth, nb) + st, F32),
           jax.ShapeDtypeStruct((depth, nb) + buf, F32), jax.ShapeDtypeStruct((depth, nb) + buf, F32)],
        scratch_shapes=[pltpu.VMEM((group * SWA_HEADS * R, LANES), F32),
                        pltpu.VMEM((group * SWA_HEADS * R, LANES), F32)],
        input_output_aliases=aliases,
        compiler_params=_params(1),
        name="mixer_sample",
    )(z, c0_all, n0_rows, m0_rows, s0_all, ckt_all, cvt_all, gb, inv, sinks, rb, *extra_args)


def _mixer_post_kernel(hml_ref, osw_ref, ort_ref, mlo_ref, rtg_ref, g0_ref, g1_ref, g2_ref, x_ref,
                       mlg_ref, retg_ref, ng_ref, wml_ref, wsw_ref, wrt_ref, wout_ref, o_ref):
    o_ref[...] = _post_math(hml_ref[...], osw_ref[...], ort_ref[...], mlo_ref[...], rtg_ref[...],
                            g0_ref[...], g1_ref[...], g2_ref[...], x_ref[...],
                            mlg_ref[...], retg_ref[...], ng_ref[...],
                            wml_ref[...], wsw_ref[...], wrt_ref[...], wout_ref[...])


def mixer_post(hml, osw, ort, z, x, mlg, retg, ng, layer, wml, wsw, wrt, wout, tm, name):
    m = x.shape[0]

    def tok(w, cb=0):
        return pl.BlockSpec((tm, w), lambda i, cb=cb: (i, cb))

    def const(shape):
        return pl.BlockSpec(shape, lambda i: (0,) * len(shape))

    return pl.pallas_call(
        _mixer_post_kernel,
        grid=(m // tm,),
        in_specs=[tok(512), tok(512), tok(512), tok(512, ZC_ML_O), tok(512, ZC_RT_G),
                  tok(1024, ZC_GATES), tok(1024, ZC_GATES + 1), tok(1024, ZC_GATES + 2), tok(D_MODEL),
                  const((1, 512)), const((1, 512)), const((1, D_MODEL)),
                  _layer_weight(wml, layer), _layer_weight(wsw, layer), _layer_weight(wrt, layer),
                  _layer_weight(wout, layer)],
        out_specs=tok(D_MODEL),
        out_shape=jax.ShapeDtypeStruct((m, D_MODEL), F32),
        compiler_params=_params(1),
        name=name,
    )(hml, osw, ort, z, z, z, z, z, x, mlg, retg, ng, wml, wsw, wrt, wout)


def _xattn_head(q, k, v, mask):

    s = _dot_nt(jnp.where(mask, q, 0.0).astype(BF16), k) * (X_D ** -0.5)
    m = jnp.max(s, axis=1, keepdims=True)
    p = jnp.exp(s - m)
    p = p / jnp.sum(p, axis=1, keepdims=True)
    return jnp.where(mask, _dot(p.astype(BF16), v), 0.0)


def _xattn_ffn_prompt_kernel(x_ref, gxi_ref, gxo_ref, wcq_ref, kv_ref, wco_ref, gfi_ref, gfo_ref, wgu_ref, wd_ref,
                             o_ref, mid_scr):
    @pl.when(pl.program_id(0) == 0)
    def _():
        mid_scr[...] = jnp.zeros(mid_scr.shape, F32)

    x_mid = mid_scr[...]
    x = x_ref[...]
    lane = lax.broadcasted_iota(jnp.int32, (1, X_HEADS * X_D), 1)
    masks = [(lane >= X_D * h) & (lane < X_D * (h + 1)) for h in range(X_HEADS)]
    fc = D_FF // FFN_SPLIT

    assert FFN_SPLIT == 1
    half = x.shape[0] // 2
    top, bot = slice(0, half), slice(half, 2 * half)
    k = kv_ref[:, 0:X_HEADS * X_D].astype(BF16)
    v = kv_ref[:, X_HEADS * X_D:2 * X_HEADS * X_D].astype(BF16)
    u_top = _rms(x_mid[top], gfi_ref[...]).astype(BF16)
    g_top = _dot(u_top, wgu_ref[:, 0:D_FF])
    u_bot = _rms(x_mid[bot], gfi_ref[...]).astype(BF16)
    g_bot = _dot(u_bot, wgu_ref[:, 0:D_FF])
    u = jnp.concatenate([u_top, u_bot], axis=0)
    g = jnp.concatenate([g_top, g_bot], axis=0)
    q = _dot(_rms(x, gxi_ref[...]).astype(BF16), wcq_ref[...])
    up = _dot(u, wgu_ref[:, D_FF:2 * D_FF])
    att = jnp.zeros(q.shape, F32)
    for h in range(X_HEADS):
        att = att + _xattn_head(q, k, v, masks[h])
    hid = (g * _sigmoid(g) * up).astype(BF16)
    down_top = _dot(hid[top], wd_ref[...])
    down_bot = _dot(hid[bot], wd_ref[...])
    o_ref[top, :] = x_mid[top] + _rms(down_top, gfo_ref[...])
    mid_scr[...] = x + _rms(_dot(att.astype(BF16), wco_ref[...]), gxo_ref[...])
    o_ref[bot, :] = x_mid[bot] + _rms(down_bot, gfo_ref[...])


def xattn_ffn_prompt(x, gxi, gxo, gfi, gfo, layer, wcq, kv, wco, wgu, wd, tm):
    m = x.shape[0]
    n = m // tm

    def const(shape):
        return pl.BlockSpec(shape, lambda s: (0,) * len(shape))

    gain = const((1, D_MODEL))
    return pl.pallas_call(
        _xattn_ffn_prompt_kernel,
        grid=(n + 1,),
        in_specs=[pl.BlockSpec((tm, D_MODEL), lambda s: (jnp.minimum(s, n - 1), 0)),
                  gain, gain, _layer_weight(wcq, layer), const(kv.shape), _layer_weight(wco, layer),
                  gain, gain, _layer_weight(wgu, layer), _layer_weight(wd, layer)],
        out_specs=pl.BlockSpec((tm, D_MODEL), lambda s: (jnp.maximum(s - 1, 0), 0)),
        out_shape=jax.ShapeDtypeStruct((m, D_MODEL), F32),
        scratch_shapes=[pltpu.VMEM((tm, D_MODEL), F32)],
        compiler_params=_params(1),
        name="xattn_ffn_prompt",
    )(x, gxi, gxo, wcq, kv, wco, gfi, gfo, wgu, wd)


def _xattn_sample_kernel(x_ref, gin_ref, gout_ref, wcq_ref, kt_ref, vt_ref, wco_ref, o_ref, q_scr, a_scr,
                         *, group):
    R = SEQ_PAD
    x = x_ref[...]
    q_scr[...] = _dot(_rms(x, gin_ref[...]).astype(BF16), wcq_ref[...])
    lane = lax.broadcasted_iota(jnp.int32, (1, X_HEADS * X_D), 1)
    masks = [(lane >= X_D * h) & (lane < X_D * (h + 1)) for h in range(X_HEADS)]
    second = (lax.broadcasted_iota(jnp.int32, (X_HEADS * R, 1), 0) & (R - 1)) >= N_NEW

    def body(pair, carry):
        rows = pl.ds(pl.multiple_of(pair * R, R), R)
        q = q_scr[rows, :]
        qs = jnp.concatenate([jnp.where(mk, q, 0.0) for mk in masks], axis=0).astype(BF16)
        s = jnp.where(second, _dot(qs, kt_ref[2 * pair + 1].astype(BF16)), _dot(qs, kt_ref[2 * pair].astype(BF16)))
        s = s * (X_D ** -0.5)
        m = jnp.max(s, axis=1, keepdims=True)
        p = jnp.exp(s - m)
        p = (p / jnp.sum(p, axis=1, keepdims=True)).astype(BF16)
        o = jnp.where(second, _dot_nt(p, vt_ref[2 * pair + 1].astype(BF16)), _dot_nt(p, vt_ref[2 * pair].astype(BF16)))
        acc = jnp.zeros((R, X_HEADS * X_D), F32)
        for h in range(X_HEADS):
            acc = acc + jnp.where(masks[h], o[R * h:R * (h + 1)], 0.0)
        a_scr[rows, :] = acc
        return carry

    lax.fori_loop(0, group // 2, body, 0, unroll=2)
    o_ref[...] = x + _rms(_dot(a_scr[...].astype(BF16), wco_ref[...]), gout_ref[...])


def xattn_sample(x, gin, gout, wcq, layer, mkt, mvt, wco, group=16):
    nb = mkt.shape[1]
    R = N_NEW

    def const(shape):
        return pl.BlockSpec(shape, lambda i: (0,) * len(shape))

    tok = pl.BlockSpec((group * R, D_MODEL), lambda i: (i, 0))
    mem = pl.BlockSpec((None, group, X_HEADS * X_D, N_MEM), lambda i: (layer, i, 0, 0))
    return pl.pallas_call(
        functools.partial(_xattn_sample_kernel, group=group),
        grid=(nb // group,),
        in_specs=[tok, const((1, D_MODEL)), const((1, D_MODEL)), _layer_weight(wcq, layer), mem, mem,
                  _layer_weight(wco, layer)],
        out_specs=tok,
        out_shape=jax.ShapeDtypeStruct((nb * R, D_MODEL), F32),
        scratch_shapes=[pltpu.VMEM((group * R, X_HEADS * X_D), F32),
                        pltpu.VMEM((group * R, X_HEADS * X_D), F32)],
        compiler_params=_params(1),
        name="xattn_sample",
    )(x, gin, gout, wcq, mkt, mvt, wco)


FFN_SPLIT = 1


def _ffn_kernel(x_ref, gin_ref, gout_ref, wgu_ref, wd_ref, o_ref):
    x = x_ref[...]
    u = _rms(x, gin_ref[...]).astype(BF16)
    fc = D_FF // FFN_SPLIT
    acc = None
    for c in range(FFN_SPLIT):
        g = _dot(u, wgu_ref[:, fc * c:fc * (c + 1)])
        up = _dot(u, wgu_ref[:, D_FF + fc * c:D_FF + fc * (c + 1)])
        h = (g * _sigmoid(g) * up).astype(BF16)
        part = _dot(h, wd_ref[fc * c:fc * (c + 1), :])
        acc = part if acc is None else acc + part
    o_ref[...] = x + _rms(acc, gout_ref[...])


def ffn(x, gin, gout, layer, wgu, wd, tm, name):
    m = x.shape[0]

    def const(shape):
        return pl.BlockSpec(shape, lambda i: (0,) * len(shape))

    tok = pl.BlockSpec((tm, D_MODEL), lambda i: (i, 0))
    return pl.pallas_call(
        _ffn_kernel,
        grid=(m // tm,),
        in_specs=[tok, const((1, D_MODEL)), const((1, D_MODEL)),
                  _layer_weight(wgu, layer), _layer_weight(wd, layer)],
        out_specs=tok,
        out_shape=jax.ShapeDtypeStruct((m, D_MODEL), F32),
        compiler_params=_params(1),
        name=name,
    )(x, gin, gout, wgu, wd)


def _reorder_w_in_t(w):
    wt = jnp.swapaxes(w, 1, 2)
    sizes = (512, 512, 512, 4, 4, 512, 512, 128, 128, 512, 512, 512, 512, 3072)
    offs = np.concatenate([[0], np.cumsum(sizes)])
    (ml_q, ml_k, ml_v, ml_i, ml_f, ml_o, sw_q, sw_k, sw_v, rt_q, rt_k, rt_v, rt_g, gates) = [
        wt[:, int(offs[i]):int(offs[i + 1])] for i in range(len(sizes))]
    pad = jnp.zeros((w.shape[0], 512 - 128 - 128 - 8, w.shape[1]), w.dtype)
    out = jnp.concatenate([ml_q, ml_k, ml_v, sw_q, sw_k, sw_v, ml_i, ml_f, pad, rt_q, rt_k, rt_v,
                           ml_o, rt_g, gates], axis=1)
    assert out.shape[1] == Z_COLS
    return out.astype(BF16)


def _row(v):
    return v.reshape(1, -1).astype(F32)


def _decoder_layer(x, z_fn, mixer_fn, xattn_ffn_fn):
    z = z_fn(x)
    x, states = mixer_fn(z, x)
    return xattn_ffn_fn(x), z, states


def kernel(x_prompt, x_sample, mem_prompt, state_mlstm_C, state_mlstm_n, state_mlstm_m, state_ret_S,
           cache_swa_k, cache_swa_v, cache_mem_k, cache_mem_v, norm_g, w_in, ml_gate_bias, ml_head_g,
           ret_head_g, swa_sinks, rel_bias, w_br_ml, w_br_swa, w_br_ret, w_out, w_cq, w_mkv, w_co,
           w_gu, w_down):
    depth = w_in.shape[0]
    bp, t, d = x_prompt.shape
    assert bp == 1 and d == D_MODEL and t % 1024 == 0
    nb, n_new, _ = x_sample.shape
    assert n_new == N_NEW
    R = SEQ_PAD

    xp = x_prompt.reshape(t, d)
    xs = x_sample.reshape(nb * n_new, d)
    mem = mem_prompt.reshape(N_MEM, d)
    half = HEAD_D // 2
    inv = ROPE_BASE ** (-jnp.arange(half, dtype=F32) / half)
    inv = jnp.concatenate([inv, inv]).reshape(1, LANES)
    rb = rel_bias.astype(F32).reshape(-1)
    mem_kt = jnp.transpose(cache_mem_k, (0, 1, 3, 4, 2)).reshape(depth, nb, X_HEADS * X_D, N_MEM)
    mem_vt = jnp.transpose(cache_mem_v, (0, 1, 3, 4, 2)).reshape(depth, nb, X_HEADS * X_D, N_MEM)

    win_t = _reorder_w_in_t(w_in)
    wml, wsw, wrt, wout = (w.astype(BF16) for w in (w_br_ml, w_br_swa, w_br_ret, w_out))
    wcq, wmkv, wco, wgu, wd = (w.astype(BF16) for w in (w_cq, w_mkv, w_co, w_gu, w_down))

    outs_p = {k: [] for k in ("C", "n", "m", "S", "k", "v", "mk", "mv")}
    outs_s = {k: [] for k in ("n", "m")}
    sample_layered = None
    swa_kt = jnp.transpose(cache_swa_k, (0, 1, 3, 4, 2)).reshape(depth, nb, LANES, CHUNK)
    swa_vt = jnp.transpose(cache_swa_v, (0, 1, 3, 4, 2)).reshape(depth, nb, LANES, CHUNK)
    for l in range(depth):
        ng = [_row(norm_g[l, i]) for i in range(7)]
        mlg, retg = _row(ml_head_g[l]), _row(ret_head_g[l])
        gb = jnp.concatenate([ml_gate_bias[l, 0], ml_gate_bias[l, 1],
                              jnp.zeros((LANES - 2 * N_HEADS,), F32)]).reshape(1, LANES).astype(F32)
        gb_rows = jnp.broadcast_to(ml_gate_bias[l].astype(F32).reshape(2 * N_HEADS, 1), (2 * N_HEADS, LANES))
        sinks = swa_sinks[l].astype(F32)

        kv = norm_matmul(mem, ng[6], wmkv, l, tm=N_MEM, tn=2 * X_HEADS * X_D, name="memory_kv")

        def seq_p(z, x):
            x1, c_, n_, m_, s_ = mixer_prompt(z, x, gb_rows, inv, sinks, rb, mlg, retg, ng[1], l,
                                              wml, wsw, wrt, wout)
            return x1, (c_, n_, m_, s_)

        xp, zp, (c_, n_, m_, s_) = _decoder_layer(
            xp,
            lambda x: norm_matmul(x, ng[0], win_t, l, tm=min(IN_PROJ_TM, t), tn=1024, name="in_proj_prompt",
                                  w_transposed=True),
            seq_p,
            lambda x: xattn_ffn_prompt(x, ng[2], ng[3], ng[4], ng[5], l, wcq, kv, wco, wgu, wd, tm=512))
        outs_p["C"].append(c_.reshape(1, N_HEADS, HEAD_D, HEAD_D))
        outs_p["n"].append(n_.reshape(1, N_HEADS, HEAD_D))
        outs_p["m"].append(m_[:, :N_HEADS])
        outs_p["S"].append(s_.reshape(1, N_HEADS, HEAD_D, HEAD_D))
        misc_last = zp[t - CHUNK:, 512 * ZC_MISC:512 * ZC_MISC + 2 * LANES]
        outs_p["k"].append(misc_last[:, :LANES].reshape(1, CHUNK, 2, SWA_D))
        outs_p["v"].append(misc_last[:, LANES:].reshape(1, CHUNK, 2, SWA_D))
        outs_p["mk"].append(kv[:, :X_HEADS * X_D].reshape(1, N_MEM, X_HEADS, X_D))
        outs_p["mv"].append(kv[:, X_HEADS * X_D:].reshape(1, N_MEM, X_HEADS, X_D))

        n0_rows = jnp.repeat(state_mlstm_n[l].astype(F32).reshape(nb, N_HEADS * HEAD_D), R, axis=0)
        m0_rows = jnp.repeat(jnp.pad(state_mlstm_m[l].astype(F32), ((0, 0), (0, LANES - N_HEADS))), R, axis=0)

        def seq_s(z, x):
            hml, osw, ort, n_, m_, *layered = mixer_sample(
                z, l, state_mlstm_C, n0_rows, m0_rows, state_ret_S, swa_kt, swa_vt, gb, inv, sinks, rb,
                sample_layered)
            x1 = mixer_post(hml, osw, ort, z, x, mlg, retg, ng[1], l, wml, wsw, wrt, wout,
                            tm=min(256, x.shape[0]), name="mixer_post_sample")
            return x1, (n_, m_, layered)

        xs, zs, (n_, m_, sample_layered) = _decoder_layer(
            xs,
            lambda x: norm_matmul(x, ng[0], win_t, l, tm=min(1024, nb * n_new), tn=1024, name="in_proj_sample",
                                  w_transposed=True),
            seq_s,
            lambda x: ffn(xattn_sample(x, ng[2], ng[3], wcq, l, mem_kt, mem_vt, wco),
                          ng[4], ng[5], l, wgu, wd, tm=min(512, nb * n_new), name="ffn_sample"))
        outs_s["n"].append(n_.reshape(nb, R, N_HEADS, HEAD_D)[:, R - 1])
        outs_s["m"].append(m_.reshape(nb, R, LANES)[:, n_new - 1, :N_HEADS])

    sample_c, sample_s, swa_kt_new, swa_vt_new = sample_layered
    s_swa_k = jnp.transpose(swa_kt_new.reshape(depth, nb, 2, SWA_D, CHUNK), (0, 1, 4, 2, 3))
    s_swa_v = jnp.transpose(swa_vt_new.reshape(depth, nb, 2, SWA_D, CHUNK), (0, 1, 4, 2, 3))
    y_p = xp.reshape(1, t, d)
    y_s = xs.reshape(nb, n_new, d)
    st = lambda d_, k: jnp.stack(d_[k])
    return (y_p, y_s,
            st(outs_p, "C"), st(outs_p, "n"), st(outs_p, "m"), st(outs_p, "S"),
            st(outs_p, "k"), st(outs_p, "v"), st(outs_p, "mk"), st(outs_p, "mv"),
            sample_c, st(outs_s, "n"), st(outs_s, "m"), sample_s, s_swa_k, s_swa_v)
```

```python
import functools
import math

import numpy as np
import jax
import jax.numpy as jnp
from jax import lax
from jax.experimental import pallas as pl
from jax.experimental.pallas import tpu as pltpu

F32 = jnp.float32
BF16 = jnp.bfloat16

D_MODEL = 1024
EPS = 1e-6
PAST_LEN = 16384
HEAD_D = 128
N_HEADS = 4
CHUNK = 128
SWA_HEADS = 8
SWA_D = 64
N_BUCKETS = 32
MAX_DISTANCE = 128
ROPE_BASE = 10000.0
N_MEM = 256
X_HEADS = 4
X_D = 64
D_FF = 2816
SEQ_PAD = 8
N_NEW = 4
LANES = 128
IN_PROJ_TM = 2048
MXU_MIN_ROWS = 16
VMEM_LIMIT = 48 * 1024 * 1024

ZC_ML_Q, ZC_ML_K, ZC_ML_V, ZC_SW_Q, ZC_MISC, ZC_RT_Q, ZC_RT_K, ZC_RT_V, ZC_ML_O, ZC_RT_G = range(10)
ZC_GATES = 5
Z_COLS = 8192
Z_HALF = Z_COLS // 2
MISC_K, MISC_V, MISC_IF = 0, 128, 256

LOG2E = math.log2(math.e)
LN2 = math.log(2.0)
LOG_GAMMA = tuple(float(v) for v in np.log1p(-np.exp2(-5.0 - np.arange(N_HEADS, dtype=np.float32))))

SWA_SAME = (0, 2, 5, 7)
SWA_SWAP = (1, 3, 4, 6)


def _t5_bucket_bounds():
    n = np.arange(CHUNK)
    max_exact = N_BUCKETS // 2
    nf = np.maximum(n, 1).astype(np.float32)
    large = max_exact + (np.log(nf / np.float32(max_exact)) / np.float32(math.log(MAX_DISTANCE / max_exact))
                         * np.float32(N_BUCKETS - max_exact)).astype(np.int32)
    large = np.minimum(large, N_BUCKETS - 1)
    b = np.where(n < max_exact, n, large)
    assert np.all(np.diff(b) >= 0)
    runs = []
    for d in range(CHUNK):
        if runs and runs[-1][1] == int(b[d]):
            runs[-1][0] = d + 1
        else:
            runs.append([d + 1, int(b[d])])
    return tuple((hi, bk) for hi, bk in runs)


BUCKET_RUNS = _t5_bucket_bounds()


def _dot(a, b):
    return jnp.dot(a, b, preferred_element_type=F32)


def _dot_nt(a, b):
    return lax.dot_general(a, b, (((1,), (1,)), ((), ())), preferred_element_type=F32)


def _dot_tn(a, b):
    return lax.dot_general(a, b, (((0,), (0,)), ((), ())), preferred_element_type=F32)


def _rms(x, g):
    return x * lax.rsqrt(jnp.mean(x * x, axis=-1, keepdims=True) + EPS) * g


def _sigmoid(x):
    return 1.0 / (1.0 + jnp.exp2(x * (-LOG2E)))


def _log_sigmoid(x):
    return jnp.minimum(x, 0.0) - jnp.log1p(jnp.exp(-jnp.abs(x)))


def _layer_weight(w_all, layer):
    shape = w_all.shape[1:]
    return pl.BlockSpec((None,) + shape, lambda *_: (layer,) + (0,) * len(shape), pipeline_mode=pl.Buffered(1))


def _params(n_grid):
    return pltpu.CompilerParams(dimension_semantics=("arbitrary",) * n_grid, vmem_limit_bytes=VMEM_LIMIT)


def _norm_matmul_kernel(x_ref, g_ref, w_ref, o_ref, u_ref, *, w_transposed):
    def project(u):
        return _dot_nt(u, w_ref[...]) if w_transposed else _dot(u, w_ref[...])

    j = pl.program_id(1)

    @pl.when(j == 0)
    def _():
        half = x_ref.shape[0] // 2
        top, bot = slice(0, half), slice(half, 2 * half)
        u_top = _rms(x_ref[top, :], g_ref[...]).astype(BF16)
        o_top = project(u_top)
        u_bot = _rms(x_ref[bot, :], g_ref[...]).astype(BF16)
        u_ref[top, :] = u_top
        u_ref[bot, :] = u_bot
        o_ref[top, :] = o_top
        o_ref[bot, :] = project(u_bot)

    @pl.when(j != 0)
    def _():
        o_ref[...] = project(u_ref[...])


def norm_matmul(x, g, w_all, layer, tm, tn, name, w_transposed=False):
    m, k = x.shape
    n = w_all.shape[1] if w_transposed else w_all.shape[2]
    if w_transposed:
        w_spec = pl.BlockSpec((None, tn, k), lambda i, j: (layer, j, 0))
    else:
        w_spec = pl.BlockSpec((None, k, tn), lambda i, j: (layer, 0, j))
    return pl.pallas_call(
        functools.partial(_norm_matmul_kernel, w_transposed=w_transposed),
        grid=(m // tm, n // tn),
        in_specs=[pl.BlockSpec((tm, k), lambda i, j: (i, 0)),
                  pl.BlockSpec((1, k), lambda i, j: (0, 0)),
                  w_spec],
        out_specs=pl.BlockSpec((tm, tn), lambda i, j: (i, j)),
        out_shape=jax.ShapeDtypeStruct((m, n), F32),
        scratch_shapes=[pltpu.VMEM((tm, k), BF16)],
        compiler_params=_params(2),
        name=name,
    )(x, g, w_all)


def _transpose_bf16(x, eye):
    return _dot_nt(eye, x).astype(BF16)


def _build_swa_table_t(tab_ref, rb_ref):
    L = CHUNK
    srow = lax.broadcasted_iota(jnp.int32, (2 * L, L), 0)
    lcol = lax.broadcasted_iota(jnp.int32, (2 * L, L), 1)
    delta = lcol + L - srow
    valid = (delta >= 0) & (delta < L)
    for var, heads in enumerate((SWA_SAME, SWA_SWAP)):
        for i, h in enumerate(heads):
            val = jnp.full((2 * L, L), rb_ref[BUCKET_RUNS[-1][1] * SWA_HEADS + h], F32)
            for hi, bk in reversed(BUCKET_RUNS[:-1]):
                val = jnp.where(delta < hi, rb_ref[bk * SWA_HEADS + h], val)
            tab_ref[var, :, i * L:(i + 1) * L] = jnp.where(valid, val, -jnp.inf)
            tab_ref[2 + var, :, i * L:(i + 1) * L] = jnp.where(valid & (srow >= L), val, -jnp.inf)


def _swa_scores_t(q, k_same, k_swap, tab_ref, first):
    lane = lax.broadcasted_iota(jnp.int32, (1, LANES), 1)
    upper = lane >= SWA_D
    q = q * (SWA_D ** -0.5)
    ks = (k_same[...], k_swap[...])
    scores = []
    for var, heads in enumerate((SWA_SAME, SWA_SWAP)):
        qm = []
        for h in heads:
            blk = q[:, LANES * (h // 2):LANES * (h // 2 + 1)]
            qm.append(jnp.where(upper if h % 2 == 1 else jnp.logical_not(upper), blk, 0.0))
        qs = jnp.concatenate(qm, axis=0).astype(BF16)
        scores.append(_dot_nt(ks[var], qs) + tab_ref[var + 2 * first])
    return scores


def _swa_softmax_t(scores, sinks_ref):
    probs = []
    for s_t, heads in zip(scores, (SWA_SAME, SWA_SWAP)):
        sink = jnp.concatenate([jnp.full((1, CHUNK), sinks_ref[h], F32) for h in heads], axis=1)
        m = jnp.maximum(jnp.max(s_t, axis=0, keepdims=True), sink)
        p = jnp.exp(s_t - m)
        norm = 1.0 / (jnp.sum(p, axis=0, keepdims=True) + jnp.exp(sink - m))
        probs.append((p.astype(BF16), norm))
    return probs


def _swa_values_t(probs, vt_buf):
    vt = vt_buf[...]
    vts = (vt, jnp.concatenate([vt[SWA_D:], vt[:SWA_D]], axis=0))
    return [_dot(vts[var], p) * norm for var, (p, norm) in enumerate(probs)]


def _swa_output_t(values):
    L = CHUNK
    lane = lax.broadcasted_iota(jnp.int32, (1, LANES), 1)
    upper = lane >= SWA_D
    outs = [None] * SWA_HEADS
    for o_t, heads in zip(values, (SWA_SAME, SWA_SWAP)):
        for i, h in enumerate(heads):
            outs[h] = o_t[:, i * L:(i + 1) * L].T
    blocks = [jnp.where(upper, outs[2 * j + 1], outs[2 * j]) for j in range(SWA_HEADS // 2)]
    return jnp.concatenate(blocks, axis=1)


def _rotary_tables(pos, inv_ref):
    lane = lax.broadcasted_iota(jnp.int32, (1, LANES), 1)
    ang = pos * inv_ref[...]
    sin = jnp.sin(ang)
    return jnp.cos(ang), jnp.where(lane < HEAD_D // 2, -sin, sin)


def _rotate(x, cos, sin_signed):
    return x * cos + pltpu.roll(x, HEAD_D // 2, 1) * sin_signed


def _gate_slab(raw, gb_ref):
    lane = lax.broadcasted_iota(jnp.int32, (1, LANES), 1)
    x = raw + gb_ref[...]
    return jnp.where((lane >= N_HEADS) & (lane < 2 * N_HEADS), _log_sigmoid(x), x)


def _head_rms(h, gain):
    parts = []
    for i in range(N_HEADS):
        blk = h[:, HEAD_D * i:HEAD_D * (i + 1)]
        parts.append(blk * lax.rsqrt(jnp.mean(blk * blk, axis=-1, keepdims=True) + EPS))
    return jnp.concatenate(parts, axis=-1) * gain


def _rms_over_rows(h_t):
    return h_t * lax.rsqrt(jnp.mean(h_t * h_t, axis=0, keepdims=True) + EPS)


def _post_math(hml, osw, ort, mlo, rtg, g0, g1, g2, x, mlg, retg, ng, wml, wsw, wrt, wout, head_normed=False):
    if not head_normed:
        hml = _head_rms(hml, mlg)
        ort = _head_rms(ort, retg)
    hm = hml * _sigmoid(mlo)
    y_ml = _dot(hm.astype(BF16), wml)
    y_sw = _dot(osw.astype(BF16), wsw)
    rt = ort * (rtg * _sigmoid(rtg))
    y_rt = _dot(rt.astype(BF16), wrt)
    merged = _sigmoid(g0) * y_ml + _sigmoid(g1) * y_sw + _sigmoid(g2) * y_rt
    return x + _rms(_dot(merged.astype(BF16), wout), ng)


def _z_views(z_half_ref, first_col_block):
    return {first_col_block + i: z_half_ref.at[:, 512 * i:512 * (i + 1)] for i in range(Z_HALF // 512)}


def _mixer_seq_prompt_kernel(zmix_ref, ztail_ref, x_ref,
                             gbr_ref, inv_ref, sinks_ref, rb_ref,
                             mlg_ref, retg_ref, ng_ref, wml_ref, wsw_ref, wrt_ref, wout_ref,
                             x1_ref, c_ref, n_ref, m_ref, s_ref,
                             st_scr, k_same, k_swap, vt_buf, tab_t, cos_l, sin_l, dec_in_t, eye_ref,
                             hml_s, osw_s, ort_s):
    zm = _z_views(zmix_ref, 0)
    mq_ref, mk_ref, mv_ref, swq_ref, misc_ref = (zm[i] for i in (ZC_ML_Q, ZC_ML_K, ZC_ML_V, ZC_SW_Q, ZC_MISC))
    rq_ref, rk_ref, rv_ref = (zm[i] for i in (ZC_RT_Q, ZC_RT_K, ZC_RT_V))
    zt = _z_views(ztail_ref, Z_HALF // 512)
    mlo_ref, rtg_ref = zt[ZC_ML_O], zt[ZC_RT_G]
    g0_ref, g1_ref, g2_ref = (ztail_ref.at[:, 1024 * (ZC_GATES + i) - Z_HALF:1024 * (ZC_GATES + i + 1) - Z_HALF]
                              for i in range(3))
    c = pl.program_id(0)
    active = c < pl.num_programs(0) - 1
    L = CHUNK
    lane = lax.broadcasted_iota(jnp.int32, (1, LANES), 1)
    row = lax.broadcasted_iota(jnp.int32, (L, L), 0)
    col = lax.broadcasted_iota(jnp.int32, (L, L), 1)
    causal_t = row <= col

    @pl.when(c == 0)
    def _init():
        c_ref[...] = jnp.zeros(c_ref.shape, F32)
        n_ref[...] = jnp.zeros(n_ref.shape, F32)
        m_ref[...] = jnp.zeros(m_ref.shape, F32)
        s_ref[...] = jnp.zeros(s_ref.shape, F32)
        hml_s[...] = jnp.zeros(hml_s.shape, F32)
        osw_s[...] = jnp.zeros(osw_s.shape, F32)
        ort_s[...] = jnp.zeros(ort_s.shape, F32)
        st_scr[...] = jnp.zeros(st_scr.shape, F32)
        k_same[...] = jnp.zeros(k_same.shape, BF16)
        k_swap[...] = jnp.zeros(k_swap.shape, BF16)
        vt_buf[...] = jnp.zeros(vt_buf.shape, BF16)
        _build_swa_table_t(tab_t, rb_ref)
        ang = row.astype(F32) * inv_ref[...]
        cos_l[...] = jnp.cos(ang)
        sin_l[...] = jnp.sin(ang)
        rel_t = (col - row).astype(F32)
        for h in range(N_HEADS):
            dec_in_t[h] = jnp.where(causal_t, jnp.exp(LOG_GAMMA[h] * rel_t), 0.0)
        eye_ref[...] = jnp.where(row == col, 1.0, 0.0).astype(BF16)

    eye = eye_ref[...]
    heads = range(N_HEADS)
    hsl = [slice(HEAD_D * h, HEAD_D * (h + 1)) for h in heads]
    sub8 = lax.broadcasted_iota(jnp.int32, (2 * N_HEADS, 1), 0)
    hml_prev, osw_prev, ort_prev = hml_s[...], osw_s[...], ort_s[...]

    raw = misc_ref[:, MISC_IF:MISC_IF + LANES].T[0:2 * N_HEADS] + gbr_ref[...]
    gates = jnp.where(sub8 >= N_HEADS, _log_sigmoid(raw), raw)
    cum = gates
    for sh in (1, 2, 4, 8, 16, 32, 64):
        cum = cum + jnp.where(lane >= sh, pltpu.roll(cum, sh, 1), 0.0)
    g8 = jnp.where(sub8 < N_HEADS, gates - pltpu.roll(cum, N_HEADS, 0), 0.0)
    g2_cols = jnp.concatenate([g8 * LOG2E, jnp.zeros((L - 2 * N_HEADS, L), F32)], axis=0).T

    ang0 = (c * L).astype(F32) * inv_ref[...]
    cos0 = jnp.cos(ang0)
    sin0 = jnp.sin(ang0)
    cos_t = cos_l[...]
    sin_t = sin_l[...]
    cos = cos0 * cos_t - sin0 * sin_t
    sin = sin0 * cos_t + cos0 * sin_t
    sin_signed = jnp.where(lane < HEAD_D // 2, -sin, sin)
    ml_ops, rt_ops = [], []
    for h in heads:
        qb = mq_ref[:, hsl[h]].astype(BF16)
        kb = (mk_ref[:, hsl[h]] * (HEAD_D ** -0.5)).astype(BF16)
        vt = _transpose_bf16(mv_ref[:, hsl[h]].astype(BF16), eye)
        ml_ops.append((qb, kb, vt, _dot_nt(kb, qb)))
    for h in heads:
        qr = _rotate(rq_ref[:, hsl[h]], cos, sin_signed).astype(BF16)
        kr = (_rotate(rk_ref[:, hsl[h]], cos, sin_signed) * (HEAD_D ** -0.5)).astype(BF16)
        vt = _transpose_bf16(rv_ref[:, hsl[h]].astype(BF16), eye)
        rt_ops.append((qr, kr, vt, _dot_nt(kr, qr)))
    k_new = misc_ref[:, MISC_K:MISC_K + LANES]
    k_same[0:L, :] = k_same[L:2 * L, :]
    k_swap[0:L, :] = k_swap[L:2 * L, :]
    k_same[L:2 * L, :] = k_new.astype(BF16)
    k_swap[L:2 * L, :] = pltpu.roll(k_new, SWA_D, 1).astype(BF16)
    vt_buf[:, 0:L] = vt_buf[:, L:2 * L]
    vt_buf[:, L:2 * L] = _transpose_bf16(misc_ref[:, MISC_V:MISC_V + LANES].astype(BF16), eye)
    swa_scores = _swa_scores_t(swq_ref[...], k_same, k_swap, tab_t, (c == 0).astype(jnp.int32))

    y_ml = _dot((hml_prev * _sigmoid(mlo_ref[...])).astype(BF16), wml_ref[...])

    m_all = m_ref[...]
    gate_ops = []
    for h in heads:
        m_prev = m_all[:, h:h + 1]
        gm2 = jnp.where(causal_t, g2_cols[:, h:h + 1], -jnp.inf)
        mx2 = jnp.maximum(jnp.max(gm2, axis=0, keepdims=True), m_prev * LOG2E)
        gate_ops.append((m_prev, mx2 * LN2, jnp.exp2(m_prev * LOG2E - mx2), jnp.exp2(gm2 - mx2)))

    y_sw = _dot(osw_prev.astype(BF16), wsw_ref[...])
    rtg = rtg_ref[...]
    y_rt = _dot((ort_prev * (rtg * _sigmoid(rtg))).astype(BF16), wrt_ref[...])

    l_row = lane.astype(F32)
    ml_out, rt_out = [], []
    for h in heads:
        qb, kb, vt, qk = ml_ops[h]
        m_prev, mx, inter, dm_t = gate_ops[h]
        s_t = qk * dm_t
        num_t = _dot(vt, s_t.astype(BF16)) + _dot_nt(c_ref[h].astype(BF16), qb) * inter
        nq = _dot_nt(jnp.broadcast_to(n_ref[h], (MXU_MIN_ROWS, HEAD_D)).astype(BF16), qb)[0:1, :]
        den = jnp.sum(s_t, axis=0, keepdims=True) + inter * nq
        m_t = cum[N_HEADS + h:N_HEADS + h + 1, :] + mx
        den = jnp.maximum(jnp.abs(den), jnp.exp(-m_t))
        ml_out.append((num_t * (1.0 / den), m_t))
    for h in heads:
        qr, kr, vt, qk = rt_ops[h]
        s_t = qk * dec_in_t[h]
        q_decay = jnp.exp(LOG_GAMMA[h] * (l_row + 1.0))
        rt_out.append(_dot(vt, s_t.astype(BF16)) + _dot_nt(st_scr[h].astype(BF16), qr) * q_decay)
    swa_values = _swa_values_t(_swa_softmax_t(swa_scores, sinks_ref), vt_buf)

    merged = _sigmoid(g0_ref[...]) * y_ml + _sigmoid(g1_ref[...]) * y_sw + _sigmoid(g2_ref[...]) * y_rt
    x1_ref[...] = x_ref[...] + _rms(_dot(merged.astype(BF16), wout_ref[...]), ng_ref[...])

    osw_s[...] = _swa_output_t(swa_values)
    for h in heads:
        hml_s[:, hsl[h]] = _rms_over_rows(ml_out[h][0]).T * mlg_ref[:, hsl[h]]
        ort_s[:, hsl[h]] = _rms_over_rows(rt_out[h]).T * retg_ref[:, hsl[h]]
    m_out = jnp.zeros((1, LANES), F32)
    for h in heads:
        qb, kb, vt, _ = ml_ops[h]
        m_prev = gate_ops[h][0]
        m_t = ml_out[h][1]
        m_new = m_t[:, L - 1:L]
        b_last = cum[N_HEADS + h:N_HEADS + h + 1, L - 1:L]
        decay = jnp.exp(b_last + m_prev - m_new)
        w_r = jnp.exp(g8[h:h + 1, :] + (b_last - m_new))
        c_old = c_ref[h]
        n_old = n_ref[h]
        c_new = decay * c_old + _dot((vt.astype(F32) * w_r).astype(BF16), kb)
        n_new = decay * n_old + _dot(jnp.broadcast_to(w_r, (MXU_MIN_ROWS, L)).astype(BF16), kb)[0:1, :]
        c_ref[h] = jnp.where(active, c_new, c_old)
        n_ref[h] = jnp.where(active, n_new, n_old)
        m_out = jnp.where(lane == h, m_new, m_out)
    m_ref[...] = jnp.where(active, m_out, m_all)
    for h in heads:
        qr, kr, vt, _ = rt_ops[h]
        lg = LOG_GAMMA[h]
        k_decay = jnp.exp(lg * (L - 1.0 - l_row))
        st_old = st_scr[h]
        st_new = math.exp(lg * L) * st_old + _dot((vt.astype(F32) * k_decay).astype(BF16), kr)
        st_scr[h] = jnp.where(active, st_new, st_old)

    @pl.when(c == pl.num_programs(0) - 1)
    def _emit_state():
        for h in heads:
            s_ref[h] = st_scr[h].T


def mixer_prompt(z, x, gb, inv, sinks, rb, mlg, retg, ng, layer, wml, wsw, wrt, wout):
    t = z.shape[0]
    L = CHUNK
    n = t // L

    z_mix = pl.BlockSpec((L, Z_HALF), lambda c: (jnp.minimum(c, n - 1), 0))

    def tail(w, cb=0):
        return pl.BlockSpec((L, w), lambda c, cb=cb: (jnp.maximum(c - 1, 0), cb))

    def const(shape):
        return pl.BlockSpec(shape, lambda c: (0,) * len(shape))

    def weight(w):
        return _layer_weight(w, layer)

    smem = pl.BlockSpec(memory_space=pltpu.SMEM)
    return pl.pallas_call(
        _mixer_seq_prompt_kernel,
        grid=(n + 1,),
        in_specs=[z_mix, tail(Z_HALF, 1), tail(D_MODEL),
                  const((2 * N_HEADS, LANES)), const((1, LANES)), smem, smem,
                  const((1, 512)), const((1, 512)), const((1, D_MODEL)),
                  weight(wml), weight(wsw), weight(wrt), weight(wout)],
        out_specs=[tail(D_MODEL),
                   const((N_HEADS, HEAD_D, HEAD_D)), const((N_HEADS, 1, HEAD_D)), const((1, LANES)),
                   const((N_HEADS, HEAD_D, HEAD_D))],
        out_shape=[jax.ShapeDtypeStruct((t, D_MODEL), F32),
                   jax.ShapeDtypeStruct((N_HEADS, HEAD_D, HEAD_D), F32),
                   jax.ShapeDtypeStruct((N_HEADS, 1, HEAD_D), F32),
                   jax.ShapeDtypeStruct((1, LANES), F32),
                   jax.ShapeDtypeStruct((N_HEADS, HEAD_D, HEAD_D), F32)],
        scratch_shapes=[pltpu.VMEM((N_HEADS, L, L), F32),
                        pltpu.VMEM((2 * L, LANES), BF16), pltpu.VMEM((2 * L, LANES), BF16),
                        pltpu.VMEM((LANES, 2 * L), BF16),
                        pltpu.VMEM((4, 2 * L, 4 * L), F32),
                        pltpu.VMEM((L, L), F32), pltpu.VMEM((L, L), F32),
                        pltpu.VMEM((N_HEADS, L, L), F32),
                        pltpu.VMEM((L, L), BF16),
                        pltpu.VMEM((L, 512), F32), pltpu.VMEM((L, 512), F32), pltpu.VMEM((L, 512), F32)],
        compiler_params=_params(1),
        name="mixer_prompt",
    )(z, z, x, gb, inv, sinks, rb, mlg, retg, ng, wml, wsw, wrt, wout)


SAMPLE_GROUP = 8


def _tile_bcast(x, l_idx, src):
    n_rows = x.shape[0]
    out = jnp.zeros_like(x)
    for j in range(SEQ_PAD):
        out = out + jnp.where(l_idx == j, pltpu.roll(x, (j - src) % n_rows, 0), 0.0)
    return out


def _tile_total(x, l_idx):
    for d in (1, 2, 4):
        x = x + jnp.where(l_idx >= d, pltpu.roll(x, d, 0), 0.0)
    return x


def _per_seq(fn, group):
    return jnp.concatenate([fn(g) for g in range(group)], axis=0)


def _build_sample_swa_tables(tab_prev, tab_cur, rb_ref, group):
    R = SEQ_PAD
    rows = group * SWA_HEADS * R
    row = lax.broadcasted_iota(jnp.int32, (rows, LANES), 0)
    col = lax.broadcasted_iota(jnp.int32, (rows, LANES), 1)
    l = row & (R - 1)
    h = (row >> 3) & (SWA_HEADS - 1)
    g = row >> 6
    delta = CHUNK + l - col
    prev = jnp.zeros((rows, LANES), F32)
    cur = jnp.zeros((rows, LANES), F32)
    dcur = l - (col & (R - 1))
    for hh in range(SWA_HEADS):
        val = jnp.full((rows, LANES), rb_ref[BUCKET_RUNS[-1][1] * SWA_HEADS + hh], F32)
        for hi, bk in reversed(BUCKET_RUNS[:-1]):
            val = jnp.where(delta < hi, rb_ref[bk * SWA_HEADS + hh], val)
        prev = jnp.where(h == hh, val, prev)
        valc = jnp.zeros((rows, LANES), F32)
        for d in range(R):
            valc = jnp.where(dcur == d, rb_ref[d * SWA_HEADS + hh], valc)
        cur = jnp.where(h == hh, valc, cur)
    tab_prev[...] = jnp.where((delta >= 0) & (delta < CHUNK), prev, -jnp.inf)
    same_seq = ((col >> 3) == g) & (col < group * R)
    tab_cur[...] = jnp.where(same_seq & (dcur >= 0), cur, -jnp.inf)


def _sample_mixers_kernel(zmix_ref, c0_ref, n0_ref, m0_ref, s0_ref, ckt_ref, cvt_ref,
                          gb_ref, inv_ref, sinks_ref, rb_ref, *rest, group, first_layer):
    zm = _z_views(zmix_ref, 0)
    mq_ref, mk_ref, mv_ref, swq_ref, misc_ref = (zm[i] for i in (ZC_ML_Q, ZC_ML_K, ZC_ML_V, ZC_SW_Q, ZC_MISC))
    rq_ref, rk_ref, rv_ref = (zm[i] for i in (ZC_RT_Q, ZC_RT_K, ZC_RT_V))
    n_layered = 4
    if not first_layer:
        rest = rest[n_layered:]
    hml_ref, osw_ref, ort_ref, n_ref, m_ref, c_all, s_all, kc_all, vc_all, tab_prev, tab_cur = rest
    layered = []
    for ref in (c_all, s_all, kc_all, vc_all):
        if first_layer:
            ref[1:] = jnp.zeros((ref.shape[0] - 1,) + ref.shape[1:], F32)
            layered.append(ref.at[0])
        else:
            layered.append(ref)
    c_ref, s_ref, kc_ref, vc_ref = layered
    R = SEQ_PAD
    NR = group * R

    @pl.when(pl.program_id(0) == 0)
    def _init():
        _build_sample_swa_tables(tab_prev, tab_cur, rb_ref, group)

    lane = lax.broadcasted_iota(jnp.int32, (1, LANES), 1)
    l_idx = lax.broadcasted_iota(jnp.int32, (NR, 1), 0) & (R - 1)
    real = l_idx < N_NEW
    l_f = l_idx.astype(F32)

    def shift(x, d):
        return x if d == 0 else pltpu.roll(x, d, 0)

    def col(slab, h):
        return slab[:, h:h + 1]

    sub = lax.broadcasted_iota(jnp.int32, (R, 1), 0)

    def ld(ref, cols=slice(None)):
        xc = ref[:, cols]
        tiles = []
        for t in range(group // 2):
            tile = xc[R * t:R * (t + 1)]
            tiles.append(jnp.where(sub < N_NEW, tile, 0.0))
            tiles.append(jnp.where(sub < N_NEW, pltpu.roll(tile, N_NEW, 0), 0.0))
        return jnp.concatenate(tiles, axis=0)

    def compact(xp):
        return jnp.concatenate(
            [jnp.where(sub < N_NEW, xp[2 * R * t:2 * R * t + R], pltpu.roll(xp[2 * R * t + R:2 * R * (t + 1)], N_NEW, 0))
             for t in range(group // 2)], axis=0)

    zero_rows = jnp.zeros((LANES - NR, LANES), F32)
    k_new = jnp.concatenate([ld(misc_ref, slice(MISC_K, MISC_K + LANES)), zero_rows], axis=0)
    v_new = jnp.concatenate([ld(misc_ref, slice(MISC_V, MISC_V + LANES)), zero_rows], axis=0)
    k_new_t = k_new.T
    v_new_t = v_new.T

    def emit_window_buffers(g):
        back = (LANES - R * g) % LANES
        for new_t, cache_ref, out_ref in ((k_new_t, ckt_ref, kc_ref), (v_new_t, cvt_ref, vc_ref)):
            merged = jnp.where(lane < N_NEW, pltpu.roll(new_t, back, 1) if back else new_t, cache_ref[g])
            out_ref[g] = pltpu.roll(merged, LANES - N_NEW, 1)

    lf = _gate_slab(ld(misc_ref, slice(MISC_IF, MISC_IF + LANES)), gb_ref)
    bsum = lf
    for d in range(1, N_NEW):
        bsum = bsum + jnp.where(l_idx >= d, shift(lf, d), 0.0)
    b = pltpu.roll(bsum, LANES - N_HEADS, 1)
    gs = lf - b
    m0 = m0_ref[...]
    log_inter = b + m0
    logd = [jnp.where(l_idx >= d, b + shift(gs, d), -jnp.inf) for d in range(N_NEW)]
    m_t = log_inter
    for d in range(N_NEW):
        m_t = jnp.maximum(m_t, logd[d])
    inter = jnp.exp(log_inter - m_t)
    dm = [jnp.exp(logd[d] - m_t) for d in range(N_NEW)]
    emt = jnp.exp(-m_t)
    b_last = _tile_bcast(b, l_idx, N_NEW - 1)
    m_new = _tile_bcast(m_t, l_idx, N_NEW - 1)
    decay = jnp.exp(b_last + m0 - m_new)
    w = jnp.where(real, jnp.exp(b_last - b + lf - m_new), 0.0)
    m_ref[...] = m_t
    n0 = n0_ref[...]
    for h in range(N_HEADS):
        hs = slice(HEAD_D * h, HEAD_D * (h + 1))
        q = ld(mq_ref, hs)
        k = ld(mk_ref, hs) * (HEAD_D ** -0.5)
        v = ld(mv_ref, hs)
        qb = q.astype(BF16)
        kb = k.astype(BF16)
        inter_c = col(inter, h)
        num = _per_seq(lambda g: _dot_nt(qb[R * g:R * (g + 1)], c0_ref[g, h].astype(BF16)), group) * inter_c
        den = inter_c * jnp.sum(q * n0[:, hs], axis=1, keepdims=True)
        for d in range(N_NEW):
            s_d = jnp.sum(q * shift(k, d), axis=1, keepdims=True) * col(dm[d], h)
            num = num + s_d * shift(v, d)
            den = den + s_d
        den = jnp.maximum(jnp.abs(den), col(emt, h))
        hml_ref[:, hs] = compact(num / den)
        w_c = col(w, h)
        dec_c = col(decay, h)
        vw = (v * w_c).astype(BF16)
        for g in range(group):
            rs = slice(R * g, R * (g + 1))
            c_ref[g, h] = dec_c[R * g:R * g + 1] * c0_ref[g, h] + _dot_tn(vw[rs], kb[rs])
        n_ref[:, hs] = dec_c * n0[:, hs] + _tile_total(k * w_c, l_idx)
        for g in range(h * group // N_HEADS, (h + 1) * group // N_HEADS):
            emit_window_buffers(g)

    cos, sin_signed = _rotary_tables((PAST_LEN + l_idx).astype(F32), inv_ref)
    for h in range(N_HEADS):
        hs = slice(HEAD_D * h, HEAD_D * (h + 1))
        lg = LOG_GAMMA[h]
        qr = _rotate(ld(rq_ref, hs), cos, sin_signed)
        kr = _rotate(ld(rk_ref, hs), cos, sin_signed) * (HEAD_D ** -0.5)
        v = ld(rv_ref, hs)
        qrb = qr.astype(BF16)
        vb = v.astype(BF16)
        o = _per_seq(lambda g: _dot(qrb[R * g:R * (g + 1)], s0_ref[g, h].astype(BF16)), group)
        o = o * jnp.exp(lg * (l_f + 1.0))
        for d in range(N_NEW):
            s_d = jnp.sum(qr * shift(kr, d), axis=1, keepdims=True) * math.exp(lg * d)
            o = o + jnp.where(l_idx >= d, s_d, 0.0) * shift(v, d)
        ort_ref[:, hs] = compact(o)
        kd =(kr * jnp.where(real, jnp.exp(lg * (N_NEW - 1.0 - l_f)), 0.0)).astype(BF16)
        for g in range(group):
            rs = slice(R * g, R * (g + 1))
            s_ref[g, h] = math.exp(lg * N_NEW) * s0_ref[g, h] + _dot_tn(kd[rs], vb[rs])

    upper = lane >= SWA_D
    q_all = ld(swq_ref) * (SWA_D ** -0.5)
    q_heads = []
    for h in range(SWA_HEADS):
        blk = q_all[:, LANES * (h // 2):LANES * (h // 2 + 1)]
        qh = jnp.where(upper if h % 2 == 1 else jnp.logical_not(upper), blk, 0.0)
        if h % 2 != h // (SWA_HEADS // 2):
            qh = pltpu.roll(qh, SWA_D, 1)
        q_heads.append(qh)
    qs = jnp.concatenate([q_heads[h][R * g:R * (g + 1)] for g in range(group) for h in range(SWA_HEADS)],
                         axis=0).astype(BF16)
    hr = SWA_HEADS * R
    s_prev = jnp.concatenate([_dot(qs[hr * g:hr * (g + 1)], ckt_ref[g].astype(BF16)) for g in range(group)],
                             axis=0) + tab_prev[...]
    s_cur = _dot_nt(qs, k_new.astype(BF16)) + tab_cur[...]
    sink64 = jnp.concatenate([jnp.full((R, 1), sinks_ref[h], F32) for h in range(SWA_HEADS)], axis=0)
    sink = jnp.concatenate([sink64] * group, axis=0)
    m = jnp.maximum(jnp.maximum(jnp.max(s_prev, axis=1, keepdims=True), jnp.max(s_cur, axis=1, keepdims=True)),
                    sink)
    p_prev = jnp.exp(s_prev - m)
    p_cur = jnp.exp(s_cur - m)
    norm = 1.0 / (jnp.sum(p_prev, axis=1, keepdims=True) + jnp.sum(p_cur, axis=1, keepdims=True)
                  + jnp.exp(sink - m))
    pb = p_prev.astype(BF16)
    o = jnp.concatenate([_dot_nt(pb[hr * g:hr * (g + 1)], cvt_ref[g].astype(BF16)) for g in range(group)],
                        axis=0)
    o = (o + _dot(p_cur.astype(BF16), v_new.astype(BF16))) * norm
    seq_tiles = []
    for g in range(group):
        blocks = []
        for j in range(SWA_HEADS // 2):
            pair = []
            for h in (2 * j, 2 * j + 1):
                oh = o[hr * g + R * h:hr * g + R * (h + 1)]
                if h % 2 != h // (SWA_HEADS // 2):
                    oh = pltpu.roll(oh, SWA_D, 1)
                pair.append(oh)
            blocks.append(jnp.where(upper, pair[1], pair[0]))
        seq_tiles.append(jnp.concatenate(blocks, axis=1))
    osw_ref[...] = compact(jnp.concatenate(seq_tiles, axis=0))


def mixer_sample(z, layer, c0_all, n0_rows, m0_rows, s0_all, ckt_all, cvt_all, gb, inv, sinks, rb, prev):
    depth, nb = c0_all.shape[:2]
    group = SAMPLE_GROUP
    R = SEQ_PAD
    NR = group * R
    first_layer = prev is None
    assert first_layer == (layer == 0)

    def const(shape):
        return pl.BlockSpec(shape, lambda i: (0,) * len(shape))

    def rows(w):
        return pl.BlockSpec((NR, w), lambda i: (i, 0))

    def tokens(w):
        return pl.BlockSpec((group * N_NEW, w), lambda i: (i, 0))

    def layered_in(shape):
        return pl.BlockSpec((None, group) + shape, lambda i: (layer, i) + (0,) * len(shape))

    def layered_out(shape):
        if first_layer:
            return pl.BlockSpec((depth, group) + shape, lambda i: (0, i) + (0,) * len(shape))
        return layered_in(shape)

    smem = pl.BlockSpec(memory_space=pltpu.SMEM)
    st = (N_HEADS, HEAD_D, HEAD_D)
    buf = (LANES, CHUNK)
    n_in = 11
    if first_layer:
        extra_specs, extra_args, aliases = [], [], {}
    else:
        extra_specs = [pl.BlockSpec(memory_space=pl.ANY)] * 4
        extra_args = list(prev)
        aliases = {n_in + i: 5 + i for i in range(4)}
    return pl.pallas_call(
        functools.partial(_sample_mixers_kernel, group=group, first_layer=first_layer),
        grid=(nb // group,),
        in_specs=[tokens(Z_HALF),
                  layered_in(st), rows(512), rows(LANES), layered_in(st),
                  layered_in(buf), layered_in(buf),
                  const((1, LANES)), const((1, LANES)), smem, smem] + extra_specs,
        out_specs=[tokens(512), tokens(512), tokens(512), rows(512), rows(LANES),
                   layered_out(st), layered_out(st), layered_out(buf), layered_out(buf)],
        out_shape=[jax.ShapeDtypeStruct((nb * N_NEW, 512), F32)] * 3
        + [jax.ShapeDtypeStruct((nb * R, 512), F32), jax.ShapeDtypeStruct((nb * R, LANES), F32),
           jax.ShapeDtypeStruct((depth, nb) + st, F32), jax.ShapeDtypeStruct((depth, nb) + st, F32),
           jax.ShapeDtypeStruct((depth, nb) + buf, F32), jax.ShapeDtypeStruct((depth, nb) + buf, F32)],
        scratch_shapes=[pltpu.VMEM((group * SWA_HEADS * R, LANES), F32),
                        pltpu.VMEM((group * SWA_HEADS * R, LANES), F32)],
        input_output_aliases=aliases,
        compiler_params=_params(1),
        name="mixer_sample",
    )(z, c0_all, n0_rows, m0_rows, s0_all, ckt_all, cvt_all, gb, inv, sinks, rb, *extra_args)


def _mixer_post_kernel(hml_ref, osw_ref, ort_ref, mlo_ref, rtg_ref, g0_ref, g1_ref, g2_ref, x_ref,
                       mlg_ref, retg_ref, ng_ref, wml_ref, wsw_ref, wrt_ref, wout_ref, o_ref):
    o_ref[...] = _post_math(hml_ref[...], osw_ref[...], ort_ref[...], mlo_ref[...], rtg_ref[...],
                            g0_ref[...], g1_ref[...], g2_ref[...], x_ref[...],
                            mlg_ref[...], retg_ref[...], ng_ref[...],
                            wml_ref[...], wsw_ref[...], wrt_ref[...], wout_ref[...])


def mixer_post(hml, osw, ort, z, x, mlg, retg, ng, layer, wml, wsw, wrt, wout, tm, name):
    m = x.shape[0]

    def tok(w, cb=0):
        return pl.BlockSpec((tm, w), lambda i, cb=cb: (i, cb))

    def const(shape):
        return pl.BlockSpec(shape, lambda i: (0,) * len(shape))

    return pl.pallas_call(
        _mixer_post_kernel,
        grid=(m // tm,),
        in_specs=[tok(512), tok(512), tok(512), tok(512, ZC_ML_O), tok(512, ZC_RT_G),
                  tok(1024, ZC_GATES), tok(1024, ZC_GATES + 1), tok(1024, ZC_GATES + 2), tok(D_MODEL),
                  const((1, 512)), const((1, 512)), const((1, D_MODEL)),
                  _layer_weight(wml, layer), _layer_weight(wsw, layer), _layer_weight(wrt, layer),
                  _layer_weight(wout, layer)],
        out_specs=tok(D_MODEL),
        out_shape=jax.ShapeDtypeStruct((m, D_MODEL), F32),
        compiler_params=_params(1),
        name=name,
    )(hml, osw, ort, z, z, z, z, z, x, mlg, retg, ng, wml, wsw, wrt, wout)


def _xattn_head(q, k, v, mask):

    s = _dot_nt(jnp.where(mask, q, 0.0).astype(BF16), k) * (X_D ** -0.5)
    m = jnp.max(s, axis=1, keepdims=True)
    p = jnp.exp(s - m)
    p = p / jnp.sum(p, axis=1, keepdims=True)
    return jnp.where(mask, _dot(p.astype(BF16), v), 0.0)


def _xattn_ffn_prompt_kernel(x_ref, gxi_ref, gxo_ref, wcq_ref, kv_ref, wco_ref, gfi_ref, gfo_ref, wgu_ref, wd_ref,
                             o_ref, mid_scr):
    @pl.when(pl.program_id(0) == 0)
    def _():
        mid_scr[...] = jnp.zeros(mid_scr.shape, F32)

    x_mid = mid_scr[...]
    x = x_ref[...]
    lane = lax.broadcasted_iota(jnp.int32, (1, X_HEADS * X_D), 1)
    masks = [(lane >= X_D * h) & (lane < X_D * (h + 1)) for h in range(X_HEADS)]
    fc = D_FF // FFN_SPLIT

    assert FFN_SPLIT == 1
    half = x.shape[0] // 2
    top, bot = slice(0, half), slice(half, 2 * half)
    k = kv_ref[:, 0:X_HEADS * X_D].astype(BF16)
    v = kv_ref[:, X_HEADS * X_D:2 * X_HEADS * X_D].astype(BF16)
    u_top = _rms(x_mid[top], gfi_ref[...]).astype(BF16)
    g_top = _dot(u_top, wgu_ref[:, 0:D_FF])
    u_bot = _rms(x_mid[bot], gfi_ref[...]).astype(BF16)
    g_bot = _dot(u_bot, wgu_ref[:, 0:D_FF])
    u = jnp.concatenate([u_top, u_bot], axis=0)
    g = jnp.concatenate([g_top, g_bot], axis=0)
    q = _dot(_rms(x, gxi_ref[...]).astype(BF16), wcq_ref[...])
    up = _dot(u, wgu_ref[:, D_FF:2 * D_FF])
    att = jnp.zeros(q.shape, F32)
    for h in range(X_HEADS):
        att = att + _xattn_head(q, k, v, masks[h])
    hid = (g * _sigmoid(g) * up).astype(BF16)
    down_top = _dot(hid[top], wd_ref[...])
    down_bot = _dot(hid[bot], wd_ref[...])
    o_ref[top, :] = x_mid[top] + _rms(down_top, gfo_ref[...])
    mid_scr[...] = x + _rms(_dot(att.astype(BF16), wco_ref[...]), gxo_ref[...])
    o_ref[bot, :] = x_mid[bot] + _rms(down_bot, gfo_ref[...])


def xattn_ffn_prompt(x, gxi, gxo, gfi, gfo, layer, wcq, kv, wco, wgu, wd, tm):
    m = x.shape[0]
    n = m // tm

    def const(shape):
        return pl.BlockSpec(shape, lambda s: (0,) * len(shape))

    gain = const((1, D_MODEL))
    return pl.pallas_call(
        _xattn_ffn_prompt_kernel,
        grid=(n + 1,),
        in_specs=[pl.BlockSpec((tm, D_MODEL), lambda s: (jnp.minimum(s, n - 1), 0)),
                  gain, gain, _layer_weight(wcq, layer), const(kv.shape), _layer_weight(wco, layer),
                  gain, gain, _layer_weight(wgu, layer), _layer_weight(wd, layer)],
        out_specs=pl.BlockSpec((tm, D_MODEL), lambda s: (jnp.maximum(s - 1, 0), 0)),
        out_shape=jax.ShapeDtypeStruct((m, D_MODEL), F32),
        scratch_shapes=[pltpu.VMEM((tm, D_MODEL), F32)],
        compiler_params=_params(1),
        name="xattn_ffn_prompt",
    )(x, gxi, gxo, wcq, kv, wco, gfi, gfo, wgu, wd)


def _xattn_sample_kernel(x_ref, gin_ref, gout_ref, wcq_ref, kt_ref, vt_ref, wco_ref, o_ref, q_scr, a_scr,
                         *, group):
    R = SEQ_PAD
    x = x_ref[...]
    q_scr[...] = _dot(_rms(x, gin_ref[...]).astype(BF16), wcq_ref[...])
    lane = lax.broadcasted_iota(jnp.int32, (1, X_HEADS * X_D), 1)
    masks = [(lane >= X_D * h) & (lane < X_D * (h + 1)) for h in range(X_HEADS)]
    second = (lax.broadcasted_iota(jnp.int32, (X_HEADS * R, 1), 0) & (R - 1)) >= N_NEW

    def body(pair, carry):
        rows = pl.ds(pl.multiple_of(pair * R, R), R)
        q = q_scr[rows, :]
        qs = jnp.concatenate([jnp.where(mk, q, 0.0) for mk in masks], axis=0).astype(BF16)
        s = jnp.where(second, _dot(qs, kt_ref[2 * pair + 1].astype(BF16)), _dot(qs, kt_ref[2 * pair].astype(BF16)))
        s = s * (X_D ** -0.5)
        m = jnp.max(s, axis=1, keepdims=True)
        p = jnp.exp(s - m)
        p = (p / jnp.sum(p, axis=1, keepdims=True)).astype(BF16)
        o = jnp.where(second, _dot_nt(p, vt_ref[2 * pair + 1].astype(BF16)), _dot_nt(p, vt_ref[2 * pair].astype(BF16)))
        acc = jnp.zeros((R, X_HEADS * X_D), F32)
        for h in range(X_HEADS):
            acc = acc + jnp.where(masks[h], o[R * h:R * (h + 1)], 0.0)
        a_scr[rows, :] = acc
        return carry

    lax.fori_loop(0, group // 2, body, 0, unroll=2)
    o_ref[...] = x + _rms(_dot(a_scr[...].astype(BF16), wco_ref[...]), gout_ref[...])


def xattn_sample(x, gin, gout, wcq, layer, mkt, mvt, wco, group=16):
    nb = mkt.shape[1]
    R = N_NEW

    def const(shape):
        return pl.BlockSpec(shape, lambda i: (0,) * len(shape))

    tok = pl.BlockSpec((group * R, D_MODEL), lambda i: (i, 0))
    mem = pl.BlockSpec((None, group, X_HEADS * X_D, N_MEM), lambda i: (layer, i, 0, 0))
    return pl.pallas_call(
        functools.partial(_xattn_sample_kernel, group=group),
        grid=(nb // group,),
        in_specs=[tok, const((1, D_MODEL)), const((1, D_MODEL)), _layer_weight(wcq, layer), mem, mem,
                  _layer_weight(wco, layer)],
        out_specs=tok,
        out_shape=jax.ShapeDtypeStruct((nb * R, D_MODEL), F32),
        scratch_shapes=[pltpu.VMEM((group * R, X_HEADS * X_D), F32),
                        pltpu.VMEM((group * R, X_HEADS * X_D), F32)],
        compiler_params=_params(1),
        name="xattn_sample",
    )(x, gin, gout, wcq, mkt, mvt, wco)


FFN_SPLIT = 1


def _ffn_kernel(x_ref, gin_ref, gout_ref, wgu_ref, wd_ref, o_ref):
    x = x_ref[...]
    u = _rms(x, gin_ref[...]).astype(BF16)
    fc = D_FF // FFN_SPLIT
    acc = None
    for c in range(FFN_SPLIT):
        g = _dot(u, wgu_ref[:, fc * c:fc * (c + 1)])
        up = _dot(u, wgu_ref[:, D_FF + fc * c:D_FF + fc * (c + 1)])
        h = (g * _sigmoid(g) * up).astype(BF16)
        part = _dot(h, wd_ref[fc * c:fc * (c + 1), :])
        acc = part if acc is None else acc + part
    o_ref[...] = x + _rms(acc, gout_ref[...])


def ffn(x, gin, gout, layer, wgu, wd, tm, name):
    m = x.shape[0]

    def const(shape):
        return pl.BlockSpec(shape, lambda i: (0,) * len(shape))

    tok = pl.BlockSpec((tm, D_MODEL), lambda i: (i, 0))
    return pl.pallas_call(
        _ffn_kernel,
        grid=(m // tm,),
        in_specs=[tok, const((1, D_MODEL)), const((1, D_MODEL)),
                  _layer_weight(wgu, layer), _layer_weight(wd, layer)],
        out_specs=tok,
        out_shape=jax.ShapeDtypeStruct((m, D_MODEL), F32),
        compiler_params=_params(1),
        name=name,
    )(x, gin, gout, wgu, wd)


def _reorder_w_in_t(w):
    wt = jnp.swapaxes(w, 1, 2)
    sizes = (512, 512, 512, 4, 4, 512, 512, 128, 128, 512, 512, 512, 512, 3072)
    offs = np.concatenate([[0], np.cumsum(sizes)])
    (ml_q, ml_k, ml_v, ml_i, ml_f, ml_o, sw_q, sw_k, sw_v, rt_q, rt_k, rt_v, rt_g, gates) = [
        wt[:, int(offs[i]):int(offs[i + 1])] for i in range(len(sizes))]
    pad = jnp.zeros((w.shape[0], 512 - 128 - 128 - 8, w.shape[1]), w.dtype)
    out = jnp.concatenate([ml_q, ml_k, ml_v, sw_q, sw_k, sw_v, ml_i, ml_f, pad, rt_q, rt_k, rt_v,
                           ml_o, rt_g, gates], axis=1)
    assert out.shape[1] == Z_COLS
    return out.astype(BF16)


def _row(v):
    return v.reshape(1, -1).astype(F32)


def _decoder_layer(x, z_fn, mixer_fn, xattn_ffn_fn):
    z = z_fn(x)
    x, states = mixer_fn(z, x)
    return xattn_ffn_fn(x), z, states


def kernel(x_prompt, x_sample, mem_prompt, state_mlstm_C, state_mlstm_n, state_mlstm_m, state_ret_S,
           cache_swa_k, cache_swa_v, cache_mem_k, cache_mem_v, norm_g, w_in, ml_gate_bias, ml_head_g,
           ret_head_g, swa_sinks, rel_bias, w_br_ml, w_br_swa, w_br_ret, w_out, w_cq, w_mkv, w_co,
           w_gu, w_down):
    depth = w_in.shape[0]
    bp, t, d = x_prompt.shape
    assert bp == 1 and d == D_MODEL and t % 1024 == 0
    nb, n_new, _ = x_sample.shape
    assert n_new == N_NEW
    R = SEQ_PAD

    xp = x_prompt.reshape(t, d)
    xs = x_sample.reshape(nb * n_new, d)
    mem = mem_prompt.reshape(N_MEM, d)
    half = HEAD_D // 2
    inv = ROPE_BASE ** (-jnp.arange(half, dtype=F32) / half)
    inv = jnp.concatenate([inv, inv]).reshape(1, LANES)
    rb = rel_bias.astype(F32).reshape(-1)
    mem_kt = jnp.transpose(cache_mem_k, (0, 1, 3, 4, 2)).reshape(depth, nb, X_HEADS * X_D, N_MEM)
    mem_vt = jnp.transpose(cache_mem_v, (0, 1, 3, 4, 2)).reshape(depth, nb, X_HEADS * X_D, N_MEM)

    win_t = _reorder_w_in_t(w_in)
    wml, wsw, wrt, wout = (w.astype(BF16) for w in (w_br_ml, w_br_swa, w_br_ret, w_out))
    wcq, wmkv, wco, wgu, wd = (w.astype(BF16) for w in (w_cq, w_mkv, w_co, w_gu, w_down))

    outs_p = {k: [] for k in ("C", "n", "m", "S", "k", "v", "mk", "mv")}
    outs_s = {k: [] for k in ("n", "m")}
    sample_layered = None
    swa_kt = jnp.transpose(cache_swa_k, (0, 1, 3, 4, 2)).reshape(depth, nb, LANES, CHUNK)
    swa_vt = jnp.transpose(cache_swa_v, (0, 1, 3, 4, 2)).reshape(depth, nb, LANES, CHUNK)
    for l in range(depth):
        ng = [_row(norm_g[l, i]) for i in range(7)]
        mlg, retg = _row(ml_head_g[l]), _row(ret_head_g[l])
        gb = jnp.concatenate([ml_gate_bias[l, 0], ml_gate_bias[l, 1],
                              jnp.zeros((LANES - 2 * N_HEADS,), F32)]).reshape(1, LANES).astype(F32)
        gb_rows = jnp.broadcast_to(ml_gate_bias[l].astype(F32).reshape(2 * N_HEADS, 1), (2 * N_HEADS, LANES))
        sinks = swa_sinks[l].astype(F32)

        kv = norm_matmul(mem, ng[6], wmkv, l, tm=N_MEM, tn=2 * X_HEADS * X_D, name="memory_kv")

        def seq_p(z, x):
            x1, c_, n_, m_, s_ = mixer_prompt(z, x, gb_rows, inv, sinks, rb, mlg, retg, ng[1], l,
                                              wml, wsw, wrt, wout)
            return x1, (c_, n_, m_, s_)

        xp, zp, (c_, n_, m_, s_) = _decoder_layer(
            xp,
            lambda x: norm_matmul(x, ng[0], win_t, l, tm=min(IN_PROJ_TM, t), tn=1024, name="in_proj_prompt",
                                  w_transposed=True),
            seq_p,
            lambda x: xattn_ffn_prompt(x, ng[2], ng[3], ng[4], ng[5], l, wcq, kv, wco, wgu, wd, tm=512))
        outs_p["C"].append(c_.reshape(1, N_HEADS, HEAD_D, HEAD_D))
        outs_p["n"].append(n_.reshape(1, N_HEADS, HEAD_D))
        outs_p["m"].append(m_[:, :N_HEADS])
        outs_p["S"].append(s_.reshape(1, N_HEADS, HEAD_D, HEAD_D))
        misc_last = zp[t - CHUNK:, 512 * ZC_MISC:512 * ZC_MISC + 2 * LANES]
        outs_p["k"].append(misc_last[:, :LANES].reshape(1, CHUNK, 2, SWA_D))
        outs_p["v"].append(misc_last[:, LANES:].reshape(1, CHUNK, 2, SWA_D))
        outs_p["mk"].append(kv[:, :X_HEADS * X_D].reshape(1, N_MEM, X_HEADS, X_D))
        outs_p["mv"].append(kv[:, X_HEADS * X_D:].reshape(1, N_MEM, X_HEADS, X_D))

        n0_rows = jnp.repeat(state_mlstm_n[l].astype(F32).reshape(nb, N_HEADS * HEAD_D), R, axis=0)
        m0_rows = jnp.repeat(jnp.pad(state_mlstm_m[l].astype(F32), ((0, 0), (0, LANES - N_HEADS))), R, axis=0)

        def seq_s(z, x):
            hml, osw, ort, n_, m_, *layered = mixer_sample(
                z, l, state_mlstm_C, n0_rows, m0_rows, state_ret_S, swa_kt, swa_vt, gb, inv, sinks, rb,
                sample_layered)
            x1 = mixer_post(hml, osw, ort, z, x, mlg, retg, ng[1], l, wml, wsw, wrt, wout,
                            tm=min(256, x.shape[0]), name="mixer_post_sample")
            return x1, (n_, m_, layered)

        xs, zs, (n_, m_, sample_layered) = _decoder_layer(
            xs,
            lambda x: norm_matmul(x, ng[0], win_t, l, tm=min(1024, nb * n_new), tn=1024, name="in_proj_sample",
                                  w_transposed=True),
            seq_s,
            lambda x: ffn(xattn_sample(x, ng[2], ng[3], wcq, l, mem_kt, mem_vt, wco),
                          ng[4], ng[5], l, wgu, wd, tm=min(512, nb * n_new), name="ffn_sample"))
        outs_s["n"].append(n_.reshape(nb, R, N_HEADS, HEAD_D)[:, R - 1])
        outs_s["m"].append(m_.reshape(nb, R, LANES)[:, n_new - 1, :N_HEADS])

    sample_c, sample_s, swa_kt_new, swa_vt_new = sample_layered
    s_swa_k = jnp.transpose(swa_kt_new.reshape(depth, nb, 2, SWA_D, CHUNK), (0, 1, 4, 2, 3))
    s_swa_v = jnp.transpose(swa_vt_new.reshape(depth, nb, 2, SWA_D, CHUNK), (0, 1, 4, 2, 3))
    y_p = xp.reshape(1, t, d)
    y_s = xs.reshape(nb, n_new, d)
    st = lambda d_, k: jnp.stack(d_[k])
    return (y_p, y_s,
            st(outs_p, "C"), st(outs_p, "n"), st(outs_p, "m"), st(outs_p, "S"),
            st(outs_p, "k"), st(outs_p, "v"), st(outs_p, "mk"), st(outs_p, "mv"),
            sample_c, st(outs_s, "n"), st(outs_s, "m"), sample_s, s_swa_k, s_swa_v)
```
